```python
import math
import jax, jax.numpy as jnp
from jax import lax
import numpy as np

D_MODEL = 1024
BATCH = 2
SEQ = 8192
DEPTH = 1

HEAD_DIM = 64
DIL_GROUPS = ((128, 1), (512, 4), (2048, 16))
HEADS_PER_GROUP = 4
N_HEADS_A = HEADS_PER_GROUP * len(DIL_GROUPS)
MAX_WINDOW = 2048
N_HEADS_B = 8
QBLOCK = 128
REL_BUCKETS = 32
REL_MAX_DIST = 2048
N_EXPERTS = 64
TOP_K = 8
D_EXPERT = 256
D_SHARED = 256
ROUTE_SCALE = 2.5
TOKEN_BLOCK = 128
EPS = 1e-6

WIDTH_A = 3 * N_HEADS_A * HEAD_DIM
WIDTH_B = 3 * N_HEADS_B * HEAD_DIM
D_IN = WIDTH_A + WIDTH_B + N_HEADS_B
OUT_A = HEADS_PER_GROUP * HEAD_DIM
OUT_B = N_HEADS_B * HEAD_DIM

kernel_name = "hybrid_dilated_fox_moe_block"


def rmsnorm(x, g):
    xf = x.astype(jnp.float32)
    y = xf * lax.rsqrt(jnp.mean(xf * xf, axis=-1, keepdims=True) + EPS)
    return (y * g.astype(jnp.float32)).astype(x.dtype)


def rel_bucket(dist):
    max_exact = REL_BUCKETS // 2
    n = jnp.maximum(dist.astype(jnp.float32), 1.0)
    large = max_exact + (jnp.log(n / max_exact) / math.log(REL_MAX_DIST / max_exact)
                         * (REL_BUCKETS - max_exact)).astype(jnp.int32)
    large = jnp.minimum(large, REL_BUCKETS - 1)
    return jnp.where(dist < max_exact, dist, large)


def swiglu(x, w_g, w_u, w_d):
    return (jax.nn.silu(x @ w_g) * (x @ w_u)) @ w_d


def dilated_attention(q, k, v, rel_bias):
    b, _, s, dh = q.shape
    pad = ((0, 0), (0, 0), (MAX_WINDOW, 0), (0, 0))
    k_pad = jnp.pad(k, pad)
    v_pad = jnp.pad(v, pad)
    groups = []
    for g, (w, d) in enumerate(DIL_GROUPS):
        hs = slice(g * HEADS_PER_GROUP, (g + 1) * HEADS_PER_GROUP)
        offs = d * jnp.arange(w // d + 1, dtype=jnp.int32)
        bias = rel_bias[rel_bucket(offs)][:, hs].T.astype(jnp.float32)
        groups.append((q[:, hs], k_pad[:, hs], v_pad[:, hs], offs, bias))

    def block(n):
        t = n * QBLOCK + jnp.arange(QBLOCK, dtype=jnp.int32)
        outs, lses = [], []
        for qg, kg_all, vg_all, offs, bias in groups:
            q_blk = lax.dynamic_slice_in_dim(qg, n * QBLOCK, QBLOCK, axis=2)
            pos = t[:, None] - offs[None, :]
            idx = pos + MAX_WINDOW
            kg = kg_all[:, :, idx]
            vg = vg_all[:, :, idx]
            sc = jnp.einsum('bhqd,bhqjd->bhqj', q_blk, kg).astype(jnp.float32)
            sc = sc + bias[None, :, None, :]
            sc = jnp.where((pos >= 0)[None, None], sc, -jnp.inf)
            lse = jax.nn.logsumexp(sc, axis=-1)
            p = jnp.exp(sc - lse[..., None])
            outs.append(jnp.einsum('bhqj,bhqjd->bhqd', p.astype(vg.dtype), vg))
            lses.append(lse)
        wgt = jax.nn.softmax(jnp.stack(lses, 0), axis=0)
        o = jnp.sum(jnp.stack(outs, 0).astype(jnp.float32) * wgt[..., None], axis=0)
        return o.astype(q.dtype)

    o = lax.map(block, jnp.arange(s // QBLOCK))
    return o.transpose(1, 0, 3, 2, 4).reshape(b, s, OUT_A)


def forgetting_attention(q, k, v, cum_logf):
    b, _, s, dh = q.shape
    kpos = jnp.arange(s, dtype=jnp.int32)

    def block(n):
        t = n * QBLOCK + jnp.arange(QBLOCK, dtype=jnp.int32)
        q_blk = lax.dynamic_slice_in_dim(q, n * QBLOCK, QBLOCK, axis=2)
        c_q = lax.dynamic_slice_in_dim(cum_logf, n * QBLOCK, QBLOCK, axis=2)
        sc = jnp.einsum('bhqd,bhkd->bhqk', q_blk, k).astype(jnp.float32)
        sc = sc + c_q[..., None] - cum_logf[:, :, None, :]
        sc = jnp.where((t[:, None] >= kpos[None, :])[None, None], sc, -jnp.inf)
        p = jax.nn.softmax(sc, axis=-1)
        return jnp.einsum('bhqk,bhkd->bhqd', p.astype(v.dtype), v)

    o = lax.map(block, jnp.arange(s // QBLOCK))
    return o.transpose(1, 0, 3, 2, 4).reshape(b, s, OUT_B)


def moe_ffn(h, w_router, router_bias, w_gate_e, w_up_e, w_down_e, w_gate_s, w_up_s, w_down_s):
    n, d = h.shape
    scores = jax.nn.sigmoid(h.astype(jnp.float32) @ w_router.astype(jnp.float32))
    _, top_idx = lax.top_k(scores + router_bias.astype(jnp.float32), TOP_K)
    top_s = jnp.take_along_axis(scores, top_idx, axis=-1)
    top_w = top_s / jnp.sum(top_s, axis=-1, keepdims=True) * ROUTE_SCALE
    gates = jnp.sum(jax.nn.one_hot(top_idx, N_EXPERTS, dtype=jnp.float32) * top_w[..., None], axis=1)
    gates = gates.astype(h.dtype)

    def block(args):
        xb, gb = args
        hg = jnp.einsum('td,edf->tef', xb, w_gate_e)
        hu = jnp.einsum('td,edf->tef', xb, w_up_e)
        hmid = jax.nn.silu(hg) * hu * gb[..., None]
        return jnp.einsum('tef,efd->td', hmid, w_down_e)

    nb = n // TOKEN_BLOCK
    routed = lax.map(block, (h.reshape(nb, TOKEN_BLOCK, d), gates.reshape(nb, TOKEN_BLOCK, N_EXPERTS)))
    return routed.reshape(n, d) + swiglu(h, w_gate_s, w_up_s, w_down_s)


def setup_inputs(seed: int = 0) -> dict:
    key = jax.random.key(seed)
    ks = jax.random.split(key, 24)
    nrm = lambda k, shape, scale: jax.random.normal(k, shape, jnp.float32) * scale
    D = D_MODEL
    return {
        "x": nrm(ks[0], (BATCH, SEQ, D), 1.0),
        "g_mix": 1.0 + nrm(ks[1], (D,), 0.02),
        "w_in": nrm(ks[2], (D, D_IN), D ** -0.5),
        "q_norm_a": 1.0 + nrm(ks[3], (HEAD_DIM,), 0.02),
        "k_norm_a": 1.0 + nrm(ks[4], (HEAD_DIM,), 0.02),
        "q_norm_b": 1.0 + nrm(ks[5], (HEAD_DIM,), 0.02),
        "k_norm_b": 1.0 + nrm(ks[6], (HEAD_DIM,), 0.02),
        "rel_bias": nrm(ks[7], (REL_BUCKETS, N_HEADS_A), 0.5),
        "b_forget": 3.0 + nrm(ks[8], (N_HEADS_B,), 0.5),
        "w_gate": nrm(ks[9], (D, 2 * D), D ** -0.5),
        "b_gate": nrm(ks[10], (2 * D,), 0.02),
        "w_proj_a": nrm(ks[11], (OUT_A, D), OUT_A ** -0.5),
        "w_proj_b": nrm(ks[12], (OUT_B, D), OUT_B ** -0.5),
        "w_out": nrm(ks[13], (D, D), D ** -0.5),
        "g_ffn": 1.0 + nrm(ks[14], (D,), 0.02),
        "w_router": nrm(ks[15], (D, N_EXPERTS), D ** -0.5),
        "router_bias": nrm(ks[16], (N_EXPERTS,), 0.01),
        "w_gate_e": nrm(ks[17], (N_EXPERTS, D, D_EXPERT), D ** -0.5),
        "w_up_e": nrm(ks[18], (N_EXPERTS, D, D_EXPERT), D ** -0.5),
        "w_down_e": nrm(ks[19], (N_EXPERTS, D_EXPERT, D), D_EXPERT ** -0.5),
        "w_gate_s": nrm(ks[20], (D, D_SHARED), D ** -0.5),
        "w_up_s": nrm(ks[21], (D, D_SHARED), D ** -0.5),
        "w_down_s": nrm(ks[22], (D_SHARED, D), D_SHARED ** -0.5),
    }


def reference(x, g_mix, w_in, q_norm_a, k_norm_a, q_norm_b, k_norm_b, rel_bias, b_forget,
              w_gate, b_gate, w_proj_a, w_proj_b, w_out, g_ffn, w_router, router_bias,
              w_gate_e, w_up_e, w_down_e, w_gate_s, w_up_s, w_down_s):
    b, s, d = x.shape
    scale = HEAD_DIM ** -0.5
    for _ in range(DEPTH):
        h = rmsnorm(x, g_mix)
        proj = h @ w_in
        pa = proj[..., :WIDTH_A].reshape(b, s, 3, N_HEADS_A, HEAD_DIM)
        pb = proj[..., WIDTH_A:WIDTH_A + WIDTH_B].reshape(b, s, 3, N_HEADS_B, HEAD_DIM)
        f_logit = proj[..., WIDTH_A + WIDTH_B:]

        qa = (rmsnorm(pa[:, :, 0], q_norm_a) * scale).transpose(0, 2, 1, 3)
        ka = rmsnorm(pa[:, :, 1], k_norm_a).transpose(0, 2, 1, 3)
        va = pa[:, :, 2].transpose(0, 2, 1, 3)
        y_a = dilated_attention(qa, ka, va, rel_bias)

        qb = (rmsnorm(pb[:, :, 0], q_norm_b) * scale).transpose(0, 2, 1, 3)
        kb = rmsnorm(pb[:, :, 1], k_norm_b).transpose(0, 2, 1, 3)
        vb = pb[:, :, 2].transpose(0, 2, 1, 3)
        log_f = jax.nn.log_sigmoid(f_logit.astype(jnp.float32) + b_forget.astype(jnp.float32))
        cum_logf = jnp.cumsum(log_f, axis=1).transpose(0, 2, 1)
        y_b = forgetting_attention(qb, kb, vb, cum_logf)

        gates = jax.nn.sigmoid(h @ w_gate + b_gate)
        merged = gates[..., :d] * (y_a @ w_proj_a) + gates[..., d:] * (y_b @ w_proj_b)
        x = x + merged @ w_out

        h2 = rmsnorm(x, g_ffn).reshape(b * s, d)
        ffn = moe_ffn(h2, w_router, router_bias, w_gate_e, w_up_e, w_down_e,
                      w_gate_s, w_up_s, w_down_s)
        x = x + ffn.reshape(b, s, d)
    return x
```

```python
import functools
import math

import jax
import jax.numpy as jnp
import numpy as np
from jax import lax
from jax.experimental import pallas as pl
from jax.experimental.pallas import tpu as pltpu

D_MODEL = 1024
HEAD_DIM = 64
DIL_GROUPS = ((128, 1), (512, 4), (2048, 16))
HEADS_PER_GROUP = 4
N_HEADS_A = HEADS_PER_GROUP * len(DIL_GROUPS)
N_HEADS_B = 8
REL_BUCKETS = 32
REL_MAX_DIST = 2048
N_EXPERTS = 64
TOP_K = 8
D_EXPERT = 256
D_SHARED = 256
ROUTE_SCALE = 2.5
EPS = 1e-6

WIDTH_A = 3 * N_HEADS_A * HEAD_DIM
WIDTH_B = 3 * N_HEADS_B * HEAD_DIM
QK_B = N_HEADS_B * HEAD_DIM
OUT_A = HEADS_PER_GROUP * HEAD_DIM
OUT_B = N_HEADS_B * HEAD_DIM

LANES = 128
GROUP_W = HEADS_PER_GROUP * HEAD_DIM
WIN_J = 128
NEG = -1e30
VMEM_LIMIT = 56 * 1024 * 1024

TM_IN = 512
TM_POST = 512
TQ_FOX = 512
TM_MOE = 1024

BF16 = jnp.bfloat16
F32 = jnp.float32


def _dot(a, b):
    return jnp.dot(a, b, preferred_element_type=F32)


def _dot_nt(a, b):
    return lax.dot_general(a, b, (((1,), (1,)), ((), ())), preferred_element_type=F32)


def _split3(v):
    hi = v.astype(BF16).astype(F32)
    r = v - hi
    mid = r.astype(BF16).astype(F32)
    lo = (r - mid).astype(BF16).astype(F32)
    return hi, mid, lo


def _inproj_kernel(x_ref, g_ref, wa_ref, wb_ref, wf_ref, bd_ref, tri_ref, gain_a_ref, gain_b_ref,
                   bf_ref, pa_ref, qkb_ref, vb_ref, carry_ref, *, tiles_per_seq):
    tm = x_ref.shape[0]
    x = x_ref[...]
    h = x * lax.rsqrt(jnp.mean(x * x, axis=-1, keepdims=True) + EPS) * g_ref[...]
    h = h.astype(BF16)
    bd = bd_ref[...]

    def headnorm(p, gain):
        ms = _dot((p * p).astype(BF16), bd)
        return p * lax.rsqrt(ms + EPS) * gain

    n_chunk_a = WIDTH_A // GROUP_W
    for c in range(n_chunk_a):
        cols = slice(c * GROUP_W, (c + 1) * GROUP_W)
        p = _dot(h, wa_ref[:, cols])
        part = c // len(DIL_GROUPS)
        if part < 2:
            p = headnorm(p, gain_a_ref[part:part + 1, :])
        pa_ref[:, cols] = p.astype(BF16)

    f = _dot(h, wf_ref[...]) + bf_ref[...]
    logf = jnp.minimum(f, 0.0) - jnp.log1p(jnp.exp(-jnp.abs(f)))
    tri = tri_ref[...]
    lh, lm, ll = _split3(logf)
    cum = _dot(tri, lh.astype(BF16)) + _dot(tri, lm.astype(BF16)) + _dot(tri, ll.astype(BF16))

    @pl.when(pl.program_id(0) % tiles_per_seq == 0)
    def _():
        carry_ref[...] = jnp.zeros_like(carry_ref)

    cum = cum + carry_ref[0:1, :]
    carry_ref[0:1, :] = cum[tm - 1:tm, :]
    ch, cm, cl = _split3(cum)

    j = lax.broadcasted_iota(jnp.int32, (tm, HEAD_DIM), 1)
    n_chunk_b = QK_B // GROUP_W
    for part in range(2):
        for c in range(n_chunk_b):
            cols = slice(part * QK_B + c * GROUP_W, part * QK_B + (c + 1) * GROUP_W)
            p = headnorm(_dot(h, wb_ref[:, cols]), gain_b_ref[part:part + 1, :])
            for hh in range(HEADS_PER_GROUP):
                head = c * HEADS_PER_GROUP + hh
                a, b_, c_ = (ch[:, head:head + 1], cm[:, head:head + 1], cl[:, head:head + 1])
                if part == 0:
                    ext = jnp.where(j == 0, a, jnp.where(j == 1, b_, jnp.where(
                        j == 2, c_, jnp.where(j < 6, 1.0, 0.0))))
                else:
                    ext = jnp.where(j < 3, 1.0, jnp.where(j == 3, -a, jnp.where(
                        j == 4, -b_, jnp.where(j == 5, -c_, 0.0))))
                piece = jnp.concatenate([p[:, hh * HEAD_DIM:(hh + 1) * HEAD_DIM], ext], axis=-1)
                o0 = (part * N_HEADS_B + head) * LANES
                qkb_ref[:, o0:o0 + LANES] = piece.astype(BF16)
    for c in range(QK_B // GROUP_W):
        cols = slice(2 * QK_B + c * GROUP_W, 2 * QK_B + (c + 1) * GROUP_W)
        vb_ref[:, c * GROUP_W:(c + 1) * GROUP_W] = _dot(h, wb_ref[:, cols]).astype(BF16)


def _inproj(xf, g_mix, w_in, q_norm_a, k_norm_a, q_norm_b, k_norm_b, b_forget, seq):
    n, d = xf.shape
    tm = TM_IN
    scale = HEAD_DIM ** -0.5
    w_bf = w_in.astype(BF16)
    wa = w_bf[:, :WIDTH_A]
    wb = w_bf[:, WIDTH_A:WIDTH_A + WIDTH_B]
    wf = jnp.pad(w_bf[:, WIDTH_A + WIDTH_B:], ((0, 0), (0, LANES - N_HEADS_B)))
    bfp = jnp.pad(b_forget.astype(F32), (0, LANES - N_HEADS_B)).reshape(1, LANES)
    seg = np.arange(GROUP_W) // HEAD_DIM
    bd = jnp.asarray((seg[:, None] == seg[None, :]).astype(np.float32) / HEAD_DIM, BF16)
    tri = jnp.asarray(np.tril(np.ones((tm, tm), np.float32)), BF16)
    gain_a = jnp.stack([jnp.tile(q_norm_a, HEADS_PER_GROUP) * scale, jnp.tile(k_norm_a, HEADS_PER_GROUP)])
    gain_b = jnp.stack([jnp.tile(q_norm_b, HEADS_PER_GROUP) * scale, jnp.tile(k_norm_b, HEADS_PER_GROUP)])
    const = lambda shape: pl.BlockSpec(shape, lambda i: (0,) * len(shape))
    return pl.pallas_call(
        functools.partial(_inproj_kernel, tiles_per_seq=seq // tm),
        grid=(n // tm,),
        in_specs=[
            pl.BlockSpec((tm, d), lambda i: (i, 0)),
            const((1, d)), const(wa.shape), const(wb.shape), const(wf.shape),
            const(bd.shape), const(tri.shape), const(gain_a.shape), const(gain_b.shape),
            const(bfp.shape),
        ],
        out_specs=[
            pl.BlockSpec((tm, WIDTH_A), lambda i: (i, 0)),
            pl.BlockSpec((tm, 2 * N_HEADS_B * LANES), lambda i: (i, 0)),
            pl.BlockSpec((tm, OUT_B), lambda i: (i, 0)),
        ],
        out_shape=[
            jax.ShapeDtypeStruct((n, WIDTH_A), BF16),
            jax.ShapeDtypeStruct((n, 2 * N_HEADS_B * LANES), BF16),
            jax.ShapeDtypeStruct((n, OUT_B), BF16),
        ],
        scratch_shapes=[pltpu.VMEM((8, LANES), F32)],
        compiler_params=pltpu.CompilerParams(
            dimension_semantics=("arbitrary",), vmem_limit_bytes=VMEM_LIMIT),
        name="inproj",
    )(xf, g_mix.reshape(1, d), wa, wb, wf, bd, tri, gain_a, gain_b, bfp)


def _dilated_kernel(*refs, has_prev, last):
    if has_prev:
        q_ref, kp_ref, kc_ref, vp_ref, vc_ref, bias_ref, po_ref, pl_ref = refs[:8]
        outs = refs[8:]
    else:
        q_ref, kp_ref, kc_ref, vp_ref, vc_ref, bias_ref = refs[:6]
        outs = refs[6:]
    tq = q_ref.shape[1]
    q = q_ref[0]
    kcat = jnp.concatenate([kp_ref[0], kc_ref[0]], axis=0)
    vcat = jnp.concatenate([vp_ref[0], vc_ref[0]], axis=0)
    lane_head = lax.broadcasted_iota(jnp.int32, (tq, GROUP_W), 1) // HEAD_DIM
    col = lax.broadcasted_iota(jnp.int32, (tq, 2 * tq), 1)
    no_prev = jnp.logical_and(pl.program_id(2) == 0, col < tq)
    acc = jnp.zeros((tq, GROUP_W), F32)
    lse = jnp.zeros((tq, GROUP_W), F32)
    for hh in range(HEADS_PER_GROUP):
        sel = lane_head == hh
        s = _dot_nt(jnp.where(sel, q, jnp.zeros_like(q)), kcat) + bias_ref[hh]
        s = jnp.where(no_prev, NEG, s)
        m = jnp.max(s, axis=-1, keepdims=True)
        p = jnp.exp(s - m)
        l = jnp.sum(p, axis=-1, keepdims=True)
        pv = _dot(p.astype(BF16), vcat)
        acc = jnp.where(sel, pv / l, acc)
        lse = jnp.where(sel, m + jnp.log(l), lse)
    if has_prev:
        l1 = pl_ref[0]
        mx = jnp.maximum(l1, lse)
        w1 = jnp.exp(l1 - mx)
        w2 = jnp.exp(lse - mx)
        den = w1 + w2
        acc = (w1 * po_ref[0] + w2 * acc) / den
        lse = mx + jnp.log(den)
    if last:
        outs[0][0] = acc.astype(outs[0].dtype)
    else:
        outs[0][0] = acc
        outs[1][0] = lse


def _rel_bucket(dist):
    max_exact = REL_BUCKETS // 2
    n = jnp.maximum(dist.astype(F32), 1.0)
    large = max_exact + (jnp.log(n / max_exact) / math.log(REL_MAX_DIST / max_exact)
                         * (REL_BUCKETS - max_exact)).astype(jnp.int32)
    large = jnp.minimum(large, REL_BUCKETS - 1)
    return jnp.where(dist < max_exact, dist, large)


def _toeplitz_bias(rel_bias, g, dil):
    tq = WIN_J
    offs = dil * jnp.arange(WIN_J + 1, dtype=jnp.int32)
    hs = slice(g * HEADS_PER_GROUP, (g + 1) * HEADS_PER_GROUP)
    tab = rel_bias[_rel_bucket(offs)][:, hs].T.astype(F32)
    jidx = tq + np.arange(tq)[:, None] - np.arange(2 * tq)[None, :]
    valid = (jidx >= 0) & (jidx <= WIN_J)
    t = tab[:, np.clip(jidx, 0, WIN_J)]
    return jnp.where(jnp.asarray(valid)[None], t, NEG)


def _dilated_group(pa, bias, prev, g, dil, batch, seq, last):
    tq = WIN_J
    sub = seq // dil
    nchunk = WIDTH_A // GROUP_W
    pav = pa.reshape(batch, sub, dil * WIDTH_A)
    ng = len(DIL_GROUPS)
    qmap = lambda b, r, i: (b, i, r * nchunk + g)
    kcmap = lambda b, r, i: (b, i, r * nchunk + ng + g)
    kpmap = lambda b, r, i: (b, jnp.maximum(i - 1, 0), r * nchunk + ng + g)
    vcmap = lambda b, r, i: (b, i, r * nchunk + 2 * ng + g)
    vpmap = lambda b, r, i: (b, jnp.maximum(i - 1, 0), r * nchunk + 2 * ng + g)
    omap = lambda b, r, i: (b, i, r)
    blk = (1, tq, GROUP_W)
    in_specs = [pl.BlockSpec(blk, qmap), pl.BlockSpec(blk, kpmap), pl.BlockSpec(blk, kcmap),
                pl.BlockSpec(blk, vpmap), pl.BlockSpec(blk, vcmap),
                pl.BlockSpec(bias.shape, lambda b, r, i: (0, 0, 0))]
    args = [pav, pav, pav, pav, pav, bias]
    if prev is not None:
        in_specs += [pl.BlockSpec(blk, omap), pl.BlockSpec(blk, omap)]
        args += [prev[0].reshape(batch, sub, dil * GROUP_W), prev[1].reshape(batch, sub, dil * GROUP_W)]
    oshape = (batch, sub, dil * GROUP_W)
    if last:
        out_specs = [pl.BlockSpec(blk, omap)]
        out_shape = [jax.ShapeDtypeStruct(oshape, BF16)]
    else:
        out_specs = [pl.BlockSpec(blk, omap), pl.BlockSpec(blk, omap)]
        out_shape = [jax.ShapeDtypeStruct(oshape, F32), jax.ShapeDtypeStruct(oshape, F32)]
    outs = pl.pallas_call(
        functools.partial(_dilated_kernel, has_prev=prev is not None, last=last),
        grid=(batch, dil, sub // tq),
        in_specs=in_specs, out_specs=out_specs, out_shape=out_shape,
        compiler_params=pltpu.CompilerParams(
            dimension_semantics=("arbitrary", "arbitrary", "arbitrary"), vmem_limit_bytes=VMEM_LIMIT),
        name=f"dilated_g{g}",
    )(*args)
    return [o.reshape(batch * seq, GROUP_W) for o in outs]


def _fox_kernel(q_ref, k_ref, v_ref, o_ref):
    tq = q_ref.shape[1]
    tk = tq
    qi = pl.program_id(2)
    qs = [q_ref[0, :, 0:LANES], q_ref[0, :, LANES:2 * LANES]]
    lane_lo = lax.broadcasted_iota(jnp.int32, (tq, LANES), 1) < HEAD_DIM
    row = lax.broadcasted_iota(jnp.int32, (tq, tk), 0)
    col = lax.broadcasted_iota(jnp.int32, (tq, tk), 1)
    causal = row >= col

    def chunk(ki, carry, diagonal):
        ms, ls, acc = carry
        start = pl.multiple_of(ki * tk, tk)
        v2 = v_ref[0, pl.ds(start, tk), :]
        new_m, new_l, alphas, pvs = [], [], [], []
        for hh in range(2):
            kh = k_ref[0, pl.ds(start, tk), hh * LANES:(hh + 1) * LANES]
            s = _dot_nt(qs[hh], kh)
            if diagonal:
                s = jnp.where(causal, s, NEG)
            m_new = jnp.maximum(ms[hh], jnp.max(s, axis=-1, keepdims=True))
            alpha = jnp.exp(ms[hh] - m_new)
            p = jnp.exp(s - m_new)
            new_l.append(alpha * ls[hh] + jnp.sum(p, axis=-1, keepdims=True))
            new_m.append(m_new)
            alphas.append(alpha)
            pvs.append(_dot(p.astype(BF16), v2))
        acc = acc * jnp.where(lane_lo, alphas[0], alphas[1]) + jnp.where(lane_lo, pvs[0], pvs[1])
        return (tuple(new_m), tuple(new_l), acc)

    m0 = jnp.full((tq, 1), NEG, F32)
    l0 = jnp.zeros((tq, 1), F32)
    init = ((m0, m0), (l0, l0), jnp.zeros((tq, LANES), F32))
    carry = lax.fori_loop(0, qi, lambda ki, c: chunk(ki, c, False), init)
    ms, ls, acc = chunk(qi, carry, True)
    o_ref[0] = (acc / jnp.where(lane_lo, ls[0], ls[1])).astype(o_ref.dtype)


def _fox(qkb, vb, batch, seq):
    tq = TQ_FOX
    pairs = N_HEADS_B // 2
    qkv = qkb.reshape(batch, seq, 2 * N_HEADS_B * LANES)
    vv = vb.reshape(batch, seq, OUT_B)
    out = pl.pallas_call(
        _fox_kernel,
        grid=(batch, pairs, seq // tq),
        in_specs=[
            pl.BlockSpec((1, tq, 2 * LANES), lambda b, p, i: (b, i, p)),
            pl.BlockSpec((1, seq, 2 * LANES), lambda b, p, i: (b, 0, pairs + p)),
            pl.BlockSpec((1, seq, LANES), lambda b, p, i: (b, 0, p)),
        ],
        out_specs=pl.BlockSpec((1, tq, LANES), lambda b, p, i: (b, i, p)),
        out_shape=jax.ShapeDtypeStruct((batch, seq, OUT_B), BF16),
        compiler_params=pltpu.CompilerParams(
            dimension_semantics=("arbitrary", "arbitrary", "arbitrary"), vmem_limit_bytes=VMEM_LIMIT),
        name="fox",
    )(qkv, qkv, vv)
    return out.reshape(batch * seq, OUT_B)


def _post_kernel(x_ref, ya_ref, yb_ref, gmix_ref, wg_ref, bg_ref, wpa_ref, wpb_ref, wo_ref,
                 gffn_ref, wr_ref, rb_ref, x1_ref, h2_ref, gates_ref):
    d = x_ref.shape[1]
    x = x_ref[...]
    h = (x * lax.rsqrt(jnp.mean(x * x, axis=-1, keepdims=True) + EPS) * gmix_ref[...]).astype(BF16)
    gates = jax.nn.sigmoid(_dot(h, wg_ref[...]) + bg_ref[...])
    merged = gates[:, :d] * _dot(ya_ref[...], wpa_ref[...]) + gates[:, d:] * _dot(yb_ref[...], wpb_ref[...])
    x1 = x + _dot(merged.astype(BF16), wo_ref[...])
    x1_ref[...] = x1
    h2 = x1 * lax.rsqrt(jnp.mean(x1 * x1, axis=-1, keepdims=True) + EPS) * gffn_ref[...]
    h2_ref[...] = h2.astype(BF16)

    hh, hm, _ = _split3(h2)
    wr = wr_ref[...]
    wh = wr.astype(BF16)
    wl = (wr - wh.astype(F32)).astype(BF16)
    hh, hm = hh.astype(BF16), hm.astype(BF16)
    logits = _dot(hh, wh) + _dot(hm, wh) + _dot(hh, wl)
    scores = jax.nn.sigmoid(logits)
    biased = scores + rb_ref[...]
    lane = lax.broadcasted_iota(jnp.int32, scores.shape, 1)
    chosen = jnp.zeros(scores.shape, jnp.bool_)
    for _ in range(TOP_K):
        cur = jnp.where(chosen, -jnp.inf, biased)
        mx = jnp.max(cur, axis=-1, keepdims=True)
        first = jnp.min(jnp.where(cur == mx, lane, N_EXPERTS), axis=-1, keepdims=True)
        chosen = jnp.logical_or(chosen, lane == first)
    top = jnp.where(chosen, scores, 0.0)
    gates_ref[...] = top / jnp.sum(top, axis=-1, keepdims=True) * ROUTE_SCALE


def _post(xf, ya, yb, g_mix, w_gate, b_gate, w_proj_a, w_proj_b, w_out, g_ffn, w_router, router_bias):
    n, d = xf.shape
    tm = TM_POST
    const = lambda shape: pl.BlockSpec(shape, lambda i: (0,) * len(shape))
    row = lambda w: pl.BlockSpec((tm, w), lambda i: (i, 0))
    args = [xf, ya, yb, g_mix.reshape(1, d), w_gate.astype(BF16), b_gate.reshape(1, 2 * d),
            w_proj_a.astype(BF16), w_proj_b.astype(BF16), w_out.astype(BF16), g_ffn.reshape(1, d),
            w_router.astype(F32), router_bias.astype(F32).reshape(1, N_EXPERTS)]
    in_specs = [row(d), row(OUT_A), row(OUT_B)] + [const(a.shape) for a in args[3:]]
    return pl.pallas_call(
        _post_kernel,
        grid=(n // tm,),
        in_specs=in_specs,
        out_specs=[row(d), row(d), row(N_EXPERTS)],
        out_shape=[jax.ShapeDtypeStruct((n, d), F32), jax.ShapeDtypeStruct((n, d), BF16),
                   jax.ShapeDtypeStruct((n, N_EXPERTS), F32)],
        compiler_params=pltpu.CompilerParams(
            dimension_semantics=("arbitrary",), vmem_limit_bytes=VMEM_LIMIT),
        name="post",
    )(*args)


def _moe_kernel(x1_ref, h2_ref, gates_ref, wgu_ref, wd_ref, wgus_ref, wds_ref, o_ref):
    e = pl.program_id(1)
    h2 = h2_ref[...]

    def swiglu_mid(w):
        gu = _dot(h2, w)
        g, u = gu[:, :D_EXPERT], gu[:, D_EXPERT:]
        return g * jax.nn.sigmoid(g) * u

    @pl.when(e == 0)
    def _():
        o_ref[...] = x1_ref[...] + _dot(swiglu_mid(wgus_ref[...]).astype(BF16), wds_ref[...])

    gates = gates_ref[...]
    lane = lax.broadcasted_iota(jnp.int32, gates.shape, 1)
    gcol = jnp.sum(jnp.where(lane == e, gates, 0.0), axis=-1, keepdims=True)
    mid = swiglu_mid(wgu_ref[0]) * gcol
    o_ref[...] += _dot(mid.astype(BF16), wd_ref[0])


def _moe(x1, h2, gates, w_gate_e, w_up_e, w_down_e, w_gate_s, w_up_s, w_down_s):
    n, d = x1.shape
    tm = TM_MOE
    wgu = jnp.concatenate([w_gate_e.astype(BF16), w_up_e.astype(BF16)], axis=-1)
    wgus = jnp.concatenate([w_gate_s.astype(BF16), w_up_s.astype(BF16)], axis=-1)
    return pl.pallas_call(
        _moe_kernel,
        grid=(n // tm, N_EXPERTS),
        in_specs=[
            pl.BlockSpec((tm, d), lambda i, e: (i, 0)),
            pl.BlockSpec((tm, d), lambda i, e: (i, 0)),
            pl.BlockSpec((tm, N_EXPERTS), lambda i, e: (i, 0)),
            pl.BlockSpec((1, d, 2 * D_EXPERT), lambda i, e: (e, 0, 0)),
            pl.BlockSpec((1, D_EXPERT, d), lambda i, e: (e, 0, 0)),
            pl.BlockSpec((d, 2 * D_SHARED), lambda i, e: (0, 0)),
            pl.BlockSpec((D_SHARED, d), lambda i, e: (0, 0)),
        ],
        out_specs=pl.BlockSpec((tm, d), lambda i, e: (i, 0)),
        out_shape=jax.ShapeDtypeStruct((n, d), F32),
        compiler_params=pltpu.CompilerParams(
            dimension_semantics=("arbitrary", "arbitrary"), vmem_limit_bytes=VMEM_LIMIT),
        name="moe",
    )(x1, h2, gates, wgu, w_down_e.astype(BF16), wgus, w_down_s.astype(BF16))


def kernel(x, g_mix, w_in, q_norm_a, k_norm_a, q_norm_b, k_norm_b, rel_bias, b_forget, w_gate, b_gate,
           w_proj_a, w_proj_b, w_out, g_ffn, w_router, router_bias, w_gate_e, w_up_e, w_down_e,
           w_gate_s, w_up_s, w_down_s):
    batch, seq, d = x.shape
    xf = x.reshape(batch * seq, d)
    pa, qkb, vb = _inproj(xf, g_mix, w_in, q_norm_a, k_norm_a, q_norm_b, k_norm_b, b_forget, seq)

    prev = None
    for g, (_, dil) in enumerate(DIL_GROUPS):
        last = g == len(DIL_GROUPS) - 1
        prev = _dilated_group(pa, _toeplitz_bias(rel_bias, g, dil), prev, g, dil, batch, seq, last)
    ya = prev[0]

    yb = _fox(qkb, vb, batch, seq)
    x1, h2, gates = _post(xf, ya, yb, g_mix, w_gate, b_gate, w_proj_a, w_proj_b, w_out, g_ffn,
                          w_router, router_bias)
    out = _moe(x1, h2, gates, w_gate_e, w_up_e, w_down_e, w_gate_s, w_up_s, w_down_s)
    return out.reshape(batch, seq, d)
```

```python
import functools
import math

import jax
import jax.numpy as jnp
import numpy as np
from jax import lax
from jax.experimental import pallas as pl
from jax.experimental.pallas import tpu as pltpu

D_MODEL = 1024
HEAD_DIM = 64
DIL_GROUPS = ((128, 1), (512, 4), (2048, 16))
HEADS_PER_GROUP = 4
N_HEADS_A = HEADS_PER_GROUP * len(DIL_GROUPS)
N_HEADS_B = 8
REL_BUCKETS = 32
REL_MAX_DIST = 2048
N_EXPERTS = 64
TOP_K = 8
D_EXPERT = 256
D_SHARED = 256
ROUTE_SCALE = 2.5
EPS = 1e-6

WIDTH_A = 3 * N_HEADS_A * HEAD_DIM
WIDTH_B = 3 * N_HEADS_B * HEAD_DIM
QK_B = N_HEADS_B * HEAD_DIM
OUT_A = HEADS_PER_GROUP * HEAD_DIM
OUT_B = N_HEADS_B * HEAD_DIM

LANES = 128
GROUP_W = HEADS_PER_GROUP * HEAD_DIM
WIN_J = 128
SUPER = DIL_GROUPS[-1][1] * WIN_J
NEG = -1e30
VMEM_LIMIT = 56 * 1024 * 1024

TM_IN = 512
TM_POST = 512
TQ_FOX = 512
TM_MOE = 1024

BF16 = jnp.bfloat16
F32 = jnp.float32


def _dot(a, b):
    return jnp.dot(a, b, preferred_element_type=F32)


def _dot_nt(a, b):
    return lax.dot_general(a, b, (((1,), (1,)), ((), ())), preferred_element_type=F32)


def _split3(v):
    hi = v.astype(BF16).astype(F32)
    r = v - hi
    mid = r.astype(BF16).astype(F32)
    lo = (r - mid).astype(BF16).astype(F32)
    return hi, mid, lo


def _inproj_kernel(x_ref, g_ref, wa_ref, wb_ref, wf_ref, bd_ref, tri_ref, gain_a_ref, gain_b_ref,
                   bf_ref, pa0_ref, pa1_ref, pa2_ref, qkb_ref, vb_ref, carry_ref, h_ref, *, tiles_per_seq):
    tm = x_ref.shape[0]
    x = x_ref[...]
    h = x * lax.rsqrt(jnp.mean(x * x, axis=-1, keepdims=True) + EPS) * g_ref[...]
    n_lane_chunks = h_ref.shape[0]
    for c in range(n_lane_chunks):
        h_ref[c] = h[:, c * LANES:(c + 1) * LANES]
    h = h.astype(BF16)
    bd = bd_ref[...]

    def headnorm(p, gain):
        ms = _dot((p * p).astype(BF16), bd)
        return p * lax.rsqrt(ms + EPS) * gain

    for g, (pa_ref, (_, dil)) in enumerate(zip((pa0_ref, pa1_ref, pa2_ref), DIL_GROUPS)):
        rows = tm // dil
        if dil == 1:
            hg = h
        else:
            hg = jnp.concatenate([jnp.concatenate(
                [h_ref[c, pl.ds(r, rows, stride=dil), :] for c in range(n_lane_chunks)], axis=1)
                for r in range(dil)], axis=0).astype(BF16)
        for part in range(3):
            cols = slice(part * GROUP_W, (part + 1) * GROUP_W)
            p = _dot(hg, wa_ref[g, :, cols])
            if part < 2:
                p = headnorm(p, gain_a_ref[part:part + 1, :])
            p = p.astype(BF16)
            if dil == 1:
                pa_ref[:, cols] = p
            else:
                for r in range(dil):
                    pa_ref[0, r, :, cols] = p[r * rows:(r + 1) * rows, :]

    f = _dot(h, wf_ref[...]) + bf_ref[...]
    logf = jnp.minimum(f, 0.0) - jnp.log1p(jnp.exp(-jnp.abs(f)))
    tri = tri_ref[...]
    lh, lm, ll = _split3(logf)
    cum = _dot(tri, lh.astype(BF16)) + _dot(tri, lm.astype(BF16)) + _dot(tri, ll.astype(BF16))

    @pl.when(pl.program_id(0) % tiles_per_seq == 0)
    def _():
        carry_ref[...] = jnp.zeros_like(carry_ref)

    cum = cum + carry_ref[0:1, :]
    carry_ref[0:1, :] = cum[tm - 1:tm, :]
    ch, cm, cl = _split3(cum)

    j = lax.broadcasted_iota(jnp.int32, (tm, HEAD_DIM), 1)
    n_chunk_b = QK_B // GROUP_W
    for part in range(2):
        for c in range(n_chunk_b):
            cols = slice(part * QK_B + c * GROUP_W, part * QK_B + (c + 1) * GROUP_W)
            p = headnorm(_dot(h, wb_ref[:, cols]), gain_b_ref[part:part + 1, :])
            for hh in range(HEADS_PER_GROUP):
                head = c * HEADS_PER_GROUP + hh
                a, b_, c_ = (ch[:, head:head + 1], cm[:, head:head + 1], cl[:, head:head + 1])
                if part == 0:
                    ext = jnp.where(j == 0, a, jnp.where(j == 1, b_, jnp.where(
                        j == 2, c_, jnp.where(j < 6, 1.0, 0.0))))
                else:
                    ext = jnp.where(j < 3, 1.0, jnp.where(j == 3, -a, jnp.where(
                        j == 4, -b_, jnp.where(j == 5, -c_, 0.0))))
                piece = jnp.concatenate([p[:, hh * HEAD_DIM:(hh + 1) * HEAD_DIM], ext], axis=-1)
                o0 = (part * N_HEADS_B + head) * LANES
                qkb_ref[:, o0:o0 + LANES] = piece.astype(BF16)
    for c in range(QK_B // GROUP_W):
        cols = slice(2 * QK_B + c * GROUP_W, 2 * QK_B + (c + 1) * GROUP_W)
        vb_ref[:, c * GROUP_W:(c + 1) * GROUP_W] = _dot(h, wb_ref[:, cols]).astype(BF16)


def _inproj(xf, g_mix, w_in, q_norm_a, k_norm_a, q_norm_b, k_norm_b, b_forget, seq):
    n, d = xf.shape
    tm = TM_IN
    scale = HEAD_DIM ** -0.5
    w_bf = w_in.astype(BF16)
    qkv_w = N_HEADS_A * HEAD_DIM
    wa = jnp.stack([jnp.concatenate(
        [w_bf[:, part * qkv_w + g * GROUP_W: part * qkv_w + (g + 1) * GROUP_W] for part in range(3)],
        axis=1) for g in range(len(DIL_GROUPS))])
    wb = w_bf[:, WIDTH_A:WIDTH_A + WIDTH_B]
    wf = jnp.pad(w_bf[:, WIDTH_A + WIDTH_B:], ((0, 0), (0, LANES - N_HEADS_B)))
    bfp = jnp.pad(b_forget.astype(F32), (0, LANES - N_HEADS_B)).reshape(1, LANES)
    seg = np.arange(GROUP_W) // HEAD_DIM
    bd = jnp.asarray((seg[:, None] == seg[None, :]).astype(np.float32) / HEAD_DIM, BF16)
    tri = jnp.asarray(np.tril(np.ones((tm, tm), np.float32)), BF16)
    gain_a = jnp.stack([jnp.tile(q_norm_a, HEADS_PER_GROUP) * scale, jnp.tile(k_norm_a, HEADS_PER_GROUP)])
    gain_b = jnp.stack([jnp.tile(q_norm_b, HEADS_PER_GROUP) * scale, jnp.tile(k_norm_b, HEADS_PER_GROUP)])
    const = lambda shape: pl.BlockSpec(shape, lambda i: (0,) * len(shape))
    tps = seq // tm
    batch = n // seq
    qkv3 = 3 * GROUP_W
    (_, d1), (_, d2) = DIL_GROUPS[1], DIL_GROUPS[2]
    return pl.pallas_call(
        functools.partial(_inproj_kernel, tiles_per_seq=tps),
        grid=(n // tm,),
        in_specs=[
            pl.BlockSpec((tm, d), lambda i: (i, 0)),
            const((1, d)), const(wa.shape), const(wb.shape), const(wf.shape),
            const(bd.shape), const(tri.shape), const(gain_a.shape), const(gain_b.shape),
            const(bfp.shape),
        ],
        out_specs=[
            pl.BlockSpec((tm, qkv3), lambda i: (i, 0)),
            pl.BlockSpec((1, d1, tm // d1, qkv3), lambda i: (i // tps, 0, i % tps, 0)),
            pl.BlockSpec((1, d2, tm // d2, qkv3), lambda i: (i // tps, 0, i % tps, 0)),
            pl.BlockSpec((tm, 2 * N_HEADS_B * LANES), lambda i: (i, 0)),
            pl.BlockSpec((tm, OUT_B), lambda i: (i, 0)),
        ],
        out_shape=[
            jax.ShapeDtypeStruct((n, qkv3), BF16),
            jax.ShapeDtypeStruct((batch, d1, seq // d1, qkv3), BF16),
            jax.ShapeDtypeStruct((batch, d2, seq // d2, qkv3), BF16),
            jax.ShapeDtypeStruct((n, 2 * N_HEADS_B * LANES), BF16),
            jax.ShapeDtypeStruct((n, OUT_B), BF16),
        ],
        scratch_shapes=[pltpu.VMEM((8, LANES), F32), pltpu.VMEM((d // LANES, tm, LANES), F32)],
        compiler_params=pltpu.CompilerParams(
            dimension_semantics=("arbitrary",), vmem_limit_bytes=VMEM_LIMIT),
        name="inproj",
    )(xf, g_mix.reshape(1, d), wa, wb, wf, bd, tri, gain_a, gain_b, bfp)


def _dilated_kernel(p0_ref, h0_ref, p1_ref, h1_ref, p2_ref, h2_ref, bias_ref, o_ref, acc_ref, lse_ref):
    tq = WIN_J
    first_sb = pl.program_id(1) == 0
    lane_head = lax.broadcasted_iota(jnp.int32, (tq, GROUP_W), 1) // HEAD_DIM
    prev_col = lax.broadcasted_iota(jnp.int32, (tq, 2 * tq), 1) < tq
    qc, kc_, vc_ = (slice(0, GROUP_W), slice(GROUP_W, 2 * GROUP_W), slice(2 * GROUP_W, 3 * GROUP_W))

    def attend(g, q, kp, kc, vp, vc, no_prev):
        kcat = jnp.concatenate([kp, kc], axis=0)
        vcat = jnp.concatenate([vp, vc], axis=0)
        dead = jnp.logical_and(no_prev, prev_col)
        q4 = jnp.concatenate([jnp.where(lane_head == hh, q, jnp.zeros_like(q))
                              for hh in range(HEADS_PER_GROUP)], axis=0)
        s = _dot_nt(q4, kcat) + bias_ref[g].reshape(HEADS_PER_GROUP * tq, 2 * tq)
        s = jnp.where(jnp.concatenate([dead] * HEADS_PER_GROUP, axis=0), NEG, s)
        m = jnp.max(s, axis=-1, keepdims=True)
        p = jnp.exp(s - m)
        l = jnp.sum(p, axis=-1, keepdims=True)
        o4 = _dot(p.astype(BF16), vcat) * (1.0 / l)
        lse4 = m + jnp.log(l)
        acc = o4[0:tq]
        lse = jnp.broadcast_to(lse4[0:tq], (tq, GROUP_W))
        for hh in range(1, HEADS_PER_GROUP):
            sel = lane_head == hh
            acc = jnp.where(sel, o4[hh * tq:(hh + 1) * tq], acc)
            lse = jnp.where(sel, lse4[hh * tq:(hh + 1) * tq], lse)
        return acc, lse

    n_half = GROUP_W // LANES

    def merge(rows, acc, lse):
        for c in range(n_half):
            lanes = slice(c * LANES, (c + 1) * LANES)
            l1 = lse_ref[c, rows, :]
            mx = jnp.maximum(l1, lse[:, lanes])
            w1 = jnp.exp(l1 - mx)
            w2 = jnp.exp(lse[:, lanes] - mx)
            den = w1 + w2
            acc_ref[c, rows, :] = (w1 * acc_ref[c, rows, :] + w2 * acc[:, lanes]) / den
            lse_ref[c, rows, :] = mx + jnp.log(den)

    def pick(first, halo, body):
        return jnp.where(first, halo, body)

    def body0(j, carry):
        st = pl.multiple_of(j * tq, tq)
        pst = pl.multiple_of(jnp.maximum(j - 1, 0) * tq, tq)
        cur, prv = pl.ds(st, tq), pl.ds(pst, tq)
        acc, lse = attend(
            0, p0_ref[0, cur, qc],
            pick(j == 0, h0_ref[0, :, kc_], p0_ref[0, prv, kc_]), p0_ref[0, cur, kc_],
            pick(j == 0, h0_ref[0, :, vc_], p0_ref[0, prv, vc_]), p0_ref[0, cur, vc_],
            jnp.logical_and(j == 0, first_sb))
        for c in range(n_half):
            acc_ref[c, cur, :] = acc[:, c * LANES:(c + 1) * LANES]
            lse_ref[c, cur, :] = lse[:, c * LANES:(c + 1) * LANES]
        return carry

    lax.fori_loop(0, SUPER // tq, body0, 0)

    d1 = DIL_GROUPS[1][1]
    nsub1 = SUPER // d1 // tq
    def body1(t, carry):
        r, ii = t // nsub1, t % nsub1
        st = pl.multiple_of(ii * tq, tq)
        pst = pl.multiple_of(jnp.maximum(ii - 1, 0) * tq, tq)
        cur, prv = pl.ds(st, tq), pl.ds(pst, tq)
        acc, lse = attend(
            1, p1_ref[0, r, cur, qc],
            pick(ii == 0, h1_ref[0, r, :, kc_], p1_ref[0, r, prv, kc_]), p1_ref[0, r, cur, kc_],
            pick(ii == 0, h1_ref[0, r, :, vc_], p1_ref[0, r, prv, vc_]), p1_ref[0, r, cur, vc_],
            jnp.logical_and(ii == 0, first_sb))
        merge(pl.ds(ii * (tq * d1) + r, tq, stride=d1), acc, lse)
        return carry

    lax.fori_loop(0, d1 * nsub1, body1, 0)

    d2 = DIL_GROUPS[2][1]

    def body2(r, carry):
        acc, lse = attend(2, p2_ref[0, r, :, qc], h2_ref[0, r, :, kc_], p2_ref[0, r, :, kc_],
                          h2_ref[0, r, :, vc_], p2_ref[0, r, :, vc_], first_sb)
        merge(pl.ds(r, tq, stride=d2), acc, lse)
        return carry

    lax.fori_loop(0, d2, body2, 0)

    for c in range(n_half):
        o_ref[0, :, c * LANES:(c + 1) * LANES] = acc_ref[c].astype(o_ref.dtype)


def _rel_bucket(dist):
    max_exact = REL_BUCKETS // 2
    n = jnp.maximum(dist.astype(F32), 1.0)
    large = max_exact + (jnp.log(n / max_exact) / math.log(REL_MAX_DIST / max_exact)
                         * (REL_BUCKETS - max_exact)).astype(jnp.int32)
    large = jnp.minimum(large, REL_BUCKETS - 1)
    return jnp.where(dist < max_exact, dist, large)


def _toeplitz_bias(rel_bias, g, dil):
    tq = WIN_J
    offs = dil * jnp.arange(WIN_J + 1, dtype=jnp.int32)
    hs = slice(g * HEADS_PER_GROUP, (g + 1) * HEADS_PER_GROUP)
    tab = rel_bias[_rel_bucket(offs)][:, hs].T.astype(F32)
    period = 3 * tq
    neg = lambda w: jnp.full((HEADS_PER_GROUP, w), NEG, F32)
    vec = jnp.concatenate([neg(tq - 1), tab[:, ::-1], neg(period - 2 * tq)], axis=1)
    flat = jnp.broadcast_to(vec[:, None, :], (HEADS_PER_GROUP, tq, period)).reshape(HEADS_PER_GROUP, -1)
    skew = flat[:, :tq * (period - 1)].reshape(HEADS_PER_GROUP, tq, period - 1)
    return skew[:, :, tq - 1:3 * tq - 1]


def _dilated(pa0, pa1, pa2, bias, batch, seq):
    tq = WIN_J
    qkv3 = 3 * GROUP_W
    (_, d1), (_, d2) = DIL_GROUPS[1], DIL_GROUPS[2]
    nsb = seq // SUPER
    p0 = pa0.reshape(batch, seq, qkv3)
    prev_blk = lambda per_sb: (lambda b, s: jnp.maximum(s * per_sb - 1, 0))
    h0i, h1i, h2i = prev_blk(SUPER // tq), prev_blk(SUPER // d1 // tq), prev_blk(SUPER // d2 // tq)
    out = pl.pallas_call(
        _dilated_kernel,
        grid=(batch, nsb),
        in_specs=[
            pl.BlockSpec((1, SUPER, qkv3), lambda b, s: (b, s, 0)),
            pl.BlockSpec((1, tq, qkv3), lambda b, s: (b, h0i(b, s), 0)),
            pl.BlockSpec((1, d1, SUPER // d1, qkv3), lambda b, s: (b, 0, s, 0)),
            pl.BlockSpec((1, d1, tq, qkv3), lambda b, s: (b, 0, h1i(b, s), 0)),
            pl.BlockSpec((1, d2, SUPER // d2, qkv3), lambda b, s: (b, 0, s, 0)),
            pl.BlockSpec((1, d2, tq, qkv3), lambda b, s: (b, 0, h2i(b, s), 0)),
            pl.BlockSpec(bias.shape, lambda b, s: (0, 0, 0, 0)),
        ],
        out_specs=pl.BlockSpec((1, SUPER, GROUP_W), lambda b, s: (b, s, 0)),
        out_shape=jax.ShapeDtypeStruct((batch, seq, GROUP_W), BF16),
        scratch_shapes=[pltpu.VMEM((GROUP_W // LANES, SUPER, LANES), F32)] * 2,
        compiler_params=pltpu.CompilerParams(
            dimension_semantics=("arbitrary", "arbitrary"), vmem_limit_bytes=VMEM_LIMIT),
        name="dilated",
    )(p0, p0, pa1, pa1, pa2, pa2, bias)
    return out.reshape(batch * seq, GROUP_W)


def _fox_kernel(q_ref, k_ref, v_ref, o_ref):
    tq = q_ref.shape[1]
    tk = tq
    qi = pl.program_id(2)
    qs = [q_ref[0, :, 0:LANES], q_ref[0, :, LANES:2 * LANES]]
    lane_lo = lax.broadcasted_iota(jnp.int32, (tq, LANES), 1) < HEAD_DIM
    row = lax.broadcasted_iota(jnp.int32, (tq, tk), 0)
    col = lax.broadcasted_iota(jnp.int32, (tq, tk), 1)
    causal = row >= col

    def chunk(ki, carry, diagonal):
        ms, ls, acc = carry
        start = pl.multiple_of(ki * tk, tk)
        v2 = v_ref[0, pl.ds(start, tk), :]
        new_m, new_l, alphas, pvs = [], [], [], []
        for hh in range(2):
            kh = k_ref[0, pl.ds(start, tk), hh * LANES:(hh + 1) * LANES]
            s = _dot_nt(qs[hh], kh)
            if diagonal:
                s = jnp.where(causal, s, NEG)
            m_new = jnp.maximum(ms[hh], jnp.max(s, axis=-1, keepdims=True))
            alpha = jnp.exp(ms[hh] - m_new)
            p = jnp.exp(s - m_new)
            new_l.append(alpha * ls[hh] + jnp.sum(p, axis=-1, keepdims=True))
            new_m.append(m_new)
            alphas.append(alpha)
            pvs.append(_dot(p.astype(BF16), v2))
        acc = acc * jnp.where(lane_lo, alphas[0], alphas[1]) + jnp.where(lane_lo, pvs[0], pvs[1])
        return (tuple(new_m), tuple(new_l), acc)

    m0 = jnp.full((tq, 1), NEG, F32)
    l0 = jnp.zeros((tq, 1), F32)
    init = ((m0, m0), (l0, l0), jnp.zeros((tq, LANES), F32))
    carry = lax.fori_loop(0, qi, lambda ki, c: chunk(ki, c, False), init)
    ms, ls, acc = chunk(qi, carry, True)
    o_ref[0] = (acc / jnp.where(lane_lo, ls[0], ls[1])).astype(o_ref.dtype)


def _fox(qkb, vb, batch, seq):
    tq = TQ_FOX
    pairs = N_HEADS_B // 2
    qkv = qkb.reshape(batch, seq, 2 * N_HEADS_B * LANES)
    vv = vb.reshape(batch, seq, OUT_B)
    out = pl.pallas_call(
        _fox_kernel,
        grid=(batch, pairs, seq // tq),
        in_specs=[
            pl.BlockSpec((1, tq, 2 * LANES), lambda b, p, i: (b, i, p)),
            pl.BlockSpec((1, seq, 2 * LANES), lambda b, p, i: (b, 0, pairs + p)),
            pl.BlockSpec((1, seq, LANES), lambda b, p, i: (b, 0, p)),
        ],
        out_specs=pl.BlockSpec((1, tq, LANES), lambda b, p, i: (b, i, p)),
        out_shape=jax.ShapeDtypeStruct((batch, seq, OUT_B), BF16),
        compiler_params=pltpu.CompilerParams(
            dimension_semantics=("arbitrary", "arbitrary", "arbitrary"), vmem_limit_bytes=VMEM_LIMIT),
        name="fox",
    )(qkv, qkv, vv)
    return out.reshape(batch * seq, OUT_B)


def _post_kernel(x_ref, ya_ref, yb_ref, gmix_ref, wg_ref, bg_ref, wpa_ref, wpb_ref, wo_ref,
                 gffn_ref, wr_ref, rb_ref, x1_ref, h2_ref, gates_ref):
    d = x_ref.shape[1]
    x = x_ref[...]
    h = (x * lax.rsqrt(jnp.mean(x * x, axis=-1, keepdims=True) + EPS) * gmix_ref[...]).astype(BF16)
    gates = jax.nn.sigmoid(_dot(h, wg_ref[...]) + bg_ref[...])
    merged = gates[:, :d] * _dot(ya_ref[...], wpa_ref[...]) + gates[:, d:] * _dot(yb_ref[...], wpb_ref[...])
    x1 = x + _dot(merged.astype(BF16), wo_ref[...])
    x1_ref[...] = x1
    h2 = x1 * lax.rsqrt(jnp.mean(x1 * x1, axis=-1, keepdims=True) + EPS) * gffn_ref[...]
    h2_ref[...] = h2.astype(BF16)

    hh, hm, _ = _split3(h2)
    wr = wr_ref[...]
    wh = wr.astype(BF16)
    wl = (wr - wh.astype(F32)).astype(BF16)
    hh, hm = hh.astype(BF16), hm.astype(BF16)
    logits = _dot(hh, wh) + _dot(hm, wh) + _dot(hh, wl)
    scores = jax.nn.sigmoid(logits)
    biased = scores + rb_ref[...]
    lane = lax.broadcasted_iota(jnp.int32, scores.shape, 1)
    chosen = jnp.zeros(scores.shape, jnp.bool_)
    for _ in range(TOP_K):
        cur = jnp.where(chosen, -jnp.inf, biased)
        mx = jnp.max(cur, axis=-1, keepdims=True)
        first = jnp.min(jnp.where(cur == mx, lane, N_EXPERTS), axis=-1, keepdims=True)
        chosen = jnp.logical_or(chosen, lane == first)
    top = jnp.where(chosen, scores, 0.0)
    gates_ref[...] = top / jnp.sum(top, axis=-1, keepdims=True) * ROUTE_SCALE


def _post(xf, ya, yb, g_mix, w_gate, b_gate, w_proj_a, w_proj_b, w_out, g_ffn, w_router, router_bias):
    n, d = xf.shape
    tm = TM_POST
    const = lambda shape: pl.BlockSpec(shape, lambda i: (0,) * len(shape))
    row = lambda w: pl.BlockSpec((tm, w), lambda i: (i, 0))
    args = [xf, ya, yb, g_mix.reshape(1, d), w_gate.astype(BF16), b_gate.reshape(1, 2 * d),
            w_proj_a.astype(BF16), w_proj_b.astype(BF16), w_out.astype(BF16), g_ffn.reshape(1, d),
            w_router.astype(F32), router_bias.astype(F32).reshape(1, N_EXPERTS)]
    in_specs = [row(d), row(OUT_A), row(OUT_B)] + [const(a.shape) for a in args[3:]]
    return pl.pallas_call(
        _post_kernel,
        grid=(n // tm,),
        in_specs=in_specs,
        out_specs=[row(d), row(d), row(N_EXPERTS)],
        out_shape=[jax.ShapeDtypeStruct((n, d), F32), jax.ShapeDtypeStruct((n, d), BF16),
                   jax.ShapeDtypeStruct((n, N_EXPERTS), F32)],
        compiler_params=pltpu.CompilerParams(
            dimension_semantics=("arbitrary",), vmem_limit_bytes=VMEM_LIMIT),
        name="post",
    )(*args)


def _moe_kernel(x1_ref, h2_ref, gates_ref, wgu_ref, wd_ref, wgus_ref, wds_ref, o_ref):
    e = pl.program_id(1)
    h2 = h2_ref[...]

    def swiglu_mid(w):
        gu = _dot(h2, w)
        g, u = gu[:, :D_EXPERT], gu[:, D_EXPERT:]
        return g * jax.nn.sigmoid(g) * u

    @pl.when(e == 0)
    def _():
        o_ref[...] = x1_ref[...] + _dot(swiglu_mid(wgus_ref[...]).astype(BF16), wds_ref[...])

    gates = gates_ref[...]
    lane = lax.broadcasted_iota(jnp.int32, gates.shape, 1)
    gcol = jnp.sum(jnp.where(lane == e, gates, 0.0), axis=-1, keepdims=True)
    mid = swiglu_mid(wgu_ref[0]) * gcol
    o_ref[...] += _dot(mid.astype(BF16), wd_ref[0])


def _moe(x1, h2, gates, w_gate_e, w_up_e, w_down_e, w_gate_s, w_up_s, w_down_s):
    n, d = x1.shape
    tm = TM_MOE
    wgu = jnp.concatenate([w_gate_e.astype(BF16), w_up_e.astype(BF16)], axis=-1)
    wgus = jnp.concatenate([w_gate_s.astype(BF16), w_up_s.astype(BF16)], axis=-1)
    return pl.pallas_call(
        _moe_kernel,
        grid=(n // tm, N_EXPERTS),
        in_specs=[
            pl.BlockSpec((tm, d), lambda i, e: (i, 0)),
            pl.BlockSpec((tm, d), lambda i, e: (i, 0)),
            pl.BlockSpec((tm, N_EXPERTS), lambda i, e: (i, 0)),
            pl.BlockSpec((1, d, 2 * D_EXPERT), lambda i, e: (e, 0, 0)),
            pl.BlockSpec((1, D_EXPERT, d), lambda i, e: (e, 0, 0)),
            pl.BlockSpec((d, 2 * D_SHARED), lambda i, e: (0, 0)),
            pl.BlockSpec((D_SHARED, d), lambda i, e: (0, 0)),
        ],
        out_specs=pl.BlockSpec((tm, d), lambda i, e: (i, 0)),
        out_shape=jax.ShapeDtypeStruct((n, d), F32),
        compiler_params=pltpu.CompilerParams(
            dimension_semantics=("arbitrary", "arbitrary"), vmem_limit_bytes=VMEM_LIMIT),
        name="moe",
    )(x1, h2, gates, wgu, w_down_e.astype(BF16), wgus, w_down_s.astype(BF16))


def kernel(x, g_mix, w_in, q_norm_a, k_norm_a, q_norm_b, k_norm_b, rel_bias, b_forget, w_gate, b_gate,
           w_proj_a, w_proj_b, w_out, g_ffn, w_router, router_bias, w_gate_e, w_up_e, w_down_e,
           w_gate_s, w_up_s, w_down_s):
    batch, seq, d = x.shape
    xf = x.reshape(batch * seq, d)
    pa0, pa1, pa2, qkb, vb = _inproj(xf, g_mix, w_in, q_norm_a, k_norm_a, q_norm_b, k_norm_b, b_forget, seq)
    bias = jnp.stack([_toeplitz_bias(rel_bias, g, dil) for g, (_, dil) in enumerate(DIL_GROUPS)])
    ya = _dilated(pa0, pa1, pa2, bias, batch, seq)

    yb = _fox(qkb, vb, batch, seq)
    x1, h2, gates = _post(xf, ya, yb, g_mix, w_gate, b_gate, w_proj_a, w_proj_b, w_out, g_ffn,
                          w_router, router_bias)
    out = _moe(x1, h2, gates, w_gate_e, w_up_e, w_down_e, w_gate_s, w_up_s, w_down_s)
    return out.reshape(batch, seq, d)
```

```python
import functools
import math

import jax
import jax.numpy as jnp
import numpy as np
from jax import lax
from jax.experimental import pallas as pl
from jax.experimental.pallas import tpu as pltpu

D_MODEL = 1024
HEAD_DIM = 64
DIL_GROUPS = ((128, 1), (512, 4), (2048, 16))
HEADS_PER_GROUP = 4
N_HEADS_A = HEADS_PER_GROUP * len(DIL_GROUPS)
N_HEADS_B = 8
REL_BUCKETS = 32
REL_MAX_DIST = 2048
N_EXPERTS = 64
TOP_K = 8
D_EXPERT = 256
D_SHARED = 256
ROUTE_SCALE = 2.5
EPS = 1e-6

WIDTH_A = 3 * N_HEADS_A * HEAD_DIM
WIDTH_B = 3 * N_HEADS_B * HEAD_DIM
QK_B = N_HEADS_B * HEAD_DIM
OUT_A = HEADS_PER_GROUP * HEAD_DIM
OUT_B = N_HEADS_B * HEAD_DIM

LANES = 128
GROUP_W = HEADS_PER_GROUP * HEAD_DIM
WIN_J = 128
SUPER = DIL_GROUPS[-1][1] * WIN_J
NEG = -1e30
VMEM_LIMIT = 56 * 1024 * 1024

TM_IN = 512
TM_POST = 512
TQ_FOX = 1024
TB = 256
GRAN = 16
CAP = TB * TOP_K + N_EXPERTS * GRAN
SLOT_CHUNK = 512
BPG = 16
FT = 256
FT_BIG = 1024

BF16 = jnp.bfloat16
F32 = jnp.float32


def _dot(a, b):
    return jnp.dot(a, b, preferred_element_type=F32)


def _dot_nt(a, b):
    return lax.dot_general(a, b, (((1,), (1,)), ((), ())), preferred_element_type=F32)


def _split3(v):
    hi = v.astype(BF16).astype(F32)
    r = v - hi
    mid = r.astype(BF16).astype(F32)
    lo = (r - mid).astype(BF16).astype(F32)
    return hi, mid, lo


def _inproj_kernel(x_ref, g_ref, wa_ref, wb_ref, wf_ref, bd_ref, tri_ref, gain_a_ref, gain_b_ref,
                   bf_ref, pa0_ref, pa1_ref, pa2_ref, qkb_ref, vb_ref, carry_ref, h_ref, *, tiles_per_seq):
    tm = x_ref.shape[0]
    x = x_ref[...]
    h = x * lax.rsqrt(jnp.mean(x * x, axis=-1, keepdims=True) + EPS) * g_ref[...]
    n_lane_chunks = h_ref.shape[0]
    for c in range(n_lane_chunks):
        h_ref[c] = h[:, c * LANES:(c + 1) * LANES]
    h = h.astype(BF16)
    bd = bd_ref[...]

    def headnorm(p, gain):
        ms = _dot((p * p).astype(BF16), bd)
        return p * lax.rsqrt(ms + EPS) * gain

    for g, (pa_ref, (_, dil)) in enumerate(zip((pa0_ref, pa1_ref, pa2_ref), DIL_GROUPS)):
        rows = tm // dil
        if dil == 1:
            hg = h
        else:
            hg = jnp.concatenate([jnp.concatenate(
                [h_ref[c, pl.ds(r, rows, stride=dil), :] for c in range(n_lane_chunks)], axis=1)
                for r in range(dil)], axis=0).astype(BF16)
        for part in range(3):
            cols = slice(part * GROUP_W, (part + 1) * GROUP_W)
            p = _dot(hg, wa_ref[g, :, cols])
            if part < 2:
                p = headnorm(p, gain_a_ref[part:part + 1, :])
            p = p.astype(BF16)
            if dil == 1:
                pa_ref[:, cols] = p
            else:
                for r in range(dil):
                    pa_ref[0, r, :, cols] = p[r * rows:(r + 1) * rows, :]

    f = _dot(h, wf_ref[...]) + bf_ref[...]
    logf = jnp.minimum(f, 0.0) - jnp.log1p(jnp.exp(-jnp.abs(f)))
    tri = tri_ref[...]
    lh, lm, ll = _split3(logf)
    cum = _dot(tri, lh.astype(BF16)) + _dot(tri, lm.astype(BF16)) + _dot(tri, ll.astype(BF16))

    @pl.when(pl.program_id(0) % tiles_per_seq == 0)
    def _():
        carry_ref[...] = jnp.zeros_like(carry_ref)

    cum = cum + carry_ref[0:1, :]
    carry_ref[0:1, :] = cum[tm - 1:tm, :]
    ch, cm, cl = _split3(cum)

    j = lax.broadcasted_iota(jnp.int32, (tm, HEAD_DIM), 1)
    n_chunk_b = QK_B // GROUP_W
    for part in range(2):
        for c in range(n_chunk_b):
            cols = slice(part * QK_B + c * GROUP_W, part * QK_B + (c + 1) * GROUP_W)
            p = headnorm(_dot(h, wb_ref[:, cols]), gain_b_ref[part:part + 1, :])
            for hh in range(HEADS_PER_GROUP):
                head = c * HEADS_PER_GROUP + hh
                a, b_, c_ = (ch[:, head:head + 1], cm[:, head:head + 1], cl[:, head:head + 1])
                if part == 0:
                    ext = jnp.where(j == 0, a, jnp.where(j == 1, b_, jnp.where(
                        j == 2, c_, jnp.where(j < 6, 1.0, 0.0))))
                else:
                    ext = jnp.where(j < 3, 1.0, jnp.where(j == 3, -a, jnp.where(
                        j == 4, -b_, jnp.where(j == 5, -c_, 0.0))))
                piece = jnp.concatenate([p[:, hh * HEAD_DIM:(hh + 1) * HEAD_DIM], ext], axis=-1)
                o0 = (part * N_HEADS_B + head) * LANES
                qkb_ref[:, o0:o0 + LANES] = piece.astype(BF16)
    for c in range(QK_B // GROUP_W):
        cols = slice(2 * QK_B + c * GROUP_W, 2 * QK_B + (c + 1) * GROUP_W)
        vb_ref[:, c * GROUP_W:(c + 1) * GROUP_W] = _dot(h, wb_ref[:, cols]).astype(BF16)


def _inproj(xf, g_mix, w_in, q_norm_a, k_norm_a, q_norm_b, k_norm_b, b_forget, seq):
    n, d = xf.shape
    tm = TM_IN
    scale = HEAD_DIM ** -0.5
    w_bf = w_in.astype(BF16)
    qkv_w = N_HEADS_A * HEAD_DIM
    wa = jnp.stack([jnp.concatenate(
        [w_bf[:, part * qkv_w + g * GROUP_W: part * qkv_w + (g + 1) * GROUP_W] for part in range(3)],
        axis=1) for g in range(len(DIL_GROUPS))])
    wb = w_bf[:, WIDTH_A:WIDTH_A + WIDTH_B]
    wf = jnp.pad(w_bf[:, WIDTH_A + WIDTH_B:], ((0, 0), (0, LANES - N_HEADS_B)))
    bfp = jnp.pad(b_forget.astype(F32), (0, LANES - N_HEADS_B)).reshape(1, LANES)
    seg = np.arange(GROUP_W) // HEAD_DIM
    bd = jnp.asarray((seg[:, None] == seg[None, :]).astype(np.float32) / HEAD_DIM, BF16)
    tri = jnp.asarray(np.tril(np.ones((tm, tm), np.float32)), BF16)
    gain_a = jnp.stack([jnp.tile(q_norm_a, HEADS_PER_GROUP) * scale, jnp.tile(k_norm_a, HEADS_PER_GROUP)])
    gain_b = jnp.stack([jnp.tile(q_norm_b, HEADS_PER_GROUP) * scale, jnp.tile(k_norm_b, HEADS_PER_GROUP)])
    const = lambda shape: pl.BlockSpec(shape, lambda i: (0,) * len(shape))
    tps = seq // tm
    batch = n // seq
    qkv3 = 3 * GROUP_W
    (_, d1), (_, d2) = DIL_GROUPS[1], DIL_GROUPS[2]
    return pl.pallas_call(
        functools.partial(_inproj_kernel, tiles_per_seq=tps),
        grid=(n // tm,),
        in_specs=[
            pl.BlockSpec((tm, d), lambda i: (i, 0)),
            const((1, d)), const(wa.shape), const(wb.shape), const(wf.shape),
            const(bd.shape), const(tri.shape), const(gain_a.shape), const(gain_b.shape),
            const(bfp.shape),
        ],
        out_specs=[
            pl.BlockSpec((tm, qkv3), lambda i: (i, 0)),
            pl.BlockSpec((1, d1, tm // d1, qkv3), lambda i: (i // tps, 0, i % tps, 0)),
            pl.BlockSpec((1, d2, tm // d2, qkv3), lambda i: (i // tps, 0, i % tps, 0)),
            pl.BlockSpec((tm, 2 * N_HEADS_B * LANES), lambda i: (i, 0)),
            pl.BlockSpec((tm, OUT_B), lambda i: (i, 0)),
        ],
        out_shape=[
            jax.ShapeDtypeStruct((n, qkv3), BF16),
            jax.ShapeDtypeStruct((batch, d1, seq // d1, qkv3), BF16),
            jax.ShapeDtypeStruct((batch, d2, seq // d2, qkv3), BF16),
            jax.ShapeDtypeStruct((n, 2 * N_HEADS_B * LANES), BF16),
            jax.ShapeDtypeStruct((n, OUT_B), BF16),
        ],
        scratch_shapes=[pltpu.VMEM((8, LANES), F32), pltpu.VMEM((d // LANES, tm, LANES), F32)],
        compiler_params=pltpu.CompilerParams(
            dimension_semantics=("arbitrary",), vmem_limit_bytes=VMEM_LIMIT),
        name="inproj",
    )(xf, g_mix.reshape(1, d), wa, wb, wf, bd, tri, gain_a, gain_b, bfp)


def _dilated_kernel(p0_ref, h0_ref, p1_ref, h1_ref, p2_ref, h2_ref, bias_ref, o_ref, acc_ref, lse_ref):
    tq = WIN_J
    first_sb = pl.program_id(1) == 0
    lane_head = lax.broadcasted_iota(jnp.int32, (tq, GROUP_W), 1) // HEAD_DIM
    prev_col = lax.broadcasted_iota(jnp.int32, (tq, 2 * tq), 1) < tq
    qc, kc_, vc_ = (slice(0, GROUP_W), slice(GROUP_W, 2 * GROUP_W), slice(2 * GROUP_W, 3 * GROUP_W))

    def attend(g, q, kp, kc, vp, vc, no_prev):
        kcat = jnp.concatenate([kp, kc], axis=0)
        vcat = jnp.concatenate([vp, vc], axis=0)
        dead = jnp.logical_and(no_prev, prev_col)
        q4 = jnp.concatenate([jnp.where(lane_head == hh, q, jnp.zeros_like(q))
                              for hh in range(HEADS_PER_GROUP)], axis=0)
        s = _dot_nt(q4, kcat) + bias_ref[g].reshape(HEADS_PER_GROUP * tq, 2 * tq)
        s = jnp.where(jnp.concatenate([dead] * HEADS_PER_GROUP, axis=0), NEG, s)
        m = jnp.max(s, axis=-1, keepdims=True)
        p = jnp.exp(s - m)
        l = jnp.sum(p, axis=-1, keepdims=True)
        o4 = _dot(p.astype(BF16), vcat) * (1.0 / l)
        lse4 = m + jnp.log(l)
        acc = o4[0:tq]
        lse = jnp.broadcast_to(lse4[0:tq], (tq, GROUP_W))
        for hh in range(1, HEADS_PER_GROUP):
            sel = lane_head == hh
            acc = jnp.where(sel, o4[hh * tq:(hh + 1) * tq], acc)
            lse = jnp.where(sel, lse4[hh * tq:(hh + 1) * tq], lse)
        return acc, lse

    n_half = GROUP_W // LANES

    def merge(rows, acc, lse):
        for c in range(n_half):
            lanes = slice(c * LANES, (c + 1) * LANES)
            l1 = lse_ref[c, rows, :]
            mx = jnp.maximum(l1, lse[:, lanes])
            w1 = jnp.exp(l1 - mx)
            w2 = jnp.exp(lse[:, lanes] - mx)
            den = w1 + w2
            acc_ref[c, rows, :] = (w1 * acc_ref[c, rows, :] + w2 * acc[:, lanes]) / den
            lse_ref[c, rows, :] = mx + jnp.log(den)

    def pick(first, halo, body):
        return jnp.where(first, halo, body)

    def body0(j, carry):
        st = pl.multiple_of(j * tq, tq)
        pst = pl.multiple_of(jnp.maximum(j - 1, 0) * tq, tq)
        cur, prv = pl.ds(st, tq), pl.ds(pst, tq)
        acc, lse = attend(
            0, p0_ref[0, cur, qc],
            pick(j == 0, h0_ref[0, :, kc_], p0_ref[0, prv, kc_]), p0_ref[0, cur, kc_],
            pick(j == 0, h0_ref[0, :, vc_], p0_ref[0, prv, vc_]), p0_ref[0, cur, vc_],
            jnp.logical_and(j == 0, first_sb))
        for c in range(n_half):
            acc_ref[c, cur, :] = acc[:, c * LANES:(c + 1) * LANES]
            lse_ref[c, cur, :] = lse[:, c * LANES:(c + 1) * LANES]
        return carry

    lax.fori_loop(0, SUPER // tq, body0, 0)

    d1 = DIL_GROUPS[1][1]
    nsub1 = SUPER // d1 // tq
    def body1(t, carry):
        r, ii = t // nsub1, t % nsub1
        st = pl.multiple_of(ii * tq, tq)
        pst = pl.multiple_of(jnp.maximum(ii - 1, 0) * tq, tq)
        cur, prv = pl.ds(st, tq), pl.ds(pst, tq)
        acc, lse = attend(
            1, p1_ref[0, r, cur, qc],
            pick(ii == 0, h1_ref[0, r, :, kc_], p1_ref[0, r, prv, kc_]), p1_ref[0, r, cur, kc_],
            pick(ii == 0, h1_ref[0, r, :, vc_], p1_ref[0, r, prv, vc_]), p1_ref[0, r, cur, vc_],
            jnp.logical_and(ii == 0, first_sb))
        merge(pl.ds(ii * (tq * d1) + r, tq, stride=d1), acc, lse)
        return carry

    lax.fori_loop(0, d1 * nsub1, body1, 0)

    d2 = DIL_GROUPS[2][1]

    def body2(r, carry):
        acc, lse = attend(2, p2_ref[0, r, :, qc], h2_ref[0, r, :, kc_], p2_ref[0, r, :, kc_],
                          h2_ref[0, r, :, vc_], p2_ref[0, r, :, vc_], first_sb)
        merge(pl.ds(r, tq, stride=d2), acc, lse)
        return carry

    lax.fori_loop(0, d2, body2, 0)

    for c in range(n_half):
        o_ref[0, :, c * LANES:(c + 1) * LANES] = acc_ref[c].astype(o_ref.dtype)


def _rel_bucket(dist):
    max_exact = REL_BUCKETS // 2
    n = jnp.maximum(dist.astype(F32), 1.0)
    large = max_exact + (jnp.log(n / max_exact) / math.log(REL_MAX_DIST / max_exact)
                         * (REL_BUCKETS - max_exact)).astype(jnp.int32)
    large = jnp.minimum(large, REL_BUCKETS - 1)
    return jnp.where(dist < max_exact, dist, large)


def _toeplitz_bias(rel_bias, g, dil):
    tq = WIN_J
    offs = dil * jnp.arange(WIN_J + 1, dtype=jnp.int32)
    hs = slice(g * HEADS_PER_GROUP, (g + 1) * HEADS_PER_GROUP)
    tab = rel_bias[_rel_bucket(offs)][:, hs].T.astype(F32)
    period = 3 * tq
    neg = lambda w: jnp.full((HEADS_PER_GROUP, w), NEG, F32)
    vec = jnp.concatenate([neg(tq - 1), tab[:, ::-1], neg(period - 2 * tq)], axis=1)
    flat = jnp.broadcast_to(vec[:, None, :], (HEADS_PER_GROUP, tq, period)).reshape(HEADS_PER_GROUP, -1)
    skew = flat[:, :tq * (period - 1)].reshape(HEADS_PER_GROUP, tq, period - 1)
    return skew[:, :, tq - 1:3 * tq - 1]


def _dilated(pa0, pa1, pa2, bias, batch, seq):
    tq = WIN_J
    qkv3 = 3 * GROUP_W
    (_, d1), (_, d2) = DIL_GROUPS[1], DIL_GROUPS[2]
    nsb = seq // SUPER
    p0 = pa0.reshape(batch, seq, qkv3)
    prev_blk = lambda per_sb: (lambda b, s: jnp.maximum(s * per_sb - 1, 0))
    h0i, h1i, h2i = prev_blk(SUPER // tq), prev_blk(SUPER // d1 // tq), prev_blk(SUPER // d2 // tq)
    out = pl.pallas_call(
        _dilated_kernel,
        grid=(batch, nsb),
        in_specs=[
            pl.BlockSpec((1, SUPER, qkv3), lambda b, s: (b, s, 0)),
            pl.BlockSpec((1, tq, qkv3), lambda b, s: (b, h0i(b, s), 0)),
            pl.BlockSpec((1, d1, SUPER // d1, qkv3), lambda b, s: (b, 0, s, 0)),
            pl.BlockSpec((1, d1, tq, qkv3), lambda b, s: (b, 0, h1i(b, s), 0)),
            pl.BlockSpec((1, d2, SUPER // d2, qkv3), lambda b, s: (b, 0, s, 0)),
            pl.BlockSpec((1, d2, tq, qkv3), lambda b, s: (b, 0, h2i(b, s), 0)),
            pl.BlockSpec(bias.shape, lambda b, s: (0, 0, 0, 0)),
        ],
        out_specs=pl.BlockSpec((1, SUPER, GROUP_W), lambda b, s: (b, s, 0)),
        out_shape=jax.ShapeDtypeStruct((batch, seq, GROUP_W), BF16),
        scratch_shapes=[pltpu.VMEM((GROUP_W // LANES, SUPER, LANES), F32)] * 2,
        compiler_params=pltpu.CompilerParams(
            dimension_semantics=("arbitrary", "arbitrary"), vmem_limit_bytes=VMEM_LIMIT),
        name="dilated",
    )(p0, p0, pa1, pa1, pa2, pa2, bias)
    return out.reshape(batch * seq, GROUP_W)


def _fox_kernel(q_ref, k_ref, v_ref, o_ref):
    tq = q_ref.shape[1]
    tk = tq
    qi = pl.program_id(2)
    qs = [q_ref[0, :, 0:LANES], q_ref[0, :, LANES:2 * LANES]]
    lane_lo = lax.broadcasted_iota(jnp.int32, (tq, LANES), 1) < HEAD_DIM
    row = lax.broadcasted_iota(jnp.int32, (tq, tk), 0)
    col = lax.broadcasted_iota(jnp.int32, (tq, tk), 1)
    causal = row >= col

    def chunk(ki, carry, diagonal):
        ms, ls, acc = carry
        start = pl.multiple_of(ki * tk, tk)
        v2 = v_ref[0, pl.ds(start, tk), :]
        new_m, new_l, alphas, pvs = [], [], [], []
        for hh in range(2):
            kh = k_ref[0, pl.ds(start, tk), hh * LANES:(hh + 1) * LANES]
            s = _dot_nt(qs[hh], kh)
            if diagonal:
                s = jnp.where(causal, s, NEG)
            m_new = jnp.maximum(ms[hh], jnp.max(s, axis=-1, keepdims=True))
            alpha = jnp.exp(ms[hh] - m_new)
            p = jnp.exp(s - m_new)
            new_l.append(alpha * ls[hh] + jnp.sum(p, axis=-1, keepdims=True))
            new_m.append(m_new)
            alphas.append(alpha)
            pvs.append(_dot(p.astype(BF16), v2))
        acc = acc * jnp.where(lane_lo, alphas[0], alphas[1]) + jnp.where(lane_lo, pvs[0], pvs[1])
        return (tuple(new_m), tuple(new_l), acc)

    m0 = jnp.full((tq, 1), NEG, F32)
    l0 = jnp.zeros((tq, 1), F32)
    init = ((m0, m0), (l0, l0), jnp.zeros((tq, LANES), F32))
    carry = lax.fori_loop(0, qi, lambda ki, c: chunk(ki, c, False), init)
    ms, ls, acc = chunk(qi, carry, True)
    o_ref[0] = (acc / jnp.where(lane_lo, ls[0], ls[1])).astype(o_ref.dtype)


def _fox(qkb, vb, batch, seq):
    tq = TQ_FOX
    pairs = N_HEADS_B // 2
    qkv = qkb.reshape(batch, seq, 2 * N_HEADS_B * LANES)
    vv = vb.reshape(batch, seq, OUT_B)
    out = pl.pallas_call(
        _fox_kernel,
        grid=(batch, pairs, seq // tq),
        in_specs=[
            pl.BlockSpec((1, tq, 2 * LANES), lambda b, p, i: (b, i, p)),
            pl.BlockSpec((1, seq, 2 * LANES), lambda b, p, i: (b, 0, pairs + p)),
            pl.BlockSpec((1, seq, LANES), lambda b, p, i: (b, 0, p)),
        ],
        out_specs=pl.BlockSpec((1, tq, LANES), lambda b, p, i: (b, i, p)),
        out_shape=jax.ShapeDtypeStruct((batch, seq, OUT_B), BF16),
        compiler_params=pltpu.CompilerParams(
            dimension_semantics=("arbitrary", "arbitrary", "arbitrary"), vmem_limit_bytes=VMEM_LIMIT),
        name="fox",
    )(qkv, qkv, vv)
    return out.reshape(batch * seq, OUT_B)


def _post_kernel(x_ref, ya_ref, yb_ref, gmix_ref, wg_ref, bg_ref, wpa_ref, wpb_ref, wo_ref,
                 gffn_ref, wr_ref, rb_ref, x1_ref, h2_ref, topi_ref, topw_ref):
    d = x_ref.shape[1]
    x = x_ref[...]
    h = (x * lax.rsqrt(jnp.mean(x * x, axis=-1, keepdims=True) + EPS) * gmix_ref[...]).astype(BF16)
    gates = jax.nn.sigmoid(_dot(h, wg_ref[...]) + bg_ref[...])
    merged = gates[:, :d] * _dot(ya_ref[...], wpa_ref[...]) + gates[:, d:] * _dot(yb_ref[...], wpb_ref[...])
    x1 = x + _dot(merged.astype(BF16), wo_ref[...])
    x1_ref[...] = x1
    h2 = x1 * lax.rsqrt(jnp.mean(x1 * x1, axis=-1, keepdims=True) + EPS) * gffn_ref[...]
    h2_ref[...] = h2.astype(BF16)

    hh, hm, _ = _split3(h2)
    wr = wr_ref[...]
    wh = wr.astype(BF16)
    wl = (wr - wh.astype(F32)).astype(BF16)
    hh, hm = hh.astype(BF16), hm.astype(BF16)
    logits = _dot(hh, wh) + _dot(hm, wh) + _dot(hh, wl)
    scores = jax.nn.sigmoid(logits)
    biased = scores + rb_ref[...]
    lane = lax.broadcasted_iota(jnp.int32, scores.shape, 1)
    chosen = jnp.zeros(scores.shape, jnp.bool_)
    idx, val = [], []
    for _ in range(TOP_K):
        cur = jnp.where(chosen, -jnp.inf, biased)
        mx = jnp.max(cur, axis=-1, keepdims=True)
        first = jnp.min(jnp.where(cur == mx, lane, N_EXPERTS), axis=-1, keepdims=True)
        pick = lane == first
        chosen = jnp.logical_or(chosen, pick)
        idx.append(first)
        val.append(jnp.sum(jnp.where(pick, scores, 0.0), axis=-1, keepdims=True))
    top_s = jnp.concatenate(val, axis=1)
    topi_ref[...] = jnp.concatenate(idx, axis=1)
    topw_ref[...] = top_s / jnp.sum(top_s, axis=-1, keepdims=True) * ROUTE_SCALE


def _post(xf, ya, yb, g_mix, w_gate, b_gate, w_proj_a, w_proj_b, w_out, g_ffn, w_router, router_bias):
    n, d = xf.shape
    tm = TM_POST
    const = lambda shape: pl.BlockSpec(shape, lambda i: (0,) * len(shape))
    row = lambda w: pl.BlockSpec((tm, w), lambda i: (i, 0))
    args = [xf, ya, yb, g_mix.reshape(1, d), w_gate.astype(BF16), b_gate.reshape(1, 2 * d),
            w_proj_a.astype(BF16), w_proj_b.astype(BF16), w_out.astype(BF16), g_ffn.reshape(1, d),
            w_router.astype(F32), router_bias.astype(F32).reshape(1, N_EXPERTS)]
    in_specs = [row(d), row(OUT_A), row(OUT_B)] + [const(a.shape) for a in args[3:]]
    return pl.pallas_call(
        _post_kernel,
        grid=(n // tm,),
        in_specs=in_specs,
        out_specs=[row(d), row(d), row(TOP_K), row(TOP_K)],
        out_shape=[jax.ShapeDtypeStruct((n, d), F32), jax.ShapeDtypeStruct((n, d), BF16),
                   jax.ShapeDtypeStruct((n, TOP_K), jnp.int32), jax.ShapeDtypeStruct((n, TOP_K), F32)],
        compiler_params=pltpu.CompilerParams(
            dimension_semantics=("arbitrary",), vmem_limit_bytes=VMEM_LIMIT),
        name="post",
    )(*args)


def _dispatch_kernel(h2_ref, topi_ref, tri_ref, upper_ref, xs_ref, slots_ref, cnt_ref, off_ref):
    tb = h2_ref.shape[0]
    topi = topi_ref[...]
    lane = lax.broadcasted_iota(jnp.int32, (tb, N_EXPERTS), 1)
    picks = [lane == topi[:, k:k + 1] for k in range(TOP_K)]
    mask = picks[0]
    for pk in picks[1:]:
        mask = jnp.logical_or(mask, pk)
    maskf = jnp.where(mask, 1.0, 0.0)
    rank = _dot(tri_ref[...], maskf.astype(BF16))
    cnt = jnp.sum(maskf, axis=0, keepdims=True)
    gran = jnp.floor((cnt + (GRAN - 1)) * (1.0 / GRAN))
    goff = _dot(jnp.broadcast_to(gran, (8, N_EXPERTS)).astype(BF16), upper_ref[...])[0:1]
    off = goff * GRAN
    slot_te = off + rank
    slots = jnp.concatenate(
        [jnp.sum(jnp.where(pk, slot_te, 0.0), axis=-1, keepdims=True) for pk in picks], axis=1)
    slots_ref[...] = slots.astype(jnp.int32)
    cnt_ref[0] = cnt.astype(jnp.int32)
    off_ref[0] = off.astype(jnp.int32)
    slots_t = jnp.concatenate([slots, jnp.zeros((tb, LANES - TOP_K), F32)], axis=1).T
    h2 = h2_ref[...]
    for c in range(CAP // SLOT_CHUNK):
        srow = (lax.broadcasted_iota(jnp.int32, (SLOT_CHUNK, tb), 0) + c * SLOT_CHUNK).astype(F32)
        onehot = jnp.zeros((SLOT_CHUNK, tb), F32)
        for k in range(TOP_K):
            onehot = onehot + jnp.where(srow == slots_t[k:k + 1, :], 1.0, 0.0)
        xs_ref[0, c * SLOT_CHUNK:(c + 1) * SLOT_CHUNK, :] = _dot(onehot.astype(BF16), h2).astype(BF16)


def _dispatch(h2, topi):
    n, d = h2.shape
    nb = n // TB
    tri = jnp.asarray(np.tril(np.ones((TB, TB), np.float32), -1), BF16)
    upper = jnp.asarray(np.triu(np.ones((N_EXPERTS, N_EXPERTS), np.float32), 1), BF16)
    const = lambda shape: pl.BlockSpec(shape, lambda i: (0,) * len(shape))
    meta = pl.BlockSpec((1, 1, N_EXPERTS), lambda i: (i, 0, 0))
    return pl.pallas_call(
        _dispatch_kernel,
        grid=(nb,),
        in_specs=[pl.BlockSpec((TB, d), lambda i: (i, 0)), pl.BlockSpec((TB, TOP_K), lambda i: (i, 0)),
                  const(tri.shape), const(upper.shape)],
        out_specs=[pl.BlockSpec((1, CAP, d), lambda i: (i, 0, 0)),
                   pl.BlockSpec((TB, TOP_K), lambda i: (i, 0)), meta, meta],
        out_shape=[jax.ShapeDtypeStruct((nb, CAP, d), BF16), jax.ShapeDtypeStruct((n, TOP_K), jnp.int32),
                   jax.ShapeDtypeStruct((nb, 1, N_EXPERTS), jnp.int32),
                   jax.ShapeDtypeStruct((nb, 1, N_EXPERTS), jnp.int32)],
        compiler_params=pltpu.CompilerParams(
            dimension_semantics=("arbitrary",), vmem_limit_bytes=VMEM_LIMIT),
        name="dispatch",
    )(h2, topi, tri, upper)


def _ffn_kernel(cnt_sm, off_sm, xs_hbm, wgu_ref, wd_ref, ys_hbm, xbuf, ybuf, sem_in, sem_out):
    ng = pl.num_programs(1)
    step = pl.program_id(0) * ng + pl.program_id(1)
    nsteps = pl.num_programs(0) * ng
    buf = step % 2

    def for_granules(st, fn):
        e, g = st // ng, st % ng

        def per_block(i, done):
            b = g * BPG + i
            c = cnt_sm[b * N_EXPERTS + e]
            row0 = b * CAP + off_sm[b * N_EXPERTS + e]
            ngr = (c + (GRAN - 1)) // GRAN

            def per_granule(j, carry):
                fn(pl.multiple_of(row0 + j * GRAN, GRAN), pl.multiple_of((done + j) * GRAN, GRAN))
                return carry

            lax.fori_loop(0, ngr, per_granule, 0)
            return done + ngr

        return lax.fori_loop(0, BPG, per_block, 0)

    def fetch(b_):
        return lambda src, dst: pltpu.make_async_copy(
            xs_hbm.at[pl.ds(src, GRAN)], xbuf.at[b_, pl.ds(dst, GRAN)], sem_in.at[b_])

    def writeback(b_):
        return lambda src, dst: pltpu.make_async_copy(
            ybuf.at[b_, pl.ds(dst, GRAN)], ys_hbm.at[pl.ds(src, GRAN)], sem_out.at[b_])

    def start(mk):
        return lambda src, dst: mk(src, dst).start()

    def wait(mk):
        return lambda src, dst: mk(src, dst).wait()

    @pl.when(step == 0)
    def _():
        xbuf[...] = jnp.zeros_like(xbuf)
        for_granules(step, start(fetch(0)))

    @pl.when(step + 1 < nsteps)
    def _():
        for_granules(step + 1, start(fetch(1 - buf)))

    ngran = for_granules(step, wait(fetch(buf)))

    @pl.when(step >= 2)
    def _():
        for_granules(step - 2, wait(writeback(buf)))

    wgu = wgu_ref[0]
    wd = wd_ref[0]

    def ffn_rows(base, rows):
        x = xbuf[buf, pl.ds(base, rows), :]
        gu = _dot(x, wgu)
        g, u = gu[:, :D_EXPERT], gu[:, D_EXPERT:]
        mid = (g * jax.nn.sigmoid(g) * u).astype(BF16)
        ybuf[buf, pl.ds(base, rows), :] = _dot(mid, wd).astype(BF16)

    nt = (ngran * GRAN + (FT - 1)) // FT
    big = FT_BIG // FT

    def big_tile(i, carry):
        ffn_rows(pl.multiple_of(i * FT_BIG, FT_BIG), FT_BIG)
        return carry

    lax.fori_loop(0, nt // big, big_tile, 0)
    size = big // 2
    while size >= 1:
        @pl.when((nt & size) != 0)
        def _(size=size):
            ffn_rows(pl.multiple_of((nt & ~(2 * size - 1)) * FT, size * FT), size * FT)
        size //= 2

    for_granules(step, start(writeback(buf)))

    @pl.when(step == nsteps - 1)
    def _():
        for_granules(step, wait(writeback(buf)))

        @pl.when(step >= 1)
        def _():
            for_granules(step - 1, wait(writeback(1 - buf)))


def _ffn(xs, cnt, off, wgu, wd):
    rows, d = xs.shape
    ngroups = (rows // CAP) // BPG
    grid_spec = pltpu.PrefetchScalarGridSpec(
        num_scalar_prefetch=2,
        grid=(N_EXPERTS, ngroups),
        in_specs=[pl.BlockSpec(memory_space=pl.ANY),
                  pl.BlockSpec((1, d, 2 * D_EXPERT), lambda e, g, c, o: (e, 0, 0)),
                  pl.BlockSpec((1, D_EXPERT, d), lambda e, g, c, o: (e, 0, 0))],
        out_specs=pl.BlockSpec(memory_space=pl.ANY),
        scratch_shapes=[pltpu.VMEM((2, BPG * TB, d), BF16), pltpu.VMEM((2, BPG * TB, d), BF16),
                        pltpu.SemaphoreType.DMA((2,)), pltpu.SemaphoreType.DMA((2,))],
    )
    return pl.pallas_call(
        _ffn_kernel,
        grid_spec=grid_spec,
        out_shape=jax.ShapeDtypeStruct(xs.shape, xs.dtype),
        input_output_aliases={2: 0},
        compiler_params=pltpu.CompilerParams(
            dimension_semantics=("arbitrary", "arbitrary"), vmem_limit_bytes=VMEM_LIMIT),
        name="ffn",
    )(cnt, off, xs, wgu, wd)


def _combine_kernel(x1_ref, h2_ref, ys_ref, slots_ref, topw_ref, wgus_ref, wds_ref, o_ref):
    tb = x1_ref.shape[0]
    gu = _dot(h2_ref[...], wgus_ref[...])
    g, u = gu[:, :D_SHARED], gu[:, D_SHARED:]
    acc = x1_ref[...] + _dot((g * jax.nn.sigmoid(g) * u).astype(BF16), wds_ref[...])
    slots = slots_ref[...].astype(F32)
    topw = topw_ref[...]
    for c in range(CAP // SLOT_CHUNK):
        scol = (lax.broadcasted_iota(jnp.int32, (tb, SLOT_CHUNK), 1) + c * SLOT_CHUNK).astype(F32)
        gate = jnp.zeros((tb, SLOT_CHUNK), F32)
        for k in range(TOP_K):
            gate = gate + jnp.where(scol == slots[:, k:k + 1], topw[:, k:k + 1], 0.0)
        acc = acc + _dot(gate.astype(BF16), ys_ref[0, c * SLOT_CHUNK:(c + 1) * SLOT_CHUNK, :])
    o_ref[...] = acc


def _combine(x1, h2, ys, slots, topw, wgus, wds):
    n, d = x1.shape
    const = lambda shape: pl.BlockSpec(shape, lambda i: (0,) * len(shape))
    row = lambda w: pl.BlockSpec((TB, w), lambda i: (i, 0))
    return pl.pallas_call(
        _combine_kernel,
        grid=(n // TB,),
        in_specs=[row(d), row(d), pl.BlockSpec((1, CAP, d), lambda i: (i, 0, 0)), row(TOP_K), row(TOP_K),
                  const(wgus.shape), const(wds.shape)],
        out_specs=row(d),
        out_shape=jax.ShapeDtypeStruct((n, d), F32),
        compiler_params=pltpu.CompilerParams(
            dimension_semantics=("arbitrary",), vmem_limit_bytes=VMEM_LIMIT),
        name="combine",
    )(x1, h2, ys, slots, topw, wgus, wds)


def _moe(x1, h2, topi, topw, w_gate_e, w_up_e, w_down_e, w_gate_s, w_up_s, w_down_s):
    n, d = x1.shape
    wgu = jnp.concatenate([w_gate_e.astype(BF16), w_up_e.astype(BF16)], axis=-1)
    wgus = jnp.concatenate([w_gate_s.astype(BF16), w_up_s.astype(BF16)], axis=-1)
    xs, slots, cnt, off = _dispatch(h2, topi)
    ys = _ffn(xs.reshape(-1, d), cnt.reshape(-1), off.reshape(-1), wgu, w_down_e.astype(BF16))
    return _combine(x1, h2, ys.reshape(n // TB, CAP, d), slots, topw, wgus, w_down_s.astype(BF16))


def kernel(x, g_mix, w_in, q_norm_a, k_norm_a, q_norm_b, k_norm_b, rel_bias, b_forget, w_gate, b_gate,
           w_proj_a, w_proj_b, w_out, g_ffn, w_router, router_bias, w_gate_e, w_up_e, w_down_e,
           w_gate_s, w_up_s, w_down_s):
    batch, seq, d = x.shape
    xf = x.reshape(batch * seq, d)
    pa0, pa1, pa2, qkb, vb = _inproj(xf, g_mix, w_in, q_norm_a, k_norm_a, q_norm_b, k_norm_b, b_forget, seq)
    bias = jnp.stack([_toeplitz_bias(rel_bias, g, dil) for g, (_, dil) in enumerate(DIL_GROUPS)])
    ya = _dilated(pa0, pa1, pa2, bias, batch, seq)

    yb = _fox(qkb, vb, batch, seq)
    x1, h2, topi, topw = _post(xf, ya, yb, g_mix, w_gate, b_gate, w_proj_a, w_proj_b, w_out, g_ffn,
                               w_router, router_bias)
    out = _moe(x1, h2, topi, topw, w_gate_e, w_up_e, w_down_e, w_gate_s, w_up_s, w_down_s)
    return out.reshape(batch, seq, d)
```

```python
import functools
import math

import jax
import jax.numpy as jnp
import numpy as np
from jax import lax
from jax.experimental import pallas as pl
from jax.experimental.pallas import tpu as pltpu

D_MODEL = 1024
HEAD_DIM = 64
DIL_GROUPS = ((128, 1), (512, 4), (2048, 16))
HEADS_PER_GROUP = 4
N_HEADS_A = HEADS_PER_GROUP * len(DIL_GROUPS)
N_HEADS_B = 8
REL_BUCKETS = 32
REL_MAX_DIST = 2048
N_EXPERTS = 64
TOP_K = 8
D_EXPERT = 256
D_SHARED = 256
ROUTE_SCALE = 2.5
EPS = 1e-6

WIDTH_A = 3 * N_HEADS_A * HEAD_DIM
WIDTH_B = 3 * N_HEADS_B * HEAD_DIM
QK_B = N_HEADS_B * HEAD_DIM
OUT_A = HEADS_PER_GROUP * HEAD_DIM
OUT_B = N_HEADS_B * HEAD_DIM

LANES = 128
GROUP_W = HEADS_PER_GROUP * HEAD_DIM
WIN_J = 128
SUPER = DIL_GROUPS[-1][1] * WIN_J
NEG = -1e30
VMEM_LIMIT = 56 * 1024 * 1024

TM_IN = 512
TM_POST = 512
TQ_FOX = 1024
TB = 256
GRAN = 16
CAP = TB * TOP_K + N_EXPERTS * GRAN
SLOT_CHUNK = 512
NO_SLOT = 4095
BPG = 16
FT = 256
FT_BIG = 1024

BF16 = jnp.bfloat16
F32 = jnp.float32


def _dot(a, b):
    return jnp.dot(a, b, preferred_element_type=F32)


def _dot_nt(a, b):
    return lax.dot_general(a, b, (((1,), (1,)), ((), ())), preferred_element_type=F32)


def _split3(v):
    hi = v.astype(BF16).astype(F32)
    r = v - hi
    mid = r.astype(BF16).astype(F32)
    lo = (r - mid).astype(BF16).astype(F32)
    return hi, mid, lo


def _inproj_kernel(x_ref, g_ref, wa_ref, wb_ref, wf_ref, bd_ref, tri_ref, gain_a_ref, gain_b_ref,
                   bf_ref, pa0_ref, pa1_ref, pa2_ref, qkb_ref, vb_ref, carry_ref, h_ref, *, tiles_per_seq):
    tm = x_ref.shape[0]
    x = x_ref[...]
    h = x * lax.rsqrt(jnp.mean(x * x, axis=-1, keepdims=True) + EPS) * g_ref[...]
    n_lane_chunks = h_ref.shape[0]
    for c in range(n_lane_chunks):
        h_ref[c] = h[:, c * LANES:(c + 1) * LANES]
    h = h.astype(BF16)
    bd = bd_ref[...]

    def headnorm(p, gain):
        ms = _dot((p * p).astype(BF16), bd)
        return p * lax.rsqrt(ms + EPS) * gain

    for g, (pa_ref, (_, dil)) in enumerate(zip((pa0_ref, pa1_ref, pa2_ref), DIL_GROUPS)):
        rows = tm // dil
        if dil == 1:
            hg = h
        else:
            hg = jnp.concatenate([jnp.concatenate(
                [h_ref[c, pl.ds(r, rows, stride=dil), :] for c in range(n_lane_chunks)], axis=1)
                for r in range(dil)], axis=0).astype(BF16)
        for part in range(3):
            cols = slice(part * GROUP_W, (part + 1) * GROUP_W)
            p = _dot(hg, wa_ref[g, :, cols])
            if part < 2:
                p = headnorm(p, gain_a_ref[part:part + 1, :])
            p = p.astype(BF16)
            if dil == 1:
                pa_ref[:, cols] = p
            else:
                for r in range(dil):
                    pa_ref[0, r, :, cols] = p[r * rows:(r + 1) * rows, :]

    f = _dot(h, wf_ref[...]) + bf_ref[...]
    logf = jnp.minimum(f, 0.0) - jnp.log1p(jnp.exp(-jnp.abs(f)))
    tri = tri_ref[...]
    lh, lm, ll = _split3(logf)
    cum = _dot(tri, lh.astype(BF16)) + _dot(tri, lm.astype(BF16)) + _dot(tri, ll.astype(BF16))

    @pl.when(pl.program_id(0) % tiles_per_seq == 0)
    def _():
        carry_ref[...] = jnp.zeros_like(carry_ref)

    cum = cum + carry_ref[0:1, :]
    carry_ref[0:1, :] = cum[tm - 1:tm, :]
    ch, cm, cl = _split3(cum)

    j = lax.broadcasted_iota(jnp.int32, (tm, HEAD_DIM), 1)
    n_chunk_b = QK_B // GROUP_W
    for part in range(2):
        for c in range(n_chunk_b):
            cols = slice(part * QK_B + c * GROUP_W, part * QK_B + (c + 1) * GROUP_W)
            p = headnorm(_dot(h, wb_ref[:, cols]), gain_b_ref[part:part + 1, :])
            for hh in range(HEADS_PER_GROUP):
                head = c * HEADS_PER_GROUP + hh
                a, b_, c_ = (ch[:, head:head + 1], cm[:, head:head + 1], cl[:, head:head + 1])
                if part == 0:
                    ext = jnp.where(j == 0, a, jnp.where(j == 1, b_, jnp.where(
                        j == 2, c_, jnp.where(j < 6, 1.0, 0.0))))
                else:
                    ext = jnp.where(j < 3, 1.0, jnp.where(j == 3, -a, jnp.where(
                        j == 4, -b_, jnp.where(j == 5, -c_, 0.0))))
                piece = jnp.concatenate([p[:, hh * HEAD_DIM:(hh + 1) * HEAD_DIM], ext], axis=-1)
                o0 = (part * N_HEADS_B + head) * LANES
                qkb_ref[:, o0:o0 + LANES] = piece.astype(BF16)
    for c in range(QK_B // GROUP_W):
        cols = slice(2 * QK_B + c * GROUP_W, 2 * QK_B + (c + 1) * GROUP_W)
        vb_ref[:, c * GROUP_W:(c + 1) * GROUP_W] = _dot(h, wb_ref[:, cols]).astype(BF16)


def _inproj(xf, g_mix, w_in, q_norm_a, k_norm_a, q_norm_b, k_norm_b, b_forget, seq):
    n, d = xf.shape
    tm = TM_IN
    scale = HEAD_DIM ** -0.5
    w_bf = w_in.astype(BF16)
    qkv_w = N_HEADS_A * HEAD_DIM
    wa = jnp.stack([jnp.concatenate(
        [w_bf[:, part * qkv_w + g * GROUP_W: part * qkv_w + (g + 1) * GROUP_W] for part in range(3)],
        axis=1) for g in range(len(DIL_GROUPS))])
    wb = w_bf[:, WIDTH_A:WIDTH_A + WIDTH_B]
    wf = jnp.pad(w_bf[:, WIDTH_A + WIDTH_B:], ((0, 0), (0, LANES - N_HEADS_B)))
    bfp = jnp.pad(b_forget.astype(F32), (0, LANES - N_HEADS_B)).reshape(1, LANES)
    seg = np.arange(GROUP_W) // HEAD_DIM
    bd = jnp.asarray((seg[:, None] == seg[None, :]).astype(np.float32) / HEAD_DIM, BF16)
    tri = jnp.asarray(np.tril(np.ones((tm, tm), np.float32)), BF16)
    gain_a = jnp.stack([jnp.tile(q_norm_a, HEADS_PER_GROUP) * scale, jnp.tile(k_norm_a, HEADS_PER_GROUP)])
    gain_b = jnp.stack([jnp.tile(q_norm_b, HEADS_PER_GROUP) * scale, jnp.tile(k_norm_b, HEADS_PER_GROUP)])
    const = lambda shape: pl.BlockSpec(shape, lambda i: (0,) * len(shape))
    tps = seq // tm
    batch = n // seq
    qkv3 = 3 * GROUP_W
    (_, d1), (_, d2) = DIL_GROUPS[1], DIL_GROUPS[2]
    return pl.pallas_call(
        functools.partial(_inproj_kernel, tiles_per_seq=tps),
        grid=(n // tm,),
        in_specs=[
            pl.BlockSpec((tm, d), lambda i: (i, 0)),
            const((1, d)), const(wa.shape), const(wb.shape), const(wf.shape),
            const(bd.shape), const(tri.shape), const(gain_a.shape), const(gain_b.shape),
            const(bfp.shape),
        ],
        out_specs=[
            pl.BlockSpec((tm, qkv3), lambda i: (i, 0)),
            pl.BlockSpec((1, d1, tm // d1, qkv3), lambda i: (i // tps, 0, i % tps, 0)),
            pl.BlockSpec((1, d2, tm // d2, qkv3), lambda i: (i // tps, 0, i % tps, 0)),
            pl.BlockSpec((tm, 2 * N_HEADS_B * LANES), lambda i: (i, 0)),
            pl.BlockSpec((tm, OUT_B), lambda i: (i, 0)),
        ],
        out_shape=[
            jax.ShapeDtypeStruct((n, qkv3), BF16),
            jax.ShapeDtypeStruct((batch, d1, seq // d1, qkv3), BF16),
            jax.ShapeDtypeStruct((batch, d2, seq // d2, qkv3), BF16),
            jax.ShapeDtypeStruct((n, 2 * N_HEADS_B * LANES), BF16),
            jax.ShapeDtypeStruct((n, OUT_B), BF16),
        ],
        scratch_shapes=[pltpu.VMEM((8, LANES), F32), pltpu.VMEM((d // LANES, tm, LANES), F32)],
        compiler_params=pltpu.CompilerParams(
            dimension_semantics=("arbitrary",), vmem_limit_bytes=VMEM_LIMIT),
        name="inproj",
    )(xf, g_mix.reshape(1, d), wa, wb, wf, bd, tri, gain_a, gain_b, bfp)


def _dilated_kernel(p0_ref, h0_ref, p1_ref, h1_ref, p2_ref, h2_ref, bias_ref, o_ref, acc_ref, lse_ref):
    tq = WIN_J
    first_sb = pl.program_id(1) == 0
    lane_head = lax.broadcasted_iota(jnp.int32, (tq, GROUP_W), 1) // HEAD_DIM
    prev_col = lax.broadcasted_iota(jnp.int32, (tq, 2 * tq), 1) < tq
    qc, kc_, vc_ = (slice(0, GROUP_W), slice(GROUP_W, 2 * GROUP_W), slice(2 * GROUP_W, 3 * GROUP_W))

    def attend(g, q, kp, kc, vp, vc, no_prev):
        kcat = jnp.concatenate([kp, kc], axis=0)
        vcat = jnp.concatenate([vp, vc], axis=0)
        dead = jnp.logical_and(no_prev, prev_col)
        q4 = jnp.concatenate([jnp.where(lane_head == hh, q, jnp.zeros_like(q))
                              for hh in range(HEADS_PER_GROUP)], axis=0)
        s = _dot_nt(q4, kcat) + bias_ref[g].reshape(HEADS_PER_GROUP * tq, 2 * tq)
        s = jnp.where(jnp.concatenate([dead] * HEADS_PER_GROUP, axis=0), NEG, s)
        m = jnp.max(s, axis=-1, keepdims=True)
        p = jnp.exp(s - m)
        l = jnp.sum(p, axis=-1, keepdims=True)
        o4 = _dot(p.astype(BF16), vcat) * (1.0 / l)
        lse4 = m + jnp.log(l)
        acc = o4[0:tq]
        lse = jnp.broadcast_to(lse4[0:tq], (tq, GROUP_W))
        for hh in range(1, HEADS_PER_GROUP):
            sel = lane_head == hh
            acc = jnp.where(sel, o4[hh * tq:(hh + 1) * tq], acc)
            lse = jnp.where(sel, lse4[hh * tq:(hh + 1) * tq], lse)
        return acc, lse

    n_half = GROUP_W // LANES

    def merge(rows, acc, lse):
        for c in range(n_half):
            lanes = slice(c * LANES, (c + 1) * LANES)
            l1 = lse_ref[c, rows, :]
            mx = jnp.maximum(l1, lse[:, lanes])
            w1 = jnp.exp(l1 - mx)
            w2 = jnp.exp(lse[:, lanes] - mx)
            den = w1 + w2
            acc_ref[c, rows, :] = (w1 * acc_ref[c, rows, :] + w2 * acc[:, lanes]) / den
            lse_ref[c, rows, :] = mx + jnp.log(den)

    def pick(first, halo, body):
        return jnp.where(first, halo, body)

    def body0(j, carry):
        st = pl.multiple_of(j * tq, tq)
        pst = pl.multiple_of(jnp.maximum(j - 1, 0) * tq, tq)
        cur, prv = pl.ds(st, tq), pl.ds(pst, tq)
        acc, lse = attend(
            0, p0_ref[0, cur, qc],
            pick(j == 0, h0_ref[0, :, kc_], p0_ref[0, prv, kc_]), p0_ref[0, cur, kc_],
            pick(j == 0, h0_ref[0, :, vc_], p0_ref[0, prv, vc_]), p0_ref[0, cur, vc_],
            jnp.logical_and(j == 0, first_sb))
        for c in range(n_half):
            acc_ref[c, cur, :] = acc[:, c * LANES:(c + 1) * LANES]
            lse_ref[c, cur, :] = lse[:, c * LANES:(c + 1) * LANES]
        return carry

    lax.fori_loop(0, SUPER // tq, body0, 0)

    d1 = DIL_GROUPS[1][1]
    nsub1 = SUPER // d1 // tq
    def body1(t, carry):
        r, ii = t // nsub1, t % nsub1
        st = pl.multiple_of(ii * tq, tq)
        pst = pl.multiple_of(jnp.maximum(ii - 1, 0) * tq, tq)
        cur, prv = pl.ds(st, tq), pl.ds(pst, tq)
        acc, lse = attend(
            1, p1_ref[0, r, cur, qc],
            pick(ii == 0, h1_ref[0, r, :, kc_], p1_ref[0, r, prv, kc_]), p1_ref[0, r, cur, kc_],
            pick(ii == 0, h1_ref[0, r, :, vc_], p1_ref[0, r, prv, vc_]), p1_ref[0, r, cur, vc_],
            jnp.logical_and(ii == 0, first_sb))
        merge(pl.ds(ii * (tq * d1) + r, tq, stride=d1), acc, lse)
        return carry

    lax.fori_loop(0, d1 * nsub1, body1, 0)

    d2 = DIL_GROUPS[2][1]

    def body2(r, carry):
        acc, lse = attend(2, p2_ref[0, r, :, qc], h2_ref[0, r, :, kc_], p2_ref[0, r, :, kc_],
                          h2_ref[0, r, :, vc_], p2_ref[0, r, :, vc_], first_sb)
        merge(pl.ds(r, tq, stride=d2), acc, lse)
        return carry

    lax.fori_loop(0, d2, body2, 0)

    for c in range(n_half):
        o_ref[0, :, c * LANES:(c + 1) * LANES] = acc_ref[c].astype(o_ref.dtype)


def _rel_bucket(dist):
    max_exact = REL_BUCKETS // 2
    n = jnp.maximum(dist.astype(F32), 1.0)
    large = max_exact + (jnp.log(n / max_exact) / math.log(REL_MAX_DIST / max_exact)
                         * (REL_BUCKETS - max_exact)).astype(jnp.int32)
    large = jnp.minimum(large, REL_BUCKETS - 1)
    return jnp.where(dist < max_exact, dist, large)


def _toeplitz_bias(rel_bias, g, dil):
    tq = WIN_J
    offs = dil * jnp.arange(WIN_J + 1, dtype=jnp.int32)
    hs = slice(g * HEADS_PER_GROUP, (g + 1) * HEADS_PER_GROUP)
    tab = rel_bias[_rel_bucket(offs)][:, hs].T.astype(F32)
    period = 3 * tq
    neg = lambda w: jnp.full((HEADS_PER_GROUP, w), NEG, F32)
    vec = jnp.concatenate([neg(tq - 1), tab[:, ::-1], neg(period - 2 * tq)], axis=1)
    flat = jnp.broadcast_to(vec[:, None, :], (HEADS_PER_GROUP, tq, period)).reshape(HEADS_PER_GROUP, -1)
    skew = flat[:, :tq * (period - 1)].reshape(HEADS_PER_GROUP, tq, period - 1)
    return skew[:, :, tq - 1:3 * tq - 1]


def _dilated(pa0, pa1, pa2, bias, batch, seq):
    tq = WIN_J
    qkv3 = 3 * GROUP_W
    (_, d1), (_, d2) = DIL_GROUPS[1], DIL_GROUPS[2]
    nsb = seq // SUPER
    p0 = pa0.reshape(batch, seq, qkv3)
    prev_blk = lambda per_sb: (lambda b, s: jnp.maximum(s * per_sb - 1, 0))
    h0i, h1i, h2i = prev_blk(SUPER // tq), prev_blk(SUPER // d1 // tq), prev_blk(SUPER // d2 // tq)
    out = pl.pallas_call(
        _dilated_kernel,
        grid=(batch, nsb),
        in_specs=[
            pl.BlockSpec((1, SUPER, qkv3), lambda b, s: (b, s, 0)),
            pl.BlockSpec((1, tq, qkv3), lambda b, s: (b, h0i(b, s), 0)),
            pl.BlockSpec((1, d1, SUPER // d1, qkv3), lambda b, s: (b, 0, s, 0)),
            pl.BlockSpec((1, d1, tq, qkv3), lambda b, s: (b, 0, h1i(b, s), 0)),
            pl.BlockSpec((1, d2, SUPER // d2, qkv3), lambda b, s: (b, 0, s, 0)),
            pl.BlockSpec((1, d2, tq, qkv3), lambda b, s: (b, 0, h2i(b, s), 0)),
            pl.BlockSpec(bias.shape, lambda b, s: (0, 0, 0, 0)),
        ],
        out_specs=pl.BlockSpec((1, SUPER, GROUP_W), lambda b, s: (b, s, 0)),
        out_shape=jax.ShapeDtypeStruct((batch, seq, GROUP_W), BF16),
        scratch_shapes=[pltpu.VMEM((GROUP_W // LANES, SUPER, LANES), F32)] * 2,
        compiler_params=pltpu.CompilerParams(
            dimension_semantics=("arbitrary", "arbitrary"), vmem_limit_bytes=VMEM_LIMIT),
        name="dilated",
    )(p0, p0, pa1, pa1, pa2, pa2, bias)
    return out.reshape(batch * seq, GROUP_W)


def _fox_kernel(q_ref, k_ref, v_ref, o_ref):
    tq = q_ref.shape[1]
    tk = tq
    qi = pl.program_id(2)
    qs = [q_ref[0, :, 0:LANES], q_ref[0, :, LANES:2 * LANES]]
    lane_lo = lax.broadcasted_iota(jnp.int32, (tq, LANES), 1) < HEAD_DIM
    row = lax.broadcasted_iota(jnp.int32, (tq, tk), 0)
    col = lax.broadcasted_iota(jnp.int32, (tq, tk), 1)
    causal = row >= col

    def chunk(ki, carry, diagonal):
        ms, ls, acc = carry
        start = pl.multiple_of(ki * tk, tk)
        v2 = v_ref[0, pl.ds(start, tk), :]
        new_m, new_l, alphas, pvs = [], [], [], []
        for hh in range(2):
            kh = k_ref[0, pl.ds(start, tk), hh * LANES:(hh + 1) * LANES]
            s = _dot_nt(qs[hh], kh)
            if diagonal:
                s = jnp.where(causal, s, NEG)
            m_new = jnp.maximum(ms[hh], jnp.max(s, axis=-1, keepdims=True))
            alpha = jnp.exp(ms[hh] - m_new)
            p = jnp.exp(s - m_new)
            new_l.append(alpha * ls[hh] + jnp.sum(p, axis=-1, keepdims=True))
            new_m.append(m_new)
            alphas.append(alpha)
            pvs.append(_dot(p.astype(BF16), v2))
        acc = acc * jnp.where(lane_lo, alphas[0], alphas[1]) + jnp.where(lane_lo, pvs[0], pvs[1])
        return (tuple(new_m), tuple(new_l), acc)

    m0 = jnp.full((tq, 1), NEG, F32)
    l0 = jnp.zeros((tq, 1), F32)
    init = ((m0, m0), (l0, l0), jnp.zeros((tq, LANES), F32))
    carry = lax.fori_loop(0, qi, lambda ki, c: chunk(ki, c, False), init)
    ms, ls, acc = chunk(qi, carry, True)
    o_ref[0] = (acc / jnp.where(lane_lo, ls[0], ls[1])).astype(o_ref.dtype)


def _fox(qkb, vb, batch, seq):
    tq = TQ_FOX
    pairs = N_HEADS_B // 2
    qkv = qkb.reshape(batch, seq, 2 * N_HEADS_B * LANES)
    vv = vb.reshape(batch, seq, OUT_B)
    out = pl.pallas_call(
        _fox_kernel,
        grid=(batch, pairs, seq // tq),
        in_specs=[
            pl.BlockSpec((1, tq, 2 * LANES), lambda b, p, i: (b, i, p)),
            pl.BlockSpec((1, seq, 2 * LANES), lambda b, p, i: (b, 0, pairs + p)),
            pl.BlockSpec((1, seq, LANES), lambda b, p, i: (b, 0, p)),
        ],
        out_specs=pl.BlockSpec((1, tq, LANES), lambda b, p, i: (b, i, p)),
        out_shape=jax.ShapeDtypeStruct((batch, seq, OUT_B), BF16),
        compiler_params=pltpu.CompilerParams(
            dimension_semantics=("arbitrary", "arbitrary", "arbitrary"), vmem_limit_bytes=VMEM_LIMIT),
        name="fox",
    )(qkv, qkv, vv)
    return out.reshape(batch * seq, OUT_B)


def _post_kernel(x_ref, ya_ref, yb_ref, gmix_ref, wg_ref, bg_ref, wpa_ref, wpb_ref, wo_ref,
                 gffn_ref, wr_ref, rb_ref, x1_ref, h2_ref, topi_ref, topw_ref):
    d = x_ref.shape[1]
    x = x_ref[...]
    h = (x * lax.rsqrt(jnp.mean(x * x, axis=-1, keepdims=True) + EPS) * gmix_ref[...]).astype(BF16)
    gates = jax.nn.sigmoid(_dot(h, wg_ref[...]) + bg_ref[...])
    merged = gates[:, :d] * _dot(ya_ref[...], wpa_ref[...]) + gates[:, d:] * _dot(yb_ref[...], wpb_ref[...])
    x1 = x + _dot(merged.astype(BF16), wo_ref[...])
    x1_ref[...] = x1
    h2 = x1 * lax.rsqrt(jnp.mean(x1 * x1, axis=-1, keepdims=True) + EPS) * gffn_ref[...]
    h2_ref[...] = h2.astype(BF16)

    hh, hm, _ = _split3(h2)
    wr = wr_ref[...]
    wh = wr.astype(BF16)
    wl = (wr - wh.astype(F32)).astype(BF16)
    hh, hm = hh.astype(BF16), hm.astype(BF16)
    logits = _dot(hh, wh) + _dot(hm, wh) + _dot(hh, wl)
    scores = jax.nn.sigmoid(logits)
    biased = scores + rb_ref[...]
    lane = lax.broadcasted_iota(jnp.int32, scores.shape, 1)
    chosen = jnp.zeros(scores.shape, jnp.bool_)
    idx, val = [], []
    for _ in range(TOP_K):
        cur = jnp.where(chosen, -jnp.inf, biased)
        mx = jnp.max(cur, axis=-1, keepdims=True)
        first = jnp.min(jnp.where(cur == mx, lane, N_EXPERTS), axis=-1, keepdims=True)
        pick = lane == first
        chosen = jnp.logical_or(chosen, pick)
        idx.append(first)
        val.append(jnp.sum(jnp.where(pick, scores, 0.0), axis=-1, keepdims=True))
    top_s = jnp.concatenate(val, axis=1)
    topi_ref[...] = jnp.concatenate(idx, axis=1)
    topw_ref[...] = top_s / jnp.sum(top_s, axis=-1, keepdims=True) * ROUTE_SCALE


def _post(xf, ya, yb, g_mix, w_gate, b_gate, w_proj_a, w_proj_b, w_out, g_ffn, w_router, router_bias):
    n, d = xf.shape
    tm = TM_POST
    const = lambda shape: pl.BlockSpec(shape, lambda i: (0,) * len(shape))
    row = lambda w: pl.BlockSpec((tm, w), lambda i: (i, 0))
    args = [xf, ya, yb, g_mix.reshape(1, d), w_gate.astype(BF16), b_gate.reshape(1, 2 * d),
            w_proj_a.astype(BF16), w_proj_b.astype(BF16), w_out.astype(BF16), g_ffn.reshape(1, d),
            w_router.astype(F32), router_bias.astype(F32).reshape(1, N_EXPERTS)]
    in_specs = [row(d), row(OUT_A), row(OUT_B)] + [const(a.shape) for a in args[3:]]
    return pl.pallas_call(
        _post_kernel,
        grid=(n // tm,),
        in_specs=in_specs,
        out_specs=[row(d), row(d), row(TOP_K), row(TOP_K)],
        out_shape=[jax.ShapeDtypeStruct((n, d), F32), jax.ShapeDtypeStruct((n, d), BF16),
                   jax.ShapeDtypeStruct((n, TOP_K), jnp.int32), jax.ShapeDtypeStruct((n, TOP_K), F32)],
        compiler_params=pltpu.CompilerParams(
            dimension_semantics=("arbitrary",), vmem_limit_bytes=VMEM_LIMIT),
        name="post",
    )(*args)


def _dispatch_kernel(h2_ref, topi_ref, tri_ref, upper_ref, xs_ref, slots_ref, cnt_ref, off_ref):
    tb = h2_ref.shape[0]
    topi = topi_ref[...]
    lane = lax.broadcasted_iota(jnp.int32, (tb, N_EXPERTS), 1)
    picks = [lane == topi[:, k:k + 1] for k in range(TOP_K)]
    mask = picks[0]
    for pk in picks[1:]:
        mask = jnp.logical_or(mask, pk)
    maskf = jnp.where(mask, 1.0, 0.0)
    rank = _dot(tri_ref[...], maskf.astype(BF16))
    cnt = jnp.sum(maskf, axis=0, keepdims=True)
    gran = jnp.floor((cnt + (GRAN - 1)) * (1.0 / GRAN))
    goff = _dot(jnp.broadcast_to(gran, (8, N_EXPERTS)).astype(BF16), upper_ref[...])[0:1]
    off = goff * GRAN
    slot_te = off + rank
    slots = jnp.concatenate(
        [jnp.sum(jnp.where(pk, slot_te, 0.0), axis=-1, keepdims=True) for pk in picks], axis=1)
    slots_ref[...] = slots.astype(jnp.int32)
    cnt_ref[0] = cnt.astype(jnp.int32)
    off_ref[0] = off.astype(jnp.int32)
    v = jnp.where(mask, slot_te, float(NO_SLOT))
    v_hi = jnp.floor(v * (1.0 / 64.0))
    w = jnp.concatenate([v_hi * 64.0, v - v_hi * 64.0], axis=1).T.astype(BF16)
    end = off + gran * GRAN
    h2 = h2_ref[...]
    for c in range(CAP // SLOT_CHUNK):
        s_e = (lax.broadcasted_iota(jnp.int32, (SLOT_CHUNK, N_EXPERTS), 0) + c * SLOT_CHUNK).astype(F32)
        own = jnp.where(jnp.logical_and(s_e >= off, s_e < end), 1.0, 0.0)
        looked = _dot(jnp.concatenate([own, own], axis=1).astype(BF16), w)
        s_t = (lax.broadcasted_iota(jnp.int32, (SLOT_CHUNK, tb), 0) + c * SLOT_CHUNK).astype(F32)
        onehot = jnp.where(looked == s_t, 1.0, 0.0).astype(BF16)
        xs_ref[0, c * SLOT_CHUNK:(c + 1) * SLOT_CHUNK, :] = _dot(onehot, h2).astype(BF16)


def _dispatch(h2, topi):
    n, d = h2.shape
    nb = n // TB
    tri = jnp.asarray(np.tril(np.ones((TB, TB), np.float32), -1), BF16)
    upper = jnp.asarray(np.triu(np.ones((N_EXPERTS, N_EXPERTS), np.float32), 1), BF16)
    const = lambda shape: pl.BlockSpec(shape, lambda i: (0,) * len(shape))
    meta = pl.BlockSpec((1, 1, N_EXPERTS), lambda i: (i, 0, 0))
    return pl.pallas_call(
        _dispatch_kernel,
        grid=(nb,),
        in_specs=[pl.BlockSpec((TB, d), lambda i: (i, 0)), pl.BlockSpec((TB, TOP_K), lambda i: (i, 0)),
                  const(tri.shape), const(upper.shape)],
        out_specs=[pl.BlockSpec((1, CAP, d), lambda i: (i, 0, 0)),
                   pl.BlockSpec((TB, TOP_K), lambda i: (i, 0)), meta, meta],
        out_shape=[jax.ShapeDtypeStruct((nb, CAP, d), BF16), jax.ShapeDtypeStruct((n, TOP_K), jnp.int32),
                   jax.ShapeDtypeStruct((nb, 1, N_EXPERTS), jnp.int32),
                   jax.ShapeDtypeStruct((nb, 1, N_EXPERTS), jnp.int32)],
        compiler_params=pltpu.CompilerParams(
            dimension_semantics=("arbitrary",), vmem_limit_bytes=VMEM_LIMIT),
        name="dispatch",
    )(h2, topi, tri, upper)


def _ffn_kernel(gstart_sm, glist_sm, xs_hbm, wgu_ref, wd_ref, ys_hbm, xbuf, ybuf, sem_in, sem_out):
    ng = pl.num_programs(1)
    step = pl.program_id(0) * ng + pl.program_id(1)
    nsteps = pl.num_programs(0) * ng
    buf = step % 2

    def for_granules(st, fn):
        g0 = gstart_sm[st]
        n = gstart_sm[st + 1] - g0

        def per_granule(j, carry):
            fn(pl.multiple_of(glist_sm[g0 + j], GRAN), pl.multiple_of(j * GRAN, GRAN))
            return carry

        lax.fori_loop(0, n, per_granule, 0)
        return n

    def fetch(b_):
        return lambda src, dst: pltpu.make_async_copy(
            xs_hbm.at[pl.ds(src, GRAN)], xbuf.at[b_, pl.ds(dst, GRAN)], sem_in.at[b_])

    def writeback(b_):
        return lambda src, dst: pltpu.make_async_copy(
            ybuf.at[b_, pl.ds(dst, GRAN)], ys_hbm.at[pl.ds(src, GRAN)], sem_out.at[b_])

    def start(mk):
        return lambda src, dst: mk(src, dst).start()

    def wait(mk):
        return lambda src, dst: mk(src, dst).wait()

    @pl.when(step == 0)
    def _():
        xbuf[...] = jnp.zeros_like(xbuf)
        for_granules(step, start(fetch(0)))

    @pl.when(step + 1 < nsteps)
    def _():
        for_granules(step + 1, start(fetch(1 - buf)))

    ngran = for_granules(step, wait(fetch(buf)))

    @pl.when(step >= 2)
    def _():
        for_granules(step - 2, wait(writeback(buf)))

    wgu = wgu_ref[0]
    wd = wd_ref[0]

    def ffn_rows(base, rows):
        x = xbuf[buf, pl.ds(base, rows), :]
        gu = _dot(x, wgu)
        g, u = gu[:, :D_EXPERT], gu[:, D_EXPERT:]
        mid = (g * jax.nn.sigmoid(g) * u).astype(BF16)
        ybuf[buf, pl.ds(base, rows), :] = _dot(mid, wd).astype(BF16)

    nt = (ngran * GRAN + (FT - 1)) // FT
    big = FT_BIG // FT

    def big_tile(i, carry):
        ffn_rows(pl.multiple_of(i * FT_BIG, FT_BIG), FT_BIG)
        return carry

    lax.fori_loop(0, nt // big, big_tile, 0)
    size = big // 2
    while size >= 1:
        @pl.when((nt & size) != 0)
        def _(size=size):
            ffn_rows(pl.multiple_of((nt & ~(2 * size - 1)) * FT, size * FT), size * FT)
        size //= 2

    for_granules(step, start(writeback(buf)))

    @pl.when(step == nsteps - 1)
    def _():
        for_granules(step, wait(writeback(buf)))

        @pl.when(step >= 1)
        def _():
            for_granules(step - 1, wait(writeback(1 - buf)))


def _granule_list(cnt, off):
    nb = cnt.shape[0]
    seg_n = ((cnt.reshape(nb, N_EXPERTS) + (GRAN - 1)) // GRAN).T.reshape(-1)
    seg_row = (off.reshape(nb, N_EXPERTS) + jnp.arange(nb, dtype=jnp.int32)[:, None] * CAP).T.reshape(-1)
    seg_end = jnp.cumsum(seg_n)
    seg_start = seg_end - seg_n
    gmax = nb * (TB * TOP_K // GRAN + N_EXPERTS)
    base = jnp.repeat(seg_row - seg_start * GRAN, seg_n, total_repeat_length=gmax)
    glist = base + jnp.arange(gmax, dtype=jnp.int32) * GRAN
    gstart = jnp.concatenate([seg_start[::BPG], seg_end[-1:]])
    return gstart.astype(jnp.int32), glist.astype(jnp.int32)


def _ffn(xs, cnt, off, wgu, wd):
    rows, d = xs.shape
    ngroups = (rows // CAP) // BPG
    gstart, glist = _granule_list(cnt, off)
    grid_spec = pltpu.PrefetchScalarGridSpec(
        num_scalar_prefetch=2,
        grid=(N_EXPERTS, ngroups),
        in_specs=[pl.BlockSpec(memory_space=pl.ANY),
                  pl.BlockSpec((1, d, 2 * D_EXPERT), lambda e, g, c, o: (e, 0, 0)),
                  pl.BlockSpec((1, D_EXPERT, d), lambda e, g, c, o: (e, 0, 0))],
        out_specs=pl.BlockSpec(memory_space=pl.ANY),
        scratch_shapes=[pltpu.VMEM((2, BPG * TB, d), BF16), pltpu.VMEM((2, BPG * TB, d), BF16),
                        pltpu.SemaphoreType.DMA((2,)), pltpu.SemaphoreType.DMA((2,))],
    )
    return pl.pallas_call(
        _ffn_kernel,
        grid_spec=grid_spec,
        out_shape=jax.ShapeDtypeStruct(xs.shape, xs.dtype),
        input_output_aliases={2: 0},
        compiler_params=pltpu.CompilerParams(
            dimension_semantics=("arbitrary", "arbitrary"), vmem_limit_bytes=VMEM_LIMIT),
        name="ffn",
    )(gstart, glist, xs, wgu, wd)


def _combine_kernel(x1_ref, h2_ref, ys_ref, slots_ref, topw_ref, wgus_ref, wds_ref, o_ref):
    tb = x1_ref.shape[0]
    gu = _dot(h2_ref[...], wgus_ref[...])
    g, u = gu[:, :D_SHARED], gu[:, D_SHARED:]
    acc = x1_ref[...] + _dot((g * jax.nn.sigmoid(g) * u).astype(BF16), wds_ref[...])
    slots = slots_ref[...].astype(F32)
    topw = topw_ref[...]
    for c in range(CAP // SLOT_CHUNK):
        scol = (lax.broadcasted_iota(jnp.int32, (tb, SLOT_CHUNK), 1) + c * SLOT_CHUNK).astype(F32)
        gate = jnp.zeros((tb, SLOT_CHUNK), F32)
        for k in range(TOP_K):
            gate = gate + jnp.where(scol == slots[:, k:k + 1], topw[:, k:k + 1], 0.0)
        acc = acc + _dot(gate.astype(BF16), ys_ref[0, c * SLOT_CHUNK:(c + 1) * SLOT_CHUNK, :])
    o_ref[...] = acc


def _combine(x1, h2, ys, slots, topw, wgus, wds):
    n, d = x1.shape
    const = lambda shape: pl.BlockSpec(shape, lambda i: (0,) * len(shape))
    row = lambda w: pl.BlockSpec((TB, w), lambda i: (i, 0))
    return pl.pallas_call(
        _combine_kernel,
        grid=(n // TB,),
        in_specs=[row(d), row(d), pl.BlockSpec((1, CAP, d), lambda i: (i, 0, 0)), row(TOP_K), row(TOP_K),
                  const(wgus.shape), const(wds.shape)],
        out_specs=row(d),
        out_shape=jax.ShapeDtypeStruct((n, d), F32),
        compiler_params=pltpu.CompilerParams(
            dimension_semantics=("arbitrary",), vmem_limit_bytes=VMEM_LIMIT),
        name="combine",
    )(x1, h2, ys, slots, topw, wgus, wds)


def _moe(x1, h2, topi, topw, w_gate_e, w_up_e, w_down_e, w_gate_s, w_up_s, w_down_s):
    n, d = x1.shape
    wgu = jnp.concatenate([w_gate_e.astype(BF16), w_up_e.astype(BF16)], axis=-1)
    wgus = jnp.concatenate([w_gate_s.astype(BF16), w_up_s.astype(BF16)], axis=-1)
    xs, slots, cnt, off = _dispatch(h2, topi)
    ys = _ffn(xs.reshape(-1, d), cnt, off, wgu, w_down_e.astype(BF16))
    return _combine(x1, h2, ys.reshape(n // TB, CAP, d), slots, topw, wgus, w_down_s.astype(BF16))


def kernel(x, g_mix, w_in, q_norm_a, k_norm_a, q_norm_b, k_norm_b, rel_bias, b_forget, w_gate, b_gate,
           w_proj_a, w_proj_b, w_out, g_ffn, w_router, router_bias, w_gate_e, w_up_e, w_down_e,
           w_gate_s, w_up_s, w_down_s):
    batch, seq, d = x.shape
    xf = x.reshape(batch * seq, d)
    pa0, pa1, pa2, qkb, vb = _inproj(xf, g_mix, w_in, q_norm_a, k_norm_a, q_norm_b, k_norm_b, b_forget, seq)
    bias = jnp.stack([_toeplitz_bias(rel_bias, g, dil) for g, (_, dil) in enumerate(DIL_GROUPS)])
    ya = _dilated(pa0, pa1, pa2, bias, batch, seq)

    yb = _fox(qkb, vb, batch, seq)
    x1, h2, topi, topw = _post(xf, ya, yb, g_mix, w_gate, b_gate, w_proj_a, w_proj_b, w_out, g_ffn,
                               w_router, router_bias)
    out = _moe(x1, h2, topi, topw, w_gate_e, w_up_e, w_down_e, w_gate_s, w_up_s, w_down_s)
    return out.reshape(batch, seq, d)
```

```python
import functools
import math

import jax
import jax.numpy as jnp
import numpy as np
from jax import lax
from jax.experimental import pallas as pl
from jax.experimental.pallas import tpu as pltpu

D_MODEL = 1024
HEAD_DIM = 64
DIL_GROUPS = ((128, 1), (512, 4), (2048, 16))
HEADS_PER_GROUP = 4
N_HEADS_A = HEADS_PER_GROUP * len(DIL_GROUPS)
N_HEADS_B = 8
REL_BUCKETS = 32
REL_MAX_DIST = 2048
N_EXPERTS = 64
TOP_K = 8
D_EXPERT = 256
D_SHARED = 256
ROUTE_SCALE = 2.5
EPS = 1e-6

WIDTH_A = 3 * N_HEADS_A * HEAD_DIM
WIDTH_B = 3 * N_HEADS_B * HEAD_DIM
QK_B = N_HEADS_B * HEAD_DIM
OUT_A = HEADS_PER_GROUP * HEAD_DIM
OUT_B = N_HEADS_B * HEAD_DIM

LANES = 128
GROUP_W = HEADS_PER_GROUP * HEAD_DIM
WIN_J = 128
SUPER = DIL_GROUPS[-1][1] * WIN_J
NEG = -1e30
VMEM_LIMIT = 56 * 1024 * 1024

TM_IN = 512
TM_POST = 512
TQ_FOX = 1024
FOX_ROUNDING_SLACK = 1.02
FOX_EXP_HEADROOM = 60.0
FOX_MAX_SHIFT = 80.0
TB = 256
GRAN = 16
CAP = TB * TOP_K + N_EXPERTS * GRAN
SLOT_CHUNK = 512
NO_SLOT = 4095
BPG = 16
FT = 256
FT_BIG = 1024

BF16 = jnp.bfloat16
F32 = jnp.float32


def _dot(a, b):
    return jnp.dot(a, b, preferred_element_type=F32)


def _dot_nt(a, b):
    return lax.dot_general(a, b, (((1,), (1,)), ((), ())), preferred_element_type=F32)


def _split3(v):
    hi = v.astype(BF16).astype(F32)
    r = v - hi
    mid = r.astype(BF16).astype(F32)
    lo = (r - mid).astype(BF16).astype(F32)
    return hi, mid, lo


def _inproj_kernel(x_ref, g_ref, wa_ref, wb_ref, wf_ref, bd_ref, tri_ref, gain_a_ref, gain_b_ref,
                   bf_ref, shift_ref, pa0_ref, pa1_ref, pa2_ref, qkb_ref, vb_ref, carry_ref, h_ref, *,
                   tiles_per_seq):
    tm = x_ref.shape[0]
    x = x_ref[...]
    h = x * lax.rsqrt(jnp.mean(x * x, axis=-1, keepdims=True) + EPS) * g_ref[...]
    n_lane_chunks = h_ref.shape[0]
    for c in range(n_lane_chunks):
        h_ref[c] = h[:, c * LANES:(c + 1) * LANES]
    h = h.astype(BF16)
    bd = bd_ref[...]

    def headnorm(p, gain):
        ms = _dot((p * p).astype(BF16), bd)
        return p * lax.rsqrt(ms + EPS) * gain

    for g, (pa_ref, (_, dil)) in enumerate(zip((pa0_ref, pa1_ref, pa2_ref), DIL_GROUPS)):
        rows = tm // dil
        if dil == 1:
            hg = h
        else:
            hg = jnp.concatenate([jnp.concatenate(
                [h_ref[c, pl.ds(r, rows, stride=dil), :] for c in range(n_lane_chunks)], axis=1)
                for r in range(dil)], axis=0).astype(BF16)
        for part in range(3):
            cols = slice(part * GROUP_W, (part + 1) * GROUP_W)
            p = _dot(hg, wa_ref[g, :, cols])
            if part < 2:
                p = headnorm(p, gain_a_ref[part:part + 1, :])
            p = p.astype(BF16)
            if dil == 1:
                pa_ref[:, cols] = p
            else:
                for r in range(dil):
                    pa_ref[0, r, :, cols] = p[r * rows:(r + 1) * rows, :]

    f = _dot(h, wf_ref[...]) + bf_ref[...]
    logf = jnp.minimum(f, 0.0) - jnp.log1p(jnp.exp(-jnp.abs(f)))
    tri = tri_ref[...]
    lh, lm, ll = _split3(logf)
    cum = _dot(tri, lh.astype(BF16)) + _dot(tri, lm.astype(BF16)) + _dot(tri, ll.astype(BF16))

    @pl.when(pl.program_id(0) % tiles_per_seq == 0)
    def _():
        carry_ref[...] = jnp.zeros_like(carry_ref)

    cum = cum + carry_ref[0:1, :]
    carry_ref[0:1, :] = cum[tm - 1:tm, :]
    ch, cm, cl = _split3(cum)

    j = lax.broadcasted_iota(jnp.int32, (tm, HEAD_DIM), 1)

    def ext_cols(vals):
        out = jnp.zeros((tm, HEAD_DIM), F32)
        for pos, val in reversed(list(enumerate(vals))):
            out = jnp.where(j == pos, val, out)
        return out

    for c in range(QK_B // GROUP_W):
        wcols = lambda part: slice(part * QK_B + c * GROUP_W, part * QK_B + (c + 1) * GROUP_W)
        pq = headnorm(_dot(h, wb_ref[:, wcols(0)]), gain_b_ref[0:1, :])
        pk = headnorm(_dot(h, wb_ref[:, wcols(1)]), gain_b_ref[1:2, :])
        pv = _dot(h, wb_ref[:, wcols(2)])
        r = _dot((pq * pk).astype(BF16), bd) * HEAD_DIM + shift_ref[...]
        rh, rm, rl = _split3(r)
        for hh in range(HEADS_PER_GROUP):
            head = c * HEADS_PER_GROUP + hh
            lanes = slice(hh * HEAD_DIM, (hh + 1) * HEAD_DIM)
            col = lambda a, idx: a[:, idx:idx + 1]
            cs = [col(ch, head), col(cm, head), col(cl, head)]
            rs = [col(rh, hh * HEAD_DIM), col(rm, hh * HEAD_DIM), col(rl, hh * HEAD_DIM)]
            ext_q = ext_cols(cs + [1.0] * 3 + [-v for v in rs])
            ext_k = ext_cols([1.0] * 3 + [-v for v in cs] + [1.0] * 3)
            ext_v = ext_cols([1.0])
            for part, (val, ext) in enumerate(((pq, ext_q), (pk, ext_k))):
                o0 = (part * N_HEADS_B + head) * LANES
                qkb_ref[:, o0:o0 + LANES] = jnp.concatenate([val[:, lanes], ext], axis=-1).astype(BF16)
            vb_ref[:, head * LANES:(head + 1) * LANES] = jnp.concatenate(
                [pv[:, lanes], ext_v], axis=-1).astype(BF16)


def _inproj(xf, g_mix, w_in, q_norm_a, k_norm_a, q_norm_b, k_norm_b, b_forget, fox_shift, seq):
    n, d = xf.shape
    tm = TM_IN
    scale = HEAD_DIM ** -0.5
    w_bf = w_in.astype(BF16)
    qkv_w = N_HEADS_A * HEAD_DIM
    wa = jnp.stack([jnp.concatenate(
        [w_bf[:, part * qkv_w + g * GROUP_W: part * qkv_w + (g + 1) * GROUP_W] for part in range(3)],
        axis=1) for g in range(len(DIL_GROUPS))])
    wb = w_bf[:, WIDTH_A:WIDTH_A + WIDTH_B]
    wf = jnp.pad(w_bf[:, WIDTH_A + WIDTH_B:], ((0, 0), (0, LANES - N_HEADS_B)))
    bfp = jnp.pad(b_forget.astype(F32), (0, LANES - N_HEADS_B)).reshape(1, LANES)
    seg = np.arange(GROUP_W) // HEAD_DIM
    bd = jnp.asarray((seg[:, None] == seg[None, :]).astype(np.float32) / HEAD_DIM, BF16)
    tri = jnp.asarray(np.tril(np.ones((tm, tm), np.float32)), BF16)
    gain_a = jnp.stack([jnp.tile(q_norm_a, HEADS_PER_GROUP) * scale, jnp.tile(k_norm_a, HEADS_PER_GROUP)])
    gain_b = jnp.stack([jnp.tile(q_norm_b, HEADS_PER_GROUP) * scale, jnp.tile(k_norm_b, HEADS_PER_GROUP)])
    const = lambda shape: pl.BlockSpec(shape, lambda i: (0,) * len(shape))
    tps = seq // tm
    batch = n // seq
    qkv3 = 3 * GROUP_W
    (_, d1), (_, d2) = DIL_GROUPS[1], DIL_GROUPS[2]
    return pl.pallas_call(
        functools.partial(_inproj_kernel, tiles_per_seq=tps),
        grid=(n // tm,),
        in_specs=[
            pl.BlockSpec((tm, d), lambda i: (i, 0)),
            const((1, d)), const(wa.shape), const(wb.shape), const(wf.shape),
            const(bd.shape), const(tri.shape), const(gain_a.shape), const(gain_b.shape),
            const(bfp.shape), const((1, 1)),
        ],
        out_specs=[
            pl.BlockSpec((tm, qkv3), lambda i: (i, 0)),
            pl.BlockSpec((1, d1, tm // d1, qkv3), lambda i: (i // tps, 0, i % tps, 0)),
            pl.BlockSpec((1, d2, tm // d2, qkv3), lambda i: (i // tps, 0, i % tps, 0)),
            pl.BlockSpec((tm, 2 * N_HEADS_B * LANES), lambda i: (i, 0)),
            pl.BlockSpec((tm, N_HEADS_B * LANES), lambda i: (i, 0)),
        ],
        out_shape=[
            jax.ShapeDtypeStruct((n, qkv3), BF16),
            jax.ShapeDtypeStruct((batch, d1, seq // d1, qkv3), BF16),
            jax.ShapeDtypeStruct((batch, d2, seq // d2, qkv3), BF16),
            jax.ShapeDtypeStruct((n, 2 * N_HEADS_B * LANES), BF16),
            jax.ShapeDtypeStruct((n, N_HEADS_B * LANES), BF16),
        ],
        scratch_shapes=[pltpu.VMEM((8, LANES), F32), pltpu.VMEM((d // LANES, tm, LANES), F32)],
        compiler_params=pltpu.CompilerParams(
            dimension_semantics=("arbitrary",), vmem_limit_bytes=VMEM_LIMIT),
        name="inproj",
    )(xf, g_mix.reshape(1, d), wa, wb, wf, bd, tri, gain_a, gain_b, bfp, fox_shift.reshape(1, 1))


def _dilated_kernel(p0_ref, h0_ref, p1_ref, h1_ref, p2_ref, h2_ref, bias_ref, o_ref, acc_ref, lse_ref):
    tq = WIN_J
    first_sb = pl.program_id(1) == 0
    lane_head = lax.broadcasted_iota(jnp.int32, (tq, GROUP_W), 1) // HEAD_DIM
    prev_col = lax.broadcasted_iota(jnp.int32, (tq, 2 * tq), 1) < tq
    qc, kc_, vc_ = (slice(0, GROUP_W), slice(GROUP_W, 2 * GROUP_W), slice(2 * GROUP_W, 3 * GROUP_W))

    def attend(g, q, kp, kc, vp, vc, no_prev):
        kcat = jnp.concatenate([kp, kc], axis=0)
        vcat = jnp.concatenate([vp, vc], axis=0)
        dead = jnp.logical_and(no_prev, prev_col)
        q4 = jnp.concatenate([jnp.where(lane_head == hh, q, jnp.zeros_like(q))
                              for hh in range(HEADS_PER_GROUP)], axis=0)
        s = _dot_nt(q4, kcat) + bias_ref[g].reshape(HEADS_PER_GROUP * tq, 2 * tq)
        s = jnp.where(jnp.concatenate([dead] * HEADS_PER_GROUP, axis=0), NEG, s)
        m = jnp.max(s, axis=-1, keepdims=True)
        p = jnp.exp(s - m)
        l = jnp.sum(p, axis=-1, keepdims=True)
        o4 = _dot(p.astype(BF16), vcat) * (1.0 / l)
        lse4 = m + jnp.log(l)
        acc = o4[0:tq]
        lse = jnp.broadcast_to(lse4[0:tq], (tq, GROUP_W))
        for hh in range(1, HEADS_PER_GROUP):
            sel = lane_head == hh
            acc = jnp.where(sel, o4[hh * tq:(hh + 1) * tq], acc)
            lse = jnp.where(sel, lse4[hh * tq:(hh + 1) * tq], lse)
        return acc, lse

    n_half = GROUP_W // LANES

    def merge(rows, acc, lse):
        for c in range(n_half):
            lanes = slice(c * LANES, (c + 1) * LANES)
            l1 = lse_ref[c, rows, :]
            mx = jnp.maximum(l1, lse[:, lanes])
            w1 = jnp.exp(l1 - mx)
            w2 = jnp.exp(lse[:, lanes] - mx)
            den = w1 + w2
            acc_ref[c, rows, :] = (w1 * acc_ref[c, rows, :] + w2 * acc[:, lanes]) / den
            lse_ref[c, rows, :] = mx + jnp.log(den)

    def pick(first, halo, body):
        return jnp.where(first, halo, body)

    def body0(j, carry):
        st = pl.multiple_of(j * tq, tq)
        pst = pl.multiple_of(jnp.maximum(j - 1, 0) * tq, tq)
        cur, prv = pl.ds(st, tq), pl.ds(pst, tq)
        acc, lse = attend(
            0, p0_ref[0, cur, qc],
            pick(j == 0, h0_ref[0, :, kc_], p0_ref[0, prv, kc_]), p0_ref[0, cur, kc_],
            pick(j == 0, h0_ref[0, :, vc_], p0_ref[0, prv, vc_]), p0_ref[0, cur, vc_],
            jnp.logical_and(j == 0, first_sb))
        for c in range(n_half):
            acc_ref[c, cur, :] = acc[:, c * LANES:(c + 1) * LANES]
            lse_ref[c, cur, :] = lse[:, c * LANES:(c + 1) * LANES]
        return carry

    lax.fori_loop(0, SUPER // tq, body0, 0)

    d1 = DIL_GROUPS[1][1]
    nsub1 = SUPER // d1 // tq
    def body1(t, carry):
        r, ii = t // nsub1, t % nsub1
        st = pl.multiple_of(ii * tq, tq)
        pst = pl.multiple_of(jnp.maximum(ii - 1, 0) * tq, tq)
        cur, prv = pl.ds(st, tq), pl.ds(pst, tq)
        acc, lse = attend(
            1, p1_ref[0, r, cur, qc],
            pick(ii == 0, h1_ref[0, r, :, kc_], p1_ref[0, r, prv, kc_]), p1_ref[0, r, cur, kc_],
            pick(ii == 0, h1_ref[0, r, :, vc_], p1_ref[0, r, prv, vc_]), p1_ref[0, r, cur, vc_],
            jnp.logical_and(ii == 0, first_sb))
        merge(pl.ds(ii * (tq * d1) + r, tq, stride=d1), acc, lse)
        return carry

    lax.fori_loop(0, d1 * nsub1, body1, 0)

    d2 = DIL_GROUPS[2][1]

    def body2(r, carry):
        acc, lse = attend(2, p2_ref[0, r, :, qc], h2_ref[0, r, :, kc_], p2_ref[0, r, :, kc_],
                          h2_ref[0, r, :, vc_], p2_ref[0, r, :, vc_], first_sb)
        merge(pl.ds(r, tq, stride=d2), acc, lse)
        return carry

    lax.fori_loop(0, d2, body2, 0)

    for c in range(n_half):
        o_ref[0, :, c * LANES:(c + 1) * LANES] = acc_ref[c].astype(o_ref.dtype)


def _rel_bucket(dist):
    max_exact = REL_BUCKETS // 2
    n = jnp.maximum(dist.astype(F32), 1.0)
    large = max_exact + (jnp.log(n / max_exact) / math.log(REL_MAX_DIST / max_exact)
                         * (REL_BUCKETS - max_exact)).astype(jnp.int32)
    large = jnp.minimum(large, REL_BUCKETS - 1)
    return jnp.where(dist < max_exact, dist, large)


def _toeplitz_bias(rel_bias, g, dil):
    tq = WIN_J
    offs = dil * jnp.arange(WIN_J + 1, dtype=jnp.int32)
    hs = slice(g * HEADS_PER_GROUP, (g + 1) * HEADS_PER_GROUP)
    tab = rel_bias[_rel_bucket(offs)][:, hs].T.astype(F32)
    period = 3 * tq
    neg = lambda w: jnp.full((HEADS_PER_GROUP, w), NEG, F32)
    vec = jnp.concatenate([neg(tq - 1), tab[:, ::-1], neg(period - 2 * tq)], axis=1)
    flat = jnp.broadcast_to(vec[:, None, :], (HEADS_PER_GROUP, tq, period)).reshape(HEADS_PER_GROUP, -1)
    skew = flat[:, :tq * (period - 1)].reshape(HEADS_PER_GROUP, tq, period - 1)
    return skew[:, :, tq - 1:3 * tq - 1]


def _dilated(pa0, pa1, pa2, bias, batch, seq):
    tq = WIN_J
    qkv3 = 3 * GROUP_W
    (_, d1), (_, d2) = DIL_GROUPS[1], DIL_GROUPS[2]
    nsb = seq // SUPER
    p0 = pa0.reshape(batch, seq, qkv3)
    prev_blk = lambda per_sb: (lambda b, s: jnp.maximum(s * per_sb - 1, 0))
    h0i, h1i, h2i = prev_blk(SUPER // tq), prev_blk(SUPER // d1 // tq), prev_blk(SUPER // d2 // tq)
    out = pl.pallas_call(
        _dilated_kernel,
        grid=(batch, nsb),
        in_specs=[
            pl.BlockSpec((1, SUPER, qkv3), lambda b, s: (b, s, 0)),
            pl.BlockSpec((1, tq, qkv3), lambda b, s: (b, h0i(b, s), 0)),
            pl.BlockSpec((1, d1, SUPER // d1, qkv3), lambda b, s: (b, 0, s, 0)),
            pl.BlockSpec((1, d1, tq, qkv3), lambda b, s: (b, 0, h1i(b, s), 0)),
            pl.BlockSpec((1, d2, SUPER // d2, qkv3), lambda b, s: (b, 0, s, 0)),
            pl.BlockSpec((1, d2, tq, qkv3), lambda b, s: (b, 0, h2i(b, s), 0)),
            pl.BlockSpec(bias.shape, lambda b, s: (0, 0, 0, 0)),
        ],
        out_specs=pl.BlockSpec((1, SUPER, GROUP_W), lambda b, s: (b, s, 0)),
        out_shape=jax.ShapeDtypeStruct((batch, seq, GROUP_W), BF16),
        scratch_shapes=[pltpu.VMEM((GROUP_W // LANES, SUPER, LANES), F32)] * 2,
        compiler_params=pltpu.CompilerParams(
            dimension_semantics=("arbitrary", "arbitrary"), vmem_limit_bytes=VMEM_LIMIT),
        name="dilated",
    )(p0, p0, pa1, pa1, pa2, pa2, bias)
    return out.reshape(batch * seq, GROUP_W)


def _fox_kernel(q_ref, k_ref, v_ref, o_ref, *, online):
    tq = q_ref.shape[1]
    tk = tq
    qi = pl.program_id(2)
    qs = [q_ref[0, :, 0:LANES], q_ref[0, :, LANES:2 * LANES]]
    row = lax.broadcasted_iota(jnp.int32, (tq, tk), 0)
    col = lax.broadcasted_iota(jnp.int32, (tq, tk), 1)
    causal = row >= col

    def chunk(ki, carry, diagonal):
        ms, accs = carry
        start = pl.multiple_of(ki * tk, tk)
        new_m, new_acc = [], []
        for hh in range(2):
            lanes = slice(hh * LANES, (hh + 1) * LANES)
            s = _dot_nt(qs[hh], k_ref[0, pl.ds(start, tk), lanes])
            if diagonal:
                s = jnp.where(causal, s, NEG)
            acc = accs[hh]
            if online:
                m_new = jnp.maximum(ms[hh], jnp.max(s, axis=-1, keepdims=True))
                acc = acc * jnp.exp(ms[hh] - m_new)
                s = s - m_new
                new_m.append(m_new)
            new_acc.append(acc + _dot(jnp.exp(s).astype(BF16), v_ref[0, pl.ds(start, tk), lanes]))
        return (tuple(new_m), tuple(new_acc))

    m0 = jnp.full((tq, 1), NEG, F32)
    a0 = jnp.zeros((tq, LANES), F32)
    init = ((m0, m0) if online else (), (a0, a0))
    carry = lax.fori_loop(0, qi, lambda ki, c: chunk(ki, c, False), init)
    _, accs = chunk(qi, carry, True)
    o_ref[0] = jnp.concatenate(
        [a[:, :HEAD_DIM] / a[:, HEAD_DIM:HEAD_DIM + 1] for a in accs], axis=-1).astype(o_ref.dtype)


def _fox(qkb, vb, online, batch, seq):
    tq = TQ_FOX
    pairs = N_HEADS_B // 2
    qkv = qkb.reshape(batch, seq, 2 * N_HEADS_B * LANES)
    vv = vb.reshape(batch, seq, N_HEADS_B * LANES)

    def call(is_online):
        return pl.pallas_call(
            functools.partial(_fox_kernel, online=is_online),
            grid=(batch, pairs, seq // tq),
            in_specs=[
                pl.BlockSpec((1, tq, 2 * LANES), lambda b, p, i: (b, i, p)),
                pl.BlockSpec((1, seq, 2 * LANES), lambda b, p, i: (b, 0, pairs + p)),
                pl.BlockSpec((1, seq, 2 * LANES), lambda b, p, i: (b, 0, p)),
            ],
            out_specs=pl.BlockSpec((1, tq, LANES), lambda b, p, i: (b, i, p)),
            out_shape=jax.ShapeDtypeStruct((batch, seq, OUT_B), BF16),
            compiler_params=pltpu.CompilerParams(
                dimension_semantics=("arbitrary", "arbitrary", "arbitrary"), vmem_limit_bytes=VMEM_LIMIT),
            name="fox_online" if is_online else "fox",
        )(qkv, qkv, vv)

    out = lax.cond(online, lambda: call(True), lambda: call(False))
    return out.reshape(batch * seq, OUT_B)


def _fox_shift(q_norm_b, k_norm_b):
    bound = HEAD_DIM * (HEAD_DIM ** -0.5) * jnp.max(jnp.abs(q_norm_b)) * jnp.max(jnp.abs(k_norm_b))
    shift = jnp.maximum(2.0 * FOX_ROUNDING_SLACK * bound - FOX_EXP_HEADROOM, 0.0).astype(F32)
    return shift, shift > FOX_MAX_SHIFT


def _post_kernel(x_ref, ya_ref, yb_ref, gmix_ref, wg_ref, bg_ref, wpa_ref, wpb_ref, wo_ref,
                 gffn_ref, wr_ref, rb_ref, x1_ref, h2_ref, topi_ref, topw_ref):
    d = x_ref.shape[1]
    x = x_ref[...]
    h = (x * lax.rsqrt(jnp.mean(x * x, axis=-1, keepdims=True) + EPS) * gmix_ref[...]).astype(BF16)
    gates = jax.nn.sigmoid(_dot(h, wg_ref[...]) + bg_ref[...])
    merged = gates[:, :d] * _dot(ya_ref[...], wpa_ref[...]) + gates[:, d:] * _dot(yb_ref[...], wpb_ref[...])
    x1 = x + _dot(merged.astype(BF16), wo_ref[...])
    x1_ref[...] = x1
    h2 = x1 * lax.rsqrt(jnp.mean(x1 * x1, axis=-1, keepdims=True) + EPS) * gffn_ref[...]
    h2_ref[...] = h2.astype(BF16)

    hh, hm, _ = _split3(h2)
    wr = wr_ref[...]
    wh = wr.astype(BF16)
    wl = (wr - wh.astype(F32)).astype(BF16)
    hh, hm = hh.astype(BF16), hm.astype(BF16)
    logits = _dot(hh, wh) + _dot(hm, wh) + _dot(hh, wl)
    scores = jax.nn.sigmoid(logits)
    biased = scores + rb_ref[...]
    lane = lax.broadcasted_iota(jnp.int32, scores.shape, 1)
    chosen = jnp.zeros(scores.shape, jnp.bool_)
    idx, val = [], []
    for _ in range(TOP_K):
        cur = jnp.where(chosen, -jnp.inf, biased)
        mx = jnp.max(cur, axis=-1, keepdims=True)
        first = jnp.min(jnp.where(cur == mx, lane, N_EXPERTS), axis=-1, keepdims=True)
        pick = lane == first
        chosen = jnp.logical_or(chosen, pick)
        idx.append(first)
        val.append(jnp.sum(jnp.where(pick, scores, 0.0), axis=-1, keepdims=True))
    top_s = jnp.concatenate(val, axis=1)
    topi_ref[...] = jnp.concatenate(idx, axis=1)
    topw_ref[...] = top_s / jnp.sum(top_s, axis=-1, keepdims=True) * ROUTE_SCALE


def _post(xf, ya, yb, g_mix, w_gate, b_gate, w_proj_a, w_proj_b, w_out, g_ffn, w_router, router_bias):
    n, d = xf.shape
    tm = TM_POST
    const = lambda shape: pl.BlockSpec(shape, lambda i: (0,) * len(shape))
    row = lambda w: pl.BlockSpec((tm, w), lambda i: (i, 0))
    args = [xf, ya, yb, g_mix.reshape(1, d), w_gate.astype(BF16), b_gate.reshape(1, 2 * d),
            w_proj_a.astype(BF16), w_proj_b.astype(BF16), w_out.astype(BF16), g_ffn.reshape(1, d),
            w_router.astype(F32), router_bias.astype(F32).reshape(1, N_EXPERTS)]
    in_specs = [row(d), row(OUT_A), row(OUT_B)] + [const(a.shape) for a in args[3:]]
    return pl.pallas_call(
        _post_kernel,
        grid=(n // tm,),
        in_specs=in_specs,
        out_specs=[row(d), row(d), row(TOP_K), row(TOP_K)],
        out_shape=[jax.ShapeDtypeStruct((n, d), F32), jax.ShapeDtypeStruct((n, d), BF16),
                   jax.ShapeDtypeStruct((n, TOP_K), jnp.int32), jax.ShapeDtypeStruct((n, TOP_K), F32)],
        compiler_params=pltpu.CompilerParams(
            dimension_semantics=("arbitrary",), vmem_limit_bytes=VMEM_LIMIT),
        name="post",
    )(*args)


def _dispatch_kernel(h2_ref, topi_ref, tri_ref, upper_ref, xs_ref, slots_ref, cnt_ref, off_ref):
    tb = h2_ref.shape[0]
    topi = topi_ref[...]
    lane = lax.broadcasted_iota(jnp.int32, (tb, N_EXPERTS), 1)
    picks = [lane == topi[:, k:k + 1] for k in range(TOP_K)]
    mask = picks[0]
    for pk in picks[1:]:
        mask = jnp.logical_or(mask, pk)
    maskf = jnp.where(mask, 1.0, 0.0)
    rank = _dot(tri_ref[...], maskf.astype(BF16))
    cnt = jnp.sum(maskf, axis=0, keepdims=True)
    gran = jnp.floor((cnt + (GRAN - 1)) * (1.0 / GRAN))
    goff = _dot(jnp.broadcast_to(gran, (8, N_EXPERTS)).astype(BF16), upper_ref[...])[0:1]
    off = goff * GRAN
    slot_te = off + rank
    slots = jnp.concatenate(
        [jnp.sum(jnp.where(pk, slot_te, 0.0), axis=-1, keepdims=True) for pk in picks], axis=1)
    slots_ref[...] = slots.astype(jnp.int32)
    cnt_ref[0] = cnt.astype(jnp.int32)
    off_ref[0] = off.astype(jnp.int32)
    v = jnp.where(mask, slot_te, float(NO_SLOT))
    v_hi = jnp.floor(v * (1.0 / 64.0))
    w = jnp.concatenate([v_hi * 64.0, v - v_hi * 64.0], axis=1).T.astype(BF16)
    end = off + gran * GRAN
    h2 = h2_ref[...]
    for c in range(CAP // SLOT_CHUNK):
        s_e = (lax.broadcasted_iota(jnp.int32, (SLOT_CHUNK, N_EXPERTS), 0) + c * SLOT_CHUNK).astype(F32)
        own = jnp.where(jnp.logical_and(s_e >= off, s_e < end), 1.0, 0.0)
        looked = _dot(jnp.concatenate([own, own], axis=1).astype(BF16), w)
        s_t = (lax.broadcasted_iota(jnp.int32, (SLOT_CHUNK, tb), 0) + c * SLOT_CHUNK).astype(F32)
        onehot = jnp.where(looked == s_t, 1.0, 0.0).astype(BF16)
        xs_ref[0, c * SLOT_CHUNK:(c + 1) * SLOT_CHUNK, :] = _dot(onehot, h2).astype(BF16)


def _dispatch(h2, topi):
    n, d = h2.shape
    nb = n // TB
    tri = jnp.asarray(np.tril(np.ones((TB, TB), np.float32), -1), BF16)
    upper = jnp.asarray(np.triu(np.ones((N_EXPERTS, N_EXPERTS), np.float32), 1), BF16)
    const = lambda shape: pl.BlockSpec(shape, lambda i: (0,) * len(shape))
    meta = pl.BlockSpec((1, 1, N_EXPERTS), lambda i: (i, 0, 0))
    return pl.pallas_call(
        _dispatch_kernel,
        grid=(nb,),
        in_specs=[pl.BlockSpec((TB, d), lambda i: (i, 0)), pl.BlockSpec((TB, TOP_K), lambda i: (i, 0)),
                  const(tri.shape), const(upper.shape)],
        out_specs=[pl.BlockSpec((1, CAP, d), lambda i: (i, 0, 0)),
                   pl.BlockSpec((TB, TOP_K), lambda i: (i, 0)), meta, meta],
        out_shape=[jax.ShapeDtypeStruct((nb, CAP, d), BF16), jax.ShapeDtypeStruct((n, TOP_K), jnp.int32),
                   jax.ShapeDtypeStruct((nb, 1, N_EXPERTS), jnp.int32),
                   jax.ShapeDtypeStruct((nb, 1, N_EXPERTS), jnp.int32)],
        compiler_params=pltpu.CompilerParams(
            dimension_semantics=("arbitrary",), vmem_limit_bytes=VMEM_LIMIT),
        name="dispatch",
    )(h2, topi, tri, upper)


def _ffn_kernel(gstart_sm, glist_sm, xs_hbm, wgu_ref, wd_ref, ys_hbm, xbuf, ybuf, sem_in, sem_out):
    ng = pl.num_programs(1)
    step = pl.program_id(0) * ng + pl.program_id(1)
    nsteps = pl.num_programs(0) * ng
    buf = step % 2

    def for_granules(st, fn):
        g0 = gstart_sm[st]
        n = gstart_sm[st + 1] - g0

        def per_granule(j, carry):
            fn(pl.multiple_of(glist_sm[g0 + j], GRAN), pl.multiple_of(j * GRAN, GRAN))
            return carry

        lax.fori_loop(0, n, per_granule, 0)
        return n

    def fetch(b_):
        return lambda src, dst: pltpu.make_async_copy(
            xs_hbm.at[pl.ds(src, GRAN)], xbuf.at[b_, pl.ds(dst, GRAN)], sem_in.at[b_])

    def writeback(b_):
        return lambda src, dst: pltpu.make_async_copy(
            ybuf.at[b_, pl.ds(dst, GRAN)], ys_hbm.at[pl.ds(src, GRAN)], sem_out.at[b_])

    def start(mk):
        return lambda src, dst: mk(src, dst).start()

    def wait(mk):
        return lambda src, dst: mk(src, dst).wait()

    @pl.when(step == 0)
    def _():
        xbuf[...] = jnp.zeros_like(xbuf)
        for_granules(step, start(fetch(0)))

    @pl.when(step + 1 < nsteps)
    def _():
        for_granules(step + 1, start(fetch(1 - buf)))

    ngran = for_granules(step, wait(fetch(buf)))

    @pl.when(step >= 2)
    def _():
        for_granules(step - 2, wait(writeback(buf)))

    wgu = wgu_ref[0]
    wd = wd_ref[0]

    def ffn_rows(base, rows):
        x = xbuf[buf, pl.ds(base, rows), :]
        gu = _dot(x, wgu)
        g, u = gu[:, :D_EXPERT], gu[:, D_EXPERT:]
        mid = (g * jax.nn.sigmoid(g) * u).astype(BF16)
        ybuf[buf, pl.ds(base, rows), :] = _dot(mid, wd).astype(BF16)

    nt = (ngran * GRAN + (FT - 1)) // FT
    big = FT_BIG // FT

    def big_tile(i, carry):
        ffn_rows(pl.multiple_of(i * FT_BIG, FT_BIG), FT_BIG)
        return carry

    lax.fori_loop(0, nt // big, big_tile, 0)
    size = big // 2
    while size >= 1:
        @pl.when((nt & size) != 0)
        def _(size=size):
            ffn_rows(pl.multiple_of((nt & ~(2 * size - 1)) * FT, size * FT), size * FT)
        size //= 2

    for_granules(step, start(writeback(buf)))

    @pl.when(step == nsteps - 1)
    def _():
        for_granules(step, wait(writeback(buf)))

        @pl.when(step >= 1)
        def _():
            for_granules(step - 1, wait(writeback(1 - buf)))


def _granule_list(cnt, off):
    nb = cnt.shape[0]
    seg_n = ((cnt.reshape(nb, N_EXPERTS) + (GRAN - 1)) // GRAN).T.reshape(-1)
    seg_row = (off.reshape(nb, N_EXPERTS) + jnp.arange(nb, dtype=jnp.int32)[:, None] * CAP).T.reshape(-1)
    seg_end = jnp.cumsum(seg_n)
    seg_start = seg_end - seg_n
    gmax = nb * (TB * TOP_K // GRAN + N_EXPERTS)
    base = jnp.repeat(seg_row - seg_start * GRAN, seg_n, total_repeat_length=gmax)
    glist = base + jnp.arange(gmax, dtype=jnp.int32) * GRAN
    gstart = jnp.concatenate([seg_start[::BPG], seg_end[-1:]])
    return gstart.astype(jnp.int32), glist.astype(jnp.int32)


def _ffn(xs, cnt, off, wgu, wd):
    rows, d = xs.shape
    ngroups = (rows // CAP) // BPG
    gstart, glist = _granule_list(cnt, off)
    grid_spec = pltpu.PrefetchScalarGridSpec(
        num_scalar_prefetch=2,
        grid=(N_EXPERTS, ngroups),
        in_specs=[pl.BlockSpec(memory_space=pl.ANY),
                  pl.BlockSpec((1, d, 2 * D_EXPERT), lambda e, g, c, o: (e, 0, 0)),
                  pl.BlockSpec((1, D_EXPERT, d), lambda e, g, c, o: (e, 0, 0))],
        out_specs=pl.BlockSpec(memory_space=pl.ANY),
        scratch_shapes=[pltpu.VMEM((2, BPG * TB, d), BF16), pltpu.VMEM((2, BPG * TB, d), BF16),
                        pltpu.SemaphoreType.DMA((2,)), pltpu.SemaphoreType.DMA((2,))],
    )
    return pl.pallas_call(
        _ffn_kernel,
        grid_spec=grid_spec,
        out_shape=jax.ShapeDtypeStruct(xs.shape, xs.dtype),
        input_output_aliases={2: 0},
        compiler_params=pltpu.CompilerParams(
            dimension_semantics=("arbitrary", "arbitrary"), vmem_limit_bytes=VMEM_LIMIT),
        name="ffn",
    )(gstart, glist, xs, wgu, wd)


def _combine_kernel(x1_ref, h2_ref, ys_ref, slots_ref, topw_ref, wgus_ref, wds_ref, o_ref):
    tb = x1_ref.shape[0]
    gu = _dot(h2_ref[...], wgus_ref[...])
    g, u = gu[:, :D_SHARED], gu[:, D_SHARED:]
    acc = x1_ref[...] + _dot((g * jax.nn.sigmoid(g) * u).astype(BF16), wds_ref[...])
    slots = slots_ref[...].astype(F32)
    topw = topw_ref[...]
    for c in range(CAP // SLOT_CHUNK):
        scol = (lax.broadcasted_iota(jnp.int32, (tb, SLOT_CHUNK), 1) + c * SLOT_CHUNK).astype(F32)
        gate = jnp.zeros((tb, SLOT_CHUNK), F32)
        for k in range(TOP_K):
            gate = gate + jnp.where(scol == slots[:, k:k + 1], topw[:, k:k + 1], 0.0)
        acc = acc + _dot(gate.astype(BF16), ys_ref[0, c * SLOT_CHUNK:(c + 1) * SLOT_CHUNK, :])
    o_ref[...] = acc


def _combine(x1, h2, ys, slots, topw, wgus, wds):
    n, d = x1.shape
    const = lambda shape: pl.BlockSpec(shape, lambda i: (0,) * len(shape))
    row = lambda w: pl.BlockSpec((TB, w), lambda i: (i, 0))
    return pl.pallas_call(
        _combine_kernel,
        grid=(n // TB,),
        in_specs=[row(d), row(d), pl.BlockSpec((1, CAP, d), lambda i: (i, 0, 0)), row(TOP_K), row(TOP_K),
                  const(wgus.shape), const(wds.shape)],
        out_specs=row(d),
        out_shape=jax.ShapeDtypeStruct((n, d), F32),
        compiler_params=pltpu.CompilerParams(
            dimension_semantics=("arbitrary",), vmem_limit_bytes=VMEM_LIMIT),
        name="combine",
    )(x1, h2, ys, slots, topw, wgus, wds)


def _moe(x1, h2, topi, topw, w_gate_e, w_up_e, w_down_e, w_gate_s, w_up_s, w_down_s):
    n, d = x1.shape
    wgu = jnp.concatenate([w_gate_e.astype(BF16), w_up_e.astype(BF16)], axis=-1)
    wgus = jnp.concatenate([w_gate_s.astype(BF16), w_up_s.astype(BF16)], axis=-1)
    xs, slots, cnt, off = _dispatch(h2, topi)
    ys = _ffn(xs.reshape(-1, d), cnt, off, wgu, w_down_e.astype(BF16))
    return _combine(x1, h2, ys.reshape(n // TB, CAP, d), slots, topw, wgus, w_down_s.astype(BF16))


def kernel(x, g_mix, w_in, q_norm_a, k_norm_a, q_norm_b, k_norm_b, rel_bias, b_forget, w_gate, b_gate,
           w_proj_a, w_proj_b, w_out, g_ffn, w_router, router_bias, w_gate_e, w_up_e, w_down_e,
           w_gate_s, w_up_s, w_down_s):
    batch, seq, d = x.shape
    xf = x.reshape(batch * seq, d)
    fox_shift, fox_online = _fox_shift(q_norm_b, k_norm_b)
    pa0, pa1, pa2, qkb, vb = _inproj(xf, g_mix, w_in, q_norm_a, k_norm_a, q_norm_b, k_norm_b, b_forget,
                                     fox_shift, seq)
    bias = jnp.stack([_toeplitz_bias(rel_bias, g, dil) for g, (_, dil) in enumerate(DIL_GROUPS)])
    ya = _dilated(pa0, pa1, pa2, bias, batch, seq)

    yb = _fox(qkb, vb, fox_online, batch, seq)
    x1, h2, topi, topw = _post(xf, ya, yb, g_mix, w_gate, b_gate, w_proj_a, w_proj_b, w_out, g_ffn,
                               w_router, router_bias)
    out = _moe(x1, h2, topi, topw, w_gate_e, w_up_e, w_down_e, w_gate_s, w_up_s, w_down_s)
    return out.reshape(batch, seq, d)
```

```python
import functools
import math

import jax
import jax.numpy as jnp
import numpy as np
from jax import lax
from jax.experimental import pallas as pl
from jax.experimental.pallas import tpu as pltpu

D_MODEL = 1024
HEAD_DIM = 64
DIL_GROUPS = ((128, 1), (512, 4), (2048, 16))
HEADS_PER_GROUP = 4
N_HEADS_A = HEADS_PER_GROUP * len(DIL_GROUPS)
N_HEADS_B = 8
REL_BUCKETS = 32
REL_MAX_DIST = 2048
N_EXPERTS = 64
TOP_K = 8
D_EXPERT = 256
D_SHARED = 256
ROUTE_SCALE = 2.5
EPS = 1e-6

WIDTH_A = 3 * N_HEADS_A * HEAD_DIM
WIDTH_B = 3 * N_HEADS_B * HEAD_DIM
QK_B = N_HEADS_B * HEAD_DIM
OUT_A = HEADS_PER_GROUP * HEAD_DIM
OUT_B = N_HEADS_B * HEAD_DIM

LANES = 128
GROUP_W = HEADS_PER_GROUP * HEAD_DIM
WIN_J = 128
SUPER = DIL_GROUPS[-1][1] * WIN_J
NEG = -1e30
VMEM_LIMIT = 56 * 1024 * 1024

DIL_UNROLL = 2
TM_IN = 512
TM_POST = 512
TQ_FOX = 1024
FOX_ROUNDING_SLACK = 1.02
FOX_EXP_HEADROOM = 60.0
FOX_DEAD_EXPONENT = -105.0
FOX_MAX_SHIFT = 80.0
TB = 256
GRAN = 16
CAP = TB * TOP_K + N_EXPERTS * GRAN
SLOT_CHUNK = 512
NO_SLOT = 4095
BPG = 16
FT = 256
FT_BIG = 1024

BF16 = jnp.bfloat16
F32 = jnp.float32


def _dot(a, b):
    return jnp.dot(a, b, preferred_element_type=F32)


def _dot_nt(a, b):
    return lax.dot_general(a, b, (((1,), (1,)), ((), ())), preferred_element_type=F32)


def _split3(v):
    hi = v.astype(BF16).astype(F32)
    r = v - hi
    mid = r.astype(BF16).astype(F32)
    lo = (r - mid).astype(BF16).astype(F32)
    return hi, mid, lo


def _inproj_kernel(x_ref, g_ref, wa_ref, wb_ref, wf_ref, bd_ref, tri_ref, gain_a_ref, gain_b_ref,
                   bf_ref, shift_ref, pa0_ref, pa1_ref, pa2_ref, qkb_ref, vb_ref, cb_ref, carry_ref, h_ref, *,
                   tiles_per_seq):
    tm = x_ref.shape[0]
    x = x_ref[...]
    h = x * lax.rsqrt(jnp.mean(x * x, axis=-1, keepdims=True) + EPS) * g_ref[...]
    n_lane_chunks = h_ref.shape[0]
    for c in range(n_lane_chunks):
        h_ref[c] = h[:, c * LANES:(c + 1) * LANES]
    h = h.astype(BF16)
    bd = bd_ref[...]

    def headnorm(p, gain):
        ms = _dot((p * p).astype(BF16), bd)
        return p * lax.rsqrt(ms + EPS) * gain

    for g, (pa_ref, (_, dil)) in enumerate(zip((pa0_ref, pa1_ref, pa2_ref), DIL_GROUPS)):
        rows = tm // dil
        if dil == 1:
            hg = h
        else:
            hg = jnp.concatenate([jnp.concatenate(
                [h_ref[c, pl.ds(r, rows, stride=dil), :] for c in range(n_lane_chunks)], axis=1)
                for r in range(dil)], axis=0).astype(BF16)
        for part in range(3):
            cols = slice(part * GROUP_W, (part + 1) * GROUP_W)
            p = _dot(hg, wa_ref[g, :, cols])
            if part < 2:
                p = headnorm(p, gain_a_ref[part:part + 1, :])
            p = p.astype(BF16)
            if dil == 1:
                pa_ref[:, cols] = p
            else:
                for r in range(dil):
                    pa_ref[0, r, :, cols] = p[r * rows:(r + 1) * rows, :]

    f = _dot(h, wf_ref[...]) + bf_ref[...]
    logf = jnp.minimum(f, 0.0) - jnp.log1p(jnp.exp(-jnp.abs(f)))
    tri = tri_ref[...]
    lh, lm, ll = _split3(logf)
    cum = _dot(tri, lh.astype(BF16)) + _dot(tri, lm.astype(BF16)) + _dot(tri, ll.astype(BF16))

    @pl.when(pl.program_id(0) % tiles_per_seq == 0)
    def _():
        carry_ref[...] = jnp.zeros_like(carry_ref)

    cum = cum + carry_ref[0:1, :]
    carry_ref[0:1, :] = cum[tm - 1:tm, :]
    cb_ref[0] = jnp.concatenate([cum[0:1, :], cum[tm - 1:tm, :], jnp.zeros((6, LANES), F32)], axis=0)
    ch, cm, cl = _split3(cum)

    j = lax.broadcasted_iota(jnp.int32, (tm, HEAD_DIM), 1)

    def ext_cols(vals):
        out = jnp.zeros((tm, HEAD_DIM), F32)
        for pos, val in reversed(list(enumerate(vals))):
            out = jnp.where(j == pos, val, out)
        return out

    for c in range(QK_B // GROUP_W):
        wcols = lambda part: slice(part * QK_B + c * GROUP_W, part * QK_B + (c + 1) * GROUP_W)
        pq = headnorm(_dot(h, wb_ref[:, wcols(0)]), gain_b_ref[0:1, :])
        pk = headnorm(_dot(h, wb_ref[:, wcols(1)]), gain_b_ref[1:2, :])
        pv = _dot(h, wb_ref[:, wcols(2)])
        r = _dot((pq * pk).astype(BF16), bd) * HEAD_DIM + shift_ref[...]
        rh, rm, rl = _split3(r)
        for hh in range(HEADS_PER_GROUP):
            head = c * HEADS_PER_GROUP + hh
            lanes = slice(hh * HEAD_DIM, (hh + 1) * HEAD_DIM)
            col = lambda a, idx: a[:, idx:idx + 1]
            cs = [col(ch, head), col(cm, head), col(cl, head)]
            rs = [col(rh, hh * HEAD_DIM), col(rm, hh * HEAD_DIM), col(rl, hh * HEAD_DIM)]
            ext_q = ext_cols(cs + [1.0] * 3 + [-v for v in rs])
            ext_k = ext_cols([1.0] * 3 + [-v for v in cs] + [1.0] * 3)
            ext_v = ext_cols([1.0])
            for part, (val, ext) in enumerate(((pq, ext_q), (pk, ext_k))):
                o0 = (part * N_HEADS_B + head) * LANES
                qkb_ref[:, o0:o0 + LANES] = jnp.concatenate([val[:, lanes], ext], axis=-1).astype(BF16)
            vb_ref[:, head * LANES:(head + 1) * LANES] = jnp.concatenate(
                [pv[:, lanes], ext_v], axis=-1).astype(BF16)


def _inproj(xf, g_mix, w_in, q_norm_a, k_norm_a, q_norm_b, k_norm_b, b_forget, fox_shift, seq):
    n, d = xf.shape
    tm = TM_IN
    scale = HEAD_DIM ** -0.5
    w_bf = w_in.astype(BF16)
    qkv_w = N_HEADS_A * HEAD_DIM
    wa = jnp.stack([jnp.concatenate(
        [w_bf[:, part * qkv_w + g * GROUP_W: part * qkv_w + (g + 1) * GROUP_W] for part in range(3)],
        axis=1) for g in range(len(DIL_GROUPS))])
    wb = w_bf[:, WIDTH_A:WIDTH_A + WIDTH_B]
    wf = jnp.pad(w_bf[:, WIDTH_A + WIDTH_B:], ((0, 0), (0, LANES - N_HEADS_B)))
    bfp = jnp.pad(b_forget.astype(F32), (0, LANES - N_HEADS_B)).reshape(1, LANES)
    seg = np.arange(GROUP_W) // HEAD_DIM
    bd = jnp.asarray((seg[:, None] == seg[None, :]).astype(np.float32) / HEAD_DIM, BF16)
    tri = jnp.asarray(np.tril(np.ones((tm, tm), np.float32)), BF16)
    gain_a = jnp.stack([jnp.tile(q_norm_a, HEADS_PER_GROUP) * scale, jnp.tile(k_norm_a, HEADS_PER_GROUP)])
    gain_b = jnp.stack([jnp.tile(q_norm_b, HEADS_PER_GROUP) * scale, jnp.tile(k_norm_b, HEADS_PER_GROUP)])
    const = lambda shape: pl.BlockSpec(shape, lambda i: (0,) * len(shape))
    tps = seq // tm
    batch = n // seq
    qkv3 = 3 * GROUP_W
    (_, d1), (_, d2) = DIL_GROUPS[1], DIL_GROUPS[2]
    return pl.pallas_call(
        functools.partial(_inproj_kernel, tiles_per_seq=tps),
        grid=(n // tm,),
        in_specs=[
            pl.BlockSpec((tm, d), lambda i: (i, 0)),
            const((1, d)), const(wa.shape), const(wb.shape), const(wf.shape),
            const(bd.shape), const(tri.shape), const(gain_a.shape), const(gain_b.shape),
            const(bfp.shape), const((1, 1)),
        ],
        out_specs=[
            pl.BlockSpec((tm, qkv3), lambda i: (i, 0)),
            pl.BlockSpec((1, d1, tm // d1, qkv3), lambda i: (i // tps, 0, i % tps, 0)),
            pl.BlockSpec((1, d2, tm // d2, qkv3), lambda i: (i // tps, 0, i % tps, 0)),
            pl.BlockSpec((tm, 2 * N_HEADS_B * LANES), lambda i: (i, 0)),
            pl.BlockSpec((tm, N_HEADS_B * LANES), lambda i: (i, 0)),
            pl.BlockSpec((1, 8, LANES), lambda i: (i, 0, 0)),
        ],
        out_shape=[
            jax.ShapeDtypeStruct((n, qkv3), BF16),
            jax.ShapeDtypeStruct((batch, d1, seq // d1, qkv3), BF16),
            jax.ShapeDtypeStruct((batch, d2, seq // d2, qkv3), BF16),
            jax.ShapeDtypeStruct((n, 2 * N_HEADS_B * LANES), BF16),
            jax.ShapeDtypeStruct((n, N_HEADS_B * LANES), BF16),
            jax.ShapeDtypeStruct((n // tm, 8, LANES), F32),
        ],
        scratch_shapes=[pltpu.VMEM((8, LANES), F32), pltpu.VMEM((d // LANES, tm, LANES), F32)],
        compiler_params=pltpu.CompilerParams(
            dimension_semantics=("arbitrary",), vmem_limit_bytes=VMEM_LIMIT),
        name="inproj",
    )(xf, g_mix.reshape(1, d), wa, wb, wf, bd, tri, gain_a, gain_b, bfp, fox_shift.reshape(1, 1))


def _dilated_kernel(p0_ref, h0_ref, p1_ref, h1_ref, p2_ref, h2_ref, bias_ref, o_ref, acc_ref, lse_ref):
    tq = WIN_J
    first_sb = pl.program_id(1) == 0
    lane_head = lax.broadcasted_iota(jnp.int32, (tq, GROUP_W), 1) // HEAD_DIM
    prev_col = lax.broadcasted_iota(jnp.int32, (tq, 2 * tq), 1) < tq
    qc, kc_, vc_ = (slice(0, GROUP_W), slice(GROUP_W, 2 * GROUP_W), slice(2 * GROUP_W, 3 * GROUP_W))

    def attend(g, q, kp, kc, vp, vc, no_prev):
        kcat = jnp.concatenate([kp, kc], axis=0)
        vcat = jnp.concatenate([vp, vc], axis=0)
        dead = jnp.logical_and(no_prev, prev_col)
        q4 = jnp.concatenate([jnp.where(lane_head == hh, q, jnp.zeros_like(q))
                              for hh in range(HEADS_PER_GROUP)], axis=0)
        s = _dot_nt(q4, kcat) + bias_ref[g].reshape(HEADS_PER_GROUP * tq, 2 * tq)
        s = jnp.where(jnp.concatenate([dead] * HEADS_PER_GROUP, axis=0), NEG, s)
        m = jnp.max(s, axis=-1, keepdims=True)
        p = jnp.exp(s - m)
        l = jnp.sum(p, axis=-1, keepdims=True)
        o4 = _dot(p.astype(BF16), vcat) * (1.0 / l)
        lse4 = m + jnp.log(l)
        acc = o4[0:tq]
        lse = jnp.broadcast_to(lse4[0:tq], (tq, GROUP_W))
        for hh in range(1, HEADS_PER_GROUP):
            sel = lane_head == hh
            acc = jnp.where(sel, o4[hh * tq:(hh + 1) * tq], acc)
            lse = jnp.where(sel, lse4[hh * tq:(hh + 1) * tq], lse)
        return acc, lse

    n_half = GROUP_W // LANES

    def merge(rows, acc, lse):
        for c in range(n_half):
            lanes = slice(c * LANES, (c + 1) * LANES)
            l1 = lse_ref[c, rows, :]
            mx = jnp.maximum(l1, lse[:, lanes])
            w1 = jnp.exp(l1 - mx)
            w2 = jnp.exp(lse[:, lanes] - mx)
            den = w1 + w2
            acc_ref[c, rows, :] = (w1 * acc_ref[c, rows, :] + w2 * acc[:, lanes]) / den
            lse_ref[c, rows, :] = mx + jnp.log(den)

    def pick(first, halo, body):
        return jnp.where(first, halo, body)

    def loop(n, body):
        def trip(i, carry):
            for u in range(DIL_UNROLL):
                body(i * DIL_UNROLL + u, carry)
            return carry
        lax.fori_loop(0, n // DIL_UNROLL, trip, 0)

    def body0(j, carry):
        st = pl.multiple_of(j * tq, tq)
        pst = pl.multiple_of(jnp.maximum(j - 1, 0) * tq, tq)
        cur, prv = pl.ds(st, tq), pl.ds(pst, tq)
        acc, lse = attend(
            0, p0_ref[0, cur, qc],
            pick(j == 0, h0_ref[0, :, kc_], p0_ref[0, prv, kc_]), p0_ref[0, cur, kc_],
            pick(j == 0, h0_ref[0, :, vc_], p0_ref[0, prv, vc_]), p0_ref[0, cur, vc_],
            jnp.logical_and(j == 0, first_sb))
        for c in range(n_half):
            acc_ref[c, cur, :] = acc[:, c * LANES:(c + 1) * LANES]
            lse_ref[c, cur, :] = lse[:, c * LANES:(c + 1) * LANES]
        return carry

    loop(SUPER // tq, body0)

    d1 = DIL_GROUPS[1][1]
    nsub1 = SUPER // d1 // tq
    def body1(t, carry):
        r, ii = t // nsub1, t % nsub1
        st = pl.multiple_of(ii * tq, tq)
        pst = pl.multiple_of(jnp.maximum(ii - 1, 0) * tq, tq)
        cur, prv = pl.ds(st, tq), pl.ds(pst, tq)
        acc, lse = attend(
            1, p1_ref[0, r, cur, qc],
            pick(ii == 0, h1_ref[0, r, :, kc_], p1_ref[0, r, prv, kc_]), p1_ref[0, r, cur, kc_],
            pick(ii == 0, h1_ref[0, r, :, vc_], p1_ref[0, r, prv, vc_]), p1_ref[0, r, cur, vc_],
            jnp.logical_and(ii == 0, first_sb))
        merge(pl.ds(ii * (tq * d1) + r, tq, stride=d1), acc, lse)
        return carry

    loop(d1 * nsub1, body1)

    d2 = DIL_GROUPS[2][1]

    def body2(r, carry):
        acc, lse = attend(2, p2_ref[0, r, :, qc], h2_ref[0, r, :, kc_], p2_ref[0, r, :, kc_],
                          h2_ref[0, r, :, vc_], p2_ref[0, r, :, vc_], first_sb)
        merge(pl.ds(r, tq, stride=d2), acc, lse)
        return carry

    loop(d2, body2)

    for c in range(n_half):
        o_ref[0, :, c * LANES:(c + 1) * LANES] = acc_ref[c].astype(o_ref.dtype)


def _rel_bucket(dist):
    max_exact = REL_BUCKETS // 2
    n = jnp.maximum(dist.astype(F32), 1.0)
    large = max_exact + (jnp.log(n / max_exact) / math.log(REL_MAX_DIST / max_exact)
                         * (REL_BUCKETS - max_exact)).astype(jnp.int32)
    large = jnp.minimum(large, REL_BUCKETS - 1)
    return jnp.where(dist < max_exact, dist, large)


def _toeplitz_bias(rel_bias, g, dil):
    tq = WIN_J
    offs = dil * jnp.arange(WIN_J + 1, dtype=jnp.int32)
    hs = slice(g * HEADS_PER_GROUP, (g + 1) * HEADS_PER_GROUP)
    tab = rel_bias[_rel_bucket(offs)][:, hs].T.astype(F32)
    period = 3 * tq
    neg = lambda w: jnp.full((HEADS_PER_GROUP, w), NEG, F32)
    vec = jnp.concatenate([neg(tq - 1), tab[:, ::-1], neg(period - 2 * tq)], axis=1)
    flat = jnp.broadcast_to(vec[:, None, :], (HEADS_PER_GROUP, tq, period)).reshape(HEADS_PER_GROUP, -1)
    skew = flat[:, :tq * (period - 1)].reshape(HEADS_PER_GROUP, tq, period - 1)
    return skew[:, :, tq - 1:3 * tq - 1]


def _dilated(pa0, pa1, pa2, bias, batch, seq):
    tq = WIN_J
    qkv3 = 3 * GROUP_W
    (_, d1), (_, d2) = DIL_GROUPS[1], DIL_GROUPS[2]
    nsb = seq // SUPER
    p0 = pa0.reshape(batch, seq, qkv3)
    prev_blk = lambda per_sb: (lambda b, s: jnp.maximum(s * per_sb - 1, 0))
    h0i, h1i, h2i = prev_blk(SUPER // tq), prev_blk(SUPER // d1 // tq), prev_blk(SUPER // d2 // tq)
    out = pl.pallas_call(
        _dilated_kernel,
        grid=(batch, nsb),
        in_specs=[
            pl.BlockSpec((1, SUPER, qkv3), lambda b, s: (b, s, 0)),
            pl.BlockSpec((1, tq, qkv3), lambda b, s: (b, h0i(b, s), 0)),
            pl.BlockSpec((1, d1, SUPER // d1, qkv3), lambda b, s: (b, 0, s, 0)),
            pl.BlockSpec((1, d1, tq, qkv3), lambda b, s: (b, 0, h1i(b, s), 0)),
            pl.BlockSpec((1, d2, SUPER // d2, qkv3), lambda b, s: (b, 0, s, 0)),
            pl.BlockSpec((1, d2, tq, qkv3), lambda b, s: (b, 0, h2i(b, s), 0)),
            pl.BlockSpec(bias.shape, lambda b, s: (0, 0, 0, 0)),
        ],
        out_specs=pl.BlockSpec((1, SUPER, GROUP_W), lambda b, s: (b, s, 0)),
        out_shape=jax.ShapeDtypeStruct((batch, seq, GROUP_W), BF16),
        scratch_shapes=[pltpu.VMEM((GROUP_W // LANES, SUPER, LANES), F32)] * 2,
        compiler_params=pltpu.CompilerParams(
            dimension_semantics=("arbitrary", "arbitrary"), vmem_limit_bytes=VMEM_LIMIT),
        name="dilated",
    )(p0, p0, pa1, pa1, pa2, pa2, bias)
    return out.reshape(batch * seq, GROUP_W)


def _fox_kernel(nlive_sm, q_ref, k_ref, v_ref, o_ref, *, online):
    tq = q_ref.shape[1]
    half = tq // 2
    qi = pl.program_id(2)
    step = (pl.program_id(0) * pl.num_programs(1) + pl.program_id(1)) * pl.num_programs(2) + qi
    row = lax.broadcasted_iota(jnp.int32, (half, half), 0)
    col = lax.broadcasted_iota(jnp.int32, (half, half), 1)
    causal = row >= col

    def attend(hh, rows, state, start, nkeys, masked):
        m, acc = state
        lanes = slice(hh * LANES, (hh + 1) * LANES)
        s = _dot_nt(q_ref[0, rows, lanes], k_ref[0, pl.ds(start, nkeys), lanes])
        if masked:
            s = jnp.where(causal, s, NEG)
        if online:
            m_new = jnp.maximum(m, jnp.max(s, axis=-1, keepdims=True))
            acc = acc * jnp.exp(m - m_new)
            s = s - m_new
            m = m_new
        return m, acc + _dot(jnp.exp(s).astype(BF16), v_ref[0, pl.ds(start, nkeys), lanes])

    def full_chunk(ki, states):
        start = pl.multiple_of(ki * tq, tq)
        return tuple(attend(hh, slice(None), states[hh], start, tq, False) for hh in range(2))

    init = (jnp.full((tq, 1), NEG, F32), jnp.zeros((tq, LANES), F32))
    states = lax.fori_loop(qi - nlive_sm[step], qi, full_chunk, (init, init))

    d0 = pl.multiple_of(qi * tq, tq)
    outs = []
    for hh in range(2):
        m, acc = states[hh]
        top, bot = slice(0, half), slice(half, tq)
        s_top = attend(hh, top, (m[top], acc[top]), d0, half, True)
        s_bot = attend(hh, bot, (m[bot], acc[bot]), d0, half, False)
        s_bot = attend(hh, bot, s_bot, d0 + half, half, True)
        a = jnp.concatenate([s_top[1], s_bot[1]], axis=0)
        outs.append(a[:, :HEAD_DIM] / a[:, HEAD_DIM:HEAD_DIM + 1])
    o_ref[0] = jnp.concatenate(outs, axis=-1).astype(o_ref.dtype)


def _fox_live_chunks(cb, top, batch, seq):
    tps = seq // TM_IN
    per = TQ_FOX // TM_IN
    nq = seq // TQ_FOX
    c_first = cb[:, 0, :N_HEADS_B].reshape(batch, tps, N_HEADS_B)[:, ::per]
    c_last = cb[:, 1, :N_HEADS_B].reshape(batch, tps, N_HEADS_B)[:, per - 1::per]
    live = (top + c_first[:, :, None, :] - c_last[:, None, :, :]) >= FOX_DEAD_EXPONENT
    live = live.reshape(batch, nq, nq, N_HEADS_B // 2, 2).any(-1)
    back = jnp.arange(nq)[:, None] - jnp.arange(nq)[None, :]
    reach = jnp.max(jnp.where(jnp.logical_and(live, (back > 0)[None, :, :, None]),
                              back[None, :, :, None], 0), axis=2)
    return reach.transpose(0, 2, 1).reshape(-1).astype(jnp.int32)


def _fox(qkb, vb, cb, top, online, batch, seq):
    tq = TQ_FOX
    pairs = N_HEADS_B // 2
    nq = seq // tq
    qkv = qkb.reshape(batch, seq, 2 * N_HEADS_B * LANES)
    vv = vb.reshape(batch, seq, N_HEADS_B * LANES)
    all_chunks = jnp.tile(jnp.arange(nq, dtype=jnp.int32), batch * pairs)

    def call(is_online, nlive):
        grid_spec = pltpu.PrefetchScalarGridSpec(
            num_scalar_prefetch=1,
            grid=(batch, pairs, nq),
            in_specs=[
                pl.BlockSpec((1, tq, 2 * LANES), lambda b, p, i, n: (b, i, p)),
                pl.BlockSpec((1, seq, 2 * LANES), lambda b, p, i, n: (b, 0, pairs + p)),
                pl.BlockSpec((1, seq, 2 * LANES), lambda b, p, i, n: (b, 0, p)),
            ],
            out_specs=pl.BlockSpec((1, tq, LANES), lambda b, p, i, n: (b, i, p)),
        )
        return pl.pallas_call(
            functools.partial(_fox_kernel, online=is_online),
            grid_spec=grid_spec,
            out_shape=jax.ShapeDtypeStruct((batch, seq, OUT_B), BF16),
            compiler_params=pltpu.CompilerParams(
                dimension_semantics=("arbitrary", "arbitrary", "arbitrary"), vmem_limit_bytes=VMEM_LIMIT),
            name="fox_online" if is_online else "fox",
        )(nlive, qkv, qkv, vv)

    out = lax.cond(online, lambda: call(True, all_chunks),
                   lambda: call(False, _fox_live_chunks(cb, top, batch, seq)))
    return out.reshape(batch * seq, OUT_B)


def _fox_shift(q_norm_b, k_norm_b):
    bound = HEAD_DIM * (HEAD_DIM ** -0.5) * jnp.max(jnp.abs(q_norm_b)) * jnp.max(jnp.abs(k_norm_b))
    shift = jnp.maximum(2.0 * FOX_ROUNDING_SLACK * bound - FOX_EXP_HEADROOM, 0.0).astype(F32)
    top = 2.0 * FOX_ROUNDING_SLACK * bound - shift
    return shift, top.astype(F32), shift > FOX_MAX_SHIFT


def _post_kernel(x_ref, ya_ref, yb_ref, gmix_ref, wg_ref, bg_ref, wpa_ref, wpb_ref, wo_ref,
                 gffn_ref, wr_ref, rb_ref, x1_ref, h2_ref, topi_ref, topw_ref):
    d = x_ref.shape[1]
    x = x_ref[...]
    h = (x * lax.rsqrt(jnp.mean(x * x, axis=-1, keepdims=True) + EPS) * gmix_ref[...]).astype(BF16)
    gates = jax.nn.sigmoid(_dot(h, wg_ref[...]) + bg_ref[...])
    merged = gates[:, :d] * _dot(ya_ref[...], wpa_ref[...]) + gates[:, d:] * _dot(yb_ref[...], wpb_ref[...])
    x1 = x + _dot(merged.astype(BF16), wo_ref[...])
    x1_ref[...] = x1
    h2 = x1 * lax.rsqrt(jnp.mean(x1 * x1, axis=-1, keepdims=True) + EPS) * gffn_ref[...]
    h2_ref[...] = h2.astype(BF16)

    hh, hm, _ = _split3(h2)
    wr = wr_ref[...]
    wh = wr.astype(BF16)
    wl = (wr - wh.astype(F32)).astype(BF16)
    hh, hm = hh.astype(BF16), hm.astype(BF16)
    logits = _dot(hh, wh) + _dot(hm, wh) + _dot(hh, wl)
    scores = jax.nn.sigmoid(logits)
    biased = scores + rb_ref[...]
    lane = lax.broadcasted_iota(jnp.int32, scores.shape, 1)
    chosen = jnp.zeros(scores.shape, jnp.bool_)
    idx, val = [], []
    for _ in range(TOP_K):
        cur = jnp.where(chosen, -jnp.inf, biased)
        mx = jnp.max(cur, axis=-1, keepdims=True)
        first = jnp.min(jnp.where(cur == mx, lane, N_EXPERTS), axis=-1, keepdims=True)
        pick = lane == first
        chosen = jnp.logical_or(chosen, pick)
        idx.append(first)
        val.append(jnp.sum(jnp.where(pick, scores, 0.0), axis=-1, keepdims=True))
    top_s = jnp.concatenate(val, axis=1)
    topi_ref[...] = jnp.concatenate(idx, axis=1)
    topw_ref[...] = top_s / jnp.sum(top_s, axis=-1, keepdims=True) * ROUTE_SCALE


def _post(xf, ya, yb, g_mix, w_gate, b_gate, w_proj_a, w_proj_b, w_out, g_ffn, w_router, router_bias):
    n, d = xf.shape
    tm = TM_POST
    const = lambda shape: pl.BlockSpec(shape, lambda i: (0,) * len(shape))
    row = lambda w: pl.BlockSpec((tm, w), lambda i: (i, 0))
    args = [xf, ya, yb, g_mix.reshape(1, d), w_gate.astype(BF16), b_gate.reshape(1, 2 * d),
            w_proj_a.astype(BF16), w_proj_b.astype(BF16), w_out.astype(BF16), g_ffn.reshape(1, d),
            w_router.astype(F32), router_bias.astype(F32).reshape(1, N_EXPERTS)]
    in_specs = [row(d), row(OUT_A), row(OUT_B)] + [const(a.shape) for a in args[3:]]
    return pl.pallas_call(
        _post_kernel,
        grid=(n // tm,),
        in_specs=in_specs,
        out_specs=[row(d), row(d), row(TOP_K), row(TOP_K)],
        out_shape=[jax.ShapeDtypeStruct((n, d), F32), jax.ShapeDtypeStruct((n, d), BF16),
                   jax.ShapeDtypeStruct((n, TOP_K), jnp.int32), jax.ShapeDtypeStruct((n, TOP_K), F32)],
        compiler_params=pltpu.CompilerParams(
            dimension_semantics=("arbitrary",), vmem_limit_bytes=VMEM_LIMIT),
        name="post",
    )(*args)


def _dispatch_kernel(h2_ref, topi_ref, tri_ref, upper_ref, xs_ref, slots_ref, cnt_ref, off_ref):
    tb = h2_ref.shape[0]
    topi = topi_ref[...]
    lane = lax.broadcasted_iota(jnp.int32, (tb, N_EXPERTS), 1)
    picks = [lane == topi[:, k:k + 1] for k in range(TOP_K)]
    mask = picks[0]
    for pk in picks[1:]:
        mask = jnp.logical_or(mask, pk)
    maskf = jnp.where(mask, 1.0, 0.0)
    rank = _dot(tri_ref[...], maskf.astype(BF16))
    cnt = jnp.sum(maskf, axis=0, keepdims=True)
    gran = jnp.floor((cnt + (GRAN - 1)) * (1.0 / GRAN))
    goff = _dot(jnp.broadcast_to(gran, (8, N_EXPERTS)).astype(BF16), upper_ref[...])[0:1]
    off = goff * GRAN
    slot_te = off + rank
    slots = jnp.concatenate(
        [jnp.sum(jnp.where(pk, slot_te, 0.0), axis=-1, keepdims=True) for pk in picks], axis=1)
    slots_ref[...] = slots.astype(jnp.int32)
    cnt_ref[0] = cnt.astype(jnp.int32)
    off_ref[0] = off.astype(jnp.int32)
    v = jnp.where(mask, slot_te, float(NO_SLOT))
    v_hi = jnp.floor(v * (1.0 / 64.0))
    w = jnp.concatenate([v_hi * 64.0, v - v_hi * 64.0], axis=1).T.astype(BF16)
    end = off + gran * GRAN
    h2 = h2_ref[...]
    for c in range(CAP // SLOT_CHUNK):
        s_e = (lax.broadcasted_iota(jnp.int32, (SLOT_CHUNK, N_EXPERTS), 0) + c * SLOT_CHUNK).astype(F32)
        own = jnp.where(jnp.logical_and(s_e >= off, s_e < end), 1.0, 0.0)
        looked = _dot(jnp.concatenate([own, own], axis=1).astype(BF16), w)
        s_t = (lax.broadcasted_iota(jnp.int32, (SLOT_CHUNK, tb), 0) + c * SLOT_CHUNK).astype(F32)
        onehot = jnp.where(looked == s_t, 1.0, 0.0).astype(BF16)
        xs_ref[0, c * SLOT_CHUNK:(c + 1) * SLOT_CHUNK, :] = _dot(onehot, h2).astype(BF16)


def _dispatch(h2, topi):
    n, d = h2.shape
    nb = n // TB
    tri = jnp.asarray(np.tril(np.ones((TB, TB), np.float32), -1), BF16)
    upper = jnp.asarray(np.triu(np.ones((N_EXPERTS, N_EXPERTS), np.float32), 1), BF16)
    const = lambda shape: pl.BlockSpec(shape, lambda i: (0,) * len(shape))
    meta = pl.BlockSpec((1, 1, N_EXPERTS), lambda i: (i, 0, 0))
    return pl.pallas_call(
        _dispatch_kernel,
        grid=(nb,),
        in_specs=[pl.BlockSpec((TB, d), lambda i: (i, 0)), pl.BlockSpec((TB, TOP_K), lambda i: (i, 0)),
                  const(tri.shape), const(upper.shape)],
        out_specs=[pl.BlockSpec((1, CAP, d), lambda i: (i, 0, 0)),
                   pl.BlockSpec((TB, TOP_K), lambda i: (i, 0)), meta, meta],
        out_shape=[jax.ShapeDtypeStruct((nb, CAP, d), BF16), jax.ShapeDtypeStruct((n, TOP_K), jnp.int32),
                   jax.ShapeDtypeStruct((nb, 1, N_EXPERTS), jnp.int32),
                   jax.ShapeDtypeStruct((nb, 1, N_EXPERTS), jnp.int32)],
        compiler_params=pltpu.CompilerParams(
            dimension_semantics=("arbitrary",), vmem_limit_bytes=VMEM_LIMIT),
        name="dispatch",
    )(h2, topi, tri, upper)


def _ffn_kernel(gstart_sm, glist_sm, xs_hbm, wg_ref, wu_ref, wd_ref, ys_hbm, xbuf, ybuf, sem_in, sem_out):
    ng = pl.num_programs(1)
    step = pl.program_id(0) * ng + pl.program_id(1)
    nsteps = pl.num_programs(0) * ng
    buf = step % 2

    def for_granules(st, fn):
        g0 = gstart_sm[st]
        n = gstart_sm[st + 1] - g0

        def per_granule(j, carry):
            fn(pl.multiple_of(glist_sm[g0 + j], GRAN), pl.multiple_of(j * GRAN, GRAN))
            return carry

        lax.fori_loop(0, n, per_granule, 0)
        return n

    def fetch(b_):
        return lambda src, dst: pltpu.make_async_copy(
            xs_hbm.at[pl.ds(src, GRAN)], xbuf.at[b_, pl.ds(dst, GRAN)], sem_in.at[b_])

    def writeback(b_):
        return lambda src, dst: pltpu.make_async_copy(
            ybuf.at[b_, pl.ds(dst, GRAN)], ys_hbm.at[pl.ds(src, GRAN)], sem_out.at[b_])

    def start(mk):
        return lambda src, dst: mk(src, dst).start()

    def wait(mk):
        return lambda src, dst: mk(src, dst).wait()

    @pl.when(step == 0)
    def _():
        xbuf[...] = jnp.zeros_like(xbuf)
        for_granules(step, start(fetch(0)))

    @pl.when(step + 1 < nsteps)
    def _():
        for_granules(step + 1, start(fetch(1 - buf)))

    ngran = for_granules(step, wait(fetch(buf)))

    @pl.when(step >= 2)
    def _():
        for_granules(step - 2, wait(writeback(buf)))

    wg = wg_ref[0].astype(BF16)
    wu = wu_ref[0].astype(BF16)
    wd = wd_ref[0].astype(BF16)

    def ffn_rows(base, rows):
        x = xbuf[buf, pl.ds(base, rows), :]
        g = _dot(x, wg)
        u = _dot(x, wu)
        mid = (g * jax.nn.sigmoid(g) * u).astype(BF16)
        ybuf[buf, pl.ds(base, rows), :] = _dot(mid, wd).astype(BF16)

    nt = (ngran * GRAN + (FT - 1)) // FT
    big = FT_BIG // FT

    def big_tile(i, carry):
        ffn_rows(pl.multiple_of(i * FT_BIG, FT_BIG), FT_BIG)
        return carry

    lax.fori_loop(0, nt // big, big_tile, 0)
    size = big // 2
    while size >= 1:
        @pl.when((nt & size) != 0)
        def _(size=size):
            ffn_rows(pl.multiple_of((nt & ~(2 * size - 1)) * FT, size * FT), size * FT)
        size //= 2

    for_granules(step, start(writeback(buf)))

    @pl.when(step == nsteps - 1)
    def _():
        for_granules(step, wait(writeback(buf)))

        @pl.when(step >= 1)
        def _():
            for_granules(step - 1, wait(writeback(1 - buf)))


def _granule_list(cnt, off):
    nb = cnt.shape[0]
    seg_n = ((cnt.reshape(nb, N_EXPERTS) + (GRAN - 1)) // GRAN).T.reshape(-1)
    seg_row = (off.reshape(nb, N_EXPERTS) + jnp.arange(nb, dtype=jnp.int32)[:, None] * CAP).T.reshape(-1)
    seg_end = jnp.cumsum(seg_n)
    seg_start = seg_end - seg_n
    gmax = nb * (TB * TOP_K // GRAN + N_EXPERTS)
    base = jnp.repeat(seg_row - seg_start * GRAN, seg_n, total_repeat_length=gmax)
    glist = base + jnp.arange(gmax, dtype=jnp.int32) * GRAN
    gstart = jnp.concatenate([seg_start[::BPG], seg_end[-1:]])
    return gstart.astype(jnp.int32), glist.astype(jnp.int32)


def _ffn(xs, cnt, off, wg, wu, wd):
    rows, d = xs.shape
    ngroups = (rows // CAP) // BPG
    gstart, glist = _granule_list(cnt, off)
    per_expert = lambda shape: pl.BlockSpec((1,) + shape, lambda e, g, c, o: (e, 0, 0))
    grid_spec = pltpu.PrefetchScalarGridSpec(
        num_scalar_prefetch=2,
        grid=(N_EXPERTS, ngroups),
        in_specs=[pl.BlockSpec(memory_space=pl.ANY), per_expert((d, D_EXPERT)), per_expert((d, D_EXPERT)),
                  per_expert((D_EXPERT, d))],
        out_specs=pl.BlockSpec(memory_space=pl.ANY),
        scratch_shapes=[pltpu.VMEM((2, BPG * TB, d), BF16), pltpu.VMEM((2, BPG * TB, d), BF16),
                        pltpu.SemaphoreType.DMA((2,)), pltpu.SemaphoreType.DMA((2,))],
    )
    return pl.pallas_call(
        _ffn_kernel,
        grid_spec=grid_spec,
        out_shape=jax.ShapeDtypeStruct(xs.shape, xs.dtype),
        input_output_aliases={2: 0},
        compiler_params=pltpu.CompilerParams(
            dimension_semantics=("arbitrary", "arbitrary"), vmem_limit_bytes=VMEM_LIMIT),
        name="ffn",
    )(gstart, glist, xs, wg, wu, wd)


def _combine_kernel(x1_ref, h2_ref, ys_ref, slots_ref, topw_ref, wgus_ref, wds_ref, o_ref):
    tb = x1_ref.shape[0]
    gu = _dot(h2_ref[...], wgus_ref[...])
    g, u = gu[:, :D_SHARED], gu[:, D_SHARED:]
    acc = x1_ref[...] + _dot((g * jax.nn.sigmoid(g) * u).astype(BF16), wds_ref[...])
    slots = slots_ref[...].astype(F32)
    topw = topw_ref[...]
    for c in range(CAP // SLOT_CHUNK):
        scol = (lax.broadcasted_iota(jnp.int32, (tb, SLOT_CHUNK), 1) + c * SLOT_CHUNK).astype(F32)
        gate = jnp.zeros((tb, SLOT_CHUNK), F32)
        for k in range(TOP_K):
            gate = gate + jnp.where(scol == slots[:, k:k + 1], topw[:, k:k + 1], 0.0)
        acc = acc + _dot(gate.astype(BF16), ys_ref[0, c * SLOT_CHUNK:(c + 1) * SLOT_CHUNK, :])
    o_ref[...] = acc


def _combine(x1, h2, ys, slots, topw, wgus, wds):
    n, d = x1.shape
    const = lambda shape: pl.BlockSpec(shape, lambda i: (0,) * len(shape))
    row = lambda w: pl.BlockSpec((TB, w), lambda i: (i, 0))
    return pl.pallas_call(
        _combine_kernel,
        grid=(n // TB,),
        in_specs=[row(d), row(d), pl.BlockSpec((1, CAP, d), lambda i: (i, 0, 0)), row(TOP_K), row(TOP_K),
                  const(wgus.shape), const(wds.shape)],
        out_specs=row(d),
        out_shape=jax.ShapeDtypeStruct((n, d), F32),
        compiler_params=pltpu.CompilerParams(
            dimension_semantics=("arbitrary",), vmem_limit_bytes=VMEM_LIMIT),
        name="combine",
    )(x1, h2, ys, slots, topw, wgus, wds)


def _moe(x1, h2, topi, topw, w_gate_e, w_up_e, w_down_e, w_gate_s, w_up_s, w_down_s):
    n, d = x1.shape
    wgus = jnp.concatenate([w_gate_s.astype(BF16), w_up_s.astype(BF16)], axis=-1)
    xs, slots, cnt, off = _dispatch(h2, topi)
    ys = _ffn(xs.reshape(-1, d), cnt, off, w_gate_e, w_up_e, w_down_e)
    return _combine(x1, h2, ys.reshape(n // TB, CAP, d), slots, topw, wgus, w_down_s.astype(BF16))


def kernel(x, g_mix, w_in, q_norm_a, k_norm_a, q_norm_b, k_norm_b, rel_bias, b_forget, w_gate, b_gate,
           w_proj_a, w_proj_b, w_out, g_ffn, w_router, router_bias, w_gate_e, w_up_e, w_down_e,
           w_gate_s, w_up_s, w_down_s):
    batch, seq, d = x.shape
    xf = x.reshape(batch * seq, d)
    fox_shift, fox_top, fox_online = _fox_shift(q_norm_b, k_norm_b)
    pa0, pa1, pa2, qkb, vb, cb = _inproj(xf, g_mix, w_in, q_norm_a, k_norm_a, q_norm_b, k_norm_b, b_forget,
                                     fox_shift, seq)
    bias = jnp.stack([_toeplitz_bias(rel_bias, g, dil) for g, (_, dil) in enumerate(DIL_GROUPS)])
    ya = _dilated(pa0, pa1, pa2, bias, batch, seq)

    yb = _fox(qkb, vb, cb, fox_top, fox_online, batch, seq)
    x1, h2, topi, topw = _post(xf, ya, yb, g_mix, w_gate, b_gate, w_proj_a, w_proj_b, w_out, g_ffn,
                               w_router, router_bias)
    out = _moe(x1, h2, topi, topw, w_gate_e, w_up_e, w_down_e, w_gate_s, w_up_s, w_down_s)
    return out.reshape(batch, seq, d)
```

```python
import functools
import math

import jax
import jax.numpy as jnp
import numpy as np
from jax import lax
from jax.experimental import pallas as pl
from jax.experimental.pallas import tpu as pltpu

D_MODEL = 1024
HEAD_DIM = 64
DIL_GROUPS = ((128, 1), (512, 4), (2048, 16))
HEADS_PER_GROUP = 4
N_HEADS_A = HEADS_PER_GROUP * len(DIL_GROUPS)
N_HEADS_B = 8
REL_BUCKETS = 32
REL_MAX_DIST = 2048
N_EXPERTS = 64
TOP_K = 8
D_EXPERT = 256
D_SHARED = 256
ROUTE_SCALE = 2.5
EPS = 1e-6

WIDTH_A = 3 * N_HEADS_A * HEAD_DIM
WIDTH_B = 3 * N_HEADS_B * HEAD_DIM
QK_B = N_HEADS_B * HEAD_DIM
OUT_A = HEADS_PER_GROUP * HEAD_DIM
OUT_B = N_HEADS_B * HEAD_DIM

LANES = 128
GROUP_W = HEADS_PER_GROUP * HEAD_DIM
WIN_J = 128
SUPER = DIL_GROUPS[-1][1] * WIN_J
NEG = -1e30
VMEM_LIMIT = 56 * 1024 * 1024

DIL_UNROLL = 2
TM_IN = 512
TM_POST = 512
TQ_FOX = 1024
FOX_ROUNDING_SLACK = 1.02
FOX_EXP_HEADROOM = 60.0
FOX_DEAD_EXPONENT = -105.0
FOX_MAX_SHIFT = 80.0
TB = 256
GRAN = 16
CAP = TB * TOP_K + N_EXPERTS * GRAN
SLOT_CHUNK = 512
NO_SLOT = 4095
PASS_GRAN = 256
FT = 256
FT_BIG = 1024

BF16 = jnp.bfloat16
F32 = jnp.float32


def _dot(a, b):
    return jnp.dot(a, b, preferred_element_type=F32)


def _dot_nt(a, b):
    return lax.dot_general(a, b, (((1,), (1,)), ((), ())), preferred_element_type=F32)


def _split3(v):
    hi = v.astype(BF16).astype(F32)
    r = v - hi
    mid = r.astype(BF16).astype(F32)
    lo = (r - mid).astype(BF16).astype(F32)
    return hi, mid, lo


def _inproj_kernel(x_ref, g_ref, wa_ref, wb_ref, wf_ref, bd_ref, tri_ref, gain_a_ref, gain_b_ref,
                   bf_ref, shift_ref, pa0_ref, pa1_ref, pa2_ref, qkb_ref, vb_ref, cb_ref, carry_ref, h_ref, *,
                   tiles_per_seq):
    tm = x_ref.shape[0]
    x = x_ref[...]
    h = x * lax.rsqrt(jnp.mean(x * x, axis=-1, keepdims=True) + EPS) * g_ref[...]
    n_lane_chunks = h_ref.shape[0]
    for c in range(n_lane_chunks):
        h_ref[c] = h[:, c * LANES:(c + 1) * LANES]
    h = h.astype(BF16)
    bd = bd_ref[...]

    def headnorm(p, gain):
        ms = _dot((p * p).astype(BF16), bd)
        return p * lax.rsqrt(ms + EPS) * gain

    for g, (pa_ref, (_, dil)) in enumerate(zip((pa0_ref, pa1_ref, pa2_ref), DIL_GROUPS)):
        rows = tm // dil
        if dil == 1:
            hg = h
        else:
            hg = jnp.concatenate([jnp.concatenate(
                [h_ref[c, pl.ds(r, rows, stride=dil), :] for c in range(n_lane_chunks)], axis=1)
                for r in range(dil)], axis=0).astype(BF16)
        for part in range(3):
            cols = slice(part * GROUP_W, (part + 1) * GROUP_W)
            p = _dot(hg, wa_ref[g, :, cols])
            if part < 2:
                p = headnorm(p, gain_a_ref[part:part + 1, :])
            p = p.astype(BF16)
            if dil == 1:
                pa_ref[:, cols] = p
            else:
                for r in range(dil):
                    pa_ref[0, r, :, cols] = p[r * rows:(r + 1) * rows, :]

    f = _dot(h, wf_ref[...]) + bf_ref[...]
    logf = jnp.minimum(f, 0.0) - jnp.log1p(jnp.exp(-jnp.abs(f)))
    tri = tri_ref[...]
    lh, lm, ll = _split3(logf)
    cum = _dot(tri, lh.astype(BF16)) + _dot(tri, lm.astype(BF16)) + _dot(tri, ll.astype(BF16))

    @pl.when(pl.program_id(0) % tiles_per_seq == 0)
    def _():
        carry_ref[...] = jnp.zeros_like(carry_ref)

    cum = cum + carry_ref[0:1, :]
    carry_ref[0:1, :] = cum[tm - 1:tm, :]
    cb_ref[0] = jnp.concatenate([cum[0:1, :], cum[tm - 1:tm, :], jnp.zeros((6, LANES), F32)], axis=0)
    ch, cm, cl = _split3(cum)

    j = lax.broadcasted_iota(jnp.int32, (tm, HEAD_DIM), 1)

    def ext_cols(vals):
        out = jnp.zeros((tm, HEAD_DIM), F32)
        for pos, val in reversed(list(enumerate(vals))):
            out = jnp.where(j == pos, val, out)
        return out

    for c in range(QK_B // GROUP_W):
        wcols = lambda part: slice(part * QK_B + c * GROUP_W, part * QK_B + (c + 1) * GROUP_W)
        pq = headnorm(_dot(h, wb_ref[:, wcols(0)]), gain_b_ref[0:1, :])
        pk = headnorm(_dot(h, wb_ref[:, wcols(1)]), gain_b_ref[1:2, :])
        pv = _dot(h, wb_ref[:, wcols(2)])
        r = _dot((pq * pk).astype(BF16), bd) * HEAD_DIM + shift_ref[...]
        rh, rm, rl = _split3(r)
        for hh in range(HEADS_PER_GROUP):
            head = c * HEADS_PER_GROUP + hh
            lanes = slice(hh * HEAD_DIM, (hh + 1) * HEAD_DIM)
            col = lambda a, idx: a[:, idx:idx + 1]
            cs = [col(ch, head), col(cm, head), col(cl, head)]
            rs = [col(rh, hh * HEAD_DIM), col(rm, hh * HEAD_DIM), col(rl, hh * HEAD_DIM)]
            ext_q = ext_cols(cs + [1.0] * 3 + [-v for v in rs])
            ext_k = ext_cols([1.0] * 3 + [-v for v in cs] + [1.0] * 3)
            ext_v = ext_cols([1.0])
            for part, (val, ext) in enumerate(((pq, ext_q), (pk, ext_k))):
                o0 = (part * N_HEADS_B + head) * LANES
                qkb_ref[:, o0:o0 + LANES] = jnp.concatenate([val[:, lanes], ext], axis=-1).astype(BF16)
            vb_ref[:, head * LANES:(head + 1) * LANES] = jnp.concatenate(
                [pv[:, lanes], ext_v], axis=-1).astype(BF16)


def _inproj(xf, g_mix, w_in, q_norm_a, k_norm_a, q_norm_b, k_norm_b, b_forget, fox_shift, seq):
    n, d = xf.shape
    tm = TM_IN
    scale = HEAD_DIM ** -0.5
    w_bf = w_in.astype(BF16)
    qkv_w = N_HEADS_A * HEAD_DIM
    wa = jnp.stack([jnp.concatenate(
        [w_bf[:, part * qkv_w + g * GROUP_W: part * qkv_w + (g + 1) * GROUP_W] for part in range(3)],
        axis=1) for g in range(len(DIL_GROUPS))])
    wb = w_bf[:, WIDTH_A:WIDTH_A + WIDTH_B]
    wf = jnp.pad(w_bf[:, WIDTH_A + WIDTH_B:], ((0, 0), (0, LANES - N_HEADS_B)))
    bfp = jnp.pad(b_forget.astype(F32), (0, LANES - N_HEADS_B)).reshape(1, LANES)
    seg = np.arange(GROUP_W) // HEAD_DIM
    bd = jnp.asarray((seg[:, None] == seg[None, :]).astype(np.float32) / HEAD_DIM, BF16)
    tri = jnp.asarray(np.tril(np.ones((tm, tm), np.float32)), BF16)
    gain_a = jnp.stack([jnp.tile(q_norm_a, HEADS_PER_GROUP) * scale, jnp.tile(k_norm_a, HEADS_PER_GROUP)])
    gain_b = jnp.stack([jnp.tile(q_norm_b, HEADS_PER_GROUP) * scale, jnp.tile(k_norm_b, HEADS_PER_GROUP)])
    const = lambda shape: pl.BlockSpec(shape, lambda i: (0,) * len(shape))
    tps = seq // tm
    batch = n // seq
    qkv3 = 3 * GROUP_W
    (_, d1), (_, d2) = DIL_GROUPS[1], DIL_GROUPS[2]
    return pl.pallas_call(
        functools.partial(_inproj_kernel, tiles_per_seq=tps),
        grid=(n // tm,),
        in_specs=[
            pl.BlockSpec((tm, d), lambda i: (i, 0)),
            const((1, d)), const(wa.shape), const(wb.shape), const(wf.shape),
            const(bd.shape), const(tri.shape), const(gain_a.shape), const(gain_b.shape),
            const(bfp.shape), const((1, 1)),
        ],
        out_specs=[
            pl.BlockSpec((tm, qkv3), lambda i: (i, 0)),
            pl.BlockSpec((1, d1, tm // d1, qkv3), lambda i: (i // tps, 0, i % tps, 0)),
            pl.BlockSpec((1, d2, tm // d2, qkv3), lambda i: (i // tps, 0, i % tps, 0)),
            pl.BlockSpec((tm, 2 * N_HEADS_B * LANES), lambda i: (i, 0)),
            pl.BlockSpec((tm, N_HEADS_B * LANES), lambda i: (i, 0)),
            pl.BlockSpec((1, 8, LANES), lambda i: (i, 0, 0)),
        ],
        out_shape=[
            jax.ShapeDtypeStruct((n, qkv3), BF16),
            jax.ShapeDtypeStruct((batch, d1, seq // d1, qkv3), BF16),
            jax.ShapeDtypeStruct((batch, d2, seq // d2, qkv3), BF16),
            jax.ShapeDtypeStruct((n, 2 * N_HEADS_B * LANES), BF16),
            jax.ShapeDtypeStruct((n, N_HEADS_B * LANES), BF16),
            jax.ShapeDtypeStruct((n // tm, 8, LANES), F32),
        ],
        scratch_shapes=[pltpu.VMEM((8, LANES), F32), pltpu.VMEM((d // LANES, tm, LANES), F32)],
        compiler_params=pltpu.CompilerParams(
            dimension_semantics=("arbitrary",), vmem_limit_bytes=VMEM_LIMIT),
        name="inproj",
    )(xf, g_mix.reshape(1, d), wa, wb, wf, bd, tri, gain_a, gain_b, bfp, fox_shift.reshape(1, 1))


def _dilated_kernel(p0_ref, h0_ref, p1_ref, h1_ref, p2_ref, h2_ref, bias_ref, o_ref, acc_ref, lse_ref):
    tq = WIN_J
    first_sb = pl.program_id(1) == 0
    lane_head = lax.broadcasted_iota(jnp.int32, (tq, GROUP_W), 1) // HEAD_DIM
    prev_col = lax.broadcasted_iota(jnp.int32, (tq, 2 * tq), 1) < tq
    qc, kc_, vc_ = (slice(0, GROUP_W), slice(GROUP_W, 2 * GROUP_W), slice(2 * GROUP_W, 3 * GROUP_W))

    def attend(g, q, kp, kc, vp, vc, no_prev):
        kcat = jnp.concatenate([kp, kc], axis=0)
        vcat = jnp.concatenate([vp, vc], axis=0)
        dead = jnp.logical_and(no_prev, prev_col)
        q4 = jnp.concatenate([jnp.where(lane_head == hh, q, jnp.zeros_like(q))
                              for hh in range(HEADS_PER_GROUP)], axis=0)
        s = _dot_nt(q4, kcat) + bias_ref[g].reshape(HEADS_PER_GROUP * tq, 2 * tq)
        s = jnp.where(jnp.concatenate([dead] * HEADS_PER_GROUP, axis=0), NEG, s)
        m = jnp.max(s, axis=-1, keepdims=True)
        p = jnp.exp(s - m)
        l = jnp.sum(p, axis=-1, keepdims=True)
        o4 = _dot(p.astype(BF16), vcat) * (1.0 / l)
        lse4 = m + jnp.log(l)
        acc = o4[0:tq]
        lse = jnp.broadcast_to(lse4[0:tq], (tq, GROUP_W))
        for hh in range(1, HEADS_PER_GROUP):
            sel = lane_head == hh
            acc = jnp.where(sel, o4[hh * tq:(hh + 1) * tq], acc)
            lse = jnp.where(sel, lse4[hh * tq:(hh + 1) * tq], lse)
        return acc, lse

    n_half = GROUP_W // LANES

    def merge(rows, acc, lse):
        for c in range(n_half):
            lanes = slice(c * LANES, (c + 1) * LANES)
            l1 = lse_ref[c, rows, :]
            mx = jnp.maximum(l1, lse[:, lanes])
            w1 = jnp.exp(l1 - mx)
            w2 = jnp.exp(lse[:, lanes] - mx)
            den = w1 + w2
            acc_ref[c, rows, :] = (w1 * acc_ref[c, rows, :] + w2 * acc[:, lanes]) / den
            lse_ref[c, rows, :] = mx + jnp.log(den)

    def pick(first, halo, body):
        return jnp.where(first, halo, body)

    def loop(n, body):
        def trip(i, carry):
            for u in range(DIL_UNROLL):
                body(i * DIL_UNROLL + u, carry)
            return carry
        lax.fori_loop(0, n // DIL_UNROLL, trip, 0)

    def body0(j, carry):
        st = pl.multiple_of(j * tq, tq)
        pst = pl.multiple_of(jnp.maximum(j - 1, 0) * tq, tq)
        cur, prv = pl.ds(st, tq), pl.ds(pst, tq)
        acc, lse = attend(
            0, p0_ref[0, cur, qc],
            pick(j == 0, h0_ref[0, :, kc_], p0_ref[0, prv, kc_]), p0_ref[0, cur, kc_],
            pick(j == 0, h0_ref[0, :, vc_], p0_ref[0, prv, vc_]), p0_ref[0, cur, vc_],
            jnp.logical_and(j == 0, first_sb))
        for c in range(n_half):
            acc_ref[c, cur, :] = acc[:, c * LANES:(c + 1) * LANES]
            lse_ref[c, cur, :] = lse[:, c * LANES:(c + 1) * LANES]
        return carry

    loop(SUPER // tq, body0)

    d1 = DIL_GROUPS[1][1]
    nsub1 = SUPER // d1 // tq
    def body1(t, carry):
        r, ii = t // nsub1, t % nsub1
        st = pl.multiple_of(ii * tq, tq)
        pst = pl.multiple_of(jnp.maximum(ii - 1, 0) * tq, tq)
        cur, prv = pl.ds(st, tq), pl.ds(pst, tq)
        acc, lse = attend(
            1, p1_ref[0, r, cur, qc],
            pick(ii == 0, h1_ref[0, r, :, kc_], p1_ref[0, r, prv, kc_]), p1_ref[0, r, cur, kc_],
            pick(ii == 0, h1_ref[0, r, :, vc_], p1_ref[0, r, prv, vc_]), p1_ref[0, r, cur, vc_],
            jnp.logical_and(ii == 0, first_sb))
        merge(pl.ds(ii * (tq * d1) + r, tq, stride=d1), acc, lse)
        return carry

    loop(d1 * nsub1, body1)

    d2 = DIL_GROUPS[2][1]

    def body2(r, carry):
        acc, lse = attend(2, p2_ref[0, r, :, qc], h2_ref[0, r, :, kc_], p2_ref[0, r, :, kc_],
                          h2_ref[0, r, :, vc_], p2_ref[0, r, :, vc_], first_sb)
        merge(pl.ds(r, tq, stride=d2), acc, lse)
        return carry

    loop(d2, body2)

    for c in range(n_half):
        o_ref[0, :, c * LANES:(c + 1) * LANES] = acc_ref[c].astype(o_ref.dtype)


def _rel_bucket(dist):
    max_exact = REL_BUCKETS // 2
    n = jnp.maximum(dist.astype(F32), 1.0)
    large = max_exact + (jnp.log(n / max_exact) / math.log(REL_MAX_DIST / max_exact)
                         * (REL_BUCKETS - max_exact)).astype(jnp.int32)
    large = jnp.minimum(large, REL_BUCKETS - 1)
    return jnp.where(dist < max_exact, dist, large)


def _toeplitz_bias(rel_bias, g, dil):
    tq = WIN_J
    offs = dil * jnp.arange(WIN_J + 1, dtype=jnp.int32)
    hs = slice(g * HEADS_PER_GROUP, (g + 1) * HEADS_PER_GROUP)
    tab = rel_bias[_rel_bucket(offs)][:, hs].T.astype(F32)
    period = 3 * tq
    neg = lambda w: jnp.full((HEADS_PER_GROUP, w), NEG, F32)
    vec = jnp.concatenate([neg(tq - 1), tab[:, ::-1], neg(period - 2 * tq)], axis=1)
    flat = jnp.broadcast_to(vec[:, None, :], (HEADS_PER_GROUP, tq, period)).reshape(HEADS_PER_GROUP, -1)
    skew = flat[:, :tq * (period - 1)].reshape(HEADS_PER_GROUP, tq, period - 1)
    return skew[:, :, tq - 1:3 * tq - 1]


def _dilated(pa0, pa1, pa2, bias, batch, seq):
    tq = WIN_J
    qkv3 = 3 * GROUP_W
    (_, d1), (_, d2) = DIL_GROUPS[1], DIL_GROUPS[2]
    nsb = seq // SUPER
    p0 = pa0.reshape(batch, seq, qkv3)
    prev_blk = lambda per_sb: (lambda b, s: jnp.maximum(s * per_sb - 1, 0))
    h0i, h1i, h2i = prev_blk(SUPER // tq), prev_blk(SUPER // d1 // tq), prev_blk(SUPER // d2 // tq)
    out = pl.pallas_call(
        _dilated_kernel,
        grid=(batch, nsb),
        in_specs=[
            pl.BlockSpec((1, SUPER, qkv3), lambda b, s: (b, s, 0)),
            pl.BlockSpec((1, tq, qkv3), lambda b, s: (b, h0i(b, s), 0)),
            pl.BlockSpec((1, d1, SUPER // d1, qkv3), lambda b, s: (b, 0, s, 0)),
            pl.BlockSpec((1, d1, tq, qkv3), lambda b, s: (b, 0, h1i(b, s), 0)),
            pl.BlockSpec((1, d2, SUPER // d2, qkv3), lambda b, s: (b, 0, s, 0)),
            pl.BlockSpec((1, d2, tq, qkv3), lambda b, s: (b, 0, h2i(b, s), 0)),
            pl.BlockSpec(bias.shape, lambda b, s: (0, 0, 0, 0)),
        ],
        out_specs=pl.BlockSpec((1, SUPER, GROUP_W), lambda b, s: (b, s, 0)),
        out_shape=jax.ShapeDtypeStruct((batch, seq, GROUP_W), BF16),
        scratch_shapes=[pltpu.VMEM((GROUP_W // LANES, SUPER, LANES), F32)] * 2,
        compiler_params=pltpu.CompilerParams(
            dimension_semantics=("arbitrary", "arbitrary"), vmem_limit_bytes=VMEM_LIMIT),
        name="dilated",
    )(p0, p0, pa1, pa1, pa2, pa2, bias)
    return out.reshape(batch * seq, GROUP_W)


def _fox_kernel(nlive_sm, q_ref, k_ref, v_ref, o_ref, *, online):
    tq = q_ref.shape[1]
    half = tq // 2
    qi = pl.program_id(2)
    step = (pl.program_id(0) * pl.num_programs(1) + pl.program_id(1)) * pl.num_programs(2) + qi
    row = lax.broadcasted_iota(jnp.int32, (half, half), 0)
    col = lax.broadcasted_iota(jnp.int32, (half, half), 1)
    causal = row >= col

    def attend(hh, rows, state, start, nkeys, masked):
        m, acc = state
        lanes = slice(hh * LANES, (hh + 1) * LANES)
        s = _dot_nt(q_ref[0, rows, lanes], k_ref[0, pl.ds(start, nkeys), lanes])
        if masked:
            s = jnp.where(causal, s, NEG)
        if online:
            m_new = jnp.maximum(m, jnp.max(s, axis=-1, keepdims=True))
            acc = acc * jnp.exp(m - m_new)
            s = s - m_new
            m = m_new
        return m, acc + _dot(jnp.exp(s).astype(BF16), v_ref[0, pl.ds(start, nkeys), lanes])

    def full_chunk(ki, states):
        start = pl.multiple_of(ki * tq, tq)
        return tuple(attend(hh, slice(None), states[hh], start, tq, False) for hh in range(2))

    init = (jnp.full((tq, 1), NEG, F32), jnp.zeros((tq, LANES), F32))
    states = lax.fori_loop(qi - nlive_sm[step], qi, full_chunk, (init, init))

    d0 = pl.multiple_of(qi * tq, tq)
    outs = []
    for hh in range(2):
        m, acc = states[hh]
        top, bot = slice(0, half), slice(half, tq)
        s_top = attend(hh, top, (m[top], acc[top]), d0, half, True)
        s_bot = attend(hh, bot, (m[bot], acc[bot]), d0, half, False)
        s_bot = attend(hh, bot, s_bot, d0 + half, half, True)
        a = jnp.concatenate([s_top[1], s_bot[1]], axis=0)
        outs.append(a[:, :HEAD_DIM] / a[:, HEAD_DIM:HEAD_DIM + 1])
    o_ref[0] = jnp.concatenate(outs, axis=-1).astype(o_ref.dtype)


def _fox_live_chunks(cb, top, batch, seq):
    tps = seq // TM_IN
    per = TQ_FOX // TM_IN
    nq = seq // TQ_FOX
    c_first = cb[:, 0, :N_HEADS_B].reshape(batch, tps, N_HEADS_B)[:, ::per]
    c_last = cb[:, 1, :N_HEADS_B].reshape(batch, tps, N_HEADS_B)[:, per - 1::per]
    live = (top + c_first[:, :, None, :] - c_last[:, None, :, :]) >= FOX_DEAD_EXPONENT
    live = live.reshape(batch, nq, nq, N_HEADS_B // 2, 2).any(-1)
    back = jnp.arange(nq)[:, None] - jnp.arange(nq)[None, :]
    reach = jnp.max(jnp.where(jnp.logical_and(live, (back > 0)[None, :, :, None]),
                              back[None, :, :, None], 0), axis=2)
    return reach.transpose(0, 2, 1).reshape(-1).astype(jnp.int32)


def _fox(qkb, vb, cb, top, online, batch, seq):
    tq = TQ_FOX
    pairs = N_HEADS_B // 2
    nq = seq // tq
    qkv = qkb.reshape(batch, seq, 2 * N_HEADS_B * LANES)
    vv = vb.reshape(batch, seq, N_HEADS_B * LANES)
    all_chunks = jnp.tile(jnp.arange(nq, dtype=jnp.int32), batch * pairs)

    def call(is_online, nlive):
        grid_spec = pltpu.PrefetchScalarGridSpec(
            num_scalar_prefetch=1,
            grid=(batch, pairs, nq),
            in_specs=[
                pl.BlockSpec((1, tq, 2 * LANES), lambda b, p, i, n: (b, i, p)),
                pl.BlockSpec((1, seq, 2 * LANES), lambda b, p, i, n: (b, 0, pairs + p)),
                pl.BlockSpec((1, seq, 2 * LANES), lambda b, p, i, n: (b, 0, p)),
            ],
            out_specs=pl.BlockSpec((1, tq, LANES), lambda b, p, i, n: (b, i, p)),
        )
        return pl.pallas_call(
            functools.partial(_fox_kernel, online=is_online),
            grid_spec=grid_spec,
            out_shape=jax.ShapeDtypeStruct((batch, seq, OUT_B), BF16),
            compiler_params=pltpu.CompilerParams(
                dimension_semantics=("arbitrary", "arbitrary", "arbitrary"), vmem_limit_bytes=VMEM_LIMIT),
            name="fox_online" if is_online else "fox",
        )(nlive, qkv, qkv, vv)

    out = lax.cond(online, lambda: call(True, all_chunks),
                   lambda: call(False, _fox_live_chunks(cb, top, batch, seq)))
    return out.reshape(batch * seq, OUT_B)


def _fox_shift(q_norm_b, k_norm_b):
    bound = HEAD_DIM * (HEAD_DIM ** -0.5) * jnp.max(jnp.abs(q_norm_b)) * jnp.max(jnp.abs(k_norm_b))
    shift = jnp.maximum(2.0 * FOX_ROUNDING_SLACK * bound - FOX_EXP_HEADROOM, 0.0).astype(F32)
    top = 2.0 * FOX_ROUNDING_SLACK * bound - shift
    return shift, top.astype(F32), shift > FOX_MAX_SHIFT


def _post_kernel(x_ref, ya_ref, yb_ref, gmix_ref, wg_ref, bg_ref, wpa_ref, wpb_ref, wo_ref,
                 gffn_ref, wr_ref, rb_ref, x1_ref, h2_ref, topi_ref, topw_ref):
    d = x_ref.shape[1]
    x = x_ref[...]
    h = (x * lax.rsqrt(jnp.mean(x * x, axis=-1, keepdims=True) + EPS) * gmix_ref[...]).astype(BF16)
    gates = jax.nn.sigmoid(_dot(h, wg_ref[...]) + bg_ref[...])
    merged = gates[:, :d] * _dot(ya_ref[...], wpa_ref[...]) + gates[:, d:] * _dot(yb_ref[...], wpb_ref[...])
    x1 = x + _dot(merged.astype(BF16), wo_ref[...])
    x1_ref[...] = x1
    h2 = x1 * lax.rsqrt(jnp.mean(x1 * x1, axis=-1, keepdims=True) + EPS) * gffn_ref[...]
    h2_ref[...] = h2.astype(BF16)

    hh, hm, _ = _split3(h2)
    wr = wr_ref[...]
    wh = wr.astype(BF16)
    wl = (wr - wh.astype(F32)).astype(BF16)
    hh, hm = hh.astype(BF16), hm.astype(BF16)
    logits = _dot(hh, wh) + _dot(hm, wh) + _dot(hh, wl)
    scores = jax.nn.sigmoid(logits)
    biased = scores + rb_ref[...]
    lane = lax.broadcasted_iota(jnp.int32, scores.shape, 1)
    chosen = jnp.zeros(scores.shape, jnp.bool_)
    idx, val = [], []
    for _ in range(TOP_K):
        cur = jnp.where(chosen, -jnp.inf, biased)
        mx = jnp.max(cur, axis=-1, keepdims=True)
        first = jnp.min(jnp.where(cur == mx, lane, N_EXPERTS), axis=-1, keepdims=True)
        pick = lane == first
        chosen = jnp.logical_or(chosen, pick)
        idx.append(first)
        val.append(jnp.sum(jnp.where(pick, scores, 0.0), axis=-1, keepdims=True))
    top_s = jnp.concatenate(val, axis=1)
    topi_ref[...] = jnp.concatenate(idx, axis=1)
    topw_ref[...] = top_s / jnp.sum(top_s, axis=-1, keepdims=True) * ROUTE_SCALE


def _post(xf, ya, yb, g_mix, w_gate, b_gate, w_proj_a, w_proj_b, w_out, g_ffn, w_router, router_bias):
    n, d = xf.shape
    tm = TM_POST
    const = lambda shape: pl.BlockSpec(shape, lambda i: (0,) * len(shape))
    row = lambda w: pl.BlockSpec((tm, w), lambda i: (i, 0))
    args = [xf, ya, yb, g_mix.reshape(1, d), w_gate.astype(BF16), b_gate.reshape(1, 2 * d),
            w_proj_a.astype(BF16), w_proj_b.astype(BF16), w_out.astype(BF16), g_ffn.reshape(1, d),
            w_router.astype(F32), router_bias.astype(F32).reshape(1, N_EXPERTS)]
    in_specs = [row(d), row(OUT_A), row(OUT_B)] + [const(a.shape) for a in args[3:]]
    return pl.pallas_call(
        _post_kernel,
        grid=(n // tm,),
        in_specs=in_specs,
        out_specs=[row(d), row(d), row(TOP_K), row(TOP_K)],
        out_shape=[jax.ShapeDtypeStruct((n, d), F32), jax.ShapeDtypeStruct((n, d), BF16),
                   jax.ShapeDtypeStruct((n, TOP_K), jnp.int32), jax.ShapeDtypeStruct((n, TOP_K), F32)],
        compiler_params=pltpu.CompilerParams(
            dimension_semantics=("arbitrary",), vmem_limit_bytes=VMEM_LIMIT),
        name="post",
    )(*args)


def _dispatch_kernel(h2_ref, topi_ref, tri_ref, upper_ref, xs_ref, slots_ref, cnt_ref, off_ref):
    tb = h2_ref.shape[0]
    topi = topi_ref[...]
    lane = lax.broadcasted_iota(jnp.int32, (tb, N_EXPERTS), 1)
    picks = [lane == topi[:, k:k + 1] for k in range(TOP_K)]
    mask = picks[0]
    for pk in picks[1:]:
        mask = jnp.logical_or(mask, pk)
    maskf = jnp.where(mask, 1.0, 0.0)
    rank = _dot(tri_ref[...], maskf.astype(BF16))
    cnt = jnp.sum(maskf, axis=0, keepdims=True)
    gran = jnp.floor((cnt + (GRAN - 1)) * (1.0 / GRAN))
    goff = _dot(jnp.broadcast_to(gran, (8, N_EXPERTS)).astype(BF16), upper_ref[...])[0:1]
    off = goff * GRAN
    slot_te = off + rank
    slots = jnp.concatenate(
        [jnp.sum(jnp.where(pk, slot_te, 0.0), axis=-1, keepdims=True) for pk in picks], axis=1)
    slots_ref[...] = slots.astype(jnp.int32)
    cnt_ref[0] = cnt.astype(jnp.int32)
    off_ref[0] = off.astype(jnp.int32)
    v = jnp.where(mask, slot_te, float(NO_SLOT))
    v_hi = jnp.floor(v * (1.0 / 64.0))
    w = jnp.concatenate([v_hi * 64.0, v - v_hi * 64.0], axis=1).T.astype(BF16)
    end = off + gran * GRAN
    h2 = h2_ref[...]
    for c in range(CAP // SLOT_CHUNK):
        s_e = (lax.broadcasted_iota(jnp.int32, (SLOT_CHUNK, N_EXPERTS), 0) + c * SLOT_CHUNK).astype(F32)
        own = jnp.where(jnp.logical_and(s_e >= off, s_e < end), 1.0, 0.0)
        looked = _dot(jnp.concatenate([own, own], axis=1).astype(BF16), w)
        s_t = (lax.broadcasted_iota(jnp.int32, (SLOT_CHUNK, tb), 0) + c * SLOT_CHUNK).astype(F32)
        onehot = jnp.where(looked == s_t, 1.0, 0.0).astype(BF16)
        xs_ref[0, c * SLOT_CHUNK:(c + 1) * SLOT_CHUNK, :] = _dot(onehot, h2).astype(BF16)


def _dispatch(h2, topi):
    n, d = h2.shape
    nb = n // TB
    tri = jnp.asarray(np.tril(np.ones((TB, TB), np.float32), -1), BF16)
    upper = jnp.asarray(np.triu(np.ones((N_EXPERTS, N_EXPERTS), np.float32), 1), BF16)
    const = lambda shape: pl.BlockSpec(shape, lambda i: (0,) * len(shape))
    meta = pl.BlockSpec((1, 1, N_EXPERTS), lambda i: (i, 0, 0))
    return pl.pallas_call(
        _dispatch_kernel,
        grid=(nb,),
        in_specs=[pl.BlockSpec((TB, d), lambda i: (i, 0)), pl.BlockSpec((TB, TOP_K), lambda i: (i, 0)),
                  const(tri.shape), const(upper.shape)],
        out_specs=[pl.BlockSpec((1, CAP, d), lambda i: (i, 0, 0)),
                   pl.BlockSpec((TB, TOP_K), lambda i: (i, 0)), meta, meta],
        out_shape=[jax.ShapeDtypeStruct((nb, CAP, d), BF16), jax.ShapeDtypeStruct((n, TOP_K), jnp.int32),
                   jax.ShapeDtypeStruct((nb, 1, N_EXPERTS), jnp.int32),
                   jax.ShapeDtypeStruct((nb, 1, N_EXPERTS), jnp.int32)],
        compiler_params=pltpu.CompilerParams(
            dimension_semantics=("arbitrary",), vmem_limit_bytes=VMEM_LIMIT),
        name="dispatch",
    )(h2, topi, tri, upper)


def _ffn_kernel(item_e_sm, item_g0_sm, item_n_sm, glist_sm, xs_hbm, wg_ref, wu_ref, wd_ref, ys_hbm,
                xbuf, ybuf, sem_in, sem_out):
    step = pl.program_id(0)
    nsteps = pl.num_programs(0)
    buf = step % 2

    def for_granules(st, fn):
        g0 = item_g0_sm[st]
        n = item_n_sm[st]

        def per_granule(j, carry):
            fn(glist_sm[g0 + j], j)
            return carry

        lax.fori_loop(0, n, per_granule, 0)
        return n

    def fetch(b_):
        return lambda src, dst: pltpu.make_async_copy(xs_hbm.at[src], xbuf.at[b_, dst], sem_in.at[b_])

    def writeback(b_):
        return lambda src, dst: pltpu.make_async_copy(ybuf.at[b_, dst], ys_hbm.at[src], sem_out.at[b_])

    def start(mk):
        return lambda src, dst: mk(src, dst).start()

    def wait(mk):
        return lambda src, dst: mk(src, dst).wait()

    @pl.when(step == 0)
    def _():
        xbuf[...] = jnp.zeros_like(xbuf)
        for_granules(step, start(fetch(0)))

    @pl.when(step + 1 < nsteps)
    def _():
        for_granules(step + 1, start(fetch(1 - buf)))

    ngran = for_granules(step, wait(fetch(buf)))

    @pl.when(step >= 2)
    def _():
        for_granules(step - 2, wait(writeback(buf)))

    wg = wg_ref[0].astype(BF16)
    wu = wu_ref[0].astype(BF16)
    wd = wd_ref[0].astype(BF16)
    x_cols = xbuf.shape[-1]

    def ffn_rows(base, rows):
        grans = pl.ds(pl.multiple_of(base // GRAN, rows // GRAN), rows // GRAN)
        x = xbuf[buf, grans].reshape(rows, x_cols)
        g = _dot(x, wg)
        u = _dot(x, wu)
        mid = (g * jax.nn.sigmoid(g) * u).astype(BF16)
        ybuf[buf, grans] = _dot(mid, wd).astype(BF16).reshape(rows // GRAN, GRAN, x_cols)

    nt = (ngran * GRAN + (FT - 1)) // FT
    big = FT_BIG // FT

    def big_tile(i, carry):
        ffn_rows(pl.multiple_of(i * FT_BIG, FT_BIG), FT_BIG)
        return carry

    lax.fori_loop(0, nt // big, big_tile, 0)
    size = big // 2
    while size >= 1:
        @pl.when((nt & size) != 0)
        def _(size=size):
            ffn_rows(pl.multiple_of((nt & ~(2 * size - 1)) * FT, size * FT), size * FT)
        size //= 2

    for_granules(step, start(writeback(buf)))

    @pl.when(step == nsteps - 1)
    def _():
        for_granules(step, wait(writeback(buf)))

        @pl.when(step >= 1)
        def _():
            for_granules(step - 1, wait(writeback(1 - buf)))


def _work_items(cnt, off):
    nb = cnt.shape[0]
    seg_n = ((cnt.reshape(nb, N_EXPERTS) + (GRAN - 1)) // GRAN).T.reshape(-1)
    seg_row = ((off.reshape(nb, N_EXPERTS) + jnp.arange(nb, dtype=jnp.int32)[:, None] * CAP) // GRAN).T.reshape(-1)
    seg_end = jnp.cumsum(seg_n)
    seg_start = seg_end - seg_n
    gmax = nb * (TB * TOP_K // GRAN + N_EXPERTS)
    base = jnp.repeat(seg_row - seg_start, seg_n, total_repeat_length=gmax)
    glist = base + jnp.arange(gmax, dtype=jnp.int32)
    per_e = seg_n.reshape(N_EXPERTS, nb).sum(axis=1)
    first_e = seg_start[::nb]
    passes = (per_e + (PASS_GRAN - 1)) // PASS_GRAN
    pass_end = jnp.cumsum(passes)
    n_items = N_EXPERTS + gmax // PASS_GRAN
    w = jnp.arange(n_items, dtype=jnp.int32)
    item_e = jnp.minimum(jnp.searchsorted(pass_end, w, side="right"), N_EXPERTS - 1).astype(jnp.int32)
    done = (w - (pass_end - passes)[item_e]) * PASS_GRAN
    item_n = jnp.clip(per_e[item_e] - done, 0, PASS_GRAN)
    item_g0 = first_e[item_e] + done
    return item_e, item_g0.astype(jnp.int32), item_n.astype(jnp.int32), glist.astype(jnp.int32)


def _ffn(xs, cnt, off, wg, wu, wd):
    _, _, d = xs.shape
    item_e, item_g0, item_n, glist = _work_items(cnt, off)
    per_expert = lambda shape: pl.BlockSpec((1,) + shape, lambda w, ie, g0, n, gl: (ie[w], 0, 0))
    grid_spec = pltpu.PrefetchScalarGridSpec(
        num_scalar_prefetch=4,
        grid=(item_e.shape[0],),
        in_specs=[pl.BlockSpec(memory_space=pl.ANY), per_expert((d, D_EXPERT)), per_expert((d, D_EXPERT)),
                  per_expert((D_EXPERT, d))],
        out_specs=pl.BlockSpec(memory_space=pl.ANY),
        scratch_shapes=[pltpu.VMEM((2, PASS_GRAN, GRAN, d), BF16)] * 2 + [
                        pltpu.SemaphoreType.DMA((2,)), pltpu.SemaphoreType.DMA((2,))],
    )
    return pl.pallas_call(
        _ffn_kernel,
        grid_spec=grid_spec,
        out_shape=jax.ShapeDtypeStruct(xs.shape, xs.dtype),
        input_output_aliases={4: 0},
        compiler_params=pltpu.CompilerParams(
            dimension_semantics=("arbitrary",), vmem_limit_bytes=VMEM_LIMIT),
        name="ffn",
    )(item_e, item_g0, item_n, glist, xs, wg, wu, wd)


def _combine_kernel(x1_ref, h2_ref, ys_ref, slots_ref, topw_ref, wgus_ref, wds_ref, o_ref):
    tb = x1_ref.shape[0]
    gu = _dot(h2_ref[...], wgus_ref[...])
    g, u = gu[:, :D_SHARED], gu[:, D_SHARED:]
    acc = x1_ref[...] + _dot((g * jax.nn.sigmoid(g) * u).astype(BF16), wds_ref[...])
    slots = slots_ref[...].astype(F32)
    topw = topw_ref[...]
    for c in range(CAP // SLOT_CHUNK):
        scol = (lax.broadcasted_iota(jnp.int32, (tb, SLOT_CHUNK), 1) + c * SLOT_CHUNK).astype(F32)
        gate = jnp.zeros((tb, SLOT_CHUNK), F32)
        for k in range(TOP_K):
            gate = gate + jnp.where(scol == slots[:, k:k + 1], topw[:, k:k + 1], 0.0)
        acc = acc + _dot(gate.astype(BF16), ys_ref[0, c * SLOT_CHUNK:(c + 1) * SLOT_CHUNK, :])
    o_ref[...] = acc


def _combine(x1, h2, ys, slots, topw, wgus, wds):
    n, d = x1.shape
    const = lambda shape: pl.BlockSpec(shape, lambda i: (0,) * len(shape))
    row = lambda w: pl.BlockSpec((TB, w), lambda i: (i, 0))
    return pl.pallas_call(
        _combine_kernel,
        grid=(n // TB,),
        in_specs=[row(d), row(d), pl.BlockSpec((1, CAP, d), lambda i: (i, 0, 0)), row(TOP_K), row(TOP_K),
                  const(wgus.shape), const(wds.shape)],
        out_specs=row(d),
        out_shape=jax.ShapeDtypeStruct((n, d), F32),
        compiler_params=pltpu.CompilerParams(
            dimension_semantics=("arbitrary",), vmem_limit_bytes=VMEM_LIMIT),
        name="combine",
    )(x1, h2, ys, slots, topw, wgus, wds)


def _moe(x1, h2, topi, topw, w_gate_e, w_up_e, w_down_e, w_gate_s, w_up_s, w_down_s):
    n, d = x1.shape
    wgus = jnp.concatenate([w_gate_s.astype(BF16), w_up_s.astype(BF16)], axis=-1)
    xs, slots, cnt, off = _dispatch(h2, topi)
    ys = _ffn(xs.reshape(-1, GRAN, d), cnt, off, w_gate_e, w_up_e, w_down_e)
    return _combine(x1, h2, ys.reshape(n // TB, CAP, d), slots, topw, wgus, w_down_s.astype(BF16))


def kernel(x, g_mix, w_in, q_norm_a, k_norm_a, q_norm_b, k_norm_b, rel_bias, b_forget, w_gate, b_gate,
           w_proj_a, w_proj_b, w_out, g_ffn, w_router, router_bias, w_gate_e, w_up_e, w_down_e,
           w_gate_s, w_up_s, w_down_s):
    batch, seq, d = x.shape
    xf = x.reshape(batch * seq, d)
    fox_shift, fox_top, fox_online = _fox_shift(q_norm_b, k_norm_b)
    pa0, pa1, pa2, qkb, vb, cb = _inproj(xf, g_mix, w_in, q_norm_a, k_norm_a, q_norm_b, k_norm_b, b_forget,
                                     fox_shift, seq)
    bias = jnp.stack([_toeplitz_bias(rel_bias, g, dil) for g, (_, dil) in enumerate(DIL_GROUPS)])
    ya = _dilated(pa0, pa1, pa2, bias, batch, seq)

    yb = _fox(qkb, vb, cb, fox_top, fox_online, batch, seq)
    x1, h2, topi, topw = _post(xf, ya, yb, g_mix, w_gate, b_gate, w_proj_a, w_proj_b, w_out, g_ffn,
                               w_router, router_bias)
    out = _moe(x1, h2, topi, topw, w_gate_e, w_up_e, w_down_e, w_gate_s, w_up_s, w_down_s)
    return out.reshape(batch, seq, d)
```

```python
import functools
import math

import jax
import jax.numpy as jnp
import numpy as np
from jax import lax
from jax.experimental import pallas as pl
from jax.experimental.pallas import tpu as pltpu

D_MODEL = 1024
HEAD_DIM = 64
DIL_GROUPS = ((128, 1), (512, 4), (2048, 16))
HEADS_PER_GROUP = 4
N_HEADS_A = HEADS_PER_GROUP * len(DIL_GROUPS)
N_HEADS_B = 8
REL_BUCKETS = 32
REL_MAX_DIST = 2048
N_EXPERTS = 64
TOP_K = 8
D_EXPERT = 256
D_SHARED = 256
ROUTE_SCALE = 2.5
EPS = 1e-6

WIDTH_A = 3 * N_HEADS_A * HEAD_DIM
WIDTH_B = 3 * N_HEADS_B * HEAD_DIM
QK_B = N_HEADS_B * HEAD_DIM
OUT_A = HEADS_PER_GROUP * HEAD_DIM
OUT_B = N_HEADS_B * HEAD_DIM

LANES = 128
GROUP_W = HEADS_PER_GROUP * HEAD_DIM
WIN_J = 128
SUPER = DIL_GROUPS[-1][1] * WIN_J
NEG = -1e30
VMEM_LIMIT = 56 * 1024 * 1024

DIL_UNROLL = 2
TM_IN = 512
TM_POST = 1024
TQ_FOX = 1024
FOX_ROUNDING_SLACK = 1.02
FOX_EXP_HEADROOM = 60.0
FOX_DEAD_EXPONENT = -105.0
FOX_MAX_SHIFT = 80.0
TB = 256
GRAN = 16
CAP = TB * TOP_K + N_EXPERTS * GRAN
SLOT_CHUNK = 512
NO_SLOT = 4095
PASS_GRAN = 256
FT = 256
FT_BIG = 1024

BF16 = jnp.bfloat16
F32 = jnp.float32


def _dot(a, b):
    return jnp.dot(a, b, preferred_element_type=F32)


def _dot_nt(a, b):
    return lax.dot_general(a, b, (((1,), (1,)), ((), ())), preferred_element_type=F32)


def _split3(v):
    hi = v.astype(BF16).astype(F32)
    r = v - hi
    mid = r.astype(BF16).astype(F32)
    lo = (r - mid).astype(BF16).astype(F32)
    return hi, mid, lo


def _inproj_kernel(x_ref, g_ref, wa_ref, wb_ref, wf_ref, bd_ref, tri_ref, gain_a_ref, gain_b_ref,
                   bf_ref, shift_ref, pa0_ref, pa1_ref, pa2_ref, qkb_ref, vb_ref, cb_ref, carry_ref, h_ref, *,
                   tiles_per_seq):
    tm = x_ref.shape[0]
    x = x_ref[...]
    h = x * lax.rsqrt(jnp.mean(x * x, axis=-1, keepdims=True) + EPS) * g_ref[...]
    n_lane_chunks = h_ref.shape[0]
    for c in range(n_lane_chunks):
        h_ref[c] = h[:, c * LANES:(c + 1) * LANES]
    h = h.astype(BF16)
    bd = bd_ref[...]

    def headnorm(p, gain):
        ms = _dot((p * p).astype(BF16), bd)
        return p * lax.rsqrt(ms + EPS) * gain

    for g, (pa_ref, (_, dil)) in enumerate(zip((pa0_ref, pa1_ref, pa2_ref), DIL_GROUPS)):
        rows = tm // dil
        if dil == 1:
            hg = h
        else:
            hg = jnp.concatenate([jnp.concatenate(
                [h_ref[c, pl.ds(r, rows, stride=dil), :] for c in range(n_lane_chunks)], axis=1)
                for r in range(dil)], axis=0).astype(BF16)
        for part in range(3):
            cols = slice(part * GROUP_W, (part + 1) * GROUP_W)
            p = _dot(hg, wa_ref[g, :, cols])
            if part < 2:
                p = headnorm(p, gain_a_ref[part:part + 1, :])
            p = p.astype(BF16)
            if dil == 1:
                pa_ref[:, cols] = p
            else:
                for r in range(dil):
                    pa_ref[0, r, :, cols] = p[r * rows:(r + 1) * rows, :]

    f = _dot(h, wf_ref[...]) + bf_ref[...]
    logf = jnp.minimum(f, 0.0) - jnp.log1p(jnp.exp(-jnp.abs(f)))
    tri = tri_ref[...]
    lh, lm, ll = _split3(logf)
    cum = _dot(tri, lh.astype(BF16)) + _dot(tri, lm.astype(BF16)) + _dot(tri, ll.astype(BF16))

    @pl.when(pl.program_id(0) % tiles_per_seq == 0)
    def _():
        carry_ref[...] = jnp.zeros_like(carry_ref)

    cum = cum + carry_ref[0:1, :]
    carry_ref[0:1, :] = cum[tm - 1:tm, :]
    cb_ref[0] = jnp.concatenate([cum[0:1, :], cum[tm - 1:tm, :], jnp.zeros((6, LANES), F32)], axis=0)
    ch, cm, cl = _split3(cum)

    j = lax.broadcasted_iota(jnp.int32, (tm, HEAD_DIM), 1)

    def ext_cols(vals):
        out = jnp.zeros((tm, HEAD_DIM), F32)
        ones = [pos for pos, val in enumerate(vals) if isinstance(val, float)]
        if ones:
            is_one = functools.reduce(jnp.logical_or, [j == pos for pos in ones])
            out = jnp.where(is_one, 1.0, out)
        for pos, val in enumerate(vals):
            if not isinstance(val, float):
                out = jnp.where(j == pos, val, out)
        return out

    for c in range(QK_B // GROUP_W):
        wcols = lambda part: slice(part * QK_B + c * GROUP_W, part * QK_B + (c + 1) * GROUP_W)
        pq = headnorm(_dot(h, wb_ref[:, wcols(0)]), gain_b_ref[0:1, :])
        pk = headnorm(_dot(h, wb_ref[:, wcols(1)]), gain_b_ref[1:2, :])
        pv = _dot(h, wb_ref[:, wcols(2)])
        r = _dot((pq * pk).astype(BF16), bd) * HEAD_DIM + shift_ref[...]
        rh, rm, rl = _split3(r)
        for hh in range(HEADS_PER_GROUP):
            head = c * HEADS_PER_GROUP + hh
            lanes = slice(hh * HEAD_DIM, (hh + 1) * HEAD_DIM)
            col = lambda a, idx: a[:, idx:idx + 1]
            cs = [col(ch, head), col(cm, head), col(cl, head)]
            rs = [col(rh, hh * HEAD_DIM), col(rm, hh * HEAD_DIM), col(rl, hh * HEAD_DIM)]
            ext_q = ext_cols(cs + [1.0] * 3 + [-v for v in rs])
            ext_k = ext_cols([1.0] * 3 + [-v for v in cs] + [1.0] * 3)
            ext_v = ext_cols([1.0])
            for part, (val, ext) in enumerate(((pq, ext_q), (pk, ext_k))):
                o0 = (part * N_HEADS_B + head) * LANES
                qkb_ref[:, o0:o0 + LANES] = jnp.concatenate([val[:, lanes], ext], axis=-1).astype(BF16)
            vb_ref[:, head * LANES:(head + 1) * LANES] = jnp.concatenate(
                [pv[:, lanes], ext_v], axis=-1).astype(BF16)


def _inproj(xf, g_mix, w_in, q_norm_a, k_norm_a, q_norm_b, k_norm_b, b_forget, fox_shift, seq):
    n, d = xf.shape
    tm = TM_IN
    scale = HEAD_DIM ** -0.5
    w_bf = w_in.astype(BF16)
    qkv_w = N_HEADS_A * HEAD_DIM
    wa = jnp.stack([jnp.concatenate(
        [w_bf[:, part * qkv_w + g * GROUP_W: part * qkv_w + (g + 1) * GROUP_W] for part in range(3)],
        axis=1) for g in range(len(DIL_GROUPS))])
    wb = w_bf[:, WIDTH_A:WIDTH_A + WIDTH_B]
    wf = jnp.pad(w_bf[:, WIDTH_A + WIDTH_B:], ((0, 0), (0, LANES - N_HEADS_B)))
    bfp = jnp.pad(b_forget.astype(F32), (0, LANES - N_HEADS_B)).reshape(1, LANES)
    seg = np.arange(GROUP_W) // HEAD_DIM
    bd = jnp.asarray((seg[:, None] == seg[None, :]).astype(np.float32) / HEAD_DIM, BF16)
    tri = jnp.asarray(np.tril(np.ones((tm, tm), np.float32)), BF16)
    gain_a = jnp.stack([jnp.tile(q_norm_a, HEADS_PER_GROUP) * scale, jnp.tile(k_norm_a, HEADS_PER_GROUP)])
    gain_b = jnp.stack([jnp.tile(q_norm_b, HEADS_PER_GROUP) * scale, jnp.tile(k_norm_b, HEADS_PER_GROUP)])
    const = lambda shape: pl.BlockSpec(shape, lambda i: (0,) * len(shape))
    tps = seq // tm
    batch = n // seq
    qkv3 = 3 * GROUP_W
    (_, d1), (_, d2) = DIL_GROUPS[1], DIL_GROUPS[2]
    return pl.pallas_call(
        functools.partial(_inproj_kernel, tiles_per_seq=tps),
        grid=(n // tm,),
        in_specs=[
            pl.BlockSpec((tm, d), lambda i: (i, 0)),
            const((1, d)), const(wa.shape), const(wb.shape), const(wf.shape),
            const(bd.shape), const(tri.shape), const(gain_a.shape), const(gain_b.shape),
            const(bfp.shape), const((1, 1)),
        ],
        out_specs=[
            pl.BlockSpec((tm, qkv3), lambda i: (i, 0)),
            pl.BlockSpec((1, d1, tm // d1, qkv3), lambda i: (i // tps, 0, i % tps, 0)),
            pl.BlockSpec((1, d2, tm // d2, qkv3), lambda i: (i // tps, 0, i % tps, 0)),
            pl.BlockSpec((tm, 2 * N_HEADS_B * LANES), lambda i: (i, 0)),
            pl.BlockSpec((tm, N_HEADS_B * LANES), lambda i: (i, 0)),
            pl.BlockSpec((1, 8, LANES), lambda i: (i, 0, 0)),
        ],
        out_shape=[
            jax.ShapeDtypeStruct((n, qkv3), BF16),
            jax.ShapeDtypeStruct((batch, d1, seq // d1, qkv3), BF16),
            jax.ShapeDtypeStruct((batch, d2, seq // d2, qkv3), BF16),
            jax.ShapeDtypeStruct((n, 2 * N_HEADS_B * LANES), BF16),
            jax.ShapeDtypeStruct((n, N_HEADS_B * LANES), BF16),
            jax.ShapeDtypeStruct((n // tm, 8, LANES), F32),
        ],
        scratch_shapes=[pltpu.VMEM((8, LANES), F32), pltpu.VMEM((d // LANES, tm, LANES), F32)],
        compiler_params=pltpu.CompilerParams(
            dimension_semantics=("arbitrary",), vmem_limit_bytes=VMEM_LIMIT),
        name="inproj",
    )(xf, g_mix.reshape(1, d), wa, wb, wf, bd, tri, gain_a, gain_b, bfp, fox_shift.reshape(1, 1))


def _dilated_kernel(p0_ref, h0_ref, p1_ref, h1_ref, p2_ref, h2_ref, bias_ref, o_ref, acc_ref, lse_ref):
    tq = WIN_J
    first_sb = pl.program_id(1) == 0
    lane_head = lax.broadcasted_iota(jnp.int32, (tq, GROUP_W), 1) // HEAD_DIM
    prev_col = lax.broadcasted_iota(jnp.int32, (tq, 2 * tq), 1) < tq
    qc, kc_, vc_ = (slice(0, GROUP_W), slice(GROUP_W, 2 * GROUP_W), slice(2 * GROUP_W, 3 * GROUP_W))

    def attend(g, q, kp, kc, vp, vc, no_prev):
        kcat = jnp.concatenate([kp, kc], axis=0)
        vcat = jnp.concatenate([vp, vc], axis=0)
        dead = jnp.logical_and(no_prev, prev_col)
        q4 = jnp.concatenate([jnp.where(lane_head == hh, q, jnp.zeros_like(q))
                              for hh in range(HEADS_PER_GROUP)], axis=0)
        s = _dot_nt(q4, kcat) + bias_ref[g].reshape(HEADS_PER_GROUP * tq, 2 * tq)
        s = jnp.where(jnp.concatenate([dead] * HEADS_PER_GROUP, axis=0), NEG, s)
        m = jnp.max(s, axis=-1, keepdims=True)
        p = jnp.exp(s - m)
        l = jnp.sum(p, axis=-1, keepdims=True)
        o4 = _dot(p.astype(BF16), vcat) * (1.0 / l)
        lse4 = m + jnp.log(l)
        acc = o4[0:tq]
        lse = jnp.broadcast_to(lse4[0:tq], (tq, GROUP_W))
        for hh in range(1, HEADS_PER_GROUP):
            sel = lane_head == hh
            acc = jnp.where(sel, o4[hh * tq:(hh + 1) * tq], acc)
            lse = jnp.where(sel, lse4[hh * tq:(hh + 1) * tq], lse)
        return acc, lse

    n_half = GROUP_W // LANES

    def merge(rows, acc, lse):
        for c in range(n_half):
            lanes = slice(c * LANES, (c + 1) * LANES)
            l1 = lse_ref[c, rows, :]
            mx = jnp.maximum(l1, lse[:, lanes])
            w1 = jnp.exp(l1 - mx)
            w2 = jnp.exp(lse[:, lanes] - mx)
            den = w1 + w2
            acc_ref[c, rows, :] = (w1 * acc_ref[c, rows, :] + w2 * acc[:, lanes]) / den
            lse_ref[c, rows, :] = mx + jnp.log(den)

    def pick(first, halo, body):
        return jnp.where(first, halo, body)

    def loop(n, body):
        def trip(i, carry):
            for u in range(DIL_UNROLL):
                body(i * DIL_UNROLL + u, carry)
            return carry
        lax.fori_loop(0, n // DIL_UNROLL, trip, 0)

    def body0(j, carry):
        st = pl.multiple_of(j * tq, tq)
        pst = pl.multiple_of(jnp.maximum(j - 1, 0) * tq, tq)
        cur, prv = pl.ds(st, tq), pl.ds(pst, tq)
        acc, lse = attend(
            0, p0_ref[0, cur, qc],
            pick(j == 0, h0_ref[0, :, kc_], p0_ref[0, prv, kc_]), p0_ref[0, cur, kc_],
            pick(j == 0, h0_ref[0, :, vc_], p0_ref[0, prv, vc_]), p0_ref[0, cur, vc_],
            jnp.logical_and(j == 0, first_sb))
        for c in range(n_half):
            acc_ref[c, cur, :] = acc[:, c * LANES:(c + 1) * LANES]
            lse_ref[c, cur, :] = lse[:, c * LANES:(c + 1) * LANES]
        return carry

    loop(SUPER // tq, body0)

    d1 = DIL_GROUPS[1][1]
    nsub1 = SUPER // d1 // tq
    def body1(t, carry):
        r, ii = t // nsub1, t % nsub1
        st = pl.multiple_of(ii * tq, tq)
        pst = pl.multiple_of(jnp.maximum(ii - 1, 0) * tq, tq)
        cur, prv = pl.ds(st, tq), pl.ds(pst, tq)
        acc, lse = attend(
            1, p1_ref[0, r, cur, qc],
            pick(ii == 0, h1_ref[0, r, :, kc_], p1_ref[0, r, prv, kc_]), p1_ref[0, r, cur, kc_],
            pick(ii == 0, h1_ref[0, r, :, vc_], p1_ref[0, r, prv, vc_]), p1_ref[0, r, cur, vc_],
            jnp.logical_and(ii == 0, first_sb))
        merge(pl.ds(ii * (tq * d1) + r, tq, stride=d1), acc, lse)
        return carry

    loop(d1 * nsub1, body1)

    d2 = DIL_GROUPS[2][1]

    def body2(r, carry):
        acc, lse = attend(2, p2_ref[0, r, :, qc], h2_ref[0, r, :, kc_], p2_ref[0, r, :, kc_],
                          h2_ref[0, r, :, vc_], p2_ref[0, r, :, vc_], first_sb)
        merge(pl.ds(r, tq, stride=d2), acc, lse)
        return carry

    loop(d2, body2)

    for c in range(n_half):
        o_ref[0, :, c * LANES:(c + 1) * LANES] = acc_ref[c].astype(o_ref.dtype)


def _rel_bucket(dist):
    max_exact = REL_BUCKETS // 2
    n = jnp.maximum(dist.astype(F32), 1.0)
    large = max_exact + (jnp.log(n / max_exact) / math.log(REL_MAX_DIST / max_exact)
                         * (REL_BUCKETS - max_exact)).astype(jnp.int32)
    large = jnp.minimum(large, REL_BUCKETS - 1)
    return jnp.where(dist < max_exact, dist, large)


def _toeplitz_bias(rel_bias, g, dil):
    tq = WIN_J
    offs = dil * jnp.arange(WIN_J + 1, dtype=jnp.int32)
    hs = slice(g * HEADS_PER_GROUP, (g + 1) * HEADS_PER_GROUP)
    tab = rel_bias[_rel_bucket(offs)][:, hs].T.astype(F32)
    period = 3 * tq
    neg = lambda w: jnp.full((HEADS_PER_GROUP, w), NEG, F32)
    vec = jnp.concatenate([neg(tq - 1), tab[:, ::-1], neg(period - 2 * tq)], axis=1)
    flat = jnp.broadcast_to(vec[:, None, :], (HEADS_PER_GROUP, tq, period)).reshape(HEADS_PER_GROUP, -1)
    skew = flat[:, :tq * (period - 1)].reshape(HEADS_PER_GROUP, tq, period - 1)
    return skew[:, :, tq - 1:3 * tq - 1]


def _dilated(pa0, pa1, pa2, bias, batch, seq):
    tq = WIN_J
    qkv3 = 3 * GROUP_W
    (_, d1), (_, d2) = DIL_GROUPS[1], DIL_GROUPS[2]
    nsb = seq // SUPER
    p0 = pa0.reshape(batch, seq, qkv3)
    prev_blk = lambda per_sb: (lambda b, s: jnp.maximum(s * per_sb - 1, 0))
    h0i, h1i, h2i = prev_blk(SUPER // tq), prev_blk(SUPER // d1 // tq), prev_blk(SUPER // d2 // tq)
    out = pl.pallas_call(
        _dilated_kernel,
        grid=(batch, nsb),
        in_specs=[
            pl.BlockSpec((1, SUPER, qkv3), lambda b, s: (b, s, 0)),
            pl.BlockSpec((1, tq, qkv3), lambda b, s: (b, h0i(b, s), 0)),
            pl.BlockSpec((1, d1, SUPER // d1, qkv3), lambda b, s: (b, 0, s, 0)),
            pl.BlockSpec((1, d1, tq, qkv3), lambda b, s: (b, 0, h1i(b, s), 0)),
            pl.BlockSpec((1, d2, SUPER // d2, qkv3), lambda b, s: (b, 0, s, 0)),
            pl.BlockSpec((1, d2, tq, qkv3), lambda b, s: (b, 0, h2i(b, s), 0)),
            pl.BlockSpec(bias.shape, lambda b, s: (0, 0, 0, 0)),
        ],
        out_specs=pl.BlockSpec((1, SUPER, GROUP_W), lambda b, s: (b, s, 0)),
        out_shape=jax.ShapeDtypeStruct((batch, seq, GROUP_W), BF16),
        scratch_shapes=[pltpu.VMEM((GROUP_W // LANES, SUPER, LANES), F32)] * 2,
        compiler_params=pltpu.CompilerParams(
            dimension_semantics=("arbitrary", "arbitrary"), vmem_limit_bytes=VMEM_LIMIT),
        name="dilated",
    )(p0, p0, pa1, pa1, pa2, pa2, bias)
    return out.reshape(batch * seq, GROUP_W)


def _fox_kernel(nlive_sm, q_ref, k_ref, v_ref, o_ref, *, online):
    tq = q_ref.shape[1]
    half = tq // 2
    qi = pl.program_id(2)
    step = (pl.program_id(0) * pl.num_programs(1) + pl.program_id(1)) * pl.num_programs(2) + qi
    row = lax.broadcasted_iota(jnp.int32, (half, half), 0)
    col = lax.broadcasted_iota(jnp.int32, (half, half), 1)
    causal = row >= col

    def attend(hh, rows, state, start, nkeys, masked):
        m, acc = state
        lanes = slice(hh * LANES, (hh + 1) * LANES)
        s = _dot_nt(q_ref[0, rows, lanes], k_ref[0, pl.ds(start, nkeys), lanes])
        if masked:
            s = jnp.where(causal, s, NEG)
        if online:
            m_new = jnp.maximum(m, jnp.max(s, axis=-1, keepdims=True))
            acc = acc * jnp.exp(m - m_new)
            s = s - m_new
            m = m_new
        return m, acc + _dot(jnp.exp(s).astype(BF16), v_ref[0, pl.ds(start, nkeys), lanes])

    def full_chunk(ki, states):
        start = pl.multiple_of(ki * tq, tq)
        return tuple(attend(hh, slice(None), states[hh], start, tq, False) for hh in range(2))

    init = (jnp.full((tq, 1), NEG, F32), jnp.zeros((tq, LANES), F32))
    states = lax.fori_loop(qi - nlive_sm[step], qi, full_chunk, (init, init))

    d0 = pl.multiple_of(qi * tq, tq)
    outs = []
    for hh in range(2):
        m, acc = states[hh]
        top, bot = slice(0, half), slice(half, tq)
        s_top = attend(hh, top, (m[top], acc[top]), d0, half, True)
        s_bot = attend(hh, bot, (m[bot], acc[bot]), d0, half, False)
        s_bot = attend(hh, bot, s_bot, d0 + half, half, True)
        a = jnp.concatenate([s_top[1], s_bot[1]], axis=0)
        outs.append(a[:, :HEAD_DIM] / a[:, HEAD_DIM:HEAD_DIM + 1])
    o_ref[0] = jnp.concatenate(outs, axis=-1).astype(o_ref.dtype)


def _fox_live_chunks(cb, top, batch, seq):
    tps = seq // TM_IN
    per = TQ_FOX // TM_IN
    nq = seq // TQ_FOX
    c_first = cb[:, 0, :N_HEADS_B].reshape(batch, tps, N_HEADS_B)[:, ::per]
    c_last = cb[:, 1, :N_HEADS_B].reshape(batch, tps, N_HEADS_B)[:, per - 1::per]
    live = (top + c_first[:, :, None, :] - c_last[:, None, :, :]) >= FOX_DEAD_EXPONENT
    live = live.reshape(batch, nq, nq, N_HEADS_B // 2, 2).any(-1)
    back = jnp.arange(nq)[:, None] - jnp.arange(nq)[None, :]
    reach = jnp.max(jnp.where(jnp.logical_and(live, (back > 0)[None, :, :, None]),
                              back[None, :, :, None], 0), axis=2)
    return reach.transpose(0, 2, 1).reshape(-1).astype(jnp.int32)


def _fox(qkb, vb, cb, top, online, batch, seq):
    tq = TQ_FOX
    pairs = N_HEADS_B // 2
    nq = seq // tq
    qkv = qkb.reshape(batch, seq, 2 * N_HEADS_B * LANES)
    vv = vb.reshape(batch, seq, N_HEADS_B * LANES)
    all_chunks = jnp.tile(jnp.arange(nq, dtype=jnp.int32), batch * pairs)

    def call(is_online, nlive):
        grid_spec = pltpu.PrefetchScalarGridSpec(
            num_scalar_prefetch=1,
            grid=(batch, pairs, nq),
            in_specs=[
                pl.BlockSpec((1, tq, 2 * LANES), lambda b, p, i, n: (b, i, p)),
                pl.BlockSpec((1, seq, 2 * LANES), lambda b, p, i, n: (b, 0, pairs + p)),
                pl.BlockSpec((1, seq, 2 * LANES), lambda b, p, i, n: (b, 0, p)),
            ],
            out_specs=pl.BlockSpec((1, tq, LANES), lambda b, p, i, n: (b, i, p)),
        )
        return pl.pallas_call(
            functools.partial(_fox_kernel, online=is_online),
            grid_spec=grid_spec,
            out_shape=jax.ShapeDtypeStruct((batch, seq, OUT_B), BF16),
            compiler_params=pltpu.CompilerParams(
                dimension_semantics=("arbitrary", "arbitrary", "arbitrary"), vmem_limit_bytes=VMEM_LIMIT),
            name="fox_online" if is_online else "fox",
        )(nlive, qkv, qkv, vv)

    out = lax.cond(online, lambda: call(True, all_chunks),
                   lambda: call(False, _fox_live_chunks(cb, top, batch, seq)))
    return out.reshape(batch * seq, OUT_B)


def _fox_shift(q_norm_b, k_norm_b):
    bound = HEAD_DIM * (HEAD_DIM ** -0.5) * jnp.max(jnp.abs(q_norm_b)) * jnp.max(jnp.abs(k_norm_b))
    shift = jnp.maximum(2.0 * FOX_ROUNDING_SLACK * bound - FOX_EXP_HEADROOM, 0.0).astype(F32)
    top = 2.0 * FOX_ROUNDING_SLACK * bound - shift
    return shift, top.astype(F32), shift > FOX_MAX_SHIFT


def _post_kernel(x_ref, ya_ref, yb_ref, gmix_ref, wg_ref, bg_ref, wpa_ref, wpb_ref, wo_ref,
                 gffn_ref, wr_ref, rb_ref, x1_ref, h2_ref, topi_ref, topw_ref):
    d = x_ref.shape[1]
    x = x_ref[...]
    h = (x * lax.rsqrt(jnp.mean(x * x, axis=-1, keepdims=True) + EPS) * gmix_ref[...]).astype(BF16)
    gates = jax.nn.sigmoid(_dot(h, wg_ref[...]) + bg_ref[...])
    merged = gates[:, :d] * _dot(ya_ref[...], wpa_ref[...]) + gates[:, d:] * _dot(yb_ref[...], wpb_ref[...])
    x1 = x + _dot(merged.astype(BF16), wo_ref[...])
    x1_ref[...] = x1
    h2 = x1 * lax.rsqrt(jnp.mean(x1 * x1, axis=-1, keepdims=True) + EPS) * gffn_ref[...]
    h2_ref[...] = h2.astype(BF16)

    hh, hm, _ = _split3(h2)
    wr = wr_ref[...]
    wh = wr.astype(BF16)
    wl = (wr - wh.astype(F32)).astype(BF16)
    hh, hm = hh.astype(BF16), hm.astype(BF16)
    logits = _dot_nt(wh, hh) + _dot_nt(wh, hm) + _dot_nt(wl, hh)
    scores = jax.nn.sigmoid(logits)
    biased = scores + rb_ref[...]
    eid = lax.broadcasted_iota(jnp.int32, scores.shape, 0).astype(F32)
    chosen = jnp.zeros(scores.shape, jnp.bool_)
    idx, val = [], []
    for _ in range(TOP_K):
        cur = jnp.where(chosen, -jnp.inf, biased)
        mx = jnp.max(cur, axis=0, keepdims=True)
        first = jnp.min(jnp.where(cur == mx, eid, float(N_EXPERTS)), axis=0, keepdims=True)
        pick = eid == first
        chosen = jnp.logical_or(chosen, pick)
        idx.append(first)
        val.append(jnp.sum(jnp.where(pick, scores, 0.0), axis=0, keepdims=True))
    top_s = jnp.concatenate(val, axis=0)
    top_w = top_s / jnp.sum(top_s, axis=0, keepdims=True) * ROUTE_SCALE
    tm = scores.shape[1]
    both = jnp.concatenate(idx + [top_w, jnp.zeros((LANES - 2 * TOP_K, tm), F32)], axis=0).T
    topi_ref[...] = both[:, :TOP_K].astype(jnp.int32)
    topw_ref[...] = both[:, TOP_K:2 * TOP_K]


def _post(xf, ya, yb, g_mix, w_gate, b_gate, w_proj_a, w_proj_b, w_out, g_ffn, w_router, router_bias):
    n, d = xf.shape
    tm = TM_POST
    const = lambda shape: pl.BlockSpec(shape, lambda i: (0,) * len(shape))
    row = lambda w: pl.BlockSpec((tm, w), lambda i: (i, 0))
    args = [xf, ya, yb, g_mix.reshape(1, d), w_gate.astype(BF16), b_gate.reshape(1, 2 * d),
            w_proj_a.astype(BF16), w_proj_b.astype(BF16), w_out.astype(BF16), g_ffn.reshape(1, d),
            w_router.astype(F32).T, router_bias.astype(F32).reshape(N_EXPERTS, 1)]
    in_specs = [row(d), row(OUT_A), row(OUT_B)] + [const(a.shape) for a in args[3:]]
    return pl.pallas_call(
        _post_kernel,
        grid=(n // tm,),
        in_specs=in_specs,
        out_specs=[row(d), row(d), row(TOP_K), row(TOP_K)],
        out_shape=[jax.ShapeDtypeStruct((n, d), F32), jax.ShapeDtypeStruct((n, d), BF16),
                   jax.ShapeDtypeStruct((n, TOP_K), jnp.int32), jax.ShapeDtypeStruct((n, TOP_K), F32)],
        compiler_params=pltpu.CompilerParams(
            dimension_semantics=("arbitrary",), vmem_limit_bytes=VMEM_LIMIT),
        name="post",
    )(*args)


def _dispatch_kernel(h2_ref, topi_ref, tri_ref, upper_ref, xs_ref, slots_ref, cnt_ref, off_ref):
    tb = h2_ref.shape[0]
    topi = topi_ref[...]
    lane = lax.broadcasted_iota(jnp.int32, (tb, N_EXPERTS), 1)
    picks = [lane == topi[:, k:k + 1] for k in range(TOP_K)]
    mask = picks[0]
    for pk in picks[1:]:
        mask = jnp.logical_or(mask, pk)
    maskf = jnp.where(mask, 1.0, 0.0)
    rank = _dot(tri_ref[...], maskf.astype(BF16))
    cnt = jnp.sum(maskf, axis=0, keepdims=True)
    gran = jnp.floor((cnt + (GRAN - 1)) * (1.0 / GRAN))
    goff = _dot(jnp.broadcast_to(gran, (8, N_EXPERTS)).astype(BF16), upper_ref[...])[0:1]
    off = goff * GRAN
    slot_te = off + rank
    slots = jnp.concatenate(
        [jnp.sum(jnp.where(pk, slot_te, 0.0), axis=-1, keepdims=True) for pk in picks], axis=1)
    slots_ref[...] = slots.astype(jnp.int32)
    cnt_ref[0] = cnt.astype(jnp.int32)
    off_ref[0] = off.astype(jnp.int32)
    v = jnp.where(mask, slot_te, float(NO_SLOT))
    v_hi = jnp.floor(v * (1.0 / 64.0))
    w = jnp.concatenate([v_hi * 64.0, v - v_hi * 64.0], axis=1).T.astype(BF16)
    end = off + gran * GRAN
    h2 = h2_ref[...]
    for c in range(CAP // SLOT_CHUNK):
        s_e = (lax.broadcasted_iota(jnp.int32, (SLOT_CHUNK, N_EXPERTS), 0) + c * SLOT_CHUNK).astype(F32)
        own = jnp.where(jnp.logical_and(s_e >= off, s_e < end), 1.0, 0.0)
        looked = _dot(jnp.concatenate([own, own], axis=1).astype(BF16), w)
        s_t = (lax.broadcasted_iota(jnp.int32, (SLOT_CHUNK, tb), 0) + c * SLOT_CHUNK).astype(F32)
        onehot = jnp.where(looked == s_t, 1.0, 0.0).astype(BF16)
        xs_ref[0, c * SLOT_CHUNK:(c + 1) * SLOT_CHUNK, :] = _dot(onehot, h2).astype(BF16)


def _dispatch(h2, topi):
    n, d = h2.shape
    nb = n // TB
    tri = jnp.asarray(np.tril(np.ones((TB, TB), np.float32), -1), BF16)
    upper = jnp.asarray(np.triu(np.ones((N_EXPERTS, N_EXPERTS), np.float32), 1), BF16)
    const = lambda shape: pl.BlockSpec(shape, lambda i: (0,) * len(shape))
    meta = pl.BlockSpec((1, 1, N_EXPERTS), lambda i: (i, 0, 0))
    return pl.pallas_call(
        _dispatch_kernel,
        grid=(nb,),
        in_specs=[pl.BlockSpec((TB, d), lambda i: (i, 0)), pl.BlockSpec((TB, TOP_K), lambda i: (i, 0)),
                  const(tri.shape), const(upper.shape)],
        out_specs=[pl.BlockSpec((1, CAP, d), lambda i: (i, 0, 0)),
                   pl.BlockSpec((TB, TOP_K), lambda i: (i, 0)), meta, meta],
        out_shape=[jax.ShapeDtypeStruct((nb, CAP, d), BF16), jax.ShapeDtypeStruct((n, TOP_K), jnp.int32),
                   jax.ShapeDtypeStruct((nb, 1, N_EXPERTS), jnp.int32),
                   jax.ShapeDtypeStruct((nb, 1, N_EXPERTS), jnp.int32)],
        compiler_params=pltpu.CompilerParams(
            dimension_semantics=("arbitrary",), vmem_limit_bytes=VMEM_LIMIT),
        name="dispatch",
    )(h2, topi, tri, upper)


def _ffn_kernel(item_e_sm, item_g0_sm, item_n_sm, glist_sm, xs_hbm, wg_ref, wu_ref, wd_ref, ys_hbm,
                xbuf, ybuf, sem_in, sem_out):
    step = pl.program_id(0)
    nsteps = pl.num_programs(0)
    buf = step % 2

    def for_granules(st, fn):
        g0 = item_g0_sm[st]
        n = item_n_sm[st]

        def per_granule(j, carry):
            fn(glist_sm[g0 + j], j)
            return carry

        lax.fori_loop(0, n, per_granule, 0)
        return n

    def fetch(b_):
        return lambda src, dst: pltpu.make_async_copy(xs_hbm.at[src], xbuf.at[b_, dst], sem_in.at[b_])

    def writeback(b_):
        return lambda src, dst: pltpu.make_async_copy(ybuf.at[b_, dst], ys_hbm.at[src], sem_out.at[b_])

    def start(mk):
        return lambda src, dst: mk(src, dst).start()

    def wait_all(st, span):
        n = item_n_sm[st]
        size = PASS_GRAN
        while size >= 1:
            @pl.when((n & size) != 0)
            def _(size=size):
                span(size).wait()
            size //= 2
        return n

    def fetch_span(b_):
        return lambda k: pltpu.make_async_copy(
            xs_hbm.at[pl.ds(0, k)], xbuf.at[b_, pl.ds(0, k)], sem_in.at[b_])

    def writeback_span(b_):
        return lambda k: pltpu.make_async_copy(
            ybuf.at[b_, pl.ds(0, k)], ys_hbm.at[pl.ds(0, k)], sem_out.at[b_])

    @pl.when(step == 0)
    def _():
        xbuf[...] = jnp.zeros_like(xbuf)
        for_granules(step, start(fetch(0)))

    @pl.when(step + 1 < nsteps)
    def _():
        for_granules(step + 1, start(fetch(1 - buf)))

    ngran = wait_all(step, fetch_span(buf))

    @pl.when(step >= 2)
    def _():
        wait_all(step - 2, writeback_span(buf))

    wg = wg_ref[0].astype(BF16)
    wu = wu_ref[0].astype(BF16)
    wd = wd_ref[0].astype(BF16)
    x_cols = xbuf.shape[-1]

    def ffn_rows(base, rows):
        grans = pl.ds(pl.multiple_of(base // GRAN, rows // GRAN), rows // GRAN)
        x = xbuf[buf, grans].reshape(rows, x_cols)
        g = _dot(x, wg)
        u = _dot(x, wu)
        mid = (g * jax.nn.sigmoid(g) * u).astype(BF16)
        ybuf[buf, grans] = _dot(mid, wd).astype(BF16).reshape(rows // GRAN, GRAN, x_cols)

    nt = (ngran * GRAN + (FT - 1)) // FT
    big = FT_BIG // FT

    def big_tile(i, carry):
        ffn_rows(pl.multiple_of(i * FT_BIG, FT_BIG), FT_BIG)
        return carry

    lax.fori_loop(0, nt // big, big_tile, 0)
    size = big // 2
    while size >= 1:
        @pl.when((nt & size) != 0)
        def _(size=size):
            ffn_rows(pl.multiple_of((nt & ~(2 * size - 1)) * FT, size * FT), size * FT)
        size //= 2

    for_granules(step, start(writeback(buf)))

    @pl.when(step == nsteps - 1)
    def _():
        wait_all(step, writeback_span(buf))

        @pl.when(step >= 1)
        def _():
            wait_all(step - 1, writeback_span(1 - buf))


def _work_items(cnt, off):
    nb = cnt.shape[0]
    seg_n = ((cnt.reshape(nb, N_EXPERTS) + (GRAN - 1)) // GRAN).T.reshape(-1)
    seg_row = ((off.reshape(nb, N_EXPERTS) + jnp.arange(nb, dtype=jnp.int32)[:, None] * CAP) // GRAN).T.reshape(-1)
    seg_end = jnp.cumsum(seg_n)
    seg_start = seg_end - seg_n
    gmax = nb * (TB * TOP_K // GRAN + N_EXPERTS)
    base = jnp.repeat(seg_row - seg_start, seg_n, total_repeat_length=gmax)
    glist = base + jnp.arange(gmax, dtype=jnp.int32)
    per_e = seg_n.reshape(N_EXPERTS, nb).sum(axis=1)
    first_e = seg_start[::nb]
    passes = (per_e + (PASS_GRAN - 1)) // PASS_GRAN
    pass_end = jnp.cumsum(passes)
    n_items = N_EXPERTS + gmax // PASS_GRAN
    w = jnp.arange(n_items, dtype=jnp.int32)
    item_e = jnp.minimum(jnp.searchsorted(pass_end, w, side="right"), N_EXPERTS - 1).astype(jnp.int32)
    done = (w - (pass_end - passes)[item_e]) * PASS_GRAN
    item_n = jnp.clip(per_e[item_e] - done, 0, PASS_GRAN)
    item_g0 = first_e[item_e] + done
    return item_e, item_g0.astype(jnp.int32), item_n.astype(jnp.int32), glist.astype(jnp.int32)


def _ffn(xs, cnt, off, wg, wu, wd):
    _, _, d = xs.shape
    item_e, item_g0, item_n, glist = _work_items(cnt, off)
    per_expert = lambda shape: pl.BlockSpec((1,) + shape, lambda w, ie, g0, n, gl: (ie[w], 0, 0))
    grid_spec = pltpu.PrefetchScalarGridSpec(
        num_scalar_prefetch=4,
        grid=(item_e.shape[0],),
        in_specs=[pl.BlockSpec(memory_space=pl.ANY), per_expert((d, D_EXPERT)), per_expert((d, D_EXPERT)),
                  per_expert((D_EXPERT, d))],
        out_specs=pl.BlockSpec(memory_space=pl.ANY),
        scratch_shapes=[pltpu.VMEM((2, PASS_GRAN, GRAN, d), BF16)] * 2 + [
                        pltpu.SemaphoreType.DMA((2,)), pltpu.SemaphoreType.DMA((2,))],
    )
    return pl.pallas_call(
        _ffn_kernel,
        grid_spec=grid_spec,
        out_shape=jax.ShapeDtypeStruct(xs.shape, xs.dtype),
        input_output_aliases={4: 0},
        compiler_params=pltpu.CompilerParams(
            dimension_semantics=("arbitrary",), vmem_limit_bytes=VMEM_LIMIT),
        name="ffn",
    )(item_e, item_g0, item_n, glist, xs, wg, wu, wd)


def _combine_kernel(x1_ref, h2_ref, ys_ref, slots_ref, topw_ref, wgus_ref, wds_ref, o_ref):
    tb = x1_ref.shape[0]
    gu = _dot(h2_ref[...], wgus_ref[...])
    g, u = gu[:, :D_SHARED], gu[:, D_SHARED:]
    acc = x1_ref[...] + _dot((g * jax.nn.sigmoid(g) * u).astype(BF16), wds_ref[...])
    slots = slots_ref[...].astype(F32)
    topw = topw_ref[...]
    for c in range(CAP // SLOT_CHUNK):
        scol = (lax.broadcasted_iota(jnp.int32, (tb, SLOT_CHUNK), 1) + c * SLOT_CHUNK).astype(F32)
        gate = jnp.zeros((tb, SLOT_CHUNK), F32)
        for k in range(TOP_K):
            gate = gate + jnp.where(scol == slots[:, k:k + 1], topw[:, k:k + 1], 0.0)
        acc = acc + _dot(gate.astype(BF16), ys_ref[0, c * SLOT_CHUNK:(c + 1) * SLOT_CHUNK, :])
    o_ref[...] = acc


def _combine(x1, h2, ys, slots, topw, wgus, wds):
    n, d = x1.shape
    const = lambda shape: pl.BlockSpec(shape, lambda i: (0,) * len(shape))
    row = lambda w: pl.BlockSpec((TB, w), lambda i: (i, 0))
    return pl.pallas_call(
        _combine_kernel,
        grid=(n // TB,),
        in_specs=[row(d), row(d), pl.BlockSpec((1, CAP, d), lambda i: (i, 0, 0)), row(TOP_K), row(TOP_K),
                  const(wgus.shape), const(wds.shape)],
        out_specs=row(d),
        out_shape=jax.ShapeDtypeStruct((n, d), F32),
        compiler_params=pltpu.CompilerParams(
            dimension_semantics=("arbitrary",), vmem_limit_bytes=VMEM_LIMIT),
        name="combine",
    )(x1, h2, ys, slots, topw, wgus, wds)


def _moe(x1, h2, topi, topw, w_gate_e, w_up_e, w_down_e, w_gate_s, w_up_s, w_down_s):
    n, d = x1.shape
    wgus = jnp.concatenate([w_gate_s.astype(BF16), w_up_s.astype(BF16)], axis=-1)
    xs, slots, cnt, off = _dispatch(h2, topi)
    ys = _ffn(xs.reshape(-1, GRAN, d), cnt, off, w_gate_e, w_up_e, w_down_e)
    return _combine(x1, h2, ys.reshape(n // TB, CAP, d), slots, topw, wgus, w_down_s.astype(BF16))


def kernel(x, g_mix, w_in, q_norm_a, k_norm_a, q_norm_b, k_norm_b, rel_bias, b_forget, w_gate, b_gate,
           w_proj_a, w_proj_b, w_out, g_ffn, w_router, router_bias, w_gate_e, w_up_e, w_down_e,
           w_gate_s, w_up_s, w_down_s):
    batch, seq, d = x.shape
    xf = x.reshape(batch * seq, d)
    fox_shift, fox_top, fox_online = _fox_shift(q_norm_b, k_norm_b)
    pa0, pa1, pa2, qkb, vb, cb = _inproj(xf, g_mix, w_in, q_norm_a, k_norm_a, q_norm_b, k_norm_b, b_forget,
                                     fox_shift, seq)
    bias = jnp.stack([_toeplitz_bias(rel_bias, g, dil) for g, (_, dil) in enumerate(DIL_GROUPS)])
    ya = _dilated(pa0, pa1, pa2, bias, batch, seq)

    yb = _fox(qkb, vb, cb, fox_top, fox_online, batch, seq)
    x1, h2, topi, topw = _post(xf, ya, yb, g_mix, w_gate, b_gate, w_proj_a, w_proj_b, w_out, g_ffn,
                               w_router, router_bias)
    out = _moe(x1, h2, topi, topw, w_gate_e, w_up_e, w_down_e, w_gate_s, w_up_s, w_down_s)
    return out.reshape(batch, seq, d)
```

```python
import functools
import math

import jax
import jax.numpy as jnp
import numpy as np
from jax import lax
from jax.experimental import pallas as pl
from jax.experimental.pallas import tpu as pltpu

D_MODEL = 1024
HEAD_DIM = 64
DIL_GROUPS = ((128, 1), (512, 4), (2048, 16))
HEADS_PER_GROUP = 4
N_HEADS_A = HEADS_PER_GROUP * len(DIL_GROUPS)
N_HEADS_B = 8
REL_BUCKETS = 32
REL_MAX_DIST = 2048
N_EXPERTS = 64
TOP_K = 8
D_EXPERT = 256
D_SHARED = 256
ROUTE_SCALE = 2.5
EPS = 1e-6

WIDTH_A = 3 * N_HEADS_A * HEAD_DIM
WIDTH_B = 3 * N_HEADS_B * HEAD_DIM
QK_B = N_HEADS_B * HEAD_DIM
OUT_A = HEADS_PER_GROUP * HEAD_DIM
OUT_B = N_HEADS_B * HEAD_DIM

LANES = 128
GROUP_W = HEADS_PER_GROUP * HEAD_DIM
WIN_J = 128
SUPER = DIL_GROUPS[-1][1] * WIN_J
NEG = -1e30
VMEM_LIMIT = 56 * 1024 * 1024

DIL_UNROLL = 4
TM_IN = 512
TM_POST = 1024
TQ_FOX = 1024
FOX_ROUNDING_SLACK = 1.02
FOX_EXP_HEADROOM = 60.0
FOX_DEAD_EXPONENT = -105.0
FOX_MAX_SHIFT = 80.0
TB = 256
GRAN = 16
CAP = TB * TOP_K + N_EXPERTS * GRAN
SLOT_CHUNK = 512
CHUNKS_TYPICAL = -(-(TB * TOP_K + N_EXPERTS * GRAN // 2) // SLOT_CHUNK)
NO_SLOT = 4095
PASS_GRAN = 256
FT = 256
FT_BIG = 1024

BF16 = jnp.bfloat16
F32 = jnp.float32


def _dot(a, b):
    return jnp.dot(a, b, preferred_element_type=F32)


def _dot_nt(a, b):
    return lax.dot_general(a, b, (((1,), (1,)), ((), ())), preferred_element_type=F32)


def _split3(v):
    hi = v.astype(BF16).astype(F32)
    r = v - hi
    mid = r.astype(BF16).astype(F32)
    lo = (r - mid).astype(BF16).astype(F32)
    return hi, mid, lo


def _inproj_kernel(x_ref, g_ref, wa_ref, wb_ref, wf_ref, bd_ref, tri_ref, gain_a_ref, gain_b_ref,
                   bf_ref, shift_ref, pa0_ref, pa1_ref, pa2_ref, qkb_ref, vb_ref, cb_ref, carry_ref, h_ref, *,
                   tiles_per_seq):
    tm = x_ref.shape[0]
    x = x_ref[...]
    h = x * lax.rsqrt(jnp.mean(x * x, axis=-1, keepdims=True) + EPS) * g_ref[...]
    n_lane_chunks = h_ref.shape[0]
    for c in range(n_lane_chunks):
        h_ref[c] = h[:, c * LANES:(c + 1) * LANES]
    h = h.astype(BF16)
    bd = bd_ref[...]

    def headnorm(p, gain):
        ms = _dot((p * p).astype(BF16), bd)
        return p * lax.rsqrt(ms + EPS) * gain

    for g, (pa_ref, (_, dil)) in enumerate(zip((pa0_ref, pa1_ref, pa2_ref), DIL_GROUPS)):
        rows = tm // dil
        if dil == 1:
            hg = h
        else:
            hg = jnp.concatenate([jnp.concatenate(
                [h_ref[c, pl.ds(r, rows, stride=dil), :] for c in range(n_lane_chunks)], axis=1)
                for r in range(dil)], axis=0).astype(BF16)
        for part in range(3):
            cols = slice(part * GROUP_W, (part + 1) * GROUP_W)
            p = _dot(hg, wa_ref[g, :, cols])
            if part < 2:
                p = headnorm(p, gain_a_ref[part:part + 1, :])
            p = p.astype(BF16)
            if dil == 1:
                pa_ref[:, cols] = p
            else:
                for r in range(dil):
                    pa_ref[0, r, :, cols] = p[r * rows:(r + 1) * rows, :]

    f = _dot(h, wf_ref[...]) + bf_ref[...]
    logf = jnp.minimum(f, 0.0) - jnp.log1p(jnp.exp(-jnp.abs(f)))
    tri = tri_ref[...]
    lh, lm, ll = _split3(logf)
    cum = _dot(tri, lh.astype(BF16)) + _dot(tri, lm.astype(BF16)) + _dot(tri, ll.astype(BF16))

    @pl.when(pl.program_id(0) % tiles_per_seq == 0)
    def _():
        carry_ref[...] = jnp.zeros_like(carry_ref)

    cum = cum + carry_ref[0:1, :]
    carry_ref[0:1, :] = cum[tm - 1:tm, :]
    cb_ref[0] = jnp.concatenate([cum[0:1, :], cum[tm - 1:tm, :], jnp.zeros((6, LANES), F32)], axis=0)
    ch, cm, cl = _split3(cum)

    j = lax.broadcasted_iota(jnp.int32, (tm, HEAD_DIM), 1)

    def ext_cols(vals):
        out = jnp.zeros((tm, HEAD_DIM), F32)
        ones = [pos for pos, val in enumerate(vals) if isinstance(val, float)]
        if ones:
            is_one = functools.reduce(jnp.logical_or, [j == pos for pos in ones])
            out = jnp.where(is_one, 1.0, out)
        for pos, val in enumerate(vals):
            if not isinstance(val, float):
                out = jnp.where(j == pos, val, out)
        return out

    for c in range(QK_B // GROUP_W):
        wcols = lambda part: slice(part * QK_B + c * GROUP_W, part * QK_B + (c + 1) * GROUP_W)
        pq = headnorm(_dot(h, wb_ref[:, wcols(0)]), gain_b_ref[0:1, :])
        pk = headnorm(_dot(h, wb_ref[:, wcols(1)]), gain_b_ref[1:2, :])
        pv = _dot(h, wb_ref[:, wcols(2)])
        r = _dot((pq * pk).astype(BF16), bd) * HEAD_DIM + shift_ref[...]
        for hh in range(HEADS_PER_GROUP):
            head = c * HEADS_PER_GROUP + hh
            lanes = slice(hh * HEAD_DIM, (hh + 1) * HEAD_DIM)
            col = lambda a, idx: a[:, idx:idx + 1]
            cs = [col(ch, head), col(cm, head), col(cl, head)]
            ext_q = ext_cols(cs + [1.0] * 3 + [-col(r, hh * HEAD_DIM)])
            ext_k = ext_cols([1.0] * 3 + [-v for v in cs] + [1.0])
            ext_v = ext_cols([1.0])
            for part, (val, ext) in enumerate(((pq, ext_q), (pk, ext_k))):
                o0 = (part * N_HEADS_B + head) * LANES
                qkb_ref[:, o0:o0 + LANES] = jnp.concatenate([val[:, lanes], ext], axis=-1).astype(BF16)
            vb_ref[:, head * LANES:(head + 1) * LANES] = jnp.concatenate(
                [pv[:, lanes], ext_v], axis=-1).astype(BF16)


def _inproj(xf, g_mix, w_in, q_norm_a, k_norm_a, q_norm_b, k_norm_b, b_forget, fox_shift, seq):
    n, d = xf.shape
    tm = TM_IN
    scale = HEAD_DIM ** -0.5
    w_bf = w_in.astype(BF16)
    qkv_w = N_HEADS_A * HEAD_DIM
    wa = jnp.stack([jnp.concatenate(
        [w_bf[:, part * qkv_w + g * GROUP_W: part * qkv_w + (g + 1) * GROUP_W] for part in range(3)],
        axis=1) for g in range(len(DIL_GROUPS))])
    wb = w_bf[:, WIDTH_A:WIDTH_A + WIDTH_B]
    wf = jnp.pad(w_bf[:, WIDTH_A + WIDTH_B:], ((0, 0), (0, LANES - N_HEADS_B)))
    bfp = jnp.pad(b_forget.astype(F32), (0, LANES - N_HEADS_B)).reshape(1, LANES)
    seg = np.arange(GROUP_W) // HEAD_DIM
    bd = jnp.asarray((seg[:, None] == seg[None, :]).astype(np.float32) / HEAD_DIM, BF16)
    tri = jnp.asarray(np.tril(np.ones((tm, tm), np.float32)), BF16)
    gain_a = jnp.stack([jnp.tile(q_norm_a, HEADS_PER_GROUP) * scale, jnp.tile(k_norm_a, HEADS_PER_GROUP)])
    gain_b = jnp.stack([jnp.tile(q_norm_b, HEADS_PER_GROUP) * scale, jnp.tile(k_norm_b, HEADS_PER_GROUP)])
    const = lambda shape: pl.BlockSpec(shape, lambda i: (0,) * len(shape))
    tps = seq // tm
    batch = n // seq
    qkv3 = 3 * GROUP_W
    (_, d1), (_, d2) = DIL_GROUPS[1], DIL_GROUPS[2]
    return pl.pallas_call(
        functools.partial(_inproj_kernel, tiles_per_seq=tps),
        grid=(n // tm,),
        in_specs=[
            pl.BlockSpec((tm, d), lambda i: (i, 0)),
            const((1, d)), const(wa.shape), const(wb.shape), const(wf.shape),
            const(bd.shape), const(tri.shape), const(gain_a.shape), const(gain_b.shape),
            const(bfp.shape), const((1, 1)),
        ],
        out_specs=[
            pl.BlockSpec((tm, qkv3), lambda i: (i, 0)),
            pl.BlockSpec((1, d1, tm // d1, qkv3), lambda i: (i // tps, 0, i % tps, 0)),
            pl.BlockSpec((1, d2, tm // d2, qkv3), lambda i: (i // tps, 0, i % tps, 0)),
            pl.BlockSpec((tm, 2 * N_HEADS_B * LANES), lambda i: (i, 0)),
            pl.BlockSpec((tm, N_HEADS_B * LANES), lambda i: (i, 0)),
            pl.BlockSpec((1, 8, LANES), lambda i: (i, 0, 0)),
        ],
        out_shape=[
            jax.ShapeDtypeStruct((n, qkv3), BF16),
            jax.ShapeDtypeStruct((batch, d1, seq // d1, qkv3), BF16),
            jax.ShapeDtypeStruct((batch, d2, seq // d2, qkv3), BF16),
            jax.ShapeDtypeStruct((n, 2 * N_HEADS_B * LANES), BF16),
            jax.ShapeDtypeStruct((n, N_HEADS_B * LANES), BF16),
            jax.ShapeDtypeStruct((n // tm, 8, LANES), F32),
        ],
        scratch_shapes=[pltpu.VMEM((8, LANES), F32), pltpu.VMEM((d // LANES, tm, LANES), F32)],
        compiler_params=pltpu.CompilerParams(
            dimension_semantics=("arbitrary",), vmem_limit_bytes=VMEM_LIMIT),
        name="inproj",
    )(xf, g_mix.reshape(1, d), wa, wb, wf, bd, tri, gain_a, gain_b, bfp, fox_shift.reshape(1, 1))


def _dilated_kernel(p0_ref, h0_ref, p1_ref, h1_ref, p2_ref, h2_ref, bias_ref, o_ref, acc_ref, lse_ref):
    tq = WIN_J
    first_sb = pl.program_id(1) == 0
    lane_head = lax.broadcasted_iota(jnp.int32, (tq, GROUP_W), 1) // HEAD_DIM
    prev_col = lax.broadcasted_iota(jnp.int32, (tq, 2 * tq), 1) < tq
    qc, kc_, vc_ = (slice(0, GROUP_W), slice(GROUP_W, 2 * GROUP_W), slice(2 * GROUP_W, 3 * GROUP_W))

    def attend(g, q, kp, kc, vp, vc, no_prev):
        kcat = jnp.concatenate([kp, kc], axis=0)
        vcat = jnp.concatenate([vp, vc], axis=0)
        dead = jnp.logical_and(no_prev, prev_col)
        q4 = jnp.concatenate([jnp.where(lane_head == hh, q, jnp.zeros_like(q))
                              for hh in range(HEADS_PER_GROUP)], axis=0)
        s = _dot_nt(q4, kcat) + bias_ref[g].reshape(HEADS_PER_GROUP * tq, 2 * tq)
        s = jnp.where(jnp.concatenate([dead] * HEADS_PER_GROUP, axis=0), NEG, s)
        m = jnp.max(s, axis=-1, keepdims=True)
        p = jnp.exp(s - m)
        l = jnp.sum(p, axis=-1, keepdims=True)
        o4 = _dot(p.astype(BF16), vcat) * (1.0 / l)
        lse4 = m + jnp.log(l)
        acc = o4[0:tq]
        lse = jnp.broadcast_to(lse4[0:tq], (tq, GROUP_W))
        for hh in range(1, HEADS_PER_GROUP):
            sel = lane_head == hh
            acc = jnp.where(sel, o4[hh * tq:(hh + 1) * tq], acc)
            lse = jnp.where(sel, lse4[hh * tq:(hh + 1) * tq], lse)
        return acc, lse

    n_half = GROUP_W // LANES

    def merge(rows, acc, lse):
        for c in range(n_half):
            lanes = slice(c * LANES, (c + 1) * LANES)
            l1 = lse_ref[c, rows, :]
            mx = jnp.maximum(l1, lse[:, lanes])
            w1 = jnp.exp(l1 - mx)
            w2 = jnp.exp(lse[:, lanes] - mx)
            den = w1 + w2
            acc_ref[c, rows, :] = (w1 * acc_ref[c, rows, :] + w2 * acc[:, lanes]) / den
            lse_ref[c, rows, :] = mx + jnp.log(den)

    def pick(first, halo, body):
        return jnp.where(first, halo, body)

    def loop(n, body):
        def trip(i, carry):
            for u in range(DIL_UNROLL):
                body(i * DIL_UNROLL + u, carry)
            return carry
        lax.fori_loop(0, n // DIL_UNROLL, trip, 0)

    def body0(j, carry):
        st = pl.multiple_of(j * tq, tq)
        pst = pl.multiple_of(jnp.maximum(j - 1, 0) * tq, tq)
        cur, prv = pl.ds(st, tq), pl.ds(pst, tq)
        acc, lse = attend(
            0, p0_ref[0, cur, qc],
            pick(j == 0, h0_ref[0, :, kc_], p0_ref[0, prv, kc_]), p0_ref[0, cur, kc_],
            pick(j == 0, h0_ref[0, :, vc_], p0_ref[0, prv, vc_]), p0_ref[0, cur, vc_],
            jnp.logical_and(j == 0, first_sb))
        for c in range(n_half):
            acc_ref[c, cur, :] = acc[:, c * LANES:(c + 1) * LANES]
            lse_ref[c, cur, :] = lse[:, c * LANES:(c + 1) * LANES]
        return carry

    loop(SUPER // tq, body0)

    d1 = DIL_GROUPS[1][1]
    nsub1 = SUPER // d1 // tq
    def body1(t, carry):
        r, ii = t // nsub1, t % nsub1
        st = pl.multiple_of(ii * tq, tq)
        pst = pl.multiple_of(jnp.maximum(ii - 1, 0) * tq, tq)
        cur, prv = pl.ds(st, tq), pl.ds(pst, tq)
        acc, lse = attend(
            1, p1_ref[0, r, cur, qc],
            pick(ii == 0, h1_ref[0, r, :, kc_], p1_ref[0, r, prv, kc_]), p1_ref[0, r, cur, kc_],
            pick(ii == 0, h1_ref[0, r, :, vc_], p1_ref[0, r, prv, vc_]), p1_ref[0, r, cur, vc_],
            jnp.logical_and(ii == 0, first_sb))
        merge(pl.ds(ii * (tq * d1) + r, tq, stride=d1), acc, lse)
        return carry

    loop(d1 * nsub1, body1)

    d2 = DIL_GROUPS[2][1]

    def body2(r, carry):
        acc, lse = attend(2, p2_ref[0, r, :, qc], h2_ref[0, r, :, kc_], p2_ref[0, r, :, kc_],
                          h2_ref[0, r, :, vc_], p2_ref[0, r, :, vc_], first_sb)
        merge(pl.ds(r, tq, stride=d2), acc, lse)
        return carry

    loop(d2, body2)

    for c in range(n_half):
        o_ref[0, :, c * LANES:(c + 1) * LANES] = acc_ref[c].astype(o_ref.dtype)


def _rel_bucket(dist):
    max_exact = REL_BUCKETS // 2
    n = jnp.maximum(dist.astype(F32), 1.0)
    large = max_exact + (jnp.log(n / max_exact) / math.log(REL_MAX_DIST / max_exact)
                         * (REL_BUCKETS - max_exact)).astype(jnp.int32)
    large = jnp.minimum(large, REL_BUCKETS - 1)
    return jnp.where(dist < max_exact, dist, large)


def _toeplitz_bias(rel_bias, g, dil):
    tq = WIN_J
    offs = dil * (WIN_J - jnp.arange(WIN_J + 1, dtype=jnp.int32))
    hs = slice(g * HEADS_PER_GROUP, (g + 1) * HEADS_PER_GROUP)
    tab_rev = rel_bias[_rel_bucket(offs)][:, hs].T.astype(F32)
    period = 3 * tq
    neg = lambda w: jnp.full((HEADS_PER_GROUP, w), NEG, F32)
    vec = jnp.concatenate([neg(tq - 1), tab_rev, neg(period - 2 * tq)], axis=1)
    flat = jnp.broadcast_to(vec[:, None, :], (HEADS_PER_GROUP, tq, period)).reshape(HEADS_PER_GROUP, -1)
    skew = flat[:, :tq * (period - 1)].reshape(HEADS_PER_GROUP, tq, period - 1)
    return skew[:, :, tq - 1:3 * tq - 1]


def _dilated(pa0, pa1, pa2, bias, batch, seq):
    tq = WIN_J
    qkv3 = 3 * GROUP_W
    (_, d1), (_, d2) = DIL_GROUPS[1], DIL_GROUPS[2]
    nsb = seq // SUPER
    p0 = pa0.reshape(batch, seq, qkv3)
    prev_blk = lambda per_sb: (lambda b, s: jnp.maximum(s * per_sb - 1, 0))
    h0i, h1i, h2i = prev_blk(SUPER // tq), prev_blk(SUPER // d1 // tq), prev_blk(SUPER // d2 // tq)
    out = pl.pallas_call(
        _dilated_kernel,
        grid=(batch, nsb),
        in_specs=[
            pl.BlockSpec((1, SUPER, qkv3), lambda b, s: (b, s, 0)),
            pl.BlockSpec((1, tq, qkv3), lambda b, s: (b, h0i(b, s), 0)),
            pl.BlockSpec((1, d1, SUPER // d1, qkv3), lambda b, s: (b, 0, s, 0)),
            pl.BlockSpec((1, d1, tq, qkv3), lambda b, s: (b, 0, h1i(b, s), 0)),
            pl.BlockSpec((1, d2, SUPER // d2, qkv3), lambda b, s: (b, 0, s, 0)),
            pl.BlockSpec((1, d2, tq, qkv3), lambda b, s: (b, 0, h2i(b, s), 0)),
            pl.BlockSpec(bias.shape, lambda b, s: (0, 0, 0, 0)),
        ],
        out_specs=pl.BlockSpec((1, SUPER, GROUP_W), lambda b, s: (b, s, 0)),
        out_shape=jax.ShapeDtypeStruct((batch, seq, GROUP_W), BF16),
        scratch_shapes=[pltpu.VMEM((GROUP_W // LANES, SUPER, LANES), F32)] * 2,
        compiler_params=pltpu.CompilerParams(
            dimension_semantics=("arbitrary", "arbitrary"), vmem_limit_bytes=VMEM_LIMIT),
        name="dilated",
    )(p0, p0, pa1, pa1, pa2, pa2, bias)
    return out.reshape(batch * seq, GROUP_W)


def _fox_kernel(nlive_sm, q_ref, k_ref, v_ref, o_ref, *, online):
    tq = q_ref.shape[1]
    half = tq // 2
    qi = pl.program_id(2)
    step = (pl.program_id(0) * pl.num_programs(1) + pl.program_id(1)) * pl.num_programs(2) + qi
    row = lax.broadcasted_iota(jnp.int32, (half, half), 0)
    col = lax.broadcasted_iota(jnp.int32, (half, half), 1)
    causal = row >= col

    def attend(hh, rows, state, start, nkeys, masked):
        m, acc = state
        lanes = slice(hh * LANES, (hh + 1) * LANES)
        s = _dot_nt(q_ref[0, rows, lanes], k_ref[0, pl.ds(start, nkeys), lanes])
        if masked:
            s = jnp.where(causal, s, NEG)
        if online:
            m_new = jnp.maximum(m, jnp.max(s, axis=-1, keepdims=True))
            acc = acc * jnp.exp(m - m_new)
            s = s - m_new
            m = m_new
        return m, acc + _dot(jnp.exp(s).astype(BF16), v_ref[0, pl.ds(start, nkeys), lanes])

    def full_chunk(ki, states):
        start = pl.multiple_of(ki * tq, tq)
        return tuple(attend(hh, slice(None), states[hh], start, tq, False) for hh in range(2))

    init = (jnp.full((tq, 1), NEG, F32), jnp.zeros((tq, LANES), F32))
    states = lax.fori_loop(qi - nlive_sm[step], qi, full_chunk, (init, init))

    d0 = pl.multiple_of(qi * tq, tq)
    outs = []
    for hh in range(2):
        m, acc = states[hh]
        top, bot = slice(0, half), slice(half, tq)
        s_top = attend(hh, top, (m[top], acc[top]), d0, half, True)
        s_bot = attend(hh, bot, (m[bot], acc[bot]), d0, half, False)
        s_bot = attend(hh, bot, s_bot, d0 + half, half, True)
        a = jnp.concatenate([s_top[1], s_bot[1]], axis=0)
        outs.append(a[:, :HEAD_DIM] / a[:, HEAD_DIM:HEAD_DIM + 1])
    o_ref[0] = jnp.concatenate(outs, axis=-1).astype(o_ref.dtype)


def _fox_live_chunks(cb, top, batch, seq):
    tps = seq // TM_IN
    per = TQ_FOX // TM_IN
    nq = seq // TQ_FOX
    c_first = cb[:, 0, :N_HEADS_B].reshape(batch, tps, N_HEADS_B)[:, ::per]
    c_last = cb[:, 1, :N_HEADS_B].reshape(batch, tps, N_HEADS_B)[:, per - 1::per]
    live = (top + c_first[:, :, None, :] - c_last[:, None, :, :]) >= FOX_DEAD_EXPONENT
    live = live.reshape(batch, nq, nq, N_HEADS_B // 2, 2).any(-1)
    back = jnp.arange(nq)[:, None] - jnp.arange(nq)[None, :]
    reach = jnp.max(jnp.where(jnp.logical_and(live, (back > 0)[None, :, :, None]),
                              back[None, :, :, None], 0), axis=2)
    return reach.transpose(0, 2, 1).reshape(-1).astype(jnp.int32)


def _fox(qkb, vb, cb, top, online, batch, seq):
    tq = TQ_FOX
    pairs = N_HEADS_B // 2
    nq = seq // tq
    qkv = qkb.reshape(batch, seq, 2 * N_HEADS_B * LANES)
    vv = vb.reshape(batch, seq, N_HEADS_B * LANES)
    all_chunks = jnp.tile(jnp.arange(nq, dtype=jnp.int32), batch * pairs)

    def call(is_online, nlive):
        grid_spec = pltpu.PrefetchScalarGridSpec(
            num_scalar_prefetch=1,
            grid=(batch, pairs, nq),
            in_specs=[
                pl.BlockSpec((1, tq, 2 * LANES), lambda b, p, i, n: (b, i, p)),
                pl.BlockSpec((1, seq, 2 * LANES), lambda b, p, i, n: (b, 0, pairs + p)),
                pl.BlockSpec((1, seq, 2 * LANES), lambda b, p, i, n: (b, 0, p)),
            ],
            out_specs=pl.BlockSpec((1, tq, LANES), lambda b, p, i, n: (b, i, p)),
        )
        return pl.pallas_call(
            functools.partial(_fox_kernel, online=is_online),
            grid_spec=grid_spec,
            out_shape=jax.ShapeDtypeStruct((batch, seq, OUT_B), BF16),
            compiler_params=pltpu.CompilerParams(
                dimension_semantics=("arbitrary", "arbitrary", "arbitrary"), vmem_limit_bytes=VMEM_LIMIT),
            name="fox_online" if is_online else "fox",
        )(nlive, qkv, qkv, vv)

    out = lax.cond(online, lambda: call(True, all_chunks),
                   lambda: call(False, _fox_live_chunks(cb, top, batch, seq)))
    return out.reshape(batch * seq, OUT_B)


def _fox_shift(q_norm_b, k_norm_b):
    bound = HEAD_DIM * (HEAD_DIM ** -0.5) * jnp.max(jnp.abs(q_norm_b)) * jnp.max(jnp.abs(k_norm_b))
    shift = jnp.maximum(2.0 * FOX_ROUNDING_SLACK * bound - FOX_EXP_HEADROOM, 0.0).astype(F32)
    top = 2.0 * FOX_ROUNDING_SLACK * bound - shift
    return shift, top.astype(F32), shift > FOX_MAX_SHIFT


def _post_kernel(x_ref, ya_ref, yb_ref, gmix_ref, wg_ref, bg_ref, wpa_ref, wpb_ref, wo_ref,
                 gffn_ref, wr_ref, rb_ref, x1_ref, h2_ref, topi_ref, topw_ref):
    d = x_ref.shape[1]
    x = x_ref[...]
    h = (x * lax.rsqrt(jnp.mean(x * x, axis=-1, keepdims=True) + EPS) * gmix_ref[...]).astype(BF16)
    gates = jax.nn.sigmoid(_dot(h, wg_ref[...]) + bg_ref[...])
    merged = gates[:, :d] * _dot(ya_ref[...], wpa_ref[...]) + gates[:, d:] * _dot(yb_ref[...], wpb_ref[...])
    x1 = x + _dot(merged.astype(BF16), wo_ref[...])
    x1_ref[...] = x1
    h2 = x1 * lax.rsqrt(jnp.mean(x1 * x1, axis=-1, keepdims=True) + EPS) * gffn_ref[...]
    h2_ref[...] = h2.astype(BF16)

    hh, hm, _ = _split3(h2)
    wr = wr_ref[...]
    wh = wr.astype(BF16)
    wl = (wr - wh.astype(F32)).astype(BF16)
    hh, hm = hh.astype(BF16), hm.astype(BF16)
    logits = _dot_nt(wh, hh) + _dot_nt(wh, hm) + _dot_nt(wl, hh)
    scores = jax.nn.sigmoid(logits)
    biased = scores + rb_ref[...]
    eid = lax.broadcasted_iota(jnp.int32, scores.shape, 0).astype(F32)
    chosen = jnp.zeros(scores.shape, jnp.bool_)
    idx, val = [], []
    for _ in range(TOP_K):
        cur = jnp.where(chosen, -jnp.inf, biased)
        mx = jnp.max(cur, axis=0, keepdims=True)
        first = jnp.min(jnp.where(cur == mx, eid, float(N_EXPERTS)), axis=0, keepdims=True)
        pick = eid == first
        chosen = jnp.logical_or(chosen, pick)
        idx.append(first)
        val.append(jnp.sum(jnp.where(pick, scores, 0.0), axis=0, keepdims=True))
    top_s = jnp.concatenate(val, axis=0)
    top_w = top_s / jnp.sum(top_s, axis=0, keepdims=True) * ROUTE_SCALE
    tm = scores.shape[1]
    both = jnp.concatenate(idx + [top_w, jnp.zeros((LANES - 2 * TOP_K, tm), F32)], axis=0).T
    topi_ref[...] = both[:, :TOP_K].astype(jnp.int32)
    topw_ref[...] = both[:, TOP_K:2 * TOP_K]


def _post(xf, ya, yb, g_mix, w_gate, b_gate, w_proj_a, w_proj_b, w_out, g_ffn, w_router, router_bias):
    n, d = xf.shape
    tm = TM_POST
    const = lambda shape: pl.BlockSpec(shape, lambda i: (0,) * len(shape))
    row = lambda w: pl.BlockSpec((tm, w), lambda i: (i, 0))
    args = [xf, ya, yb, g_mix.reshape(1, d), w_gate.astype(BF16), b_gate.reshape(1, 2 * d),
            w_proj_a.astype(BF16), w_proj_b.astype(BF16), w_out.astype(BF16), g_ffn.reshape(1, d),
            w_router.astype(F32).T, router_bias.astype(F32).reshape(N_EXPERTS, 1)]
    in_specs = [row(d), row(OUT_A), row(OUT_B)] + [const(a.shape) for a in args[3:]]
    return pl.pallas_call(
        _post_kernel,
        grid=(n // tm,),
        in_specs=in_specs,
        out_specs=[row(d), row(d), row(TOP_K), row(TOP_K)],
        out_shape=[jax.ShapeDtypeStruct((n, d), F32), jax.ShapeDtypeStruct((n, d), BF16),
                   jax.ShapeDtypeStruct((n, TOP_K), jnp.int32), jax.ShapeDtypeStruct((n, TOP_K), F32)],
        compiler_params=pltpu.CompilerParams(
            dimension_semantics=("arbitrary",), vmem_limit_bytes=VMEM_LIMIT),
        name="post",
    )(*args)


def _dispatch_kernel(h2_ref, topi_ref, tri_ref, upper_ref, xs_ref, slots_ref, cnt_ref, off_ref):
    tb = h2_ref.shape[0]
    topi = topi_ref[...]
    lane = lax.broadcasted_iota(jnp.int32, (tb, N_EXPERTS), 1)
    picks = [lane == topi[:, k:k + 1] for k in range(TOP_K)]
    mask = picks[0]
    for pk in picks[1:]:
        mask = jnp.logical_or(mask, pk)
    maskf = jnp.where(mask, 1.0, 0.0)
    rank = _dot(tri_ref[...], maskf.astype(BF16))
    cnt = jnp.sum(maskf, axis=0, keepdims=True)
    gran = jnp.floor((cnt + (GRAN - 1)) * (1.0 / GRAN))
    goff = _dot(jnp.broadcast_to(gran, (8, N_EXPERTS)).astype(BF16), upper_ref[...])[0:1]
    off = goff * GRAN
    slot_te = off + rank
    slots = jnp.concatenate(
        [jnp.sum(jnp.where(pk, slot_te, 0.0), axis=-1, keepdims=True) for pk in picks], axis=1)
    slots_ref[...] = slots.astype(jnp.int32)
    cnt_ref[0] = cnt.astype(jnp.int32)
    off_ref[0] = off.astype(jnp.int32)
    v = jnp.where(mask, slot_te, float(NO_SLOT))
    v_hi = jnp.floor(v * (1.0 / 64.0))
    w = jnp.concatenate([v_hi * 64.0, v - v_hi * 64.0], axis=1).T.astype(BF16)
    end = off + gran * GRAN
    used = jnp.max(end).astype(jnp.int32)
    h2 = h2_ref[...]

    def sort_chunk(c):
        s_e = (lax.broadcasted_iota(jnp.int32, (SLOT_CHUNK, N_EXPERTS), 0) + c * SLOT_CHUNK).astype(F32)
        own = jnp.where(jnp.logical_and(s_e >= off, s_e < end), 1.0, 0.0)
        looked = _dot(jnp.concatenate([own, own], axis=1).astype(BF16), w)
        s_t = (lax.broadcasted_iota(jnp.int32, (SLOT_CHUNK, tb), 0) + c * SLOT_CHUNK).astype(F32)
        onehot = jnp.where(looked == s_t, 1.0, 0.0).astype(BF16)
        xs_ref[0, c * SLOT_CHUNK:(c + 1) * SLOT_CHUNK, :] = _dot(onehot, h2).astype(BF16)

    for c in range(CAP // SLOT_CHUNK):
        if c < CHUNKS_TYPICAL:
            sort_chunk(c)
        else:
            @pl.when(c * SLOT_CHUNK < used)
            def _(c=c):
                sort_chunk(c)

            @pl.when(c * SLOT_CHUNK >= used)
            def _(c=c):
                xs_ref[0, c * SLOT_CHUNK:(c + 1) * SLOT_CHUNK, :] = jnp.zeros((SLOT_CHUNK, xs_ref.shape[2]), BF16)


def _dispatch(h2, topi):
    n, d = h2.shape
    nb = n // TB
    tri = jnp.asarray(np.tril(np.ones((TB, TB), np.float32), -1), BF16)
    upper = jnp.asarray(np.triu(np.ones((N_EXPERTS, N_EXPERTS), np.float32), 1), BF16)
    const = lambda shape: pl.BlockSpec(shape, lambda i: (0,) * len(shape))
    meta = pl.BlockSpec((1, 1, N_EXPERTS), lambda i: (i, 0, 0))
    return pl.pallas_call(
        _dispatch_kernel,
        grid=(nb,),
        in_specs=[pl.BlockSpec((TB, d), lambda i: (i, 0)), pl.BlockSpec((TB, TOP_K), lambda i: (i, 0)),
                  const(tri.shape), const(upper.shape)],
        out_specs=[pl.BlockSpec((1, CAP, d), lambda i: (i, 0, 0)),
                   pl.BlockSpec((TB, TOP_K), lambda i: (i, 0)), meta, meta],
        out_shape=[jax.ShapeDtypeStruct((nb, CAP, d), BF16), jax.ShapeDtypeStruct((n, TOP_K), jnp.int32),
                   jax.ShapeDtypeStruct((nb, 1, N_EXPERTS), jnp.int32),
                   jax.ShapeDtypeStruct((nb, 1, N_EXPERTS), jnp.int32)],
        compiler_params=pltpu.CompilerParams(
            dimension_semantics=("arbitrary",), vmem_limit_bytes=VMEM_LIMIT),
        name="dispatch",
    )(h2, topi, tri, upper)


def _ffn_kernel(item_e_sm, item_g0_sm, item_n_sm, glist_sm, xs_hbm, wg_ref, wu_ref, wd_ref, ys_hbm,
                xbuf, ybuf, sem_in, sem_out):
    step = pl.program_id(0)
    nsteps = pl.num_programs(0)
    buf = step % 2

    def for_granules(st, fn):
        g0 = item_g0_sm[st]
        n = item_n_sm[st]

        def per_granule(j, carry):
            fn(glist_sm[g0 + j], j)
            return carry

        lax.fori_loop(0, n, per_granule, 0)
        return n

    def fetch(b_):
        return lambda src, dst: pltpu.make_async_copy(xs_hbm.at[src], xbuf.at[b_, dst], sem_in.at[b_])

    def writeback(b_):
        return lambda src, dst: pltpu.make_async_copy(ybuf.at[b_, dst], ys_hbm.at[src], sem_out.at[b_])

    def start(mk):
        return lambda src, dst: mk(src, dst).start()

    def wait_all(st, span):
        n = item_n_sm[st]
        size = PASS_GRAN
        while size >= 1:
            @pl.when((n & size) != 0)
            def _(size=size):
                span(size).wait()
            size //= 2
        return n

    def fetch_span(b_):
        return lambda k: pltpu.make_async_copy(
            xs_hbm.at[pl.ds(0, k)], xbuf.at[b_, pl.ds(0, k)], sem_in.at[b_])

    def writeback_span(b_):
        return lambda k: pltpu.make_async_copy(
            ybuf.at[b_, pl.ds(0, k)], ys_hbm.at[pl.ds(0, k)], sem_out.at[b_])

    @pl.when(step == 0)
    def _():
        xbuf[...] = jnp.zeros_like(xbuf)
        for_granules(step, start(fetch(0)))

    @pl.when(step + 1 < nsteps)
    def _():
        for_granules(step + 1, start(fetch(1 - buf)))

    ngran = wait_all(step, fetch_span(buf))

    @pl.when(step >= 2)
    def _():
        wait_all(step - 2, writeback_span(buf))

    wg = wg_ref[0].astype(BF16)
    wu = wu_ref[0].astype(BF16)
    wd = wd_ref[0].astype(BF16)
    x_cols = xbuf.shape[-1]

    def ffn_rows(base, rows):
        grans = pl.ds(pl.multiple_of(base // GRAN, rows // GRAN), rows // GRAN)
        x = xbuf[buf, grans].reshape(rows, x_cols)
        g = _dot(x, wg)
        u = _dot(x, wu)
        mid = (g * jax.nn.sigmoid(g) * u).astype(BF16)
        ybuf[buf, grans] = _dot(mid, wd).astype(BF16).reshape(rows // GRAN, GRAN, x_cols)

    nt = (ngran * GRAN + (FT - 1)) // FT
    big = FT_BIG // FT

    def big_tile(i, carry):
        ffn_rows(pl.multiple_of(i * FT_BIG, FT_BIG), FT_BIG)
        return carry

    lax.fori_loop(0, nt // big, big_tile, 0)
    size = big // 2
    while size >= 1:
        @pl.when((nt & size) != 0)
        def _(size=size):
            ffn_rows(pl.multiple_of((nt & ~(2 * size - 1)) * FT, size * FT), size * FT)
        size //= 2

    for_granules(step, start(writeback(buf)))

    @pl.when(step == nsteps - 1)
    def _():
        wait_all(step, writeback_span(buf))

        @pl.when(step >= 1)
        def _():
            wait_all(step - 1, writeback_span(1 - buf))


def _work_items(cnt, off):
    nb = cnt.shape[0]
    seg_n = ((cnt.reshape(nb, N_EXPERTS) + (GRAN - 1)) // GRAN).T.reshape(-1)
    seg_row = ((off.reshape(nb, N_EXPERTS) + jnp.arange(nb, dtype=jnp.int32)[:, None] * CAP) // GRAN).T.reshape(-1)
    seg_end = jnp.cumsum(seg_n)
    seg_start = seg_end - seg_n
    gmax = nb * (TB * TOP_K // GRAN + N_EXPERTS)
    base = jnp.repeat(seg_row - seg_start, seg_n, total_repeat_length=gmax)
    glist = base + jnp.arange(gmax, dtype=jnp.int32)
    per_e = seg_n.reshape(N_EXPERTS, nb).sum(axis=1)
    first_e = seg_start[::nb]
    passes = (per_e + (PASS_GRAN - 1)) // PASS_GRAN
    pass_end = jnp.cumsum(passes)
    n_items = N_EXPERTS + gmax // PASS_GRAN
    w = jnp.arange(n_items, dtype=jnp.int32)
    item_e = jnp.minimum(jnp.searchsorted(pass_end, w, side="right"), N_EXPERTS - 1).astype(jnp.int32)
    done = (w - (pass_end - passes)[item_e]) * PASS_GRAN
    item_n = jnp.clip(per_e[item_e] - done, 0, PASS_GRAN)
    item_g0 = first_e[item_e] + done
    return item_e, item_g0.astype(jnp.int32), item_n.astype(jnp.int32), glist.astype(jnp.int32)


def _ffn(xs, cnt, off, wg, wu, wd):
    _, _, d = xs.shape
    item_e, item_g0, item_n, glist = _work_items(cnt, off)
    per_expert = lambda shape: pl.BlockSpec((1,) + shape, lambda w, ie, g0, n, gl: (ie[w], 0, 0))
    grid_spec = pltpu.PrefetchScalarGridSpec(
        num_scalar_prefetch=4,
        grid=(item_e.shape[0],),
        in_specs=[pl.BlockSpec(memory_space=pl.ANY), per_expert((d, D_EXPERT)), per_expert((d, D_EXPERT)),
                  per_expert((D_EXPERT, d))],
        out_specs=pl.BlockSpec(memory_space=pl.ANY),
        scratch_shapes=[pltpu.VMEM((2, PASS_GRAN, GRAN, d), BF16)] * 2 + [
                        pltpu.SemaphoreType.DMA((2,)), pltpu.SemaphoreType.DMA((2,))],
    )
    return pl.pallas_call(
        _ffn_kernel,
        grid_spec=grid_spec,
        out_shape=jax.ShapeDtypeStruct(xs.shape, xs.dtype),
        input_output_aliases={4: 0},
        compiler_params=pltpu.CompilerParams(
            dimension_semantics=("arbitrary",), vmem_limit_bytes=VMEM_LIMIT),
        name="ffn",
    )(item_e, item_g0, item_n, glist, xs, wg, wu, wd)


def _combine_kernel(used_sm, x1_ref, h2_ref, ys_ref, slots_ref, topw_ref, wgus_ref, wds_ref, o_ref):
    tb = x1_ref.shape[0]
    gu = _dot(h2_ref[...], wgus_ref[...])
    g, u = gu[:, :D_SHARED], gu[:, D_SHARED:]
    acc = x1_ref[...] + _dot((g * jax.nn.sigmoid(g) * u).astype(BF16), wds_ref[...])
    slots = slots_ref[...].astype(F32)
    topw = topw_ref[...]
    used = used_sm[pl.program_id(0)]

    def gather_chunk(c):
        scol = (lax.broadcasted_iota(jnp.int32, (tb, SLOT_CHUNK), 1) + c * SLOT_CHUNK).astype(F32)
        gate = jnp.zeros((tb, SLOT_CHUNK), F32)
        for k in range(TOP_K):
            gate = gate + jnp.where(scol == slots[:, k:k + 1], topw[:, k:k + 1], 0.0)
        return _dot(gate.astype(BF16), ys_ref[0, c * SLOT_CHUNK:(c + 1) * SLOT_CHUNK, :])

    for c in range(CHUNKS_TYPICAL):
        acc = acc + gather_chunk(c)
    o_ref[...] = acc
    for c in range(CHUNKS_TYPICAL, CAP // SLOT_CHUNK):
        @pl.when(c * SLOT_CHUNK < used)
        def _(c=c):
            o_ref[...] += gather_chunk(c)


def _combine(x1, h2, ys, slots, topw, used, wgus, wds):
    n, d = x1.shape
    const = lambda shape: pl.BlockSpec(shape, lambda i, u: (0,) * len(shape))
    row = lambda w: pl.BlockSpec((TB, w), lambda i, u: (i, 0))
    grid_spec = pltpu.PrefetchScalarGridSpec(
        num_scalar_prefetch=1,
        grid=(n // TB,),
        in_specs=[row(d), row(d), pl.BlockSpec((1, CAP, d), lambda i, u: (i, 0, 0)), row(TOP_K), row(TOP_K),
                  const(wgus.shape), const(wds.shape)],
        out_specs=row(d),
    )
    return pl.pallas_call(
        _combine_kernel,
        grid_spec=grid_spec,
        out_shape=jax.ShapeDtypeStruct((n, d), F32),
        compiler_params=pltpu.CompilerParams(
            dimension_semantics=("arbitrary",), vmem_limit_bytes=VMEM_LIMIT),
        name="combine",
    )(used, x1, h2, ys, slots, topw, wgus, wds)


def _moe(x1, h2, topi, topw, w_gate_e, w_up_e, w_down_e, w_gate_s, w_up_s, w_down_s):
    n, d = x1.shape
    wgus = jnp.concatenate([w_gate_s.astype(BF16), w_up_s.astype(BF16)], axis=-1)
    xs, slots, cnt, off = _dispatch(h2, topi)
    ys = _ffn(xs.reshape(-1, GRAN, d), cnt, off, w_gate_e, w_up_e, w_down_e)
    used = jnp.max(off + (cnt + (GRAN - 1)) // GRAN * GRAN, axis=(1, 2)).astype(jnp.int32)
    return _combine(x1, h2, ys.reshape(n // TB, CAP, d), slots, topw, used, wgus, w_down_s.astype(BF16))


def kernel(x, g_mix, w_in, q_norm_a, k_norm_a, q_norm_b, k_norm_b, rel_bias, b_forget, w_gate, b_gate,
           w_proj_a, w_proj_b, w_out, g_ffn, w_router, router_bias, w_gate_e, w_up_e, w_down_e,
           w_gate_s, w_up_s, w_down_s):
    batch, seq, d = x.shape
    xf = x.reshape(batch * seq, d)
    fox_shift, fox_top, fox_online = _fox_shift(q_norm_b, k_norm_b)
    pa0, pa1, pa2, qkb, vb, cb = _inproj(xf, g_mix, w_in, q_norm_a, k_norm_a, q_norm_b, k_norm_b, b_forget,
                                     fox_shift, seq)
    bias = jnp.stack([_toeplitz_bias(rel_bias, g, dil) for g, (_, dil) in enumerate(DIL_GROUPS)])
    ya = _dilated(pa0, pa1, pa2, bias, batch, seq)

    yb = _fox(qkb, vb, cb, fox_top, fox_online, batch, seq)
    x1, h2, topi, topw = _post(xf, ya, yb, g_mix, w_gate, b_gate, w_proj_a, w_proj_b, w_out, g_ffn,
                               w_router, router_bias)
    out = _moe(x1, h2, topi, topw, w_gate_e, w_up_e, w_down_e, w_gate_s, w_up_s, w_down_s)
    return out.reshape(batch, seq, d)
```

```python
import functools
import math

import jax
import jax.numpy as jnp
import numpy as np
from jax import lax
from jax.experimental import pallas as pl
from jax.experimental.pallas import tpu as pltpu

D_MODEL = 1024
HEAD_DIM = 64
DIL_GROUPS = ((128, 1), (512, 4), (2048, 16))
HEADS_PER_GROUP = 4
N_HEADS_A = HEADS_PER_GROUP * len(DIL_GROUPS)
N_HEADS_B = 8
REL_BUCKETS = 32
REL_MAX_DIST = 2048
N_EXPERTS = 64
TOP_K = 8
D_EXPERT = 256
D_SHARED = 256
ROUTE_SCALE = 2.5
EPS = 1e-6

WIDTH_A = 3 * N_HEADS_A * HEAD_DIM
WIDTH_B = 3 * N_HEADS_B * HEAD_DIM
QK_B = N_HEADS_B * HEAD_DIM
OUT_A = HEADS_PER_GROUP * HEAD_DIM
OUT_B = N_HEADS_B * HEAD_DIM

LANES = 128
GROUP_W = HEADS_PER_GROUP * HEAD_DIM
WIN_J = 128
SUPER = DIL_GROUPS[-1][1] * WIN_J
NEG = -1e30
VMEM_LIMIT = 56 * 1024 * 1024

DIL_UNROLL = 8
TM_IN = 512
TM_POST = 1024
TQ_FOX = 1024
FOX_ROUNDING_SLACK = 1.02
FOX_EXP_HEADROOM = 60.0
FOX_DEAD_EXPONENT = -105.0
FOX_MAX_SHIFT = 80.0
TB = 256
GRAN = 16
CAP = TB * TOP_K + N_EXPERTS * GRAN
SLOT_CHUNK = 512
CHUNKS_TYPICAL = -(-(TB * TOP_K + N_EXPERTS * GRAN // 2) // SLOT_CHUNK)
NO_SLOT = 4095
PASS_GRAN = 256
FT = 256
FT_BIG = 1024

BF16 = jnp.bfloat16
F32 = jnp.float32


def _dot(a, b):
    return jnp.dot(a, b, preferred_element_type=F32)


def _dot_nt(a, b):
    return lax.dot_general(a, b, (((1,), (1,)), ((), ())), preferred_element_type=F32)


def _split3(v):
    hi = v.astype(BF16).astype(F32)
    r = v - hi
    mid = r.astype(BF16).astype(F32)
    lo = (r - mid).astype(BF16).astype(F32)
    return hi, mid, lo


def _inproj_kernel(x_ref, g_ref, wa_ref, wb_ref, wf_ref, bd_ref, tri_ref, gain_a_ref, gain_b_ref,
                   bf_ref, shift_ref, pa0_ref, pa1_ref, pa2_ref, qkb_ref, vb_ref, cb_ref, carry_ref, h_ref, *,
                   tiles_per_seq):
    tm = x_ref.shape[0]
    x = x_ref[...]
    h = x * lax.rsqrt(jnp.mean(x * x, axis=-1, keepdims=True) + EPS) * g_ref[...]
    n_lane_chunks = h_ref.shape[0]
    for c in range(n_lane_chunks):
        h_ref[c] = h[:, c * LANES:(c + 1) * LANES]
    h = h.astype(BF16)
    bd = bd_ref[...]

    def headnorm(p, gain):
        ms = _dot((p * p).astype(BF16), bd)
        return p * lax.rsqrt(ms + EPS) * gain

    for g, (pa_ref, (_, dil)) in enumerate(zip((pa0_ref, pa1_ref, pa2_ref), DIL_GROUPS)):
        rows = tm // dil
        if dil == 1:
            hg = h
        else:
            hg = jnp.concatenate([jnp.concatenate(
                [h_ref[c, pl.ds(r, rows, stride=dil), :] for c in range(n_lane_chunks)], axis=1)
                for r in range(dil)], axis=0).astype(BF16)
        for part in range(3):
            cols = slice(part * GROUP_W, (part + 1) * GROUP_W)
            p = _dot(hg, wa_ref[g, :, cols])
            if part < 2:
                p = headnorm(p, gain_a_ref[part:part + 1, :])
            p = p.astype(BF16)
            if dil == 1:
                pa_ref[:, cols] = p
            else:
                for r in range(dil):
                    pa_ref[0, r, :, cols] = p[r * rows:(r + 1) * rows, :]

    f = _dot(h, wf_ref[...]) + bf_ref[...]
    logf = jnp.minimum(f, 0.0) - jnp.log1p(jnp.exp(-jnp.abs(f)))
    tri = tri_ref[...]
    lh, lm, ll = _split3(logf)
    cum = _dot(tri, lh.astype(BF16)) + _dot(tri, lm.astype(BF16)) + _dot(tri, ll.astype(BF16))

    @pl.when(pl.program_id(0) % tiles_per_seq == 0)
    def _():
        carry_ref[...] = jnp.zeros_like(carry_ref)

    cum = cum + carry_ref[0:1, :]
    carry_ref[0:1, :] = cum[tm - 1:tm, :]
    cb_ref[0] = jnp.concatenate([cum[0:1, :], cum[tm - 1:tm, :], jnp.zeros((6, LANES), F32)], axis=0)
    ch, cm, cl = _split3(cum)

    j = lax.broadcasted_iota(jnp.int32, (tm, HEAD_DIM), 1)

    def ext_cols(vals):
        out = jnp.zeros((tm, HEAD_DIM), F32)
        ones = [pos for pos, val in enumerate(vals) if isinstance(val, float)]
        if ones:
            is_one = functools.reduce(jnp.logical_or, [j == pos for pos in ones])
            out = jnp.where(is_one, 1.0, out)
        for pos, val in enumerate(vals):
            if not isinstance(val, float):
                out = jnp.where(j == pos, val, out)
        return out

    for c in range(QK_B // GROUP_W):
        wcols = lambda part: slice(part * QK_B + c * GROUP_W, part * QK_B + (c + 1) * GROUP_W)
        pq = headnorm(_dot(h, wb_ref[:, wcols(0)]), gain_b_ref[0:1, :])
        pk = headnorm(_dot(h, wb_ref[:, wcols(1)]), gain_b_ref[1:2, :])
        pv = _dot(h, wb_ref[:, wcols(2)])
        r = _dot((pq * pk).astype(BF16), bd) * HEAD_DIM + shift_ref[...]
        for hh in range(HEADS_PER_GROUP):
            head = c * HEADS_PER_GROUP + hh
            lanes = slice(hh * HEAD_DIM, (hh + 1) * HEAD_DIM)
            col = lambda a, idx: a[:, idx:idx + 1]
            cs = [col(ch, head), col(cm, head), col(cl, head)]
            ext_q = ext_cols(cs + [1.0] * 3 + [-col(r, hh * HEAD_DIM)])
            ext_k = ext_cols([1.0] * 3 + [-v for v in cs] + [1.0])
            ext_v = ext_cols([1.0])
            for part, (val, ext) in enumerate(((pq, ext_q), (pk, ext_k))):
                o0 = (part * N_HEADS_B + head) * LANES
                qkb_ref[:, o0:o0 + LANES] = jnp.concatenate([val[:, lanes], ext], axis=-1).astype(BF16)
            vb_ref[:, head * LANES:(head + 1) * LANES] = jnp.concatenate(
                [pv[:, lanes], ext_v], axis=-1).astype(BF16)


def _inproj(xf, g_mix, w_in, q_norm_a, k_norm_a, q_norm_b, k_norm_b, b_forget, fox_shift, seq):
    n, d = xf.shape
    tm = TM_IN
    scale = HEAD_DIM ** -0.5
    w_bf = w_in.astype(BF16)
    qkv_w = N_HEADS_A * HEAD_DIM
    wa = jnp.stack([jnp.concatenate(
        [w_bf[:, part * qkv_w + g * GROUP_W: part * qkv_w + (g + 1) * GROUP_W] for part in range(3)],
        axis=1) for g in range(len(DIL_GROUPS))])
    wb = w_bf[:, WIDTH_A:WIDTH_A + WIDTH_B]
    wf = jnp.pad(w_bf[:, WIDTH_A + WIDTH_B:], ((0, 0), (0, LANES - N_HEADS_B)))
    bfp = jnp.pad(b_forget.astype(F32), (0, LANES - N_HEADS_B)).reshape(1, LANES)
    seg = np.arange(GROUP_W) // HEAD_DIM
    bd = jnp.asarray((seg[:, None] == seg[None, :]).astype(np.float32) / HEAD_DIM, BF16)
    tri = jnp.asarray(np.tril(np.ones((tm, tm), np.float32)), BF16)
    gain_a = jnp.stack([jnp.tile(q_norm_a, HEADS_PER_GROUP) * scale, jnp.tile(k_norm_a, HEADS_PER_GROUP)])
    gain_b = jnp.stack([jnp.tile(q_norm_b, HEADS_PER_GROUP) * scale, jnp.tile(k_norm_b, HEADS_PER_GROUP)])
    const = lambda shape: pl.BlockSpec(shape, lambda i: (0,) * len(shape))
    tps = seq // tm
    batch = n // seq
    qkv3 = 3 * GROUP_W
    (_, d1), (_, d2) = DIL_GROUPS[1], DIL_GROUPS[2]
    return pl.pallas_call(
        functools.partial(_inproj_kernel, tiles_per_seq=tps),
        grid=(n // tm,),
        in_specs=[
            pl.BlockSpec((tm, d), lambda i: (i, 0)),
            const((1, d)), const(wa.shape), const(wb.shape), const(wf.shape),
            const(bd.shape), const(tri.shape), const(gain_a.shape), const(gain_b.shape),
            const(bfp.shape), const((1, 1)),
        ],
        out_specs=[
            pl.BlockSpec((tm, qkv3), lambda i: (i, 0)),
            pl.BlockSpec((1, d1, tm // d1, qkv3), lambda i: (i // tps, 0, i % tps, 0)),
            pl.BlockSpec((1, d2, tm // d2, qkv3), lambda i: (i // tps, 0, i % tps, 0)),
            pl.BlockSpec((tm, 2 * N_HEADS_B * LANES), lambda i: (i, 0)),
            pl.BlockSpec((tm, N_HEADS_B * LANES), lambda i: (i, 0)),
            pl.BlockSpec((1, 8, LANES), lambda i: (i, 0, 0)),
        ],
        out_shape=[
            jax.ShapeDtypeStruct((n, qkv3), BF16),
            jax.ShapeDtypeStruct((batch, d1, seq // d1, qkv3), BF16),
            jax.ShapeDtypeStruct((batch, d2, seq // d2, qkv3), BF16),
            jax.ShapeDtypeStruct((n, 2 * N_HEADS_B * LANES), BF16),
            jax.ShapeDtypeStruct((n, N_HEADS_B * LANES), BF16),
            jax.ShapeDtypeStruct((n // tm, 8, LANES), F32),
        ],
        scratch_shapes=[pltpu.VMEM((8, LANES), F32), pltpu.VMEM((d // LANES, tm, LANES), F32)],
        compiler_params=pltpu.CompilerParams(
            dimension_semantics=("arbitrary",), vmem_limit_bytes=VMEM_LIMIT),
        name="inproj",
    )(xf, g_mix.reshape(1, d), wa, wb, wf, bd, tri, gain_a, gain_b, bfp, fox_shift.reshape(1, 1))


def _dilated_kernel(p0_ref, h0_ref, p1_ref, h1_ref, p2_ref, h2_ref, bias_ref, o_ref, acc_ref, lse_ref):
    tq = WIN_J
    first_sb = pl.program_id(1) == 0
    lane_head = lax.broadcasted_iota(jnp.int32, (tq, GROUP_W), 1) // HEAD_DIM
    prev_col = lax.broadcasted_iota(jnp.int32, (tq, 2 * tq), 1) < tq
    qc, kc_, vc_ = (slice(0, GROUP_W), slice(GROUP_W, 2 * GROUP_W), slice(2 * GROUP_W, 3 * GROUP_W))

    def attend(g, q, kp, kc, vp, vc, no_prev):
        kcat = jnp.concatenate([kp, kc], axis=0)
        vcat = jnp.concatenate([vp, vc], axis=0)
        dead = jnp.logical_and(no_prev, prev_col)
        q4 = jnp.concatenate([jnp.where(lane_head == hh, q, jnp.zeros_like(q))
                              for hh in range(HEADS_PER_GROUP)], axis=0)
        s = _dot_nt(q4, kcat) + bias_ref[g].reshape(HEADS_PER_GROUP * tq, 2 * tq)
        s = jnp.where(jnp.concatenate([dead] * HEADS_PER_GROUP, axis=0), NEG, s)
        m = jnp.max(s, axis=-1, keepdims=True)
        p = jnp.exp(s - m)
        l = jnp.sum(p, axis=-1, keepdims=True)
        o4 = _dot(p.astype(BF16), vcat) * (1.0 / l)
        lse4 = m + jnp.log(l)
        acc = o4[0:tq]
        lse = jnp.broadcast_to(lse4[0:tq], (tq, GROUP_W))
        for hh in range(1, HEADS_PER_GROUP):
            sel = lane_head == hh
            acc = jnp.where(sel, o4[hh * tq:(hh + 1) * tq], acc)
            lse = jnp.where(sel, lse4[hh * tq:(hh + 1) * tq], lse)
        return acc, lse

    n_half = GROUP_W // LANES

    def merge(rows, acc, lse):
        for c in range(n_half):
            lanes = slice(c * LANES, (c + 1) * LANES)
            l1 = lse_ref[c, rows, :]
            mx = jnp.maximum(l1, lse[:, lanes])
            w1 = jnp.exp(l1 - mx)
            w2 = jnp.exp(lse[:, lanes] - mx)
            den = w1 + w2
            acc_ref[c, rows, :] = (w1 * acc_ref[c, rows, :] + w2 * acc[:, lanes]) / den
            lse_ref[c, rows, :] = mx + jnp.log(den)

    def pick(first, halo, body):
        return jnp.where(first, halo, body)

    def loop(n, body):
        def trip(i, carry):
            for u in range(DIL_UNROLL):
                body(i * DIL_UNROLL + u, carry)
            return carry
        lax.fori_loop(0, n // DIL_UNROLL, trip, 0)

    def body0(j, carry):
        st = pl.multiple_of(j * tq, tq)
        pst = pl.multiple_of(jnp.maximum(j - 1, 0) * tq, tq)
        cur, prv = pl.ds(st, tq), pl.ds(pst, tq)
        acc, lse = attend(
            0, p0_ref[0, cur, qc],
            pick(j == 0, h0_ref[0, :, kc_], p0_ref[0, prv, kc_]), p0_ref[0, cur, kc_],
            pick(j == 0, h0_ref[0, :, vc_], p0_ref[0, prv, vc_]), p0_ref[0, cur, vc_],
            jnp.logical_and(j == 0, first_sb))
        for c in range(n_half):
            acc_ref[c, cur, :] = acc[:, c * LANES:(c + 1) * LANES]
            lse_ref[c, cur, :] = lse[:, c * LANES:(c + 1) * LANES]
        return carry

    loop(SUPER // tq, body0)

    d1 = DIL_GROUPS[1][1]
    nsub1 = SUPER // d1 // tq
    def body1(t, carry):
        r, ii = t // nsub1, t % nsub1
        st = pl.multiple_of(ii * tq, tq)
        pst = pl.multiple_of(jnp.maximum(ii - 1, 0) * tq, tq)
        cur, prv = pl.ds(st, tq), pl.ds(pst, tq)
        acc, lse = attend(
            1, p1_ref[0, r, cur, qc],
            pick(ii == 0, h1_ref[0, r, :, kc_], p1_ref[0, r, prv, kc_]), p1_ref[0, r, cur, kc_],
            pick(ii == 0, h1_ref[0, r, :, vc_], p1_ref[0, r, prv, vc_]), p1_ref[0, r, cur, vc_],
            jnp.logical_and(ii == 0, first_sb))
        merge(pl.ds(ii * (tq * d1) + r, tq, stride=d1), acc, lse)
        return carry

    loop(d1 * nsub1, body1)

    d2 = DIL_GROUPS[2][1]

    def body2(r, carry):
        acc, lse = attend(2, p2_ref[0, r, :, qc], h2_ref[0, r, :, kc_], p2_ref[0, r, :, kc_],
                          h2_ref[0, r, :, vc_], p2_ref[0, r, :, vc_], first_sb)
        merge(pl.ds(r, tq, stride=d2), acc, lse)
        return carry

    loop(d2, body2)

    for c in range(n_half):
        o_ref[0, :, c * LANES:(c + 1) * LANES] = acc_ref[c].astype(o_ref.dtype)


def _rel_bucket(dist):
    max_exact = REL_BUCKETS // 2
    n = jnp.maximum(dist.astype(F32), 1.0)
    large = max_exact + (jnp.log(n / max_exact) / math.log(REL_MAX_DIST / max_exact)
                         * (REL_BUCKETS - max_exact)).astype(jnp.int32)
    large = jnp.minimum(large, REL_BUCKETS - 1)
    return jnp.where(dist < max_exact, dist, large)


def _toeplitz_bias(rel_bias, g, dil):
    tq = WIN_J
    offs = dil * (WIN_J - jnp.arange(WIN_J + 1, dtype=jnp.int32))
    hs = slice(g * HEADS_PER_GROUP, (g + 1) * HEADS_PER_GROUP)
    tab_rev = rel_bias[_rel_bucket(offs)][:, hs].T.astype(F32)
    period = 3 * tq
    neg = lambda w: jnp.full((HEADS_PER_GROUP, w), NEG, F32)
    vec = jnp.concatenate([neg(tq - 1), tab_rev, neg(period - 2 * tq)], axis=1)
    flat = jnp.broadcast_to(vec[:, None, :], (HEADS_PER_GROUP, tq, period)).reshape(HEADS_PER_GROUP, -1)
    skew = flat[:, :tq * (period - 1)].reshape(HEADS_PER_GROUP, tq, period - 1)
    return skew[:, :, tq - 1:3 * tq - 1]


def _dilated(pa0, pa1, pa2, bias, batch, seq):
    tq = WIN_J
    qkv3 = 3 * GROUP_W
    (_, d1), (_, d2) = DIL_GROUPS[1], DIL_GROUPS[2]
    nsb = seq // SUPER
    p0 = pa0.reshape(batch, seq, qkv3)
    prev_blk = lambda per_sb: (lambda b, s: jnp.maximum(s * per_sb - 1, 0))
    h0i, h1i, h2i = prev_blk(SUPER // tq), prev_blk(SUPER // d1 // tq), prev_blk(SUPER // d2 // tq)
    out = pl.pallas_call(
        _dilated_kernel,
        grid=(batch, nsb),
        in_specs=[
            pl.BlockSpec((1, SUPER, qkv3), lambda b, s: (b, s, 0)),
            pl.BlockSpec((1, tq, qkv3), lambda b, s: (b, h0i(b, s), 0)),
            pl.BlockSpec((1, d1, SUPER // d1, qkv3), lambda b, s: (b, 0, s, 0)),
            pl.BlockSpec((1, d1, tq, qkv3), lambda b, s: (b, 0, h1i(b, s), 0)),
            pl.BlockSpec((1, d2, SUPER // d2, qkv3), lambda b, s: (b, 0, s, 0)),
            pl.BlockSpec((1, d2, tq, qkv3), lambda b, s: (b, 0, h2i(b, s), 0)),
            pl.BlockSpec(bias.shape, lambda b, s: (0, 0, 0, 0)),
        ],
        out_specs=pl.BlockSpec((1, SUPER, GROUP_W), lambda b, s: (b, s, 0)),
        out_shape=jax.ShapeDtypeStruct((batch, seq, GROUP_W), BF16),
        scratch_shapes=[pltpu.VMEM((GROUP_W // LANES, SUPER, LANES), F32)] * 2,
        compiler_params=pltpu.CompilerParams(
            dimension_semantics=("arbitrary", "arbitrary"), vmem_limit_bytes=VMEM_LIMIT),
        name="dilated",
    )(p0, p0, pa1, pa1, pa2, pa2, bias)
    return out.reshape(batch * seq, GROUP_W)


def _fox_kernel(nlive_sm, q_ref, k_ref, v_ref, o_ref, m_ref, acc_ref, *, online):
    tq = q_ref.shape[1]
    half = tq // 2
    qi = pl.program_id(2)
    step = (pl.program_id(0) * pl.num_programs(1) + pl.program_id(1)) * pl.num_programs(2) + qi
    row = lax.broadcasted_iota(jnp.int32, (half, half), 0)
    col = lax.broadcasted_iota(jnp.int32, (half, half), 1)
    causal = row >= col

    def attend(hh, rows, state, start, nkeys, masked):
        m, acc = state
        lanes = slice(hh * LANES, (hh + 1) * LANES)
        s = _dot_nt(q_ref[0, rows, lanes], k_ref[0, pl.ds(start, nkeys), lanes])
        if masked:
            s = jnp.where(causal, s, NEG)
        if online:
            m_new = jnp.maximum(m, jnp.max(s, axis=-1, keepdims=True))
            acc = acc * jnp.exp(m - m_new)
            s = s - m_new
            m = m_new
        return m, acc + _dot(jnp.exp(s).astype(BF16), v_ref[0, pl.ds(start, nkeys), lanes])

    first = [qi - nlive_sm[2 * step + hh] for hh in range(2)]
    for hh in range(2):
        m_ref[hh] = jnp.full((tq, 1), NEG, F32)
        acc_ref[hh] = jnp.zeros((tq, LANES), F32)

    def full_chunk(ki, carry):
        start = pl.multiple_of(ki * tq, tq)
        for hh in range(2):
            @pl.when(ki >= first[hh])
            def _(hh=hh):
                m, acc = attend(hh, slice(None), (m_ref[hh], acc_ref[hh]), start, tq, False)
                acc_ref[hh] = acc
                if online:
                    m_ref[hh] = m
        return carry

    lax.fori_loop(jnp.minimum(first[0], first[1]), qi, full_chunk, 0)

    d0 = pl.multiple_of(qi * tq, tq)
    outs = []
    for hh in range(2):
        m, acc = m_ref[hh], acc_ref[hh]
        top, bot = slice(0, half), slice(half, tq)
        s_top = attend(hh, top, (m[top], acc[top]), d0, half, True)
        s_bot = attend(hh, bot, (m[bot], acc[bot]), d0, half, False)
        s_bot = attend(hh, bot, s_bot, d0 + half, half, True)
        a = jnp.concatenate([s_top[1], s_bot[1]], axis=0)
        outs.append(a[:, :HEAD_DIM] / a[:, HEAD_DIM:HEAD_DIM + 1])
    o_ref[0] = jnp.concatenate(outs, axis=-1).astype(o_ref.dtype)


def _fox_live_chunks(cb, top, batch, seq):
    tps = seq // TM_IN
    per = TQ_FOX // TM_IN
    nq = seq // TQ_FOX
    c_first = cb[:, 0, :N_HEADS_B].reshape(batch, tps, N_HEADS_B)[:, ::per]
    c_last = cb[:, 1, :N_HEADS_B].reshape(batch, tps, N_HEADS_B)[:, per - 1::per]
    live = (top + c_first[:, :, None, :] - c_last[:, None, :, :]) >= FOX_DEAD_EXPONENT
    back = jnp.arange(nq)[:, None] - jnp.arange(nq)[None, :]
    reach = jnp.max(jnp.where(jnp.logical_and(live, (back > 0)[None, :, :, None]),
                              back[None, :, :, None], 0), axis=2)
    reach = reach.reshape(batch, nq, N_HEADS_B // 2, 2).transpose(0, 2, 1, 3)
    return reach.reshape(-1).astype(jnp.int32)


def _fox(qkb, vb, cb, top, online, batch, seq):
    tq = TQ_FOX
    pairs = N_HEADS_B // 2
    nq = seq // tq
    qkv = qkb.reshape(batch, seq, 2 * N_HEADS_B * LANES)
    vv = vb.reshape(batch, seq, N_HEADS_B * LANES)
    all_chunks = jnp.tile(jnp.repeat(jnp.arange(nq, dtype=jnp.int32), 2), batch * pairs)

    def call(is_online, nlive):
        grid_spec = pltpu.PrefetchScalarGridSpec(
            num_scalar_prefetch=1,
            grid=(batch, pairs, nq),
            in_specs=[
                pl.BlockSpec((1, tq, 2 * LANES), lambda b, p, i, n: (b, i, p)),
                pl.BlockSpec((1, seq, 2 * LANES), lambda b, p, i, n: (b, 0, pairs + p)),
                pl.BlockSpec((1, seq, 2 * LANES), lambda b, p, i, n: (b, 0, p)),
            ],
            out_specs=pl.BlockSpec((1, tq, LANES), lambda b, p, i, n: (b, i, p)),
            scratch_shapes=[pltpu.VMEM((2, tq, 1), F32), pltpu.VMEM((2, tq, LANES), F32)],
        )
        return pl.pallas_call(
            functools.partial(_fox_kernel, online=is_online),
            grid_spec=grid_spec,
            out_shape=jax.ShapeDtypeStruct((batch, seq, OUT_B), BF16),
            compiler_params=pltpu.CompilerParams(
                dimension_semantics=("arbitrary", "arbitrary", "arbitrary"), vmem_limit_bytes=VMEM_LIMIT),
            name="fox_online" if is_online else "fox",
        )(nlive, qkv, qkv, vv)

    out = lax.cond(online, lambda: call(True, all_chunks),
                   lambda: call(False, _fox_live_chunks(cb, top, batch, seq)))
    return out.reshape(batch * seq, OUT_B)


def _fox_shift(q_norm_b, k_norm_b):
    bound = HEAD_DIM * (HEAD_DIM ** -0.5) * jnp.max(jnp.abs(q_norm_b)) * jnp.max(jnp.abs(k_norm_b))
    shift = jnp.maximum(2.0 * FOX_ROUNDING_SLACK * bound - FOX_EXP_HEADROOM, 0.0).astype(F32)
    top = 2.0 * FOX_ROUNDING_SLACK * bound - shift
    return shift, top.astype(F32), shift > FOX_MAX_SHIFT


def _post_kernel(x_ref, ya_ref, yb_ref, gmix_ref, wg_ref, bg_ref, wpa_ref, wpb_ref, wo_ref,
                 gffn_ref, wr_ref, rb_ref, x1_ref, h2_ref, topi_ref, topw_ref):
    d = x_ref.shape[1]
    x = x_ref[...]
    h = (x * lax.rsqrt(jnp.mean(x * x, axis=-1, keepdims=True) + EPS) * gmix_ref[...]).astype(BF16)
    gates = jax.nn.sigmoid(_dot(h, wg_ref[...]) + bg_ref[...])
    merged = gates[:, :d] * _dot(ya_ref[...], wpa_ref[...]) + gates[:, d:] * _dot(yb_ref[...], wpb_ref[...])
    x1 = x + _dot(merged.astype(BF16), wo_ref[...])
    x1_ref[...] = x1
    h2 = x1 * lax.rsqrt(jnp.mean(x1 * x1, axis=-1, keepdims=True) + EPS) * gffn_ref[...]
    h2_ref[...] = h2.astype(BF16)

    hh, hm, _ = _split3(h2)
    wr = wr_ref[...]
    wh = wr.astype(BF16)
    wl = (wr - wh.astype(F32)).astype(BF16)
    hh, hm = hh.astype(BF16), hm.astype(BF16)
    logits = _dot_nt(wh, hh) + _dot_nt(wh, hm) + _dot_nt(wl, hh)
    scores = jax.nn.sigmoid(logits)
    biased = scores + rb_ref[...]
    eid = lax.broadcasted_iota(jnp.int32, scores.shape, 0).astype(F32)
    chosen = jnp.zeros(scores.shape, jnp.bool_)
    idx, val = [], []
    for _ in range(TOP_K):
        cur = jnp.where(chosen, -jnp.inf, biased)
        mx = jnp.max(cur, axis=0, keepdims=True)
        first = jnp.min(jnp.where(cur == mx, eid, float(N_EXPERTS)), axis=0, keepdims=True)
        pick = eid == first
        chosen = jnp.logical_or(chosen, pick)
        idx.append(first)
        val.append(jnp.sum(jnp.where(pick, scores, 0.0), axis=0, keepdims=True))
    top_s = jnp.concatenate(val, axis=0)
    top_w = top_s / jnp.sum(top_s, axis=0, keepdims=True) * ROUTE_SCALE
    tm = scores.shape[1]
    both = jnp.concatenate(idx + [top_w, jnp.zeros((LANES - 2 * TOP_K, tm), F32)], axis=0).T
    topi_ref[...] = both[:, :TOP_K].astype(jnp.int32)
    topw_ref[...] = both[:, TOP_K:2 * TOP_K]


def _post(xf, ya, yb, g_mix, w_gate, b_gate, w_proj_a, w_proj_b, w_out, g_ffn, w_router, router_bias):
    n, d = xf.shape
    tm = TM_POST
    const = lambda shape: pl.BlockSpec(shape, lambda i: (0,) * len(shape))
    row = lambda w: pl.BlockSpec((tm, w), lambda i: (i, 0))
    args = [xf, ya, yb, g_mix.reshape(1, d), w_gate.astype(BF16), b_gate.reshape(1, 2 * d),
            w_proj_a.astype(BF16), w_proj_b.astype(BF16), w_out.astype(BF16), g_ffn.reshape(1, d),
            w_router.astype(F32).T, router_bias.astype(F32).reshape(N_EXPERTS, 1)]
    in_specs = [row(d), row(OUT_A), row(OUT_B)] + [const(a.shape) for a in args[3:]]
    return pl.pallas_call(
        _post_kernel,
        grid=(n // tm,),
        in_specs=in_specs,
        out_specs=[row(d), row(d), row(TOP_K), row(TOP_K)],
        out_shape=[jax.ShapeDtypeStruct((n, d), F32), jax.ShapeDtypeStruct((n, d), BF16),
                   jax.ShapeDtypeStruct((n, TOP_K), jnp.int32), jax.ShapeDtypeStruct((n, TOP_K), F32)],
        compiler_params=pltpu.CompilerParams(
            dimension_semantics=("arbitrary",), vmem_limit_bytes=VMEM_LIMIT),
        name="post",
    )(*args)


def _dispatch_kernel(h2_ref, topi_ref, tri_ref, upper_ref, xs_ref, slots_ref, cnt_ref, off_ref):
    tb = h2_ref.shape[0]
    topi = topi_ref[...]
    lane = lax.broadcasted_iota(jnp.int32, (tb, N_EXPERTS), 1)
    picks = [lane == topi[:, k:k + 1] for k in range(TOP_K)]
    mask = picks[0]
    for pk in picks[1:]:
        mask = jnp.logical_or(mask, pk)
    maskf = jnp.where(mask, 1.0, 0.0)
    rank = _dot(tri_ref[...], maskf.astype(BF16))
    cnt = jnp.sum(maskf, axis=0, keepdims=True)
    gran = jnp.floor((cnt + (GRAN - 1)) * (1.0 / GRAN))
    goff = _dot(jnp.broadcast_to(gran, (8, N_EXPERTS)).astype(BF16), upper_ref[...])[0:1]
    off = goff * GRAN
    slot_te = off + rank
    slots = jnp.concatenate(
        [jnp.sum(jnp.where(pk, slot_te, 0.0), axis=-1, keepdims=True) for pk in picks], axis=1)
    slots_ref[...] = slots.astype(jnp.int32)
    cnt_ref[0] = cnt.astype(jnp.int32)
    off_ref[0] = off.astype(jnp.int32)
    v = jnp.where(mask, slot_te, float(NO_SLOT))
    v_hi = jnp.floor(v * (1.0 / 64.0))
    w = jnp.concatenate([v_hi * 64.0, v - v_hi * 64.0], axis=1).T.astype(BF16)
    end = off + gran * GRAN
    used = jnp.max(end).astype(jnp.int32)
    h2 = h2_ref[...]

    def sort_chunk(c):
        s_e = (lax.broadcasted_iota(jnp.int32, (SLOT_CHUNK, N_EXPERTS), 0) + c * SLOT_CHUNK).astype(F32)
        own = jnp.where(jnp.logical_and(s_e >= off, s_e < end), 1.0, 0.0)
        looked = _dot(jnp.concatenate([own, own], axis=1).astype(BF16), w)
        s_t = (lax.broadcasted_iota(jnp.int32, (SLOT_CHUNK, tb), 0) + c * SLOT_CHUNK).astype(F32)
        onehot = jnp.where(looked == s_t, 1.0, 0.0).astype(BF16)
        xs_ref[0, c * SLOT_CHUNK:(c + 1) * SLOT_CHUNK, :] = _dot(onehot, h2).astype(BF16)

    for c in range(CAP // SLOT_CHUNK):
        if c < CHUNKS_TYPICAL:
            sort_chunk(c)
        else:
            @pl.when(c * SLOT_CHUNK < used)
            def _(c=c):
                sort_chunk(c)

            @pl.when(c * SLOT_CHUNK >= used)
            def _(c=c):
                xs_ref[0, c * SLOT_CHUNK:(c + 1) * SLOT_CHUNK, :] = jnp.zeros((SLOT_CHUNK, xs_ref.shape[2]), BF16)


def _dispatch(h2, topi):
    n, d = h2.shape
    nb = n // TB
    tri = jnp.asarray(np.tril(np.ones((TB, TB), np.float32), -1), BF16)
    upper = jnp.asarray(np.triu(np.ones((N_EXPERTS, N_EXPERTS), np.float32), 1), BF16)
    const = lambda shape: pl.BlockSpec(shape, lambda i: (0,) * len(shape))
    meta = pl.BlockSpec((1, 1, N_EXPERTS), lambda i: (i, 0, 0))
    return pl.pallas_call(
        _dispatch_kernel,
        grid=(nb,),
        in_specs=[pl.BlockSpec((TB, d), lambda i: (i, 0)), pl.BlockSpec((TB, TOP_K), lambda i: (i, 0)),
                  const(tri.shape), const(upper.shape)],
        out_specs=[pl.BlockSpec((1, CAP, d), lambda i: (i, 0, 0)),
                   pl.BlockSpec((TB, TOP_K), lambda i: (i, 0)), meta, meta],
        out_shape=[jax.ShapeDtypeStruct((nb, CAP, d), BF16), jax.ShapeDtypeStruct((n, TOP_K), jnp.int32),
                   jax.ShapeDtypeStruct((nb, 1, N_EXPERTS), jnp.int32),
                   jax.ShapeDtypeStruct((nb, 1, N_EXPERTS), jnp.int32)],
        compiler_params=pltpu.CompilerParams(
            dimension_semantics=("arbitrary",), vmem_limit_bytes=VMEM_LIMIT),
        name="dispatch",
    )(h2, topi, tri, upper)


def _ffn_kernel(item_e_sm, item_g0_sm, item_n_sm, glist_sm, xs_hbm, wg_ref, wu_ref, wd_ref, ys_hbm,
                xbuf, ybuf, sem_in, sem_out):
    step = pl.program_id(0)
    nsteps = pl.num_programs(0)
    buf = step % 2

    def for_granules(st, fn):
        g0 = item_g0_sm[st]
        n = item_n_sm[st]

        def per_granule(j, carry):
            fn(glist_sm[g0 + j], j)
            return carry

        lax.fori_loop(0, n, per_granule, 0)
        return n

    def fetch(b_):
        return lambda src, dst: pltpu.make_async_copy(xs_hbm.at[src], xbuf.at[b_, dst], sem_in.at[b_])

    def writeback(b_):
        return lambda src, dst: pltpu.make_async_copy(ybuf.at[b_, dst], ys_hbm.at[src], sem_out.at[b_])

    def start(mk):
        return lambda src, dst: mk(src, dst).start()

    def wait_all(st, span):
        n = item_n_sm[st]
        size = PASS_GRAN
        while size >= 1:
            @pl.when((n & size) != 0)
            def _(size=size):
                span(size).wait()
            size //= 2
        return n

    def fetch_span(b_):
        return lambda k: pltpu.make_async_copy(
            xs_hbm.at[pl.ds(0, k)], xbuf.at[b_, pl.ds(0, k)], sem_in.at[b_])

    def writeback_span(b_):
        return lambda k: pltpu.make_async_copy(
            ybuf.at[b_, pl.ds(0, k)], ys_hbm.at[pl.ds(0, k)], sem_out.at[b_])

    @pl.when(step == 0)
    def _():
        xbuf[...] = jnp.zeros_like(xbuf)
        for_granules(step, start(fetch(0)))

    @pl.when(step + 1 < nsteps)
    def _():
        for_granules(step + 1, start(fetch(1 - buf)))

    ngran = wait_all(step, fetch_span(buf))

    @pl.when(step >= 2)
    def _():
        wait_all(step - 2, writeback_span(buf))

    wg = wg_ref[0].astype(BF16)
    wu = wu_ref[0].astype(BF16)
    wd = wd_ref[0].astype(BF16)
    x_cols = xbuf.shape[-1]

    def ffn_rows(base, rows):
        grans = pl.ds(pl.multiple_of(base // GRAN, rows // GRAN), rows // GRAN)
        x = xbuf[buf, grans].reshape(rows, x_cols)
        g = _dot(x, wg)
        u = _dot(x, wu)
        mid = (g * jax.nn.sigmoid(g) * u).astype(BF16)
        ybuf[buf, grans] = _dot(mid, wd).astype(BF16).reshape(rows // GRAN, GRAN, x_cols)

    nt = (ngran * GRAN + (FT - 1)) // FT
    big = FT_BIG // FT

    def big_tile(i, carry):
        ffn_rows(pl.multiple_of(i * FT_BIG, FT_BIG), FT_BIG)
        return carry

    lax.fori_loop(0, nt // big, big_tile, 0)
    size = big // 2
    while size >= 1:
        @pl.when((nt & size) != 0)
        def _(size=size):
            ffn_rows(pl.multiple_of((nt & ~(2 * size - 1)) * FT, size * FT), size * FT)
        size //= 2

    for_granules(step, start(writeback(buf)))

    @pl.when(step == nsteps - 1)
    def _():
        wait_all(step, writeback_span(buf))

        @pl.when(step >= 1)
        def _():
            wait_all(step - 1, writeback_span(1 - buf))


def _work_items(cnt, off):
    nb = cnt.shape[0]
    seg_n = ((cnt.reshape(nb, N_EXPERTS) + (GRAN - 1)) // GRAN).T.reshape(-1)
    seg_row = ((off.reshape(nb, N_EXPERTS) + jnp.arange(nb, dtype=jnp.int32)[:, None] * CAP) // GRAN).T.reshape(-1)
    seg_end = jnp.cumsum(seg_n)
    seg_start = seg_end - seg_n
    gmax = nb * (TB * TOP_K // GRAN + N_EXPERTS)
    prev_end = jnp.concatenate([jnp.ones((1,), jnp.int32), (seg_row + seg_n)[:-1]])
    steps = jnp.ones((gmax,), jnp.int32).at[seg_start].add(seg_row - prev_end, mode="drop")
    glist = jnp.cumsum(steps)
    per_e = seg_n.reshape(N_EXPERTS, nb).sum(axis=1)
    first_e = seg_start[::nb]
    passes = (per_e + (PASS_GRAN - 1)) // PASS_GRAN
    pass_end = jnp.cumsum(passes)
    n_items = N_EXPERTS + gmax // PASS_GRAN
    w = jnp.arange(n_items, dtype=jnp.int32)
    item_e = jnp.minimum(jnp.searchsorted(pass_end, w, side="right"), N_EXPERTS - 1).astype(jnp.int32)
    done = (w - (pass_end - passes)[item_e]) * PASS_GRAN
    item_n = jnp.clip(per_e[item_e] - done, 0, PASS_GRAN)
    item_g0 = first_e[item_e] + done
    return item_e, item_g0.astype(jnp.int32), item_n.astype(jnp.int32), glist.astype(jnp.int32)


def _ffn(xs, cnt, off, wg, wu, wd):
    _, _, d = xs.shape
    item_e, item_g0, item_n, glist = _work_items(cnt, off)
    per_expert = lambda shape: pl.BlockSpec((1,) + shape, lambda w, ie, g0, n, gl: (ie[w], 0, 0))
    grid_spec = pltpu.PrefetchScalarGridSpec(
        num_scalar_prefetch=4,
        grid=(item_e.shape[0],),
        in_specs=[pl.BlockSpec(memory_space=pl.ANY), per_expert((d, D_EXPERT)), per_expert((d, D_EXPERT)),
                  per_expert((D_EXPERT, d))],
        out_specs=pl.BlockSpec(memory_space=pl.ANY),
        scratch_shapes=[pltpu.VMEM((2, PASS_GRAN, GRAN, d), BF16)] * 2 + [
                        pltpu.SemaphoreType.DMA((2,)), pltpu.SemaphoreType.DMA((2,))],
    )
    return pl.pallas_call(
        _ffn_kernel,
        grid_spec=grid_spec,
        out_shape=jax.ShapeDtypeStruct(xs.shape, xs.dtype),
        input_output_aliases={4: 0},
        compiler_params=pltpu.CompilerParams(
            dimension_semantics=("arbitrary",), vmem_limit_bytes=VMEM_LIMIT),
        name="ffn",
    )(item_e, item_g0, item_n, glist, xs, wg, wu, wd)


def _combine_kernel(used_sm, x1_ref, h2_ref, ys_ref, slots_ref, topw_ref, wgus_ref, wds_ref, o_ref):
    tb = x1_ref.shape[0]
    gu = _dot(h2_ref[...], wgus_ref[...])
    g, u = gu[:, :D_SHARED], gu[:, D_SHARED:]
    acc = x1_ref[...] + _dot((g * jax.nn.sigmoid(g) * u).astype(BF16), wds_ref[...])
    slots = slots_ref[...].astype(F32)
    topw = topw_ref[...]
    used = used_sm[pl.program_id(0)]

    def gather_chunk(c):
        scol = (lax.broadcasted_iota(jnp.int32, (tb, SLOT_CHUNK), 1) + c * SLOT_CHUNK).astype(F32)
        gate = jnp.zeros((tb, SLOT_CHUNK), F32)
        for k in range(TOP_K):
            gate = gate + jnp.where(scol == slots[:, k:k + 1], topw[:, k:k + 1], 0.0)
        return _dot(gate.astype(BF16), ys_ref[0, c * SLOT_CHUNK:(c + 1) * SLOT_CHUNK, :])

    for c in range(CHUNKS_TYPICAL):
        acc = acc + gather_chunk(c)
    o_ref[...] = acc
    for c in range(CHUNKS_TYPICAL, CAP // SLOT_CHUNK):
        @pl.when(c * SLOT_CHUNK < used)
        def _(c=c):
            o_ref[...] += gather_chunk(c)


def _combine(x1, h2, ys, slots, topw, used, wgus, wds):
    n, d = x1.shape
    const = lambda shape: pl.BlockSpec(shape, lambda i, u: (0,) * len(shape))
    row = lambda w: pl.BlockSpec((TB, w), lambda i, u: (i, 0))
    grid_spec = pltpu.PrefetchScalarGridSpec(
        num_scalar_prefetch=1,
        grid=(n // TB,),
        in_specs=[row(d), row(d), pl.BlockSpec((1, CAP, d), lambda i, u: (i, 0, 0)), row(TOP_K), row(TOP_K),
                  const(wgus.shape), const(wds.shape)],
        out_specs=row(d),
    )
    return pl.pallas_call(
        _combine_kernel,
        grid_spec=grid_spec,
        out_shape=jax.ShapeDtypeStruct((n, d), F32),
        compiler_params=pltpu.CompilerParams(
            dimension_semantics=("arbitrary",), vmem_limit_bytes=VMEM_LIMIT),
        name="combine",
    )(used, x1, h2, ys, slots, topw, wgus, wds)


def _moe(x1, h2, topi, topw, w_gate_e, w_up_e, w_down_e, w_gate_s, w_up_s, w_down_s):
    n, d = x1.shape
    wgus = jnp.concatenate([w_gate_s.astype(BF16), w_up_s.astype(BF16)], axis=-1)
    xs, slots, cnt, off = _dispatch(h2, topi)
    ys = _ffn(xs.reshape(-1, GRAN, d), cnt, off, w_gate_e, w_up_e, w_down_e)
    used = jnp.max(off + (cnt + (GRAN - 1)) // GRAN * GRAN, axis=(1, 2)).astype(jnp.int32)
    return _combine(x1, h2, ys.reshape(n // TB, CAP, d), slots, topw, used, wgus, w_down_s.astype(BF16))


def kernel(x, g_mix, w_in, q_norm_a, k_norm_a, q_norm_b, k_norm_b, rel_bias, b_forget, w_gate, b_gate,
           w_proj_a, w_proj_b, w_out, g_ffn, w_router, router_bias, w_gate_e, w_up_e, w_down_e,
           w_gate_s, w_up_s, w_down_s):
    batch, seq, d = x.shape
    xf = x.reshape(batch * seq, d)
    fox_shift, fox_top, fox_online = _fox_shift(q_norm_b, k_norm_b)
    pa0, pa1, pa2, qkb, vb, cb = _inproj(xf, g_mix, w_in, q_norm_a, k_norm_a, q_norm_b, k_norm_b, b_forget,
                                     fox_shift, seq)
    bias = jnp.stack([_toeplitz_bias(rel_bias, g, dil) for g, (_, dil) in enumerate(DIL_GROUPS)])
    ya = _dilated(pa0, pa1, pa2, bias, batch, seq)

    yb = _fox(qkb, vb, cb, fox_top, fox_online, batch, seq)
    x1, h2, topi, topw = _post(xf, ya, yb, g_mix, w_gate, b_gate, w_proj_a, w_proj_b, w_out, g_ffn,
                               w_router, router_bias)
    out = _moe(x1, h2, topi, topw, w_gate_e, w_up_e, w_down_e, w_gate_s, w_up_s, w_down_s)
    return out.reshape(batch, seq, d)
```

```python
import functools
import math

import jax
import jax.numpy as jnp
import numpy as np
from jax import lax
from jax.experimental import pallas as pl
from jax.experimental.pallas import tpu as pltpu

D_MODEL = 1024
HEAD_DIM = 64
DIL_GROUPS = ((128, 1), (512, 4), (2048, 16))
HEADS_PER_GROUP = 4
N_HEADS_A = HEADS_PER_GROUP * len(DIL_GROUPS)
N_HEADS_B = 8
REL_BUCKETS = 32
REL_MAX_DIST = 2048
N_EXPERTS = 64
TOP_K = 8
D_EXPERT = 256
D_SHARED = 256
ROUTE_SCALE = 2.5
EPS = 1e-6

WIDTH_A = 3 * N_HEADS_A * HEAD_DIM
WIDTH_B = 3 * N_HEADS_B * HEAD_DIM
QK_B = N_HEADS_B * HEAD_DIM
OUT_A = HEADS_PER_GROUP * HEAD_DIM
OUT_B = N_HEADS_B * HEAD_DIM

LANES = 128
GROUP_W = HEADS_PER_GROUP * HEAD_DIM
WIN_J = 128
SUPER = DIL_GROUPS[-1][1] * WIN_J
NEG = -1e30
VMEM_LIMIT = 56 * 1024 * 1024

DIL_UNROLL = 8
TM_IN = 512
TM_POST = 1024
TQ_FOX = 1024
FOX_ROUNDING_SLACK = 1.02
FOX_EXP_HEADROOM = 60.0
FOX_DEAD_EXPONENT = -105.0
FOX_MAX_SHIFT = 80.0
TB = 256
GRAN = 16
CAP = TB * TOP_K + N_EXPERTS * GRAN
SLOT_CHUNK = 512
CHUNKS_TYPICAL = -(-(TB * TOP_K + N_EXPERTS * GRAN // 2) // SLOT_CHUNK)
NO_SLOT = 4095
PASS_GRAN = 256
FT = 256
FT_BIG = 1024

BF16 = jnp.bfloat16
F32 = jnp.float32


def _dot(a, b):
    return jnp.dot(a, b, preferred_element_type=F32)


def _dot_nt(a, b):
    return lax.dot_general(a, b, (((1,), (1,)), ((), ())), preferred_element_type=F32)


def _split3(v):
    hi = v.astype(BF16).astype(F32)
    r = v - hi
    mid = r.astype(BF16).astype(F32)
    lo = (r - mid).astype(BF16).astype(F32)
    return hi, mid, lo


def _inproj_kernel(x_ref, g_ref, wa_ref, wb_ref, wf_ref, bd_ref, tri_ref, gain_a_ref, gain_b_ref,
                   bf_ref, shift_ref, pa0_ref, pa1_ref, pa2_ref, qkb_ref, vb_ref, cb_ref, carry_ref, h_ref, *,
                   tiles_per_seq):
    tm = x_ref.shape[0]
    x = x_ref[...]
    h = x * lax.rsqrt(jnp.mean(x * x, axis=-1, keepdims=True) + EPS) * g_ref[...]
    n_lane_chunks = h_ref.shape[0]
    for c in range(n_lane_chunks):
        h_ref[c] = h[:, c * LANES:(c + 1) * LANES]
    h = h.astype(BF16)
    bd = bd_ref[...]

    def headnorm(p, gain):
        ms = _dot((p * p).astype(BF16), bd)
        return p * lax.rsqrt(ms + EPS) * gain

    for g, (pa_ref, (_, dil)) in enumerate(zip((pa0_ref, pa1_ref, pa2_ref), DIL_GROUPS)):
        rows = tm // dil
        if dil == 1:
            hg = h
        else:
            hg = jnp.concatenate([jnp.concatenate(
                [h_ref[c, pl.ds(r, rows, stride=dil), :] for c in range(n_lane_chunks)], axis=1)
                for r in range(dil)], axis=0).astype(BF16)
        for part in range(3):
            cols = slice(part * GROUP_W, (part + 1) * GROUP_W)
            p = _dot(hg, wa_ref[g, :, cols])
            if part < 2:
                p = headnorm(p, gain_a_ref[part:part + 1, :])
            p = p.astype(BF16)
            if dil == 1:
                pa_ref[:, cols] = p
            else:
                for r in range(dil):
                    pa_ref[0, r, :, cols] = p[r * rows:(r + 1) * rows, :]

    f = _dot(h, wf_ref[...]) + bf_ref[...]
    logf = jnp.minimum(f, 0.0) - jnp.log1p(jnp.exp(-jnp.abs(f)))
    tri = tri_ref[...]
    lh, lm, ll = _split3(logf)
    cum = _dot(tri, lh.astype(BF16)) + _dot(tri, lm.astype(BF16)) + _dot(tri, ll.astype(BF16))

    @pl.when(pl.program_id(0) % tiles_per_seq == 0)
    def _():
        carry_ref[...] = jnp.zeros_like(carry_ref)

    cum = cum + carry_ref[0:1, :]
    carry_ref[0:1, :] = cum[tm - 1:tm, :]
    cb_ref[0] = jnp.concatenate([cum[0:1, :], cum[tm - 1:tm, :], jnp.zeros((6, LANES), F32)], axis=0)
    ch, cm, cl = _split3(cum)

    j = lax.broadcasted_iota(jnp.int32, (tm, HEAD_DIM), 1)

    def ext_cols(vals):
        out = jnp.zeros((tm, HEAD_DIM), F32)
        ones = [pos for pos, val in enumerate(vals) if isinstance(val, float)]
        if ones:
            is_one = functools.reduce(jnp.logical_or, [j == pos for pos in ones])
            out = jnp.where(is_one, 1.0, out)
        for pos, val in enumerate(vals):
            if not isinstance(val, float):
                out = jnp.where(j == pos, val, out)
        return out

    for c in range(QK_B // GROUP_W):
        wcols = lambda part: slice(part * QK_B + c * GROUP_W, part * QK_B + (c + 1) * GROUP_W)
        pq = headnorm(_dot(h, wb_ref[:, wcols(0)]), gain_b_ref[0:1, :])
        pk = headnorm(_dot(h, wb_ref[:, wcols(1)]), gain_b_ref[1:2, :])
        pv = _dot(h, wb_ref[:, wcols(2)])
        r = _dot((pq * pk).astype(BF16), bd) * HEAD_DIM + shift_ref[...]
        for hh in range(HEADS_PER_GROUP):
            head = c * HEADS_PER_GROUP + hh
            lanes = slice(hh * HEAD_DIM, (hh + 1) * HEAD_DIM)
            col = lambda a, idx: a[:, idx:idx + 1]
            cs = [col(ch, head), col(cm, head), col(cl, head)]
            ext_q = ext_cols(cs + [1.0] * 3 + [-col(r, hh * HEAD_DIM)])
            ext_k = ext_cols([1.0] * 3 + [-v for v in cs] + [1.0])
            ext_v = ext_cols([1.0])
            for part, (val, ext) in enumerate(((pq, ext_q), (pk, ext_k))):
                o0 = (part * N_HEADS_B + head) * LANES
                qkb_ref[:, o0:o0 + LANES] = jnp.concatenate([val[:, lanes], ext], axis=-1).astype(BF16)
            vb_ref[:, head * LANES:(head + 1) * LANES] = jnp.concatenate(
                [pv[:, lanes], ext_v], axis=-1).astype(BF16)


def _inproj(xf, g_mix, w_in, q_norm_a, k_norm_a, q_norm_b, k_norm_b, b_forget, fox_shift, seq):
    n, d = xf.shape
    tm = TM_IN
    scale = HEAD_DIM ** -0.5
    w_bf = w_in.astype(BF16)
    qkv_w = N_HEADS_A * HEAD_DIM
    wa = jnp.stack([jnp.concatenate(
        [w_bf[:, part * qkv_w + g * GROUP_W: part * qkv_w + (g + 1) * GROUP_W] for part in range(3)],
        axis=1) for g in range(len(DIL_GROUPS))])
    wb = w_bf[:, WIDTH_A:WIDTH_A + WIDTH_B]
    wf = jnp.pad(w_bf[:, WIDTH_A + WIDTH_B:], ((0, 0), (0, LANES - N_HEADS_B)))
    bfp = jnp.pad(b_forget.astype(F32), (0, LANES - N_HEADS_B)).reshape(1, LANES)
    seg = np.arange(GROUP_W) // HEAD_DIM
    bd = jnp.asarray((seg[:, None] == seg[None, :]).astype(np.float32) / HEAD_DIM, BF16)
    tri = jnp.asarray(np.tril(np.ones((tm, tm), np.float32)), BF16)
    gain_a = jnp.stack([jnp.tile(q_norm_a, HEADS_PER_GROUP) * scale, jnp.tile(k_norm_a, HEADS_PER_GROUP)])
    gain_b = jnp.stack([jnp.tile(q_norm_b, HEADS_PER_GROUP) * scale, jnp.tile(k_norm_b, HEADS_PER_GROUP)])
    const = lambda shape: pl.BlockSpec(shape, lambda i: (0,) * len(shape))
    tps = seq // tm
    batch = n // seq
    qkv3 = 3 * GROUP_W
    (_, d1), (_, d2) = DIL_GROUPS[1], DIL_GROUPS[2]
    return pl.pallas_call(
        functools.partial(_inproj_kernel, tiles_per_seq=tps),
        grid=(n // tm,),
        in_specs=[
            pl.BlockSpec((tm, d), lambda i: (i, 0)),
            const((1, d)), const(wa.shape), const(wb.shape), const(wf.shape),
            const(bd.shape), const(tri.shape), const(gain_a.shape), const(gain_b.shape),
            const(bfp.shape), const((1, 1)),
        ],
        out_specs=[
            pl.BlockSpec((tm, qkv3), lambda i: (i, 0)),
            pl.BlockSpec((1, d1, tm // d1, qkv3), lambda i: (i // tps, 0, i % tps, 0)),
            pl.BlockSpec((1, d2, tm // d2, qkv3), lambda i: (i // tps, 0, i % tps, 0)),
            pl.BlockSpec((tm, 2 * N_HEADS_B * LANES), lambda i: (i, 0)),
            pl.BlockSpec((tm, N_HEADS_B * LANES), lambda i: (i, 0)),
            pl.BlockSpec((1, 8, LANES), lambda i: (i, 0, 0)),
        ],
        out_shape=[
            jax.ShapeDtypeStruct((n, qkv3), BF16),
            jax.ShapeDtypeStruct((batch, d1, seq // d1, qkv3), BF16),
            jax.ShapeDtypeStruct((batch, d2, seq // d2, qkv3), BF16),
            jax.ShapeDtypeStruct((n, 2 * N_HEADS_B * LANES), BF16),
            jax.ShapeDtypeStruct((n, N_HEADS_B * LANES), BF16),
            jax.ShapeDtypeStruct((n // tm, 8, LANES), F32),
        ],
        scratch_shapes=[pltpu.VMEM((8, LANES), F32), pltpu.VMEM((d // LANES, tm, LANES), F32)],
        compiler_params=pltpu.CompilerParams(
            dimension_semantics=("arbitrary",), vmem_limit_bytes=VMEM_LIMIT),
        name="inproj",
    )(xf, g_mix.reshape(1, d), wa, wb, wf, bd, tri, gain_a, gain_b, bfp, fox_shift.reshape(1, 1))


def _dilated_kernel(p0_ref, h0_ref, p1_ref, h1_ref, p2_ref, h2_ref, bias_ref, o_ref, acc_ref, lse_ref):
    tq = WIN_J
    first_sb = pl.program_id(1) == 0
    lane_head = lax.broadcasted_iota(jnp.int32, (tq, GROUP_W), 1) // HEAD_DIM
    prev_col = lax.broadcasted_iota(jnp.int32, (tq, 2 * tq), 1) < tq
    qc, kc_, vc_ = (slice(0, GROUP_W), slice(GROUP_W, 2 * GROUP_W), slice(2 * GROUP_W, 3 * GROUP_W))

    def attend(g, q, kp, kc, vp, vc, no_prev):
        kcat = jnp.concatenate([kp, kc], axis=0)
        vcat = jnp.concatenate([vp, vc], axis=0)
        dead = jnp.logical_and(no_prev, prev_col)
        q4 = jnp.concatenate([jnp.where(lane_head == hh, q, jnp.zeros_like(q))
                              for hh in range(HEADS_PER_GROUP)], axis=0)
        s = _dot_nt(q4, kcat) + bias_ref[g].reshape(HEADS_PER_GROUP * tq, 2 * tq)
        s = jnp.where(jnp.concatenate([dead] * HEADS_PER_GROUP, axis=0), NEG, s)
        m = jnp.max(s, axis=-1, keepdims=True)
        p = jnp.exp(s - m)
        l = jnp.sum(p, axis=-1, keepdims=True)
        o4 = _dot(p.astype(BF16), vcat) * (1.0 / l)
        lse4 = m + jnp.log(l)
        acc = o4[0:tq]
        lse = jnp.broadcast_to(lse4[0:tq], (tq, GROUP_W))
        for hh in range(1, HEADS_PER_GROUP):
            sel = lane_head == hh
            acc = jnp.where(sel, o4[hh * tq:(hh + 1) * tq], acc)
            lse = jnp.where(sel, lse4[hh * tq:(hh + 1) * tq], lse)
        return acc, lse

    n_half = GROUP_W // LANES

    def merge(rows, acc, lse):
        for c in range(n_half):
            lanes = slice(c * LANES, (c + 1) * LANES)
            l1 = lse_ref[c, rows, :]
            mx = jnp.maximum(l1, lse[:, lanes])
            w1 = jnp.exp(l1 - mx)
            w2 = jnp.exp(lse[:, lanes] - mx)
            den = w1 + w2
            acc_ref[c, rows, :] = (w1 * acc_ref[c, rows, :] + w2 * acc[:, lanes]) / den
            lse_ref[c, rows, :] = mx + jnp.log(den)

    def pick(first, halo, body):
        return jnp.where(first, halo, body)

    def loop(n, body):
        def trip(i, carry):
            for u in range(DIL_UNROLL):
                body(i * DIL_UNROLL + u, carry)
            return carry
        lax.fori_loop(0, n // DIL_UNROLL, trip, 0)

    def body0(j, carry):
        st = pl.multiple_of(j * tq, tq)
        pst = pl.multiple_of(jnp.maximum(j - 1, 0) * tq, tq)
        cur, prv = pl.ds(st, tq), pl.ds(pst, tq)
        acc, lse = attend(
            0, p0_ref[0, cur, qc],
            pick(j == 0, h0_ref[0, :, kc_], p0_ref[0, prv, kc_]), p0_ref[0, cur, kc_],
            pick(j == 0, h0_ref[0, :, vc_], p0_ref[0, prv, vc_]), p0_ref[0, cur, vc_],
            jnp.logical_and(j == 0, first_sb))
        for c in range(n_half):
            acc_ref[c, cur, :] = acc[:, c * LANES:(c + 1) * LANES]
            lse_ref[c, cur, :] = lse[:, c * LANES:(c + 1) * LANES]
        return carry

    loop(SUPER // tq, body0)

    d1 = DIL_GROUPS[1][1]
    nsub1 = SUPER // d1 // tq
    def body1(t, carry):
        r, ii = t // nsub1, t % nsub1
        st = pl.multiple_of(ii * tq, tq)
        pst = pl.multiple_of(jnp.maximum(ii - 1, 0) * tq, tq)
        cur, prv = pl.ds(st, tq), pl.ds(pst, tq)
        acc, lse = attend(
            1, p1_ref[0, r, cur, qc],
            pick(ii == 0, h1_ref[0, r, :, kc_], p1_ref[0, r, prv, kc_]), p1_ref[0, r, cur, kc_],
            pick(ii == 0, h1_ref[0, r, :, vc_], p1_ref[0, r, prv, vc_]), p1_ref[0, r, cur, vc_],
            jnp.logical_and(ii == 0, first_sb))
        merge(pl.ds(ii * (tq * d1) + r, tq, stride=d1), acc, lse)
        return carry

    loop(d1 * nsub1, body1)

    d2 = DIL_GROUPS[2][1]

    def body2(r, carry):
        acc, lse = attend(2, p2_ref[0, r, :, qc], h2_ref[0, r, :, kc_], p2_ref[0, r, :, kc_],
                          h2_ref[0, r, :, vc_], p2_ref[0, r, :, vc_], first_sb)
        merge(pl.ds(r, tq, stride=d2), acc, lse)
        return carry

    loop(d2, body2)

    for c in range(n_half):
        o_ref[0, :, c * LANES:(c + 1) * LANES] = acc_ref[c].astype(o_ref.dtype)


def _rel_bucket(dist):
    max_exact = REL_BUCKETS // 2
    n = jnp.maximum(dist.astype(F32), 1.0)
    large = max_exact + (jnp.log(n / max_exact) / math.log(REL_MAX_DIST / max_exact)
                         * (REL_BUCKETS - max_exact)).astype(jnp.int32)
    large = jnp.minimum(large, REL_BUCKETS - 1)
    return jnp.where(dist < max_exact, dist, large)


def _toeplitz_bias(rel_bias, g, dil):
    tq = WIN_J
    offs = dil * (WIN_J - jnp.arange(WIN_J + 1, dtype=jnp.int32))
    hs = slice(g * HEADS_PER_GROUP, (g + 1) * HEADS_PER_GROUP)
    tab_rev = rel_bias[_rel_bucket(offs)][:, hs].T.astype(F32)
    period = 3 * tq
    neg = lambda w: jnp.full((HEADS_PER_GROUP, w), NEG, F32)
    vec = jnp.concatenate([neg(tq - 1), tab_rev, neg(period - 2 * tq)], axis=1)
    flat = jnp.broadcast_to(vec[:, None, :], (HEADS_PER_GROUP, tq, period)).reshape(HEADS_PER_GROUP, -1)
    skew = flat[:, :tq * (period - 1)].reshape(HEADS_PER_GROUP, tq, period - 1)
    return skew[:, :, tq - 1:3 * tq - 1]


def _dilated(pa0, pa1, pa2, bias, batch, seq):
    tq = WIN_J
    qkv3 = 3 * GROUP_W
    (_, d1), (_, d2) = DIL_GROUPS[1], DIL_GROUPS[2]
    nsb = seq // SUPER
    p0 = pa0.reshape(batch, seq, qkv3)
    prev_blk = lambda per_sb: (lambda b, s: jnp.maximum(s * per_sb - 1, 0))
    h0i, h1i, h2i = prev_blk(SUPER // tq), prev_blk(SUPER // d1 // tq), prev_blk(SUPER // d2 // tq)
    out = pl.pallas_call(
        _dilated_kernel,
        grid=(batch, nsb),
        in_specs=[
            pl.BlockSpec((1, SUPER, qkv3), lambda b, s: (b, s, 0)),
            pl.BlockSpec((1, tq, qkv3), lambda b, s: (b, h0i(b, s), 0)),
            pl.BlockSpec((1, d1, SUPER // d1, qkv3), lambda b, s: (b, 0, s, 0)),
            pl.BlockSpec((1, d1, tq, qkv3), lambda b, s: (b, 0, h1i(b, s), 0)),
            pl.BlockSpec((1, d2, SUPER // d2, qkv3), lambda b, s: (b, 0, s, 0)),
            pl.BlockSpec((1, d2, tq, qkv3), lambda b, s: (b, 0, h2i(b, s), 0)),
            pl.BlockSpec(bias.shape, lambda b, s: (0, 0, 0, 0)),
        ],
        out_specs=pl.BlockSpec((1, SUPER, GROUP_W), lambda b, s: (b, s, 0)),
        out_shape=jax.ShapeDtypeStruct((batch, seq, GROUP_W), BF16),
        scratch_shapes=[pltpu.VMEM((GROUP_W // LANES, SUPER, LANES), F32)] * 2,
        compiler_params=pltpu.CompilerParams(
            dimension_semantics=("arbitrary", "arbitrary"), vmem_limit_bytes=VMEM_LIMIT),
        name="dilated",
    )(p0, p0, pa1, pa1, pa2, pa2, bias)
    return out.reshape(batch * seq, GROUP_W)


def _fox_kernel(nlive_sm, q_ref, k_ref, v_ref, o_ref, m_ref, acc_ref, *, online):
    tq = q_ref.shape[1]
    half = tq // 2
    qi = pl.program_id(2)
    step = (pl.program_id(0) * pl.num_programs(1) + pl.program_id(1)) * pl.num_programs(2) + qi
    row = lax.broadcasted_iota(jnp.int32, (half, half), 0)
    col = lax.broadcasted_iota(jnp.int32, (half, half), 1)
    causal = row >= col

    def attend(hh, rows, state, start, nkeys, masked):
        m, acc = state
        lanes = slice(hh * LANES, (hh + 1) * LANES)
        s = _dot_nt(q_ref[0, rows, lanes], k_ref[0, pl.ds(start, nkeys), lanes])
        if masked:
            s = jnp.where(causal, s, NEG)
        if online:
            m_new = jnp.maximum(m, jnp.max(s, axis=-1, keepdims=True))
            acc = acc * jnp.exp(m - m_new)
            s = s - m_new
            m = m_new
        return m, acc + _dot(jnp.exp(s).astype(BF16), v_ref[0, pl.ds(start, nkeys), lanes])

    first = [qi - nlive_sm[2 * step + hh] for hh in range(2)]
    for hh in range(2):
        m_ref[hh] = jnp.full((tq, 1), NEG, F32)
        acc_ref[hh] = jnp.zeros((tq, LANES), F32)

    def full_chunk(ki, carry):
        start = pl.multiple_of(ki * tq, tq)
        for hh in range(2):
            @pl.when(ki >= first[hh])
            def _(hh=hh):
                m, acc = attend(hh, slice(None), (m_ref[hh], acc_ref[hh]), start, tq, False)
                acc_ref[hh] = acc
                if online:
                    m_ref[hh] = m
        return carry

    lax.fori_loop(jnp.minimum(first[0], first[1]), qi, full_chunk, 0)

    d0 = pl.multiple_of(qi * tq, tq)
    outs = []
    for hh in range(2):
        m, acc = m_ref[hh], acc_ref[hh]
        top, bot = slice(0, half), slice(half, tq)
        s_top = attend(hh, top, (m[top], acc[top]), d0, half, True)
        s_bot = attend(hh, bot, (m[bot], acc[bot]), d0, half, False)
        s_bot = attend(hh, bot, s_bot, d0 + half, half, True)
        a = jnp.concatenate([s_top[1], s_bot[1]], axis=0)
        outs.append(a[:, :HEAD_DIM] / a[:, HEAD_DIM:HEAD_DIM + 1])
    o_ref[0] = jnp.concatenate(outs, axis=-1).astype(o_ref.dtype)


def _fox_live_chunks(cb, top, batch, seq):
    tps = seq // TM_IN
    per = TQ_FOX // TM_IN
    nq = seq // TQ_FOX
    c_first = cb[:, 0, :N_HEADS_B].reshape(batch, tps, N_HEADS_B)[:, ::per]
    c_last = cb[:, 1, :N_HEADS_B].reshape(batch, tps, N_HEADS_B)[:, per - 1::per]
    live = (top + c_first[:, :, None, :] - c_last[:, None, :, :]) >= FOX_DEAD_EXPONENT
    back = jnp.arange(nq)[:, None] - jnp.arange(nq)[None, :]
    reach = jnp.max(jnp.where(jnp.logical_and(live, (back > 0)[None, :, :, None]),
                              back[None, :, :, None], 0), axis=2)
    reach = reach.reshape(batch, nq, N_HEADS_B // 2, 2).transpose(0, 2, 1, 3)
    return reach.reshape(-1).astype(jnp.int32)


def _fox(qkb, vb, cb, top, online, batch, seq):
    tq = TQ_FOX
    pairs = N_HEADS_B // 2
    nq = seq // tq
    qkv = qkb.reshape(batch, seq, 2 * N_HEADS_B * LANES)
    vv = vb.reshape(batch, seq, N_HEADS_B * LANES)
    all_chunks = jnp.tile(jnp.repeat(jnp.arange(nq, dtype=jnp.int32), 2), batch * pairs)

    def call(is_online, nlive):
        grid_spec = pltpu.PrefetchScalarGridSpec(
            num_scalar_prefetch=1,
            grid=(batch, pairs, nq),
            in_specs=[
                pl.BlockSpec((1, tq, 2 * LANES), lambda b, p, i, n: (b, i, p)),
                pl.BlockSpec((1, seq, 2 * LANES), lambda b, p, i, n: (b, 0, pairs + p)),
                pl.BlockSpec((1, seq, 2 * LANES), lambda b, p, i, n: (b, 0, p)),
            ],
            out_specs=pl.BlockSpec((1, tq, LANES), lambda b, p, i, n: (b, i, p)),
            scratch_shapes=[pltpu.VMEM((2, tq, 1), F32), pltpu.VMEM((2, tq, LANES), F32)],
        )
        return pl.pallas_call(
            functools.partial(_fox_kernel, online=is_online),
            grid_spec=grid_spec,
            out_shape=jax.ShapeDtypeStruct((batch, seq, OUT_B), BF16),
            compiler_params=pltpu.CompilerParams(
                dimension_semantics=("arbitrary", "arbitrary", "arbitrary"), vmem_limit_bytes=VMEM_LIMIT),
            name="fox_online" if is_online else "fox",
        )(nlive, qkv, qkv, vv)

    out = lax.cond(online, lambda: call(True, all_chunks),
                   lambda: call(False, _fox_live_chunks(cb, top, batch, seq)))
    return out.reshape(batch * seq, OUT_B)


def _fox_shift(q_norm_b, k_norm_b):
    bound = HEAD_DIM * (HEAD_DIM ** -0.5) * jnp.max(jnp.abs(q_norm_b)) * jnp.max(jnp.abs(k_norm_b))
    shift = jnp.maximum(2.0 * FOX_ROUNDING_SLACK * bound - FOX_EXP_HEADROOM, 0.0).astype(F32)
    top = 2.0 * FOX_ROUNDING_SLACK * bound - shift
    return shift, top.astype(F32), shift > FOX_MAX_SHIFT


def _post_kernel(x_ref, ya_ref, yb_ref, gmix_ref, wg_ref, bg_ref, wpa_ref, wpb_ref, wo_ref,
                 gffn_ref, wr_ref, rb_ref, x1_ref, h2_ref, topi_ref, topw_ref):
    d = x_ref.shape[1]
    x = x_ref[...]
    h = (x * lax.rsqrt(jnp.mean(x * x, axis=-1, keepdims=True) + EPS) * gmix_ref[...]).astype(BF16)
    gates = jax.nn.sigmoid(_dot(h, wg_ref[...]) + bg_ref[...])
    merged = gates[:, :d] * _dot(ya_ref[...], wpa_ref[...]) + gates[:, d:] * _dot(yb_ref[...], wpb_ref[...])
    x1 = x + _dot(merged.astype(BF16), wo_ref[...])
    x1_ref[...] = x1
    h2 = x1 * lax.rsqrt(jnp.mean(x1 * x1, axis=-1, keepdims=True) + EPS) * gffn_ref[...]
    h2_ref[...] = h2.astype(BF16)

    hh, hm, _ = _split3(h2)
    wr = wr_ref[...]
    wh = wr.astype(BF16)
    wl = (wr - wh.astype(F32)).astype(BF16)
    hh, hm = hh.astype(BF16), hm.astype(BF16)
    logits = _dot_nt(wh, hh) + _dot_nt(wh, hm) + _dot_nt(wl, hh)
    scores = jax.nn.sigmoid(logits)
    biased = scores + rb_ref[...]
    eid = lax.broadcasted_iota(jnp.int32, scores.shape, 0).astype(F32)
    chosen = jnp.zeros(scores.shape, jnp.bool_)
    idx, val = [], []
    for _ in range(TOP_K):
        cur = jnp.where(chosen, -jnp.inf, biased)
        mx = jnp.max(cur, axis=0, keepdims=True)
        first = jnp.min(jnp.where(cur == mx, eid, float(N_EXPERTS)), axis=0, keepdims=True)
        pick = eid == first
        chosen = jnp.logical_or(chosen, pick)
        idx.append(first)
        val.append(jnp.sum(jnp.where(pick, scores, 0.0), axis=0, keepdims=True))
    top_s = jnp.concatenate(val, axis=0)
    top_w = top_s / jnp.sum(top_s, axis=0, keepdims=True) * ROUTE_SCALE
    tm = scores.shape[1]
    both = jnp.concatenate(idx + [top_w, jnp.zeros((LANES - 2 * TOP_K, tm), F32)], axis=0).T
    topi_ref[...] = both[:, :TOP_K].astype(jnp.int32)
    topw_ref[...] = both[:, TOP_K:2 * TOP_K]


def _post(xf, ya, yb, g_mix, w_gate, b_gate, w_proj_a, w_proj_b, w_out, g_ffn, w_router, router_bias):
    n, d = xf.shape
    tm = TM_POST
    const = lambda shape: pl.BlockSpec(shape, lambda i: (0,) * len(shape))
    row = lambda w: pl.BlockSpec((tm, w), lambda i: (i, 0))
    args = [xf, ya, yb, g_mix.reshape(1, d), w_gate.astype(BF16), b_gate.reshape(1, 2 * d),
            w_proj_a.astype(BF16), w_proj_b.astype(BF16), w_out.astype(BF16), g_ffn.reshape(1, d),
            w_router.astype(F32).T, router_bias.astype(F32).reshape(N_EXPERTS, 1)]
    in_specs = [row(d), row(OUT_A), row(OUT_B)] + [const(a.shape) for a in args[3:]]
    return pl.pallas_call(
        _post_kernel,
        grid=(n // tm,),
        in_specs=in_specs,
        out_specs=[row(d), row(d), row(TOP_K), row(TOP_K)],
        out_shape=[jax.ShapeDtypeStruct((n, d), F32), jax.ShapeDtypeStruct((n, d), BF16),
                   jax.ShapeDtypeStruct((n, TOP_K), jnp.int32), jax.ShapeDtypeStruct((n, TOP_K), F32)],
        compiler_params=pltpu.CompilerParams(
            dimension_semantics=("arbitrary",), vmem_limit_bytes=VMEM_LIMIT),
        name="post",
    )(*args)


def _dispatch_kernel(h2_ref, topi_ref, tri_ref, upper_ref, xs_ref, slots_ref, cnt_ref, off_ref):
    tb = h2_ref.shape[0]
    topi = topi_ref[...]
    lane = lax.broadcasted_iota(jnp.int32, (tb, N_EXPERTS), 1)
    picks = [lane == topi[:, k:k + 1] for k in range(TOP_K)]
    mask = picks[0]
    for pk in picks[1:]:
        mask = jnp.logical_or(mask, pk)
    maskf = jnp.where(mask, 1.0, 0.0)
    rank = _dot(tri_ref[...], maskf.astype(BF16))
    cnt = jnp.sum(maskf, axis=0, keepdims=True)
    gran = jnp.floor((cnt + (GRAN - 1)) * (1.0 / GRAN))
    goff = _dot(jnp.broadcast_to(gran, (8, N_EXPERTS)).astype(BF16), upper_ref[...])[0:1]
    off = goff * GRAN
    slot_te = off + rank
    slots = jnp.concatenate(
        [jnp.sum(jnp.where(pk, slot_te, 0.0), axis=-1, keepdims=True) for pk in picks], axis=1)
    slots_ref[...] = slots.astype(jnp.int32)
    cnt_ref[0] = cnt.astype(jnp.int32)
    off_ref[0] = off.astype(jnp.int32)
    v = jnp.where(mask, slot_te, float(NO_SLOT))
    v_hi = jnp.floor(v * (1.0 / 64.0))
    w = jnp.concatenate([v_hi * 64.0, v - v_hi * 64.0], axis=1).T.astype(BF16)
    end = off + gran * GRAN
    used = jnp.max(end).astype(jnp.int32)
    h2 = h2_ref[...]

    def sort_chunk(c):
        s_e = (lax.broadcasted_iota(jnp.int32, (SLOT_CHUNK, N_EXPERTS), 0) + c * SLOT_CHUNK).astype(F32)
        own = jnp.where(jnp.logical_and(s_e >= off, s_e < end), 1.0, 0.0)
        looked = _dot(jnp.concatenate([own, own], axis=1).astype(BF16), w)
        s_t = (lax.broadcasted_iota(jnp.int32, (SLOT_CHUNK, tb), 0) + c * SLOT_CHUNK).astype(F32)
        onehot = jnp.where(looked == s_t, 1.0, 0.0).astype(BF16)
        xs_ref[0, c * SLOT_CHUNK:(c + 1) * SLOT_CHUNK, :] = _dot(onehot, h2).astype(BF16)

    for c in range(CAP // SLOT_CHUNK):
        if c < CHUNKS_TYPICAL:
            sort_chunk(c)
        else:
            @pl.when(c * SLOT_CHUNK < used)
            def _(c=c):
                sort_chunk(c)

            @pl.when(c * SLOT_CHUNK >= used)
            def _(c=c):
                xs_ref[0, c * SLOT_CHUNK:(c + 1) * SLOT_CHUNK, :] = jnp.zeros((SLOT_CHUNK, xs_ref.shape[2]), BF16)


def _dispatch(h2, topi):
    n, d = h2.shape
    nb = n // TB
    tri = jnp.asarray(np.tril(np.ones((TB, TB), np.float32), -1), BF16)
    upper = jnp.asarray(np.triu(np.ones((N_EXPERTS, N_EXPERTS), np.float32), 1), BF16)
    const = lambda shape: pl.BlockSpec(shape, lambda i: (0,) * len(shape))
    meta = pl.BlockSpec((1, 1, N_EXPERTS), lambda i: (i, 0, 0))
    return pl.pallas_call(
        _dispatch_kernel,
        grid=(nb,),
        in_specs=[pl.BlockSpec((TB, d), lambda i: (i, 0)), pl.BlockSpec((TB, TOP_K), lambda i: (i, 0)),
                  const(tri.shape), const(upper.shape)],
        out_specs=[pl.BlockSpec((1, CAP, d), lambda i: (i, 0, 0)),
                   pl.BlockSpec((TB, TOP_K), lambda i: (i, 0)), meta, meta],
        out_shape=[jax.ShapeDtypeStruct((nb, CAP, d), BF16), jax.ShapeDtypeStruct((n, TOP_K), jnp.int32),
                   jax.ShapeDtypeStruct((nb, 1, N_EXPERTS), jnp.int32),
                   jax.ShapeDtypeStruct((nb, 1, N_EXPERTS), jnp.int32)],
        compiler_params=pltpu.CompilerParams(
            dimension_semantics=("arbitrary",), vmem_limit_bytes=VMEM_LIMIT),
        name="dispatch",
    )(h2, topi, tri, upper)


def _ffn_kernel(item_e_sm, item_g0_sm, item_n_sm, glist_sm, xs_hbm, wg_ref, wu_ref, wd_ref, ys_hbm,
                xbuf, ybuf, wg_bf, wu_bf, wd_bf, sem_in, sem_out):
    step = pl.program_id(0)
    nsteps = pl.num_programs(0)
    buf = step % 2

    def for_granules(st, fn):
        g0 = item_g0_sm[st]
        n = item_n_sm[st]

        def per_granule(j, carry):
            fn(glist_sm[g0 + j], j)
            return carry

        lax.fori_loop(0, n, per_granule, 0)
        return n

    def fetch(b_):
        return lambda src, dst: pltpu.make_async_copy(xs_hbm.at[src], xbuf.at[b_, dst], sem_in.at[b_])

    def writeback(b_):
        return lambda src, dst: pltpu.make_async_copy(ybuf.at[b_, dst], ys_hbm.at[src], sem_out.at[b_])

    def start(mk):
        return lambda src, dst: mk(src, dst).start()

    def wait_all(st, span):
        n = item_n_sm[st]
        size = PASS_GRAN
        while size >= 1:
            @pl.when((n & size) != 0)
            def _(size=size):
                span(size).wait()
            size //= 2
        return n

    def fetch_span(b_):
        return lambda k: pltpu.make_async_copy(
            xs_hbm.at[pl.ds(0, k)], xbuf.at[b_, pl.ds(0, k)], sem_in.at[b_])

    def writeback_span(b_):
        return lambda k: pltpu.make_async_copy(
            ybuf.at[b_, pl.ds(0, k)], ys_hbm.at[pl.ds(0, k)], sem_out.at[b_])

    @pl.when(step == 0)
    def _():
        xbuf[...] = jnp.zeros_like(xbuf)
        for_granules(step, start(fetch(0)))

    @pl.when(step + 1 < nsteps)
    def _():
        for_granules(step + 1, start(fetch(1 - buf)))

    ngran = wait_all(step, fetch_span(buf))

    @pl.when(step >= 2)
    def _():
        wait_all(step - 2, writeback_span(buf))

    @pl.when(ngran > 0)
    def _():
        wg_bf[...] = wg_ref[0].astype(BF16)
        wu_bf[...] = wu_ref[0].astype(BF16)
        wd_bf[...] = wd_ref[0].astype(BF16)

    x_cols = xbuf.shape[-1]

    def ffn_rows(base, rows):
        grans = pl.ds(pl.multiple_of(base // GRAN, rows // GRAN), rows // GRAN)
        x = xbuf[buf, grans].reshape(rows, x_cols)
        g = _dot(x, wg_bf[...])
        u = _dot(x, wu_bf[...])
        mid = (g * jax.nn.sigmoid(g) * u).astype(BF16)
        ybuf[buf, grans] = _dot(mid, wd_bf[...]).astype(BF16).reshape(rows // GRAN, GRAN, x_cols)

    nt = (ngran * GRAN + (FT - 1)) // FT
    big = FT_BIG // FT

    def big_tile(i, carry):
        ffn_rows(pl.multiple_of(i * FT_BIG, FT_BIG), FT_BIG)
        return carry

    lax.fori_loop(0, nt // big, big_tile, 0)
    size = big // 2
    while size >= 1:
        @pl.when((nt & size) != 0)
        def _(size=size):
            ffn_rows(pl.multiple_of((nt & ~(2 * size - 1)) * FT, size * FT), size * FT)
        size //= 2

    for_granules(step, start(writeback(buf)))

    @pl.when(step == nsteps - 1)
    def _():
        wait_all(step, writeback_span(buf))

        @pl.when(step >= 1)
        def _():
            wait_all(step - 1, writeback_span(1 - buf))


def _work_items(cnt, off):
    nb = cnt.shape[0]
    seg_n = ((cnt.reshape(nb, N_EXPERTS) + (GRAN - 1)) // GRAN).T.reshape(-1)
    seg_row = ((off.reshape(nb, N_EXPERTS) + jnp.arange(nb, dtype=jnp.int32)[:, None] * CAP) // GRAN).T.reshape(-1)
    seg_end = jnp.cumsum(seg_n)
    seg_start = seg_end - seg_n
    gmax = nb * (TB * TOP_K // GRAN + N_EXPERTS)
    prev_end = jnp.concatenate([jnp.ones((1,), jnp.int32), (seg_row + seg_n)[:-1]])
    steps = jnp.ones((gmax,), jnp.int32).at[seg_start].add(seg_row - prev_end, mode="drop")
    glist = jnp.cumsum(steps)
    per_e = seg_n.reshape(N_EXPERTS, nb).sum(axis=1)
    first_e = seg_start[::nb]
    passes = (per_e + (PASS_GRAN - 1)) // PASS_GRAN
    pass_end = jnp.cumsum(passes)
    n_items = N_EXPERTS + gmax // PASS_GRAN
    w = jnp.arange(n_items, dtype=jnp.int32)
    item_e = jnp.minimum(jnp.sum(pass_end[None, :] <= w[:, None], axis=1), N_EXPERTS - 1).astype(jnp.int32)
    done = (w - (pass_end - passes)[item_e]) * PASS_GRAN
    item_n = jnp.clip(per_e[item_e] - done, 0, PASS_GRAN)
    item_g0 = first_e[item_e] + done
    return item_e, item_g0.astype(jnp.int32), item_n.astype(jnp.int32), glist.astype(jnp.int32)


def _ffn(xs, cnt, off, wg, wu, wd):
    _, _, d = xs.shape
    item_e, item_g0, item_n, glist = _work_items(cnt, off)
    per_expert = lambda shape: pl.BlockSpec((1,) + shape, lambda w, ie, g0, n, gl: (ie[w], 0, 0))
    grid_spec = pltpu.PrefetchScalarGridSpec(
        num_scalar_prefetch=4,
        grid=(item_e.shape[0],),
        in_specs=[pl.BlockSpec(memory_space=pl.ANY), per_expert((d, D_EXPERT)), per_expert((d, D_EXPERT)),
                  per_expert((D_EXPERT, d))],
        out_specs=pl.BlockSpec(memory_space=pl.ANY),
        scratch_shapes=[pltpu.VMEM((2, PASS_GRAN, GRAN, d), BF16)] * 2 + [
                        pltpu.VMEM((d, D_EXPERT), BF16), pltpu.VMEM((d, D_EXPERT), BF16),
                        pltpu.VMEM((D_EXPERT, d), BF16),
                        pltpu.SemaphoreType.DMA((2,)), pltpu.SemaphoreType.DMA((2,))],
    )
    return pl.pallas_call(
        _ffn_kernel,
        grid_spec=grid_spec,
        out_shape=jax.ShapeDtypeStruct(xs.shape, xs.dtype),
        input_output_aliases={4: 0},
        compiler_params=pltpu.CompilerParams(
            dimension_semantics=("arbitrary",), vmem_limit_bytes=VMEM_LIMIT),
        name="ffn",
    )(item_e, item_g0, item_n, glist, xs, wg, wu, wd)


def _combine_kernel(used_sm, x1_ref, h2_ref, ys_ref, slots_ref, topw_ref, wgus_ref, wds_ref, o_ref):
    tb = x1_ref.shape[0]
    gu = _dot(h2_ref[...], wgus_ref[...])
    g, u = gu[:, :D_SHARED], gu[:, D_SHARED:]
    acc = x1_ref[...] + _dot((g * jax.nn.sigmoid(g) * u).astype(BF16), wds_ref[...])
    slots = slots_ref[...].astype(F32)
    topw = topw_ref[...]
    used = used_sm[pl.program_id(0)]

    def gather_chunk(c):
        scol = (lax.broadcasted_iota(jnp.int32, (tb, SLOT_CHUNK), 1) + c * SLOT_CHUNK).astype(F32)
        gate = jnp.zeros((tb, SLOT_CHUNK), F32)
        for k in range(TOP_K):
            gate = gate + jnp.where(scol == slots[:, k:k + 1], topw[:, k:k + 1], 0.0)
        return _dot(gate.astype(BF16), ys_ref[0, c * SLOT_CHUNK:(c + 1) * SLOT_CHUNK, :])

    for c in range(CHUNKS_TYPICAL):
        acc = acc + gather_chunk(c)
    o_ref[...] = acc
    for c in range(CHUNKS_TYPICAL, CAP // SLOT_CHUNK):
        @pl.when(c * SLOT_CHUNK < used)
        def _(c=c):
            o_ref[...] += gather_chunk(c)


def _combine(x1, h2, ys, slots, topw, used, wgus, wds):
    n, d = x1.shape
    const = lambda shape: pl.BlockSpec(shape, lambda i, u: (0,) * len(shape))
    row = lambda w: pl.BlockSpec((TB, w), lambda i, u: (i, 0))
    grid_spec = pltpu.PrefetchScalarGridSpec(
        num_scalar_prefetch=1,
        grid=(n // TB,),
        in_specs=[row(d), row(d), pl.BlockSpec((1, CAP, d), lambda i, u: (i, 0, 0)), row(TOP_K), row(TOP_K),
                  const(wgus.shape), const(wds.shape)],
        out_specs=row(d),
    )
    return pl.pallas_call(
        _combine_kernel,
        grid_spec=grid_spec,
        out_shape=jax.ShapeDtypeStruct((n, d), F32),
        compiler_params=pltpu.CompilerParams(
            dimension_semantics=("arbitrary",), vmem_limit_bytes=VMEM_LIMIT),
        name="combine",
    )(used, x1, h2, ys, slots, topw, wgus, wds)


def _moe(x1, h2, topi, topw, w_gate_e, w_up_e, w_down_e, w_gate_s, w_up_s, w_down_s):
    n, d = x1.shape
    wgus = jnp.concatenate([w_gate_s.astype(BF16), w_up_s.astype(BF16)], axis=-1)
    xs, slots, cnt, off = _dispatch(h2, topi)
    ys = _ffn(xs.reshape(-1, GRAN, d), cnt, off, w_gate_e, w_up_e, w_down_e)
    used = jnp.max(off + (cnt + (GRAN - 1)) // GRAN * GRAN, axis=(1, 2)).astype(jnp.int32)
    return _combine(x1, h2, ys.reshape(n // TB, CAP, d), slots, topw, used, wgus, w_down_s.astype(BF16))


def kernel(x, g_mix, w_in, q_norm_a, k_norm_a, q_norm_b, k_norm_b, rel_bias, b_forget, w_gate, b_gate,
           w_proj_a, w_proj_b, w_out, g_ffn, w_router, router_bias, w_gate_e, w_up_e, w_down_e,
           w_gate_s, w_up_s, w_down_s):
    batch, seq, d = x.shape
    xf = x.reshape(batch * seq, d)
    fox_shift, fox_top, fox_online = _fox_shift(q_norm_b, k_norm_b)
    pa0, pa1, pa2, qkb, vb, cb = _inproj(xf, g_mix, w_in, q_norm_a, k_norm_a, q_norm_b, k_norm_b, b_forget,
                                     fox_shift, seq)
    bias = jnp.stack([_toeplitz_bias(rel_bias, g, dil) for g, (_, dil) in enumerate(DIL_GROUPS)])
    ya = _dilated(pa0, pa1, pa2, bias, batch, seq)

    yb = _fox(qkb, vb, cb, fox_top, fox_online, batch, seq)
    x1, h2, topi, topw = _post(xf, ya, yb, g_mix, w_gate, b_gate, w_proj_a, w_proj_b, w_out, g_ffn,
                               w_router, router_bias)
    out = _moe(x1, h2, topi, topw, w_gate_e, w_up_e, w_down_e, w_gate_s, w_up_s, w_down_s)
    return out.reshape(batch, seq, d)
```

```python
import functools
import math

import jax
import jax.numpy as jnp
import numpy as np
from jax import lax
from jax.experimental import pallas as pl
from jax.experimental.pallas import tpu as pltpu

D_MODEL = 1024
HEAD_DIM = 64
DIL_GROUPS = ((128, 1), (512, 4), (2048, 16))
HEADS_PER_GROUP = 4
N_HEADS_A = HEADS_PER_GROUP * len(DIL_GROUPS)
N_HEADS_B = 8
REL_BUCKETS = 32
REL_MAX_DIST = 2048
N_EXPERTS = 64
TOP_K = 8
D_EXPERT = 256
D_SHARED = 256
ROUTE_SCALE = 2.5
EPS = 1e-6

WIDTH_A = 3 * N_HEADS_A * HEAD_DIM
WIDTH_B = 3 * N_HEADS_B * HEAD_DIM
QK_B = N_HEADS_B * HEAD_DIM
OUT_A = HEADS_PER_GROUP * HEAD_DIM
OUT_B = N_HEADS_B * HEAD_DIM

LANES = 128
GROUP_W = HEADS_PER_GROUP * HEAD_DIM
WIN_J = 128
SUPER = DIL_GROUPS[-1][1] * WIN_J
NEG = -1e30
VMEM_LIMIT = 56 * 1024 * 1024

DIL_UNROLL = 8
TM_IN = 512
TM_POST = 1024
TQ_FOX = 1024
FOX_ROUNDING_SLACK = 1.02
FOX_EXP_HEADROOM = 60.0
FOX_DEAD_EXPONENT = -105.0
FOX_MAX_SHIFT = 80.0
TB = 256
GRAN = 16
CAP = TB * TOP_K + N_EXPERTS * GRAN
SLOT_CHUNK = 512
CHUNKS_TYPICAL = -(-(TB * TOP_K + N_EXPERTS * GRAN // 2) // SLOT_CHUNK)
NO_SLOT = 4095
PASS_GRAN = 256
FT = 256
FT_BIG = 1024

BF16 = jnp.bfloat16
F32 = jnp.float32


def _dot(a, b):
    return jnp.dot(a, b, preferred_element_type=F32)


def _dot_nt(a, b):
    return lax.dot_general(a, b, (((1,), (1,)), ((), ())), preferred_element_type=F32)


def _split3(v):
    hi = v.astype(BF16).astype(F32)
    r = v - hi
    mid = r.astype(BF16).astype(F32)
    lo = (r - mid).astype(BF16).astype(F32)
    return hi, mid, lo


def _inproj_kernel(x_ref, g_ref, wa_ref, wb_ref, wf_ref, bd_ref, tri_ref, gain_a_ref, gain_b_ref,
                   bf_ref, shift_ref, pa0_ref, pa1_ref, pa2_ref, qkb_ref, vb_ref, cb_ref, carry_ref, h_ref, *,
                   tiles_per_seq):
    tm = x_ref.shape[0]
    x = x_ref[...]
    h = x * lax.rsqrt(jnp.mean(x * x, axis=-1, keepdims=True) + EPS) * g_ref[...]
    n_lane_chunks = h_ref.shape[0]
    for c in range(n_lane_chunks):
        h_ref[c] = h[:, c * LANES:(c + 1) * LANES]
    h = h.astype(BF16)
    bd = bd_ref[...]

    def headnorm(p, gain):
        ms = _dot((p * p).astype(BF16), bd)
        return p * lax.rsqrt(ms + EPS) * gain

    for g, (pa_ref, (_, dil)) in enumerate(zip((pa0_ref, pa1_ref, pa2_ref), DIL_GROUPS)):
        rows = tm // dil
        if dil == 1:
            hg = h
        else:
            hg = jnp.concatenate([jnp.concatenate(
                [h_ref[c, pl.ds(r, rows, stride=dil), :] for c in range(n_lane_chunks)], axis=1)
                for r in range(dil)], axis=0).astype(BF16)
        qkv = _dot(hg, wa_ref[g])
        for part in range(3):
            cols = slice(part * GROUP_W, (part + 1) * GROUP_W)
            p = qkv[:, cols]
            if part < 2:
                p = headnorm(p, gain_a_ref[part:part + 1, :])
            p = p.astype(BF16)
            if dil == 1:
                pa_ref[:, cols] = p
            else:
                for r in range(dil):
                    pa_ref[0, r, :, cols] = p[r * rows:(r + 1) * rows, :]

    f = _dot(h, wf_ref[...]) + bf_ref[...]
    logf = jnp.minimum(f, 0.0) - jnp.log1p(jnp.exp(-jnp.abs(f)))
    tri = tri_ref[...]
    lh, lm, ll = _split3(logf)
    cum = _dot(tri, lh.astype(BF16)) + _dot(tri, lm.astype(BF16)) + _dot(tri, ll.astype(BF16))

    @pl.when(pl.program_id(0) % tiles_per_seq == 0)
    def _():
        carry_ref[...] = jnp.zeros_like(carry_ref)

    cum = cum + carry_ref[0:1, :]
    carry_ref[0:1, :] = cum[tm - 1:tm, :]
    cb_ref[0] = jnp.concatenate([cum[0:1, :], cum[tm - 1:tm, :], jnp.zeros((6, LANES), F32)], axis=0)
    ch, cm, cl = _split3(cum)

    j = lax.broadcasted_iota(jnp.int32, (tm, HEAD_DIM), 1)

    def ext_cols(vals):
        out = jnp.zeros((tm, HEAD_DIM), F32)
        ones = [pos for pos, val in enumerate(vals) if isinstance(val, float)]
        if ones:
            is_one = functools.reduce(jnp.logical_or, [j == pos for pos in ones])
            out = jnp.where(is_one, 1.0, out)
        for pos, val in enumerate(vals):
            if not isinstance(val, float):
                out = jnp.where(j == pos, val, out)
        return out

    for c in range(QK_B // GROUP_W):
        qkv = _dot(h, wb_ref[c])
        pq = headnorm(qkv[:, 0:GROUP_W], gain_b_ref[0:1, :])
        pk = headnorm(qkv[:, GROUP_W:2 * GROUP_W], gain_b_ref[1:2, :])
        pv = qkv[:, 2 * GROUP_W:3 * GROUP_W]
        r = _dot((pq * pk).astype(BF16), bd) * HEAD_DIM + shift_ref[...]
        for hh in range(HEADS_PER_GROUP):
            head = c * HEADS_PER_GROUP + hh
            lanes = slice(hh * HEAD_DIM, (hh + 1) * HEAD_DIM)
            col = lambda a, idx: a[:, idx:idx + 1]
            cs = [col(ch, head), col(cm, head), col(cl, head)]
            ext_q = ext_cols(cs + [1.0] * 3 + [-col(r, hh * HEAD_DIM)])
            ext_k = ext_cols([1.0] * 3 + [-v for v in cs] + [1.0])
            ext_v = ext_cols([1.0])
            for part, (val, ext) in enumerate(((pq, ext_q), (pk, ext_k))):
                o0 = (part * N_HEADS_B + head) * LANES
                qkb_ref[:, o0:o0 + LANES] = jnp.concatenate([val[:, lanes], ext], axis=-1).astype(BF16)
            vb_ref[:, head * LANES:(head + 1) * LANES] = jnp.concatenate(
                [pv[:, lanes], ext_v], axis=-1).astype(BF16)


def _inproj(xf, g_mix, w_in, q_norm_a, k_norm_a, q_norm_b, k_norm_b, b_forget, fox_shift, seq):
    n, d = xf.shape
    tm = TM_IN
    scale = HEAD_DIM ** -0.5
    w_bf = w_in.astype(BF16)
    qkv_w = N_HEADS_A * HEAD_DIM
    wa = jnp.stack([jnp.concatenate(
        [w_bf[:, part * qkv_w + g * GROUP_W: part * qkv_w + (g + 1) * GROUP_W] for part in range(3)],
        axis=1) for g in range(len(DIL_GROUPS))])
    wb = jnp.stack([jnp.concatenate(
        [w_bf[:, WIDTH_A + part * QK_B + c * GROUP_W: WIDTH_A + part * QK_B + (c + 1) * GROUP_W]
         for part in range(3)], axis=1) for c in range(QK_B // GROUP_W)])
    wf = jnp.pad(w_bf[:, WIDTH_A + WIDTH_B:], ((0, 0), (0, LANES - N_HEADS_B)))
    bfp = jnp.pad(b_forget.astype(F32), (0, LANES - N_HEADS_B)).reshape(1, LANES)
    seg = np.arange(GROUP_W) // HEAD_DIM
    bd = jnp.asarray((seg[:, None] == seg[None, :]).astype(np.float32) / HEAD_DIM, BF16)
    tri = jnp.asarray(np.tril(np.ones((tm, tm), np.float32)), BF16)
    gain_a = jnp.stack([jnp.tile(q_norm_a, HEADS_PER_GROUP) * scale, jnp.tile(k_norm_a, HEADS_PER_GROUP)])
    gain_b = jnp.stack([jnp.tile(q_norm_b, HEADS_PER_GROUP) * scale, jnp.tile(k_norm_b, HEADS_PER_GROUP)])
    const = lambda shape: pl.BlockSpec(shape, lambda i: (0,) * len(shape))
    tps = seq // tm
    batch = n // seq
    qkv3 = 3 * GROUP_W
    (_, d1), (_, d2) = DIL_GROUPS[1], DIL_GROUPS[2]
    return pl.pallas_call(
        functools.partial(_inproj_kernel, tiles_per_seq=tps),
        grid=(n // tm,),
        in_specs=[
            pl.BlockSpec((tm, d), lambda i: (i, 0)),
            const((1, d)), const(wa.shape), const(wb.shape), const(wf.shape),
            const(bd.shape), const(tri.shape), const(gain_a.shape), const(gain_b.shape),
            const(bfp.shape), const((1, 1)),
        ],
        out_specs=[
            pl.BlockSpec((tm, qkv3), lambda i: (i, 0)),
            pl.BlockSpec((1, d1, tm // d1, qkv3), lambda i: (i // tps, 0, i % tps, 0)),
            pl.BlockSpec((1, d2, tm // d2, qkv3), lambda i: (i // tps, 0, i % tps, 0)),
            pl.BlockSpec((tm, 2 * N_HEADS_B * LANES), lambda i: (i, 0)),
            pl.BlockSpec((tm, N_HEADS_B * LANES), lambda i: (i, 0)),
            pl.BlockSpec((1, 8, LANES), lambda i: (i, 0, 0)),
        ],
        out_shape=[
            jax.ShapeDtypeStruct((n, qkv3), BF16),
            jax.ShapeDtypeStruct((batch, d1, seq // d1, qkv3), BF16),
            jax.ShapeDtypeStruct((batch, d2, seq // d2, qkv3), BF16),
            jax.ShapeDtypeStruct((n, 2 * N_HEADS_B * LANES), BF16),
            jax.ShapeDtypeStruct((n, N_HEADS_B * LANES), BF16),
            jax.ShapeDtypeStruct((n // tm, 8, LANES), F32),
        ],
        scratch_shapes=[pltpu.VMEM((8, LANES), F32), pltpu.VMEM((d // LANES, tm, LANES), F32)],
        compiler_params=pltpu.CompilerParams(
            dimension_semantics=("arbitrary",), vmem_limit_bytes=VMEM_LIMIT),
        name="inproj",
    )(xf, g_mix.reshape(1, d), wa, wb, wf, bd, tri, gain_a, gain_b, bfp, fox_shift.reshape(1, 1))


def _dilated_kernel(p0_ref, h0_ref, p1_ref, h1_ref, p2_ref, h2_ref, bias_ref, o_ref, acc_ref, lse_ref):
    tq = WIN_J
    first_sb = pl.program_id(1) == 0
    lane_head = lax.broadcasted_iota(jnp.int32, (tq, GROUP_W), 1) // HEAD_DIM
    prev_col = lax.broadcasted_iota(jnp.int32, (tq, 2 * tq), 1) < tq
    qc, kc_, vc_ = (slice(0, GROUP_W), slice(GROUP_W, 2 * GROUP_W), slice(2 * GROUP_W, 3 * GROUP_W))

    def attend(g, q, kp, kc, vp, vc, no_prev):
        kcat = jnp.concatenate([kp, kc], axis=0)
        vcat = jnp.concatenate([vp, vc], axis=0)
        dead = jnp.logical_and(no_prev, prev_col)
        q4 = jnp.concatenate([jnp.where(lane_head == hh, q, jnp.zeros_like(q))
                              for hh in range(HEADS_PER_GROUP)], axis=0)
        s = _dot_nt(q4, kcat) + bias_ref[g].reshape(HEADS_PER_GROUP * tq, 2 * tq)
        s = jnp.where(jnp.concatenate([dead] * HEADS_PER_GROUP, axis=0), NEG, s)
        m = jnp.max(s, axis=-1, keepdims=True)
        p = jnp.exp(s - m)
        l = jnp.sum(p, axis=-1, keepdims=True)
        o4 = _dot(p.astype(BF16), vcat) * (1.0 / l)
        lse4 = m + jnp.log(l)
        acc = o4[0:tq]
        lse = jnp.broadcast_to(lse4[0:tq], (tq, GROUP_W))
        for hh in range(1, HEADS_PER_GROUP):
            sel = lane_head == hh
            acc = jnp.where(sel, o4[hh * tq:(hh + 1) * tq], acc)
            lse = jnp.where(sel, lse4[hh * tq:(hh + 1) * tq], lse)
        return acc, lse

    n_half = GROUP_W // LANES

    def merge(rows, acc, lse):
        for c in range(n_half):
            lanes = slice(c * LANES, (c + 1) * LANES)
            l1 = lse_ref[c, rows, :]
            mx = jnp.maximum(l1, lse[:, lanes])
            w1 = jnp.exp(l1 - mx)
            w2 = jnp.exp(lse[:, lanes] - mx)
            den = w1 + w2
            acc_ref[c, rows, :] = (w1 * acc_ref[c, rows, :] + w2 * acc[:, lanes]) / den
            lse_ref[c, rows, :] = mx + jnp.log(den)

    def pick(first, halo, body):
        return jnp.where(first, halo, body)

    def loop(n, body):
        def trip(i, carry):
            for u in range(DIL_UNROLL):
                body(i * DIL_UNROLL + u, carry)
            return carry
        lax.fori_loop(0, n // DIL_UNROLL, trip, 0)

    def body0(j, carry):
        st = pl.multiple_of(j * tq, tq)
        pst = pl.multiple_of(jnp.maximum(j - 1, 0) * tq, tq)
        cur, prv = pl.ds(st, tq), pl.ds(pst, tq)
        acc, lse = attend(
            0, p0_ref[0, cur, qc],
            pick(j == 0, h0_ref[0, :, kc_], p0_ref[0, prv, kc_]), p0_ref[0, cur, kc_],
            pick(j == 0, h0_ref[0, :, vc_], p0_ref[0, prv, vc_]), p0_ref[0, cur, vc_],
            jnp.logical_and(j == 0, first_sb))
        for c in range(n_half):
            acc_ref[c, cur, :] = acc[:, c * LANES:(c + 1) * LANES]
            lse_ref[c, cur, :] = lse[:, c * LANES:(c + 1) * LANES]
        return carry

    loop(SUPER // tq, body0)

    d1 = DIL_GROUPS[1][1]
    nsub1 = SUPER // d1 // tq
    def body1(t, carry):
        r, ii = t // nsub1, t % nsub1
        st = pl.multiple_of(ii * tq, tq)
        pst = pl.multiple_of(jnp.maximum(ii - 1, 0) * tq, tq)
        cur, prv = pl.ds(st, tq), pl.ds(pst, tq)
        acc, lse = attend(
            1, p1_ref[0, r, cur, qc],
            pick(ii == 0, h1_ref[0, r, :, kc_], p1_ref[0, r, prv, kc_]), p1_ref[0, r, cur, kc_],
            pick(ii == 0, h1_ref[0, r, :, vc_], p1_ref[0, r, prv, vc_]), p1_ref[0, r, cur, vc_],
            jnp.logical_and(ii == 0, first_sb))
        merge(pl.ds(ii * (tq * d1) + r, tq, stride=d1), acc, lse)
        return carry

    loop(d1 * nsub1, body1)

    d2 = DIL_GROUPS[2][1]

    def body2(r, carry):
        acc, lse = attend(2, p2_ref[0, r, :, qc], h2_ref[0, r, :, kc_], p2_ref[0, r, :, kc_],
                          h2_ref[0, r, :, vc_], p2_ref[0, r, :, vc_], first_sb)
        merge(pl.ds(r, tq, stride=d2), acc, lse)
        return carry

    loop(d2, body2)

    for c in range(n_half):
        o_ref[0, :, c * LANES:(c + 1) * LANES] = acc_ref[c].astype(o_ref.dtype)


def _rel_bucket(dist):
    max_exact = REL_BUCKETS // 2
    n = jnp.maximum(dist.astype(F32), 1.0)
    large = max_exact + (jnp.log(n / max_exact) / math.log(REL_MAX_DIST / max_exact)
                         * (REL_BUCKETS - max_exact)).astype(jnp.int32)
    large = jnp.minimum(large, REL_BUCKETS - 1)
    return jnp.where(dist < max_exact, dist, large)


def _toeplitz_bias(rel_bias, g, dil):
    tq = WIN_J
    offs = dil * (WIN_J - jnp.arange(WIN_J + 1, dtype=jnp.int32))
    hs = slice(g * HEADS_PER_GROUP, (g + 1) * HEADS_PER_GROUP)
    tab_rev = rel_bias[_rel_bucket(offs)][:, hs].T.astype(F32)
    period = 3 * tq
    neg = lambda w: jnp.full((HEADS_PER_GROUP, w), NEG, F32)
    vec = jnp.concatenate([neg(tq - 1), tab_rev, neg(period - 2 * tq)], axis=1)
    flat = jnp.broadcast_to(vec[:, None, :], (HEADS_PER_GROUP, tq, period)).reshape(HEADS_PER_GROUP, -1)
    skew = flat[:, :tq * (period - 1)].reshape(HEADS_PER_GROUP, tq, period - 1)
    return skew[:, :, tq - 1:3 * tq - 1]


def _dilated(pa0, pa1, pa2, bias, batch, seq):
    tq = WIN_J
    qkv3 = 3 * GROUP_W
    (_, d1), (_, d2) = DIL_GROUPS[1], DIL_GROUPS[2]
    nsb = seq // SUPER
    p0 = pa0.reshape(batch, seq, qkv3)
    prev_blk = lambda per_sb: (lambda b, s: jnp.maximum(s * per_sb - 1, 0))
    h0i, h1i, h2i = prev_blk(SUPER // tq), prev_blk(SUPER // d1 // tq), prev_blk(SUPER // d2 // tq)
    out = pl.pallas_call(
        _dilated_kernel,
        grid=(batch, nsb),
        in_specs=[
            pl.BlockSpec((1, SUPER, qkv3), lambda b, s: (b, s, 0)),
            pl.BlockSpec((1, tq, qkv3), lambda b, s: (b, h0i(b, s), 0)),
            pl.BlockSpec((1, d1, SUPER // d1, qkv3), lambda b, s: (b, 0, s, 0)),
            pl.BlockSpec((1, d1, tq, qkv3), lambda b, s: (b, 0, h1i(b, s), 0)),
            pl.BlockSpec((1, d2, SUPER // d2, qkv3), lambda b, s: (b, 0, s, 0)),
            pl.BlockSpec((1, d2, tq, qkv3), lambda b, s: (b, 0, h2i(b, s), 0)),
            pl.BlockSpec(bias.shape, lambda b, s: (0, 0, 0, 0)),
        ],
        out_specs=pl.BlockSpec((1, SUPER, GROUP_W), lambda b, s: (b, s, 0)),
        out_shape=jax.ShapeDtypeStruct((batch, seq, GROUP_W), BF16),
        scratch_shapes=[pltpu.VMEM((GROUP_W // LANES, SUPER, LANES), F32)] * 2,
        compiler_params=pltpu.CompilerParams(
            dimension_semantics=("arbitrary", "arbitrary"), vmem_limit_bytes=VMEM_LIMIT),
        name="dilated",
    )(p0, p0, pa1, pa1, pa2, pa2, bias)
    return out.reshape(batch * seq, GROUP_W)


def _fox_kernel(nlive_sm, q_ref, k_ref, v_ref, o_ref, m_ref, acc_ref, *, online):
    tq = q_ref.shape[1]
    half = tq // 2
    qi = pl.program_id(2)
    step = (pl.program_id(0) * pl.num_programs(1) + pl.program_id(1)) * pl.num_programs(2) + qi
    row = lax.broadcasted_iota(jnp.int32, (half, half), 0)
    col = lax.broadcasted_iota(jnp.int32, (half, half), 1)
    causal = row >= col

    def attend(hh, rows, state, start, nkeys, masked):
        m, acc = state
        lanes = slice(hh * LANES, (hh + 1) * LANES)
        s = _dot_nt(q_ref[0, rows, lanes], k_ref[0, pl.ds(start, nkeys), lanes])
        if masked:
            s = jnp.where(causal, s, NEG)
        if online:
            m_new = jnp.maximum(m, jnp.max(s, axis=-1, keepdims=True))
            acc = acc * jnp.exp(m - m_new)
            s = s - m_new
            m = m_new
        return m, acc + _dot(jnp.exp(s).astype(BF16), v_ref[0, pl.ds(start, nkeys), lanes])

    first = [qi - nlive_sm[2 * step + hh] for hh in range(2)]
    for hh in range(2):
        m_ref[hh] = jnp.full((tq, 1), NEG, F32)
        acc_ref[hh] = jnp.zeros((tq, LANES), F32)

    def full_chunk(ki, carry):
        start = pl.multiple_of(ki * tq, tq)
        for hh in range(2):
            @pl.when(ki >= first[hh])
            def _(hh=hh):
                m, acc = attend(hh, slice(None), (m_ref[hh], acc_ref[hh]), start, tq, False)
                acc_ref[hh] = acc
                if online:
                    m_ref[hh] = m
        return carry

    lax.fori_loop(jnp.minimum(first[0], first[1]), qi, full_chunk, 0)

    d0 = pl.multiple_of(qi * tq, tq)
    outs = []
    for hh in range(2):
        m, acc = m_ref[hh], acc_ref[hh]
        top, bot = slice(0, half), slice(half, tq)
        s_top = attend(hh, top, (m[top], acc[top]), d0, half, True)
        s_bot = attend(hh, bot, (m[bot], acc[bot]), d0, half, False)
        s_bot = attend(hh, bot, s_bot, d0 + half, half, True)
        a = jnp.concatenate([s_top[1], s_bot[1]], axis=0)
        outs.append(a[:, :HEAD_DIM] / a[:, HEAD_DIM:HEAD_DIM + 1])
    o_ref[0] = jnp.concatenate(outs, axis=-1).astype(o_ref.dtype)


def _fox_live_chunks(cb, top, batch, seq):
    tps = seq // TM_IN
    per = TQ_FOX // TM_IN
    nq = seq // TQ_FOX
    c_first = cb[:, 0, :N_HEADS_B].reshape(batch, tps, N_HEADS_B)[:, ::per]
    c_last = cb[:, 1, :N_HEADS_B].reshape(batch, tps, N_HEADS_B)[:, per - 1::per]
    live = (top + c_first[:, :, None, :] - c_last[:, None, :, :]) >= FOX_DEAD_EXPONENT
    back = jnp.arange(nq)[:, None] - jnp.arange(nq)[None, :]
    reach = jnp.max(jnp.where(jnp.logical_and(live, (back > 0)[None, :, :, None]),
                              back[None, :, :, None], 0), axis=2)
    reach = reach.reshape(batch, nq, N_HEADS_B // 2, 2).transpose(0, 2, 1, 3)
    return reach.reshape(-1).astype(jnp.int32)


def _fox(qkb, vb, cb, top, online, batch, seq):
    tq = TQ_FOX
    pairs = N_HEADS_B // 2
    nq = seq // tq
    qkv = qkb.reshape(batch, seq, 2 * N_HEADS_B * LANES)
    vv = vb.reshape(batch, seq, N_HEADS_B * LANES)
    all_chunks = jnp.tile(jnp.repeat(jnp.arange(nq, dtype=jnp.int32), 2), batch * pairs)

    def call(is_online, nlive):
        grid_spec = pltpu.PrefetchScalarGridSpec(
            num_scalar_prefetch=1,
            grid=(batch, pairs, nq),
            in_specs=[
                pl.BlockSpec((1, tq, 2 * LANES), lambda b, p, i, n: (b, i, p)),
                pl.BlockSpec((1, seq, 2 * LANES), lambda b, p, i, n: (b, 0, pairs + p)),
                pl.BlockSpec((1, seq, 2 * LANES), lambda b, p, i, n: (b, 0, p)),
            ],
            out_specs=pl.BlockSpec((1, tq, LANES), lambda b, p, i, n: (b, i, p)),
            scratch_shapes=[pltpu.VMEM((2, tq, 1), F32), pltpu.VMEM((2, tq, LANES), F32)],
        )
        return pl.pallas_call(
            functools.partial(_fox_kernel, online=is_online),
            grid_spec=grid_spec,
            out_shape=jax.ShapeDtypeStruct((batch, seq, OUT_B), BF16),
            compiler_params=pltpu.CompilerParams(
                dimension_semantics=("arbitrary", "arbitrary", "arbitrary"), vmem_limit_bytes=VMEM_LIMIT),
            name="fox_online" if is_online else "fox",
        )(nlive, qkv, qkv, vv)

    out = lax.cond(online, lambda: call(True, all_chunks),
                   lambda: call(False, _fox_live_chunks(cb, top, batch, seq)))
    return out.reshape(batch * seq, OUT_B)


def _fox_shift(q_norm_b, k_norm_b):
    bound = HEAD_DIM * (HEAD_DIM ** -0.5) * jnp.max(jnp.abs(q_norm_b)) * jnp.max(jnp.abs(k_norm_b))
    shift = jnp.maximum(2.0 * FOX_ROUNDING_SLACK * bound - FOX_EXP_HEADROOM, 0.0).astype(F32)
    top = 2.0 * FOX_ROUNDING_SLACK * bound - shift
    return shift, top.astype(F32), shift > FOX_MAX_SHIFT


def _post_kernel(x_ref, ya_ref, yb_ref, gmix_ref, wg_ref, bg_ref, wpa_ref, wpb_ref, wo_ref,
                 gffn_ref, wr_ref, rb_ref, x1_ref, h2_ref, topi_ref, topw_ref):
    d = x_ref.shape[1]
    x = x_ref[...]
    h = (x * lax.rsqrt(jnp.mean(x * x, axis=-1, keepdims=True) + EPS) * gmix_ref[...]).astype(BF16)
    gates = jax.nn.sigmoid(_dot(h, wg_ref[...]) + bg_ref[...])
    merged = gates[:, :d] * _dot(ya_ref[...], wpa_ref[...]) + gates[:, d:] * _dot(yb_ref[...], wpb_ref[...])
    x1 = x + _dot(merged.astype(BF16), wo_ref[...])
    x1_ref[...] = x1
    h2 = x1 * lax.rsqrt(jnp.mean(x1 * x1, axis=-1, keepdims=True) + EPS) * gffn_ref[...]
    h2_ref[...] = h2.astype(BF16)

    hh, hm, _ = _split3(h2)
    wr = wr_ref[...]
    wh = wr.astype(BF16)
    wl = (wr - wh.astype(F32)).astype(BF16)
    hh, hm = hh.astype(BF16), hm.astype(BF16)
    logits = _dot_nt(wh, hh) + _dot_nt(wh, hm) + _dot_nt(wl, hh)
    scores = jax.nn.sigmoid(logits)
    biased = scores + rb_ref[...]
    eid = lax.broadcasted_iota(jnp.int32, scores.shape, 0).astype(F32)
    chosen = jnp.zeros(scores.shape, jnp.bool_)
    idx, val = [], []
    for _ in range(TOP_K):
        cur = jnp.where(chosen, -jnp.inf, biased)
        mx = jnp.max(cur, axis=0, keepdims=True)
        first = jnp.min(jnp.where(cur == mx, eid, float(N_EXPERTS)), axis=0, keepdims=True)
        pick = eid == first
        chosen = jnp.logical_or(chosen, pick)
        idx.append(first)
        val.append(jnp.sum(jnp.where(pick, scores, 0.0), axis=0, keepdims=True))
    top_s = jnp.concatenate(val, axis=0)
    top_w = top_s / jnp.sum(top_s, axis=0, keepdims=True) * ROUTE_SCALE
    tm = scores.shape[1]
    both = jnp.concatenate(idx + [top_w, jnp.zeros((LANES - 2 * TOP_K, tm), F32)], axis=0).T
    topi_ref[...] = both[:, :TOP_K].astype(jnp.int32)
    topw_ref[...] = both[:, TOP_K:2 * TOP_K]


def _post(xf, ya, yb, g_mix, w_gate, b_gate, w_proj_a, w_proj_b, w_out, g_ffn, w_router, router_bias):
    n, d = xf.shape
    tm = TM_POST
    const = lambda shape: pl.BlockSpec(shape, lambda i: (0,) * len(shape))
    row = lambda w: pl.BlockSpec((tm, w), lambda i: (i, 0))
    args = [xf, ya, yb, g_mix.reshape(1, d), w_gate.astype(BF16), b_gate.reshape(1, 2 * d),
            w_proj_a.astype(BF16), w_proj_b.astype(BF16), w_out.astype(BF16), g_ffn.reshape(1, d),
            w_router.astype(F32).T, router_bias.astype(F32).reshape(N_EXPERTS, 1)]
    in_specs = [row(d), row(OUT_A), row(OUT_B)] + [const(a.shape) for a in args[3:]]
    return pl.pallas_call(
        _post_kernel,
        grid=(n // tm,),
        in_specs=in_specs,
        out_specs=[row(d), row(d), row(TOP_K), row(TOP_K)],
        out_shape=[jax.ShapeDtypeStruct((n, d), F32), jax.ShapeDtypeStruct((n, d), BF16),
                   jax.ShapeDtypeStruct((n, TOP_K), jnp.int32), jax.ShapeDtypeStruct((n, TOP_K), F32)],
        compiler_params=pltpu.CompilerParams(
            dimension_semantics=("arbitrary",), vmem_limit_bytes=VMEM_LIMIT),
        name="post",
    )(*args)


def _dispatch_kernel(h2_ref, topi_ref, tri_ref, upper_ref, xs_ref, slots_ref, cnt_ref, off_ref):
    tb = h2_ref.shape[0]
    topi = topi_ref[...]
    lane = lax.broadcasted_iota(jnp.int32, (tb, N_EXPERTS), 1)
    picks = [lane == topi[:, k:k + 1] for k in range(TOP_K)]
    mask = picks[0]
    for pk in picks[1:]:
        mask = jnp.logical_or(mask, pk)
    maskf = jnp.where(mask, 1.0, 0.0)
    rank = _dot(tri_ref[...], maskf.astype(BF16))
    cnt = jnp.sum(maskf, axis=0, keepdims=True)
    gran = jnp.floor((cnt + (GRAN - 1)) * (1.0 / GRAN))
    goff = _dot(jnp.broadcast_to(gran, (8, N_EXPERTS)).astype(BF16), upper_ref[...])[0:1]
    off = goff * GRAN
    slot_te = off + rank
    slots = jnp.concatenate(
        [jnp.sum(jnp.where(pk, slot_te, 0.0), axis=-1, keepdims=True) for pk in picks], axis=1)
    slots_ref[...] = slots.astype(jnp.int32)
    cnt_ref[0] = cnt.astype(jnp.int32)
    off_ref[0] = off.astype(jnp.int32)
    v = jnp.where(mask, slot_te, float(NO_SLOT))
    v_hi = jnp.floor(v * (1.0 / 64.0))
    w = jnp.concatenate([v_hi * 64.0, v - v_hi * 64.0], axis=1).T.astype(BF16)
    end = off + gran * GRAN
    used = jnp.max(end).astype(jnp.int32)
    h2 = h2_ref[...]

    def sort_chunk(c):
        s_e = (lax.broadcasted_iota(jnp.int32, (SLOT_CHUNK, N_EXPERTS), 0) + c * SLOT_CHUNK).astype(F32)
        own = jnp.where(jnp.logical_and(s_e >= off, s_e < end), 1.0, 0.0)
        looked = _dot(jnp.concatenate([own, own], axis=1).astype(BF16), w)
        s_t = (lax.broadcasted_iota(jnp.int32, (SLOT_CHUNK, tb), 0) + c * SLOT_CHUNK).astype(F32)
        onehot = jnp.where(looked == s_t, 1.0, 0.0).astype(BF16)
        xs_ref[0, c * SLOT_CHUNK:(c + 1) * SLOT_CHUNK, :] = _dot(onehot, h2).astype(BF16)

    for c in range(CAP // SLOT_CHUNK):
        if c < CHUNKS_TYPICAL:
            sort_chunk(c)
        else:
            @pl.when(c * SLOT_CHUNK < used)
            def _(c=c):
                sort_chunk(c)

            @pl.when(c * SLOT_CHUNK >= used)
            def _(c=c):
                xs_ref[0, c * SLOT_CHUNK:(c + 1) * SLOT_CHUNK, :] = jnp.zeros((SLOT_CHUNK, xs_ref.shape[2]), BF16)


def _dispatch(h2, topi):
    n, d = h2.shape
    nb = n // TB
    tri = jnp.asarray(np.tril(np.ones((TB, TB), np.float32), -1), BF16)
    upper = jnp.asarray(np.triu(np.ones((N_EXPERTS, N_EXPERTS), np.float32), 1), BF16)
    const = lambda shape: pl.BlockSpec(shape, lambda i: (0,) * len(shape))
    meta = pl.BlockSpec((1, 1, N_EXPERTS), lambda i: (i, 0, 0))
    return pl.pallas_call(
        _dispatch_kernel,
        grid=(nb,),
        in_specs=[pl.BlockSpec((TB, d), lambda i: (i, 0)), pl.BlockSpec((TB, TOP_K), lambda i: (i, 0)),
                  const(tri.shape), const(upper.shape)],
        out_specs=[pl.BlockSpec((1, CAP, d), lambda i: (i, 0, 0)),
                   pl.BlockSpec((TB, TOP_K), lambda i: (i, 0)), meta, meta],
        out_shape=[jax.ShapeDtypeStruct((nb, CAP, d), BF16), jax.ShapeDtypeStruct((n, TOP_K), jnp.int32),
                   jax.ShapeDtypeStruct((nb, 1, N_EXPERTS), jnp.int32),
                   jax.ShapeDtypeStruct((nb, 1, N_EXPERTS), jnp.int32)],
        compiler_params=pltpu.CompilerParams(
            dimension_semantics=("arbitrary",), vmem_limit_bytes=VMEM_LIMIT),
        name="dispatch",
    )(h2, topi, tri, upper)


def _ffn_kernel(item_e_sm, item_g0_sm, item_n_sm, glist_sm, xs_hbm, wg_ref, wu_ref, wd_ref, ys_hbm,
                xbuf, ybuf, wg_bf, wu_bf, wd_bf, sem_in, sem_out):
    step = pl.program_id(0)
    nsteps = pl.num_programs(0)
    buf = step % 2

    def for_granules(st, fn):
        g0 = item_g0_sm[st]
        n = item_n_sm[st]

        def per_granule(j, carry):
            fn(glist_sm[g0 + j], j)
            return carry

        lax.fori_loop(0, n, per_granule, 0)
        return n

    def fetch(b_):
        return lambda src, dst: pltpu.make_async_copy(xs_hbm.at[src], xbuf.at[b_, dst], sem_in.at[b_])

    def writeback(b_):
        return lambda src, dst: pltpu.make_async_copy(ybuf.at[b_, dst], ys_hbm.at[src], sem_out.at[b_])

    def start(mk):
        return lambda src, dst: mk(src, dst).start()

    def wait_all(st, span):
        n = item_n_sm[st]
        size = PASS_GRAN
        while size >= 1:
            @pl.when((n & size) != 0)
            def _(size=size):
                span(size).wait()
            size //= 2
        return n

    def fetch_span(b_):
        return lambda k: pltpu.make_async_copy(
            xs_hbm.at[pl.ds(0, k)], xbuf.at[b_, pl.ds(0, k)], sem_in.at[b_])

    def writeback_span(b_):
        return lambda k: pltpu.make_async_copy(
            ybuf.at[b_, pl.ds(0, k)], ys_hbm.at[pl.ds(0, k)], sem_out.at[b_])

    @pl.when(step == 0)
    def _():
        xbuf[...] = jnp.zeros_like(xbuf)
        for_granules(step, start(fetch(0)))

    @pl.when(step + 1 < nsteps)
    def _():
        for_granules(step + 1, start(fetch(1 - buf)))

    ngran = wait_all(step, fetch_span(buf))

    @pl.when(step >= 2)
    def _():
        wait_all(step - 2, writeback_span(buf))

    @pl.when(ngran > 0)
    def _():
        wg_bf[...] = wg_ref[0].astype(BF16)
        wu_bf[...] = wu_ref[0].astype(BF16)
        wd_bf[...] = wd_ref[0].astype(BF16)

    x_cols = xbuf.shape[-1]

    def ffn_rows(base, rows):
        grans = pl.ds(pl.multiple_of(base // GRAN, rows // GRAN), rows // GRAN)
        x = xbuf[buf, grans].reshape(rows, x_cols)
        g = _dot(x, wg_bf[...])
        u = _dot(x, wu_bf[...])
        mid = (g * jax.nn.sigmoid(g) * u).astype(BF16)
        ybuf[buf, grans] = _dot(mid, wd_bf[...]).astype(BF16).reshape(rows // GRAN, GRAN, x_cols)

    nt = (ngran * GRAN + (FT - 1)) // FT
    big = FT_BIG // FT

    def big_tile(i, carry):
        ffn_rows(pl.multiple_of(i * FT_BIG, FT_BIG), FT_BIG)
        return carry

    lax.fori_loop(0, nt // big, big_tile, 0)
    size = big // 2
    while size >= 1:
        @pl.when((nt & size) != 0)
        def _(size=size):
            ffn_rows(pl.multiple_of((nt & ~(2 * size - 1)) * FT, size * FT), size * FT)
        size //= 2

    for_granules(step, start(writeback(buf)))

    @pl.when(step == nsteps - 1)
    def _():
        wait_all(step, writeback_span(buf))

        @pl.when(step >= 1)
        def _():
            wait_all(step - 1, writeback_span(1 - buf))


def _work_items(cnt, off):
    nb = cnt.shape[0]
    seg_n = ((cnt.reshape(nb, N_EXPERTS) + (GRAN - 1)) // GRAN).T.reshape(-1)
    seg_row = ((off.reshape(nb, N_EXPERTS) + jnp.arange(nb, dtype=jnp.int32)[:, None] * CAP) // GRAN).T.reshape(-1)
    seg_end = jnp.cumsum(seg_n)
    seg_start = seg_end - seg_n
    gmax = nb * (TB * TOP_K // GRAN + N_EXPERTS)
    prev_end = jnp.concatenate([jnp.ones((1,), jnp.int32), (seg_row + seg_n)[:-1]])
    steps = jnp.ones((gmax,), jnp.int32).at[seg_start].add(seg_row - prev_end, mode="drop")
    glist = jnp.cumsum(steps)
    per_e = seg_n.reshape(N_EXPERTS, nb).sum(axis=1)
    first_e = seg_start[::nb]
    passes = (per_e + (PASS_GRAN - 1)) // PASS_GRAN
    pass_end = jnp.cumsum(passes)
    n_items = N_EXPERTS + gmax // PASS_GRAN
    w = jnp.arange(n_items, dtype=jnp.int32)
    item_e = jnp.minimum(jnp.sum(pass_end[None, :] <= w[:, None], axis=1), N_EXPERTS - 1).astype(jnp.int32)
    done = (w - (pass_end - passes)[item_e]) * PASS_GRAN
    item_n = jnp.clip(per_e[item_e] - done, 0, PASS_GRAN)
    item_g0 = first_e[item_e] + done
    return item_e, item_g0.astype(jnp.int32), item_n.astype(jnp.int32), glist.astype(jnp.int32)


def _ffn(xs, cnt, off, wg, wu, wd):
    _, _, d = xs.shape
    item_e, item_g0, item_n, glist = _work_items(cnt, off)
    per_expert = lambda shape: pl.BlockSpec((1,) + shape, lambda w, ie, g0, n, gl: (ie[w], 0, 0))
    grid_spec = pltpu.PrefetchScalarGridSpec(
        num_scalar_prefetch=4,
        grid=(item_e.shape[0],),
        in_specs=[pl.BlockSpec(memory_space=pl.ANY), per_expert((d, D_EXPERT)), per_expert((d, D_EXPERT)),
                  per_expert((D_EXPERT, d))],
        out_specs=pl.BlockSpec(memory_space=pl.ANY),
        scratch_shapes=[pltpu.VMEM((2, PASS_GRAN, GRAN, d), BF16)] * 2 + [
                        pltpu.VMEM((d, D_EXPERT), BF16), pltpu.VMEM((d, D_EXPERT), BF16),
                        pltpu.VMEM((D_EXPERT, d), BF16),
                        pltpu.SemaphoreType.DMA((2,)), pltpu.SemaphoreType.DMA((2,))],
    )
    return pl.pallas_call(
        _ffn_kernel,
        grid_spec=grid_spec,
        out_shape=jax.ShapeDtypeStruct(xs.shape, xs.dtype),
        input_output_aliases={4: 0},
        compiler_params=pltpu.CompilerParams(
            dimension_semantics=("arbitrary",), vmem_limit_bytes=VMEM_LIMIT),
        name="ffn",
    )(item_e, item_g0, item_n, glist, xs, wg, wu, wd)


def _combine_kernel(used_sm, x1_ref, h2_ref, ys_ref, slots_ref, topw_ref, wgus_ref, wds_ref, o_ref):
    tb = x1_ref.shape[0]
    gu = _dot(h2_ref[...], wgus_ref[...])
    g, u = gu[:, :D_SHARED], gu[:, D_SHARED:]
    acc = x1_ref[...] + _dot((g * jax.nn.sigmoid(g) * u).astype(BF16), wds_ref[...])
    slots = slots_ref[...].astype(F32)
    topw = topw_ref[...]
    used = used_sm[pl.program_id(0)]

    def gather_chunk(c):
        scol = (lax.broadcasted_iota(jnp.int32, (tb, SLOT_CHUNK), 1) + c * SLOT_CHUNK).astype(F32)
        gate = jnp.zeros((tb, SLOT_CHUNK), F32)
        for k in range(TOP_K):
            gate = gate + jnp.where(scol == slots[:, k:k + 1], topw[:, k:k + 1], 0.0)
        return _dot(gate.astype(BF16), ys_ref[0, c * SLOT_CHUNK:(c + 1) * SLOT_CHUNK, :])

    for c in range(CHUNKS_TYPICAL):
        acc = acc + gather_chunk(c)
    o_ref[...] = acc
    for c in range(CHUNKS_TYPICAL, CAP // SLOT_CHUNK):
        @pl.when(c * SLOT_CHUNK < used)
        def _(c=c):
            o_ref[...] += gather_chunk(c)


def _combine(x1, h2, ys, slots, topw, used, wgus, wds):
    n, d = x1.shape
    const = lambda shape: pl.BlockSpec(shape, lambda i, u: (0,) * len(shape))
    row = lambda w: pl.BlockSpec((TB, w), lambda i, u: (i, 0))
    grid_spec = pltpu.PrefetchScalarGridSpec(
        num_scalar_prefetch=1,
        grid=(n // TB,),
        in_specs=[row(d), row(d), pl.BlockSpec((1, CAP, d), lambda i, u: (i, 0, 0)), row(TOP_K), row(TOP_K),
                  const(wgus.shape), const(wds.shape)],
        out_specs=row(d),
    )
    return pl.pallas_call(
        _combine_kernel,
        grid_spec=grid_spec,
        out_shape=jax.ShapeDtypeStruct((n, d), F32),
        compiler_params=pltpu.CompilerParams(
            dimension_semantics=("arbitrary",), vmem_limit_bytes=VMEM_LIMIT),
        name="combine",
    )(used, x1, h2, ys, slots, topw, wgus, wds)


def _moe(x1, h2, topi, topw, w_gate_e, w_up_e, w_down_e, w_gate_s, w_up_s, w_down_s):
    n, d = x1.shape
    wgus = jnp.concatenate([w_gate_s.astype(BF16), w_up_s.astype(BF16)], axis=-1)
    xs, slots, cnt, off = _dispatch(h2, topi)
    ys = _ffn(xs.reshape(-1, GRAN, d), cnt, off, w_gate_e, w_up_e, w_down_e)
    used = jnp.max(off + (cnt + (GRAN - 1)) // GRAN * GRAN, axis=(1, 2)).astype(jnp.int32)
    return _combine(x1, h2, ys.reshape(n // TB, CAP, d), slots, topw, used, wgus, w_down_s.astype(BF16))


def kernel(x, g_mix, w_in, q_norm_a, k_norm_a, q_norm_b, k_norm_b, rel_bias, b_forget, w_gate, b_gate,
           w_proj_a, w_proj_b, w_out, g_ffn, w_router, router_bias, w_gate_e, w_up_e, w_down_e,
           w_gate_s, w_up_s, w_down_s):
    batch, seq, d = x.shape
    xf = x.reshape(batch * seq, d)
    fox_shift, fox_top, fox_online = _fox_shift(q_norm_b, k_norm_b)
    pa0, pa1, pa2, qkb, vb, cb = _inproj(xf, g_mix, w_in, q_norm_a, k_norm_a, q_norm_b, k_norm_b, b_forget,
                                     fox_shift, seq)
    bias = jnp.stack([_toeplitz_bias(rel_bias, g, dil) for g, (_, dil) in enumerate(DIL_GROUPS)])
    ya = _dilated(pa0, pa1, pa2, bias, batch, seq)

    yb = _fox(qkb, vb, cb, fox_top, fox_online, batch, seq)
    x1, h2, topi, topw = _post(xf, ya, yb, g_mix, w_gate, b_gate, w_proj_a, w_proj_b, w_out, g_ffn,
                               w_router, router_bias)
    out = _moe(x1, h2, topi, topw, w_gate_e, w_up_e, w_down_e, w_gate_s, w_up_s, w_down_s)
    return out.reshape(batch, seq, d)
```

```python
import functools
import math

import jax
import jax.numpy as jnp
import numpy as np
from jax import lax
from jax.experimental import pallas as pl
from jax.experimental.pallas import tpu as pltpu

D_MODEL = 1024
HEAD_DIM = 64
DIL_GROUPS = ((128, 1), (512, 4), (2048, 16))
HEADS_PER_GROUP = 4
N_HEADS_A = HEADS_PER_GROUP * len(DIL_GROUPS)
N_HEADS_B = 8
REL_BUCKETS = 32
REL_MAX_DIST = 2048
N_EXPERTS = 64
TOP_K = 8
D_EXPERT = 256
D_SHARED = 256
ROUTE_SCALE = 2.5
EPS = 1e-6

WIDTH_A = 3 * N_HEADS_A * HEAD_DIM
WIDTH_B = 3 * N_HEADS_B * HEAD_DIM
QK_B = N_HEADS_B * HEAD_DIM
OUT_A = HEADS_PER_GROUP * HEAD_DIM
OUT_B = N_HEADS_B * HEAD_DIM

LANES = 128
GROUP_W = HEADS_PER_GROUP * HEAD_DIM
WIN_J = 128
SUPER = DIL_GROUPS[-1][1] * WIN_J
NEG = -1e30
VMEM_LIMIT = 56 * 1024 * 1024

DIL_UNROLL = 8
TM_IN = 512
TM_POST = 1024
TQ_FOX = 1024
FOX_ROUNDING_SLACK = 1.02
FOX_EXP_HEADROOM = 60.0
FOX_DEAD_EXPONENT = -105.0
FOX_MAX_SHIFT = 80.0
TB = 256
GRAN = 16
CAP = TB * TOP_K + N_EXPERTS * GRAN
SLOT_CHUNK = 512
CHUNKS_TYPICAL = -(-(TB * TOP_K + N_EXPERTS * GRAN // 2) // SLOT_CHUNK)
NO_SLOT = 4095
PASS_GRAN = 256
FT = 256
FT_BIG = 1024

BF16 = jnp.bfloat16
F32 = jnp.float32


def _dot(a, b):
    return jnp.dot(a, b, preferred_element_type=F32)


def _dot_nt(a, b):
    return lax.dot_general(a, b, (((1,), (1,)), ((), ())), preferred_element_type=F32)


def _split3(v):
    hi = v.astype(BF16).astype(F32)
    r = v - hi
    mid = r.astype(BF16).astype(F32)
    lo = (r - mid).astype(BF16).astype(F32)
    return hi, mid, lo


def _inproj_kernel(x_ref, g_ref, wa_ref, wb_ref, wf_ref, bd_ref, tri_ref, gain_a_ref, gain_b_ref,
                   bf_ref, shift_ref, pa0_ref, pa1_ref, pa2_ref, qkb_ref, vb_ref, cb_ref, carry_ref, h_ref, *,
                   tiles_per_seq):
    tm = x_ref.shape[0]
    x = x_ref[...]
    h = x * lax.rsqrt(jnp.mean(x * x, axis=-1, keepdims=True) + EPS) * g_ref[...]
    n_lane_chunks = h_ref.shape[0]
    for c in range(n_lane_chunks):
        h_ref[c] = h[:, c * LANES:(c + 1) * LANES]
    h = h.astype(BF16)
    bd = bd_ref[...]

    def headnorm(p, gain):
        ms = _dot((p * p).astype(BF16), bd)
        return p * lax.rsqrt(ms + EPS) * gain

    for g, (pa_ref, (_, dil)) in enumerate(zip((pa0_ref, pa1_ref, pa2_ref), DIL_GROUPS)):
        rows = tm // dil
        if dil == 1:
            hg = h
        else:
            hg = jnp.concatenate([jnp.concatenate(
                [h_ref[c, pl.ds(r, rows, stride=dil), :] for c in range(n_lane_chunks)], axis=1)
                for r in range(dil)], axis=0).astype(BF16)
        qkv = _dot(hg, wa_ref[g])
        for part in range(3):
            cols = slice(part * GROUP_W, (part + 1) * GROUP_W)
            p = qkv[:, cols]
            if part < 2:
                p = headnorm(p, gain_a_ref[part:part + 1, :])
            p = p.astype(BF16)
            if dil == 1:
                pa_ref[:, cols] = p
            else:
                for r in range(dil):
                    pa_ref[0, r, :, cols] = p[r * rows:(r + 1) * rows, :]

    f = _dot(h, wf_ref[...]) + bf_ref[...]
    logf = jnp.minimum(f, 0.0) - jnp.log1p(jnp.exp(-jnp.abs(f)))
    tri = tri_ref[...]
    lh, lm, ll = _split3(logf)
    cum = _dot(tri, lh.astype(BF16)) + _dot(tri, lm.astype(BF16)) + _dot(tri, ll.astype(BF16))

    @pl.when(pl.program_id(0) % tiles_per_seq == 0)
    def _():
        carry_ref[...] = jnp.zeros_like(carry_ref)

    cum = cum + carry_ref[0:1, :]
    carry_ref[0:1, :] = cum[tm - 1:tm, :]
    cb_ref[0] = jnp.concatenate([cum[0:1, :], cum[tm - 1:tm, :], jnp.zeros((6, LANES), F32)], axis=0)
    ch, cm, cl = _split3(cum)

    j = lax.broadcasted_iota(jnp.int32, (tm, HEAD_DIM), 1)

    def ext_cols(vals):
        out = jnp.zeros((tm, HEAD_DIM), F32)
        ones = [pos for pos, val in enumerate(vals) if isinstance(val, float)]
        if ones:
            is_one = functools.reduce(jnp.logical_or, [j == pos for pos in ones])
            out = jnp.where(is_one, 1.0, out)
        for pos, val in enumerate(vals):
            if not isinstance(val, float):
                out = jnp.where(j == pos, val, out)
        return out

    for c in range(QK_B // GROUP_W):
        qkv = _dot(h, wb_ref[c])
        pq = headnorm(qkv[:, 0:GROUP_W], gain_b_ref[0:1, :])
        pk = headnorm(qkv[:, GROUP_W:2 * GROUP_W], gain_b_ref[1:2, :])
        pv = qkv[:, 2 * GROUP_W:3 * GROUP_W]
        r = _dot((pq * pk).astype(BF16), bd) * HEAD_DIM + shift_ref[...]
        for hh in range(HEADS_PER_GROUP):
            head = c * HEADS_PER_GROUP + hh
            lanes = slice(hh * HEAD_DIM, (hh + 1) * HEAD_DIM)
            col = lambda a, idx: a[:, idx:idx + 1]
            cs = [col(ch, head), col(cm, head), col(cl, head)]
            ext_q = ext_cols(cs + [1.0] * 3 + [-col(r, hh * HEAD_DIM)])
            ext_k = ext_cols([1.0] * 3 + [-v for v in cs] + [1.0])
            ext_v = ext_cols([1.0])
            for part, (val, ext) in enumerate(((pq, ext_q), (pk, ext_k))):
                o0 = (part * N_HEADS_B + head) * LANES
                qkb_ref[:, o0:o0 + LANES] = jnp.concatenate([val[:, lanes], ext], axis=-1).astype(BF16)
            vb_ref[:, head * LANES:(head + 1) * LANES] = jnp.concatenate(
                [pv[:, lanes], ext_v], axis=-1).astype(BF16)


def _inproj(xf, g_mix, w_in, q_norm_a, k_norm_a, q_norm_b, k_norm_b, b_forget, fox_shift, seq):
    n, d = xf.shape
    tm = TM_IN
    scale = HEAD_DIM ** -0.5
    w_bf = w_in.astype(BF16)
    qkv_w = N_HEADS_A * HEAD_DIM
    wa = jnp.stack([jnp.concatenate(
        [w_bf[:, part * qkv_w + g * GROUP_W: part * qkv_w + (g + 1) * GROUP_W] for part in range(3)],
        axis=1) for g in range(len(DIL_GROUPS))])
    wb = jnp.stack([jnp.concatenate(
        [w_bf[:, WIDTH_A + part * QK_B + c * GROUP_W: WIDTH_A + part * QK_B + (c + 1) * GROUP_W]
         for part in range(3)], axis=1) for c in range(QK_B // GROUP_W)])
    wf = jnp.pad(w_bf[:, WIDTH_A + WIDTH_B:], ((0, 0), (0, LANES - N_HEADS_B)))
    bfp = jnp.pad(b_forget.astype(F32), (0, LANES - N_HEADS_B)).reshape(1, LANES)
    seg = np.arange(GROUP_W) // HEAD_DIM
    bd = jnp.asarray((seg[:, None] == seg[None, :]).astype(np.float32) / HEAD_DIM, BF16)
    tri = jnp.asarray(np.tril(np.ones((tm, tm), np.float32)), BF16)
    gain_a = jnp.stack([jnp.tile(q_norm_a, HEADS_PER_GROUP) * scale, jnp.tile(k_norm_a, HEADS_PER_GROUP)])
    gain_b = jnp.stack([jnp.tile(q_norm_b, HEADS_PER_GROUP) * scale, jnp.tile(k_norm_b, HEADS_PER_GROUP)])
    const = lambda shape: pl.BlockSpec(shape, lambda i: (0,) * len(shape))
    tps = seq // tm
    batch = n // seq
    qkv3 = 3 * GROUP_W
    (_, d1), (_, d2) = DIL_GROUPS[1], DIL_GROUPS[2]
    return pl.pallas_call(
        functools.partial(_inproj_kernel, tiles_per_seq=tps),
        grid=(n // tm,),
        in_specs=[
            pl.BlockSpec((tm, d), lambda i: (i, 0)),
            const((1, d)), const(wa.shape), const(wb.shape), const(wf.shape),
            const(bd.shape), const(tri.shape), const(gain_a.shape), const(gain_b.shape),
            const(bfp.shape), const((1, 1)),
        ],
        out_specs=[
            pl.BlockSpec((tm, qkv3), lambda i: (i, 0)),
            pl.BlockSpec((1, d1, tm // d1, qkv3), lambda i: (i // tps, 0, i % tps, 0)),
            pl.BlockSpec((1, d2, tm // d2, qkv3), lambda i: (i // tps, 0, i % tps, 0)),
            pl.BlockSpec((tm, 2 * N_HEADS_B * LANES), lambda i: (i, 0)),
            pl.BlockSpec((tm, N_HEADS_B * LANES), lambda i: (i, 0)),
            pl.BlockSpec((1, 8, LANES), lambda i: (i, 0, 0)),
        ],
        out_shape=[
            jax.ShapeDtypeStruct((n, qkv3), BF16),
            jax.ShapeDtypeStruct((batch, d1, seq // d1, qkv3), BF16),
            jax.ShapeDtypeStruct((batch, d2, seq // d2, qkv3), BF16),
            jax.ShapeDtypeStruct((n, 2 * N_HEADS_B * LANES), BF16),
            jax.ShapeDtypeStruct((n, N_HEADS_B * LANES), BF16),
            jax.ShapeDtypeStruct((n // tm, 8, LANES), F32),
        ],
        scratch_shapes=[pltpu.VMEM((8, LANES), F32), pltpu.VMEM((d // LANES, tm, LANES), F32)],
        compiler_params=pltpu.CompilerParams(
            dimension_semantics=("arbitrary",), vmem_limit_bytes=VMEM_LIMIT),
        name="inproj",
    )(xf, g_mix.reshape(1, d), wa, wb, wf, bd, tri, gain_a, gain_b, bfp, fox_shift.reshape(1, 1))


def _dilated_kernel(p0_ref, h0_ref, p1_ref, h1_ref, p2_ref, h2_ref, bias_ref, o_ref, acc_ref, lse_ref):
    tq = WIN_J
    first_sb = pl.program_id(1) == 0
    lane_head = lax.broadcasted_iota(jnp.int32, (tq, GROUP_W), 1) // HEAD_DIM
    prev_col = lax.broadcasted_iota(jnp.int32, (tq, 2 * tq), 1) < tq
    qc, kc_, vc_ = (slice(0, GROUP_W), slice(GROUP_W, 2 * GROUP_W), slice(2 * GROUP_W, 3 * GROUP_W))

    def attend(g, q, kp, kc, vp, vc, no_prev):
        kcat = jnp.concatenate([kp, kc], axis=0)
        vcat = jnp.concatenate([vp, vc], axis=0)
        dead = jnp.logical_and(no_prev, prev_col)
        q4 = jnp.concatenate([jnp.where(lane_head == hh, q, jnp.zeros_like(q))
                              for hh in range(HEADS_PER_GROUP)], axis=0)
        s = _dot_nt(q4, kcat) + bias_ref[g].reshape(HEADS_PER_GROUP * tq, 2 * tq)
        s = jnp.where(jnp.concatenate([dead] * HEADS_PER_GROUP, axis=0), NEG, s)
        m = jnp.max(s, axis=-1, keepdims=True)
        p = jnp.exp(s - m)
        l = jnp.sum(p, axis=-1, keepdims=True)
        o4 = _dot(p.astype(BF16), vcat) * (1.0 / l)
        lse4 = m + jnp.log(l)
        acc = o4[0:tq]
        lse = jnp.broadcast_to(lse4[0:tq], (tq, GROUP_W))
        for hh in range(1, HEADS_PER_GROUP):
            sel = lane_head == hh
            acc = jnp.where(sel, o4[hh * tq:(hh + 1) * tq], acc)
            lse = jnp.where(sel, lse4[hh * tq:(hh + 1) * tq], lse)
        return acc, lse

    n_half = GROUP_W // LANES

    def merge(rows, acc, lse):
        for c in range(n_half):
            lanes = slice(c * LANES, (c + 1) * LANES)
            l1 = lse_ref[c, rows, :]
            mx = jnp.maximum(l1, lse[:, lanes])
            w1 = jnp.exp(l1 - mx)
            w2 = jnp.exp(lse[:, lanes] - mx)
            den = w1 + w2
            acc_ref[c, rows, :] = (w1 * acc_ref[c, rows, :] + w2 * acc[:, lanes]) / den
            lse_ref[c, rows, :] = mx + jnp.log(den)

    def pick(first, halo, body):
        return jnp.where(first, halo, body)

    def loop(n, body):
        def trip(i, carry):
            for u in range(DIL_UNROLL):
                body(i * DIL_UNROLL + u, carry)
            return carry
        lax.fori_loop(0, n // DIL_UNROLL, trip, 0)

    def body0(j, carry):
        st = pl.multiple_of(j * tq, tq)
        pst = pl.multiple_of(jnp.maximum(j - 1, 0) * tq, tq)
        cur, prv = pl.ds(st, tq), pl.ds(pst, tq)
        acc, lse = attend(
            0, p0_ref[0, cur, qc],
            pick(j == 0, h0_ref[0, :, kc_], p0_ref[0, prv, kc_]), p0_ref[0, cur, kc_],
            pick(j == 0, h0_ref[0, :, vc_], p0_ref[0, prv, vc_]), p0_ref[0, cur, vc_],
            jnp.logical_and(j == 0, first_sb))
        for c in range(n_half):
            acc_ref[c, cur, :] = acc[:, c * LANES:(c + 1) * LANES]
            lse_ref[c, cur, :] = lse[:, c * LANES:(c + 1) * LANES]
        return carry

    loop(SUPER // tq, body0)

    d1 = DIL_GROUPS[1][1]
    nsub1 = SUPER // d1 // tq
    def body1(t, carry):
        r, ii = t // nsub1, t % nsub1
        st = pl.multiple_of(ii * tq, tq)
        pst = pl.multiple_of(jnp.maximum(ii - 1, 0) * tq, tq)
        cur, prv = pl.ds(st, tq), pl.ds(pst, tq)
        acc, lse = attend(
            1, p1_ref[0, r, cur, qc],
            pick(ii == 0, h1_ref[0, r, :, kc_], p1_ref[0, r, prv, kc_]), p1_ref[0, r, cur, kc_],
            pick(ii == 0, h1_ref[0, r, :, vc_], p1_ref[0, r, prv, vc_]), p1_ref[0, r, cur, vc_],
            jnp.logical_and(ii == 0, first_sb))
        merge(pl.ds(ii * (tq * d1) + r, tq, stride=d1), acc, lse)
        return carry

    loop(d1 * nsub1, body1)

    d2 = DIL_GROUPS[2][1]

    def body2(r, carry):
        acc, lse = attend(2, p2_ref[0, r, :, qc], h2_ref[0, r, :, kc_], p2_ref[0, r, :, kc_],
                          h2_ref[0, r, :, vc_], p2_ref[0, r, :, vc_], first_sb)
        merge(pl.ds(r, tq, stride=d2), acc, lse)
        return carry

    loop(d2, body2)

    for c in range(n_half):
        o_ref[0, :, c * LANES:(c + 1) * LANES] = acc_ref[c].astype(o_ref.dtype)


def _rel_bucket(dist):
    max_exact = REL_BUCKETS // 2
    n = jnp.maximum(dist.astype(F32), 1.0)
    large = max_exact + (jnp.log(n / max_exact) / math.log(REL_MAX_DIST / max_exact)
                         * (REL_BUCKETS - max_exact)).astype(jnp.int32)
    large = jnp.minimum(large, REL_BUCKETS - 1)
    return jnp.where(dist < max_exact, dist, large)


def _toeplitz_bias(rel_bias, g, dil):
    tq = WIN_J
    offs = dil * (WIN_J - jnp.arange(WIN_J + 1, dtype=jnp.int32))
    hs = slice(g * HEADS_PER_GROUP, (g + 1) * HEADS_PER_GROUP)
    tab_rev = rel_bias[_rel_bucket(offs)][:, hs].T.astype(F32)
    period = 3 * tq
    neg = lambda w: jnp.full((HEADS_PER_GROUP, w), NEG, F32)
    vec = jnp.concatenate([neg(tq - 1), tab_rev, neg(period - 2 * tq)], axis=1)
    flat = jnp.broadcast_to(vec[:, None, :], (HEADS_PER_GROUP, tq, period)).reshape(HEADS_PER_GROUP, -1)
    skew = flat[:, :tq * (period - 1)].reshape(HEADS_PER_GROUP, tq, period - 1)
    return skew[:, :, tq - 1:3 * tq - 1]


def _dilated(pa0, pa1, pa2, bias, batch, seq):
    tq = WIN_J
    qkv3 = 3 * GROUP_W
    (_, d1), (_, d2) = DIL_GROUPS[1], DIL_GROUPS[2]
    nsb = seq // SUPER
    p0 = pa0.reshape(batch, seq, qkv3)
    prev_blk = lambda per_sb: (lambda b, s: jnp.maximum(s * per_sb - 1, 0))
    h0i, h1i, h2i = prev_blk(SUPER // tq), prev_blk(SUPER // d1 // tq), prev_blk(SUPER // d2 // tq)
    out = pl.pallas_call(
        _dilated_kernel,
        grid=(batch, nsb),
        in_specs=[
            pl.BlockSpec((1, SUPER, qkv3), lambda b, s: (b, s, 0)),
            pl.BlockSpec((1, tq, qkv3), lambda b, s: (b, h0i(b, s), 0)),
            pl.BlockSpec((1, d1, SUPER // d1, qkv3), lambda b, s: (b, 0, s, 0)),
            pl.BlockSpec((1, d1, tq, qkv3), lambda b, s: (b, 0, h1i(b, s), 0)),
            pl.BlockSpec((1, d2, SUPER // d2, qkv3), lambda b, s: (b, 0, s, 0)),
            pl.BlockSpec((1, d2, tq, qkv3), lambda b, s: (b, 0, h2i(b, s), 0)),
            pl.BlockSpec(bias.shape, lambda b, s: (0, 0, 0, 0)),
        ],
        out_specs=pl.BlockSpec((1, SUPER, GROUP_W), lambda b, s: (b, s, 0)),
        out_shape=jax.ShapeDtypeStruct((batch, seq, GROUP_W), BF16),
        scratch_shapes=[pltpu.VMEM((GROUP_W // LANES, SUPER, LANES), F32)] * 2,
        compiler_params=pltpu.CompilerParams(
            dimension_semantics=("arbitrary", "arbitrary"), vmem_limit_bytes=VMEM_LIMIT),
        name="dilated",
    )(p0, p0, pa1, pa1, pa2, pa2, bias)
    return out.reshape(batch * seq, GROUP_W)


def _fox_kernel(nlive_sm, q_ref, k_ref, v_ref, o_ref, m_ref, acc_ref, *, online):
    tq = q_ref.shape[1]
    half = tq // 2
    qi = pl.program_id(2)
    step = (pl.program_id(0) * pl.num_programs(1) + pl.program_id(1)) * pl.num_programs(2) + qi
    row = lax.broadcasted_iota(jnp.int32, (half, half), 0)
    col = lax.broadcasted_iota(jnp.int32, (half, half), 1)
    causal = row >= col

    def attend(hh, rows, state, start, nkeys, masked):
        m, acc = state
        lanes = slice(hh * LANES, (hh + 1) * LANES)
        s = _dot_nt(q_ref[0, rows, lanes], k_ref[0, pl.ds(start, nkeys), lanes])
        if masked:
            s = jnp.where(causal, s, NEG)
        if online:
            m_new = jnp.maximum(m, jnp.max(s, axis=-1, keepdims=True))
            acc = acc * jnp.exp(m - m_new)
            s = s - m_new
            m = m_new
        return m, acc + _dot(jnp.exp(s).astype(BF16), v_ref[0, pl.ds(start, nkeys), lanes])

    first = [qi - nlive_sm[2 * step + hh] for hh in range(2)]
    for hh in range(2):
        m_ref[hh] = jnp.full((tq, 1), NEG, F32)
        acc_ref[hh] = jnp.zeros((tq, LANES), F32)

    def full_chunk(ki, carry):
        start = pl.multiple_of(ki * tq, tq)
        for hh in range(2):
            @pl.when(ki >= first[hh])
            def _(hh=hh):
                m, acc = attend(hh, slice(None), (m_ref[hh], acc_ref[hh]), start, tq, False)
                acc_ref[hh] = acc
                if online:
                    m_ref[hh] = m
        return carry

    lax.fori_loop(jnp.minimum(first[0], first[1]), qi, full_chunk, 0)

    d0 = pl.multiple_of(qi * tq, tq)
    outs = []
    for hh in range(2):
        m, acc = m_ref[hh], acc_ref[hh]
        top, bot = slice(0, half), slice(half, tq)
        s_top = attend(hh, top, (m[top], acc[top]), d0, half, True)
        s_bot = attend(hh, bot, (m[bot], acc[bot]), d0, half, False)
        s_bot = attend(hh, bot, s_bot, d0 + half, half, True)
        a = jnp.concatenate([s_top[1], s_bot[1]], axis=0)
        outs.append(a[:, :HEAD_DIM] / a[:, HEAD_DIM:HEAD_DIM + 1])
    o_ref[0] = jnp.concatenate(outs, axis=-1).astype(o_ref.dtype)


def _fox_live_chunks(cb, top, batch, seq):
    tps = seq // TM_IN
    per = TQ_FOX // TM_IN
    nq = seq // TQ_FOX
    c_first = cb[:, 0, :N_HEADS_B].reshape(batch, tps, N_HEADS_B)[:, ::per]
    c_last = cb[:, 1, :N_HEADS_B].reshape(batch, tps, N_HEADS_B)[:, per - 1::per]
    live = (top + c_first[:, :, None, :] - c_last[:, None, :, :]) >= FOX_DEAD_EXPONENT
    back = jnp.arange(nq)[:, None] - jnp.arange(nq)[None, :]
    reach = jnp.max(jnp.where(jnp.logical_and(live, (back > 0)[None, :, :, None]),
                              back[None, :, :, None], 0), axis=2)
    reach = reach.reshape(batch, nq, N_HEADS_B // 2, 2).transpose(0, 2, 1, 3)
    return reach.reshape(-1).astype(jnp.int32)


def _fox(qkb, vb, cb, top, online, batch, seq):
    tq = TQ_FOX
    pairs = N_HEADS_B // 2
    nq = seq // tq
    qkv = qkb.reshape(batch, seq, 2 * N_HEADS_B * LANES)
    vv = vb.reshape(batch, seq, N_HEADS_B * LANES)
    all_chunks = jnp.tile(jnp.repeat(jnp.arange(nq, dtype=jnp.int32), 2), batch * pairs)

    def call(is_online, nlive):
        grid_spec = pltpu.PrefetchScalarGridSpec(
            num_scalar_prefetch=1,
            grid=(batch, pairs, nq),
            in_specs=[
                pl.BlockSpec((1, tq, 2 * LANES), lambda b, p, i, n: (b, i, p)),
                pl.BlockSpec((1, seq, 2 * LANES), lambda b, p, i, n: (b, 0, pairs + p)),
                pl.BlockSpec((1, seq, 2 * LANES), lambda b, p, i, n: (b, 0, p)),
            ],
            out_specs=pl.BlockSpec((1, tq, LANES), lambda b, p, i, n: (b, i, p)),
            scratch_shapes=[pltpu.VMEM((2, tq, 1), F32), pltpu.VMEM((2, tq, LANES), F32)],
        )
        return pl.pallas_call(
            functools.partial(_fox_kernel, online=is_online),
            grid_spec=grid_spec,
            out_shape=jax.ShapeDtypeStruct((batch, seq, OUT_B), BF16),
            compiler_params=pltpu.CompilerParams(
                dimension_semantics=("arbitrary", "arbitrary", "arbitrary"), vmem_limit_bytes=VMEM_LIMIT),
            name="fox_online" if is_online else "fox",
        )(nlive, qkv, qkv, vv)

    out = lax.cond(online, lambda: call(True, all_chunks),
                   lambda: call(False, _fox_live_chunks(cb, top, batch, seq)))
    return out.reshape(batch * seq, OUT_B)


def _fox_shift(q_norm_b, k_norm_b):
    bound = HEAD_DIM * (HEAD_DIM ** -0.5) * jnp.max(jnp.abs(q_norm_b)) * jnp.max(jnp.abs(k_norm_b))
    shift = jnp.maximum(2.0 * FOX_ROUNDING_SLACK * bound - FOX_EXP_HEADROOM, 0.0).astype(F32)
    top = 2.0 * FOX_ROUNDING_SLACK * bound - shift
    return shift, top.astype(F32), shift > FOX_MAX_SHIFT


def _post_kernel(x_ref, ya_ref, yb_ref, gmix_ref, wg_ref, bg_ref, wpa_ref, wpb_ref, wo_ref,
                 gffn_ref, wr_ref, rb_ref, x1_ref, h2_ref, topi_ref, topw_ref):
    d = x_ref.shape[1]
    x = x_ref[...]
    h = (x * lax.rsqrt(jnp.mean(x * x, axis=-1, keepdims=True) + EPS) * gmix_ref[...]).astype(BF16)
    gates = jax.nn.sigmoid(_dot(h, wg_ref[...]) + bg_ref[...])
    merged = gates[:, :d] * _dot(ya_ref[...], wpa_ref[...]) + gates[:, d:] * _dot(yb_ref[...], wpb_ref[...])
    x1 = x + _dot(merged.astype(BF16), wo_ref[...])
    x1_ref[...] = x1
    h2 = x1 * lax.rsqrt(jnp.mean(x1 * x1, axis=-1, keepdims=True) + EPS) * gffn_ref[...]
    h2_ref[...] = h2.astype(BF16)

    hh, hm, _ = _split3(h2)
    wr = wr_ref[...]
    wh = wr.astype(BF16)
    wl = (wr - wh.astype(F32)).astype(BF16)
    hh, hm = hh.astype(BF16), hm.astype(BF16)
    logits = _dot_nt(wh, hh) + _dot_nt(wh, hm) + _dot_nt(wl, hh)
    scores = jax.nn.sigmoid(logits)
    biased = scores + rb_ref[...]
    eid = lax.broadcasted_iota(jnp.int32, scores.shape, 0).astype(F32)
    chosen = jnp.zeros(scores.shape, jnp.bool_)
    idx, val = [], []
    for _ in range(TOP_K):
        cur = jnp.where(chosen, -jnp.inf, biased)
        mx = jnp.max(cur, axis=0, keepdims=True)
        first = jnp.min(jnp.where(cur == mx, eid, float(N_EXPERTS)), axis=0, keepdims=True)
        pick = eid == first
        chosen = jnp.logical_or(chosen, pick)
        idx.append(first)
        val.append(jnp.sum(jnp.where(pick, scores, 0.0), axis=0, keepdims=True))
    top_s = jnp.concatenate(val, axis=0)
    top_w = top_s / jnp.sum(top_s, axis=0, keepdims=True) * ROUTE_SCALE
    tm = scores.shape[1]
    both = jnp.concatenate(idx + [top_w, jnp.zeros((LANES - 2 * TOP_K, tm), F32)], axis=0).T
    topi_ref[...] = both[:, :TOP_K].astype(jnp.int32)
    topw_ref[...] = both[:, TOP_K:2 * TOP_K]


def _post(xf, ya, yb, g_mix, w_gate, b_gate, w_proj_a, w_proj_b, w_out, g_ffn, w_router, router_bias):
    n, d = xf.shape
    tm = TM_POST
    const = lambda shape: pl.BlockSpec(shape, lambda i: (0,) * len(shape))
    row = lambda w: pl.BlockSpec((tm, w), lambda i: (i, 0))
    args = [xf, ya, yb, g_mix.reshape(1, d), w_gate.astype(BF16), b_gate.reshape(1, 2 * d),
            w_proj_a.astype(BF16), w_proj_b.astype(BF16), w_out.astype(BF16), g_ffn.reshape(1, d),
            w_router.astype(F32).T, router_bias.astype(F32).reshape(N_EXPERTS, 1)]
    in_specs = [row(d), row(OUT_A), row(OUT_B)] + [const(a.shape) for a in args[3:]]
    return pl.pallas_call(
        _post_kernel,
        grid=(n // tm,),
        in_specs=in_specs,
        out_specs=[row(d), row(d), row(TOP_K), row(TOP_K)],
        out_shape=[jax.ShapeDtypeStruct((n, d), F32), jax.ShapeDtypeStruct((n, d), BF16),
                   jax.ShapeDtypeStruct((n, TOP_K), jnp.int32), jax.ShapeDtypeStruct((n, TOP_K), F32)],
        compiler_params=pltpu.CompilerParams(
            dimension_semantics=("arbitrary",), vmem_limit_bytes=VMEM_LIMIT),
        name="post",
    )(*args)


def _dispatch_kernel(h2_ref, topi_ref, tri_ref, upper_ref, xs_ref, slots_ref, cnt_ref, off_ref):
    tb = h2_ref.shape[0]
    topi = topi_ref[...]
    lane = lax.broadcasted_iota(jnp.int32, (tb, N_EXPERTS), 1)
    picks = [lane == topi[:, k:k + 1] for k in range(TOP_K)]
    mask = picks[0]
    for pk in picks[1:]:
        mask = jnp.logical_or(mask, pk)
    maskf = jnp.where(mask, 1.0, 0.0)
    rank = _dot(tri_ref[...], maskf.astype(BF16))
    cnt = jnp.sum(maskf, axis=0, keepdims=True)
    gran = jnp.floor((cnt + (GRAN - 1)) * (1.0 / GRAN))
    goff = _dot(jnp.broadcast_to(gran, (8, N_EXPERTS)).astype(BF16), upper_ref[...])[0:1]
    off = goff * GRAN
    slot_te = off + rank
    slots = jnp.concatenate(
        [jnp.sum(jnp.where(pk, slot_te, 0.0), axis=-1, keepdims=True) for pk in picks], axis=1)
    slots_ref[...] = slots.astype(jnp.int32)
    cnt_ref[0] = cnt.astype(jnp.int32)
    off_ref[0] = off.astype(jnp.int32)
    v = jnp.where(mask, slot_te, float(NO_SLOT))
    v_hi = jnp.floor(v * (1.0 / 64.0))
    w = jnp.concatenate([v_hi * 64.0, v - v_hi * 64.0], axis=1).T.astype(BF16)
    end = off + gran * GRAN
    used = jnp.max(end).astype(jnp.int32)
    h2 = h2_ref[...]

    def lookup(first, rows):
        s_e = (lax.broadcasted_iota(jnp.int32, (rows, N_EXPERTS), 0) + first).astype(F32)
        own = jnp.where(jnp.logical_and(s_e >= off, s_e < end), 1.0, 0.0)
        return _dot(jnp.concatenate([own, own], axis=1).astype(BF16), w)

    def sort_chunk(c, looked):
        s_t = (lax.broadcasted_iota(jnp.int32, (SLOT_CHUNK, tb), 0) + c * SLOT_CHUNK).astype(F32)
        onehot = jnp.where(looked == s_t, 1.0, 0.0).astype(BF16)
        xs_ref[0, c * SLOT_CHUNK:(c + 1) * SLOT_CHUNK, :] = _dot(onehot, h2).astype(BF16)

    looked_typ = lookup(0, CHUNKS_TYPICAL * SLOT_CHUNK)
    for c in range(CAP // SLOT_CHUNK):
        if c < CHUNKS_TYPICAL:
            sort_chunk(c, looked_typ[c * SLOT_CHUNK:(c + 1) * SLOT_CHUNK, :])
        else:
            @pl.when(c * SLOT_CHUNK < used)
            def _(c=c):
                sort_chunk(c, lookup(c * SLOT_CHUNK, SLOT_CHUNK))

            @pl.when(c * SLOT_CHUNK >= used)
            def _(c=c):
                xs_ref[0, c * SLOT_CHUNK:(c + 1) * SLOT_CHUNK, :] = jnp.zeros((SLOT_CHUNK, xs_ref.shape[2]), BF16)


def _dispatch(h2, topi):
    n, d = h2.shape
    nb = n // TB
    tri = jnp.asarray(np.tril(np.ones((TB, TB), np.float32), -1), BF16)
    upper = jnp.asarray(np.triu(np.ones((N_EXPERTS, N_EXPERTS), np.float32), 1), BF16)
    const = lambda shape: pl.BlockSpec(shape, lambda i: (0,) * len(shape))
    meta = pl.BlockSpec((1, 1, N_EXPERTS), lambda i: (i, 0, 0))
    return pl.pallas_call(
        _dispatch_kernel,
        grid=(nb,),
        in_specs=[pl.BlockSpec((TB, d), lambda i: (i, 0)), pl.BlockSpec((TB, TOP_K), lambda i: (i, 0)),
                  const(tri.shape), const(upper.shape)],
        out_specs=[pl.BlockSpec((1, CAP, d), lambda i: (i, 0, 0)),
                   pl.BlockSpec((TB, TOP_K), lambda i: (i, 0)), meta, meta],
        out_shape=[jax.ShapeDtypeStruct((nb, CAP, d), BF16), jax.ShapeDtypeStruct((n, TOP_K), jnp.int32),
                   jax.ShapeDtypeStruct((nb, 1, N_EXPERTS), jnp.int32),
                   jax.ShapeDtypeStruct((nb, 1, N_EXPERTS), jnp.int32)],
        compiler_params=pltpu.CompilerParams(
            dimension_semantics=("arbitrary",), vmem_limit_bytes=VMEM_LIMIT),
        name="dispatch",
    )(h2, topi, tri, upper)


def _ffn_kernel(item_e_sm, item_g0_sm, item_n_sm, glist_sm, xs_hbm, wg_ref, wu_ref, wd_ref, ys_hbm,
                xbuf, ybuf, wg_bf, wu_bf, wd_bf, sem_in, sem_out):
    step = pl.program_id(0)
    nsteps = pl.num_programs(0)
    buf = step % 2

    def for_granules(st, fn):
        g0 = item_g0_sm[st]
        n = item_n_sm[st]

        def per_granule(j, carry):
            fn(glist_sm[g0 + j], j)
            return carry

        lax.fori_loop(0, n, per_granule, 0)
        return n

    def fetch(b_):
        return lambda src, dst: pltpu.make_async_copy(xs_hbm.at[src], xbuf.at[b_, dst], sem_in.at[b_])

    def writeback(b_):
        return lambda src, dst: pltpu.make_async_copy(ybuf.at[b_, dst], ys_hbm.at[src], sem_out.at[b_])

    def start(mk):
        return lambda src, dst: mk(src, dst).start()

    def wait_all(st, span):
        n = item_n_sm[st]
        size = PASS_GRAN
        while size >= 1:
            @pl.when((n & size) != 0)
            def _(size=size):
                span(size).wait()
            size //= 2
        return n

    def fetch_span(b_):
        return lambda k: pltpu.make_async_copy(
            xs_hbm.at[pl.ds(0, k)], xbuf.at[b_, pl.ds(0, k)], sem_in.at[b_])

    def writeback_span(b_):
        return lambda k: pltpu.make_async_copy(
            ybuf.at[b_, pl.ds(0, k)], ys_hbm.at[pl.ds(0, k)], sem_out.at[b_])

    @pl.when(step == 0)
    def _():
        xbuf[...] = jnp.zeros_like(xbuf)
        for_granules(step, start(fetch(0)))

    @pl.when(step + 1 < nsteps)
    def _():
        for_granules(step + 1, start(fetch(1 - buf)))

    ngran = wait_all(step, fetch_span(buf))

    @pl.when(step >= 2)
    def _():
        wait_all(step - 2, writeback_span(buf))

    @pl.when(ngran > 0)
    def _():
        wg_bf[...] = wg_ref[0].astype(BF16)
        wu_bf[...] = wu_ref[0].astype(BF16)
        wd_bf[...] = wd_ref[0].astype(BF16)

    x_cols = xbuf.shape[-1]

    def ffn_rows(base, rows):
        grans = pl.ds(pl.multiple_of(base // GRAN, rows // GRAN), rows // GRAN)
        x = xbuf[buf, grans].reshape(rows, x_cols)
        g = _dot(x, wg_bf[...])
        u = _dot(x, wu_bf[...])
        mid = (g * jax.nn.sigmoid(g) * u).astype(BF16)
        ybuf[buf, grans] = _dot(mid, wd_bf[...]).astype(BF16).reshape(rows // GRAN, GRAN, x_cols)

    nt = (ngran * GRAN + (FT - 1)) // FT
    big = FT_BIG // FT

    def big_tile(i, carry):
        ffn_rows(pl.multiple_of(i * FT_BIG, FT_BIG), FT_BIG)
        return carry

    lax.fori_loop(0, nt // big, big_tile, 0)
    size = big // 2
    while size >= 1:
        @pl.when((nt & size) != 0)
        def _(size=size):
            ffn_rows(pl.multiple_of((nt & ~(2 * size - 1)) * FT, size * FT), size * FT)
        size //= 2

    for_granules(step, start(writeback(buf)))

    @pl.when(step == nsteps - 1)
    def _():
        wait_all(step, writeback_span(buf))

        @pl.when(step >= 1)
        def _():
            wait_all(step - 1, writeback_span(1 - buf))


def _work_items(cnt, off):
    nb = cnt.shape[0]
    seg_n = ((cnt.reshape(nb, N_EXPERTS) + (GRAN - 1)) // GRAN).T.reshape(-1)
    seg_row = ((off.reshape(nb, N_EXPERTS) + jnp.arange(nb, dtype=jnp.int32)[:, None] * CAP) // GRAN).T.reshape(-1)
    seg_end = jnp.cumsum(seg_n)
    seg_start = seg_end - seg_n
    gmax = nb * (TB * TOP_K // GRAN + N_EXPERTS)
    prev_end = jnp.concatenate([jnp.ones((1,), jnp.int32), (seg_row + seg_n)[:-1]])
    steps = jnp.ones((gmax,), jnp.int32).at[seg_start].add(seg_row - prev_end, mode="drop")
    glist = jnp.cumsum(steps)
    per_e = seg_n.reshape(N_EXPERTS, nb).sum(axis=1)
    first_e = seg_start[::nb]
    passes = (per_e + (PASS_GRAN - 1)) // PASS_GRAN
    pass_end = jnp.cumsum(passes)
    n_items = N_EXPERTS + gmax // PASS_GRAN
    w = jnp.arange(n_items, dtype=jnp.int32)
    item_e = jnp.minimum(jnp.sum(pass_end[None, :] <= w[:, None], axis=1), N_EXPERTS - 1).astype(jnp.int32)
    done = (w - (pass_end - passes)[item_e]) * PASS_GRAN
    item_n = jnp.clip(per_e[item_e] - done, 0, PASS_GRAN)
    item_g0 = first_e[item_e] + done
    return item_e, item_g0.astype(jnp.int32), item_n.astype(jnp.int32), glist.astype(jnp.int32)


def _ffn(xs, cnt, off, wg, wu, wd):
    _, _, d = xs.shape
    item_e, item_g0, item_n, glist = _work_items(cnt, off)
    per_expert = lambda shape: pl.BlockSpec((1,) + shape, lambda w, ie, g0, n, gl: (ie[w], 0, 0))
    grid_spec = pltpu.PrefetchScalarGridSpec(
        num_scalar_prefetch=4,
        grid=(item_e.shape[0],),
        in_specs=[pl.BlockSpec(memory_space=pl.ANY), per_expert((d, D_EXPERT)), per_expert((d, D_EXPERT)),
                  per_expert((D_EXPERT, d))],
        out_specs=pl.BlockSpec(memory_space=pl.ANY),
        scratch_shapes=[pltpu.VMEM((2, PASS_GRAN, GRAN, d), BF16)] * 2 + [
                        pltpu.VMEM((d, D_EXPERT), BF16), pltpu.VMEM((d, D_EXPERT), BF16),
                        pltpu.VMEM((D_EXPERT, d), BF16),
                        pltpu.SemaphoreType.DMA((2,)), pltpu.SemaphoreType.DMA((2,))],
    )
    return pl.pallas_call(
        _ffn_kernel,
        grid_spec=grid_spec,
        out_shape=jax.ShapeDtypeStruct(xs.shape, xs.dtype),
        input_output_aliases={4: 0},
        compiler_params=pltpu.CompilerParams(
            dimension_semantics=("arbitrary",), vmem_limit_bytes=VMEM_LIMIT),
        name="ffn",
    )(item_e, item_g0, item_n, glist, xs, wg, wu, wd)


def _combine_kernel(used_sm, x1_ref, h2_ref, ys_ref, slots_ref, topw_ref, wgus_ref, wds_ref, o_ref):
    tb = x1_ref.shape[0]
    gu = _dot(h2_ref[...], wgus_ref[...])
    g, u = gu[:, :D_SHARED], gu[:, D_SHARED:]
    acc = x1_ref[...] + _dot((g * jax.nn.sigmoid(g) * u).astype(BF16), wds_ref[...])
    slots = slots_ref[...].astype(F32)
    topw = topw_ref[...]
    used = used_sm[pl.program_id(0)]

    def gather_chunk(c):
        scol = (lax.broadcasted_iota(jnp.int32, (tb, SLOT_CHUNK), 1) + c * SLOT_CHUNK).astype(F32)
        gate = jnp.zeros((tb, SLOT_CHUNK), F32)
        for k in range(TOP_K):
            gate = jnp.where(scol == slots[:, k:k + 1], topw[:, k:k + 1], gate)
        return _dot(gate.astype(BF16), ys_ref[0, c * SLOT_CHUNK:(c + 1) * SLOT_CHUNK, :])

    for c in range(CHUNKS_TYPICAL):
        acc = acc + gather_chunk(c)
    o_ref[...] = acc
    for c in range(CHUNKS_TYPICAL, CAP // SLOT_CHUNK):
        @pl.when(c * SLOT_CHUNK < used)
        def _(c=c):
            o_ref[...] += gather_chunk(c)


def _combine(x1, h2, ys, slots, topw, used, wgus, wds):
    n, d = x1.shape
    const = lambda shape: pl.BlockSpec(shape, lambda i, u: (0,) * len(shape))
    row = lambda w: pl.BlockSpec((TB, w), lambda i, u: (i, 0))
    grid_spec = pltpu.PrefetchScalarGridSpec(
        num_scalar_prefetch=1,
        grid=(n // TB,),
        in_specs=[row(d), row(d), pl.BlockSpec((1, CAP, d), lambda i, u: (i, 0, 0)), row(TOP_K), row(TOP_K),
                  const(wgus.shape), const(wds.shape)],
        out_specs=row(d),
    )
    return pl.pallas_call(
        _combine_kernel,
        grid_spec=grid_spec,
        out_shape=jax.ShapeDtypeStruct((n, d), F32),
        compiler_params=pltpu.CompilerParams(
            dimension_semantics=("arbitrary",), vmem_limit_bytes=VMEM_LIMIT),
        name="combine",
    )(used, x1, h2, ys, slots, topw, wgus, wds)


def _moe(x1, h2, topi, topw, w_gate_e, w_up_e, w_down_e, w_gate_s, w_up_s, w_down_s):
    n, d = x1.shape
    wgus = jnp.concatenate([w_gate_s.astype(BF16), w_up_s.astype(BF16)], axis=-1)
    xs, slots, cnt, off = _dispatch(h2, topi)
    ys = _ffn(xs.reshape(-1, GRAN, d), cnt, off, w_gate_e, w_up_e, w_down_e)
    used = jnp.max(off + (cnt + (GRAN - 1)) // GRAN * GRAN, axis=(1, 2)).astype(jnp.int32)
    return _combine(x1, h2, ys.reshape(n // TB, CAP, d), slots, topw, used, wgus, w_down_s.astype(BF16))


def kernel(x, g_mix, w_in, q_norm_a, k_norm_a, q_norm_b, k_norm_b, rel_bias, b_forget, w_gate, b_gate,
           w_proj_a, w_proj_b, w_out, g_ffn, w_router, router_bias, w_gate_e, w_up_e, w_down_e,
           w_gate_s, w_up_s, w_down_s):
    batch, seq, d = x.shape
    xf = x.reshape(batch * seq, d)
    fox_shift, fox_top, fox_online = _fox_shift(q_norm_b, k_norm_b)
    pa0, pa1, pa2, qkb, vb, cb = _inproj(xf, g_mix, w_in, q_norm_a, k_norm_a, q_norm_b, k_norm_b, b_forget,
                                     fox_shift, seq)
    bias = jnp.stack([_toeplitz_bias(rel_bias, g, dil) for g, (_, dil) in enumerate(DIL_GROUPS)])
    ya = _dilated(pa0, pa1, pa2, bias, batch, seq)

    yb = _fox(qkb, vb, cb, fox_top, fox_online, batch, seq)
    x1, h2, topi, topw = _post(xf, ya, yb, g_mix, w_gate, b_gate, w_proj_a, w_proj_b, w_out, g_ffn,
                               w_router, router_bias)
    out = _moe(x1, h2, topi, topw, w_gate_e, w_up_e, w_down_e, w_gate_s, w_up_s, w_down_s)
    return out.reshape(batch, seq, d)
```

```python
import functools
import math

import jax
import jax.numpy as jnp
import numpy as np
from jax import lax
from jax.experimental import pallas as pl
from jax.experimental.pallas import tpu as pltpu

D_MODEL = 1024
HEAD_DIM = 64
DIL_GROUPS = ((128, 1), (512, 4), (2048, 16))
HEADS_PER_GROUP = 4
N_HEADS_A = HEADS_PER_GROUP * len(DIL_GROUPS)
N_HEADS_B = 8
REL_BUCKETS = 32
REL_MAX_DIST = 2048
N_EXPERTS = 64
TOP_K = 8
D_EXPERT = 256
D_SHARED = 256
ROUTE_SCALE = 2.5
EPS = 1e-6

WIDTH_A = 3 * N_HEADS_A * HEAD_DIM
WIDTH_B = 3 * N_HEADS_B * HEAD_DIM
QK_B = N_HEADS_B * HEAD_DIM
OUT_A = HEADS_PER_GROUP * HEAD_DIM
OUT_B = N_HEADS_B * HEAD_DIM

LANES = 128
GROUP_W = HEADS_PER_GROUP * HEAD_DIM
WIN_J = 128
SUPER = DIL_GROUPS[-1][1] * WIN_J
NEG = -1e30
VMEM_LIMIT = 56 * 1024 * 1024

DIL_UNROLL = 8
TM_IN = 512
TM_POST = 1024
TQ_FOX = 1024
FOX_ROUNDING_SLACK = 1.02
FOX_EXP_HEADROOM = 60.0
FOX_DEAD_EXPONENT = -105.0
FOX_MAX_SHIFT = 80.0
TB = 256
GRAN = 16
CAP = TB * TOP_K + N_EXPERTS * GRAN
SLOT_CHUNK = 512
CHUNKS_TYPICAL = -(-(TB * TOP_K + N_EXPERTS * GRAN // 2) // SLOT_CHUNK)
NO_SLOT = 4095
PASS_GRAN = 256
FT = 256
FT_BIG = 1024

BF16 = jnp.bfloat16
F32 = jnp.float32


def _dot(a, b):
    return jnp.dot(a, b, preferred_element_type=F32)


def _dot_nt(a, b):
    return lax.dot_general(a, b, (((1,), (1,)), ((), ())), preferred_element_type=F32)


def _split3(v):
    hi = v.astype(BF16).astype(F32)
    r = v - hi
    mid = r.astype(BF16).astype(F32)
    lo = (r - mid).astype(BF16).astype(F32)
    return hi, mid, lo


def _inproj_kernel(x_ref, g_ref, wa_ref, wb_ref, wf_ref, bd_ref, tri_ref, gain_a_ref, gain_b_ref,
                   bf_ref, shift_ref, pa0_ref, pa1_ref, pa2_ref, qkb_ref, vb_ref, cb_ref, carry_ref, h_ref, *,
                   tiles_per_seq):
    tm = x_ref.shape[0]
    x = x_ref[...]
    h = x * lax.rsqrt(jnp.mean(x * x, axis=-1, keepdims=True) + EPS) * g_ref[...]
    n_lane_chunks = h_ref.shape[0]
    for c in range(n_lane_chunks):
        h_ref[c] = h[:, c * LANES:(c + 1) * LANES]
    h = h.astype(BF16)
    bd = bd_ref[...]

    def headnorm(p, gain):
        ms = _dot((p * p).astype(BF16), bd)
        return p * lax.rsqrt(ms + EPS) * gain

    for g, (pa_ref, (_, dil)) in enumerate(zip((pa0_ref, pa1_ref, pa2_ref), DIL_GROUPS)):
        rows = tm // dil
        if dil == 1:
            hg = h
        else:
            hg = jnp.concatenate([jnp.concatenate(
                [h_ref[c, pl.ds(r, rows, stride=dil), :] for c in range(n_lane_chunks)], axis=1)
                for r in range(dil)], axis=0).astype(BF16)
        qkv = _dot(hg, wa_ref[g])
        for part in range(3):
            cols = slice(part * GROUP_W, (part + 1) * GROUP_W)
            p = qkv[:, cols]
            if part < 2:
                p = headnorm(p, gain_a_ref[part:part + 1, :])
            p = p.astype(BF16)
            if dil == 1:
                pa_ref[:, cols] = p
            else:
                for r in range(dil):
                    pa_ref[0, r, :, cols] = p[r * rows:(r + 1) * rows, :]

    f = _dot(h, wf_ref[...]) + bf_ref[...]
    logf = jnp.minimum(f, 0.0) - jnp.log1p(jnp.exp(-jnp.abs(f)))
    tri = tri_ref[...]
    lh, lm, ll = _split3(logf)
    cum3 = _dot(tri, jnp.concatenate([lh, lm, ll], axis=1).astype(BF16))
    cum = cum3[:, :LANES] + cum3[:, LANES:2 * LANES] + cum3[:, 2 * LANES:]

    @pl.when(pl.program_id(0) % tiles_per_seq == 0)
    def _():
        carry_ref[...] = jnp.zeros_like(carry_ref)

    cum = cum + carry_ref[0:1, :]
    carry_ref[0:1, :] = cum[tm - 1:tm, :]
    cb_ref[0] = jnp.concatenate([cum[0:1, :], cum[tm - 1:tm, :], jnp.zeros((6, LANES), F32)], axis=0)
    ch, cm, cl = _split3(cum)

    j = lax.broadcasted_iota(jnp.int32, (tm, HEAD_DIM), 1)

    def ext_cols(vals):
        out = jnp.zeros((tm, HEAD_DIM), F32)
        ones = [pos for pos, val in enumerate(vals) if isinstance(val, float)]
        if ones:
            is_one = functools.reduce(jnp.logical_or, [j == pos for pos in ones])
            out = jnp.where(is_one, 1.0, out)
        for pos, val in enumerate(vals):
            if not isinstance(val, float):
                out = jnp.where(j == pos, val, out)
        return out

    for c in range(QK_B // GROUP_W):
        qkv = _dot(h, wb_ref[c])
        pq = headnorm(qkv[:, 0:GROUP_W], gain_b_ref[0:1, :])
        pk = headnorm(qkv[:, GROUP_W:2 * GROUP_W], gain_b_ref[1:2, :])
        pv = qkv[:, 2 * GROUP_W:3 * GROUP_W]
        r = _dot((pq * pk).astype(BF16), bd) * HEAD_DIM + shift_ref[...]
        for hh in range(HEADS_PER_GROUP):
            head = c * HEADS_PER_GROUP + hh
            lanes = slice(hh * HEAD_DIM, (hh + 1) * HEAD_DIM)
            col = lambda a, idx: a[:, idx:idx + 1]
            cs = [col(ch, head), col(cm, head), col(cl, head)]
            ext_q = ext_cols(cs + [1.0] * 3 + [-col(r, hh * HEAD_DIM)])
            ext_k = ext_cols([1.0] * 3 + [-v for v in cs] + [1.0])
            ext_v = ext_cols([1.0])
            for part, (val, ext) in enumerate(((pq, ext_q), (pk, ext_k))):
                o0 = (part * N_HEADS_B + head) * LANES
                qkb_ref[:, o0:o0 + LANES] = jnp.concatenate([val[:, lanes], ext], axis=-1).astype(BF16)
            vb_ref[:, head * LANES:(head + 1) * LANES] = jnp.concatenate(
                [pv[:, lanes], ext_v], axis=-1).astype(BF16)


def _inproj(xf, g_mix, w_in, q_norm_a, k_norm_a, q_norm_b, k_norm_b, b_forget, fox_shift, seq):
    n, d = xf.shape
    tm = TM_IN
    scale = HEAD_DIM ** -0.5
    w_bf = w_in.astype(BF16)
    qkv_w = N_HEADS_A * HEAD_DIM
    wa = jnp.stack([jnp.concatenate(
        [w_bf[:, part * qkv_w + g * GROUP_W: part * qkv_w + (g + 1) * GROUP_W] for part in range(3)],
        axis=1) for g in range(len(DIL_GROUPS))])
    wb = jnp.stack([jnp.concatenate(
        [w_bf[:, WIDTH_A + part * QK_B + c * GROUP_W: WIDTH_A + part * QK_B + (c + 1) * GROUP_W]
         for part in range(3)], axis=1) for c in range(QK_B // GROUP_W)])
    wf = jnp.pad(w_bf[:, WIDTH_A + WIDTH_B:], ((0, 0), (0, LANES - N_HEADS_B)))
    bfp = jnp.pad(b_forget.astype(F32), (0, LANES - N_HEADS_B)).reshape(1, LANES)
    seg = np.arange(GROUP_W) // HEAD_DIM
    bd = jnp.asarray((seg[:, None] == seg[None, :]).astype(np.float32) / HEAD_DIM, BF16)
    tri = jnp.asarray(np.tril(np.ones((tm, tm), np.float32)), BF16)
    gain_a = jnp.stack([jnp.tile(q_norm_a, HEADS_PER_GROUP) * scale, jnp.tile(k_norm_a, HEADS_PER_GROUP)])
    gain_b = jnp.stack([jnp.tile(q_norm_b, HEADS_PER_GROUP) * scale, jnp.tile(k_norm_b, HEADS_PER_GROUP)])
    const = lambda shape: pl.BlockSpec(shape, lambda i: (0,) * len(shape))
    tps = seq // tm
    batch = n // seq
    qkv3 = 3 * GROUP_W
    (_, d1), (_, d2) = DIL_GROUPS[1], DIL_GROUPS[2]
    return pl.pallas_call(
        functools.partial(_inproj_kernel, tiles_per_seq=tps),
        grid=(n // tm,),
        in_specs=[
            pl.BlockSpec((tm, d), lambda i: (i, 0)),
            const((1, d)), const(wa.shape), const(wb.shape), const(wf.shape),
            const(bd.shape), const(tri.shape), const(gain_a.shape), const(gain_b.shape),
            const(bfp.shape), const((1, 1)),
        ],
        out_specs=[
            pl.BlockSpec((tm, qkv3), lambda i: (i, 0)),
            pl.BlockSpec((1, d1, tm // d1, qkv3), lambda i: (i // tps, 0, i % tps, 0)),
            pl.BlockSpec((1, d2, tm // d2, qkv3), lambda i: (i // tps, 0, i % tps, 0)),
            pl.BlockSpec((tm, 2 * N_HEADS_B * LANES), lambda i: (i, 0)),
            pl.BlockSpec((tm, N_HEADS_B * LANES), lambda i: (i, 0)),
            pl.BlockSpec((1, 8, LANES), lambda i: (i, 0, 0)),
        ],
        out_shape=[
            jax.ShapeDtypeStruct((n, qkv3), BF16),
            jax.ShapeDtypeStruct((batch, d1, seq // d1, qkv3), BF16),
            jax.ShapeDtypeStruct((batch, d2, seq // d2, qkv3), BF16),
            jax.ShapeDtypeStruct((n, 2 * N_HEADS_B * LANES), BF16),
            jax.ShapeDtypeStruct((n, N_HEADS_B * LANES), BF16),
            jax.ShapeDtypeStruct((n // tm, 8, LANES), F32),
        ],
        scratch_shapes=[pltpu.VMEM((8, LANES), F32), pltpu.VMEM((d // LANES, tm, LANES), F32)],
        compiler_params=pltpu.CompilerParams(
            dimension_semantics=("arbitrary",), vmem_limit_bytes=VMEM_LIMIT),
        name="inproj",
    )(xf, g_mix.reshape(1, d), wa, wb, wf, bd, tri, gain_a, gain_b, bfp, fox_shift.reshape(1, 1))


def _dilated_kernel(p0_ref, h0_ref, p1_ref, h1_ref, p2_ref, h2_ref, bias_ref, o_ref, acc_ref, lse_ref):
    tq = WIN_J
    first_sb = pl.program_id(1) == 0
    lane_head = lax.broadcasted_iota(jnp.int32, (tq, GROUP_W), 1) // HEAD_DIM
    prev_col = lax.broadcasted_iota(jnp.int32, (tq, 2 * tq), 1) < tq
    qc, kc_, vc_ = (slice(0, GROUP_W), slice(GROUP_W, 2 * GROUP_W), slice(2 * GROUP_W, 3 * GROUP_W))

    def attend(g, q, kp, kc, vp, vc, no_prev):
        kcat = jnp.concatenate([kp, kc], axis=0)
        vcat = jnp.concatenate([vp, vc], axis=0)
        dead = jnp.logical_and(no_prev, prev_col)
        q4 = jnp.concatenate([jnp.where(lane_head == hh, q, jnp.zeros_like(q))
                              for hh in range(HEADS_PER_GROUP)], axis=0)
        s = _dot_nt(q4, kcat) + bias_ref[g].reshape(HEADS_PER_GROUP * tq, 2 * tq)
        s = jnp.where(jnp.concatenate([dead] * HEADS_PER_GROUP, axis=0), NEG, s)
        m = jnp.max(s, axis=-1, keepdims=True)
        p = jnp.exp(s - m)
        l = jnp.sum(p, axis=-1, keepdims=True)
        o4 = _dot(p.astype(BF16), vcat) * (1.0 / l)
        lse4 = m + jnp.log(l)
        acc = o4[0:tq]
        lse = jnp.broadcast_to(lse4[0:tq], (tq, GROUP_W))
        for hh in range(1, HEADS_PER_GROUP):
            sel = lane_head == hh
            acc = jnp.where(sel, o4[hh * tq:(hh + 1) * tq], acc)
            lse = jnp.where(sel, lse4[hh * tq:(hh + 1) * tq], lse)
        return acc, lse

    n_half = GROUP_W // LANES

    def merge(rows, acc, lse):
        for c in range(n_half):
            lanes = slice(c * LANES, (c + 1) * LANES)
            l1 = lse_ref[c, rows, :]
            mx = jnp.maximum(l1, lse[:, lanes])
            w1 = jnp.exp(l1 - mx)
            w2 = jnp.exp(lse[:, lanes] - mx)
            den = w1 + w2
            acc_ref[c, rows, :] = (w1 * acc_ref[c, rows, :] + w2 * acc[:, lanes]) / den
            lse_ref[c, rows, :] = mx + jnp.log(den)

    def pick(first, halo, body):
        return jnp.where(first, halo, body)

    def loop(n, body):
        def trip(i, carry):
            for u in range(DIL_UNROLL):
                body(i * DIL_UNROLL + u, carry)
            return carry
        lax.fori_loop(0, n // DIL_UNROLL, trip, 0)

    def body0(j, carry):
        st = pl.multiple_of(j * tq, tq)
        pst = pl.multiple_of(jnp.maximum(j - 1, 0) * tq, tq)
        cur, prv = pl.ds(st, tq), pl.ds(pst, tq)
        acc, lse = attend(
            0, p0_ref[0, cur, qc],
            pick(j == 0, h0_ref[0, :, kc_], p0_ref[0, prv, kc_]), p0_ref[0, cur, kc_],
            pick(j == 0, h0_ref[0, :, vc_], p0_ref[0, prv, vc_]), p0_ref[0, cur, vc_],
            jnp.logical_and(j == 0, first_sb))
        for c in range(n_half):
            acc_ref[c, cur, :] = acc[:, c * LANES:(c + 1) * LANES]
            lse_ref[c, cur, :] = lse[:, c * LANES:(c + 1) * LANES]
        return carry

    loop(SUPER // tq, body0)

    d1 = DIL_GROUPS[1][1]
    nsub1 = SUPER // d1 // tq
    def body1(t, carry):
        r, ii = t // nsub1, t % nsub1
        st = pl.multiple_of(ii * tq, tq)
        pst = pl.multiple_of(jnp.maximum(ii - 1, 0) * tq, tq)
        cur, prv = pl.ds(st, tq), pl.ds(pst, tq)
        acc, lse = attend(
            1, p1_ref[0, r, cur, qc],
            pick(ii == 0, h1_ref[0, r, :, kc_], p1_ref[0, r, prv, kc_]), p1_ref[0, r, cur, kc_],
            pick(ii == 0, h1_ref[0, r, :, vc_], p1_ref[0, r, prv, vc_]), p1_ref[0, r, cur, vc_],
            jnp.logical_and(ii == 0, first_sb))
        merge(pl.ds(ii * (tq * d1) + r, tq, stride=d1), acc, lse)
        return carry

    loop(d1 * nsub1, body1)

    d2 = DIL_GROUPS[2][1]

    def body2(r, carry):
        acc, lse = attend(2, p2_ref[0, r, :, qc], h2_ref[0, r, :, kc_], p2_ref[0, r, :, kc_],
                          h2_ref[0, r, :, vc_], p2_ref[0, r, :, vc_], first_sb)
        merge(pl.ds(r, tq, stride=d2), acc, lse)
        return carry

    loop(d2, body2)

    for c in range(n_half):
        o_ref[0, :, c * LANES:(c + 1) * LANES] = acc_ref[c].astype(o_ref.dtype)


def _rel_bucket(dist):
    max_exact = REL_BUCKETS // 2
    n = jnp.maximum(dist.astype(F32), 1.0)
    large = max_exact + (jnp.log(n / max_exact) / math.log(REL_MAX_DIST / max_exact)
                         * (REL_BUCKETS - max_exact)).astype(jnp.int32)
    large = jnp.minimum(large, REL_BUCKETS - 1)
    return jnp.where(dist < max_exact, dist, large)


def _toeplitz_bias(rel_bias, g, dil):
    tq = WIN_J
    offs = dil * (WIN_J - jnp.arange(WIN_J + 1, dtype=jnp.int32))
    hs = slice(g * HEADS_PER_GROUP, (g + 1) * HEADS_PER_GROUP)
    tab_rev = rel_bias[_rel_bucket(offs)][:, hs].T.astype(F32)
    period = 3 * tq
    neg = lambda w: jnp.full((HEADS_PER_GROUP, w), NEG, F32)
    vec = jnp.concatenate([neg(tq - 1), tab_rev, neg(period - 2 * tq)], axis=1)
    flat = jnp.broadcast_to(vec[:, None, :], (HEADS_PER_GROUP, tq, period)).reshape(HEADS_PER_GROUP, -1)
    skew = flat[:, :tq * (period - 1)].reshape(HEADS_PER_GROUP, tq, period - 1)
    return skew[:, :, tq - 1:3 * tq - 1]


def _dilated(pa0, pa1, pa2, bias, batch, seq):
    tq = WIN_J
    qkv3 = 3 * GROUP_W
    (_, d1), (_, d2) = DIL_GROUPS[1], DIL_GROUPS[2]
    nsb = seq // SUPER
    p0 = pa0.reshape(batch, seq, qkv3)
    prev_blk = lambda per_sb: (lambda b, s: jnp.maximum(s * per_sb - 1, 0))
    h0i, h1i, h2i = prev_blk(SUPER // tq), prev_blk(SUPER // d1 // tq), prev_blk(SUPER // d2 // tq)
    out = pl.pallas_call(
        _dilated_kernel,
        grid=(batch, nsb),
        in_specs=[
            pl.BlockSpec((1, SUPER, qkv3), lambda b, s: (b, s, 0)),
            pl.BlockSpec((1, tq, qkv3), lambda b, s: (b, h0i(b, s), 0)),
            pl.BlockSpec((1, d1, SUPER // d1, qkv3), lambda b, s: (b, 0, s, 0)),
            pl.BlockSpec((1, d1, tq, qkv3), lambda b, s: (b, 0, h1i(b, s), 0)),
            pl.BlockSpec((1, d2, SUPER // d2, qkv3), lambda b, s: (b, 0, s, 0)),
            pl.BlockSpec((1, d2, tq, qkv3), lambda b, s: (b, 0, h2i(b, s), 0)),
            pl.BlockSpec(bias.shape, lambda b, s: (0, 0, 0, 0)),
        ],
        out_specs=pl.BlockSpec((1, SUPER, GROUP_W), lambda b, s: (b, s, 0)),
        out_shape=jax.ShapeDtypeStruct((batch, seq, GROUP_W), BF16),
        scratch_shapes=[pltpu.VMEM((GROUP_W // LANES, SUPER, LANES), F32)] * 2,
        compiler_params=pltpu.CompilerParams(
            dimension_semantics=("arbitrary", "arbitrary"), vmem_limit_bytes=VMEM_LIMIT),
        name="dilated",
    )(p0, p0, pa1, pa1, pa2, pa2, bias)
    return out.reshape(batch * seq, GROUP_W)


def _fox_kernel(nlive_sm, q_ref, k_ref, v_ref, o_ref, m_ref, acc_ref, *, online):
    tq = q_ref.shape[1]
    half = tq // 2
    qi = pl.program_id(2)
    step = (pl.program_id(0) * pl.num_programs(1) + pl.program_id(1)) * pl.num_programs(2) + qi
    row = lax.broadcasted_iota(jnp.int32, (half, half), 0)
    col = lax.broadcasted_iota(jnp.int32, (half, half), 1)
    causal = row >= col

    def attend(hh, rows, state, start, nkeys, masked):
        m, acc = state
        lanes = slice(hh * LANES, (hh + 1) * LANES)
        s = _dot_nt(q_ref[0, rows, lanes], k_ref[0, pl.ds(start, nkeys), lanes])
        if masked:
            s = jnp.where(causal, s, NEG)
        if online:
            m_new = jnp.maximum(m, jnp.max(s, axis=-1, keepdims=True))
            acc = acc * jnp.exp(m - m_new)
            s = s - m_new
            m = m_new
        return m, acc + _dot(jnp.exp(s).astype(BF16), v_ref[0, pl.ds(start, nkeys), lanes])

    first = [qi - nlive_sm[2 * step + hh] for hh in range(2)]
    for hh in range(2):
        m_ref[hh] = jnp.full((tq, 1), NEG, F32)
        acc_ref[hh] = jnp.zeros((tq, LANES), F32)

    def full_chunk(ki, carry):
        start = pl.multiple_of(ki * tq, tq)
        for hh in range(2):
            @pl.when(ki >= first[hh])
            def _(hh=hh):
                m, acc = attend(hh, slice(None), (m_ref[hh], acc_ref[hh]), start, tq, False)
                acc_ref[hh] = acc
                if online:
                    m_ref[hh] = m
        return carry

    lax.fori_loop(jnp.minimum(first[0], first[1]), qi, full_chunk, 0)

    d0 = pl.multiple_of(qi * tq, tq)
    outs = []
    for hh in range(2):
        m, acc = m_ref[hh], acc_ref[hh]
        top, bot = slice(0, half), slice(half, tq)
        s_top = attend(hh, top, (m[top], acc[top]), d0, half, True)
        s_bot = attend(hh, bot, (m[bot], acc[bot]), d0, half, False)
        s_bot = attend(hh, bot, s_bot, d0 + half, half, True)
        a = jnp.concatenate([s_top[1], s_bot[1]], axis=0)
        outs.append(a[:, :HEAD_DIM] / a[:, HEAD_DIM:HEAD_DIM + 1])
    o_ref[0] = jnp.concatenate(outs, axis=-1).astype(o_ref.dtype)


def _fox_live_chunks(cb, top, batch, seq):
    tps = seq // TM_IN
    per = TQ_FOX // TM_IN
    nq = seq // TQ_FOX
    c_first = cb[:, 0, :N_HEADS_B].reshape(batch, tps, N_HEADS_B)[:, ::per]
    c_last = cb[:, 1, :N_HEADS_B].reshape(batch, tps, N_HEADS_B)[:, per - 1::per]
    live = (top + c_first[:, :, None, :] - c_last[:, None, :, :]) >= FOX_DEAD_EXPONENT
    back = jnp.arange(nq)[:, None] - jnp.arange(nq)[None, :]
    reach = jnp.max(jnp.where(jnp.logical_and(live, (back > 0)[None, :, :, None]),
                              back[None, :, :, None], 0), axis=2)
    reach = reach.reshape(batch, nq, N_HEADS_B // 2, 2).transpose(0, 2, 1, 3)
    return reach.reshape(-1).astype(jnp.int32)


def _fox(qkb, vb, cb, top, online, batch, seq):
    tq = TQ_FOX
    pairs = N_HEADS_B // 2
    nq = seq // tq
    qkv = qkb.reshape(batch, seq, 2 * N_HEADS_B * LANES)
    vv = vb.reshape(batch, seq, N_HEADS_B * LANES)
    all_chunks = jnp.tile(jnp.repeat(jnp.arange(nq, dtype=jnp.int32), 2), batch * pairs)

    def call(is_online, nlive):
        grid_spec = pltpu.PrefetchScalarGridSpec(
            num_scalar_prefetch=1,
            grid=(batch, pairs, nq),
            in_specs=[
                pl.BlockSpec((1, tq, 2 * LANES), lambda b, p, i, n: (b, i, p)),
                pl.BlockSpec((1, seq, 2 * LANES), lambda b, p, i, n: (b, 0, pairs + p)),
                pl.BlockSpec((1, seq, 2 * LANES), lambda b, p, i, n: (b, 0, p)),
            ],
            out_specs=pl.BlockSpec((1, tq, LANES), lambda b, p, i, n: (b, i, p)),
            scratch_shapes=[pltpu.VMEM((2, tq, 1), F32), pltpu.VMEM((2, tq, LANES), F32)],
        )
        return pl.pallas_call(
            functools.partial(_fox_kernel, online=is_online),
            grid_spec=grid_spec,
            out_shape=jax.ShapeDtypeStruct((batch, seq, OUT_B), BF16),
            compiler_params=pltpu.CompilerParams(
                dimension_semantics=("arbitrary", "arbitrary", "arbitrary"), vmem_limit_bytes=VMEM_LIMIT),
            name="fox_online" if is_online else "fox",
        )(nlive, qkv, qkv, vv)

    out = lax.cond(online, lambda: call(True, all_chunks),
                   lambda: call(False, _fox_live_chunks(cb, top, batch, seq)))
    return out.reshape(batch * seq, OUT_B)


def _fox_shift(q_norm_b, k_norm_b):
    bound = HEAD_DIM * (HEAD_DIM ** -0.5) * jnp.max(jnp.abs(q_norm_b)) * jnp.max(jnp.abs(k_norm_b))
    shift = jnp.maximum(2.0 * FOX_ROUNDING_SLACK * bound - FOX_EXP_HEADROOM, 0.0).astype(F32)
    top = 2.0 * FOX_ROUNDING_SLACK * bound - shift
    return shift, top.astype(F32), shift > FOX_MAX_SHIFT


def _post_kernel(x_ref, ya_ref, yb_ref, gmix_ref, wg_ref, bg_ref, wpa_ref, wpb_ref, wo_ref,
                 gffn_ref, wr_ref, rb_ref, x1_ref, h2_ref, topi_ref, topw_ref):
    d = x_ref.shape[1]
    x = x_ref[...]
    h = (x * lax.rsqrt(jnp.mean(x * x, axis=-1, keepdims=True) + EPS) * gmix_ref[...]).astype(BF16)
    gates = jax.nn.sigmoid(_dot(h, wg_ref[...]) + bg_ref[...])
    merged = gates[:, :d] * _dot(ya_ref[...], wpa_ref[...]) + gates[:, d:] * _dot(yb_ref[...], wpb_ref[...])
    x1 = x + _dot(merged.astype(BF16), wo_ref[...])
    x1_ref[...] = x1
    h2 = x1 * lax.rsqrt(jnp.mean(x1 * x1, axis=-1, keepdims=True) + EPS) * gffn_ref[...]
    h2_ref[...] = h2.astype(BF16)

    hh, hm, _ = _split3(h2)
    wr = wr_ref[...]
    wh = wr.astype(BF16)
    wl = (wr - wh.astype(F32)).astype(BF16)
    hh, hm = hh.astype(BF16), hm.astype(BF16)
    logits = _dot_nt(wh, hh) + _dot_nt(wh, hm) + _dot_nt(wl, hh)
    scores = jax.nn.sigmoid(logits)
    biased = scores + rb_ref[...]
    eid = lax.broadcasted_iota(jnp.int32, scores.shape, 0).astype(F32)
    chosen = jnp.zeros(scores.shape, jnp.bool_)
    idx, val = [], []
    for _ in range(TOP_K):
        cur = jnp.where(chosen, -jnp.inf, biased)
        mx = jnp.max(cur, axis=0, keepdims=True)
        first = jnp.min(jnp.where(cur == mx, eid, float(N_EXPERTS)), axis=0, keepdims=True)
        pick = eid == first
        chosen = jnp.logical_or(chosen, pick)
        idx.append(first)
        val.append(jnp.sum(jnp.where(pick, scores, 0.0), axis=0, keepdims=True))
    top_s = jnp.concatenate(val, axis=0)
    top_w = top_s / jnp.sum(top_s, axis=0, keepdims=True) * ROUTE_SCALE
    tm = scores.shape[1]
    both = jnp.concatenate(idx + [top_w, jnp.zeros((LANES - 2 * TOP_K, tm), F32)], axis=0).T
    topi_ref[...] = both[:, :TOP_K].astype(jnp.int32)
    topw_ref[...] = both[:, TOP_K:2 * TOP_K]


def _post(xf, ya, yb, g_mix, w_gate, b_gate, w_proj_a, w_proj_b, w_out, g_ffn, w_router, router_bias):
    n, d = xf.shape
    tm = TM_POST
    const = lambda shape: pl.BlockSpec(shape, lambda i: (0,) * len(shape))
    row = lambda w: pl.BlockSpec((tm, w), lambda i: (i, 0))
    args = [xf, ya, yb, g_mix.reshape(1, d), w_gate.astype(BF16), b_gate.reshape(1, 2 * d),
            w_proj_a.astype(BF16), w_proj_b.astype(BF16), w_out.astype(BF16), g_ffn.reshape(1, d),
            w_router.astype(F32).T, router_bias.astype(F32).reshape(N_EXPERTS, 1)]
    in_specs = [row(d), row(OUT_A), row(OUT_B)] + [const(a.shape) for a in args[3:]]
    return pl.pallas_call(
        _post_kernel,
        grid=(n // tm,),
        in_specs=in_specs,
        out_specs=[row(d), row(d), row(TOP_K), row(TOP_K)],
        out_shape=[jax.ShapeDtypeStruct((n, d), F32), jax.ShapeDtypeStruct((n, d), BF16),
                   jax.ShapeDtypeStruct((n, TOP_K), jnp.int32), jax.ShapeDtypeStruct((n, TOP_K), F32)],
        compiler_params=pltpu.CompilerParams(
            dimension_semantics=("arbitrary",), vmem_limit_bytes=VMEM_LIMIT),
        name="post",
    )(*args)


def _dispatch_kernel(h2_ref, topi_ref, tri_ref, upper_ref, xs_ref, slots_ref, cnt_ref, off_ref):
    tb = h2_ref.shape[0]
    topi = topi_ref[...]
    lane = lax.broadcasted_iota(jnp.int32, (tb, N_EXPERTS), 1)
    picks = [lane == topi[:, k:k + 1] for k in range(TOP_K)]
    mask = picks[0]
    for pk in picks[1:]:
        mask = jnp.logical_or(mask, pk)
    maskf = jnp.where(mask, 1.0, 0.0)
    rank = _dot(tri_ref[...], maskf.astype(BF16))
    cnt = jnp.sum(maskf, axis=0, keepdims=True)
    gran = jnp.floor((cnt + (GRAN - 1)) * (1.0 / GRAN))
    goff = _dot(jnp.broadcast_to(gran, (8, N_EXPERTS)).astype(BF16), upper_ref[...])[0:1]
    off = goff * GRAN
    slot_te = off + rank
    slots = jnp.concatenate(
        [jnp.sum(jnp.where(pk, slot_te, 0.0), axis=-1, keepdims=True) for pk in picks], axis=1)
    slots_ref[...] = slots.astype(jnp.int32)
    cnt_ref[0] = cnt.astype(jnp.int32)
    off_ref[0] = off.astype(jnp.int32)
    v = jnp.where(mask, slot_te, float(NO_SLOT))
    v_hi = jnp.floor(v * (1.0 / 64.0))
    w = jnp.concatenate([v_hi * 64.0, v - v_hi * 64.0], axis=1).T.astype(BF16)
    end = off + gran * GRAN
    used = jnp.max(end).astype(jnp.int32)
    h2 = h2_ref[...]

    def lookup(first, rows):
        s_e = (lax.broadcasted_iota(jnp.int32, (rows, N_EXPERTS), 0) + first).astype(F32)
        own = jnp.where(jnp.logical_and(s_e >= off, s_e < end), 1.0, 0.0)
        return _dot(jnp.concatenate([own, own], axis=1).astype(BF16), w)

    def sort_chunk(c, looked):
        s_t = (lax.broadcasted_iota(jnp.int32, (SLOT_CHUNK, tb), 0) + c * SLOT_CHUNK).astype(F32)
        onehot = jnp.where(looked == s_t, 1.0, 0.0).astype(BF16)
        xs_ref[0, c * SLOT_CHUNK:(c + 1) * SLOT_CHUNK, :] = _dot(onehot, h2).astype(BF16)

    looked_typ = lookup(0, CHUNKS_TYPICAL * SLOT_CHUNK)
    for c in range(CAP // SLOT_CHUNK):
        if c < CHUNKS_TYPICAL:
            sort_chunk(c, looked_typ[c * SLOT_CHUNK:(c + 1) * SLOT_CHUNK, :])
        else:
            @pl.when(c * SLOT_CHUNK < used)
            def _(c=c):
                sort_chunk(c, lookup(c * SLOT_CHUNK, SLOT_CHUNK))

            @pl.when(c * SLOT_CHUNK >= used)
            def _(c=c):
                xs_ref[0, c * SLOT_CHUNK:(c + 1) * SLOT_CHUNK, :] = jnp.zeros((SLOT_CHUNK, xs_ref.shape[2]), BF16)


def _dispatch(h2, topi):
    n, d = h2.shape
    nb = n // TB
    tri = jnp.asarray(np.tril(np.ones((TB, TB), np.float32), -1), BF16)
    upper = jnp.asarray(np.triu(np.ones((N_EXPERTS, N_EXPERTS), np.float32), 1), BF16)
    const = lambda shape: pl.BlockSpec(shape, lambda i: (0,) * len(shape))
    meta = pl.BlockSpec((1, 1, N_EXPERTS), lambda i: (i, 0, 0))
    return pl.pallas_call(
        _dispatch_kernel,
        grid=(nb,),
        in_specs=[pl.BlockSpec((TB, d), lambda i: (i, 0)), pl.BlockSpec((TB, TOP_K), lambda i: (i, 0)),
                  const(tri.shape), const(upper.shape)],
        out_specs=[pl.BlockSpec((1, CAP, d), lambda i: (i, 0, 0)),
                   pl.BlockSpec((TB, TOP_K), lambda i: (i, 0)), meta, meta],
        out_shape=[jax.ShapeDtypeStruct((nb, CAP, d), BF16), jax.ShapeDtypeStruct((n, TOP_K), jnp.int32),
                   jax.ShapeDtypeStruct((nb, 1, N_EXPERTS), jnp.int32),
                   jax.ShapeDtypeStruct((nb, 1, N_EXPERTS), jnp.int32)],
        compiler_params=pltpu.CompilerParams(
            dimension_semantics=("arbitrary",), vmem_limit_bytes=VMEM_LIMIT),
        name="dispatch",
    )(h2, topi, tri, upper)


def _ffn_kernel(item_e_sm, item_g0_sm, item_n_sm, glist_sm, xs_hbm, wg_ref, wu_ref, wd_ref, ys_hbm,
                xbuf, ybuf, wg_bf, wu_bf, wd_bf, sem_in, sem_out):
    step = pl.program_id(0)
    nsteps = pl.num_programs(0)
    buf = step % 2

    def for_granules(st, fn):
        g0 = item_g0_sm[st]
        n = item_n_sm[st]

        def per_granule(j, carry):
            fn(glist_sm[g0 + j], j)
            return carry

        lax.fori_loop(0, n, per_granule, 0)
        return n

    def fetch(b_):
        return lambda src, dst: pltpu.make_async_copy(xs_hbm.at[src], xbuf.at[b_, dst], sem_in.at[b_])

    def writeback(b_):
        return lambda src, dst: pltpu.make_async_copy(ybuf.at[b_, dst], ys_hbm.at[src], sem_out.at[b_])

    def start(mk):
        return lambda src, dst: mk(src, dst).start()

    def wait_all(st, span):
        n = item_n_sm[st]
        size = PASS_GRAN
        while size >= 1:
            @pl.when((n & size) != 0)
            def _(size=size):
                span(size).wait()
            size //= 2
        return n

    def fetch_span(b_):
        return lambda k: pltpu.make_async_copy(
            xs_hbm.at[pl.ds(0, k)], xbuf.at[b_, pl.ds(0, k)], sem_in.at[b_])

    def writeback_span(b_):
        return lambda k: pltpu.make_async_copy(
            ybuf.at[b_, pl.ds(0, k)], ys_hbm.at[pl.ds(0, k)], sem_out.at[b_])

    @pl.when(step == 0)
    def _():
        xbuf[...] = jnp.zeros_like(xbuf)
        for_granules(step, start(fetch(0)))

    @pl.when(step + 1 < nsteps)
    def _():
        for_granules(step + 1, start(fetch(1 - buf)))

    ngran = wait_all(step, fetch_span(buf))

    @pl.when(step >= 2)
    def _():
        wait_all(step - 2, writeback_span(buf))

    @pl.when(ngran > 0)
    def _():
        wg_bf[...] = wg_ref[0].astype(BF16)
        wu_bf[...] = wu_ref[0].astype(BF16)
        wd_bf[...] = wd_ref[0].astype(BF16)

    x_cols = xbuf.shape[-1]

    def ffn_rows(base, rows):
        grans = pl.ds(pl.multiple_of(base // GRAN, rows // GRAN), rows // GRAN)
        x = xbuf[buf, grans].reshape(rows, x_cols)
        g = _dot(x, wg_bf[...])
        u = _dot(x, wu_bf[...])
        mid = (g * jax.nn.sigmoid(g) * u).astype(BF16)
        ybuf[buf, grans] = _dot(mid, wd_bf[...]).astype(BF16).reshape(rows // GRAN, GRAN, x_cols)

    nt = (ngran * GRAN + (FT - 1)) // FT
    big = FT_BIG // FT

    def big_tile(i, carry):
        ffn_rows(pl.multiple_of(i * FT_BIG, FT_BIG), FT_BIG)
        return carry

    lax.fori_loop(0, nt // big, big_tile, 0)
    size = big // 2
    while size >= 1:
        @pl.when((nt & size) != 0)
        def _(size=size):
            ffn_rows(pl.multiple_of((nt & ~(2 * size - 1)) * FT, size * FT), size * FT)
        size //= 2

    for_granules(step, start(writeback(buf)))

    @pl.when(step == nsteps - 1)
    def _():
        wait_all(step, writeback_span(buf))

        @pl.when(step >= 1)
        def _():
            wait_all(step - 1, writeback_span(1 - buf))


def _work_items(cnt, off):
    nb = cnt.shape[0]
    seg_n = ((cnt.reshape(nb, N_EXPERTS) + (GRAN - 1)) // GRAN).T.reshape(-1)
    seg_row = ((off.reshape(nb, N_EXPERTS) + jnp.arange(nb, dtype=jnp.int32)[:, None] * CAP) // GRAN).T.reshape(-1)
    seg_end = jnp.cumsum(seg_n)
    seg_start = seg_end - seg_n
    gmax = nb * (TB * TOP_K // GRAN + N_EXPERTS)
    prev_end = jnp.concatenate([jnp.ones((1,), jnp.int32), (seg_row + seg_n)[:-1]])
    steps = jnp.ones((gmax,), jnp.int32).at[seg_start].add(seg_row - prev_end, mode="drop")
    glist = jnp.cumsum(steps)
    per_e = seg_n.reshape(N_EXPERTS, nb).sum(axis=1)
    first_e = seg_start[::nb]
    passes = (per_e + (PASS_GRAN - 1)) // PASS_GRAN
    pass_end = jnp.cumsum(passes)
    n_items = N_EXPERTS + gmax // PASS_GRAN
    w = jnp.arange(n_items, dtype=jnp.int32)
    item_e = jnp.minimum(jnp.sum(pass_end[None, :] <= w[:, None], axis=1), N_EXPERTS - 1).astype(jnp.int32)
    done = (w - (pass_end - passes)[item_e]) * PASS_GRAN
    item_n = jnp.clip(per_e[item_e] - done, 0, PASS_GRAN)
    item_g0 = first_e[item_e] + done
    return item_e, item_g0.astype(jnp.int32), item_n.astype(jnp.int32), glist.astype(jnp.int32)


def _ffn(xs, cnt, off, wg, wu, wd):
    _, _, d = xs.shape
    item_e, item_g0, item_n, glist = _work_items(cnt, off)
    per_expert = lambda shape: pl.BlockSpec((1,) + shape, lambda w, ie, g0, n, gl: (ie[w], 0, 0))
    grid_spec = pltpu.PrefetchScalarGridSpec(
        num_scalar_prefetch=4,
        grid=(item_e.shape[0],),
        in_specs=[pl.BlockSpec(memory_space=pl.ANY), per_expert((d, D_EXPERT)), per_expert((d, D_EXPERT)),
                  per_expert((D_EXPERT, d))],
        out_specs=pl.BlockSpec(memory_space=pl.ANY),
        scratch_shapes=[pltpu.VMEM((2, PASS_GRAN, GRAN, d), BF16)] * 2 + [
                        pltpu.VMEM((d, D_EXPERT), BF16), pltpu.VMEM((d, D_EXPERT), BF16),
                        pltpu.VMEM((D_EXPERT, d), BF16),
                        pltpu.SemaphoreType.DMA((2,)), pltpu.SemaphoreType.DMA((2,))],
    )
    return pl.pallas_call(
        _ffn_kernel,
        grid_spec=grid_spec,
        out_shape=jax.ShapeDtypeStruct(xs.shape, xs.dtype),
        input_output_aliases={4: 0},
        compiler_params=pltpu.CompilerParams(
            dimension_semantics=("arbitrary",), vmem_limit_bytes=VMEM_LIMIT),
        name="ffn",
    )(item_e, item_g0, item_n, glist, xs, wg, wu, wd)


def _combine_kernel(used_sm, tail_sm, x1_ref, h2_ref, *rest):
    n_chunks = CAP // SLOT_CHUNK
    ys_refs = rest[:n_chunks]
    slots_ref, topw_ref, wgus_ref, wds_ref, o_ref = rest[n_chunks:]
    tb = x1_ref.shape[0]
    gu = _dot(h2_ref[...], wgus_ref[...])
    g, u = gu[:, :D_SHARED], gu[:, D_SHARED:]
    acc = x1_ref[...] + _dot((g * jax.nn.sigmoid(g) * u).astype(BF16), wds_ref[...])
    slots = slots_ref[...].astype(F32)
    topw = topw_ref[...]
    used = used_sm[pl.program_id(0)]

    def gather_chunk(c):
        scol = (lax.broadcasted_iota(jnp.int32, (tb, SLOT_CHUNK), 1) + c * SLOT_CHUNK).astype(F32)
        gate = jnp.zeros((tb, SLOT_CHUNK), F32)
        for k in range(TOP_K):
            gate = jnp.where(scol == slots[:, k:k + 1], topw[:, k:k + 1], gate)
        return _dot(gate.astype(BF16), ys_refs[c][0])

    for c in range(CHUNKS_TYPICAL):
        acc = acc + gather_chunk(c)
    o_ref[...] = acc
    for c in range(CHUNKS_TYPICAL, CAP // SLOT_CHUNK):
        @pl.when(c * SLOT_CHUNK < used)
        def _(c=c):
            o_ref[...] += gather_chunk(c)


def _combine(x1, h2, ys, slots, topw, used, wgus, wds):
    n, d = x1.shape
    n_chunks = CAP // SLOT_CHUNK
    const = lambda shape: pl.BlockSpec(shape, lambda i, u, t: (0,) * len(shape))
    row = lambda w: pl.BlockSpec((TB, w), lambda i, u, t: (i, 0))
    blocks = jnp.arange(n // TB, dtype=jnp.int32)
    tails = [lax.cummax(jnp.where(used > c * SLOT_CHUNK, blocks, 0)) for c in range(CHUNKS_TYPICAL, n_chunks)]
    tail_idx = jnp.stack(tails).astype(jnp.int32)
    n_blocks = n // TB

    def chunk_spec(c):
        if c < CHUNKS_TYPICAL:
            return pl.BlockSpec((1, SLOT_CHUNK, d), lambda i, u, t: (i * n_chunks + c, 0, 0))
        late = c - CHUNKS_TYPICAL
        return pl.BlockSpec((1, SLOT_CHUNK, d), lambda i, u, t: (t[late * n_blocks + i] * n_chunks + c, 0, 0))

    grid_spec = pltpu.PrefetchScalarGridSpec(
        num_scalar_prefetch=2,
        grid=(n_blocks,),
        in_specs=[row(d), row(d)] + [chunk_spec(c) for c in range(n_chunks)] + [
            row(TOP_K), row(TOP_K), const(wgus.shape), const(wds.shape)],
        out_specs=row(d),
    )
    ys_chunks = ys.reshape(n_blocks * n_chunks, SLOT_CHUNK, d)
    return pl.pallas_call(
        _combine_kernel,
        grid_spec=grid_spec,
        out_shape=jax.ShapeDtypeStruct((n, d), F32),
        compiler_params=pltpu.CompilerParams(
            dimension_semantics=("arbitrary",), vmem_limit_bytes=VMEM_LIMIT),
        name="combine",
    )(used, tail_idx.reshape(-1), x1, h2, *([ys_chunks] * n_chunks), slots, topw, wgus, wds)


def _moe(x1, h2, topi, topw, w_gate_e, w_up_e, w_down_e, w_gate_s, w_up_s, w_down_s):
    n, d = x1.shape
    wgus = jnp.concatenate([w_gate_s.astype(BF16), w_up_s.astype(BF16)], axis=-1)
    xs, slots, cnt, off = _dispatch(h2, topi)
    ys = _ffn(xs.reshape(-1, GRAN, d), cnt, off, w_gate_e, w_up_e, w_down_e)
    used = jnp.max(off + (cnt + (GRAN - 1)) // GRAN * GRAN, axis=(1, 2)).astype(jnp.int32)
    return _combine(x1, h2, ys.reshape(n // TB, CAP, d), slots, topw, used, wgus, w_down_s.astype(BF16))


def kernel(x, g_mix, w_in, q_norm_a, k_norm_a, q_norm_b, k_norm_b, rel_bias, b_forget, w_gate, b_gate,
           w_proj_a, w_proj_b, w_out, g_ffn, w_router, router_bias, w_gate_e, w_up_e, w_down_e,
           w_gate_s, w_up_s, w_down_s):
    batch, seq, d = x.shape
    xf = x.reshape(batch * seq, d)
    fox_shift, fox_top, fox_online = _fox_shift(q_norm_b, k_norm_b)
    pa0, pa1, pa2, qkb, vb, cb = _inproj(xf, g_mix, w_in, q_norm_a, k_norm_a, q_norm_b, k_norm_b, b_forget,
                                     fox_shift, seq)
    bias = jnp.stack([_toeplitz_bias(rel_bias, g, dil) for g, (_, dil) in enumerate(DIL_GROUPS)])
    ya = _dilated(pa0, pa1, pa2, bias, batch, seq)

    yb = _fox(qkb, vb, cb, fox_top, fox_online, batch, seq)
    x1, h2, topi, topw = _post(xf, ya, yb, g_mix, w_gate, b_gate, w_proj_a, w_proj_b, w_out, g_ffn,
                               w_router, router_bias)
    out = _moe(x1, h2, topi, topw, w_gate_e, w_up_e, w_down_e, w_gate_s, w_up_s, w_down_s)
    return out.reshape(batch, seq, d)
```

```python
import functools
import math

import jax
import jax.numpy as jnp
import numpy as np
from jax import lax
from jax.experimental import pallas as pl
from jax.experimental.pallas import tpu as pltpu

HEAD_DIM = 64
DIL_GROUPS = ((128, 1), (512, 4), (2048, 16))
HEADS_PER_GROUP = 4
N_HEADS_A = HEADS_PER_GROUP * len(DIL_GROUPS)
N_HEADS_B = 8
REL_BUCKETS = 32
REL_MAX_DIST = 2048
N_EXPERTS = 64
TOP_K = 8
D_EXPERT = 256
D_SHARED = 256
ROUTE_SCALE = 2.5
EPS = 1e-6

WIDTH_A = 3 * N_HEADS_A * HEAD_DIM
WIDTH_B = 3 * N_HEADS_B * HEAD_DIM
QK_B = N_HEADS_B * HEAD_DIM
OUT_A = HEADS_PER_GROUP * HEAD_DIM
OUT_B = N_HEADS_B * HEAD_DIM

LANES = 128
GROUP_W = HEADS_PER_GROUP * HEAD_DIM
WIN_J = 128
SUPER = DIL_GROUPS[-1][1] * WIN_J
NEG = -1e30
VMEM_LIMIT = 56 * 1024 * 1024

DIL_UNROLL = 8
TM_IN = 512
TM_POST = 1024
TQ_FOX = 1024
FOX_ROUNDING_SLACK = 1.02
FOX_EXP_HEADROOM = 60.0
FOX_DEAD_EXPONENT = -105.0
FOX_MAX_SHIFT = 80.0
TB = 256
GRAN = 16
CAP = TB * TOP_K + N_EXPERTS * GRAN
SLOT_CHUNK = 512
CHUNKS_TYPICAL = -(-(TB * TOP_K + N_EXPERTS * GRAN // 2) // SLOT_CHUNK)
SLOT_RADIX = 64
NO_SLOT = SLOT_RADIX ** 2 - 1
assert CAP <= NO_SLOT
PASS_GRAN = 256
FT = 256
FT_BIG = 1024

BF16 = jnp.bfloat16
F32 = jnp.float32


def _dot(a, b):
    return jnp.dot(a, b, preferred_element_type=F32)


def _dot_nt(a, b):
    return lax.dot_general(a, b, (((1,), (1,)), ((), ())), preferred_element_type=F32)


def _split3(v):
    hi = v.astype(BF16).astype(F32)
    r = v - hi
    mid = r.astype(BF16).astype(F32)
    lo = (r - mid).astype(BF16).astype(F32)
    return hi, mid, lo


def _inproj_kernel(x_ref, g_ref, wa_ref, wb_ref, wf_ref, bd_ref, tri_ref, gain_a_ref, gain_b_ref,
                   bf_ref, shift_ref, pa0_ref, pa1_ref, pa2_ref, qkb_ref, vb_ref, cb_ref, carry_ref, h_ref, *,
                   tiles_per_seq):
    tm = x_ref.shape[0]
    x = x_ref[...]
    h = x * lax.rsqrt(jnp.mean(x * x, axis=-1, keepdims=True) + EPS) * g_ref[...]
    n_lane_chunks = h_ref.shape[0]
    for c in range(n_lane_chunks):
        h_ref[c] = h[:, c * LANES:(c + 1) * LANES]
    h = h.astype(BF16)
    bd = bd_ref[...]

    def headnorm(p, gain):
        ms = _dot((p * p).astype(BF16), bd)
        return p * lax.rsqrt(ms + EPS) * gain

    for g, (pa_ref, (_, dil)) in enumerate(zip((pa0_ref, pa1_ref, pa2_ref), DIL_GROUPS)):
        rows = tm // dil
        if dil == 1:
            hg = h
        else:
            hg = jnp.concatenate([jnp.concatenate(
                [h_ref[c, pl.ds(r, rows, stride=dil), :] for c in range(n_lane_chunks)], axis=1)
                for r in range(dil)], axis=0).astype(BF16)
        qkv = _dot(hg, wa_ref[g])
        for part in range(3):
            cols = slice(part * GROUP_W, (part + 1) * GROUP_W)
            p = qkv[:, cols]
            if part < 2:
                p = headnorm(p, gain_a_ref[part:part + 1, :])
            p = p.astype(BF16)
            if dil == 1:
                pa_ref[:, cols] = p
            else:
                for r in range(dil):
                    pa_ref[0, r, :, cols] = p[r * rows:(r + 1) * rows, :]

    f = _dot(h, wf_ref[...]) + bf_ref[...]
    logf = jnp.minimum(f, 0.0) - jnp.log1p(jnp.exp(-jnp.abs(f)))
    tri = tri_ref[...]
    lh, lm, ll = _split3(logf)
    cum3 = _dot(tri, jnp.concatenate([lh, lm, ll], axis=1).astype(BF16))
    cum = cum3[:, :LANES] + cum3[:, LANES:2 * LANES] + cum3[:, 2 * LANES:]

    @pl.when(pl.program_id(0) % tiles_per_seq == 0)
    def _():
        carry_ref[...] = jnp.zeros_like(carry_ref)

    cum = cum + carry_ref[0:1, :]
    carry_ref[0:1, :] = cum[tm - 1:tm, :]
    cb_ref[0] = jnp.concatenate([cum[0:1, :], cum[tm - 1:tm, :], jnp.zeros((6, LANES), F32)], axis=0)
    ch, cm, cl = _split3(cum)

    j = lax.broadcasted_iota(jnp.int32, (tm, HEAD_DIM), 1)

    def ext_cols(vals):
        out = jnp.zeros((tm, HEAD_DIM), F32)
        ones = [pos for pos, val in enumerate(vals) if isinstance(val, float)]
        if ones:
            is_one = functools.reduce(jnp.logical_or, [j == pos for pos in ones])
            out = jnp.where(is_one, 1.0, out)
        for pos, val in enumerate(vals):
            if not isinstance(val, float):
                out = jnp.where(j == pos, val, out)
        return out

    for c in range(QK_B // GROUP_W):
        qkv = _dot(h, wb_ref[c])
        pq = headnorm(qkv[:, 0:GROUP_W], gain_b_ref[0:1, :])
        pk = headnorm(qkv[:, GROUP_W:2 * GROUP_W], gain_b_ref[1:2, :])
        pv = qkv[:, 2 * GROUP_W:3 * GROUP_W]
        r = _dot((pq * pk).astype(BF16), bd) * HEAD_DIM + shift_ref[...]
        for hh in range(HEADS_PER_GROUP):
            head = c * HEADS_PER_GROUP + hh
            lanes = slice(hh * HEAD_DIM, (hh + 1) * HEAD_DIM)
            col = lambda a, idx: a[:, idx:idx + 1]
            cs = [col(ch, head), col(cm, head), col(cl, head)]
            ext_q = ext_cols(cs + [1.0] * 3 + [-col(r, hh * HEAD_DIM)])
            ext_k = ext_cols([1.0] * 3 + [-v for v in cs] + [1.0])
            ext_v = ext_cols([1.0])
            for part, (val, ext) in enumerate(((pq, ext_q), (pk, ext_k))):
                o0 = (part * N_HEADS_B + head) * LANES
                qkb_ref[:, o0:o0 + LANES] = jnp.concatenate([val[:, lanes], ext], axis=-1).astype(BF16)
            vb_ref[:, head * LANES:(head + 1) * LANES] = jnp.concatenate(
                [pv[:, lanes], ext_v], axis=-1).astype(BF16)


def _inproj(xf, g_mix, w_in, q_norm_a, k_norm_a, q_norm_b, k_norm_b, b_forget, fox_shift, seq):
    n, d = xf.shape
    tm = TM_IN
    scale = HEAD_DIM ** -0.5
    w_bf = w_in.astype(BF16)
    qkv_w = N_HEADS_A * HEAD_DIM
    wa = jnp.stack([jnp.concatenate(
        [w_bf[:, part * qkv_w + g * GROUP_W: part * qkv_w + (g + 1) * GROUP_W] for part in range(3)],
        axis=1) for g in range(len(DIL_GROUPS))])
    wb = jnp.stack([jnp.concatenate(
        [w_bf[:, WIDTH_A + part * QK_B + c * GROUP_W: WIDTH_A + part * QK_B + (c + 1) * GROUP_W]
         for part in range(3)], axis=1) for c in range(QK_B // GROUP_W)])
    wf = jnp.pad(w_bf[:, WIDTH_A + WIDTH_B:], ((0, 0), (0, LANES - N_HEADS_B)))
    bfp = jnp.pad(b_forget.astype(F32), (0, LANES - N_HEADS_B)).reshape(1, LANES)
    seg = np.arange(GROUP_W) // HEAD_DIM
    bd = jnp.asarray((seg[:, None] == seg[None, :]).astype(np.float32) / HEAD_DIM, BF16)
    tri = jnp.asarray(np.tril(np.ones((tm, tm), np.float32)), BF16)
    gain_a = jnp.stack([jnp.tile(q_norm_a, HEADS_PER_GROUP) * scale, jnp.tile(k_norm_a, HEADS_PER_GROUP)])
    gain_b = jnp.stack([jnp.tile(q_norm_b, HEADS_PER_GROUP) * scale, jnp.tile(k_norm_b, HEADS_PER_GROUP)])
    const = lambda shape: pl.BlockSpec(shape, lambda i: (0,) * len(shape))
    tps = seq // tm
    batch = n // seq
    qkv3 = 3 * GROUP_W
    (_, d1), (_, d2) = DIL_GROUPS[1], DIL_GROUPS[2]
    return pl.pallas_call(
        functools.partial(_inproj_kernel, tiles_per_seq=tps),
        grid=(n // tm,),
        in_specs=[
            pl.BlockSpec((tm, d), lambda i: (i, 0)),
            const((1, d)), const(wa.shape), const(wb.shape), const(wf.shape),
            const(bd.shape), const(tri.shape), const(gain_a.shape), const(gain_b.shape),
            const(bfp.shape), const((1, 1)),
        ],
        out_specs=[
            pl.BlockSpec((tm, qkv3), lambda i: (i, 0)),
            pl.BlockSpec((1, d1, tm // d1, qkv3), lambda i: (i // tps, 0, i % tps, 0)),
            pl.BlockSpec((1, d2, tm // d2, qkv3), lambda i: (i // tps, 0, i % tps, 0)),
            pl.BlockSpec((tm, 2 * N_HEADS_B * LANES), lambda i: (i, 0)),
            pl.BlockSpec((tm, N_HEADS_B * LANES), lambda i: (i, 0)),
            pl.BlockSpec((1, 8, LANES), lambda i: (i, 0, 0)),
        ],
        out_shape=[
            jax.ShapeDtypeStruct((n, qkv3), BF16),
            jax.ShapeDtypeStruct((batch, d1, seq // d1, qkv3), BF16),
            jax.ShapeDtypeStruct((batch, d2, seq // d2, qkv3), BF16),
            jax.ShapeDtypeStruct((n, 2 * N_HEADS_B * LANES), BF16),
            jax.ShapeDtypeStruct((n, N_HEADS_B * LANES), BF16),
            jax.ShapeDtypeStruct((n // tm, 8, LANES), F32),
        ],
        scratch_shapes=[pltpu.VMEM((8, LANES), F32), pltpu.VMEM((d // LANES, tm, LANES), F32)],
        compiler_params=pltpu.CompilerParams(
            dimension_semantics=("arbitrary",), vmem_limit_bytes=VMEM_LIMIT),
        name="inproj",
    )(xf, g_mix.reshape(1, d), wa, wb, wf, bd, tri, gain_a, gain_b, bfp, fox_shift.reshape(1, 1))


def _dilated_kernel(p0_ref, h0_ref, p1_ref, h1_ref, p2_ref, h2_ref, bias_ref, o_ref, acc_ref, lse_ref):
    tq = WIN_J
    first_sb = pl.program_id(1) == 0
    lane_head = lax.broadcasted_iota(jnp.int32, (tq, GROUP_W), 1) // HEAD_DIM
    prev_col = lax.broadcasted_iota(jnp.int32, (tq, 2 * tq), 1) < tq
    qc, kc_, vc_ = (slice(0, GROUP_W), slice(GROUP_W, 2 * GROUP_W), slice(2 * GROUP_W, 3 * GROUP_W))

    def attend(g, q, kp, kc, vp, vc, no_prev):
        kcat = jnp.concatenate([kp, kc], axis=0)
        vcat = jnp.concatenate([vp, vc], axis=0)
        dead = jnp.logical_and(no_prev, prev_col)
        q4 = jnp.concatenate([jnp.where(lane_head == hh, q, jnp.zeros_like(q))
                              for hh in range(HEADS_PER_GROUP)], axis=0)
        s = _dot_nt(q4, kcat) + bias_ref[g].reshape(HEADS_PER_GROUP * tq, 2 * tq)
        s = jnp.where(jnp.concatenate([dead] * HEADS_PER_GROUP, axis=0), NEG, s)
        m = jnp.max(s, axis=-1, keepdims=True)
        p = jnp.exp(s - m)
        l = jnp.sum(p, axis=-1, keepdims=True)
        o4 = _dot(p.astype(BF16), vcat) * (1.0 / l)
        lse4 = m + jnp.log(l)
        acc = o4[0:tq]
        lse = jnp.broadcast_to(lse4[0:tq], (tq, GROUP_W))
        for hh in range(1, HEADS_PER_GROUP):
            sel = lane_head == hh
            acc = jnp.where(sel, o4[hh * tq:(hh + 1) * tq], acc)
            lse = jnp.where(sel, lse4[hh * tq:(hh + 1) * tq], lse)
        return acc, lse

    n_half = GROUP_W // LANES

    def merge(rows, acc, lse):
        for c in range(n_half):
            lanes = slice(c * LANES, (c + 1) * LANES)
            l1 = lse_ref[c, rows, :]
            mx = jnp.maximum(l1, lse[:, lanes])
            w1 = jnp.exp(l1 - mx)
            w2 = jnp.exp(lse[:, lanes] - mx)
            den = w1 + w2
            acc_ref[c, rows, :] = (w1 * acc_ref[c, rows, :] + w2 * acc[:, lanes]) / den
            lse_ref[c, rows, :] = mx + jnp.log(den)

    def pick(first, halo, body):
        return jnp.where(first, halo, body)

    def loop(n, body):
        def trip(i, carry):
            for u in range(DIL_UNROLL):
                body(i * DIL_UNROLL + u, carry)
            return carry
        lax.fori_loop(0, n // DIL_UNROLL, trip, 0)

    def body0(j, carry):
        st = pl.multiple_of(j * tq, tq)
        pst = pl.multiple_of(jnp.maximum(j - 1, 0) * tq, tq)
        cur, prv = pl.ds(st, tq), pl.ds(pst, tq)
        acc, lse = attend(
            0, p0_ref[0, cur, qc],
            pick(j == 0, h0_ref[0, :, kc_], p0_ref[0, prv, kc_]), p0_ref[0, cur, kc_],
            pick(j == 0, h0_ref[0, :, vc_], p0_ref[0, prv, vc_]), p0_ref[0, cur, vc_],
            jnp.logical_and(j == 0, first_sb))
        for c in range(n_half):
            acc_ref[c, cur, :] = acc[:, c * LANES:(c + 1) * LANES]
            lse_ref[c, cur, :] = lse[:, c * LANES:(c + 1) * LANES]
        return carry

    loop(SUPER // tq, body0)

    d1 = DIL_GROUPS[1][1]
    nsub1 = SUPER // d1 // tq
    def body1(t, carry):
        r, ii = t // nsub1, t % nsub1
        st = pl.multiple_of(ii * tq, tq)
        pst = pl.multiple_of(jnp.maximum(ii - 1, 0) * tq, tq)
        cur, prv = pl.ds(st, tq), pl.ds(pst, tq)
        acc, lse = attend(
            1, p1_ref[0, r, cur, qc],
            pick(ii == 0, h1_ref[0, r, :, kc_], p1_ref[0, r, prv, kc_]), p1_ref[0, r, cur, kc_],
            pick(ii == 0, h1_ref[0, r, :, vc_], p1_ref[0, r, prv, vc_]), p1_ref[0, r, cur, vc_],
            jnp.logical_and(ii == 0, first_sb))
        merge(pl.ds(ii * (tq * d1) + r, tq, stride=d1), acc, lse)
        return carry

    loop(d1 * nsub1, body1)

    d2 = DIL_GROUPS[2][1]

    def body2(r, carry):
        acc, lse = attend(2, p2_ref[0, r, :, qc], h2_ref[0, r, :, kc_], p2_ref[0, r, :, kc_],
                          h2_ref[0, r, :, vc_], p2_ref[0, r, :, vc_], first_sb)
        merge(pl.ds(r, tq, stride=d2), acc, lse)
        return carry

    loop(d2, body2)

    for c in range(n_half):
        o_ref[0, :, c * LANES:(c + 1) * LANES] = acc_ref[c].astype(o_ref.dtype)


def _rel_bucket(dist):
    max_exact = REL_BUCKETS // 2
    n = jnp.maximum(dist.astype(F32), 1.0)
    large = max_exact + (jnp.log(n / max_exact) / math.log(REL_MAX_DIST / max_exact)
                         * (REL_BUCKETS - max_exact)).astype(jnp.int32)
    large = jnp.minimum(large, REL_BUCKETS - 1)
    return jnp.where(dist < max_exact, dist, large)


def _toeplitz_bias(rel_bias, g, dil):
    tq = WIN_J
    offs = dil * (WIN_J - jnp.arange(WIN_J + 1, dtype=jnp.int32))
    hs = slice(g * HEADS_PER_GROUP, (g + 1) * HEADS_PER_GROUP)
    tab_rev = rel_bias[_rel_bucket(offs)][:, hs].T.astype(F32)
    period = 3 * tq
    neg = lambda w: jnp.full((HEADS_PER_GROUP, w), NEG, F32)
    vec = jnp.concatenate([neg(tq - 1), tab_rev, neg(period - 2 * tq)], axis=1)
    flat = jnp.broadcast_to(vec[:, None, :], (HEADS_PER_GROUP, tq, period)).reshape(HEADS_PER_GROUP, -1)
    skew = flat[:, :tq * (period - 1)].reshape(HEADS_PER_GROUP, tq, period - 1)
    return skew[:, :, tq - 1:3 * tq - 1]


def _dilated(pa0, pa1, pa2, bias, batch, seq):
    tq = WIN_J
    qkv3 = 3 * GROUP_W
    (_, d1), (_, d2) = DIL_GROUPS[1], DIL_GROUPS[2]
    nsb = seq // SUPER
    p0 = pa0.reshape(batch, seq, qkv3)
    prev_blk = lambda per_sb: (lambda b, s: jnp.maximum(s * per_sb - 1, 0))
    h0i, h1i, h2i = prev_blk(SUPER // tq), prev_blk(SUPER // d1 // tq), prev_blk(SUPER // d2 // tq)
    out = pl.pallas_call(
        _dilated_kernel,
        grid=(batch, nsb),
        in_specs=[
            pl.BlockSpec((1, SUPER, qkv3), lambda b, s: (b, s, 0)),
            pl.BlockSpec((1, tq, qkv3), lambda b, s: (b, h0i(b, s), 0)),
            pl.BlockSpec((1, d1, SUPER // d1, qkv3), lambda b, s: (b, 0, s, 0)),
            pl.BlockSpec((1, d1, tq, qkv3), lambda b, s: (b, 0, h1i(b, s), 0)),
            pl.BlockSpec((1, d2, SUPER // d2, qkv3), lambda b, s: (b, 0, s, 0)),
            pl.BlockSpec((1, d2, tq, qkv3), lambda b, s: (b, 0, h2i(b, s), 0)),
            pl.BlockSpec(bias.shape, lambda b, s: (0, 0, 0, 0)),
        ],
        out_specs=pl.BlockSpec((1, SUPER, GROUP_W), lambda b, s: (b, s, 0)),
        out_shape=jax.ShapeDtypeStruct((batch, seq, GROUP_W), BF16),
        scratch_shapes=[pltpu.VMEM((GROUP_W // LANES, SUPER, LANES), F32)] * 2,
        compiler_params=pltpu.CompilerParams(
            dimension_semantics=("arbitrary", "arbitrary"), vmem_limit_bytes=VMEM_LIMIT),
        name="dilated",
    )(p0, p0, pa1, pa1, pa2, pa2, bias)
    return out.reshape(batch * seq, GROUP_W)


def _fox_kernel(nlive_sm, q_ref, k_ref, v_ref, o_ref, m_ref, acc_ref, *, online):
    tq = q_ref.shape[1]
    half = tq // 2
    qi = pl.program_id(2)
    step = (pl.program_id(0) * pl.num_programs(1) + pl.program_id(1)) * pl.num_programs(2) + qi
    row = lax.broadcasted_iota(jnp.int32, (half, half), 0)
    col = lax.broadcasted_iota(jnp.int32, (half, half), 1)
    causal = row >= col

    def attend(hh, rows, state, start, nkeys, masked):
        m, acc = state
        lanes = slice(hh * LANES, (hh + 1) * LANES)
        s = _dot_nt(q_ref[0, rows, lanes], k_ref[0, pl.ds(start, nkeys), lanes])
        if masked:
            s = jnp.where(causal, s, NEG)
        if online:
            m_new = jnp.maximum(m, jnp.max(s, axis=-1, keepdims=True))
            acc = acc * jnp.exp(m - m_new)
            s = s - m_new
            m = m_new
        return m, acc + _dot(jnp.exp(s).astype(BF16), v_ref[0, pl.ds(start, nkeys), lanes])

    first = [qi - nlive_sm[2 * step + hh] for hh in range(2)]
    for hh in range(2):
        m_ref[hh] = jnp.full((tq, 1), NEG, F32)
        acc_ref[hh] = jnp.zeros((tq, LANES), F32)

    def full_chunk(ki, carry):
        start = pl.multiple_of(ki * tq, tq)
        for hh in range(2):
            @pl.when(ki >= first[hh])
            def _(hh=hh):
                m, acc = attend(hh, slice(None), (m_ref[hh], acc_ref[hh]), start, tq, False)
                acc_ref[hh] = acc
                if online:
                    m_ref[hh] = m
        return carry

    lax.fori_loop(jnp.minimum(first[0], first[1]), qi, full_chunk, 0)

    d0 = pl.multiple_of(qi * tq, tq)
    outs = []
    for hh in range(2):
        m, acc = m_ref[hh], acc_ref[hh]
        top, bot = slice(0, half), slice(half, tq)
        s_top = attend(hh, top, (m[top], acc[top]), d0, half, True)
        s_bot = attend(hh, bot, (m[bot], acc[bot]), d0, half, False)
        s_bot = attend(hh, bot, s_bot, d0 + half, half, True)
        a = jnp.concatenate([s_top[1], s_bot[1]], axis=0)
        outs.append(a[:, :HEAD_DIM] / a[:, HEAD_DIM:HEAD_DIM + 1])
    o_ref[0] = jnp.concatenate(outs, axis=-1).astype(o_ref.dtype)


def _fox_live_chunks(cb, top, batch, seq):
    tps = seq // TM_IN
    per = TQ_FOX // TM_IN
    nq = seq // TQ_FOX
    c_first = cb[:, 0, :N_HEADS_B].reshape(batch, tps, N_HEADS_B)[:, ::per]
    c_last = cb[:, 1, :N_HEADS_B].reshape(batch, tps, N_HEADS_B)[:, per - 1::per]
    live = (top + c_first[:, :, None, :] - c_last[:, None, :, :]) >= FOX_DEAD_EXPONENT
    back = jnp.arange(nq)[:, None] - jnp.arange(nq)[None, :]
    reach = jnp.max(jnp.where(jnp.logical_and(live, (back > 0)[None, :, :, None]),
                              back[None, :, :, None], 0), axis=2)
    reach = reach.reshape(batch, nq, N_HEADS_B // 2, 2).transpose(0, 2, 1, 3)
    return reach.reshape(-1).astype(jnp.int32)


def _fox(qkb, vb, cb, top, online, batch, seq):
    tq = TQ_FOX
    pairs = N_HEADS_B // 2
    nq = seq // tq
    qkv = qkb.reshape(batch, seq, 2 * N_HEADS_B * LANES)
    vv = vb.reshape(batch, seq, N_HEADS_B * LANES)
    all_chunks = jnp.tile(jnp.repeat(jnp.arange(nq, dtype=jnp.int32), 2), batch * pairs)

    def call(is_online, nlive):
        grid_spec = pltpu.PrefetchScalarGridSpec(
            num_scalar_prefetch=1,
            grid=(batch, pairs, nq),
            in_specs=[
                pl.BlockSpec((1, tq, 2 * LANES), lambda b, p, i, n: (b, i, p)),
                pl.BlockSpec((1, seq, 2 * LANES), lambda b, p, i, n: (b, 0, pairs + p)),
                pl.BlockSpec((1, seq, 2 * LANES), lambda b, p, i, n: (b, 0, p)),
            ],
            out_specs=pl.BlockSpec((1, tq, LANES), lambda b, p, i, n: (b, i, p)),
            scratch_shapes=[pltpu.VMEM((2, tq, 1), F32), pltpu.VMEM((2, tq, LANES), F32)],
        )
        return pl.pallas_call(
            functools.partial(_fox_kernel, online=is_online),
            grid_spec=grid_spec,
            out_shape=jax.ShapeDtypeStruct((batch, seq, OUT_B), BF16),
            compiler_params=pltpu.CompilerParams(
                dimension_semantics=("arbitrary", "arbitrary", "arbitrary"), vmem_limit_bytes=VMEM_LIMIT),
            name="fox_online" if is_online else "fox",
        )(nlive, qkv, qkv, vv)

    out = lax.cond(online, lambda: call(True, all_chunks),
                   lambda: call(False, _fox_live_chunks(cb, top, batch, seq)))
    return out.reshape(batch * seq, OUT_B)


def _fox_shift(q_norm_b, k_norm_b):
    bound = HEAD_DIM * (HEAD_DIM ** -0.5) * jnp.max(jnp.abs(q_norm_b)) * jnp.max(jnp.abs(k_norm_b))
    shift = jnp.maximum(2.0 * FOX_ROUNDING_SLACK * bound - FOX_EXP_HEADROOM, 0.0).astype(F32)
    top = 2.0 * FOX_ROUNDING_SLACK * bound - shift
    return shift, top.astype(F32), shift > FOX_MAX_SHIFT


def _post_kernel(x_ref, ya_ref, yb_ref, gmix_ref, wg_ref, bg_ref, wpa_ref, wpb_ref, wo_ref,
                 gffn_ref, wr_ref, rb_ref, x1_ref, h2_ref, topi_ref, topw_ref):
    d = x_ref.shape[1]
    x = x_ref[...]
    h = (x * lax.rsqrt(jnp.mean(x * x, axis=-1, keepdims=True) + EPS) * gmix_ref[...]).astype(BF16)
    gates = jax.nn.sigmoid(_dot(h, wg_ref[...]) + bg_ref[...])
    merged = gates[:, :d] * _dot(ya_ref[...], wpa_ref[...]) + gates[:, d:] * _dot(yb_ref[...], wpb_ref[...])
    x1 = x + _dot(merged.astype(BF16), wo_ref[...])
    x1_ref[...] = x1
    h2 = x1 * lax.rsqrt(jnp.mean(x1 * x1, axis=-1, keepdims=True) + EPS) * gffn_ref[...]
    h2_ref[...] = h2.astype(BF16)

    hh, hm, _ = _split3(h2)
    wr = wr_ref[...]
    wh = wr.astype(BF16)
    wl = (wr - wh.astype(F32)).astype(BF16)
    hh, hm = hh.astype(BF16), hm.astype(BF16)
    logits = _dot_nt(wh, hh) + _dot_nt(wh, hm) + _dot_nt(wl, hh)
    scores = jax.nn.sigmoid(logits)
    biased = scores + rb_ref[...]
    eid = lax.broadcasted_iota(jnp.int32, scores.shape, 0).astype(F32)
    chosen = jnp.zeros(scores.shape, jnp.bool_)
    idx, val = [], []
    for _ in range(TOP_K):
        cur = jnp.where(chosen, -jnp.inf, biased)
        mx = jnp.max(cur, axis=0, keepdims=True)
        first = jnp.min(jnp.where(cur == mx, eid, float(N_EXPERTS)), axis=0, keepdims=True)
        pick = eid == first
        chosen = jnp.logical_or(chosen, pick)
        idx.append(first)
        val.append(jnp.sum(jnp.where(pick, scores, 0.0), axis=0, keepdims=True))
    top_s = jnp.concatenate(val, axis=0)
    top_w = top_s / jnp.sum(top_s, axis=0, keepdims=True) * ROUTE_SCALE
    tm = scores.shape[1]
    both = jnp.concatenate(idx + [top_w, jnp.zeros((LANES - 2 * TOP_K, tm), F32)], axis=0).T
    topi_ref[...] = both[:, :TOP_K].astype(jnp.int32)
    topw_ref[...] = both[:, TOP_K:2 * TOP_K]


def _post(xf, ya, yb, g_mix, w_gate, b_gate, w_proj_a, w_proj_b, w_out, g_ffn, w_router, router_bias):
    n, d = xf.shape
    tm = TM_POST
    const = lambda shape: pl.BlockSpec(shape, lambda i: (0,) * len(shape))
    row = lambda w: pl.BlockSpec((tm, w), lambda i: (i, 0))
    args = [xf, ya, yb, g_mix.reshape(1, d), w_gate.astype(BF16), b_gate.reshape(1, 2 * d),
            w_proj_a.astype(BF16), w_proj_b.astype(BF16), w_out.astype(BF16), g_ffn.reshape(1, d),
            w_router.astype(F32).T, router_bias.astype(F32).reshape(N_EXPERTS, 1)]
    in_specs = [row(d), row(OUT_A), row(OUT_B)] + [const(a.shape) for a in args[3:]]
    return pl.pallas_call(
        _post_kernel,
        grid=(n // tm,),
        in_specs=in_specs,
        out_specs=[row(d), row(d), row(TOP_K), row(TOP_K)],
        out_shape=[jax.ShapeDtypeStruct((n, d), F32), jax.ShapeDtypeStruct((n, d), BF16),
                   jax.ShapeDtypeStruct((n, TOP_K), jnp.int32), jax.ShapeDtypeStruct((n, TOP_K), F32)],
        compiler_params=pltpu.CompilerParams(
            dimension_semantics=("arbitrary",), vmem_limit_bytes=VMEM_LIMIT),
        name="post",
    )(*args)


def _dispatch_kernel(h2_ref, topi_ref, tri_ref, upper_ref, xs_ref, slots_ref, cnt_ref, off_ref):
    tb = h2_ref.shape[0]
    topi = topi_ref[...]
    lane = lax.broadcasted_iota(jnp.int32, (tb, N_EXPERTS), 1)
    picks = [lane == topi[:, k:k + 1] for k in range(TOP_K)]
    mask = picks[0]
    for pk in picks[1:]:
        mask = jnp.logical_or(mask, pk)
    maskf = jnp.where(mask, 1.0, 0.0)
    rank = _dot(tri_ref[...], maskf.astype(BF16))
    cnt = jnp.sum(maskf, axis=0, keepdims=True)
    gran = jnp.floor((cnt + (GRAN - 1)) * (1.0 / GRAN))
    goff = _dot(jnp.broadcast_to(gran, (8, N_EXPERTS)).astype(BF16), upper_ref[...])[0:1]
    off = goff * GRAN
    slot_te = off + rank
    slots = jnp.concatenate(
        [jnp.sum(jnp.where(pk, slot_te, 0.0), axis=-1, keepdims=True) for pk in picks], axis=1)
    slots_ref[...] = slots.astype(jnp.int32)
    cnt_ref[0] = cnt.astype(jnp.int32)
    off_ref[0] = off.astype(jnp.int32)
    v = jnp.where(mask, slot_te, float(NO_SLOT))
    v_hi = jnp.floor(v * (1.0 / SLOT_RADIX))
    w = jnp.concatenate([v_hi * SLOT_RADIX, v - v_hi * SLOT_RADIX], axis=1).T.astype(BF16)
    end = off + gran * GRAN
    used = jnp.max(end).astype(jnp.int32)
    h2 = h2_ref[...]

    def lookup(first, rows):
        s_e = (lax.broadcasted_iota(jnp.int32, (rows, N_EXPERTS), 0) + first).astype(F32)
        own = jnp.where(jnp.logical_and(s_e >= off, s_e < end), 1.0, 0.0)
        return _dot(jnp.concatenate([own, own], axis=1).astype(BF16), w)

    def sort_chunk(c, looked):
        s_t = (lax.broadcasted_iota(jnp.int32, (SLOT_CHUNK, tb), 0) + c * SLOT_CHUNK).astype(F32)
        onehot = jnp.where(looked == s_t, 1.0, 0.0).astype(BF16)
        xs_ref[0, c * SLOT_CHUNK:(c + 1) * SLOT_CHUNK, :] = _dot(onehot, h2).astype(BF16)

    looked_typ = lookup(0, CHUNKS_TYPICAL * SLOT_CHUNK)
    for c in range(CAP // SLOT_CHUNK):
        if c < CHUNKS_TYPICAL:
            sort_chunk(c, looked_typ[c * SLOT_CHUNK:(c + 1) * SLOT_CHUNK, :])
        else:
            @pl.when(c * SLOT_CHUNK < used)
            def _(c=c):
                sort_chunk(c, lookup(c * SLOT_CHUNK, SLOT_CHUNK))

            @pl.when(c * SLOT_CHUNK >= used)
            def _(c=c):
                xs_ref[0, c * SLOT_CHUNK:(c + 1) * SLOT_CHUNK, :] = jnp.zeros((SLOT_CHUNK, xs_ref.shape[2]), BF16)


def _dispatch(h2, topi):
    n, d = h2.shape
    nb = n // TB
    tri = jnp.asarray(np.tril(np.ones((TB, TB), np.float32), -1), BF16)
    upper = jnp.asarray(np.triu(np.ones((N_EXPERTS, N_EXPERTS), np.float32), 1), BF16)
    const = lambda shape: pl.BlockSpec(shape, lambda i: (0,) * len(shape))
    meta = pl.BlockSpec((1, 1, N_EXPERTS), lambda i: (i, 0, 0))
    return pl.pallas_call(
        _dispatch_kernel,
        grid=(nb,),
        in_specs=[pl.BlockSpec((TB, d), lambda i: (i, 0)), pl.BlockSpec((TB, TOP_K), lambda i: (i, 0)),
                  const(tri.shape), const(upper.shape)],
        out_specs=[pl.BlockSpec((1, CAP, d), lambda i: (i, 0, 0)),
                   pl.BlockSpec((TB, TOP_K), lambda i: (i, 0)), meta, meta],
        out_shape=[jax.ShapeDtypeStruct((nb, CAP, d), BF16), jax.ShapeDtypeStruct((n, TOP_K), jnp.int32),
                   jax.ShapeDtypeStruct((nb, 1, N_EXPERTS), jnp.int32),
                   jax.ShapeDtypeStruct((nb, 1, N_EXPERTS), jnp.int32)],
        compiler_params=pltpu.CompilerParams(
            dimension_semantics=("arbitrary",), vmem_limit_bytes=VMEM_LIMIT),
        name="dispatch",
    )(h2, topi, tri, upper)


def _ffn_kernel(item_e_sm, item_d0_sm, item_nd_sm, item_s0_sm, item_ns_sm, dlist_sm, slist_sm,
                xs_hbm, wg_ref, wu_ref, wd_ref, ys_hbm, xbuf, ybuf, wg_bf, wu_bf, wd_bf, sem_in, sem_out):
    step = pl.program_id(0)
    nsteps = pl.num_programs(0)
    buf = step % 2

    def granules(st):
        return 2 * item_nd_sm[st] + item_ns_sm[st]

    def for_copies(st, fn):
        d0, nd = item_d0_sm[st], item_nd_sm[st]
        s0, ns = item_s0_sm[st], item_ns_sm[st]

        def pair(j, carry):
            fn(dlist_sm[d0 + j], 2 * j, 2)
            return carry

        def single(j, carry):
            fn(slist_sm[s0 + j], 2 * nd + j, 1)
            return carry

        lax.fori_loop(0, nd, pair, 0)
        lax.fori_loop(0, ns, single, 0)

    def fetch(b_):
        return lambda src, dst, k: pltpu.make_async_copy(
            xs_hbm.at[pl.ds(src, k)], xbuf.at[b_, pl.ds(dst, k)], sem_in.at[b_])

    def writeback(b_):
        return lambda src, dst, k: pltpu.make_async_copy(
            ybuf.at[b_, pl.ds(dst, k)], ys_hbm.at[pl.ds(src, k)], sem_out.at[b_])

    def start(mk):
        return lambda src, dst, k: mk(src, dst, k).start()

    def wait_all(st, span):
        n = granules(st)
        size = PASS_GRAN
        while size >= 1:
            @pl.when((n & size) != 0)
            def _(size=size):
                span(size).wait()
            size //= 2
        return n

    def fetch_span(b_):
        return lambda k: pltpu.make_async_copy(
            xs_hbm.at[pl.ds(0, k)], xbuf.at[b_, pl.ds(0, k)], sem_in.at[b_])

    def writeback_span(b_):
        return lambda k: pltpu.make_async_copy(
            ybuf.at[b_, pl.ds(0, k)], ys_hbm.at[pl.ds(0, k)], sem_out.at[b_])

    @pl.when(step == 0)
    def _():
        xbuf[...] = jnp.zeros_like(xbuf)
        for_copies(step, start(fetch(0)))

    @pl.when(step + 1 < nsteps)
    def _():
        for_copies(step + 1, start(fetch(1 - buf)))

    ngran = wait_all(step, fetch_span(buf))

    @pl.when(step >= 2)
    def _():
        wait_all(step - 2, writeback_span(buf))

    @pl.when(ngran > 0)
    def _():
        wg_bf[...] = wg_ref[0].astype(BF16)
        wu_bf[...] = wu_ref[0].astype(BF16)
        wd_bf[...] = wd_ref[0].astype(BF16)

    x_cols = xbuf.shape[-1]

    def ffn_rows(base, rows):
        grans = pl.ds(pl.multiple_of(base // GRAN, rows // GRAN), rows // GRAN)
        x = xbuf[buf, grans].reshape(rows, x_cols)
        g = _dot(x, wg_bf[...])
        u = _dot(x, wu_bf[...])
        mid = (g * jax.nn.sigmoid(g) * u).astype(BF16)
        ybuf[buf, grans] = _dot(mid, wd_bf[...]).astype(BF16).reshape(rows // GRAN, GRAN, x_cols)

    nt = (ngran * GRAN + (FT - 1)) // FT
    big = FT_BIG // FT

    def big_tile(i, carry):
        ffn_rows(pl.multiple_of(i * FT_BIG, FT_BIG), FT_BIG)
        return carry

    lax.fori_loop(0, nt // big, big_tile, 0)
    size = big // 2
    while size >= 1:
        @pl.when((nt & size) != 0)
        def _(size=size):
            ffn_rows(pl.multiple_of((nt & ~(2 * size - 1)) * FT, size * FT), size * FT)
        size //= 2

    for_copies(step, start(writeback(buf)))

    @pl.when(step == nsteps - 1)
    def _():
        wait_all(step, writeback_span(buf))

        @pl.when(step >= 1)
        def _():
            wait_all(step - 1, writeback_span(1 - buf))


def _work_items(cnt, off):
    nb = cnt.shape[0]
    nseg = N_EXPERTS * nb
    i32 = jnp.int32
    seg_n = ((cnt.reshape(nb, N_EXPERTS) + (GRAN - 1)) // GRAN).T
    seg_row = ((off.reshape(nb, N_EXPERTS) + jnp.arange(nb, dtype=i32)[:, None] * CAP) // GRAN).T.reshape(-1)
    before = jnp.cumsum(seg_n, axis=1) - seg_n
    pass_id = (before // (PASS_GRAN - TB // GRAN)).reshape(-1)
    seg_n = seg_n.reshape(-1)
    nd, ns = seg_n // 2, seg_n % 2
    gmax = nb * (TB * TOP_K // GRAN + N_EXPERTS)

    d_end = jnp.cumsum(nd)
    d_start = d_end - nd
    prev_end = jnp.concatenate([jnp.full((1,), 2, i32), (seg_row + 2 * nd)[:-1]])
    dlist = jnp.cumsum(jnp.full((gmax // 2,), 2, i32).at[d_start].add(seg_row - prev_end, mode="drop"))
    s_end = jnp.cumsum(ns)
    s_start = s_end - ns
    slist = jnp.zeros((nseg,), i32).at[jnp.where(ns == 1, s_start, nseg)].set(seg_row + 2 * nd, mode="drop")

    seg = jnp.arange(nseg, dtype=i32)
    new_item = jnp.logical_or(seg % nb == 0, pass_id != jnp.concatenate([pass_id[:1], pass_id[:-1]]))
    item_of_seg = jnp.cumsum(new_item.astype(i32)) - 1
    n_items = N_EXPERTS + gmax // (PASS_GRAN - TB // GRAN)
    first_seg = jnp.full((n_items + 1,), nseg, i32).at[
        jnp.where(new_item, item_of_seg, n_items + 1)].set(seg, mode="drop")
    lo, hi = first_seg[:-1], first_seg[1:]
    d_bound = jnp.concatenate([d_start, d_end[-1:]])
    s_bound = jnp.concatenate([s_start, s_end[-1:]])
    item_e = jnp.minimum(lo // nb, N_EXPERTS - 1)
    return (item_e.astype(i32), d_bound[lo], d_bound[hi] - d_bound[lo], s_bound[lo], s_bound[hi] - s_bound[lo],
            dlist.astype(i32), slist)


def _ffn(xs, cnt, off, wg, wu, wd):
    _, _, d = xs.shape
    items = _work_items(cnt, off)
    per_expert = lambda shape: pl.BlockSpec((1,) + shape, lambda w, ie, *_: (ie[w], 0, 0))
    grid_spec = pltpu.PrefetchScalarGridSpec(
        num_scalar_prefetch=len(items),
        grid=(items[0].shape[0],),
        in_specs=[pl.BlockSpec(memory_space=pl.ANY), per_expert((d, D_EXPERT)), per_expert((d, D_EXPERT)),
                  per_expert((D_EXPERT, d))],
        out_specs=pl.BlockSpec(memory_space=pl.ANY),
        scratch_shapes=[pltpu.VMEM((2, PASS_GRAN, GRAN, d), BF16)] * 2 + [
                        pltpu.VMEM((d, D_EXPERT), BF16), pltpu.VMEM((d, D_EXPERT), BF16),
                        pltpu.VMEM((D_EXPERT, d), BF16),
                        pltpu.SemaphoreType.DMA((2,)), pltpu.SemaphoreType.DMA((2,))],
    )
    return pl.pallas_call(
        _ffn_kernel,
        grid_spec=grid_spec,
        out_shape=jax.ShapeDtypeStruct(xs.shape, xs.dtype),
        input_output_aliases={len(items): 0},
        compiler_params=pltpu.CompilerParams(
            dimension_semantics=("arbitrary",), vmem_limit_bytes=VMEM_LIMIT),
        name="ffn",
    )(*items, xs, wg, wu, wd)


def _combine_kernel(used_sm, tail_sm, x1_ref, h2_ref, *rest):
    n_chunks = CAP // SLOT_CHUNK
    ys_refs = rest[:n_chunks]
    slots_ref, topw_ref, wgus_ref, wds_ref, o_ref = rest[n_chunks:]
    tb = x1_ref.shape[0]
    gu = _dot(h2_ref[...], wgus_ref[...])
    g, u = gu[:, :D_SHARED], gu[:, D_SHARED:]
    acc = x1_ref[...] + _dot((g * jax.nn.sigmoid(g) * u).astype(BF16), wds_ref[...])
    slots = slots_ref[...].astype(F32)
    topw = topw_ref[...]
    used = used_sm[pl.program_id(0)]

    def gather_chunk(c):
        scol = (lax.broadcasted_iota(jnp.int32, (tb, SLOT_CHUNK), 1) + c * SLOT_CHUNK).astype(F32)
        gate = jnp.zeros((tb, SLOT_CHUNK), F32)
        for k in range(TOP_K):
            gate = jnp.where(scol == slots[:, k:k + 1], topw[:, k:k + 1], gate)
        return _dot(gate.astype(BF16), ys_refs[c][0])

    for c in range(CHUNKS_TYPICAL):
        acc = acc + gather_chunk(c)
    o_ref[...] = acc
    for c in range(CHUNKS_TYPICAL, CAP // SLOT_CHUNK):
        @pl.when(c * SLOT_CHUNK < used)
        def _(c=c):
            o_ref[...] += gather_chunk(c)


def _combine(x1, h2, ys, slots, topw, used, wgus, wds):
    n, d = x1.shape
    n_chunks = CAP // SLOT_CHUNK
    const = lambda shape: pl.BlockSpec(shape, lambda i, u, t: (0,) * len(shape))
    row = lambda w: pl.BlockSpec((TB, w), lambda i, u, t: (i, 0))
    blocks = jnp.arange(n // TB, dtype=jnp.int32)
    tails = [lax.cummax(jnp.where(used > c * SLOT_CHUNK, blocks, 0)) for c in range(CHUNKS_TYPICAL, n_chunks)]
    tail_idx = jnp.stack(tails).astype(jnp.int32)
    n_blocks = n // TB

    def chunk_spec(c):
        if c < CHUNKS_TYPICAL:
            return pl.BlockSpec((1, SLOT_CHUNK, d), lambda i, u, t: (i * n_chunks + c, 0, 0))
        late = c - CHUNKS_TYPICAL
        return pl.BlockSpec((1, SLOT_CHUNK, d), lambda i, u, t: (t[late * n_blocks + i] * n_chunks + c, 0, 0))

    grid_spec = pltpu.PrefetchScalarGridSpec(
        num_scalar_prefetch=2,
        grid=(n_blocks,),
        in_specs=[row(d), row(d)] + [chunk_spec(c) for c in range(n_chunks)] + [
            row(TOP_K), row(TOP_K), const(wgus.shape), const(wds.shape)],
        out_specs=row(d),
    )
    ys_chunks = ys.reshape(n_blocks * n_chunks, SLOT_CHUNK, d)
    return pl.pallas_call(
        _combine_kernel,
        grid_spec=grid_spec,
        out_shape=jax.ShapeDtypeStruct((n, d), F32),
        compiler_params=pltpu.CompilerParams(
            dimension_semantics=("arbitrary",), vmem_limit_bytes=VMEM_LIMIT),
        name="combine",
    )(used, tail_idx.reshape(-1), x1, h2, *([ys_chunks] * n_chunks), slots, topw, wgus, wds)


def _moe(x1, h2, topi, topw, w_gate_e, w_up_e, w_down_e, w_gate_s, w_up_s, w_down_s):
    n, d = x1.shape
    wgus = jnp.concatenate([w_gate_s.astype(BF16), w_up_s.astype(BF16)], axis=-1)
    xs, slots, cnt, off = _dispatch(h2, topi)
    ys = _ffn(xs.reshape(-1, GRAN, d), cnt, off, w_gate_e, w_up_e, w_down_e)
    used = jnp.max(off + (cnt + (GRAN - 1)) // GRAN * GRAN, axis=(1, 2)).astype(jnp.int32)
    return _combine(x1, h2, ys.reshape(n // TB, CAP, d), slots, topw, used, wgus, w_down_s.astype(BF16))


def kernel(x, g_mix, w_in, q_norm_a, k_norm_a, q_norm_b, k_norm_b, rel_bias, b_forget, w_gate, b_gate,
           w_proj_a, w_proj_b, w_out, g_ffn, w_router, router_bias, w_gate_e, w_up_e, w_down_e,
           w_gate_s, w_up_s, w_down_s):
    batch, seq, d = x.shape
    xf = x.reshape(batch * seq, d)
    fox_shift, fox_top, fox_online = _fox_shift(q_norm_b, k_norm_b)
    pa0, pa1, pa2, qkb, vb, cb = _inproj(xf, g_mix, w_in, q_norm_a, k_norm_a, q_norm_b, k_norm_b, b_forget,
                                     fox_shift, seq)
    bias = jnp.stack([_toeplitz_bias(rel_bias, g, dil) for g, (_, dil) in enumerate(DIL_GROUPS)])
    ya = _dilated(pa0, pa1, pa2, bias, batch, seq)

    yb = _fox(qkb, vb, cb, fox_top, fox_online, batch, seq)
    x1, h2, topi, topw = _post(xf, ya, yb, g_mix, w_gate, b_gate, w_proj_a, w_proj_b, w_out, g_ffn,
                               w_router, router_bias)
    out = _moe(x1, h2, topi, topw, w_gate_e, w_up_e, w_down_e, w_gate_s, w_up_s, w_down_s)
    return out.reshape(batch, seq, d)
```

```python
import functools
import math

import jax
import jax.numpy as jnp
import numpy as np
from jax import lax
from jax.experimental import pallas as pl
from jax.experimental.pallas import tpu as pltpu

HEAD_DIM = 64
DIL_GROUPS = ((128, 1), (512, 4), (2048, 16))
HEADS_PER_GROUP = 4
N_HEADS_A = HEADS_PER_GROUP * len(DIL_GROUPS)
N_HEADS_B = 8
REL_BUCKETS = 32
REL_MAX_DIST = 2048
N_EXPERTS = 64
TOP_K = 8
D_EXPERT = 256
D_SHARED = 256
ROUTE_SCALE = 2.5
EPS = 1e-6

WIDTH_A = 3 * N_HEADS_A * HEAD_DIM
WIDTH_B = 3 * N_HEADS_B * HEAD_DIM
QK_B = N_HEADS_B * HEAD_DIM
OUT_A = HEADS_PER_GROUP * HEAD_DIM
OUT_B = N_HEADS_B * HEAD_DIM

LANES = 128
GROUP_W = HEADS_PER_GROUP * HEAD_DIM
WIN_J = 128
SUPER = DIL_GROUPS[-1][1] * WIN_J
NEG = -1e30
VMEM_LIMIT = 56 * 1024 * 1024

DIL_UNROLL = 8
TM_IN = 512
TM_POST = 1024
TQ_FOX = 1024
FOX_ROUNDING_SLACK = 1.02
FOX_EXP_HEADROOM = 60.0
FOX_DEAD_EXPONENT = -105.0
FOX_MAX_SHIFT = 80.0
TB = 256
GRAN = 16
CAP = TB * TOP_K + N_EXPERTS * GRAN
SLOT_CHUNK = 512
CHUNKS_TYPICAL = -(-(TB * TOP_K + N_EXPERTS * GRAN // 2) // SLOT_CHUNK)
SLOT_RADIX = 64
NO_SLOT = SLOT_RADIX ** 2 - 1
assert CAP <= NO_SLOT
PASS_GRAN = 256
FT = 256
FT_BIG = 1024

BF16 = jnp.bfloat16
F32 = jnp.float32


def _dot(a, b):
    return jnp.dot(a, b, preferred_element_type=F32)


def _dot_nt(a, b):
    return lax.dot_general(a, b, (((1,), (1,)), ((), ())), preferred_element_type=F32)


def _split3(v):
    hi = v.astype(BF16).astype(F32)
    r = v - hi
    mid = r.astype(BF16).astype(F32)
    lo = (r - mid).astype(BF16).astype(F32)
    return hi, mid, lo


def _inproj_kernel(x_ref, g_ref, wa_ref, wb_ref, wf_ref, bd_ref, tri_ref, gain_a_ref, gain_b_ref,
                   bf_ref, shift_ref, pa0_ref, pa1_ref, pa2_ref, qkb_ref, vb_ref, cb_ref, carry_ref, h_ref, *,
                   tiles_per_seq):
    tm = x_ref.shape[0]
    x = x_ref[...]
    h = x * lax.rsqrt(jnp.mean(x * x, axis=-1, keepdims=True) + EPS) * g_ref[...]
    n_lane_chunks = h_ref.shape[0]
    for c in range(n_lane_chunks):
        h_ref[c] = h[:, c * LANES:(c + 1) * LANES]
    h = h.astype(BF16)
    bd = bd_ref[...]

    def headnorm(p, gain):
        ms = _dot((p * p).astype(BF16), bd)
        return p * lax.rsqrt(ms + EPS) * gain

    for g, (pa_ref, (_, dil)) in enumerate(zip((pa0_ref, pa1_ref, pa2_ref), DIL_GROUPS)):
        rows = tm // dil
        if dil == 1:
            hg = h
        else:
            hg = jnp.concatenate([jnp.concatenate(
                [h_ref[c, pl.ds(r, rows, stride=dil), :] for c in range(n_lane_chunks)], axis=1)
                for r in range(dil)], axis=0).astype(BF16)
        qkv = _dot(hg, wa_ref[g])
        for part in range(3):
            cols = slice(part * GROUP_W, (part + 1) * GROUP_W)
            p = qkv[:, cols]
            if part < 2:
                p = headnorm(p, gain_a_ref[part:part + 1, :])
            p = p.astype(BF16)
            if dil == 1:
                pa_ref[:, cols] = p
            else:
                for r in range(dil):
                    pa_ref[0, r, :, cols] = p[r * rows:(r + 1) * rows, :]

    f = _dot(h, wf_ref[...]) + bf_ref[...]
    logf = jnp.minimum(f, 0.0) - jnp.log1p(jnp.exp(-jnp.abs(f)))
    tri = tri_ref[...]
    lh, lm, ll = _split3(logf)
    cum3 = _dot(tri, jnp.concatenate([lh, lm, ll], axis=1).astype(BF16))
    cum = cum3[:, :LANES] + cum3[:, LANES:2 * LANES] + cum3[:, 2 * LANES:]

    @pl.when(pl.program_id(0) % tiles_per_seq == 0)
    def _():
        carry_ref[...] = jnp.zeros_like(carry_ref)

    cum = cum + carry_ref[0:1, :]
    carry_ref[0:1, :] = cum[tm - 1:tm, :]
    cb_ref[0] = jnp.concatenate([cum[0:1, :], cum[tm - 1:tm, :], jnp.zeros((6, LANES), F32)], axis=0)
    ch, cm, cl = _split3(cum)

    j = lax.broadcasted_iota(jnp.int32, (tm, HEAD_DIM), 1)

    def ext_cols(vals):
        out = jnp.zeros((tm, HEAD_DIM), F32)
        ones = [pos for pos, val in enumerate(vals) if isinstance(val, float)]
        if ones:
            is_one = functools.reduce(jnp.logical_or, [j == pos for pos in ones])
            out = jnp.where(is_one, 1.0, out)
        for pos, val in enumerate(vals):
            if not isinstance(val, float):
                out = jnp.where(j == pos, val, out)
        return out

    for c in range(QK_B // GROUP_W):
        qkv = _dot(h, wb_ref[c])
        pq = headnorm(qkv[:, 0:GROUP_W], gain_b_ref[0:1, :])
        pk = headnorm(qkv[:, GROUP_W:2 * GROUP_W], gain_b_ref[1:2, :])
        pv = qkv[:, 2 * GROUP_W:3 * GROUP_W]
        r = _dot((pq * pk).astype(BF16), bd) * HEAD_DIM + shift_ref[...]
        for hh in range(HEADS_PER_GROUP):
            head = c * HEADS_PER_GROUP + hh
            lanes = slice(hh * HEAD_DIM, (hh + 1) * HEAD_DIM)
            col = lambda a, idx: a[:, idx:idx + 1]
            cs = [col(ch, head), col(cm, head), col(cl, head)]
            ext_q = ext_cols(cs + [1.0] * 3 + [-col(r, hh * HEAD_DIM)])
            ext_k = ext_cols([1.0] * 3 + [-v for v in cs] + [1.0])
            ext_v = ext_cols([1.0])
            for part, (val, ext) in enumerate(((pq, ext_q), (pk, ext_k))):
                o0 = (part * N_HEADS_B + head) * LANES
                qkb_ref[:, o0:o0 + LANES] = jnp.concatenate([val[:, lanes], ext], axis=-1).astype(BF16)
            vb_ref[:, head * LANES:(head + 1) * LANES] = jnp.concatenate(
                [pv[:, lanes], ext_v], axis=-1).astype(BF16)


def _inproj(xf, g_mix, w_in, q_norm_a, k_norm_a, q_norm_b, k_norm_b, b_forget, fox_shift, seq):
    n, d = xf.shape
    tm = TM_IN
    scale = HEAD_DIM ** -0.5
    w_bf = w_in.astype(BF16)
    qkv_w = N_HEADS_A * HEAD_DIM
    wa = jnp.stack([jnp.concatenate(
        [w_bf[:, part * qkv_w + g * GROUP_W: part * qkv_w + (g + 1) * GROUP_W] for part in range(3)],
        axis=1) for g in range(len(DIL_GROUPS))])
    wb = jnp.stack([jnp.concatenate(
        [w_bf[:, WIDTH_A + part * QK_B + c * GROUP_W: WIDTH_A + part * QK_B + (c + 1) * GROUP_W]
         for part in range(3)], axis=1) for c in range(QK_B // GROUP_W)])
    wf = jnp.pad(w_bf[:, WIDTH_A + WIDTH_B:], ((0, 0), (0, LANES - N_HEADS_B)))
    bfp = jnp.pad(b_forget.astype(F32), (0, LANES - N_HEADS_B)).reshape(1, LANES)
    seg = np.arange(GROUP_W) // HEAD_DIM
    bd = jnp.asarray((seg[:, None] == seg[None, :]).astype(np.float32) / HEAD_DIM, BF16)
    tri = jnp.asarray(np.tril(np.ones((tm, tm), np.float32)), BF16)
    gain_a = jnp.stack([jnp.tile(q_norm_a, HEADS_PER_GROUP) * scale, jnp.tile(k_norm_a, HEADS_PER_GROUP)])
    gain_b = jnp.stack([jnp.tile(q_norm_b, HEADS_PER_GROUP) * scale, jnp.tile(k_norm_b, HEADS_PER_GROUP)])
    const = lambda shape: pl.BlockSpec(shape, lambda i: (0,) * len(shape))
    tps = seq // tm
    batch = n // seq
    qkv3 = 3 * GROUP_W
    (_, d1), (_, d2) = DIL_GROUPS[1], DIL_GROUPS[2]
    return pl.pallas_call(
        functools.partial(_inproj_kernel, tiles_per_seq=tps),
        grid=(n // tm,),
        in_specs=[
            pl.BlockSpec((tm, d), lambda i: (i, 0)),
            const((1, d)), const(wa.shape), const(wb.shape), const(wf.shape),
            const(bd.shape), const(tri.shape), const(gain_a.shape), const(gain_b.shape),
            const(bfp.shape), const((1, 1)),
        ],
        out_specs=[
            pl.BlockSpec((tm, qkv3), lambda i: (i, 0)),
            pl.BlockSpec((1, d1, tm // d1, qkv3), lambda i: (i // tps, 0, i % tps, 0)),
            pl.BlockSpec((1, d2, tm // d2, qkv3), lambda i: (i // tps, 0, i % tps, 0)),
            pl.BlockSpec((tm, 2 * N_HEADS_B * LANES), lambda i: (i, 0)),
            pl.BlockSpec((tm, N_HEADS_B * LANES), lambda i: (i, 0)),
            pl.BlockSpec((1, 8, LANES), lambda i: (i, 0, 0)),
        ],
        out_shape=[
            jax.ShapeDtypeStruct((n, qkv3), BF16),
            jax.ShapeDtypeStruct((batch, d1, seq // d1, qkv3), BF16),
            jax.ShapeDtypeStruct((batch, d2, seq // d2, qkv3), BF16),
            jax.ShapeDtypeStruct((n, 2 * N_HEADS_B * LANES), BF16),
            jax.ShapeDtypeStruct((n, N_HEADS_B * LANES), BF16),
            jax.ShapeDtypeStruct((n // tm, 8, LANES), F32),
        ],
        scratch_shapes=[pltpu.VMEM((8, LANES), F32), pltpu.VMEM((d // LANES, tm, LANES), F32)],
        compiler_params=pltpu.CompilerParams(
            dimension_semantics=("arbitrary",), vmem_limit_bytes=VMEM_LIMIT),
        name="inproj",
    )(xf, g_mix.reshape(1, d), wa, wb, wf, bd, tri, gain_a, gain_b, bfp, fox_shift.reshape(1, 1))


def _dilated_kernel(p0_ref, h0_ref, p1_ref, h1_ref, p2_ref, h2_ref, bias_ref, o_ref, acc_ref, lse_ref):
    tq = WIN_J
    first_sb = pl.program_id(1) == 0
    lane_head = lax.broadcasted_iota(jnp.int32, (tq, GROUP_W), 1) // HEAD_DIM
    prev_col = lax.broadcasted_iota(jnp.int32, (tq, 2 * tq), 1) < tq
    qc, kc_, vc_ = (slice(0, GROUP_W), slice(GROUP_W, 2 * GROUP_W), slice(2 * GROUP_W, 3 * GROUP_W))

    def attend(g, q, kp, kc, vp, vc, no_prev):
        kcat = jnp.concatenate([kp, kc], axis=0)
        vcat = jnp.concatenate([vp, vc], axis=0)
        dead = jnp.logical_and(no_prev, prev_col)
        q4 = jnp.concatenate([jnp.where(lane_head == hh, q, jnp.zeros_like(q))
                              for hh in range(HEADS_PER_GROUP)], axis=0)
        s = _dot_nt(q4, kcat) + bias_ref[g].reshape(HEADS_PER_GROUP * tq, 2 * tq)
        s = jnp.where(jnp.concatenate([dead] * HEADS_PER_GROUP, axis=0), NEG, s)
        m = jnp.max(s, axis=-1, keepdims=True)
        p = jnp.exp(s - m)
        l = jnp.sum(p, axis=-1, keepdims=True)
        o4 = _dot(p.astype(BF16), vcat) * (1.0 / l)
        lse4 = m + jnp.log(l)
        acc = o4[0:tq]
        lse = jnp.broadcast_to(lse4[0:tq], (tq, GROUP_W))
        for hh in range(1, HEADS_PER_GROUP):
            sel = lane_head == hh
            acc = jnp.where(sel, o4[hh * tq:(hh + 1) * tq], acc)
            lse = jnp.where(sel, lse4[hh * tq:(hh + 1) * tq], lse)
        return acc, lse

    n_half = GROUP_W // LANES

    def merge(rows, acc, lse):
        for c in range(n_half):
            lanes = slice(c * LANES, (c + 1) * LANES)
            l1 = lse_ref[c, rows, :]
            mx = jnp.maximum(l1, lse[:, lanes])
            w1 = jnp.exp(l1 - mx)
            w2 = jnp.exp(lse[:, lanes] - mx)
            den = w1 + w2
            acc_ref[c, rows, :] = (w1 * acc_ref[c, rows, :] + w2 * acc[:, lanes]) / den
            lse_ref[c, rows, :] = mx + jnp.log(den)

    def pick(first, halo, body):
        return jnp.where(first, halo, body)

    def loop(n, body):
        def trip(i, carry):
            for u in range(DIL_UNROLL):
                body(i * DIL_UNROLL + u, carry)
            return carry
        lax.fori_loop(0, n // DIL_UNROLL, trip, 0)

    def body0(j, carry):
        st = pl.multiple_of(j * tq, tq)
        pst = pl.multiple_of(jnp.maximum(j - 1, 0) * tq, tq)
        cur, prv = pl.ds(st, tq), pl.ds(pst, tq)
        acc, lse = attend(
            0, p0_ref[0, cur, qc],
            pick(j == 0, h0_ref[0, :, kc_], p0_ref[0, prv, kc_]), p0_ref[0, cur, kc_],
            pick(j == 0, h0_ref[0, :, vc_], p0_ref[0, prv, vc_]), p0_ref[0, cur, vc_],
            jnp.logical_and(j == 0, first_sb))
        for c in range(n_half):
            acc_ref[c, cur, :] = acc[:, c * LANES:(c + 1) * LANES]
            lse_ref[c, cur, :] = lse[:, c * LANES:(c + 1) * LANES]
        return carry

    loop(SUPER // tq, body0)

    d1 = DIL_GROUPS[1][1]
    nsub1 = SUPER // d1 // tq
    def body1(t, carry):
        r, ii = t // nsub1, t % nsub1
        st = pl.multiple_of(ii * tq, tq)
        pst = pl.multiple_of(jnp.maximum(ii - 1, 0) * tq, tq)
        cur, prv = pl.ds(st, tq), pl.ds(pst, tq)
        acc, lse = attend(
            1, p1_ref[0, r, cur, qc],
            pick(ii == 0, h1_ref[0, r, :, kc_], p1_ref[0, r, prv, kc_]), p1_ref[0, r, cur, kc_],
            pick(ii == 0, h1_ref[0, r, :, vc_], p1_ref[0, r, prv, vc_]), p1_ref[0, r, cur, vc_],
            jnp.logical_and(ii == 0, first_sb))
        merge(pl.ds(ii * (tq * d1) + r, tq, stride=d1), acc, lse)
        return carry

    loop(d1 * nsub1, body1)

    d2 = DIL_GROUPS[2][1]

    def body2(r, carry):
        acc, lse = attend(2, p2_ref[0, r, :, qc], h2_ref[0, r, :, kc_], p2_ref[0, r, :, kc_],
                          h2_ref[0, r, :, vc_], p2_ref[0, r, :, vc_], first_sb)
        merge(pl.ds(r, tq, stride=d2), acc, lse)
        return carry

    loop(d2, body2)

    for c in range(n_half):
        o_ref[0, :, c * LANES:(c + 1) * LANES] = acc_ref[c].astype(o_ref.dtype)


def _rel_bucket(dist):
    max_exact = REL_BUCKETS // 2
    n = jnp.maximum(dist.astype(F32), 1.0)
    large = max_exact + (jnp.log(n / max_exact) / math.log(REL_MAX_DIST / max_exact)
                         * (REL_BUCKETS - max_exact)).astype(jnp.int32)
    large = jnp.minimum(large, REL_BUCKETS - 1)
    return jnp.where(dist < max_exact, dist, large)


def _toeplitz_bias(rel_bias, g, dil):
    tq = WIN_J
    offs = dil * (WIN_J - jnp.arange(WIN_J + 1, dtype=jnp.int32))
    hs = slice(g * HEADS_PER_GROUP, (g + 1) * HEADS_PER_GROUP)
    tab_rev = rel_bias[_rel_bucket(offs)][:, hs].T.astype(F32)
    period = 3 * tq
    neg = lambda w: jnp.full((HEADS_PER_GROUP, w), NEG, F32)
    vec = jnp.concatenate([neg(tq - 1), tab_rev, neg(period - 2 * tq)], axis=1)
    flat = jnp.broadcast_to(vec[:, None, :], (HEADS_PER_GROUP, tq, period)).reshape(HEADS_PER_GROUP, -1)
    skew = flat[:, :tq * (period - 1)].reshape(HEADS_PER_GROUP, tq, period - 1)
    return skew[:, :, tq - 1:3 * tq - 1]


def _dilated(pa0, pa1, pa2, bias, batch, seq):
    tq = WIN_J
    qkv3 = 3 * GROUP_W
    (_, d1), (_, d2) = DIL_GROUPS[1], DIL_GROUPS[2]
    nsb = seq // SUPER
    p0 = pa0.reshape(batch, seq, qkv3)
    prev_blk = lambda per_sb: (lambda b, s: jnp.maximum(s * per_sb - 1, 0))
    h0i, h1i, h2i = prev_blk(SUPER // tq), prev_blk(SUPER // d1 // tq), prev_blk(SUPER // d2 // tq)
    out = pl.pallas_call(
        _dilated_kernel,
        grid=(batch, nsb),
        in_specs=[
            pl.BlockSpec((1, SUPER, qkv3), lambda b, s: (b, s, 0)),
            pl.BlockSpec((1, tq, qkv3), lambda b, s: (b, h0i(b, s), 0)),
            pl.BlockSpec((1, d1, SUPER // d1, qkv3), lambda b, s: (b, 0, s, 0)),
            pl.BlockSpec((1, d1, tq, qkv3), lambda b, s: (b, 0, h1i(b, s), 0)),
            pl.BlockSpec((1, d2, SUPER // d2, qkv3), lambda b, s: (b, 0, s, 0)),
            pl.BlockSpec((1, d2, tq, qkv3), lambda b, s: (b, 0, h2i(b, s), 0)),
            pl.BlockSpec(bias.shape, lambda b, s: (0, 0, 0, 0)),
        ],
        out_specs=pl.BlockSpec((1, SUPER, GROUP_W), lambda b, s: (b, s, 0)),
        out_shape=jax.ShapeDtypeStruct((batch, seq, GROUP_W), BF16),
        scratch_shapes=[pltpu.VMEM((GROUP_W // LANES, SUPER, LANES), F32)] * 2,
        compiler_params=pltpu.CompilerParams(
            dimension_semantics=("arbitrary", "arbitrary"), vmem_limit_bytes=VMEM_LIMIT),
        name="dilated",
    )(p0, p0, pa1, pa1, pa2, pa2, bias)
    return out.reshape(batch * seq, GROUP_W)


def _fox_kernel(nlive_sm, q_ref, k_ref, v_ref, o_ref, m_ref, acc_ref, *, online):
    tq = q_ref.shape[1]
    half = tq // 2
    qi = pl.program_id(2)
    step = (pl.program_id(0) * pl.num_programs(1) + pl.program_id(1)) * pl.num_programs(2) + qi
    row = lax.broadcasted_iota(jnp.int32, (half, half), 0)
    col = lax.broadcasted_iota(jnp.int32, (half, half), 1)
    causal = row >= col

    def attend(hh, rows, state, start, nkeys, masked):
        m, acc = state
        lanes = slice(hh * LANES, (hh + 1) * LANES)
        s = _dot_nt(q_ref[0, rows, lanes], k_ref[0, pl.ds(start, nkeys), lanes])
        if masked:
            s = jnp.where(causal, s, NEG)
        if online:
            m_new = jnp.maximum(m, jnp.max(s, axis=-1, keepdims=True))
            acc = acc * jnp.exp(m - m_new)
            s = s - m_new
            m = m_new
        return m, acc + _dot(jnp.exp(s).astype(BF16), v_ref[0, pl.ds(start, nkeys), lanes])

    first = [qi - nlive_sm[2 * step + hh] for hh in range(2)]
    for hh in range(2):
        m_ref[hh] = jnp.full((tq, 1), NEG, F32)
        acc_ref[hh] = jnp.zeros((tq, LANES), F32)

    def full_chunk(ki, carry):
        start = pl.multiple_of(ki * tq, tq)
        for hh in range(2):
            @pl.when(ki >= first[hh])
            def _(hh=hh):
                m, acc = attend(hh, slice(None), (m_ref[hh], acc_ref[hh]), start, tq, False)
                acc_ref[hh] = acc
                if online:
                    m_ref[hh] = m
        return carry

    lax.fori_loop(jnp.minimum(first[0], first[1]), qi, full_chunk, 0)

    d0 = pl.multiple_of(qi * tq, tq)
    outs = []
    for hh in range(2):
        m, acc = m_ref[hh], acc_ref[hh]
        top, bot = slice(0, half), slice(half, tq)
        s_top = attend(hh, top, (m[top], acc[top]), d0, half, True)
        s_bot = attend(hh, bot, (m[bot], acc[bot]), d0, half, False)
        s_bot = attend(hh, bot, s_bot, d0 + half, half, True)
        a = jnp.concatenate([s_top[1], s_bot[1]], axis=0)
        outs.append(a[:, :HEAD_DIM] / a[:, HEAD_DIM:HEAD_DIM + 1])
    o_ref[0] = jnp.concatenate(outs, axis=-1).astype(o_ref.dtype)


def _fox_live_chunks(cb, top, batch, seq):
    tps = seq // TM_IN
    per = TQ_FOX // TM_IN
    nq = seq // TQ_FOX
    c_first = cb[:, 0, :N_HEADS_B].reshape(batch, tps, N_HEADS_B)[:, ::per]
    c_last = cb[:, 1, :N_HEADS_B].reshape(batch, tps, N_HEADS_B)[:, per - 1::per]
    live = (top + c_first[:, :, None, :] - c_last[:, None, :, :]) >= FOX_DEAD_EXPONENT
    back = jnp.arange(nq)[:, None] - jnp.arange(nq)[None, :]
    reach = jnp.max(jnp.where(jnp.logical_and(live, (back > 0)[None, :, :, None]),
                              back[None, :, :, None], 0), axis=2)
    reach = reach.reshape(batch, nq, N_HEADS_B // 2, 2).transpose(0, 2, 1, 3)
    return reach.reshape(-1).astype(jnp.int32)


def _fox(qkb, vb, cb, top, online, batch, seq):
    tq = TQ_FOX
    pairs = N_HEADS_B // 2
    nq = seq // tq
    qkv = qkb.reshape(batch, seq, 2 * N_HEADS_B * LANES)
    vv = vb.reshape(batch, seq, N_HEADS_B * LANES)
    all_chunks = jnp.tile(jnp.repeat(jnp.arange(nq, dtype=jnp.int32), 2), batch * pairs)

    def call(is_online, nlive):
        grid_spec = pltpu.PrefetchScalarGridSpec(
            num_scalar_prefetch=1,
            grid=(batch, pairs, nq),
            in_specs=[
                pl.BlockSpec((1, tq, 2 * LANES), lambda b, p, i, n: (b, i, p)),
                pl.BlockSpec((1, seq, 2 * LANES), lambda b, p, i, n: (b, 0, pairs + p)),
                pl.BlockSpec((1, seq, 2 * LANES), lambda b, p, i, n: (b, 0, p)),
            ],
            out_specs=pl.BlockSpec((1, tq, LANES), lambda b, p, i, n: (b, i, p)),
            scratch_shapes=[pltpu.VMEM((2, tq, 1), F32), pltpu.VMEM((2, tq, LANES), F32)],
        )
        return pl.pallas_call(
            functools.partial(_fox_kernel, online=is_online),
            grid_spec=grid_spec,
            out_shape=jax.ShapeDtypeStruct((batch, seq, OUT_B), BF16),
            compiler_params=pltpu.CompilerParams(
                dimension_semantics=("arbitrary", "arbitrary", "arbitrary"), vmem_limit_bytes=VMEM_LIMIT),
            name="fox_online" if is_online else "fox",
        )(nlive, qkv, qkv, vv)

    out = lax.cond(online, lambda: call(True, all_chunks),
                   lambda: call(False, _fox_live_chunks(cb, top, batch, seq)))
    return out.reshape(batch * seq, OUT_B)


def _fox_shift(q_norm_b, k_norm_b):
    bound = HEAD_DIM * (HEAD_DIM ** -0.5) * jnp.max(jnp.abs(q_norm_b)) * jnp.max(jnp.abs(k_norm_b))
    shift = jnp.maximum(2.0 * FOX_ROUNDING_SLACK * bound - FOX_EXP_HEADROOM, 0.0).astype(F32)
    top = 2.0 * FOX_ROUNDING_SLACK * bound - shift
    return shift, top.astype(F32), shift > FOX_MAX_SHIFT


def _post_kernel(x_ref, ya_ref, yb_ref, gmix_ref, wg_ref, bg_ref, wpa_ref, wpb_ref, wo_ref,
                 gffn_ref, wr_ref, rb_ref, x1_ref, h2_ref, topi_ref, topw_ref):
    d = x_ref.shape[1]
    x = x_ref[...]
    h = (x * lax.rsqrt(jnp.mean(x * x, axis=-1, keepdims=True) + EPS) * gmix_ref[...]).astype(BF16)
    gates = jax.nn.sigmoid(_dot(h, wg_ref[...]) + bg_ref[...])
    merged = gates[:, :d] * _dot(ya_ref[...], wpa_ref[...]) + gates[:, d:] * _dot(yb_ref[...], wpb_ref[...])
    x1 = x + _dot(merged.astype(BF16), wo_ref[...])
    x1_ref[...] = x1
    h2 = x1 * lax.rsqrt(jnp.mean(x1 * x1, axis=-1, keepdims=True) + EPS) * gffn_ref[...]
    h2_ref[...] = h2.astype(BF16)

    hh, hm, _ = _split3(h2)
    wr = wr_ref[...]
    wh = wr.astype(BF16)
    wl = (wr - wh.astype(F32)).astype(BF16)
    hh, hm = hh.astype(BF16), hm.astype(BF16)
    logits = _dot_nt(wh, hh) + _dot_nt(wh, hm) + _dot_nt(wl, hh)
    scores = jax.nn.sigmoid(logits)
    biased = scores + rb_ref[...]
    eid = lax.broadcasted_iota(jnp.int32, scores.shape, 0).astype(F32)
    chosen = jnp.zeros(scores.shape, jnp.bool_)
    idx, val = [], []
    for _ in range(TOP_K):
        cur = jnp.where(chosen, -jnp.inf, biased)
        mx = jnp.max(cur, axis=0, keepdims=True)
        first = jnp.min(jnp.where(cur == mx, eid, float(N_EXPERTS)), axis=0, keepdims=True)
        pick = eid == first
        chosen = jnp.logical_or(chosen, pick)
        idx.append(first)
        val.append(jnp.sum(jnp.where(pick, scores, 0.0), axis=0, keepdims=True))
    top_s = jnp.concatenate(val, axis=0)
    top_w = top_s / jnp.sum(top_s, axis=0, keepdims=True) * ROUTE_SCALE
    tm = scores.shape[1]
    both = jnp.concatenate(idx + [top_w, jnp.zeros((LANES - 2 * TOP_K, tm), F32)], axis=0).T
    topi_ref[...] = both[:, :TOP_K].astype(jnp.int32)
    topw_ref[...] = both[:, TOP_K:2 * TOP_K]


def _post(xf, ya, yb, g_mix, w_gate, b_gate, w_proj_a, w_proj_b, w_out, g_ffn, w_router, router_bias):
    n, d = xf.shape
    tm = TM_POST
    const = lambda shape: pl.BlockSpec(shape, lambda i: (0,) * len(shape))
    row = lambda w: pl.BlockSpec((tm, w), lambda i: (i, 0))
    args = [xf, ya, yb, g_mix.reshape(1, d), w_gate.astype(BF16), b_gate.reshape(1, 2 * d),
            w_proj_a.astype(BF16), w_proj_b.astype(BF16), w_out.astype(BF16), g_ffn.reshape(1, d),
            w_router.astype(F32).T, router_bias.astype(F32).reshape(N_EXPERTS, 1)]
    in_specs = [row(d), row(OUT_A), row(OUT_B)] + [const(a.shape) for a in args[3:]]
    return pl.pallas_call(
        _post_kernel,
        grid=(n // tm,),
        in_specs=in_specs,
        out_specs=[row(d), row(d), row(TOP_K), row(TOP_K)],
        out_shape=[jax.ShapeDtypeStruct((n, d), F32), jax.ShapeDtypeStruct((n, d), BF16),
                   jax.ShapeDtypeStruct((n, TOP_K), jnp.int32), jax.ShapeDtypeStruct((n, TOP_K), F32)],
        compiler_params=pltpu.CompilerParams(
            dimension_semantics=("arbitrary",), vmem_limit_bytes=VMEM_LIMIT),
        name="post",
    )(*args)


def _dispatch_kernel(h2_ref, topi_ref, tri_ref, upper_ref, xs_ref, slots_ref, cnt_ref, off_ref):
    tb = h2_ref.shape[0]
    topi = topi_ref[...]
    lane = lax.broadcasted_iota(jnp.int32, (tb, N_EXPERTS), 1)
    picks = [lane == topi[:, k:k + 1] for k in range(TOP_K)]
    mask = picks[0]
    for pk in picks[1:]:
        mask = jnp.logical_or(mask, pk)
    maskf = jnp.where(mask, 1.0, 0.0)
    rank = _dot(tri_ref[...], maskf.astype(BF16))
    cnt = jnp.sum(maskf, axis=0, keepdims=True)
    gran = jnp.floor((cnt + (GRAN - 1)) * (1.0 / GRAN))
    goff = _dot(jnp.broadcast_to(gran, (8, N_EXPERTS)).astype(BF16), upper_ref[...])[0:1]
    off = goff * GRAN
    slot_te = off + rank
    slots = jnp.concatenate(
        [jnp.sum(jnp.where(pk, slot_te, 0.0), axis=-1, keepdims=True) for pk in picks], axis=1)
    slots_ref[...] = slots.astype(jnp.int32)
    cnt_ref[0] = cnt.astype(jnp.int32)
    off_ref[0] = off.astype(jnp.int32)
    v = jnp.where(mask, slot_te, float(NO_SLOT))
    v_hi = jnp.floor(v * (1.0 / SLOT_RADIX))
    w = jnp.concatenate([v_hi * SLOT_RADIX, v - v_hi * SLOT_RADIX], axis=1).T.astype(BF16)
    end = off + gran * GRAN
    used = jnp.max(end).astype(jnp.int32)
    h2 = h2_ref[...]

    def lookup(first, rows):
        s_e = (lax.broadcasted_iota(jnp.int32, (rows, N_EXPERTS), 0) + first).astype(F32)
        own = jnp.where(jnp.logical_and(s_e >= off, s_e < end), 1.0, 0.0)
        return _dot(jnp.concatenate([own, own], axis=1).astype(BF16), w)

    def sort_chunk(c, looked):
        s_t = (lax.broadcasted_iota(jnp.int32, (SLOT_CHUNK, tb), 0) + c * SLOT_CHUNK).astype(F32)
        onehot = jnp.where(looked == s_t, 1.0, 0.0).astype(BF16)
        xs_ref[0, c * SLOT_CHUNK:(c + 1) * SLOT_CHUNK, :] = _dot(onehot, h2).astype(BF16)

    looked_typ = lookup(0, CHUNKS_TYPICAL * SLOT_CHUNK)
    for c in range(CAP // SLOT_CHUNK):
        if c < CHUNKS_TYPICAL:
            sort_chunk(c, looked_typ[c * SLOT_CHUNK:(c + 1) * SLOT_CHUNK, :])
        else:
            @pl.when(c * SLOT_CHUNK < used)
            def _(c=c):
                sort_chunk(c, lookup(c * SLOT_CHUNK, SLOT_CHUNK))

            @pl.when(c * SLOT_CHUNK >= used)
            def _(c=c):
                xs_ref[0, c * SLOT_CHUNK:(c + 1) * SLOT_CHUNK, :] = jnp.zeros((SLOT_CHUNK, xs_ref.shape[2]), BF16)


def _dispatch(h2, topi):
    n, d = h2.shape
    nb = n // TB
    tri = jnp.asarray(np.tril(np.ones((TB, TB), np.float32), -1), BF16)
    upper = jnp.asarray(np.triu(np.ones((N_EXPERTS, N_EXPERTS), np.float32), 1), BF16)
    const = lambda shape: pl.BlockSpec(shape, lambda i: (0,) * len(shape))
    meta = pl.BlockSpec((1, 1, N_EXPERTS), lambda i: (i, 0, 0))
    return pl.pallas_call(
        _dispatch_kernel,
        grid=(nb,),
        in_specs=[pl.BlockSpec((TB, d), lambda i: (i, 0)), pl.BlockSpec((TB, TOP_K), lambda i: (i, 0)),
                  const(tri.shape), const(upper.shape)],
        out_specs=[pl.BlockSpec((1, CAP, d), lambda i: (i, 0, 0)),
                   pl.BlockSpec((TB, TOP_K), lambda i: (i, 0)), meta, meta],
        out_shape=[jax.ShapeDtypeStruct((nb, CAP, d), BF16), jax.ShapeDtypeStruct((n, TOP_K), jnp.int32),
                   jax.ShapeDtypeStruct((nb, 1, N_EXPERTS), jnp.int32),
                   jax.ShapeDtypeStruct((nb, 1, N_EXPERTS), jnp.int32)],
        compiler_params=pltpu.CompilerParams(
            dimension_semantics=("arbitrary",), vmem_limit_bytes=VMEM_LIMIT),
        name="dispatch",
    )(h2, topi, tri, upper)


def _ffn_kernel(item_e_sm, item_d0_sm, item_nd_sm, item_s0_sm, item_ns_sm, dlist_sm, slist_sm,
                xs_hbm, wg_ref, wu_ref, wd_ref, ys_hbm, xbuf, ybuf, wg_bf, wu_bf, wd_bf, sem_in, sem_out):
    step = pl.program_id(0)
    nsteps = pl.num_programs(0)
    buf = step % 2

    def granules(st):
        return 2 * item_nd_sm[st] + item_ns_sm[st]

    def for_copies(st, fn):
        d0, nd = item_d0_sm[st], item_nd_sm[st]
        s0, ns = item_s0_sm[st], item_ns_sm[st]

        def pair(j, carry):
            fn(dlist_sm[d0 + j], 2 * j, 2)
            return carry

        def single(j, carry):
            fn(slist_sm[s0 + j], 2 * nd + j, 1)
            return carry

        lax.fori_loop(0, nd, pair, 0)
        lax.fori_loop(0, ns, single, 0)

    def fetch(b_):
        return lambda src, dst, k: pltpu.make_async_copy(
            xs_hbm.at[pl.ds(src, k)], xbuf.at[b_, pl.ds(dst, k)], sem_in.at[b_])

    def writeback(b_):
        return lambda src, dst, k: pltpu.make_async_copy(
            ybuf.at[b_, pl.ds(dst, k)], ys_hbm.at[pl.ds(src, k)], sem_out.at[b_])

    def start(mk):
        return lambda src, dst, k: mk(src, dst, k).start()

    def wait_all(st, span):
        n = granules(st)
        size = PASS_GRAN
        while size >= 1:
            @pl.when((n & size) != 0)
            def _(size=size):
                span(size).wait()
            size //= 2
        return n

    def fetch_span(b_):
        return lambda k: pltpu.make_async_copy(
            xs_hbm.at[pl.ds(0, k)], xbuf.at[b_, pl.ds(0, k)], sem_in.at[b_])

    def writeback_span(b_):
        return lambda k: pltpu.make_async_copy(
            ybuf.at[b_, pl.ds(0, k)], ys_hbm.at[pl.ds(0, k)], sem_out.at[b_])

    @pl.when(step == 0)
    def _():
        xbuf[...] = jnp.zeros_like(xbuf)
        for_copies(step, start(fetch(0)))

    @pl.when(step + 1 < nsteps)
    def _():
        for_copies(step + 1, start(fetch(1 - buf)))

    ngran = wait_all(step, fetch_span(buf))

    @pl.when(step >= 2)
    def _():
        wait_all(step - 2, writeback_span(buf))

    @pl.when(ngran > 0)
    def _():
        wg_bf[...] = wg_ref[0].astype(BF16)
        wu_bf[...] = wu_ref[0].astype(BF16)
        wd_bf[...] = wd_ref[0].astype(BF16)

    x_cols = xbuf.shape[-1]

    def ffn_rows(base, rows):
        grans = pl.ds(pl.multiple_of(base // GRAN, rows // GRAN), rows // GRAN)
        x = xbuf[buf, grans].reshape(rows, x_cols)
        g = _dot(x, wg_bf[...])
        u = _dot(x, wu_bf[...])
        mid = (g * jax.nn.sigmoid(g) * u).astype(BF16)
        ybuf[buf, grans] = _dot(mid, wd_bf[...]).astype(BF16).reshape(rows // GRAN, GRAN, x_cols)

    nt = (ngran * GRAN + (FT - 1)) // FT
    big = FT_BIG // FT

    def big_tile(i, carry):
        ffn_rows(pl.multiple_of(i * FT_BIG, FT_BIG), FT_BIG)
        return carry

    lax.fori_loop(0, nt // big, big_tile, 0)
    size = big // 2
    while size >= 1:
        @pl.when((nt & size) != 0)
        def _(size=size):
            ffn_rows(pl.multiple_of((nt & ~(2 * size - 1)) * FT, size * FT), size * FT)
        size //= 2

    for_copies(step, start(writeback(buf)))

    @pl.when(step == nsteps - 1)
    def _():
        wait_all(step, writeback_span(buf))

        @pl.when(step >= 1)
        def _():
            wait_all(step - 1, writeback_span(1 - buf))


def _work_items(cnt, off):
    nb = cnt.shape[0]
    nseg = N_EXPERTS * nb
    i32 = jnp.int32
    seg_n = ((cnt.reshape(nb, N_EXPERTS) + (GRAN - 1)) // GRAN).T
    seg_row = ((off.reshape(nb, N_EXPERTS) + jnp.arange(nb, dtype=i32)[:, None] * CAP) // GRAN).T.reshape(-1)
    before = jnp.cumsum(seg_n, axis=1) - seg_n
    pass_id = (before // (PASS_GRAN - TB // GRAN)).reshape(-1)
    seg_n = seg_n.reshape(-1)
    nd, ns = seg_n // 2, seg_n % 2
    gmax = nb * (TB * TOP_K // GRAN + N_EXPERTS)

    d_end = jnp.cumsum(nd)
    d_start = d_end - nd
    prev_end = jnp.concatenate([jnp.full((1,), 2, i32), (seg_row + 2 * nd)[:-1]])
    dlist = jnp.cumsum(jnp.full((gmax // 2,), 2, i32).at[d_start].add(seg_row - prev_end, mode="drop"))
    s_end = jnp.cumsum(ns)
    s_start = s_end - ns
    slist = jnp.zeros((nseg,), i32).at[jnp.where(ns == 1, s_start, nseg)].add(seg_row + 2 * nd, mode="drop")

    seg = jnp.arange(nseg, dtype=i32)
    new_item = jnp.logical_or(seg % nb == 0, pass_id != jnp.concatenate([pass_id[:1], pass_id[:-1]]))
    item_of_seg = jnp.cumsum(new_item.astype(i32)) - 1
    n_items = N_EXPERTS + gmax // (PASS_GRAN - TB // GRAN)
    first_seg = jnp.full((n_items + 1,), nseg, i32).at[
        jnp.where(new_item, item_of_seg, n_items + 1)].add(seg - nseg, mode="drop")
    lo, hi = first_seg[:-1], first_seg[1:]
    d_bound = jnp.concatenate([d_start, d_end[-1:]])
    s_bound = jnp.concatenate([s_start, s_end[-1:]])
    item_e = jnp.minimum(lo // nb, N_EXPERTS - 1)
    return (item_e.astype(i32), d_bound[lo], d_bound[hi] - d_bound[lo], s_bound[lo], s_bound[hi] - s_bound[lo],
            dlist.astype(i32), slist)


def _ffn(xs, cnt, off, wg, wu, wd):
    _, _, d = xs.shape
    items = _work_items(cnt, off)
    per_expert = lambda shape: pl.BlockSpec((1,) + shape, lambda w, ie, *_: (ie[w], 0, 0))
    grid_spec = pltpu.PrefetchScalarGridSpec(
        num_scalar_prefetch=len(items),
        grid=(items[0].shape[0],),
        in_specs=[pl.BlockSpec(memory_space=pl.ANY), per_expert((d, D_EXPERT)), per_expert((d, D_EXPERT)),
                  per_expert((D_EXPERT, d))],
        out_specs=pl.BlockSpec(memory_space=pl.ANY),
        scratch_shapes=[pltpu.VMEM((2, PASS_GRAN, GRAN, d), BF16)] * 2 + [
                        pltpu.VMEM((d, D_EXPERT), BF16), pltpu.VMEM((d, D_EXPERT), BF16),
                        pltpu.VMEM((D_EXPERT, d), BF16),
                        pltpu.SemaphoreType.DMA((2,)), pltpu.SemaphoreType.DMA((2,))],
    )
    return pl.pallas_call(
        _ffn_kernel,
        grid_spec=grid_spec,
        out_shape=jax.ShapeDtypeStruct(xs.shape, xs.dtype),
        input_output_aliases={len(items): 0},
        compiler_params=pltpu.CompilerParams(
            dimension_semantics=("arbitrary",), vmem_limit_bytes=VMEM_LIMIT),
        name="ffn",
    )(*items, xs, wg, wu, wd)


def _combine_kernel(used_sm, tail_sm, x1_ref, h2_ref, *rest):
    n_chunks = CAP // SLOT_CHUNK
    ys_refs = rest[:n_chunks]
    slots_ref, topw_ref, wgus_ref, wds_ref, o_ref = rest[n_chunks:]
    tb = x1_ref.shape[0]
    gu = _dot(h2_ref[...], wgus_ref[...])
    g, u = gu[:, :D_SHARED], gu[:, D_SHARED:]
    acc = x1_ref[...] + _dot((g * jax.nn.sigmoid(g) * u).astype(BF16), wds_ref[...])
    slots = slots_ref[...].astype(F32)
    topw = topw_ref[...]
    used = used_sm[pl.program_id(0)]

    def gather_chunk(c):
        scol = (lax.broadcasted_iota(jnp.int32, (tb, SLOT_CHUNK), 1) + c * SLOT_CHUNK).astype(F32)
        gate = jnp.zeros((tb, SLOT_CHUNK), F32)
        for k in range(TOP_K):
            gate = jnp.where(scol == slots[:, k:k + 1], topw[:, k:k + 1], gate)
        return _dot(gate.astype(BF16), ys_refs[c][0])

    for c in range(CHUNKS_TYPICAL):
        acc = acc + gather_chunk(c)
    o_ref[...] = acc
    for c in range(CHUNKS_TYPICAL, CAP // SLOT_CHUNK):
        @pl.when(c * SLOT_CHUNK < used)
        def _(c=c):
            o_ref[...] += gather_chunk(c)


def _combine(x1, h2, ys, slots, topw, used, wgus, wds):
    n, d = x1.shape
    n_chunks = CAP // SLOT_CHUNK
    const = lambda shape: pl.BlockSpec(shape, lambda i, u, t: (0,) * len(shape))
    row = lambda w: pl.BlockSpec((TB, w), lambda i, u, t: (i, 0))
    blocks = jnp.arange(n // TB, dtype=jnp.int32)
    tails = [lax.cummax(jnp.where(used > c * SLOT_CHUNK, blocks, 0)) for c in range(CHUNKS_TYPICAL, n_chunks)]
    tail_idx = jnp.stack(tails).astype(jnp.int32)
    n_blocks = n // TB

    def chunk_spec(c):
        if c < CHUNKS_TYPICAL:
            return pl.BlockSpec((1, SLOT_CHUNK, d), lambda i, u, t: (i * n_chunks + c, 0, 0))
        late = c - CHUNKS_TYPICAL
        return pl.BlockSpec((1, SLOT_CHUNK, d), lambda i, u, t: (t[late * n_blocks + i] * n_chunks + c, 0, 0))

    grid_spec = pltpu.PrefetchScalarGridSpec(
        num_scalar_prefetch=2,
        grid=(n_blocks,),
        in_specs=[row(d), row(d)] + [chunk_spec(c) for c in range(n_chunks)] + [
            row(TOP_K), row(TOP_K), const(wgus.shape), const(wds.shape)],
        out_specs=row(d),
    )
    ys_chunks = ys.reshape(n_blocks * n_chunks, SLOT_CHUNK, d)
    return pl.pallas_call(
        _combine_kernel,
        grid_spec=grid_spec,
        out_shape=jax.ShapeDtypeStruct((n, d), F32),
        compiler_params=pltpu.CompilerParams(
            dimension_semantics=("arbitrary",), vmem_limit_bytes=VMEM_LIMIT),
        name="combine",
    )(used, tail_idx.reshape(-1), x1, h2, *([ys_chunks] * n_chunks), slots, topw, wgus, wds)


def _moe(x1, h2, topi, topw, w_gate_e, w_up_e, w_down_e, w_gate_s, w_up_s, w_down_s):
    n, d = x1.shape
    wgus = jnp.concatenate([w_gate_s.astype(BF16), w_up_s.astype(BF16)], axis=-1)
    xs, slots, cnt, off = _dispatch(h2, topi)
    ys = _ffn(xs.reshape(-1, GRAN, d), cnt, off, w_gate_e, w_up_e, w_down_e)
    used = jnp.max(off + (cnt + (GRAN - 1)) // GRAN * GRAN, axis=(1, 2)).astype(jnp.int32)
    return _combine(x1, h2, ys.reshape(n // TB, CAP, d), slots, topw, used, wgus, w_down_s.astype(BF16))


def kernel(x, g_mix, w_in, q_norm_a, k_norm_a, q_norm_b, k_norm_b, rel_bias, b_forget, w_gate, b_gate,
           w_proj_a, w_proj_b, w_out, g_ffn, w_router, router_bias, w_gate_e, w_up_e, w_down_e,
           w_gate_s, w_up_s, w_down_s):
    batch, seq, d = x.shape
    xf = x.reshape(batch * seq, d)
    fox_shift, fox_top, fox_online = _fox_shift(q_norm_b, k_norm_b)
    pa0, pa1, pa2, qkb, vb, cb = _inproj(xf, g_mix, w_in, q_norm_a, k_norm_a, q_norm_b, k_norm_b, b_forget,
                                     fox_shift, seq)
    bias = jnp.stack([_toeplitz_bias(rel_bias, g, dil) for g, (_, dil) in enumerate(DIL_GROUPS)])
    ya = _dilated(pa0, pa1, pa2, bias, batch, seq)

    yb = _fox(qkb, vb, cb, fox_top, fox_online, batch, seq)
    x1, h2, topi, topw = _post(xf, ya, yb, g_mix, w_gate, b_gate, w_proj_a, w_proj_b, w_out, g_ffn,
                               w_router, router_bias)
    out = _moe(x1, h2, topi, topw, w_gate_e, w_up_e, w_down_e, w_gate_s, w_up_s, w_down_s)
    return out.reshape(batch, seq, d)
```

```python
import functools
import math

import jax
import jax.numpy as jnp
import numpy as np
from jax import lax
from jax.experimental import pallas as pl
from jax.experimental.pallas import tpu as pltpu

HEAD_DIM = 64
DIL_GROUPS = ((128, 1), (512, 4), (2048, 16))
HEADS_PER_GROUP = 4
N_HEADS_A = HEADS_PER_GROUP * len(DIL_GROUPS)
N_HEADS_B = 8
REL_BUCKETS = 32
REL_MAX_DIST = 2048
N_EXPERTS = 64
TOP_K = 8
D_EXPERT = 256
D_SHARED = 256
ROUTE_SCALE = 2.5
EPS = 1e-6

WIDTH_A = 3 * N_HEADS_A * HEAD_DIM
WIDTH_B = 3 * N_HEADS_B * HEAD_DIM
QK_B = N_HEADS_B * HEAD_DIM
OUT_A = HEADS_PER_GROUP * HEAD_DIM
OUT_B = N_HEADS_B * HEAD_DIM

LANES = 128
GROUP_W = HEADS_PER_GROUP * HEAD_DIM
WIN_J = 128
SUPER = DIL_GROUPS[-1][1] * WIN_J
NEG = -1e30
VMEM_LIMIT = 56 * 1024 * 1024

DIL_UNROLL = 8
TM_IN = 512
TM_POST = 1024
TQ_FOX = 1024
FOX_ROUNDING_SLACK = 1.02
FOX_EXP_HEADROOM = 60.0
FOX_DEAD_EXPONENT = -105.0
FOX_MAX_SHIFT = 80.0
TB = 256
GRAN = 16
CAP = TB * TOP_K + N_EXPERTS * GRAN
SLOT_CHUNK = 512
CHUNKS_TYPICAL = -(-(TB * TOP_K + N_EXPERTS * GRAN // 2) // SLOT_CHUNK)
SLOT_RADIX = 64
NO_SLOT = SLOT_RADIX ** 2 - 1
assert CAP <= NO_SLOT
PASS_GRAN = 256
FT = 256
FT_BIG = 1024

BF16 = jnp.bfloat16
F32 = jnp.float32


def _dot(a, b):
    return jnp.dot(a, b, preferred_element_type=F32)


def _dot_nt(a, b):
    return lax.dot_general(a, b, (((1,), (1,)), ((), ())), preferred_element_type=F32)


def _split3(v):
    hi = v.astype(BF16).astype(F32)
    r = v - hi
    mid = r.astype(BF16).astype(F32)
    lo = (r - mid).astype(BF16).astype(F32)
    return hi, mid, lo


def _inproj_kernel(x_ref, g_ref, wa_ref, wb_ref, wf_ref, bd_ref, tri_ref, gain_a_ref, gain_b_ref,
                   bf_ref, shift_ref, pa0_ref, pa1_ref, pa2_ref, qkb_ref, vb_ref, cb_ref, carry_ref, h_ref, *,
                   tiles_per_seq):
    tm = x_ref.shape[0]
    x = x_ref[...]
    h = x * lax.rsqrt(jnp.mean(x * x, axis=-1, keepdims=True) + EPS) * g_ref[...]
    n_lane_chunks = h_ref.shape[0]
    for c in range(n_lane_chunks):
        h_ref[c] = h[:, c * LANES:(c + 1) * LANES]
    h = h.astype(BF16)
    bd = bd_ref[...]

    def headnorm(p, gain):
        ms = _dot((p * p).astype(BF16), bd)
        return p * lax.rsqrt(ms + EPS) * gain

    for g, (pa_ref, (_, dil)) in enumerate(zip((pa0_ref, pa1_ref, pa2_ref), DIL_GROUPS)):
        rows = tm // dil
        if dil == 1:
            hg = h
        else:
            hg = jnp.concatenate([jnp.concatenate(
                [h_ref[c, pl.ds(r, rows, stride=dil), :] for c in range(n_lane_chunks)], axis=1)
                for r in range(dil)], axis=0).astype(BF16)
        qkv = _dot(hg, wa_ref[g])
        for part in range(3):
            cols = slice(part * GROUP_W, (part + 1) * GROUP_W)
            p = qkv[:, cols]
            if part < 2:
                p = headnorm(p, gain_a_ref[part:part + 1, :])
            p = p.astype(BF16)
            if dil == 1:
                pa_ref[:, cols] = p
            else:
                for r in range(dil):
                    pa_ref[0, r, :, cols] = p[r * rows:(r + 1) * rows, :]

    f = _dot(h, wf_ref[...]) + bf_ref[...]
    logf = jnp.minimum(f, 0.0) - jnp.log1p(jnp.exp(-jnp.abs(f)))
    tri = tri_ref[...]
    lh, lm, ll = _split3(logf)
    cum3 = _dot(tri, jnp.concatenate([lh, lm, ll], axis=1).astype(BF16))
    cum = cum3[:, :LANES] + cum3[:, LANES:2 * LANES] + cum3[:, 2 * LANES:]

    @pl.when(pl.program_id(0) % tiles_per_seq == 0)
    def _():
        carry_ref[...] = jnp.zeros_like(carry_ref)

    cum = cum + carry_ref[0:1, :]
    carry_ref[0:1, :] = cum[tm - 1:tm, :]
    cb_ref[0] = jnp.concatenate([cum[0:1, :], cum[tm - 1:tm, :], jnp.zeros((6, LANES), F32)], axis=0)
    ch, cm, cl = _split3(cum)

    j = lax.broadcasted_iota(jnp.int32, (tm, HEAD_DIM), 1)

    def ext_cols(vals):
        out = jnp.zeros((tm, HEAD_DIM), F32)
        ones = [pos for pos, val in enumerate(vals) if isinstance(val, float)]
        if ones:
            is_one = functools.reduce(jnp.logical_or, [j == pos for pos in ones])
            out = jnp.where(is_one, 1.0, out)
        for pos, val in enumerate(vals):
            if not isinstance(val, float):
                out = jnp.where(j == pos, val, out)
        return out

    for c in range(QK_B // GROUP_W):
        qkv = _dot(h, wb_ref[c])
        pq = headnorm(qkv[:, 0:GROUP_W], gain_b_ref[0:1, :])
        pk = headnorm(qkv[:, GROUP_W:2 * GROUP_W], gain_b_ref[1:2, :])
        pv = qkv[:, 2 * GROUP_W:3 * GROUP_W]
        r = _dot((pq * pk).astype(BF16), bd) * HEAD_DIM + shift_ref[...]
        for hh in range(HEADS_PER_GROUP):
            head = c * HEADS_PER_GROUP + hh
            lanes = slice(hh * HEAD_DIM, (hh + 1) * HEAD_DIM)
            col = lambda a, idx: a[:, idx:idx + 1]
            cs = [col(ch, head), col(cm, head), col(cl, head)]
            ext_q = ext_cols(cs + [1.0] * 3 + [-col(r, hh * HEAD_DIM)])
            ext_k = ext_cols([1.0] * 3 + [-v for v in cs] + [1.0])
            ext_v = ext_cols([1.0])
            for part, (val, ext) in enumerate(((pq, ext_q), (pk, ext_k))):
                o0 = (part * N_HEADS_B + head) * LANES
                qkb_ref[:, o0:o0 + LANES] = jnp.concatenate([val[:, lanes], ext], axis=-1).astype(BF16)
            vb_ref[:, head * LANES:(head + 1) * LANES] = jnp.concatenate(
                [pv[:, lanes], ext_v], axis=-1).astype(BF16)


def _inproj(xf, g_mix, w_in, q_norm_a, k_norm_a, q_norm_b, k_norm_b, b_forget, fox_shift, seq):
    n, d = xf.shape
    tm = TM_IN
    scale = HEAD_DIM ** -0.5
    w_bf = w_in.astype(BF16)
    qkv_w = N_HEADS_A * HEAD_DIM
    wa = jnp.stack([jnp.concatenate(
        [w_bf[:, part * qkv_w + g * GROUP_W: part * qkv_w + (g + 1) * GROUP_W] for part in range(3)],
        axis=1) for g in range(len(DIL_GROUPS))])
    wb = jnp.stack([jnp.concatenate(
        [w_bf[:, WIDTH_A + part * QK_B + c * GROUP_W: WIDTH_A + part * QK_B + (c + 1) * GROUP_W]
         for part in range(3)], axis=1) for c in range(QK_B // GROUP_W)])
    wf = jnp.pad(w_bf[:, WIDTH_A + WIDTH_B:], ((0, 0), (0, LANES - N_HEADS_B)))
    bfp = jnp.pad(b_forget.astype(F32), (0, LANES - N_HEADS_B)).reshape(1, LANES)
    seg = np.arange(GROUP_W) // HEAD_DIM
    bd = jnp.asarray((seg[:, None] == seg[None, :]).astype(np.float32) / HEAD_DIM, BF16)
    tri = jnp.asarray(np.tril(np.ones((tm, tm), np.float32)), BF16)
    gain_a = jnp.stack([jnp.tile(q_norm_a, HEADS_PER_GROUP) * scale, jnp.tile(k_norm_a, HEADS_PER_GROUP)])
    gain_b = jnp.stack([jnp.tile(q_norm_b, HEADS_PER_GROUP) * scale, jnp.tile(k_norm_b, HEADS_PER_GROUP)])
    const = lambda shape: pl.BlockSpec(shape, lambda i: (0,) * len(shape))
    tps = seq // tm
    batch = n // seq
    qkv3 = 3 * GROUP_W
    (_, d1), (_, d2) = DIL_GROUPS[1], DIL_GROUPS[2]
    return pl.pallas_call(
        functools.partial(_inproj_kernel, tiles_per_seq=tps),
        grid=(n // tm,),
        in_specs=[
            pl.BlockSpec((tm, d), lambda i: (i, 0)),
            const((1, d)), const(wa.shape), const(wb.shape), const(wf.shape),
            const(bd.shape), const(tri.shape), const(gain_a.shape), const(gain_b.shape),
            const(bfp.shape), const((1, 1)),
        ],
        out_specs=[
            pl.BlockSpec((tm, qkv3), lambda i: (i, 0)),
            pl.BlockSpec((1, d1, tm // d1, qkv3), lambda i: (i // tps, 0, i % tps, 0)),
            pl.BlockSpec((1, d2, tm // d2, qkv3), lambda i: (i // tps, 0, i % tps, 0)),
            pl.BlockSpec((tm, 2 * N_HEADS_B * LANES), lambda i: (i, 0)),
            pl.BlockSpec((tm, N_HEADS_B * LANES), lambda i: (i, 0)),
            pl.BlockSpec((1, 8, LANES), lambda i: (i, 0, 0)),
        ],
        out_shape=[
            jax.ShapeDtypeStruct((n, qkv3), BF16),
            jax.ShapeDtypeStruct((batch, d1, seq // d1, qkv3), BF16),
            jax.ShapeDtypeStruct((batch, d2, seq // d2, qkv3), BF16),
            jax.ShapeDtypeStruct((n, 2 * N_HEADS_B * LANES), BF16),
            jax.ShapeDtypeStruct((n, N_HEADS_B * LANES), BF16),
            jax.ShapeDtypeStruct((n // tm, 8, LANES), F32),
        ],
        scratch_shapes=[pltpu.VMEM((8, LANES), F32), pltpu.VMEM((d // LANES, tm, LANES), F32)],
        compiler_params=pltpu.CompilerParams(
            dimension_semantics=("arbitrary",), vmem_limit_bytes=VMEM_LIMIT),
        name="inproj",
    )(xf, g_mix.reshape(1, d), wa, wb, wf, bd, tri, gain_a, gain_b, bfp, fox_shift.reshape(1, 1))


def _dilated_kernel(p0_ref, h0_ref, p1_ref, h1_ref, p2_ref, h2_ref, bias_ref, o_ref, acc_ref, lse_ref):
    tq = WIN_J
    first_sb = pl.program_id(1) == 0
    lane_head = lax.broadcasted_iota(jnp.int32, (tq, GROUP_W), 1) // HEAD_DIM
    prev_col = lax.broadcasted_iota(jnp.int32, (tq, 2 * tq), 1) < tq
    qc, kc_, vc_ = (slice(0, GROUP_W), slice(GROUP_W, 2 * GROUP_W), slice(2 * GROUP_W, 3 * GROUP_W))

    def attend(g, q, kp, kc, vp, vc, no_prev):
        kcat = jnp.concatenate([kp, kc], axis=0)
        vcat = jnp.concatenate([vp, vc], axis=0)
        dead = jnp.logical_and(no_prev, prev_col)
        q4 = jnp.concatenate([jnp.where(lane_head == hh, q, jnp.zeros_like(q))
                              for hh in range(HEADS_PER_GROUP)], axis=0)
        s = _dot_nt(q4, kcat) + bias_ref[g].reshape(HEADS_PER_GROUP * tq, 2 * tq)
        s = jnp.where(jnp.concatenate([dead] * HEADS_PER_GROUP, axis=0), NEG, s)
        m = jnp.max(s, axis=-1, keepdims=True)
        p = jnp.exp(s - m)
        l = jnp.sum(p, axis=-1, keepdims=True)
        o4 = _dot(p.astype(BF16), vcat) * (1.0 / l)
        lse4 = m + jnp.log(l)
        acc = o4[0:tq]
        lse = jnp.broadcast_to(lse4[0:tq], (tq, GROUP_W))
        for hh in range(1, HEADS_PER_GROUP):
            sel = lane_head == hh
            acc = jnp.where(sel, o4[hh * tq:(hh + 1) * tq], acc)
            lse = jnp.where(sel, lse4[hh * tq:(hh + 1) * tq], lse)
        return acc, lse

    n_half = GROUP_W // LANES

    def merge(rows, acc, lse):
        for c in range(n_half):
            lanes = slice(c * LANES, (c + 1) * LANES)
            l1 = lse_ref[c, rows, :]
            mx = jnp.maximum(l1, lse[:, lanes])
            w1 = jnp.exp(l1 - mx)
            w2 = jnp.exp(lse[:, lanes] - mx)
            den = w1 + w2
            acc_ref[c, rows, :] = (w1 * acc_ref[c, rows, :] + w2 * acc[:, lanes]) / den
            lse_ref[c, rows, :] = mx + jnp.log(den)

    def pick(first, halo, body):
        return jnp.where(first, halo, body)

    def loop(n, body):
        def trip(i, carry):
            for u in range(DIL_UNROLL):
                body(i * DIL_UNROLL + u, carry)
            return carry
        lax.fori_loop(0, n // DIL_UNROLL, trip, 0)

    def body0(j, carry):
        st = pl.multiple_of(j * tq, tq)
        pst = pl.multiple_of(jnp.maximum(j - 1, 0) * tq, tq)
        cur, prv = pl.ds(st, tq), pl.ds(pst, tq)
        acc, lse = attend(
            0, p0_ref[0, cur, qc],
            pick(j == 0, h0_ref[0, :, kc_], p0_ref[0, prv, kc_]), p0_ref[0, cur, kc_],
            pick(j == 0, h0_ref[0, :, vc_], p0_ref[0, prv, vc_]), p0_ref[0, cur, vc_],
            jnp.logical_and(j == 0, first_sb))
        for c in range(n_half):
            acc_ref[c, cur, :] = acc[:, c * LANES:(c + 1) * LANES]
            lse_ref[c, cur, :] = lse[:, c * LANES:(c + 1) * LANES]
        return carry

    loop(SUPER // tq, body0)

    d1 = DIL_GROUPS[1][1]
    nsub1 = SUPER // d1 // tq
    def body1(t, carry):
        r, ii = t // nsub1, t % nsub1
        st = pl.multiple_of(ii * tq, tq)
        pst = pl.multiple_of(jnp.maximum(ii - 1, 0) * tq, tq)
        cur, prv = pl.ds(st, tq), pl.ds(pst, tq)
        acc, lse = attend(
            1, p1_ref[0, r, cur, qc],
            pick(ii == 0, h1_ref[0, r, :, kc_], p1_ref[0, r, prv, kc_]), p1_ref[0, r, cur, kc_],
            pick(ii == 0, h1_ref[0, r, :, vc_], p1_ref[0, r, prv, vc_]), p1_ref[0, r, cur, vc_],
            jnp.logical_and(ii == 0, first_sb))
        merge(pl.ds(ii * (tq * d1) + r, tq, stride=d1), acc, lse)
        return carry

    loop(d1 * nsub1, body1)

    d2 = DIL_GROUPS[2][1]

    def body2(r, carry):
        acc, lse = attend(2, p2_ref[0, r, :, qc], h2_ref[0, r, :, kc_], p2_ref[0, r, :, kc_],
                          h2_ref[0, r, :, vc_], p2_ref[0, r, :, vc_], first_sb)
        merge(pl.ds(r, tq, stride=d2), acc, lse)
        return carry

    loop(d2, body2)

    for c in range(n_half):
        o_ref[0, :, c * LANES:(c + 1) * LANES] = acc_ref[c].astype(o_ref.dtype)


def _rel_bucket(dist):
    max_exact = REL_BUCKETS // 2
    n = jnp.maximum(dist.astype(F32), 1.0)
    large = max_exact + (jnp.log(n / max_exact) / math.log(REL_MAX_DIST / max_exact)
                         * (REL_BUCKETS - max_exact)).astype(jnp.int32)
    large = jnp.minimum(large, REL_BUCKETS - 1)
    return jnp.where(dist < max_exact, dist, large)


def _toeplitz_bias(rel_bias, g, dil):
    tq = WIN_J
    offs = dil * (WIN_J - jnp.arange(WIN_J + 1, dtype=jnp.int32))
    hs = slice(g * HEADS_PER_GROUP, (g + 1) * HEADS_PER_GROUP)
    tab_rev = rel_bias[_rel_bucket(offs)][:, hs].T.astype(F32)
    period = 3 * tq
    neg = lambda w: jnp.full((HEADS_PER_GROUP, w), NEG, F32)
    vec = jnp.concatenate([neg(tq - 1), tab_rev, neg(period - 2 * tq)], axis=1)
    flat = jnp.broadcast_to(vec[:, None, :], (HEADS_PER_GROUP, tq, period)).reshape(HEADS_PER_GROUP, -1)
    skew = flat[:, :tq * (period - 1)].reshape(HEADS_PER_GROUP, tq, period - 1)
    return skew[:, :, tq - 1:3 * tq - 1]


def _dilated(pa0, pa1, pa2, bias, batch, seq):
    tq = WIN_J
    qkv3 = 3 * GROUP_W
    (_, d1), (_, d2) = DIL_GROUPS[1], DIL_GROUPS[2]
    nsb = seq // SUPER
    p0 = pa0.reshape(batch, seq, qkv3)
    prev_blk = lambda per_sb: (lambda b, s: jnp.maximum(s * per_sb - 1, 0))
    h0i, h1i, h2i = prev_blk(SUPER // tq), prev_blk(SUPER // d1 // tq), prev_blk(SUPER // d2 // tq)
    out = pl.pallas_call(
        _dilated_kernel,
        grid=(batch, nsb),
        in_specs=[
            pl.BlockSpec((1, SUPER, qkv3), lambda b, s: (b, s, 0)),
            pl.BlockSpec((1, tq, qkv3), lambda b, s: (b, h0i(b, s), 0)),
            pl.BlockSpec((1, d1, SUPER // d1, qkv3), lambda b, s: (b, 0, s, 0)),
            pl.BlockSpec((1, d1, tq, qkv3), lambda b, s: (b, 0, h1i(b, s), 0)),
            pl.BlockSpec((1, d2, SUPER // d2, qkv3), lambda b, s: (b, 0, s, 0)),
            pl.BlockSpec((1, d2, tq, qkv3), lambda b, s: (b, 0, h2i(b, s), 0)),
            pl.BlockSpec(bias.shape, lambda b, s: (0, 0, 0, 0)),
        ],
        out_specs=pl.BlockSpec((1, SUPER, GROUP_W), lambda b, s: (b, s, 0)),
        out_shape=jax.ShapeDtypeStruct((batch, seq, GROUP_W), BF16),
        scratch_shapes=[pltpu.VMEM((GROUP_W // LANES, SUPER, LANES), F32)] * 2,
        compiler_params=pltpu.CompilerParams(
            dimension_semantics=("arbitrary", "arbitrary"), vmem_limit_bytes=VMEM_LIMIT),
        name="dilated",
    )(p0, p0, pa1, pa1, pa2, pa2, bias)
    return out.reshape(batch * seq, GROUP_W)


def _fox_kernel(nlive_sm, online_sm, *refs):
    @pl.when(online_sm[0] == 0)
    def _():
        _fox_body(nlive_sm, *refs, online=False)

    @pl.when(online_sm[0] != 0)
    def _():
        _fox_body(nlive_sm, *refs, online=True)


def _fox_body(nlive_sm, q_ref, k_ref, v_ref, o_ref, m_ref, acc_ref, *, online):
    tq = q_ref.shape[1]
    half = tq // 2
    qi = pl.program_id(2)
    step = (pl.program_id(0) * pl.num_programs(1) + pl.program_id(1)) * pl.num_programs(2) + qi
    row = lax.broadcasted_iota(jnp.int32, (half, half), 0)
    col = lax.broadcasted_iota(jnp.int32, (half, half), 1)
    causal = row >= col

    def attend(hh, rows, state, start, nkeys, masked):
        m, acc = state
        lanes = slice(hh * LANES, (hh + 1) * LANES)
        s = _dot_nt(q_ref[0, rows, lanes], k_ref[0, pl.ds(start, nkeys), lanes])
        if masked:
            s = jnp.where(causal, s, NEG)
        if online:
            m_new = jnp.maximum(m, jnp.max(s, axis=-1, keepdims=True))
            acc = acc * jnp.exp(m - m_new)
            s = s - m_new
            m = m_new
        return m, acc + _dot(jnp.exp(s).astype(BF16), v_ref[0, pl.ds(start, nkeys), lanes])

    first = [qi - nlive_sm[2 * step + hh] for hh in range(2)]
    for hh in range(2):
        m_ref[hh] = jnp.full((tq, 1), NEG, F32)
        acc_ref[hh] = jnp.zeros((tq, LANES), F32)

    def full_chunk(ki, carry):
        start = pl.multiple_of(ki * tq, tq)
        for hh in range(2):
            @pl.when(ki >= first[hh])
            def _(hh=hh):
                m, acc = attend(hh, slice(None), (m_ref[hh], acc_ref[hh]), start, tq, False)
                acc_ref[hh] = acc
                if online:
                    m_ref[hh] = m
        return carry

    lax.fori_loop(jnp.minimum(first[0], first[1]), qi, full_chunk, 0)

    d0 = pl.multiple_of(qi * tq, tq)
    outs = []
    for hh in range(2):
        m, acc = m_ref[hh], acc_ref[hh]
        top, bot = slice(0, half), slice(half, tq)
        s_top = attend(hh, top, (m[top], acc[top]), d0, half, True)
        s_bot = attend(hh, bot, (m[bot], acc[bot]), d0, half, False)
        s_bot = attend(hh, bot, s_bot, d0 + half, half, True)
        a = jnp.concatenate([s_top[1], s_bot[1]], axis=0)
        outs.append(a[:, :HEAD_DIM] / a[:, HEAD_DIM:HEAD_DIM + 1])
    o_ref[0] = jnp.concatenate(outs, axis=-1).astype(o_ref.dtype)


def _fox_live_chunks(cb, top, batch, seq):
    tps = seq // TM_IN
    per = TQ_FOX // TM_IN
    nq = seq // TQ_FOX
    c_first = cb[:, 0, :N_HEADS_B].reshape(batch, tps, N_HEADS_B)[:, ::per]
    c_last = cb[:, 1, :N_HEADS_B].reshape(batch, tps, N_HEADS_B)[:, per - 1::per]
    live = (top + c_first[:, :, None, :] - c_last[:, None, :, :]) >= FOX_DEAD_EXPONENT
    back = jnp.arange(nq)[:, None] - jnp.arange(nq)[None, :]
    reach = jnp.max(jnp.where(jnp.logical_and(live, (back > 0)[None, :, :, None]),
                              back[None, :, :, None], 0), axis=2)
    reach = reach.reshape(batch, nq, N_HEADS_B // 2, 2).transpose(0, 2, 1, 3)
    return reach.reshape(-1).astype(jnp.int32)


def _fox(qkb, vb, cb, top, online, batch, seq):
    tq = TQ_FOX
    pairs = N_HEADS_B // 2
    nq = seq // tq
    qkv = qkb.reshape(batch, seq, 2 * N_HEADS_B * LANES)
    vv = vb.reshape(batch, seq, N_HEADS_B * LANES)
    all_chunks = jnp.tile(jnp.repeat(jnp.arange(nq, dtype=jnp.int32), 2), batch * pairs)

    nlive = jnp.where(online, all_chunks, _fox_live_chunks(cb, top, batch, seq))
    grid_spec = pltpu.PrefetchScalarGridSpec(
        num_scalar_prefetch=2,
        grid=(batch, pairs, nq),
        in_specs=[
            pl.BlockSpec((1, tq, 2 * LANES), lambda b, p, i, n, o: (b, i, p)),
            pl.BlockSpec((1, seq, 2 * LANES), lambda b, p, i, n, o: (b, 0, pairs + p)),
            pl.BlockSpec((1, seq, 2 * LANES), lambda b, p, i, n, o: (b, 0, p)),
        ],
        out_specs=pl.BlockSpec((1, tq, LANES), lambda b, p, i, n, o: (b, i, p)),
        scratch_shapes=[pltpu.VMEM((2, tq, 1), F32), pltpu.VMEM((2, tq, LANES), F32)],
    )
    out = pl.pallas_call(
        _fox_kernel,
        grid_spec=grid_spec,
        out_shape=jax.ShapeDtypeStruct((batch, seq, OUT_B), BF16),
        compiler_params=pltpu.CompilerParams(
            dimension_semantics=("arbitrary", "arbitrary", "arbitrary"), vmem_limit_bytes=VMEM_LIMIT),
        name="fox",
    )(nlive, online.astype(jnp.int32).reshape(1), qkv, qkv, vv)
    return out.reshape(batch * seq, OUT_B)


def _fox_shift(q_norm_b, k_norm_b):
    bound = HEAD_DIM * (HEAD_DIM ** -0.5) * jnp.max(jnp.abs(q_norm_b)) * jnp.max(jnp.abs(k_norm_b))
    shift = jnp.maximum(2.0 * FOX_ROUNDING_SLACK * bound - FOX_EXP_HEADROOM, 0.0).astype(F32)
    top = 2.0 * FOX_ROUNDING_SLACK * bound - shift
    return shift, top.astype(F32), shift > FOX_MAX_SHIFT


def _post_kernel(x_ref, ya_ref, yb_ref, gmix_ref, wg_ref, bg_ref, wpa_ref, wpb_ref, wo_ref,
                 gffn_ref, wr_ref, rb_ref, x1_ref, h2_ref, topi_ref, topw_ref):
    d = x_ref.shape[1]
    x = x_ref[...]
    h = (x * lax.rsqrt(jnp.mean(x * x, axis=-1, keepdims=True) + EPS) * gmix_ref[...]).astype(BF16)
    gates = jax.nn.sigmoid(_dot(h, wg_ref[...]) + bg_ref[...])
    merged = gates[:, :d] * _dot(ya_ref[...], wpa_ref[...]) + gates[:, d:] * _dot(yb_ref[...], wpb_ref[...])
    x1 = x + _dot(merged.astype(BF16), wo_ref[...])
    x1_ref[...] = x1
    h2 = x1 * lax.rsqrt(jnp.mean(x1 * x1, axis=-1, keepdims=True) + EPS) * gffn_ref[...]
    h2_ref[...] = h2.astype(BF16)

    hh, hm, _ = _split3(h2)
    wr = wr_ref[...]
    wh = wr.astype(BF16)
    wl = (wr - wh.astype(F32)).astype(BF16)
    hh, hm = hh.astype(BF16), hm.astype(BF16)
    logits = _dot_nt(wh, hh) + _dot_nt(wh, hm) + _dot_nt(wl, hh)
    scores = jax.nn.sigmoid(logits)
    biased = scores + rb_ref[...]
    eid = lax.broadcasted_iota(jnp.int32, scores.shape, 0).astype(F32)
    chosen = jnp.zeros(scores.shape, jnp.bool_)
    idx, val = [], []
    for _ in range(TOP_K):
        cur = jnp.where(chosen, -jnp.inf, biased)
        mx = jnp.max(cur, axis=0, keepdims=True)
        first = jnp.min(jnp.where(cur == mx, eid, float(N_EXPERTS)), axis=0, keepdims=True)
        pick = eid == first
        chosen = jnp.logical_or(chosen, pick)
        idx.append(first)
        val.append(jnp.sum(jnp.where(pick, scores, 0.0), axis=0, keepdims=True))
    top_s = jnp.concatenate(val, axis=0)
    top_w = top_s / jnp.sum(top_s, axis=0, keepdims=True) * ROUTE_SCALE
    tm = scores.shape[1]
    both = jnp.concatenate(idx + [top_w, jnp.zeros((LANES - 2 * TOP_K, tm), F32)], axis=0).T
    topi_ref[...] = both[:, :TOP_K].astype(jnp.int32)
    topw_ref[...] = both[:, TOP_K:2 * TOP_K]


def _post(xf, ya, yb, g_mix, w_gate, b_gate, w_proj_a, w_proj_b, w_out, g_ffn, w_router, router_bias):
    n, d = xf.shape
    tm = TM_POST
    const = lambda shape: pl.BlockSpec(shape, lambda i: (0,) * len(shape))
    row = lambda w: pl.BlockSpec((tm, w), lambda i: (i, 0))
    args = [xf, ya, yb, g_mix.reshape(1, d), w_gate.astype(BF16), b_gate.reshape(1, 2 * d),
            w_proj_a.astype(BF16), w_proj_b.astype(BF16), w_out.astype(BF16), g_ffn.reshape(1, d),
            w_router.astype(F32).T, router_bias.astype(F32).reshape(N_EXPERTS, 1)]
    in_specs = [row(d), row(OUT_A), row(OUT_B)] + [const(a.shape) for a in args[3:]]
    return pl.pallas_call(
        _post_kernel,
        grid=(n // tm,),
        in_specs=in_specs,
        out_specs=[row(d), row(d), row(TOP_K), row(TOP_K)],
        out_shape=[jax.ShapeDtypeStruct((n, d), F32), jax.ShapeDtypeStruct((n, d), BF16),
                   jax.ShapeDtypeStruct((n, TOP_K), jnp.int32), jax.ShapeDtypeStruct((n, TOP_K), F32)],
        compiler_params=pltpu.CompilerParams(
            dimension_semantics=("arbitrary",), vmem_limit_bytes=VMEM_LIMIT),
        name="post",
    )(*args)


def _dispatch_kernel(h2_ref, topi_ref, tri_ref, upper_ref, xs_ref, slots_ref, cnt_ref, off_ref):
    tb = h2_ref.shape[0]
    topi = topi_ref[...]
    lane = lax.broadcasted_iota(jnp.int32, (tb, N_EXPERTS), 1)
    picks = [lane == topi[:, k:k + 1] for k in range(TOP_K)]
    mask = picks[0]
    for pk in picks[1:]:
        mask = jnp.logical_or(mask, pk)
    maskf = jnp.where(mask, 1.0, 0.0)
    rank = _dot(tri_ref[...], maskf.astype(BF16))
    cnt = jnp.sum(maskf, axis=0, keepdims=True)
    gran = jnp.floor((cnt + (GRAN - 1)) * (1.0 / GRAN))
    goff = _dot(jnp.broadcast_to(gran, (8, N_EXPERTS)).astype(BF16), upper_ref[...])[0:1]
    off = goff * GRAN
    slot_te = off + rank
    slots = jnp.concatenate(
        [jnp.sum(jnp.where(pk, slot_te, 0.0), axis=-1, keepdims=True) for pk in picks], axis=1)
    slots_ref[...] = slots.astype(jnp.int32)
    cnt_ref[0] = cnt.astype(jnp.int32)
    off_ref[0] = off.astype(jnp.int32)
    v = jnp.where(mask, slot_te, float(NO_SLOT))
    v_hi = jnp.floor(v * (1.0 / SLOT_RADIX))
    w = jnp.concatenate([v_hi * SLOT_RADIX, v - v_hi * SLOT_RADIX], axis=1).T.astype(BF16)
    end = off + gran * GRAN
    used = jnp.max(end).astype(jnp.int32)
    h2 = h2_ref[...]

    def lookup(first, rows):
        s_e = (lax.broadcasted_iota(jnp.int32, (rows, N_EXPERTS), 0) + first).astype(F32)
        own = jnp.where(jnp.logical_and(s_e >= off, s_e < end), 1.0, 0.0)
        return _dot(jnp.concatenate([own, own], axis=1).astype(BF16), w)

    def sort_chunk(c, looked):
        s_t = (lax.broadcasted_iota(jnp.int32, (SLOT_CHUNK, tb), 0) + c * SLOT_CHUNK).astype(F32)
        onehot = jnp.where(looked == s_t, 1.0, 0.0).astype(BF16)
        xs_ref[0, c * SLOT_CHUNK:(c + 1) * SLOT_CHUNK, :] = _dot(onehot, h2).astype(BF16)

    looked_typ = lookup(0, CHUNKS_TYPICAL * SLOT_CHUNK)
    for c in range(CAP // SLOT_CHUNK):
        if c < CHUNKS_TYPICAL:
            sort_chunk(c, looked_typ[c * SLOT_CHUNK:(c + 1) * SLOT_CHUNK, :])
        else:
            @pl.when(c * SLOT_CHUNK < used)
            def _(c=c):
                sort_chunk(c, lookup(c * SLOT_CHUNK, SLOT_CHUNK))

            @pl.when(c * SLOT_CHUNK >= used)
            def _(c=c):
                xs_ref[0, c * SLOT_CHUNK:(c + 1) * SLOT_CHUNK, :] = jnp.zeros((SLOT_CHUNK, xs_ref.shape[2]), BF16)


def _dispatch(h2, topi):
    n, d = h2.shape
    nb = n // TB
    tri = jnp.asarray(np.tril(np.ones((TB, TB), np.float32), -1), BF16)
    upper = jnp.asarray(np.triu(np.ones((N_EXPERTS, N_EXPERTS), np.float32), 1), BF16)
    const = lambda shape: pl.BlockSpec(shape, lambda i: (0,) * len(shape))
    meta = pl.BlockSpec((1, 1, N_EXPERTS), lambda i: (i, 0, 0))
    return pl.pallas_call(
        _dispatch_kernel,
        grid=(nb,),
        in_specs=[pl.BlockSpec((TB, d), lambda i: (i, 0)), pl.BlockSpec((TB, TOP_K), lambda i: (i, 0)),
                  const(tri.shape), const(upper.shape)],
        out_specs=[pl.BlockSpec((1, CAP, d), lambda i: (i, 0, 0)),
                   pl.BlockSpec((TB, TOP_K), lambda i: (i, 0)), meta, meta],
        out_shape=[jax.ShapeDtypeStruct((nb, CAP, d), BF16), jax.ShapeDtypeStruct((n, TOP_K), jnp.int32),
                   jax.ShapeDtypeStruct((nb, 1, N_EXPERTS), jnp.int32),
                   jax.ShapeDtypeStruct((nb, 1, N_EXPERTS), jnp.int32)],
        compiler_params=pltpu.CompilerParams(
            dimension_semantics=("arbitrary",), vmem_limit_bytes=VMEM_LIMIT),
        name="dispatch",
    )(h2, topi, tri, upper)


def _ffn_kernel(item_e_sm, item_d0_sm, item_nd_sm, item_s0_sm, item_ns_sm, dlist_sm, slist_sm,
                xs_hbm, wg_ref, wu_ref, wd_ref, ys_hbm, xbuf, ybuf, wg_bf, wu_bf, wd_bf, sem_in, sem_out):
    step = pl.program_id(0)
    nsteps = pl.num_programs(0)
    buf = step % 2

    def granules(st):
        return 2 * item_nd_sm[st] + item_ns_sm[st]

    def for_copies(st, fn):
        d0, nd = item_d0_sm[st], item_nd_sm[st]
        s0, ns = item_s0_sm[st], item_ns_sm[st]

        def pair(j, carry):
            fn(dlist_sm[d0 + j], 2 * j, 2)
            return carry

        def single(j, carry):
            fn(slist_sm[s0 + j], 2 * nd + j, 1)
            return carry

        lax.fori_loop(0, nd, pair, 0)
        lax.fori_loop(0, ns, single, 0)

    def fetch(b_):
        return lambda src, dst, k: pltpu.make_async_copy(
            xs_hbm.at[pl.ds(src, k)], xbuf.at[b_, pl.ds(dst, k)], sem_in.at[b_])

    def writeback(b_):
        return lambda src, dst, k: pltpu.make_async_copy(
            ybuf.at[b_, pl.ds(dst, k)], ys_hbm.at[pl.ds(src, k)], sem_out.at[b_])

    def start(mk):
        return lambda src, dst, k: mk(src, dst, k).start()

    def wait_all(st, span):
        n = granules(st)
        size = PASS_GRAN
        while size >= 1:
            @pl.when((n & size) != 0)
            def _(size=size):
                span(size).wait()
            size //= 2
        return n

    def fetch_span(b_):
        return lambda k: pltpu.make_async_copy(
            xs_hbm.at[pl.ds(0, k)], xbuf.at[b_, pl.ds(0, k)], sem_in.at[b_])

    def writeback_span(b_):
        return lambda k: pltpu.make_async_copy(
            ybuf.at[b_, pl.ds(0, k)], ys_hbm.at[pl.ds(0, k)], sem_out.at[b_])

    @pl.when(step == 0)
    def _():
        xbuf[...] = jnp.zeros_like(xbuf)
        for_copies(step, start(fetch(0)))

    @pl.when(step + 1 < nsteps)
    def _():
        for_copies(step + 1, start(fetch(1 - buf)))

    ngran = wait_all(step, fetch_span(buf))

    @pl.when(step >= 2)
    def _():
        wait_all(step - 2, writeback_span(buf))

    @pl.when(ngran > 0)
    def _():
        wg_bf[...] = wg_ref[0].astype(BF16)
        wu_bf[...] = wu_ref[0].astype(BF16)
        wd_bf[...] = wd_ref[0].astype(BF16)

    x_cols = xbuf.shape[-1]

    def ffn_rows(base, rows):
        grans = pl.ds(pl.multiple_of(base // GRAN, rows // GRAN), rows // GRAN)
        x = xbuf[buf, grans].reshape(rows, x_cols)
        g = _dot(x, wg_bf[...])
        u = _dot(x, wu_bf[...])
        mid = (g * jax.nn.sigmoid(g) * u).astype(BF16)
        ybuf[buf, grans] = _dot(mid, wd_bf[...]).astype(BF16).reshape(rows // GRAN, GRAN, x_cols)

    nt = (ngran * GRAN + (FT - 1)) // FT
    big = FT_BIG // FT

    def big_tile(i, carry):
        ffn_rows(pl.multiple_of(i * FT_BIG, FT_BIG), FT_BIG)
        return carry

    lax.fori_loop(0, nt // big, big_tile, 0)
    size = big // 2
    while size >= 1:
        @pl.when((nt & size) != 0)
        def _(size=size):
            ffn_rows(pl.multiple_of((nt & ~(2 * size - 1)) * FT, size * FT), size * FT)
        size //= 2

    for_copies(step, start(writeback(buf)))

    @pl.when(step == nsteps - 1)
    def _():
        wait_all(step, writeback_span(buf))

        @pl.when(step >= 1)
        def _():
            wait_all(step - 1, writeback_span(1 - buf))


def _work_items(cnt, off):
    nb = cnt.shape[0]
    nseg = N_EXPERTS * nb
    i32 = jnp.int32
    seg_n = ((cnt.reshape(nb, N_EXPERTS) + (GRAN - 1)) // GRAN).T
    seg_row = ((off.reshape(nb, N_EXPERTS) + jnp.arange(nb, dtype=i32)[:, None] * CAP) // GRAN).T.reshape(-1)
    before = jnp.cumsum(seg_n, axis=1) - seg_n
    pass_id = (before // (PASS_GRAN - TB // GRAN)).reshape(-1)
    seg_n = seg_n.reshape(-1)
    nd, ns = seg_n // 2, seg_n % 2
    gmax = nb * (TB * TOP_K // GRAN + N_EXPERTS)

    d_end = jnp.cumsum(nd)
    d_start = d_end - nd
    prev_end = jnp.concatenate([jnp.full((1,), 2, i32), (seg_row + 2 * nd)[:-1]])
    dlist = jnp.cumsum(jnp.full((gmax // 2,), 2, i32).at[d_start].add(seg_row - prev_end, mode="drop"))
    s_end = jnp.cumsum(ns)
    s_start = s_end - ns
    slist = jnp.zeros((nseg,), i32).at[jnp.where(ns == 1, s_start, nseg)].add(seg_row + 2 * nd, mode="drop")

    seg = jnp.arange(nseg, dtype=i32)
    new_item = jnp.logical_or(seg % nb == 0, pass_id != jnp.concatenate([pass_id[:1], pass_id[:-1]]))
    item_of_seg = jnp.cumsum(new_item.astype(i32)) - 1
    n_items = N_EXPERTS + gmax // (PASS_GRAN - TB // GRAN)
    first_seg = jnp.full((n_items + 1,), nseg, i32).at[
        jnp.where(new_item, item_of_seg, n_items + 1)].add(seg - nseg, mode="drop")
    lo, hi = first_seg[:-1], first_seg[1:]
    d_bound = jnp.concatenate([d_start, d_end[-1:]])
    s_bound = jnp.concatenate([s_start, s_end[-1:]])
    item_e = jnp.minimum(lo // nb, N_EXPERTS - 1)
    return (item_e.astype(i32), d_bound[lo], d_bound[hi] - d_bound[lo], s_bound[lo], s_bound[hi] - s_bound[lo],
            dlist.astype(i32), slist)


def _ffn(xs, cnt, off, wg, wu, wd):
    _, _, d = xs.shape
    items = _work_items(cnt, off)
    per_expert = lambda shape: pl.BlockSpec((1,) + shape, lambda w, ie, *_: (ie[w], 0, 0))
    grid_spec = pltpu.PrefetchScalarGridSpec(
        num_scalar_prefetch=len(items),
        grid=(items[0].shape[0],),
        in_specs=[pl.BlockSpec(memory_space=pl.ANY), per_expert((d, D_EXPERT)), per_expert((d, D_EXPERT)),
                  per_expert((D_EXPERT, d))],
        out_specs=pl.BlockSpec(memory_space=pl.ANY),
        scratch_shapes=[pltpu.VMEM((2, PASS_GRAN, GRAN, d), BF16)] * 2 + [
                        pltpu.VMEM((d, D_EXPERT), BF16), pltpu.VMEM((d, D_EXPERT), BF16),
                        pltpu.VMEM((D_EXPERT, d), BF16),
                        pltpu.SemaphoreType.DMA((2,)), pltpu.SemaphoreType.DMA((2,))],
    )
    return pl.pallas_call(
        _ffn_kernel,
        grid_spec=grid_spec,
        out_shape=jax.ShapeDtypeStruct(xs.shape, xs.dtype),
        input_output_aliases={len(items): 0},
        compiler_params=pltpu.CompilerParams(
            dimension_semantics=("arbitrary",), vmem_limit_bytes=VMEM_LIMIT),
        name="ffn",
    )(*items, xs, wg, wu, wd)


def _combine_kernel(used_sm, tail_sm, x1_ref, h2_ref, *rest):
    n_chunks = CAP // SLOT_CHUNK
    ys_refs = rest[:n_chunks]
    slots_ref, topw_ref, wgus_ref, wds_ref, o_ref = rest[n_chunks:]
    tb = x1_ref.shape[0]
    gu = _dot(h2_ref[...], wgus_ref[...])
    g, u = gu[:, :D_SHARED], gu[:, D_SHARED:]
    acc = x1_ref[...] + _dot((g * jax.nn.sigmoid(g) * u).astype(BF16), wds_ref[...])
    slots = slots_ref[...].astype(F32)
    topw = topw_ref[...]
    used = used_sm[pl.program_id(0)]

    def gather_chunk(c):
        scol = (lax.broadcasted_iota(jnp.int32, (tb, SLOT_CHUNK), 1) + c * SLOT_CHUNK).astype(F32)
        gate = jnp.zeros((tb, SLOT_CHUNK), F32)
        for k in range(TOP_K):
            gate = jnp.where(scol == slots[:, k:k + 1], topw[:, k:k + 1], gate)
        return _dot(gate.astype(BF16), ys_refs[c][0])

    for c in range(CHUNKS_TYPICAL):
        acc = acc + gather_chunk(c)
    o_ref[...] = acc
    for c in range(CHUNKS_TYPICAL, CAP // SLOT_CHUNK):
        @pl.when(c * SLOT_CHUNK < used)
        def _(c=c):
            o_ref[...] += gather_chunk(c)


def _combine(x1, h2, ys, slots, topw, used, wgus, wds):
    n, d = x1.shape
    n_chunks = CAP // SLOT_CHUNK
    const = lambda shape: pl.BlockSpec(shape, lambda i, u, t: (0,) * len(shape))
    row = lambda w: pl.BlockSpec((TB, w), lambda i, u, t: (i, 0))
    blocks = jnp.arange(n // TB, dtype=jnp.int32)
    tails = [lax.cummax(jnp.where(used > c * SLOT_CHUNK, blocks, 0)) for c in range(CHUNKS_TYPICAL, n_chunks)]
    tail_idx = jnp.stack(tails).astype(jnp.int32)
    n_blocks = n // TB

    def chunk_spec(c):
        if c < CHUNKS_TYPICAL:
            return pl.BlockSpec((1, SLOT_CHUNK, d), lambda i, u, t: (i * n_chunks + c, 0, 0))
        late = c - CHUNKS_TYPICAL
        return pl.BlockSpec((1, SLOT_CHUNK, d), lambda i, u, t: (t[late * n_blocks + i] * n_chunks + c, 0, 0))

    grid_spec = pltpu.PrefetchScalarGridSpec(
        num_scalar_prefetch=2,
        grid=(n_blocks,),
        in_specs=[row(d), row(d)] + [chunk_spec(c) for c in range(n_chunks)] + [
            row(TOP_K), row(TOP_K), const(wgus.shape), const(wds.shape)],
        out_specs=row(d),
    )
    ys_chunks = ys.reshape(n_blocks * n_chunks, SLOT_CHUNK, d)
    return pl.pallas_call(
        _combine_kernel,
        grid_spec=grid_spec,
        out_shape=jax.ShapeDtypeStruct((n, d), F32),
        compiler_params=pltpu.CompilerParams(
            dimension_semantics=("arbitrary",), vmem_limit_bytes=VMEM_LIMIT),
        name="combine",
    )(used, tail_idx.reshape(-1), x1, h2, *([ys_chunks] * n_chunks), slots, topw, wgus, wds)


def _moe(x1, h2, topi, topw, w_gate_e, w_up_e, w_down_e, w_gate_s, w_up_s, w_down_s):
    n, d = x1.shape
    wgus = jnp.concatenate([w_gate_s.astype(BF16), w_up_s.astype(BF16)], axis=-1)
    xs, slots, cnt, off = _dispatch(h2, topi)
    ys = _ffn(xs.reshape(-1, GRAN, d), cnt, off, w_gate_e, w_up_e, w_down_e)
    used = jnp.max(off + (cnt + (GRAN - 1)) // GRAN * GRAN, axis=(1, 2)).astype(jnp.int32)
    return _combine(x1, h2, ys.reshape(n // TB, CAP, d), slots, topw, used, wgus, w_down_s.astype(BF16))


def kernel(x, g_mix, w_in, q_norm_a, k_norm_a, q_norm_b, k_norm_b, rel_bias, b_forget, w_gate, b_gate,
           w_proj_a, w_proj_b, w_out, g_ffn, w_router, router_bias, w_gate_e, w_up_e, w_down_e,
           w_gate_s, w_up_s, w_down_s):
    batch, seq, d = x.shape
    xf = x.reshape(batch * seq, d)
    fox_shift, fox_top, fox_online = _fox_shift(q_norm_b, k_norm_b)
    pa0, pa1, pa2, qkb, vb, cb = _inproj(xf, g_mix, w_in, q_norm_a, k_norm_a, q_norm_b, k_norm_b, b_forget,
                                     fox_shift, seq)
    bias = jnp.stack([_toeplitz_bias(rel_bias, g, dil) for g, (_, dil) in enumerate(DIL_GROUPS)])
    ya = _dilated(pa0, pa1, pa2, bias, batch, seq)

    yb = _fox(qkb, vb, cb, fox_top, fox_online, batch, seq)
    x1, h2, topi, topw = _post(xf, ya, yb, g_mix, w_gate, b_gate, w_proj_a, w_proj_b, w_out, g_ffn,
                               w_router, router_bias)
    out = _moe(x1, h2, topi, topw, w_gate_e, w_up_e, w_down_e, w_gate_s, w_up_s, w_down_s)
    return out.reshape(batch, seq, d)
```

```python
import functools
import math

import jax
import jax.numpy as jnp
import numpy as np
from jax import lax
from jax.experimental import pallas as pl
from jax.experimental.pallas import tpu as pltpu

HEAD_DIM = 64
DIL_GROUPS = ((128, 1), (512, 4), (2048, 16))
HEADS_PER_GROUP = 4
N_HEADS_A = HEADS_PER_GROUP * len(DIL_GROUPS)
N_HEADS_B = 8
REL_BUCKETS = 32
REL_MAX_DIST = 2048
N_EXPERTS = 64
TOP_K = 8
D_EXPERT = 256
D_SHARED = 256
ROUTE_SCALE = 2.5
EPS = 1e-6

WIDTH_A = 3 * N_HEADS_A * HEAD_DIM
WIDTH_B = 3 * N_HEADS_B * HEAD_DIM
QK_B = N_HEADS_B * HEAD_DIM
OUT_A = HEADS_PER_GROUP * HEAD_DIM
OUT_B = N_HEADS_B * HEAD_DIM

LANES = 128
GROUP_W = HEADS_PER_GROUP * HEAD_DIM
WIN_J = 128
SUPER = DIL_GROUPS[-1][1] * WIN_J
NEG = -1e30
VMEM_LIMIT = 56 * 1024 * 1024

DIL_UNROLL = 8
TM_IN = 512
TM_POST = 1024
TQ_FOX = 1024
FOX_ROUNDING_SLACK = 1.02
FOX_EXP_HEADROOM = 60.0
FOX_DEAD_EXPONENT = -105.0
FOX_MAX_SHIFT = 80.0
TB = 256
GRAN = 16
CAP = TB * TOP_K + N_EXPERTS * GRAN
SLOT_CHUNK = 512
CHUNKS_TYPICAL = -(-(TB * TOP_K + N_EXPERTS * GRAN // 2) // SLOT_CHUNK)
SLOT_RADIX = 64
NO_SLOT = SLOT_RADIX ** 2 - 1
assert CAP <= NO_SLOT
PASS_GRAN = 256
FT = 256
FT_BIG = 1024

BF16 = jnp.bfloat16
F32 = jnp.float32


def _dot(a, b):
    return jnp.dot(a, b, preferred_element_type=F32)


def _dot_nt(a, b):
    return lax.dot_general(a, b, (((1,), (1,)), ((), ())), preferred_element_type=F32)


def _split3(v):
    hi = v.astype(BF16).astype(F32)
    r = v - hi
    mid = r.astype(BF16).astype(F32)
    lo = (r - mid).astype(BF16).astype(F32)
    return hi, mid, lo


def _inproj_kernel(x_ref, g_ref, w_ref, bd_ref, tri_ref, gain_a_ref, gain_b_ref,
                   bf_ref, shift_ref, pa0_ref, pa1_ref, pa2_ref, qkb_ref, vb_ref, cb_ref, carry_ref, h_ref,
                   wa_ref, wb_ref, wf_ref, *, tiles_per_seq):
    tm = x_ref.shape[0]

    @pl.when(pl.program_id(0) == 0)
    def _():
        qkv_a, qkv_b = N_HEADS_A * HEAD_DIM, N_HEADS_B * HEAD_DIM
        for part in range(3):
            cols = slice(part * GROUP_W, (part + 1) * GROUP_W)
            for g in range(len(DIL_GROUPS)):
                c0 = part * qkv_a + g * GROUP_W
                wa_ref[g, :, cols] = w_ref[:, c0:c0 + GROUP_W].astype(BF16)
            for c in range(QK_B // GROUP_W):
                c0 = WIDTH_A + part * qkv_b + c * GROUP_W
                wb_ref[c, :, cols] = w_ref[:, c0:c0 + GROUP_W].astype(BF16)
        wf = w_ref[:, WIDTH_A + WIDTH_B:]
        wf_ref[...] = jnp.concatenate(
            [wf, jnp.zeros((wf.shape[0], LANES - N_HEADS_B), F32)], axis=1).astype(BF16)

    x = x_ref[...]
    h = x * lax.rsqrt(jnp.mean(x * x, axis=-1, keepdims=True) + EPS) * g_ref[...]
    n_lane_chunks = h_ref.shape[0]
    for c in range(n_lane_chunks):
        h_ref[c] = h[:, c * LANES:(c + 1) * LANES]
    h = h.astype(BF16)
    bd = bd_ref[...]

    def headnorm(p, gain):
        ms = _dot((p * p).astype(BF16), bd)
        return p * lax.rsqrt(ms + EPS) * gain

    for g, (pa_ref, (_, dil)) in enumerate(zip((pa0_ref, pa1_ref, pa2_ref), DIL_GROUPS)):
        rows = tm // dil
        if dil == 1:
            hg = h
        else:
            hg = jnp.concatenate([jnp.concatenate(
                [h_ref[c, pl.ds(r, rows, stride=dil), :] for c in range(n_lane_chunks)], axis=1)
                for r in range(dil)], axis=0).astype(BF16)
        qkv = _dot(hg, wa_ref[g])
        for part in range(3):
            cols = slice(part * GROUP_W, (part + 1) * GROUP_W)
            p = qkv[:, cols]
            if part < 2:
                p = headnorm(p, gain_a_ref[part:part + 1, :])
            p = p.astype(BF16)
            if dil == 1:
                pa_ref[:, cols] = p
            else:
                for r in range(dil):
                    pa_ref[0, r, :, cols] = p[r * rows:(r + 1) * rows, :]

    f = _dot(h, wf_ref[...]) + bf_ref[...]
    logf = jnp.minimum(f, 0.0) - jnp.log1p(jnp.exp(-jnp.abs(f)))
    tri = tri_ref[...]
    lh, lm, ll = _split3(logf)
    cum3 = _dot(tri, jnp.concatenate([lh, lm, ll], axis=1).astype(BF16))
    cum = cum3[:, :LANES] + cum3[:, LANES:2 * LANES] + cum3[:, 2 * LANES:]

    @pl.when(pl.program_id(0) % tiles_per_seq == 0)
    def _():
        carry_ref[...] = jnp.zeros_like(carry_ref)

    cum = cum + carry_ref[0:1, :]
    carry_ref[0:1, :] = cum[tm - 1:tm, :]
    cb_ref[0] = jnp.concatenate([cum[0:1, :], cum[tm - 1:tm, :], jnp.zeros((6, LANES), F32)], axis=0)
    ch, cm, cl = _split3(cum)

    j = lax.broadcasted_iota(jnp.int32, (tm, HEAD_DIM), 1)

    def ext_cols(vals):
        out = jnp.zeros((tm, HEAD_DIM), F32)
        ones = [pos for pos, val in enumerate(vals) if isinstance(val, float)]
        if ones:
            is_one = functools.reduce(jnp.logical_or, [j == pos for pos in ones])
            out = jnp.where(is_one, 1.0, out)
        for pos, val in enumerate(vals):
            if not isinstance(val, float):
                out = jnp.where(j == pos, val, out)
        return out

    for c in range(QK_B // GROUP_W):
        qkv = _dot(h, wb_ref[c])
        pq = headnorm(qkv[:, 0:GROUP_W], gain_b_ref[0:1, :])
        pk = headnorm(qkv[:, GROUP_W:2 * GROUP_W], gain_b_ref[1:2, :])
        pv = qkv[:, 2 * GROUP_W:3 * GROUP_W]
        r = _dot((pq * pk).astype(BF16), bd) * HEAD_DIM + shift_ref[...]
        for hh in range(HEADS_PER_GROUP):
            head = c * HEADS_PER_GROUP + hh
            lanes = slice(hh * HEAD_DIM, (hh + 1) * HEAD_DIM)
            col = lambda a, idx: a[:, idx:idx + 1]
            cs = [col(ch, head), col(cm, head), col(cl, head)]
            ext_q = ext_cols(cs + [1.0] * 3 + [-col(r, hh * HEAD_DIM)])
            ext_k = ext_cols([1.0] * 3 + [-v for v in cs] + [1.0])
            ext_v = ext_cols([1.0])
            for part, (val, ext) in enumerate(((pq, ext_q), (pk, ext_k))):
                o0 = (part * N_HEADS_B + head) * LANES
                qkb_ref[:, o0:o0 + LANES] = jnp.concatenate([val[:, lanes], ext], axis=-1).astype(BF16)
            vb_ref[:, head * LANES:(head + 1) * LANES] = jnp.concatenate(
                [pv[:, lanes], ext_v], axis=-1).astype(BF16)


def _inproj(xf, g_mix, w_in, q_norm_a, k_norm_a, q_norm_b, k_norm_b, b_forget, fox_shift, seq):
    n, d = xf.shape
    tm = TM_IN
    scale = HEAD_DIM ** -0.5
    bfp = jnp.pad(b_forget.astype(F32), (0, LANES - N_HEADS_B)).reshape(1, LANES)
    seg = np.arange(GROUP_W) // HEAD_DIM
    bd = jnp.asarray((seg[:, None] == seg[None, :]).astype(np.float32) / HEAD_DIM, BF16)
    tri = jnp.asarray(np.tril(np.ones((tm, tm), np.float32)), BF16)
    gain_a = jnp.stack([jnp.tile(q_norm_a, HEADS_PER_GROUP) * scale, jnp.tile(k_norm_a, HEADS_PER_GROUP)])
    gain_b = jnp.stack([jnp.tile(q_norm_b, HEADS_PER_GROUP) * scale, jnp.tile(k_norm_b, HEADS_PER_GROUP)])
    const = lambda shape: pl.BlockSpec(shape, lambda i: (0,) * len(shape))
    tps = seq // tm
    batch = n // seq
    qkv3 = 3 * GROUP_W
    (_, d1), (_, d2) = DIL_GROUPS[1], DIL_GROUPS[2]
    return pl.pallas_call(
        functools.partial(_inproj_kernel, tiles_per_seq=tps),
        grid=(n // tm,),
        in_specs=[
            pl.BlockSpec((tm, d), lambda i: (i, 0)),
            const((1, d)), pl.BlockSpec(w_in.shape, lambda i: (0, 0), pipeline_mode=pl.Buffered(1)),
            const(bd.shape), const(tri.shape), const(gain_a.shape), const(gain_b.shape),
            const(bfp.shape), const((1, 1)),
        ],
        out_specs=[
            pl.BlockSpec((tm, qkv3), lambda i: (i, 0)),
            pl.BlockSpec((1, d1, tm // d1, qkv3), lambda i: (i // tps, 0, i % tps, 0)),
            pl.BlockSpec((1, d2, tm // d2, qkv3), lambda i: (i // tps, 0, i % tps, 0)),
            pl.BlockSpec((tm, 2 * N_HEADS_B * LANES), lambda i: (i, 0)),
            pl.BlockSpec((tm, N_HEADS_B * LANES), lambda i: (i, 0)),
            pl.BlockSpec((1, 8, LANES), lambda i: (i, 0, 0)),
        ],
        out_shape=[
            jax.ShapeDtypeStruct((n, qkv3), BF16),
            jax.ShapeDtypeStruct((batch, d1, seq // d1, qkv3), BF16),
            jax.ShapeDtypeStruct((batch, d2, seq // d2, qkv3), BF16),
            jax.ShapeDtypeStruct((n, 2 * N_HEADS_B * LANES), BF16),
            jax.ShapeDtypeStruct((n, N_HEADS_B * LANES), BF16),
            jax.ShapeDtypeStruct((n // tm, 8, LANES), F32),
        ],
        scratch_shapes=[pltpu.VMEM((8, LANES), F32), pltpu.VMEM((d // LANES, tm, LANES), F32),
                        pltpu.VMEM((len(DIL_GROUPS), d, qkv3), BF16),
                        pltpu.VMEM((QK_B // GROUP_W, d, qkv3), BF16), pltpu.VMEM((d, LANES), BF16)],
        compiler_params=pltpu.CompilerParams(
            dimension_semantics=("arbitrary",), vmem_limit_bytes=VMEM_LIMIT),
        name="inproj",
    )(xf, g_mix.reshape(1, d), w_in, bd, tri, gain_a, gain_b, bfp, fox_shift.reshape(1, 1))


def _dilated_kernel(p0_ref, h0_ref, p1_ref, h1_ref, p2_ref, h2_ref, bias_ref, o_ref, acc_ref, lse_ref):
    tq = WIN_J
    first_sb = pl.program_id(1) == 0
    lane_head = lax.broadcasted_iota(jnp.int32, (tq, GROUP_W), 1) // HEAD_DIM
    prev_col = lax.broadcasted_iota(jnp.int32, (tq, 2 * tq), 1) < tq
    qc, kc_, vc_ = (slice(0, GROUP_W), slice(GROUP_W, 2 * GROUP_W), slice(2 * GROUP_W, 3 * GROUP_W))

    def attend(g, q, kp, kc, vp, vc, no_prev):
        kcat = jnp.concatenate([kp, kc], axis=0)
        vcat = jnp.concatenate([vp, vc], axis=0)
        dead = jnp.logical_and(no_prev, prev_col)
        q4 = jnp.concatenate([jnp.where(lane_head == hh, q, jnp.zeros_like(q))
                              for hh in range(HEADS_PER_GROUP)], axis=0)
        s = _dot_nt(q4, kcat) + bias_ref[g].reshape(HEADS_PER_GROUP * tq, 2 * tq)
        s = jnp.where(jnp.concatenate([dead] * HEADS_PER_GROUP, axis=0), NEG, s)
        m = jnp.max(s, axis=-1, keepdims=True)
        p = jnp.exp(s - m)
        l = jnp.sum(p, axis=-1, keepdims=True)
        o4 = _dot(p.astype(BF16), vcat) * (1.0 / l)
        lse4 = m + jnp.log(l)
        acc = o4[0:tq]
        lse = jnp.broadcast_to(lse4[0:tq], (tq, GROUP_W))
        for hh in range(1, HEADS_PER_GROUP):
            sel = lane_head == hh
            acc = jnp.where(sel, o4[hh * tq:(hh + 1) * tq], acc)
            lse = jnp.where(sel, lse4[hh * tq:(hh + 1) * tq], lse)
        return acc, lse

    n_half = GROUP_W // LANES

    def merge(rows, acc, lse):
        for c in range(n_half):
            lanes = slice(c * LANES, (c + 1) * LANES)
            l1 = lse_ref[c, rows, :]
            mx = jnp.maximum(l1, lse[:, lanes])
            w1 = jnp.exp(l1 - mx)
            w2 = jnp.exp(lse[:, lanes] - mx)
            den = w1 + w2
            acc_ref[c, rows, :] = (w1 * acc_ref[c, rows, :] + w2 * acc[:, lanes]) / den
            lse_ref[c, rows, :] = mx + jnp.log(den)

    def pick(first, halo, body):
        return jnp.where(first, halo, body)

    def loop(n, body):
        def trip(i, carry):
            for u in range(DIL_UNROLL):
                body(i * DIL_UNROLL + u, carry)
            return carry
        lax.fori_loop(0, n // DIL_UNROLL, trip, 0)

    def body0(j, carry):
        st = pl.multiple_of(j * tq, tq)
        pst = pl.multiple_of(jnp.maximum(j - 1, 0) * tq, tq)
        cur, prv = pl.ds(st, tq), pl.ds(pst, tq)
        acc, lse = attend(
            0, p0_ref[0, cur, qc],
            pick(j == 0, h0_ref[0, :, kc_], p0_ref[0, prv, kc_]), p0_ref[0, cur, kc_],
            pick(j == 0, h0_ref[0, :, vc_], p0_ref[0, prv, vc_]), p0_ref[0, cur, vc_],
            jnp.logical_and(j == 0, first_sb))
        for c in range(n_half):
            acc_ref[c, cur, :] = acc[:, c * LANES:(c + 1) * LANES]
            lse_ref[c, cur, :] = lse[:, c * LANES:(c + 1) * LANES]
        return carry

    loop(SUPER // tq, body0)

    d1 = DIL_GROUPS[1][1]
    nsub1 = SUPER // d1 // tq
    def body1(t, carry):
        r, ii = t // nsub1, t % nsub1
        st = pl.multiple_of(ii * tq, tq)
        pst = pl.multiple_of(jnp.maximum(ii - 1, 0) * tq, tq)
        cur, prv = pl.ds(st, tq), pl.ds(pst, tq)
        acc, lse = attend(
            1, p1_ref[0, r, cur, qc],
            pick(ii == 0, h1_ref[0, r, :, kc_], p1_ref[0, r, prv, kc_]), p1_ref[0, r, cur, kc_],
            pick(ii == 0, h1_ref[0, r, :, vc_], p1_ref[0, r, prv, vc_]), p1_ref[0, r, cur, vc_],
            jnp.logical_and(ii == 0, first_sb))
        merge(pl.ds(ii * (tq * d1) + r, tq, stride=d1), acc, lse)
        return carry

    loop(d1 * nsub1, body1)

    d2 = DIL_GROUPS[2][1]

    def body2(r, carry):
        acc, lse = attend(2, p2_ref[0, r, :, qc], h2_ref[0, r, :, kc_], p2_ref[0, r, :, kc_],
                          h2_ref[0, r, :, vc_], p2_ref[0, r, :, vc_], first_sb)
        merge(pl.ds(r, tq, stride=d2), acc, lse)
        return carry

    loop(d2, body2)

    for c in range(n_half):
        o_ref[0, :, c * LANES:(c + 1) * LANES] = acc_ref[c].astype(o_ref.dtype)


def _rel_bucket(dist):
    max_exact = REL_BUCKETS // 2
    n = jnp.maximum(dist.astype(F32), 1.0)
    large = max_exact + (jnp.log(n / max_exact) / math.log(REL_MAX_DIST / max_exact)
                         * (REL_BUCKETS - max_exact)).astype(jnp.int32)
    large = jnp.minimum(large, REL_BUCKETS - 1)
    return jnp.where(dist < max_exact, dist, large)


def _toeplitz_bias(rel_bias, g, dil):
    tq = WIN_J
    offs = dil * (WIN_J - jnp.arange(WIN_J + 1, dtype=jnp.int32))
    hs = slice(g * HEADS_PER_GROUP, (g + 1) * HEADS_PER_GROUP)
    tab_rev = rel_bias[_rel_bucket(offs)][:, hs].T.astype(F32)
    period = 3 * tq
    neg = lambda w: jnp.full((HEADS_PER_GROUP, w), NEG, F32)
    vec = jnp.concatenate([neg(tq - 1), tab_rev, neg(period - 2 * tq)], axis=1)
    flat = jnp.broadcast_to(vec[:, None, :], (HEADS_PER_GROUP, tq, period)).reshape(HEADS_PER_GROUP, -1)
    skew = flat[:, :tq * (period - 1)].reshape(HEADS_PER_GROUP, tq, period - 1)
    return skew[:, :, tq - 1:3 * tq - 1]


def _dilated(pa0, pa1, pa2, bias, batch, seq):
    tq = WIN_J
    qkv3 = 3 * GROUP_W
    (_, d1), (_, d2) = DIL_GROUPS[1], DIL_GROUPS[2]
    nsb = seq // SUPER
    p0 = pa0.reshape(batch, seq, qkv3)
    prev_blk = lambda per_sb: (lambda b, s: jnp.maximum(s * per_sb - 1, 0))
    h0i, h1i, h2i = prev_blk(SUPER // tq), prev_blk(SUPER // d1 // tq), prev_blk(SUPER // d2 // tq)
    out = pl.pallas_call(
        _dilated_kernel,
        grid=(batch, nsb),
        in_specs=[
            pl.BlockSpec((1, SUPER, qkv3), lambda b, s: (b, s, 0)),
            pl.BlockSpec((1, tq, qkv3), lambda b, s: (b, h0i(b, s), 0)),
            pl.BlockSpec((1, d1, SUPER // d1, qkv3), lambda b, s: (b, 0, s, 0)),
            pl.BlockSpec((1, d1, tq, qkv3), lambda b, s: (b, 0, h1i(b, s), 0)),
            pl.BlockSpec((1, d2, SUPER // d2, qkv3), lambda b, s: (b, 0, s, 0)),
            pl.BlockSpec((1, d2, tq, qkv3), lambda b, s: (b, 0, h2i(b, s), 0)),
            pl.BlockSpec(bias.shape, lambda b, s: (0, 0, 0, 0)),
        ],
        out_specs=pl.BlockSpec((1, SUPER, GROUP_W), lambda b, s: (b, s, 0)),
        out_shape=jax.ShapeDtypeStruct((batch, seq, GROUP_W), BF16),
        scratch_shapes=[pltpu.VMEM((GROUP_W // LANES, SUPER, LANES), F32)] * 2,
        compiler_params=pltpu.CompilerParams(
            dimension_semantics=("arbitrary", "arbitrary"), vmem_limit_bytes=VMEM_LIMIT),
        name="dilated",
    )(p0, p0, pa1, pa1, pa2, pa2, bias)
    return out.reshape(batch * seq, GROUP_W)


def _fox_kernel(nlive_sm, q_ref, k_ref, v_ref, o_ref, m_ref, acc_ref, *, online):
    tq = q_ref.shape[1]
    half = tq // 2
    qi = pl.program_id(2)
    step = (pl.program_id(0) * pl.num_programs(1) + pl.program_id(1)) * pl.num_programs(2) + qi
    row = lax.broadcasted_iota(jnp.int32, (half, half), 0)
    col = lax.broadcasted_iota(jnp.int32, (half, half), 1)
    causal = row >= col

    def attend(hh, rows, state, start, nkeys, masked):
        m, acc = state
        lanes = slice(hh * LANES, (hh + 1) * LANES)
        s = _dot_nt(q_ref[0, rows, lanes], k_ref[0, pl.ds(start, nkeys), lanes])
        if masked:
            s = jnp.where(causal, s, NEG)
        if online:
            m_new = jnp.maximum(m, jnp.max(s, axis=-1, keepdims=True))
            acc = acc * jnp.exp(m - m_new)
            s = s - m_new
            m = m_new
        return m, acc + _dot(jnp.exp(s).astype(BF16), v_ref[0, pl.ds(start, nkeys), lanes])

    first = [qi - nlive_sm[2 * step + hh] for hh in range(2)]
    for hh in range(2):
        m_ref[hh] = jnp.full((tq, 1), NEG, F32)
        acc_ref[hh] = jnp.zeros((tq, LANES), F32)

    def full_chunk(ki, carry):
        start = pl.multiple_of(ki * tq, tq)
        for hh in range(2):
            @pl.when(ki >= first[hh])
            def _(hh=hh):
                m, acc = attend(hh, slice(None), (m_ref[hh], acc_ref[hh]), start, tq, False)
                acc_ref[hh] = acc
                if online:
                    m_ref[hh] = m
        return carry

    lax.fori_loop(jnp.minimum(first[0], first[1]), qi, full_chunk, 0)

    d0 = pl.multiple_of(qi * tq, tq)
    outs = []
    for hh in range(2):
        m, acc = m_ref[hh], acc_ref[hh]
        top, bot = slice(0, half), slice(half, tq)
        s_top = attend(hh, top, (m[top], acc[top]), d0, half, True)
        s_bot = attend(hh, bot, (m[bot], acc[bot]), d0, half, False)
        s_bot = attend(hh, bot, s_bot, d0 + half, half, True)
        a = jnp.concatenate([s_top[1], s_bot[1]], axis=0)
        outs.append(a[:, :HEAD_DIM] / a[:, HEAD_DIM:HEAD_DIM + 1])
    o_ref[0] = jnp.concatenate(outs, axis=-1).astype(o_ref.dtype)


def _fox_live_chunks(cb, top, batch, seq):
    tps = seq // TM_IN
    per = TQ_FOX // TM_IN
    nq = seq // TQ_FOX
    c_first = cb[:, 0, :N_HEADS_B].reshape(batch, tps, N_HEADS_B)[:, ::per]
    c_last = cb[:, 1, :N_HEADS_B].reshape(batch, tps, N_HEADS_B)[:, per - 1::per]
    live = (top + c_first[:, :, None, :] - c_last[:, None, :, :]) >= FOX_DEAD_EXPONENT
    back = jnp.arange(nq)[:, None] - jnp.arange(nq)[None, :]
    reach = jnp.max(jnp.where(jnp.logical_and(live, (back > 0)[None, :, :, None]),
                              back[None, :, :, None], 0), axis=2)
    reach = reach.reshape(batch, nq, N_HEADS_B // 2, 2).transpose(0, 2, 1, 3)
    return reach.reshape(-1).astype(jnp.int32)


def _fox(qkb, vb, cb, top, online, batch, seq):
    tq = TQ_FOX
    pairs = N_HEADS_B // 2
    nq = seq // tq
    qkv = qkb.reshape(batch, seq, 2 * N_HEADS_B * LANES)
    vv = vb.reshape(batch, seq, N_HEADS_B * LANES)
    all_chunks = jnp.tile(jnp.repeat(jnp.arange(nq, dtype=jnp.int32), 2), batch * pairs)

    def call(is_online, nlive):
        grid_spec = pltpu.PrefetchScalarGridSpec(
            num_scalar_prefetch=1,
            grid=(batch, pairs, nq),
            in_specs=[
                pl.BlockSpec((1, tq, 2 * LANES), lambda b, p, i, n: (b, i, p)),
                pl.BlockSpec((1, seq, 2 * LANES), lambda b, p, i, n: (b, 0, pairs + p)),
                pl.BlockSpec((1, seq, 2 * LANES), lambda b, p, i, n: (b, 0, p)),
            ],
            out_specs=pl.BlockSpec((1, tq, LANES), lambda b, p, i, n: (b, i, p)),
            scratch_shapes=[pltpu.VMEM((2, tq, 1), F32), pltpu.VMEM((2, tq, LANES), F32)],
        )
        return pl.pallas_call(
            functools.partial(_fox_kernel, online=is_online),
            grid_spec=grid_spec,
            out_shape=jax.ShapeDtypeStruct((batch, seq, OUT_B), BF16),
            compiler_params=pltpu.CompilerParams(
                dimension_semantics=("arbitrary", "arbitrary", "arbitrary"), vmem_limit_bytes=VMEM_LIMIT),
            name="fox_online" if is_online else "fox",
        )(nlive, qkv, qkv, vv)

    out = lax.cond(online, lambda: call(True, all_chunks),
                   lambda: call(False, _fox_live_chunks(cb, top, batch, seq)))
    return out.reshape(batch * seq, OUT_B)


def _fox_shift(q_norm_b, k_norm_b):
    bound = HEAD_DIM * (HEAD_DIM ** -0.5) * jnp.max(jnp.abs(q_norm_b)) * jnp.max(jnp.abs(k_norm_b))
    shift = jnp.maximum(2.0 * FOX_ROUNDING_SLACK * bound - FOX_EXP_HEADROOM, 0.0).astype(F32)
    top = 2.0 * FOX_ROUNDING_SLACK * bound - shift
    return shift, top.astype(F32), shift > FOX_MAX_SHIFT


def _post_kernel(x_ref, ya_ref, yb_ref, gmix_ref, wg_ref, bg_ref, wpa_ref, wpb_ref, wo_ref,
                 gffn_ref, wr_ref, rb_ref, x1_ref, h2_ref, topi_ref, topw_ref):
    d = x_ref.shape[1]
    x = x_ref[...]
    h = (x * lax.rsqrt(jnp.mean(x * x, axis=-1, keepdims=True) + EPS) * gmix_ref[...]).astype(BF16)
    gates = jax.nn.sigmoid(_dot(h, wg_ref[...]) + bg_ref[...])
    merged = gates[:, :d] * _dot(ya_ref[...], wpa_ref[...]) + gates[:, d:] * _dot(yb_ref[...], wpb_ref[...])
    x1 = x + _dot(merged.astype(BF16), wo_ref[...])
    x1_ref[...] = x1
    h2 = x1 * lax.rsqrt(jnp.mean(x1 * x1, axis=-1, keepdims=True) + EPS) * gffn_ref[...]
    h2_ref[...] = h2.astype(BF16)

    hh, hm, _ = _split3(h2)
    wr = wr_ref[...]
    wh = wr.astype(BF16)
    wl = (wr - wh.astype(F32)).astype(BF16)
    hh, hm = hh.astype(BF16), hm.astype(BF16)
    logits = _dot_nt(wh, hh) + _dot_nt(wh, hm) + _dot_nt(wl, hh)
    scores = jax.nn.sigmoid(logits)
    biased = scores + rb_ref[...]
    eid = lax.broadcasted_iota(jnp.int32, scores.shape, 0).astype(F32)
    chosen = jnp.zeros(scores.shape, jnp.bool_)
    idx, val = [], []
    for _ in range(TOP_K):
        cur = jnp.where(chosen, -jnp.inf, biased)
        mx = jnp.max(cur, axis=0, keepdims=True)
        first = jnp.min(jnp.where(cur == mx, eid, float(N_EXPERTS)), axis=0, keepdims=True)
        pick = eid == first
        chosen = jnp.logical_or(chosen, pick)
        idx.append(first)
        val.append(jnp.sum(jnp.where(pick, scores, 0.0), axis=0, keepdims=True))
    top_s = jnp.concatenate(val, axis=0)
    top_w = top_s / jnp.sum(top_s, axis=0, keepdims=True) * ROUTE_SCALE
    tm = scores.shape[1]
    both = jnp.concatenate(idx + [top_w, jnp.zeros((LANES - 2 * TOP_K, tm), F32)], axis=0).T
    topi_ref[...] = both[:, :TOP_K].astype(jnp.int32)
    topw_ref[...] = both[:, TOP_K:2 * TOP_K]


def _post(xf, ya, yb, g_mix, w_gate, b_gate, w_proj_a, w_proj_b, w_out, g_ffn, w_router, router_bias):
    n, d = xf.shape
    tm = TM_POST
    const = lambda shape: pl.BlockSpec(shape, lambda i: (0,) * len(shape))
    row = lambda w: pl.BlockSpec((tm, w), lambda i: (i, 0))
    args = [xf, ya, yb, g_mix.reshape(1, d), w_gate.astype(BF16), b_gate.reshape(1, 2 * d),
            w_proj_a.astype(BF16), w_proj_b.astype(BF16), w_out.astype(BF16), g_ffn.reshape(1, d),
            w_router.astype(F32).T, router_bias.astype(F32).reshape(N_EXPERTS, 1)]
    in_specs = [row(d), row(OUT_A), row(OUT_B)] + [const(a.shape) for a in args[3:]]
    return pl.pallas_call(
        _post_kernel,
        grid=(n // tm,),
        in_specs=in_specs,
        out_specs=[row(d), row(d), row(TOP_K), row(TOP_K)],
        out_shape=[jax.ShapeDtypeStruct((n, d), F32), jax.ShapeDtypeStruct((n, d), BF16),
                   jax.ShapeDtypeStruct((n, TOP_K), jnp.int32), jax.ShapeDtypeStruct((n, TOP_K), F32)],
        compiler_params=pltpu.CompilerParams(
            dimension_semantics=("arbitrary",), vmem_limit_bytes=VMEM_LIMIT),
        name="post",
    )(*args)


def _dispatch_kernel(h2_ref, topi_ref, tri_ref, upper_ref, xs_ref, slots_ref, cnt_ref, off_ref):
    tb = h2_ref.shape[0]
    topi = topi_ref[...]
    lane = lax.broadcasted_iota(jnp.int32, (tb, N_EXPERTS), 1)
    picks = [lane == topi[:, k:k + 1] for k in range(TOP_K)]
    mask = picks[0]
    for pk in picks[1:]:
        mask = jnp.logical_or(mask, pk)
    maskf = jnp.where(mask, 1.0, 0.0)
    rank = _dot(tri_ref[...], maskf.astype(BF16))
    cnt = jnp.sum(maskf, axis=0, keepdims=True)
    gran = jnp.floor((cnt + (GRAN - 1)) * (1.0 / GRAN))
    goff = _dot(jnp.broadcast_to(gran, (8, N_EXPERTS)).astype(BF16), upper_ref[...])[0:1]
    off = goff * GRAN
    slot_te = off + rank
    slots = jnp.concatenate(
        [jnp.sum(jnp.where(pk, slot_te, 0.0), axis=-1, keepdims=True) for pk in picks], axis=1)
    slots_ref[...] = slots.astype(jnp.int32)
    cnt_ref[0] = cnt.astype(jnp.int32)
    off_ref[0] = off.astype(jnp.int32)
    v = jnp.where(mask, slot_te, float(NO_SLOT))
    v_hi = jnp.floor(v * (1.0 / SLOT_RADIX))
    w = jnp.concatenate([v_hi * SLOT_RADIX, v - v_hi * SLOT_RADIX], axis=1).T.astype(BF16)
    end = off + gran * GRAN
    used = jnp.max(end).astype(jnp.int32)
    h2 = h2_ref[...]

    def lookup(first, rows):
        s_e = (lax.broadcasted_iota(jnp.int32, (rows, N_EXPERTS), 0) + first).astype(F32)
        own = jnp.where(jnp.logical_and(s_e >= off, s_e < end), 1.0, 0.0)
        return _dot(jnp.concatenate([own, own], axis=1).astype(BF16), w)

    def sort_chunk(c, looked):
        s_t = (lax.broadcasted_iota(jnp.int32, (SLOT_CHUNK, tb), 0) + c * SLOT_CHUNK).astype(F32)
        onehot = jnp.where(looked == s_t, 1.0, 0.0).astype(BF16)
        xs_ref[0, c * SLOT_CHUNK:(c + 1) * SLOT_CHUNK, :] = _dot(onehot, h2).astype(BF16)

    looked_typ = lookup(0, CHUNKS_TYPICAL * SLOT_CHUNK)
    for c in range(CAP // SLOT_CHUNK):
        if c < CHUNKS_TYPICAL:
            sort_chunk(c, looked_typ[c * SLOT_CHUNK:(c + 1) * SLOT_CHUNK, :])
        else:
            @pl.when(c * SLOT_CHUNK < used)
            def _(c=c):
                sort_chunk(c, lookup(c * SLOT_CHUNK, SLOT_CHUNK))

            @pl.when(c * SLOT_CHUNK >= used)
            def _(c=c):
                xs_ref[0, c * SLOT_CHUNK:(c + 1) * SLOT_CHUNK, :] = jnp.zeros((SLOT_CHUNK, xs_ref.shape[2]), BF16)


def _dispatch(h2, topi):
    n, d = h2.shape
    nb = n // TB
    tri = jnp.asarray(np.tril(np.ones((TB, TB), np.float32), -1), BF16)
    upper = jnp.asarray(np.triu(np.ones((N_EXPERTS, N_EXPERTS), np.float32), 1), BF16)
    const = lambda shape: pl.BlockSpec(shape, lambda i: (0,) * len(shape))
    meta = pl.BlockSpec((1, 1, N_EXPERTS), lambda i: (i, 0, 0))
    return pl.pallas_call(
        _dispatch_kernel,
        grid=(nb,),
        in_specs=[pl.BlockSpec((TB, d), lambda i: (i, 0)), pl.BlockSpec((TB, TOP_K), lambda i: (i, 0)),
                  const(tri.shape), const(upper.shape)],
        out_specs=[pl.BlockSpec((1, CAP, d), lambda i: (i, 0, 0)),
                   pl.BlockSpec((TB, TOP_K), lambda i: (i, 0)), meta, meta],
        out_shape=[jax.ShapeDtypeStruct((nb, CAP, d), BF16), jax.ShapeDtypeStruct((n, TOP_K), jnp.int32),
                   jax.ShapeDtypeStruct((nb, 1, N_EXPERTS), jnp.int32),
                   jax.ShapeDtypeStruct((nb, 1, N_EXPERTS), jnp.int32)],
        compiler_params=pltpu.CompilerParams(
            dimension_semantics=("arbitrary",), vmem_limit_bytes=VMEM_LIMIT),
        name="dispatch",
    )(h2, topi, tri, upper)


def _ffn_kernel(item_e_sm, item_d0_sm, item_nd_sm, item_s0_sm, item_ns_sm, dlist_sm, slist_sm,
                xs_hbm, wg_ref, wu_ref, wd_ref, ys_hbm, xbuf, ybuf, wg_bf, wu_bf, wd_bf, sem_in, sem_out):
    step = pl.program_id(0)
    nsteps = pl.num_programs(0)
    buf = step % 2

    def granules(st):
        return 2 * item_nd_sm[st] + item_ns_sm[st]

    def for_copies(st, fn):
        d0, nd = item_d0_sm[st], item_nd_sm[st]
        s0, ns = item_s0_sm[st], item_ns_sm[st]

        def pair(j, carry):
            fn(dlist_sm[d0 + j], 2 * j, 2)
            return carry

        def single(j, carry):
            fn(slist_sm[s0 + j], 2 * nd + j, 1)
            return carry

        lax.fori_loop(0, nd, pair, 0)
        lax.fori_loop(0, ns, single, 0)

    def fetch(b_):
        return lambda src, dst, k: pltpu.make_async_copy(
            xs_hbm.at[pl.ds(src, k)], xbuf.at[b_, pl.ds(dst, k)], sem_in.at[b_])

    def writeback(b_):
        return lambda src, dst, k: pltpu.make_async_copy(
            ybuf.at[b_, pl.ds(dst, k)], ys_hbm.at[pl.ds(src, k)], sem_out.at[b_])

    def start(mk):
        return lambda src, dst, k: mk(src, dst, k).start()

    def wait_all(st, span):
        n = granules(st)
        size = PASS_GRAN
        while size >= 1:
            @pl.when((n & size) != 0)
            def _(size=size):
                span(size).wait()
            size //= 2
        return n

    def fetch_span(b_):
        return lambda k: pltpu.make_async_copy(
            xs_hbm.at[pl.ds(0, k)], xbuf.at[b_, pl.ds(0, k)], sem_in.at[b_])

    def writeback_span(b_):
        return lambda k: pltpu.make_async_copy(
            ybuf.at[b_, pl.ds(0, k)], ys_hbm.at[pl.ds(0, k)], sem_out.at[b_])

    @pl.when(step == 0)
    def _():
        xbuf[...] = jnp.zeros_like(xbuf)
        for_copies(step, start(fetch(0)))

    @pl.when(step + 1 < nsteps)
    def _():
        for_copies(step + 1, start(fetch(1 - buf)))

    ngran = wait_all(step, fetch_span(buf))

    @pl.when(step >= 2)
    def _():
        wait_all(step - 2, writeback_span(buf))

    @pl.when(ngran > 0)
    def _():
        wg_bf[...] = wg_ref[0].astype(BF16)
        wu_bf[...] = wu_ref[0].astype(BF16)
        wd_bf[...] = wd_ref[0].astype(BF16)

    x_cols = xbuf.shape[-1]

    def ffn_rows(base, rows):
        grans = pl.ds(pl.multiple_of(base // GRAN, rows // GRAN), rows // GRAN)
        x = xbuf[buf, grans].reshape(rows, x_cols)
        g = _dot(x, wg_bf[...])
        u = _dot(x, wu_bf[...])
        mid = (g * jax.nn.sigmoid(g) * u).astype(BF16)
        ybuf[buf, grans] = _dot(mid, wd_bf[...]).astype(BF16).reshape(rows // GRAN, GRAN, x_cols)

    nt = (ngran * GRAN + (FT - 1)) // FT
    big = FT_BIG // FT

    def big_tile(i, carry):
        ffn_rows(pl.multiple_of(i * FT_BIG, FT_BIG), FT_BIG)
        return carry

    lax.fori_loop(0, nt // big, big_tile, 0)
    size = big // 2
    while size >= 1:
        @pl.when((nt & size) != 0)
        def _(size=size):
            ffn_rows(pl.multiple_of((nt & ~(2 * size - 1)) * FT, size * FT), size * FT)
        size //= 2

    for_copies(step, start(writeback(buf)))

    @pl.when(step == nsteps - 1)
    def _():
        wait_all(step, writeback_span(buf))

        @pl.when(step >= 1)
        def _():
            wait_all(step - 1, writeback_span(1 - buf))


def _work_items(cnt, off):
    nb = cnt.shape[0]
    nseg = N_EXPERTS * nb
    i32 = jnp.int32
    seg_n = ((cnt.reshape(nb, N_EXPERTS) + (GRAN - 1)) // GRAN).T
    seg_row = ((off.reshape(nb, N_EXPERTS) + jnp.arange(nb, dtype=i32)[:, None] * CAP) // GRAN).T.reshape(-1)
    before = jnp.cumsum(seg_n, axis=1) - seg_n
    pass_id = (before // (PASS_GRAN - TB // GRAN)).reshape(-1)
    seg_n = seg_n.reshape(-1)
    nd, ns = seg_n // 2, seg_n % 2
    gmax = nb * (TB * TOP_K // GRAN + N_EXPERTS)

    d_end = jnp.cumsum(nd)
    d_start = d_end - nd
    prev_end = jnp.concatenate([jnp.full((1,), 2, i32), (seg_row + 2 * nd)[:-1]])
    dlist = jnp.cumsum(jnp.full((gmax // 2,), 2, i32).at[d_start].add(seg_row - prev_end, mode="drop"))
    s_end = jnp.cumsum(ns)
    s_start = s_end - ns
    slist = jnp.zeros((nseg,), i32).at[jnp.where(ns == 1, s_start, nseg)].add(seg_row + 2 * nd, mode="drop")

    seg = jnp.arange(nseg, dtype=i32)
    new_item = jnp.logical_or(seg % nb == 0, pass_id != jnp.concatenate([pass_id[:1], pass_id[:-1]]))
    item_of_seg = jnp.cumsum(new_item.astype(i32)) - 1
    n_items = N_EXPERTS + gmax // (PASS_GRAN - TB // GRAN)
    first_seg = jnp.full((n_items + 1,), nseg, i32).at[
        jnp.where(new_item, item_of_seg, n_items + 1)].add(seg - nseg, mode="drop")
    lo, hi = first_seg[:-1], first_seg[1:]
    d_bound = jnp.concatenate([d_start, d_end[-1:]])
    s_bound = jnp.concatenate([s_start, s_end[-1:]])
    item_e = jnp.minimum(lo // nb, N_EXPERTS - 1)
    return (item_e.astype(i32), d_bound[lo], d_bound[hi] - d_bound[lo], s_bound[lo], s_bound[hi] - s_bound[lo],
            dlist.astype(i32), slist)


def _ffn(xs, cnt, off, wg, wu, wd):
    _, _, d = xs.shape
    items = _work_items(cnt, off)
    per_expert = lambda shape: pl.BlockSpec((1,) + shape, lambda w, ie, *_: (ie[w], 0, 0))
    grid_spec = pltpu.PrefetchScalarGridSpec(
        num_scalar_prefetch=len(items),
        grid=(items[0].shape[0],),
        in_specs=[pl.BlockSpec(memory_space=pl.ANY), per_expert((d, D_EXPERT)), per_expert((d, D_EXPERT)),
                  per_expert((D_EXPERT, d))],
        out_specs=pl.BlockSpec(memory_space=pl.ANY),
        scratch_shapes=[pltpu.VMEM((2, PASS_GRAN, GRAN, d), BF16)] * 2 + [
                        pltpu.VMEM((d, D_EXPERT), BF16), pltpu.VMEM((d, D_EXPERT), BF16),
                        pltpu.VMEM((D_EXPERT, d), BF16),
                        pltpu.SemaphoreType.DMA((2,)), pltpu.SemaphoreType.DMA((2,))],
    )
    return pl.pallas_call(
        _ffn_kernel,
        grid_spec=grid_spec,
        out_shape=jax.ShapeDtypeStruct(xs.shape, xs.dtype),
        input_output_aliases={len(items): 0},
        compiler_params=pltpu.CompilerParams(
            dimension_semantics=("arbitrary",), vmem_limit_bytes=VMEM_LIMIT),
        name="ffn",
    )(*items, xs, wg, wu, wd)


def _combine_kernel(used_sm, tail_sm, x1_ref, h2_ref, *rest):
    n_chunks = CAP // SLOT_CHUNK
    ys_refs = rest[:n_chunks]
    slots_ref, topw_ref, wgus_ref, wds_ref, o_ref = rest[n_chunks:]
    tb = x1_ref.shape[0]
    gu = _dot(h2_ref[...], wgus_ref[...])
    g, u = gu[:, :D_SHARED], gu[:, D_SHARED:]
    acc = x1_ref[...] + _dot((g * jax.nn.sigmoid(g) * u).astype(BF16), wds_ref[...])
    slots = slots_ref[...].astype(F32)
    topw = topw_ref[...]
    used = used_sm[pl.program_id(0)]

    def gather_chunk(c):
        scol = (lax.broadcasted_iota(jnp.int32, (tb, SLOT_CHUNK), 1) + c * SLOT_CHUNK).astype(F32)
        gate = jnp.zeros((tb, SLOT_CHUNK), F32)
        for k in range(TOP_K):
            gate = jnp.where(scol == slots[:, k:k + 1], topw[:, k:k + 1], gate)
        return _dot(gate.astype(BF16), ys_refs[c][0])

    for c in range(CHUNKS_TYPICAL):
        acc = acc + gather_chunk(c)
    o_ref[...] = acc
    for c in range(CHUNKS_TYPICAL, CAP // SLOT_CHUNK):
        @pl.when(c * SLOT_CHUNK < used)
        def _(c=c):
            o_ref[...] += gather_chunk(c)


def _combine(x1, h2, ys, slots, topw, used, wgus, wds):
    n, d = x1.shape
    n_chunks = CAP // SLOT_CHUNK
    const = lambda shape: pl.BlockSpec(shape, lambda i, u, t: (0,) * len(shape))
    row = lambda w: pl.BlockSpec((TB, w), lambda i, u, t: (i, 0))
    blocks = jnp.arange(n // TB, dtype=jnp.int32)
    tails = [lax.cummax(jnp.where(used > c * SLOT_CHUNK, blocks, 0)) for c in range(CHUNKS_TYPICAL, n_chunks)]
    tail_idx = jnp.stack(tails).astype(jnp.int32)
    n_blocks = n // TB

    def chunk_spec(c):
        if c < CHUNKS_TYPICAL:
            return pl.BlockSpec((1, SLOT_CHUNK, d), lambda i, u, t: (i * n_chunks + c, 0, 0))
        late = c - CHUNKS_TYPICAL
        return pl.BlockSpec((1, SLOT_CHUNK, d), lambda i, u, t: (t[late * n_blocks + i] * n_chunks + c, 0, 0))

    grid_spec = pltpu.PrefetchScalarGridSpec(
        num_scalar_prefetch=2,
        grid=(n_blocks,),
        in_specs=[row(d), row(d)] + [chunk_spec(c) for c in range(n_chunks)] + [
            row(TOP_K), row(TOP_K), const(wgus.shape), const(wds.shape)],
        out_specs=row(d),
    )
    ys_chunks = ys.reshape(n_blocks * n_chunks, SLOT_CHUNK, d)
    return pl.pallas_call(
        _combine_kernel,
        grid_spec=grid_spec,
        out_shape=jax.ShapeDtypeStruct((n, d), F32),
        compiler_params=pltpu.CompilerParams(
            dimension_semantics=("arbitrary",), vmem_limit_bytes=VMEM_LIMIT),
        name="combine",
    )(used, tail_idx.reshape(-1), x1, h2, *([ys_chunks] * n_chunks), slots, topw, wgus, wds)


def _moe(x1, h2, topi, topw, w_gate_e, w_up_e, w_down_e, w_gate_s, w_up_s, w_down_s):
    n, d = x1.shape
    wgus = jnp.concatenate([w_gate_s.astype(BF16), w_up_s.astype(BF16)], axis=-1)
    xs, slots, cnt, off = _dispatch(h2, topi)
    ys = _ffn(xs.reshape(-1, GRAN, d), cnt, off, w_gate_e, w_up_e, w_down_e)
    used = jnp.max(off + (cnt + (GRAN - 1)) // GRAN * GRAN, axis=(1, 2)).astype(jnp.int32)
    return _combine(x1, h2, ys.reshape(n // TB, CAP, d), slots, topw, used, wgus, w_down_s.astype(BF16))


def kernel(x, g_mix, w_in, q_norm_a, k_norm_a, q_norm_b, k_norm_b, rel_bias, b_forget, w_gate, b_gate,
           w_proj_a, w_proj_b, w_out, g_ffn, w_router, router_bias, w_gate_e, w_up_e, w_down_e,
           w_gate_s, w_up_s, w_down_s):
    batch, seq, d = x.shape
    xf = x.reshape(batch * seq, d)
    fox_shift, fox_top, fox_online = _fox_shift(q_norm_b, k_norm_b)
    pa0, pa1, pa2, qkb, vb, cb = _inproj(xf, g_mix, w_in, q_norm_a, k_norm_a, q_norm_b, k_norm_b, b_forget,
                                     fox_shift, seq)
    bias = jnp.stack([_toeplitz_bias(rel_bias, g, dil) for g, (_, dil) in enumerate(DIL_GROUPS)])
    ya = _dilated(pa0, pa1, pa2, bias, batch, seq)

    yb = _fox(qkb, vb, cb, fox_top, fox_online, batch, seq)
    x1, h2, topi, topw = _post(xf, ya, yb, g_mix, w_gate, b_gate, w_proj_a, w_proj_b, w_out, g_ffn,
                               w_router, router_bias)
    out = _moe(x1, h2, topi, topw, w_gate_e, w_up_e, w_down_e, w_gate_s, w_up_s, w_down_s)
    return out.reshape(batch, seq, d)
```

```python
import functools
import math

import jax
import jax.numpy as jnp
import numpy as np
from jax import lax
from jax.experimental import pallas as pl
from jax.experimental.pallas import tpu as pltpu

HEAD_DIM = 64
DIL_GROUPS = ((128, 1), (512, 4), (2048, 16))
HEADS_PER_GROUP = 4
N_HEADS_A = HEADS_PER_GROUP * len(DIL_GROUPS)
N_HEADS_B = 8
REL_BUCKETS = 32
REL_MAX_DIST = 2048
N_EXPERTS = 64
TOP_K = 8
D_EXPERT = 256
D_SHARED = 256
ROUTE_SCALE = 2.5
EPS = 1e-6

WIDTH_A = 3 * N_HEADS_A * HEAD_DIM
WIDTH_B = 3 * N_HEADS_B * HEAD_DIM
QK_B = N_HEADS_B * HEAD_DIM
OUT_A = HEADS_PER_GROUP * HEAD_DIM
OUT_B = N_HEADS_B * HEAD_DIM

LANES = 128
GROUP_W = HEADS_PER_GROUP * HEAD_DIM
WIN_J = 128
SUPER = DIL_GROUPS[-1][1] * WIN_J
NEG = -1e30
VMEM_LIMIT = 56 * 1024 * 1024

DIL_UNROLL = 8
TM_IN = 512
TM_POST = 1024
TQ_FOX = 1024
FOX_ROUNDING_SLACK = 1.02
FOX_EXP_HEADROOM = 60.0
FOX_DEAD_EXPONENT = -105.0
FOX_MAX_SHIFT = 80.0
TB = 256
GRAN = 16
CAP = TB * TOP_K + N_EXPERTS * GRAN
SLOT_CHUNK = 512
CHUNKS_TYPICAL = -(-(TB * TOP_K + N_EXPERTS * GRAN // 2) // SLOT_CHUNK)
SLOT_RADIX = 64
NO_SLOT = SLOT_RADIX ** 2 - 1
assert CAP <= NO_SLOT
PASS_GRAN = 256
FT = 256
FT_BIG = 1024

BF16 = jnp.bfloat16
F32 = jnp.float32


def _dot(a, b):
    return jnp.dot(a, b, preferred_element_type=F32)


def _dot_nt(a, b):
    return lax.dot_general(a, b, (((1,), (1,)), ((), ())), preferred_element_type=F32)


def _split3(v):
    hi = v.astype(BF16).astype(F32)
    r = v - hi
    mid = r.astype(BF16).astype(F32)
    lo = (r - mid).astype(BF16).astype(F32)
    return hi, mid, lo


def _inproj_kernel(x_ref, g_ref, wa_ref, wb_ref, wf_ref, bd_ref, tri_ref, gain_a_ref, gain_b_ref,
                   bf_ref, shift_ref, pa0_ref, pa1_ref, pa2_ref, qkb_ref, vb_ref, cb_ref, carry_ref, h_ref, *,
                   tiles_per_seq):
    tm = x_ref.shape[0]
    x = x_ref[...]
    h = x * lax.rsqrt(jnp.mean(x * x, axis=-1, keepdims=True) + EPS) * g_ref[...]
    n_lane_chunks = h_ref.shape[0]
    for c in range(n_lane_chunks):
        h_ref[c] = h[:, c * LANES:(c + 1) * LANES]
    h = h.astype(BF16)
    bd = bd_ref[...]

    def headnorm(p, gain):
        ms = _dot((p * p).astype(BF16), bd)
        return p * lax.rsqrt(ms + EPS) * gain

    for g, (pa_ref, (_, dil)) in enumerate(zip((pa0_ref, pa1_ref, pa2_ref), DIL_GROUPS)):
        rows = tm // dil
        if dil == 1:
            hg = h
        else:
            hg = jnp.concatenate([jnp.concatenate(
                [h_ref[c, pl.ds(r, rows, stride=dil), :] for c in range(n_lane_chunks)], axis=1)
                for r in range(dil)], axis=0).astype(BF16)
        qkv = _dot(hg, wa_ref[g])
        for part in range(3):
            cols = slice(part * GROUP_W, (part + 1) * GROUP_W)
            p = qkv[:, cols]
            if part < 2:
                p = headnorm(p, gain_a_ref[part:part + 1, :])
            p = p.astype(BF16)
            if dil == 1:
                pa_ref[:, cols] = p
            else:
                for r in range(dil):
                    pa_ref[0, r, :, cols] = p[r * rows:(r + 1) * rows, :]

    f = _dot(h, wf_ref[...]) + bf_ref[...]
    logf = jnp.minimum(f, 0.0) - jnp.log1p(jnp.exp(-jnp.abs(f)))
    tri = tri_ref[...]
    lh, lm, ll = _split3(logf)
    cum3 = _dot(tri, jnp.concatenate([lh, lm, ll], axis=1).astype(BF16))
    cum = cum3[:, :LANES] + cum3[:, LANES:2 * LANES] + cum3[:, 2 * LANES:]

    @pl.when(pl.program_id(0) % tiles_per_seq == 0)
    def _():
        carry_ref[...] = jnp.zeros_like(carry_ref)

    cum = cum + carry_ref[0:1, :]
    carry_ref[0:1, :] = cum[tm - 1:tm, :]
    cb_ref[0] = jnp.concatenate([cum[0:1, :], cum[tm - 1:tm, :], jnp.zeros((6, LANES), F32)], axis=0)
    ch, cm, cl = _split3(cum)

    j = lax.broadcasted_iota(jnp.int32, (tm, HEAD_DIM), 1)

    def ext_cols(vals):
        out = jnp.zeros((tm, HEAD_DIM), F32)
        ones = [pos for pos, val in enumerate(vals) if isinstance(val, float)]
        if ones:
            is_one = functools.reduce(jnp.logical_or, [j == pos for pos in ones])
            out = jnp.where(is_one, 1.0, out)
        for pos, val in enumerate(vals):
            if not isinstance(val, float):
                out = jnp.where(j == pos, val, out)
        return out

    for c in range(QK_B // GROUP_W):
        qkv = _dot(h, wb_ref[c])
        pq = headnorm(qkv[:, 0:GROUP_W], gain_b_ref[0:1, :])
        pk = headnorm(qkv[:, GROUP_W:2 * GROUP_W], gain_b_ref[1:2, :])
        pv = qkv[:, 2 * GROUP_W:3 * GROUP_W]
        r = _dot((pq * pk).astype(BF16), bd) * HEAD_DIM + shift_ref[...]
        for hh in range(HEADS_PER_GROUP):
            head = c * HEADS_PER_GROUP + hh
            lanes = slice(hh * HEAD_DIM, (hh + 1) * HEAD_DIM)
            col = lambda a, idx: a[:, idx:idx + 1]
            cs = [col(ch, head), col(cm, head), col(cl, head)]
            ext_q = ext_cols(cs + [1.0] * 3 + [-col(r, hh * HEAD_DIM)])
            ext_k = ext_cols([1.0] * 3 + [-v for v in cs] + [1.0])
            ext_v = ext_cols([1.0])
            for part, (val, ext) in enumerate(((pq, ext_q), (pk, ext_k))):
                o0 = (part * N_HEADS_B + head) * LANES
                qkb_ref[:, o0:o0 + LANES] = jnp.concatenate([val[:, lanes], ext], axis=-1).astype(BF16)
            vb_ref[:, head * LANES:(head + 1) * LANES] = jnp.concatenate(
                [pv[:, lanes], ext_v], axis=-1).astype(BF16)


def _inproj(xf, g_mix, w_in, q_norm_a, k_norm_a, q_norm_b, k_norm_b, b_forget, fox_shift, seq):
    n, d = xf.shape
    tm = TM_IN
    scale = HEAD_DIM ** -0.5
    w_bf = w_in.astype(BF16)
    qkv_w = N_HEADS_A * HEAD_DIM
    wa = jnp.stack([jnp.concatenate(
        [w_bf[:, part * qkv_w + g * GROUP_W: part * qkv_w + (g + 1) * GROUP_W] for part in range(3)],
        axis=1) for g in range(len(DIL_GROUPS))])
    wb = jnp.stack([jnp.concatenate(
        [w_bf[:, WIDTH_A + part * QK_B + c * GROUP_W: WIDTH_A + part * QK_B + (c + 1) * GROUP_W]
         for part in range(3)], axis=1) for c in range(QK_B // GROUP_W)])
    wf = jnp.pad(w_bf[:, WIDTH_A + WIDTH_B:], ((0, 0), (0, LANES - N_HEADS_B)))
    bfp = jnp.pad(b_forget.astype(F32), (0, LANES - N_HEADS_B)).reshape(1, LANES)
    seg = np.arange(GROUP_W) // HEAD_DIM
    bd = jnp.asarray((seg[:, None] == seg[None, :]).astype(np.float32) / HEAD_DIM, BF16)
    tri = jnp.asarray(np.tril(np.ones((tm, tm), np.float32)), BF16)
    gain_a = jnp.stack([jnp.tile(q_norm_a, HEADS_PER_GROUP) * scale, jnp.tile(k_norm_a, HEADS_PER_GROUP)])
    gain_b = jnp.stack([jnp.tile(q_norm_b, HEADS_PER_GROUP) * scale, jnp.tile(k_norm_b, HEADS_PER_GROUP)])
    const = lambda shape: pl.BlockSpec(shape, lambda i: (0,) * len(shape))
    tps = seq // tm
    batch = n // seq
    qkv3 = 3 * GROUP_W
    (_, d1), (_, d2) = DIL_GROUPS[1], DIL_GROUPS[2]
    return pl.pallas_call(
        functools.partial(_inproj_kernel, tiles_per_seq=tps),
        grid=(n // tm,),
        in_specs=[
            pl.BlockSpec((tm, d), lambda i: (i, 0)),
            const((1, d)), const(wa.shape), const(wb.shape), const(wf.shape),
            const(bd.shape), const(tri.shape), const(gain_a.shape), const(gain_b.shape),
            const(bfp.shape), const((1, 1)),
        ],
        out_specs=[
            pl.BlockSpec((tm, qkv3), lambda i: (i, 0)),
            pl.BlockSpec((1, d1, tm // d1, qkv3), lambda i: (i // tps, 0, i % tps, 0)),
            pl.BlockSpec((1, d2, tm // d2, qkv3), lambda i: (i // tps, 0, i % tps, 0)),
            pl.BlockSpec((tm, 2 * N_HEADS_B * LANES), lambda i: (i, 0)),
            pl.BlockSpec((tm, N_HEADS_B * LANES), lambda i: (i, 0)),
            pl.BlockSpec((1, 8, LANES), lambda i: (i, 0, 0)),
        ],
        out_shape=[
            jax.ShapeDtypeStruct((n, qkv3), BF16),
            jax.ShapeDtypeStruct((batch, d1, seq // d1, qkv3), BF16),
            jax.ShapeDtypeStruct((batch, d2, seq // d2, qkv3), BF16),
            jax.ShapeDtypeStruct((n, 2 * N_HEADS_B * LANES), BF16),
            jax.ShapeDtypeStruct((n, N_HEADS_B * LANES), BF16),
            jax.ShapeDtypeStruct((n // tm, 8, LANES), F32),
        ],
        scratch_shapes=[pltpu.VMEM((8, LANES), F32), pltpu.VMEM((d // LANES, tm, LANES), F32)],
        compiler_params=pltpu.CompilerParams(
            dimension_semantics=("arbitrary",), vmem_limit_bytes=VMEM_LIMIT),
        name="inproj",
    )(xf, g_mix.reshape(1, d), wa, wb, wf, bd, tri, gain_a, gain_b, bfp, fox_shift.reshape(1, 1))


def _dilated_kernel(p0_ref, h0_ref, p1_ref, h1_ref, p2_ref, h2_ref, bias_ref, o_ref, acc_ref, lse_ref):
    tq = WIN_J
    first_sb = pl.program_id(1) == 0
    lane_head = lax.broadcasted_iota(jnp.int32, (tq, GROUP_W), 1) // HEAD_DIM
    prev_col = lax.broadcasted_iota(jnp.int32, (tq, 2 * tq), 1) < tq
    qc, kc_, vc_ = (slice(0, GROUP_W), slice(GROUP_W, 2 * GROUP_W), slice(2 * GROUP_W, 3 * GROUP_W))

    def attend(g, q, kp, kc, vp, vc, no_prev):
        kcat = jnp.concatenate([kp, kc], axis=0)
        vcat = jnp.concatenate([vp, vc], axis=0)
        dead = jnp.logical_and(no_prev, prev_col)
        q4 = jnp.concatenate([jnp.where(lane_head == hh, q, jnp.zeros_like(q))
                              for hh in range(HEADS_PER_GROUP)], axis=0)
        s = _dot_nt(q4, kcat) + bias_ref[g].reshape(HEADS_PER_GROUP * tq, 2 * tq)
        s = jnp.where(jnp.concatenate([dead] * HEADS_PER_GROUP, axis=0), NEG, s)
        m = jnp.max(s, axis=-1, keepdims=True)
        p = jnp.exp(s - m)
        l = jnp.sum(p, axis=-1, keepdims=True)
        o4 = _dot(p.astype(BF16), vcat) * (1.0 / l)
        lse4 = m + jnp.log(l)
        acc = o4[0:tq]
        lse = jnp.broadcast_to(lse4[0:tq], (tq, GROUP_W))
        for hh in range(1, HEADS_PER_GROUP):
            sel = lane_head == hh
            acc = jnp.where(sel, o4[hh * tq:(hh + 1) * tq], acc)
            lse = jnp.where(sel, lse4[hh * tq:(hh + 1) * tq], lse)
        return acc, lse

    n_half = GROUP_W // LANES

    def merge(rows, acc, lse):
        for c in range(n_half):
            lanes = slice(c * LANES, (c + 1) * LANES)
            l1 = lse_ref[c, rows, :]
            mx = jnp.maximum(l1, lse[:, lanes])
            w1 = jnp.exp(l1 - mx)
            w2 = jnp.exp(lse[:, lanes] - mx)
            den = w1 + w2
            acc_ref[c, rows, :] = (w1 * acc_ref[c, rows, :] + w2 * acc[:, lanes]) / den
            lse_ref[c, rows, :] = mx + jnp.log(den)

    def pick(first, halo, body):
        return jnp.where(first, halo, body)

    def loop(n, body):
        def trip(i, carry):
            for u in range(DIL_UNROLL):
                body(i * DIL_UNROLL + u, carry)
            return carry
        lax.fori_loop(0, n // DIL_UNROLL, trip, 0)

    def body0(j, carry):
        st = pl.multiple_of(j * tq, tq)
        pst = pl.multiple_of(jnp.maximum(j - 1, 0) * tq, tq)
        cur, prv = pl.ds(st, tq), pl.ds(pst, tq)
        acc, lse = attend(
            0, p0_ref[0, cur, qc],
            pick(j == 0, h0_ref[0, :, kc_], p0_ref[0, prv, kc_]), p0_ref[0, cur, kc_],
            pick(j == 0, h0_ref[0, :, vc_], p0_ref[0, prv, vc_]), p0_ref[0, cur, vc_],
            jnp.logical_and(j == 0, first_sb))
        for c in range(n_half):
            acc_ref[c, cur, :] = acc[:, c * LANES:(c + 1) * LANES]
            lse_ref[c, cur, :] = lse[:, c * LANES:(c + 1) * LANES]
        return carry

    loop(SUPER // tq, body0)

    d1 = DIL_GROUPS[1][1]
    nsub1 = SUPER // d1 // tq
    def body1(t, carry):
        r, ii = t // nsub1, t % nsub1
        st = pl.multiple_of(ii * tq, tq)
        pst = pl.multiple_of(jnp.maximum(ii - 1, 0) * tq, tq)
        cur, prv = pl.ds(st, tq), pl.ds(pst, tq)
        acc, lse = attend(
            1, p1_ref[0, r, cur, qc],
            pick(ii == 0, h1_ref[0, r, :, kc_], p1_ref[0, r, prv, kc_]), p1_ref[0, r, cur, kc_],
            pick(ii == 0, h1_ref[0, r, :, vc_], p1_ref[0, r, prv, vc_]), p1_ref[0, r, cur, vc_],
            jnp.logical_and(ii == 0, first_sb))
        merge(pl.ds(ii * (tq * d1) + r, tq, stride=d1), acc, lse)
        return carry

    loop(d1 * nsub1, body1)

    d2 = DIL_GROUPS[2][1]

    def body2(r, carry):
        acc, lse = attend(2, p2_ref[0, r, :, qc], h2_ref[0, r, :, kc_], p2_ref[0, r, :, kc_],
                          h2_ref[0, r, :, vc_], p2_ref[0, r, :, vc_], first_sb)
        merge(pl.ds(r, tq, stride=d2), acc, lse)
        return carry

    loop(d2, body2)

    for c in range(n_half):
        o_ref[0, :, c * LANES:(c + 1) * LANES] = acc_ref[c].astype(o_ref.dtype)


def _rel_bucket(dist):
    max_exact = REL_BUCKETS // 2
    n = jnp.maximum(dist.astype(F32), 1.0)
    large = max_exact + (jnp.log(n / max_exact) / math.log(REL_MAX_DIST / max_exact)
                         * (REL_BUCKETS - max_exact)).astype(jnp.int32)
    large = jnp.minimum(large, REL_BUCKETS - 1)
    return jnp.where(dist < max_exact, dist, large)


def _toeplitz_bias(rel_bias):
    tq = WIN_J
    ng = len(DIL_GROUPS)
    dils = jnp.asarray([dil for _, dil in DIL_GROUPS], jnp.int32)
    offs = dils[:, None] * (WIN_J - jnp.arange(WIN_J + 1, dtype=jnp.int32))[None, :]
    by_group = rel_bias.reshape(REL_BUCKETS, ng, HEADS_PER_GROUP).astype(F32)
    tab_rev = by_group[_rel_bucket(offs), jnp.arange(ng)[:, None]]
    tab_rev = tab_rev.transpose(0, 2, 1)
    period = 3 * tq
    lead = (ng, HEADS_PER_GROUP)
    neg = lambda w: jnp.full(lead + (w,), NEG, F32)
    vec = jnp.concatenate([neg(tq - 1), tab_rev, neg(period - 2 * tq)], axis=-1)
    flat = jnp.broadcast_to(vec[:, :, None, :], lead + (tq, period)).reshape(lead + (-1,))
    skew = flat[..., :tq * (period - 1)].reshape(lead + (tq, period - 1))
    return skew[..., tq - 1:3 * tq - 1]


def _dilated(pa0, pa1, pa2, bias, batch, seq):
    tq = WIN_J
    qkv3 = 3 * GROUP_W
    (_, d1), (_, d2) = DIL_GROUPS[1], DIL_GROUPS[2]
    nsb = seq // SUPER
    p0 = pa0.reshape(batch, seq, qkv3)
    prev_blk = lambda per_sb: (lambda b, s: jnp.maximum(s * per_sb - 1, 0))
    h0i, h1i, h2i = prev_blk(SUPER // tq), prev_blk(SUPER // d1 // tq), prev_blk(SUPER // d2 // tq)
    out = pl.pallas_call(
        _dilated_kernel,
        grid=(batch, nsb),
        in_specs=[
            pl.BlockSpec((1, SUPER, qkv3), lambda b, s: (b, s, 0)),
            pl.BlockSpec((1, tq, qkv3), lambda b, s: (b, h0i(b, s), 0)),
            pl.BlockSpec((1, d1, SUPER // d1, qkv3), lambda b, s: (b, 0, s, 0)),
            pl.BlockSpec((1, d1, tq, qkv3), lambda b, s: (b, 0, h1i(b, s), 0)),
            pl.BlockSpec((1, d2, SUPER // d2, qkv3), lambda b, s: (b, 0, s, 0)),
            pl.BlockSpec((1, d2, tq, qkv3), lambda b, s: (b, 0, h2i(b, s), 0)),
            pl.BlockSpec(bias.shape, lambda b, s: (0, 0, 0, 0)),
        ],
        out_specs=pl.BlockSpec((1, SUPER, GROUP_W), lambda b, s: (b, s, 0)),
        out_shape=jax.ShapeDtypeStruct((batch, seq, GROUP_W), BF16),
        scratch_shapes=[pltpu.VMEM((GROUP_W // LANES, SUPER, LANES), F32)] * 2,
        compiler_params=pltpu.CompilerParams(
            dimension_semantics=("arbitrary", "arbitrary"), vmem_limit_bytes=VMEM_LIMIT),
        name="dilated",
    )(p0, p0, pa1, pa1, pa2, pa2, bias)
    return out.reshape(batch * seq, GROUP_W)


def _fox_kernel(nlive_sm, q_ref, k_ref, v_ref, o_ref, m_ref, acc_ref, *, online):
    tq = q_ref.shape[1]
    half = tq // 2
    qi = pl.program_id(2)
    step = (pl.program_id(0) * pl.num_programs(1) + pl.program_id(1)) * pl.num_programs(2) + qi
    row = lax.broadcasted_iota(jnp.int32, (half, half), 0)
    col = lax.broadcasted_iota(jnp.int32, (half, half), 1)
    causal = row >= col

    def attend(hh, rows, state, start, nkeys, masked):
        m, acc = state
        lanes = slice(hh * LANES, (hh + 1) * LANES)
        s = _dot_nt(q_ref[0, rows, lanes], k_ref[0, pl.ds(start, nkeys), lanes])
        if masked:
            s = jnp.where(causal, s, NEG)
        if online:
            m_new = jnp.maximum(m, jnp.max(s, axis=-1, keepdims=True))
            acc = acc * jnp.exp(m - m_new)
            s = s - m_new
            m = m_new
        return m, acc + _dot(jnp.exp(s).astype(BF16), v_ref[0, pl.ds(start, nkeys), lanes])

    first = [qi - nlive_sm[2 * step + hh] for hh in range(2)]
    for hh in range(2):
        m_ref[hh] = jnp.full((tq, 1), NEG, F32)
        acc_ref[hh] = jnp.zeros((tq, LANES), F32)

    def full_chunk(ki, carry):
        start = pl.multiple_of(ki * tq, tq)
        for hh in range(2):
            @pl.when(ki >= first[hh])
            def _(hh=hh):
                m, acc = attend(hh, slice(None), (m_ref[hh], acc_ref[hh]), start, tq, False)
                acc_ref[hh] = acc
                if online:
                    m_ref[hh] = m
        return carry

    lax.fori_loop(jnp.minimum(first[0], first[1]), qi, full_chunk, 0)

    d0 = pl.multiple_of(qi * tq, tq)
    outs = []
    for hh in range(2):
        m, acc = m_ref[hh], acc_ref[hh]
        top, bot = slice(0, half), slice(half, tq)
        s_top = attend(hh, top, (m[top], acc[top]), d0, half, True)
        s_bot = attend(hh, bot, (m[bot], acc[bot]), d0, half, False)
        s_bot = attend(hh, bot, s_bot, d0 + half, half, True)
        a = jnp.concatenate([s_top[1], s_bot[1]], axis=0)
        outs.append(a[:, :HEAD_DIM] / a[:, HEAD_DIM:HEAD_DIM + 1])
    o_ref[0] = jnp.concatenate(outs, axis=-1).astype(o_ref.dtype)


def _fox_live_chunks(cb, top, batch, seq):
    tps = seq // TM_IN
    per = TQ_FOX // TM_IN
    nq = seq // TQ_FOX
    c_first = cb[:, 0, :N_HEADS_B].reshape(batch, tps, N_HEADS_B)[:, ::per]
    c_last = cb[:, 1, :N_HEADS_B].reshape(batch, tps, N_HEADS_B)[:, per - 1::per]
    live = (top + c_first[:, :, None, :] - c_last[:, None, :, :]) >= FOX_DEAD_EXPONENT
    back = jnp.arange(nq)[:, None] - jnp.arange(nq)[None, :]
    reach = jnp.max(jnp.where(jnp.logical_and(live, (back > 0)[None, :, :, None]),
                              back[None, :, :, None], 0), axis=2)
    reach = reach.reshape(batch, nq, N_HEADS_B // 2, 2).transpose(0, 2, 1, 3)
    return reach.reshape(-1).astype(jnp.int32)


def _fox(qkb, vb, cb, top, online, batch, seq):
    tq = TQ_FOX
    pairs = N_HEADS_B // 2
    nq = seq // tq
    qkv = qkb.reshape(batch, seq, 2 * N_HEADS_B * LANES)
    vv = vb.reshape(batch, seq, N_HEADS_B * LANES)
    all_chunks = jnp.tile(jnp.repeat(jnp.arange(nq, dtype=jnp.int32), 2), batch * pairs)

    def call(is_online, nlive):
        grid_spec = pltpu.PrefetchScalarGridSpec(
            num_scalar_prefetch=1,
            grid=(batch, pairs, nq),
            in_specs=[
                pl.BlockSpec((1, tq, 2 * LANES), lambda b, p, i, n: (b, i, p)),
                pl.BlockSpec((1, seq, 2 * LANES), lambda b, p, i, n: (b, 0, pairs + p)),
                pl.BlockSpec((1, seq, 2 * LANES), lambda b, p, i, n: (b, 0, p)),
            ],
            out_specs=pl.BlockSpec((1, tq, LANES), lambda b, p, i, n: (b, i, p)),
            scratch_shapes=[pltpu.VMEM((2, tq, 1), F32), pltpu.VMEM((2, tq, LANES), F32)],
        )
        return pl.pallas_call(
            functools.partial(_fox_kernel, online=is_online),
            grid_spec=grid_spec,
            out_shape=jax.ShapeDtypeStruct((batch, seq, OUT_B), BF16),
            compiler_params=pltpu.CompilerParams(
                dimension_semantics=("arbitrary", "arbitrary", "arbitrary"), vmem_limit_bytes=VMEM_LIMIT),
            name="fox_online" if is_online else "fox",
        )(nlive, qkv, qkv, vv)

    out = lax.cond(online, lambda: call(True, all_chunks),
                   lambda: call(False, _fox_live_chunks(cb, top, batch, seq)))
    return out.reshape(batch * seq, OUT_B)


def _fox_shift(q_norm_b, k_norm_b):
    bound = HEAD_DIM * (HEAD_DIM ** -0.5) * jnp.max(jnp.abs(q_norm_b)) * jnp.max(jnp.abs(k_norm_b))
    shift = jnp.maximum(2.0 * FOX_ROUNDING_SLACK * bound - FOX_EXP_HEADROOM, 0.0).astype(F32)
    top = 2.0 * FOX_ROUNDING_SLACK * bound - shift
    return shift, top.astype(F32), shift > FOX_MAX_SHIFT


def _post_kernel(x_ref, ya_ref, yb_ref, gmix_ref, wg_ref, bg_ref, wpa_ref, wpb_ref, wo_ref,
                 gffn_ref, wr_ref, rb_ref, x1_ref, h2_ref, topi_ref, topw_ref):
    d = x_ref.shape[1]
    x = x_ref[...]
    h = (x * lax.rsqrt(jnp.mean(x * x, axis=-1, keepdims=True) + EPS) * gmix_ref[...]).astype(BF16)
    gates = jax.nn.sigmoid(_dot(h, wg_ref[...]) + bg_ref[...])
    merged = gates[:, :d] * _dot(ya_ref[...], wpa_ref[...]) + gates[:, d:] * _dot(yb_ref[...], wpb_ref[...])
    x1 = x + _dot(merged.astype(BF16), wo_ref[...])
    x1_ref[...] = x1
    h2 = x1 * lax.rsqrt(jnp.mean(x1 * x1, axis=-1, keepdims=True) + EPS) * gffn_ref[...]
    h2_ref[...] = h2.astype(BF16)

    hh, hm, _ = _split3(h2)
    wr = wr_ref[...]
    wh = wr.astype(BF16)
    wl = (wr - wh.astype(F32)).astype(BF16)
    hh, hm = hh.astype(BF16), hm.astype(BF16)
    logits = _dot_nt(wh, hh) + _dot_nt(wh, hm) + _dot_nt(wl, hh)
    scores = jax.nn.sigmoid(logits)
    biased = scores + rb_ref[...]
    eid = lax.broadcasted_iota(jnp.int32, scores.shape, 0).astype(F32)
    chosen = jnp.zeros(scores.shape, jnp.bool_)
    idx, val = [], []
    for _ in range(TOP_K):
        cur = jnp.where(chosen, -jnp.inf, biased)
        mx = jnp.max(cur, axis=0, keepdims=True)
        first = jnp.min(jnp.where(cur == mx, eid, float(N_EXPERTS)), axis=0, keepdims=True)
        pick = eid == first
        chosen = jnp.logical_or(chosen, pick)
        idx.append(first)
        val.append(jnp.sum(jnp.where(pick, scores, 0.0), axis=0, keepdims=True))
    top_s = jnp.concatenate(val, axis=0)
    top_w = top_s / jnp.sum(top_s, axis=0, keepdims=True) * ROUTE_SCALE
    tm = scores.shape[1]
    both = jnp.concatenate(idx + [top_w, jnp.zeros((LANES - 2 * TOP_K, tm), F32)], axis=0).T
    topi_ref[...] = both[:, :TOP_K].astype(jnp.int32)
    topw_ref[...] = both[:, TOP_K:2 * TOP_K]


def _post(xf, ya, yb, g_mix, w_gate, b_gate, w_proj_a, w_proj_b, w_out, g_ffn, w_router, router_bias):
    n, d = xf.shape
    tm = TM_POST
    const = lambda shape: pl.BlockSpec(shape, lambda i: (0,) * len(shape))
    row = lambda w: pl.BlockSpec((tm, w), lambda i: (i, 0))
    args = [xf, ya, yb, g_mix.reshape(1, d), w_gate.astype(BF16), b_gate.reshape(1, 2 * d),
            w_proj_a.astype(BF16), w_proj_b.astype(BF16), w_out.astype(BF16), g_ffn.reshape(1, d),
            w_router.astype(F32).T, router_bias.astype(F32).reshape(N_EXPERTS, 1)]
    in_specs = [row(d), row(OUT_A), row(OUT_B)] + [const(a.shape) for a in args[3:]]
    return pl.pallas_call(
        _post_kernel,
        grid=(n // tm,),
        in_specs=in_specs,
        out_specs=[row(d), row(d), row(TOP_K), row(TOP_K)],
        out_shape=[jax.ShapeDtypeStruct((n, d), F32), jax.ShapeDtypeStruct((n, d), BF16),
                   jax.ShapeDtypeStruct((n, TOP_K), jnp.int32), jax.ShapeDtypeStruct((n, TOP_K), F32)],
        compiler_params=pltpu.CompilerParams(
            dimension_semantics=("arbitrary",), vmem_limit_bytes=VMEM_LIMIT),
        name="post",
    )(*args)


def _dispatch_kernel(h2_ref, topi_ref, tri_ref, upper_ref, xs_ref, slots_ref, cnt_ref, off_ref):
    tb = h2_ref.shape[0]
    topi = topi_ref[...]
    lane = lax.broadcasted_iota(jnp.int32, (tb, N_EXPERTS), 1)
    picks = [lane == topi[:, k:k + 1] for k in range(TOP_K)]
    mask = picks[0]
    for pk in picks[1:]:
        mask = jnp.logical_or(mask, pk)
    maskf = jnp.where(mask, 1.0, 0.0)
    rank = _dot(tri_ref[...], maskf.astype(BF16))
    cnt = jnp.sum(maskf, axis=0, keepdims=True)
    gran = jnp.floor((cnt + (GRAN - 1)) * (1.0 / GRAN))
    goff = _dot(jnp.broadcast_to(gran, (8, N_EXPERTS)).astype(BF16), upper_ref[...])[0:1]
    off = goff * GRAN
    slot_te = off + rank
    slots = jnp.concatenate(
        [jnp.sum(jnp.where(pk, slot_te, 0.0), axis=-1, keepdims=True) for pk in picks], axis=1)
    slots_ref[...] = slots.astype(jnp.int32)
    cnt_ref[0] = cnt.astype(jnp.int32)
    off_ref[0] = off.astype(jnp.int32)
    v = jnp.where(mask, slot_te, float(NO_SLOT))
    v_hi = jnp.floor(v * (1.0 / SLOT_RADIX))
    w = jnp.concatenate([v_hi * SLOT_RADIX, v - v_hi * SLOT_RADIX], axis=1).T.astype(BF16)
    end = off + gran * GRAN
    used = jnp.max(end).astype(jnp.int32)
    h2 = h2_ref[...]

    def lookup(first, rows):
        s_e = (lax.broadcasted_iota(jnp.int32, (rows, N_EXPERTS), 0) + first).astype(F32)
        own = jnp.where(jnp.logical_and(s_e >= off, s_e < end), 1.0, 0.0)
        return _dot(jnp.concatenate([own, own], axis=1).astype(BF16), w)

    def sort_chunk(c, looked):
        s_t = (lax.broadcasted_iota(jnp.int32, (SLOT_CHUNK, tb), 0) + c * SLOT_CHUNK).astype(F32)
        onehot = jnp.where(looked == s_t, 1.0, 0.0).astype(BF16)
        xs_ref[0, c * SLOT_CHUNK:(c + 1) * SLOT_CHUNK, :] = _dot(onehot, h2).astype(BF16)

    looked_typ = lookup(0, CHUNKS_TYPICAL * SLOT_CHUNK)
    for c in range(CAP // SLOT_CHUNK):
        if c < CHUNKS_TYPICAL:
            sort_chunk(c, looked_typ[c * SLOT_CHUNK:(c + 1) * SLOT_CHUNK, :])
        else:
            @pl.when(c * SLOT_CHUNK < used)
            def _(c=c):
                sort_chunk(c, lookup(c * SLOT_CHUNK, SLOT_CHUNK))

            @pl.when(c * SLOT_CHUNK >= used)
            def _(c=c):
                xs_ref[0, c * SLOT_CHUNK:(c + 1) * SLOT_CHUNK, :] = jnp.zeros((SLOT_CHUNK, xs_ref.shape[2]), BF16)


def _dispatch(h2, topi):
    n, d = h2.shape
    nb = n // TB
    tri = jnp.asarray(np.tril(np.ones((TB, TB), np.float32), -1), BF16)
    upper = jnp.asarray(np.triu(np.ones((N_EXPERTS, N_EXPERTS), np.float32), 1), BF16)
    const = lambda shape: pl.BlockSpec(shape, lambda i: (0,) * len(shape))
    meta = pl.BlockSpec((1, 1, N_EXPERTS), lambda i: (i, 0, 0))
    return pl.pallas_call(
        _dispatch_kernel,
        grid=(nb,),
        in_specs=[pl.BlockSpec((TB, d), lambda i: (i, 0)), pl.BlockSpec((TB, TOP_K), lambda i: (i, 0)),
                  const(tri.shape), const(upper.shape)],
        out_specs=[pl.BlockSpec((1, CAP, d), lambda i: (i, 0, 0)),
                   pl.BlockSpec((TB, TOP_K), lambda i: (i, 0)), meta, meta],
        out_shape=[jax.ShapeDtypeStruct((nb, CAP, d), BF16), jax.ShapeDtypeStruct((n, TOP_K), jnp.int32),
                   jax.ShapeDtypeStruct((nb, 1, N_EXPERTS), jnp.int32),
                   jax.ShapeDtypeStruct((nb, 1, N_EXPERTS), jnp.int32)],
        compiler_params=pltpu.CompilerParams(
            dimension_semantics=("arbitrary",), vmem_limit_bytes=VMEM_LIMIT),
        name="dispatch",
    )(h2, topi, tri, upper)


def _ffn_kernel(item_e_sm, item_d0_sm, item_nd_sm, item_s0_sm, item_ns_sm, dlist_sm, slist_sm,
                xs_hbm, wg_ref, wu_ref, wd_ref, ys_hbm, xbuf, ybuf, wg_bf, wu_bf, wd_bf, sem_in, sem_out):
    step = pl.program_id(0)
    nsteps = pl.num_programs(0)
    buf = step % 2

    def granules(st):
        return 2 * item_nd_sm[st] + item_ns_sm[st]

    def for_copies(st, fn):
        d0, nd = item_d0_sm[st], item_nd_sm[st]
        s0, ns = item_s0_sm[st], item_ns_sm[st]

        def pair(j, carry):
            fn(dlist_sm[d0 + j], 2 * j, 2)
            return carry

        def single(j, carry):
            fn(slist_sm[s0 + j], 2 * nd + j, 1)
            return carry

        lax.fori_loop(0, nd, pair, 0)
        lax.fori_loop(0, ns, single, 0)

    def fetch(b_):
        return lambda src, dst, k: pltpu.make_async_copy(
            xs_hbm.at[pl.ds(src, k)], xbuf.at[b_, pl.ds(dst, k)], sem_in.at[b_])

    def writeback(b_):
        return lambda src, dst, k: pltpu.make_async_copy(
            ybuf.at[b_, pl.ds(dst, k)], ys_hbm.at[pl.ds(src, k)], sem_out.at[b_])

    def start(mk):
        return lambda src, dst, k: mk(src, dst, k).start()

    def wait_all(st, span):
        n = granules(st)
        size = PASS_GRAN
        while size >= 1:
            @pl.when((n & size) != 0)
            def _(size=size):
                span(size).wait()
            size //= 2
        return n

    def fetch_span(b_):
        return lambda k: pltpu.make_async_copy(
            xs_hbm.at[pl.ds(0, k)], xbuf.at[b_, pl.ds(0, k)], sem_in.at[b_])

    def writeback_span(b_):
        return lambda k: pltpu.make_async_copy(
            ybuf.at[b_, pl.ds(0, k)], ys_hbm.at[pl.ds(0, k)], sem_out.at[b_])

    @pl.when(step == 0)
    def _():
        xbuf[...] = jnp.zeros_like(xbuf)
        for_copies(step, start(fetch(0)))

    @pl.when(step + 1 < nsteps)
    def _():
        for_copies(step + 1, start(fetch(1 - buf)))

    ngran = wait_all(step, fetch_span(buf))

    @pl.when(step >= 2)
    def _():
        wait_all(step - 2, writeback_span(buf))

    @pl.when(ngran > 0)
    def _():
        wg_bf[...] = wg_ref[0].astype(BF16)
        wu_bf[...] = wu_ref[0].astype(BF16)
        wd_bf[...] = wd_ref[0].astype(BF16)

    x_cols = xbuf.shape[-1]

    def ffn_rows(base, rows):
        grans = pl.ds(pl.multiple_of(base // GRAN, rows // GRAN), rows // GRAN)
        x = xbuf[buf, grans].reshape(rows, x_cols)
        g = _dot(x, wg_bf[...])
        u = _dot(x, wu_bf[...])
        mid = (g * jax.nn.sigmoid(g) * u).astype(BF16)
        ybuf[buf, grans] = _dot(mid, wd_bf[...]).astype(BF16).reshape(rows // GRAN, GRAN, x_cols)

    nt = (ngran * GRAN + (FT - 1)) // FT
    big = FT_BIG // FT

    def big_tile(i, carry):
        ffn_rows(pl.multiple_of(i * FT_BIG, FT_BIG), FT_BIG)
        return carry

    lax.fori_loop(0, nt // big, big_tile, 0)
    size = big // 2
    while size >= 1:
        @pl.when((nt & size) != 0)
        def _(size=size):
            ffn_rows(pl.multiple_of((nt & ~(2 * size - 1)) * FT, size * FT), size * FT)
        size //= 2

    for_copies(step, start(writeback(buf)))

    @pl.when(step == nsteps - 1)
    def _():
        wait_all(step, writeback_span(buf))

        @pl.when(step >= 1)
        def _():
            wait_all(step - 1, writeback_span(1 - buf))


def _work_items(cnt, off):
    nb = cnt.shape[0]
    nseg = N_EXPERTS * nb
    i32 = jnp.int32
    seg_n = ((cnt.reshape(nb, N_EXPERTS) + (GRAN - 1)) // GRAN).T
    seg_row = ((off.reshape(nb, N_EXPERTS) + jnp.arange(nb, dtype=i32)[:, None] * CAP) // GRAN).T.reshape(-1)
    before = jnp.cumsum(seg_n, axis=1) - seg_n
    pass_id = (before // (PASS_GRAN - TB // GRAN)).reshape(-1)
    seg_n = seg_n.reshape(-1)
    nd, ns = seg_n // 2, seg_n % 2
    gmax = nb * (TB * TOP_K // GRAN + N_EXPERTS)

    d_end = jnp.cumsum(nd)
    d_start = d_end - nd
    prev_end = jnp.concatenate([jnp.full((1,), 2, i32), (seg_row + 2 * nd)[:-1]])
    dlist = jnp.cumsum(jnp.full((gmax // 2,), 2, i32).at[d_start].add(seg_row - prev_end, mode="drop"))
    s_end = jnp.cumsum(ns)
    s_start = s_end - ns
    slist = jnp.zeros((nseg,), i32).at[jnp.where(ns == 1, s_start, nseg)].add(seg_row + 2 * nd, mode="drop")

    seg = jnp.arange(nseg, dtype=i32)
    new_item = jnp.logical_or(seg % nb == 0, pass_id != jnp.concatenate([pass_id[:1], pass_id[:-1]]))
    item_of_seg = jnp.cumsum(new_item.astype(i32)) - 1
    n_items = N_EXPERTS + gmax // (PASS_GRAN - TB // GRAN)
    first_seg = jnp.full((n_items + 1,), nseg, i32).at[
        jnp.where(new_item, item_of_seg, n_items + 1)].add(seg - nseg, mode="drop")
    lo, hi = first_seg[:-1], first_seg[1:]
    d_bound = jnp.concatenate([d_start, d_end[-1:]])
    s_bound = jnp.concatenate([s_start, s_end[-1:]])
    item_e = jnp.minimum(lo // nb, N_EXPERTS - 1)
    return (item_e.astype(i32), d_bound[lo], d_bound[hi] - d_bound[lo], s_bound[lo], s_bound[hi] - s_bound[lo],
            dlist.astype(i32), slist)


def _ffn(xs, cnt, off, wg, wu, wd):
    _, _, d = xs.shape
    items = _work_items(cnt, off)
    per_expert = lambda shape: pl.BlockSpec((1,) + shape, lambda w, ie, *_: (ie[w], 0, 0))
    grid_spec = pltpu.PrefetchScalarGridSpec(
        num_scalar_prefetch=len(items),
        grid=(items[0].shape[0],),
        in_specs=[pl.BlockSpec(memory_space=pl.ANY), per_expert((d, D_EXPERT)), per_expert((d, D_EXPERT)),
                  per_expert((D_EXPERT, d))],
        out_specs=pl.BlockSpec(memory_space=pl.ANY),
        scratch_shapes=[pltpu.VMEM((2, PASS_GRAN, GRAN, d), BF16)] * 2 + [
                        pltpu.VMEM((d, D_EXPERT), BF16), pltpu.VMEM((d, D_EXPERT), BF16),
                        pltpu.VMEM((D_EXPERT, d), BF16),
                        pltpu.SemaphoreType.DMA((2,)), pltpu.SemaphoreType.DMA((2,))],
    )
    return pl.pallas_call(
        _ffn_kernel,
        grid_spec=grid_spec,
        out_shape=jax.ShapeDtypeStruct(xs.shape, xs.dtype),
        input_output_aliases={len(items): 0},
        compiler_params=pltpu.CompilerParams(
            dimension_semantics=("arbitrary",), vmem_limit_bytes=VMEM_LIMIT),
        name="ffn",
    )(*items, xs, wg, wu, wd)


def _combine_kernel(used_sm, tail_sm, x1_ref, h2_ref, *rest):
    n_chunks = CAP // SLOT_CHUNK
    ys_refs = rest[:n_chunks]
    slots_ref, topw_ref, wgus_ref, wds_ref, o_ref = rest[n_chunks:]
    tb = x1_ref.shape[0]
    gu = _dot(h2_ref[...], wgus_ref[...])
    g, u = gu[:, :D_SHARED], gu[:, D_SHARED:]
    acc = x1_ref[...] + _dot((g * jax.nn.sigmoid(g) * u).astype(BF16), wds_ref[...])
    slots = slots_ref[...].astype(F32)
    topw = topw_ref[...]
    used = used_sm[pl.program_id(0)]

    def gather_chunk(c):
        scol = (lax.broadcasted_iota(jnp.int32, (tb, SLOT_CHUNK), 1) + c * SLOT_CHUNK).astype(F32)
        gate = jnp.zeros((tb, SLOT_CHUNK), F32)
        for k in range(TOP_K):
            gate = jnp.where(scol == slots[:, k:k + 1], topw[:, k:k + 1], gate)
        return _dot(gate.astype(BF16), ys_refs[c][0])

    for c in range(CHUNKS_TYPICAL):
        acc = acc + gather_chunk(c)
    o_ref[...] = acc
    for c in range(CHUNKS_TYPICAL, CAP // SLOT_CHUNK):
        @pl.when(c * SLOT_CHUNK < used)
        def _(c=c):
            o_ref[...] += gather_chunk(c)


def _combine(x1, h2, ys, slots, topw, used, wgus, wds):
    n, d = x1.shape
    n_chunks = CAP // SLOT_CHUNK
    const = lambda shape: pl.BlockSpec(shape, lambda i, u, t: (0,) * len(shape))
    row = lambda w: pl.BlockSpec((TB, w), lambda i, u, t: (i, 0))
    blocks = jnp.arange(n // TB, dtype=jnp.int32)
    tails = [lax.cummax(jnp.where(used > c * SLOT_CHUNK, blocks, 0)) for c in range(CHUNKS_TYPICAL, n_chunks)]
    tail_idx = jnp.stack(tails).astype(jnp.int32)
    n_blocks = n // TB

    def chunk_spec(c):
        if c < CHUNKS_TYPICAL:
            return pl.BlockSpec((1, SLOT_CHUNK, d), lambda i, u, t: (i * n_chunks + c, 0, 0))
        late = c - CHUNKS_TYPICAL
        return pl.BlockSpec((1, SLOT_CHUNK, d), lambda i, u, t: (t[late * n_blocks + i] * n_chunks + c, 0, 0))

    grid_spec = pltpu.PrefetchScalarGridSpec(
        num_scalar_prefetch=2,
        grid=(n_blocks,),
        in_specs=[row(d), row(d)] + [chunk_spec(c) for c in range(n_chunks)] + [
            row(TOP_K), row(TOP_K), const(wgus.shape), const(wds.shape)],
        out_specs=row(d),
    )
    ys_chunks = ys.reshape(n_blocks * n_chunks, SLOT_CHUNK, d)
    return pl.pallas_call(
        _combine_kernel,
        grid_spec=grid_spec,
        out_shape=jax.ShapeDtypeStruct((n, d), F32),
        compiler_params=pltpu.CompilerParams(
            dimension_semantics=("arbitrary",), vmem_limit_bytes=VMEM_LIMIT),
        name="combine",
    )(used, tail_idx.reshape(-1), x1, h2, *([ys_chunks] * n_chunks), slots, topw, wgus, wds)


def _moe(x1, h2, topi, topw, w_gate_e, w_up_e, w_down_e, w_gate_s, w_up_s, w_down_s):
    n, d = x1.shape
    wgus = jnp.concatenate([w_gate_s.astype(BF16), w_up_s.astype(BF16)], axis=-1)
    xs, slots, cnt, off = _dispatch(h2, topi)
    ys = _ffn(xs.reshape(-1, GRAN, d), cnt, off, w_gate_e, w_up_e, w_down_e)
    used = jnp.max(off + (cnt + (GRAN - 1)) // GRAN * GRAN, axis=(1, 2)).astype(jnp.int32)
    return _combine(x1, h2, ys.reshape(n // TB, CAP, d), slots, topw, used, wgus, w_down_s.astype(BF16))


def kernel(x, g_mix, w_in, q_norm_a, k_norm_a, q_norm_b, k_norm_b, rel_bias, b_forget, w_gate, b_gate,
           w_proj_a, w_proj_b, w_out, g_ffn, w_router, router_bias, w_gate_e, w_up_e, w_down_e,
           w_gate_s, w_up_s, w_down_s):
    batch, seq, d = x.shape
    xf = x.reshape(batch * seq, d)
    fox_shift, fox_top, fox_online = _fox_shift(q_norm_b, k_norm_b)
    pa0, pa1, pa2, qkb, vb, cb = _inproj(xf, g_mix, w_in, q_norm_a, k_norm_a, q_norm_b, k_norm_b, b_forget,
                                     fox_shift, seq)
    bias = _toeplitz_bias(rel_bias)
    ya = _dilated(pa0, pa1, pa2, bias, batch, seq)

    yb = _fox(qkb, vb, cb, fox_top, fox_online, batch, seq)
    x1, h2, topi, topw = _post(xf, ya, yb, g_mix, w_gate, b_gate, w_proj_a, w_proj_b, w_out, g_ffn,
                               w_router, router_bias)
    out = _moe(x1, h2, topi, topw, w_gate_e, w_up_e, w_down_e, w_gate_s, w_up_s, w_down_s)
    return out.reshape(batch, seq, d)
```

```python
import functools
import math

import jax
import jax.numpy as jnp
import numpy as np
from jax import lax
from jax.experimental import pallas as pl
from jax.experimental.pallas import tpu as pltpu

HEAD_DIM = 64
DIL_GROUPS = ((128, 1), (512, 4), (2048, 16))
HEADS_PER_GROUP = 4
N_HEADS_A = HEADS_PER_GROUP * len(DIL_GROUPS)
N_HEADS_B = 8
REL_BUCKETS = 32
REL_MAX_DIST = 2048
N_EXPERTS = 64
TOP_K = 8
D_EXPERT = 256
D_SHARED = 256
ROUTE_SCALE = 2.5
EPS = 1e-6

WIDTH_A = 3 * N_HEADS_A * HEAD_DIM
WIDTH_B = 3 * N_HEADS_B * HEAD_DIM
QK_B = N_HEADS_B * HEAD_DIM
OUT_A = HEADS_PER_GROUP * HEAD_DIM
OUT_B = N_HEADS_B * HEAD_DIM

LANES = 128
GROUP_W = HEADS_PER_GROUP * HEAD_DIM
WIN_J = 128
SUPER = DIL_GROUPS[-1][1] * WIN_J
NEG = -1e30
VMEM_LIMIT = 56 * 1024 * 1024

DIL_UNROLL = 16
TM_IN = 512
TM_POST = 1024
TQ_FOX = 1024
FOX_ROUNDING_SLACK = 1.02
FOX_EXP_HEADROOM = 60.0
FOX_DEAD_EXPONENT = -105.0
FOX_MAX_SHIFT = 80.0
TB = 256
GRAN = 16
CAP = TB * TOP_K + N_EXPERTS * GRAN
SLOT_CHUNK = 512
CHUNKS_TYPICAL = -(-(TB * TOP_K + N_EXPERTS * GRAN // 2) // SLOT_CHUNK)
SLOT_RADIX = 64
NO_SLOT = SLOT_RADIX ** 2 - 1
assert CAP <= NO_SLOT
PASS_GRAN = 256
FT = 256
FT_BIG = 1024

BF16 = jnp.bfloat16
F32 = jnp.float32


def _dot(a, b):
    return jnp.dot(a, b, preferred_element_type=F32)


def _dot_nt(a, b):
    return lax.dot_general(a, b, (((1,), (1,)), ((), ())), preferred_element_type=F32)


def _split3(v):
    hi = v.astype(BF16).astype(F32)
    r = v - hi
    mid = r.astype(BF16).astype(F32)
    lo = (r - mid).astype(BF16).astype(F32)
    return hi, mid, lo


def _inproj_kernel(x_ref, g_ref, wa_ref, wb_ref, wf_ref, bd_ref, tri_ref, gain_a_ref, gain_b_ref,
                   bf_ref, shift_ref, pa0_ref, pa1_ref, pa2_ref, qkb_ref, vb_ref, cb_ref, carry_ref, h_ref, *,
                   tiles_per_seq):
    tm = x_ref.shape[0]
    x = x_ref[...]
    h = x * lax.rsqrt(jnp.mean(x * x, axis=-1, keepdims=True) + EPS) * g_ref[...]
    n_lane_chunks = h_ref.shape[0]
    for c in range(n_lane_chunks):
        h_ref[c] = h[:, c * LANES:(c + 1) * LANES]
    h = h.astype(BF16)
    bd = bd_ref[...]

    def headnorm(p, gain):
        ms = _dot((p * p).astype(BF16), bd)
        return p * lax.rsqrt(ms + EPS) * gain

    for g, (pa_ref, (_, dil)) in enumerate(zip((pa0_ref, pa1_ref, pa2_ref), DIL_GROUPS)):
        rows = tm // dil
        if dil == 1:
            hg = h
        else:
            hg = jnp.concatenate([jnp.concatenate(
                [h_ref[c, pl.ds(r, rows, stride=dil), :] for c in range(n_lane_chunks)], axis=1)
                for r in range(dil)], axis=0).astype(BF16)
        qkv = _dot(hg, wa_ref[g])
        for part in range(3):
            cols = slice(part * GROUP_W, (part + 1) * GROUP_W)
            p = qkv[:, cols]
            if part < 2:
                p = headnorm(p, gain_a_ref[part:part + 1, :])
            p = p.astype(BF16)
            if dil == 1:
                pa_ref[:, cols] = p
            else:
                for r in range(dil):
                    pa_ref[0, r, :, cols] = p[r * rows:(r + 1) * rows, :]

    f = _dot(h, wf_ref[...]) + bf_ref[...]
    logf = jnp.minimum(f, 0.0) - jnp.log1p(jnp.exp(-jnp.abs(f)))
    tri = tri_ref[...]
    lh, lm, ll = _split3(logf)
    cum3 = _dot(tri, jnp.concatenate([lh, lm, ll], axis=1).astype(BF16))
    cum = cum3[:, :LANES] + cum3[:, LANES:2 * LANES] + cum3[:, 2 * LANES:]

    @pl.when(pl.program_id(0) % tiles_per_seq == 0)
    def _():
        carry_ref[...] = jnp.zeros_like(carry_ref)

    cum = cum + carry_ref[0:1, :]
    carry_ref[0:1, :] = cum[tm - 1:tm, :]
    cb_ref[0] = jnp.concatenate([cum[0:1, :], cum[tm - 1:tm, :], jnp.zeros((6, LANES), F32)], axis=0)
    ch, cm, cl = _split3(cum)

    j = lax.broadcasted_iota(jnp.int32, (tm, HEAD_DIM), 1)

    def ext_cols(vals):
        out = jnp.zeros((tm, HEAD_DIM), F32)
        ones = [pos for pos, val in enumerate(vals) if isinstance(val, float)]
        if ones:
            is_one = functools.reduce(jnp.logical_or, [j == pos for pos in ones])
            out = jnp.where(is_one, 1.0, out)
        for pos, val in enumerate(vals):
            if not isinstance(val, float):
                out = jnp.where(j == pos, val, out)
        return out

    for c in range(QK_B // GROUP_W):
        qkv = _dot(h, wb_ref[c])
        pq = headnorm(qkv[:, 0:GROUP_W], gain_b_ref[0:1, :])
        pk = headnorm(qkv[:, GROUP_W:2 * GROUP_W], gain_b_ref[1:2, :])
        pv = qkv[:, 2 * GROUP_W:3 * GROUP_W]
        r = _dot((pq * pk).astype(BF16), bd) * HEAD_DIM + shift_ref[...]
        for hh in range(HEADS_PER_GROUP):
            head = c * HEADS_PER_GROUP + hh
            lanes = slice(hh * HEAD_DIM, (hh + 1) * HEAD_DIM)
            col = lambda a, idx: a[:, idx:idx + 1]
            cs = [col(ch, head), col(cm, head), col(cl, head)]
            ext_q = ext_cols(cs + [1.0] * 3 + [-col(r, hh * HEAD_DIM)])
            ext_k = ext_cols([1.0] * 3 + [-v for v in cs] + [1.0])
            ext_v = ext_cols([1.0])
            for part, (val, ext) in enumerate(((pq, ext_q), (pk, ext_k))):
                o0 = (part * N_HEADS_B + head) * LANES
                qkb_ref[:, o0:o0 + LANES] = jnp.concatenate([val[:, lanes], ext], axis=-1).astype(BF16)
            vb_ref[:, head * LANES:(head + 1) * LANES] = jnp.concatenate(
                [pv[:, lanes], ext_v], axis=-1).astype(BF16)


def _inproj(xf, g_mix, w_in, q_norm_a, k_norm_a, q_norm_b, k_norm_b, b_forget, fox_shift, seq):
    n, d = xf.shape
    tm = TM_IN
    scale = HEAD_DIM ** -0.5
    w_bf = w_in.astype(BF16)
    qkv_w = N_HEADS_A * HEAD_DIM
    wa = jnp.stack([jnp.concatenate(
        [w_bf[:, part * qkv_w + g * GROUP_W: part * qkv_w + (g + 1) * GROUP_W] for part in range(3)],
        axis=1) for g in range(len(DIL_GROUPS))])
    wb = jnp.stack([jnp.concatenate(
        [w_bf[:, WIDTH_A + part * QK_B + c * GROUP_W: WIDTH_A + part * QK_B + (c + 1) * GROUP_W]
         for part in range(3)], axis=1) for c in range(QK_B // GROUP_W)])
    wf = jnp.pad(w_bf[:, WIDTH_A + WIDTH_B:], ((0, 0), (0, LANES - N_HEADS_B)))
    bfp = jnp.pad(b_forget.astype(F32), (0, LANES - N_HEADS_B)).reshape(1, LANES)
    seg = np.arange(GROUP_W) // HEAD_DIM
    bd = jnp.asarray((seg[:, None] == seg[None, :]).astype(np.float32) / HEAD_DIM, BF16)
    tri = jnp.asarray(np.tril(np.ones((tm, tm), np.float32)), BF16)
    gain_a = jnp.stack([jnp.tile(q_norm_a, HEADS_PER_GROUP) * scale, jnp.tile(k_norm_a, HEADS_PER_GROUP)])
    gain_b = jnp.stack([jnp.tile(q_norm_b, HEADS_PER_GROUP) * scale, jnp.tile(k_norm_b, HEADS_PER_GROUP)])
    const = lambda shape: pl.BlockSpec(shape, lambda i: (0,) * len(shape))
    tps = seq // tm
    batch = n // seq
    qkv3 = 3 * GROUP_W
    (_, d1), (_, d2) = DIL_GROUPS[1], DIL_GROUPS[2]
    return pl.pallas_call(
        functools.partial(_inproj_kernel, tiles_per_seq=tps),
        grid=(n // tm,),
        in_specs=[
            pl.BlockSpec((tm, d), lambda i: (i, 0)),
            const((1, d)), const(wa.shape), const(wb.shape), const(wf.shape),
            const(bd.shape), const(tri.shape), const(gain_a.shape), const(gain_b.shape),
            const(bfp.shape), const((1, 1)),
        ],
        out_specs=[
            pl.BlockSpec((tm, qkv3), lambda i: (i, 0)),
            pl.BlockSpec((1, d1, tm // d1, qkv3), lambda i: (i // tps, 0, i % tps, 0)),
            pl.BlockSpec((1, d2, tm // d2, qkv3), lambda i: (i // tps, 0, i % tps, 0)),
            pl.BlockSpec((tm, 2 * N_HEADS_B * LANES), lambda i: (i, 0)),
            pl.BlockSpec((tm, N_HEADS_B * LANES), lambda i: (i, 0)),
            pl.BlockSpec((1, 8, LANES), lambda i: (i, 0, 0)),
        ],
        out_shape=[
            jax.ShapeDtypeStruct((n, qkv3), BF16),
            jax.ShapeDtypeStruct((batch, d1, seq // d1, qkv3), BF16),
            jax.ShapeDtypeStruct((batch, d2, seq // d2, qkv3), BF16),
            jax.ShapeDtypeStruct((n, 2 * N_HEADS_B * LANES), BF16),
            jax.ShapeDtypeStruct((n, N_HEADS_B * LANES), BF16),
            jax.ShapeDtypeStruct((n // tm, 8, LANES), F32),
        ],
        scratch_shapes=[pltpu.VMEM((8, LANES), F32), pltpu.VMEM((d // LANES, tm, LANES), F32)],
        compiler_params=pltpu.CompilerParams(
            dimension_semantics=("arbitrary",), vmem_limit_bytes=VMEM_LIMIT),
        name="inproj",
    )(xf, g_mix.reshape(1, d), wa, wb, wf, bd, tri, gain_a, gain_b, bfp, fox_shift.reshape(1, 1))


def _dilated_kernel(p0_ref, h0_ref, p1_ref, h1_ref, p2_ref, h2_ref, bias_ref, o_ref, acc_ref, lse_ref):
    tq = WIN_J
    first_sb = pl.program_id(1) == 0
    lane_head = lax.broadcasted_iota(jnp.int32, (tq, GROUP_W), 1) // HEAD_DIM
    prev_col = lax.broadcasted_iota(jnp.int32, (tq, 2 * tq), 1) < tq
    qc, kc_, vc_ = (slice(0, GROUP_W), slice(GROUP_W, 2 * GROUP_W), slice(2 * GROUP_W, 3 * GROUP_W))

    def attend(g, q, kp, kc, vp, vc, no_prev):
        kcat = jnp.concatenate([kp, kc], axis=0)
        vcat = jnp.concatenate([vp, vc], axis=0)
        dead = jnp.logical_and(no_prev, prev_col)
        q4 = jnp.concatenate([jnp.where(lane_head == hh, q, jnp.zeros_like(q))
                              for hh in range(HEADS_PER_GROUP)], axis=0)
        s = _dot_nt(q4, kcat) + bias_ref[g].reshape(HEADS_PER_GROUP * tq, 2 * tq)
        s = jnp.where(jnp.concatenate([dead] * HEADS_PER_GROUP, axis=0), NEG, s)
        m = jnp.max(s, axis=-1, keepdims=True)
        p = jnp.exp(s - m)
        l = jnp.sum(p, axis=-1, keepdims=True)
        o4 = _dot(p.astype(BF16), vcat) * (1.0 / l)
        lse4 = m + jnp.log(l)
        acc = o4[0:tq]
        lse = jnp.broadcast_to(lse4[0:tq], (tq, GROUP_W))
        for hh in range(1, HEADS_PER_GROUP):
            sel = lane_head == hh
            acc = jnp.where(sel, o4[hh * tq:(hh + 1) * tq], acc)
            lse = jnp.where(sel, lse4[hh * tq:(hh + 1) * tq], lse)
        return acc, lse

    n_half = GROUP_W // LANES

    def merge(rows, acc, lse):
        for c in range(n_half):
            lanes = slice(c * LANES, (c + 1) * LANES)
            l1 = lse_ref[c, rows, :]
            mx = jnp.maximum(l1, lse[:, lanes])
            w1 = jnp.exp(l1 - mx)
            w2 = jnp.exp(lse[:, lanes] - mx)
            den = w1 + w2
            acc_ref[c, rows, :] = (w1 * acc_ref[c, rows, :] + w2 * acc[:, lanes]) / den
            lse_ref[c, rows, :] = mx + jnp.log(den)

    def pick(first, halo, body):
        return jnp.where(first, halo, body)

    def loop(n, body):
        def trip(i, carry):
            for u in range(DIL_UNROLL):
                body(i * DIL_UNROLL + u, carry)
            return carry
        lax.fori_loop(0, n // DIL_UNROLL, trip, 0)

    def body0(j, carry):
        st = pl.multiple_of(j * tq, tq)
        pst = pl.multiple_of(jnp.maximum(j - 1, 0) * tq, tq)
        cur, prv = pl.ds(st, tq), pl.ds(pst, tq)
        acc, lse = attend(
            0, p0_ref[0, cur, qc],
            pick(j == 0, h0_ref[0, :, kc_], p0_ref[0, prv, kc_]), p0_ref[0, cur, kc_],
            pick(j == 0, h0_ref[0, :, vc_], p0_ref[0, prv, vc_]), p0_ref[0, cur, vc_],
            jnp.logical_and(j == 0, first_sb))
        for c in range(n_half):
            acc_ref[c, cur, :] = acc[:, c * LANES:(c + 1) * LANES]
            lse_ref[c, cur, :] = lse[:, c * LANES:(c + 1) * LANES]
        return carry

    loop(SUPER // tq, body0)

    d1 = DIL_GROUPS[1][1]
    nsub1 = SUPER // d1 // tq
    def body1(t, carry):
        r, ii = t // nsub1, t % nsub1
        st = pl.multiple_of(ii * tq, tq)
        pst = pl.multiple_of(jnp.maximum(ii - 1, 0) * tq, tq)
        cur, prv = pl.ds(st, tq), pl.ds(pst, tq)
        acc, lse = attend(
            1, p1_ref[0, r, cur, qc],
            pick(ii == 0, h1_ref[0, r, :, kc_], p1_ref[0, r, prv, kc_]), p1_ref[0, r, cur, kc_],
            pick(ii == 0, h1_ref[0, r, :, vc_], p1_ref[0, r, prv, vc_]), p1_ref[0, r, cur, vc_],
            jnp.logical_and(ii == 0, first_sb))
        merge(pl.ds(ii * (tq * d1) + r, tq, stride=d1), acc, lse)
        return carry

    loop(d1 * nsub1, body1)

    d2 = DIL_GROUPS[2][1]

    def body2(r, carry):
        acc, lse = attend(2, p2_ref[0, r, :, qc], h2_ref[0, r, :, kc_], p2_ref[0, r, :, kc_],
                          h2_ref[0, r, :, vc_], p2_ref[0, r, :, vc_], first_sb)
        merge(pl.ds(r, tq, stride=d2), acc, lse)
        return carry

    loop(d2, body2)

    for c in range(n_half):
        o_ref[0, :, c * LANES:(c + 1) * LANES] = acc_ref[c].astype(o_ref.dtype)


def _rel_bucket(dist):
    max_exact = REL_BUCKETS // 2
    n = jnp.maximum(dist.astype(F32), 1.0)
    large = max_exact + (jnp.log(n / max_exact) / math.log(REL_MAX_DIST / max_exact)
                         * (REL_BUCKETS - max_exact)).astype(jnp.int32)
    large = jnp.minimum(large, REL_BUCKETS - 1)
    return jnp.where(dist < max_exact, dist, large)


def _toeplitz_bias(rel_bias, g, dil):
    tq = WIN_J
    offs = dil * (WIN_J - jnp.arange(WIN_J + 1, dtype=jnp.int32))
    hs = slice(g * HEADS_PER_GROUP, (g + 1) * HEADS_PER_GROUP)
    tab_rev = rel_bias[_rel_bucket(offs)][:, hs].T.astype(F32)
    period = 3 * tq
    neg = lambda w: jnp.full((HEADS_PER_GROUP, w), NEG, F32)
    vec = jnp.concatenate([neg(tq - 1), tab_rev, neg(period - 2 * tq)], axis=1)
    flat = jnp.broadcast_to(vec[:, None, :], (HEADS_PER_GROUP, tq, period)).reshape(HEADS_PER_GROUP, -1)
    skew = flat[:, :tq * (period - 1)].reshape(HEADS_PER_GROUP, tq, period - 1)
    return skew[:, :, tq - 1:3 * tq - 1]


def _dilated(pa0, pa1, pa2, bias, batch, seq):
    tq = WIN_J
    qkv3 = 3 * GROUP_W
    (_, d1), (_, d2) = DIL_GROUPS[1], DIL_GROUPS[2]
    nsb = seq // SUPER
    p0 = pa0.reshape(batch, seq, qkv3)
    prev_blk = lambda per_sb: (lambda b, s: jnp.maximum(s * per_sb - 1, 0))
    h0i, h1i, h2i = prev_blk(SUPER // tq), prev_blk(SUPER // d1 // tq), prev_blk(SUPER // d2 // tq)
    out = pl.pallas_call(
        _dilated_kernel,
        grid=(batch, nsb),
        in_specs=[
            pl.BlockSpec((1, SUPER, qkv3), lambda b, s: (b, s, 0)),
            pl.BlockSpec((1, tq, qkv3), lambda b, s: (b, h0i(b, s), 0)),
            pl.BlockSpec((1, d1, SUPER // d1, qkv3), lambda b, s: (b, 0, s, 0)),
            pl.BlockSpec((1, d1, tq, qkv3), lambda b, s: (b, 0, h1i(b, s), 0)),
            pl.BlockSpec((1, d2, SUPER // d2, qkv3), lambda b, s: (b, 0, s, 0)),
            pl.BlockSpec((1, d2, tq, qkv3), lambda b, s: (b, 0, h2i(b, s), 0)),
            pl.BlockSpec(bias.shape, lambda b, s: (0, 0, 0, 0)),
        ],
        out_specs=pl.BlockSpec((1, SUPER, GROUP_W), lambda b, s: (b, s, 0)),
        out_shape=jax.ShapeDtypeStruct((batch, seq, GROUP_W), BF16),
        scratch_shapes=[pltpu.VMEM((GROUP_W // LANES, SUPER, LANES), F32)] * 2,
        compiler_params=pltpu.CompilerParams(
            dimension_semantics=("arbitrary", "arbitrary"), vmem_limit_bytes=VMEM_LIMIT),
        name="dilated",
    )(p0, p0, pa1, pa1, pa2, pa2, bias)
    return out.reshape(batch * seq, GROUP_W)


def _fox_kernel(nlive_sm, q_ref, k_ref, v_ref, o_ref, m_ref, acc_ref, *, online):
    tq = q_ref.shape[1]
    half = tq // 2
    qi = pl.program_id(2)
    step = (pl.program_id(0) * pl.num_programs(1) + pl.program_id(1)) * pl.num_programs(2) + qi
    row = lax.broadcasted_iota(jnp.int32, (half, half), 0)
    col = lax.broadcasted_iota(jnp.int32, (half, half), 1)
    causal = row >= col

    def attend(hh, rows, state, start, nkeys, masked):
        m, acc = state
        lanes = slice(hh * LANES, (hh + 1) * LANES)
        s = _dot_nt(q_ref[0, rows, lanes], k_ref[0, pl.ds(start, nkeys), lanes])
        if masked:
            s = jnp.where(causal, s, NEG)
        if online:
            m_new = jnp.maximum(m, jnp.max(s, axis=-1, keepdims=True))
            acc = acc * jnp.exp(m - m_new)
            s = s - m_new
            m = m_new
        return m, acc + _dot(jnp.exp(s).astype(BF16), v_ref[0, pl.ds(start, nkeys), lanes])

    first = [qi - nlive_sm[2 * step + hh] for hh in range(2)]
    for hh in range(2):
        m_ref[hh] = jnp.full((tq, 1), NEG, F32)
        acc_ref[hh] = jnp.zeros((tq, LANES), F32)

    def full_chunk(ki, carry):
        start = pl.multiple_of(ki * tq, tq)
        for hh in range(2):
            @pl.when(ki >= first[hh])
            def _(hh=hh):
                m, acc = attend(hh, slice(None), (m_ref[hh], acc_ref[hh]), start, tq, False)
                acc_ref[hh] = acc
                if online:
                    m_ref[hh] = m
        return carry

    lax.fori_loop(jnp.minimum(first[0], first[1]), qi, full_chunk, 0)

    d0 = pl.multiple_of(qi * tq, tq)
    outs = []
    for hh in range(2):
        m, acc = m_ref[hh], acc_ref[hh]
        top, bot = slice(0, half), slice(half, tq)
        s_top = attend(hh, top, (m[top], acc[top]), d0, half, True)
        s_bot = attend(hh, bot, (m[bot], acc[bot]), d0, half, False)
        s_bot = attend(hh, bot, s_bot, d0 + half, half, True)
        a = jnp.concatenate([s_top[1], s_bot[1]], axis=0)
        outs.append(a[:, :HEAD_DIM] / a[:, HEAD_DIM:HEAD_DIM + 1])
    o_ref[0] = jnp.concatenate(outs, axis=-1).astype(o_ref.dtype)


def _fox_live_chunks(cb, top, batch, seq):
    tps = seq // TM_IN
    per = TQ_FOX // TM_IN
    nq = seq // TQ_FOX
    c_first = cb[:, 0, :N_HEADS_B].reshape(batch, tps, N_HEADS_B)[:, ::per]
    c_last = cb[:, 1, :N_HEADS_B].reshape(batch, tps, N_HEADS_B)[:, per - 1::per]
    live = (top + c_first[:, :, None, :] - c_last[:, None, :, :]) >= FOX_DEAD_EXPONENT
    back = jnp.arange(nq)[:, None] - jnp.arange(nq)[None, :]
    reach = jnp.max(jnp.where(jnp.logical_and(live, (back > 0)[None, :, :, None]),
                              back[None, :, :, None], 0), axis=2)
    reach = reach.reshape(batch, nq, N_HEADS_B // 2, 2).transpose(0, 2, 1, 3)
    return reach.reshape(-1).astype(jnp.int32)


def _fox(qkb, vb, cb, top, online, batch, seq):
    tq = TQ_FOX
    pairs = N_HEADS_B // 2
    nq = seq // tq
    qkv = qkb.reshape(batch, seq, 2 * N_HEADS_B * LANES)
    vv = vb.reshape(batch, seq, N_HEADS_B * LANES)
    all_chunks = jnp.tile(jnp.repeat(jnp.arange(nq, dtype=jnp.int32), 2), batch * pairs)

    def call(is_online, nlive):
        grid_spec = pltpu.PrefetchScalarGridSpec(
            num_scalar_prefetch=1,
            grid=(batch, pairs, nq),
            in_specs=[
                pl.BlockSpec((1, tq, 2 * LANES), lambda b, p, i, n: (b, i, p)),
                pl.BlockSpec((1, seq, 2 * LANES), lambda b, p, i, n: (b, 0, pairs + p)),
                pl.BlockSpec((1, seq, 2 * LANES), lambda b, p, i, n: (b, 0, p)),
            ],
            out_specs=pl.BlockSpec((1, tq, LANES), lambda b, p, i, n: (b, i, p)),
            scratch_shapes=[pltpu.VMEM((2, tq, 1), F32), pltpu.VMEM((2, tq, LANES), F32)],
        )
        return pl.pallas_call(
            functools.partial(_fox_kernel, online=is_online),
            grid_spec=grid_spec,
            out_shape=jax.ShapeDtypeStruct((batch, seq, OUT_B), BF16),
            compiler_params=pltpu.CompilerParams(
                dimension_semantics=("arbitrary", "arbitrary", "arbitrary"), vmem_limit_bytes=VMEM_LIMIT),
            name="fox_online" if is_online else "fox",
        )(nlive, qkv, qkv, vv)

    out = lax.cond(online, lambda: call(True, all_chunks),
                   lambda: call(False, _fox_live_chunks(cb, top, batch, seq)))
    return out.reshape(batch * seq, OUT_B)


def _fox_shift(q_norm_b, k_norm_b):
    bound = HEAD_DIM * (HEAD_DIM ** -0.5) * jnp.max(jnp.abs(q_norm_b)) * jnp.max(jnp.abs(k_norm_b))
    shift = jnp.maximum(2.0 * FOX_ROUNDING_SLACK * bound - FOX_EXP_HEADROOM, 0.0).astype(F32)
    top = 2.0 * FOX_ROUNDING_SLACK * bound - shift
    return shift, top.astype(F32), shift > FOX_MAX_SHIFT


def _post_kernel(x_ref, ya_ref, yb_ref, gmix_ref, wg_ref, bg_ref, wpa_ref, wpb_ref, wo_ref,
                 gffn_ref, wr_ref, rb_ref, x1_ref, h2_ref, topi_ref, topw_ref):
    d = x_ref.shape[1]
    x = x_ref[...]
    h = (x * lax.rsqrt(jnp.mean(x * x, axis=-1, keepdims=True) + EPS) * gmix_ref[...]).astype(BF16)
    gates = jax.nn.sigmoid(_dot(h, wg_ref[...]) + bg_ref[...])
    merged = gates[:, :d] * _dot(ya_ref[...], wpa_ref[...]) + gates[:, d:] * _dot(yb_ref[...], wpb_ref[...])
    x1 = x + _dot(merged.astype(BF16), wo_ref[...])
    x1_ref[...] = x1
    h2 = x1 * lax.rsqrt(jnp.mean(x1 * x1, axis=-1, keepdims=True) + EPS) * gffn_ref[...]
    h2_ref[...] = h2.astype(BF16)

    hh, hm, _ = _split3(h2)
    wr = wr_ref[...]
    wh = wr.astype(BF16)
    wl = (wr - wh.astype(F32)).astype(BF16)
    hh, hm = hh.astype(BF16), hm.astype(BF16)
    logits = _dot_nt(wh, hh) + _dot_nt(wh, hm) + _dot_nt(wl, hh)
    scores = jax.nn.sigmoid(logits)
    biased = scores + rb_ref[...]
    eid = lax.broadcasted_iota(jnp.int32, scores.shape, 0).astype(F32)
    chosen = jnp.zeros(scores.shape, jnp.bool_)
    idx, val = [], []
    for _ in range(TOP_K):
        cur = jnp.where(chosen, -jnp.inf, biased)
        mx = jnp.max(cur, axis=0, keepdims=True)
        first = jnp.min(jnp.where(cur == mx, eid, float(N_EXPERTS)), axis=0, keepdims=True)
        pick = eid == first
        chosen = jnp.logical_or(chosen, pick)
        idx.append(first)
        val.append(jnp.sum(jnp.where(pick, scores, 0.0), axis=0, keepdims=True))
    top_s = jnp.concatenate(val, axis=0)
    top_w = top_s / jnp.sum(top_s, axis=0, keepdims=True) * ROUTE_SCALE
    tm = scores.shape[1]
    both = jnp.concatenate(idx + [top_w, jnp.zeros((LANES - 2 * TOP_K, tm), F32)], axis=0).T
    topi_ref[...] = both[:, :TOP_K].astype(jnp.int32)
    topw_ref[...] = both[:, TOP_K:2 * TOP_K]


def _post(xf, ya, yb, g_mix, w_gate, b_gate, w_proj_a, w_proj_b, w_out, g_ffn, w_router, router_bias):
    n, d = xf.shape
    tm = TM_POST
    const = lambda shape: pl.BlockSpec(shape, lambda i: (0,) * len(shape))
    row = lambda w: pl.BlockSpec((tm, w), lambda i: (i, 0))
    args = [xf, ya, yb, g_mix.reshape(1, d), w_gate.astype(BF16), b_gate.reshape(1, 2 * d),
            w_proj_a.astype(BF16), w_proj_b.astype(BF16), w_out.astype(BF16), g_ffn.reshape(1, d),
            w_router.astype(F32).T, router_bias.astype(F32).reshape(N_EXPERTS, 1)]
    in_specs = [row(d), row(OUT_A), row(OUT_B)] + [const(a.shape) for a in args[3:]]
    return pl.pallas_call(
        _post_kernel,
        grid=(n // tm,),
        in_specs=in_specs,
        out_specs=[row(d), row(d), row(TOP_K), row(TOP_K)],
        out_shape=[jax.ShapeDtypeStruct((n, d), F32), jax.ShapeDtypeStruct((n, d), BF16),
                   jax.ShapeDtypeStruct((n, TOP_K), jnp.int32), jax.ShapeDtypeStruct((n, TOP_K), F32)],
        compiler_params=pltpu.CompilerParams(
            dimension_semantics=("arbitrary",), vmem_limit_bytes=VMEM_LIMIT),
        name="post",
    )(*args)


def _dispatch_kernel(h2_ref, topi_ref, tri_ref, upper_ref, xs_ref, slots_ref, cnt_ref, off_ref):
    tb = h2_ref.shape[0]
    topi = topi_ref[...]
    lane = lax.broadcasted_iota(jnp.int32, (tb, N_EXPERTS), 1)
    picks = [lane == topi[:, k:k + 1] for k in range(TOP_K)]
    mask = picks[0]
    for pk in picks[1:]:
        mask = jnp.logical_or(mask, pk)
    maskf = jnp.where(mask, 1.0, 0.0)
    rank = _dot(tri_ref[...], maskf.astype(BF16))
    cnt = jnp.sum(maskf, axis=0, keepdims=True)
    gran = jnp.floor((cnt + (GRAN - 1)) * (1.0 / GRAN))
    goff = _dot(jnp.broadcast_to(gran, (8, N_EXPERTS)).astype(BF16), upper_ref[...])[0:1]
    off = goff * GRAN
    slot_te = off + rank
    slots = jnp.concatenate(
        [jnp.sum(jnp.where(pk, slot_te, 0.0), axis=-1, keepdims=True) for pk in picks], axis=1)
    slots_ref[...] = slots.astype(jnp.int32)
    cnt_ref[0] = cnt.astype(jnp.int32)
    off_ref[0] = off.astype(jnp.int32)
    v = jnp.where(mask, slot_te, float(NO_SLOT))
    v_hi = jnp.floor(v * (1.0 / SLOT_RADIX))
    w = jnp.concatenate([v_hi * SLOT_RADIX, v - v_hi * SLOT_RADIX], axis=1).T.astype(BF16)
    end = off + gran * GRAN
    used = jnp.max(end).astype(jnp.int32)
    h2 = h2_ref[...]

    def lookup(first, rows):
        s_e = (lax.broadcasted_iota(jnp.int32, (rows, N_EXPERTS), 0) + first).astype(F32)
        own = jnp.where(jnp.logical_and(s_e >= off, s_e < end), 1.0, 0.0)
        return _dot(jnp.concatenate([own, own], axis=1).astype(BF16), w)

    def sort_chunk(c, looked):
        s_t = (lax.broadcasted_iota(jnp.int32, (SLOT_CHUNK, tb), 0) + c * SLOT_CHUNK).astype(F32)
        onehot = jnp.where(looked == s_t, 1.0, 0.0).astype(BF16)
        xs_ref[0, c * SLOT_CHUNK:(c + 1) * SLOT_CHUNK, :] = _dot(onehot, h2).astype(BF16)

    looked_typ = lookup(0, CHUNKS_TYPICAL * SLOT_CHUNK)
    for c in range(CAP // SLOT_CHUNK):
        if c < CHUNKS_TYPICAL:
            sort_chunk(c, looked_typ[c * SLOT_CHUNK:(c + 1) * SLOT_CHUNK, :])
        else:
            @pl.when(c * SLOT_CHUNK < used)
            def _(c=c):
                sort_chunk(c, lookup(c * SLOT_CHUNK, SLOT_CHUNK))

            @pl.when(c * SLOT_CHUNK >= used)
            def _(c=c):
                xs_ref[0, c * SLOT_CHUNK:(c + 1) * SLOT_CHUNK, :] = jnp.zeros((SLOT_CHUNK, xs_ref.shape[2]), BF16)


def _dispatch(h2, topi):
    n, d = h2.shape
    nb = n // TB
    tri = jnp.asarray(np.tril(np.ones((TB, TB), np.float32), -1), BF16)
    upper = jnp.asarray(np.triu(np.ones((N_EXPERTS, N_EXPERTS), np.float32), 1), BF16)
    const = lambda shape: pl.BlockSpec(shape, lambda i: (0,) * len(shape))
    meta = pl.BlockSpec((1, 1, N_EXPERTS), lambda i: (i, 0, 0))
    return pl.pallas_call(
        _dispatch_kernel,
        grid=(nb,),
        in_specs=[pl.BlockSpec((TB, d), lambda i: (i, 0)), pl.BlockSpec((TB, TOP_K), lambda i: (i, 0)),
                  const(tri.shape), const(upper.shape)],
        out_specs=[pl.BlockSpec((1, CAP, d), lambda i: (i, 0, 0)),
                   pl.BlockSpec((TB, TOP_K), lambda i: (i, 0)), meta, meta],
        out_shape=[jax.ShapeDtypeStruct((nb, CAP, d), BF16), jax.ShapeDtypeStruct((n, TOP_K), jnp.int32),
                   jax.ShapeDtypeStruct((nb, 1, N_EXPERTS), jnp.int32),
                   jax.ShapeDtypeStruct((nb, 1, N_EXPERTS), jnp.int32)],
        compiler_params=pltpu.CompilerParams(
            dimension_semantics=("arbitrary",), vmem_limit_bytes=VMEM_LIMIT),
        name="dispatch",
    )(h2, topi, tri, upper)


def _ffn_kernel(item_e_sm, item_d0_sm, item_nd_sm, item_s0_sm, item_ns_sm, dlist_sm, slist_sm,
                xs_hbm, wg_ref, wu_ref, wd_ref, ys_hbm, xbuf, ybuf, wg_bf, wu_bf, wd_bf, sem_in, sem_out):
    step = pl.program_id(0)
    nsteps = pl.num_programs(0)
    buf = step % 2

    def granules(st):
        return 2 * item_nd_sm[st] + item_ns_sm[st]

    def for_copies(st, fn):
        d0, nd = item_d0_sm[st], item_nd_sm[st]
        s0, ns = item_s0_sm[st], item_ns_sm[st]

        def pair(j, carry):
            fn(dlist_sm[d0 + j], 2 * j, 2)
            return carry

        def single(j, carry):
            fn(slist_sm[s0 + j], 2 * nd + j, 1)
            return carry

        lax.fori_loop(0, nd, pair, 0)
        lax.fori_loop(0, ns, single, 0)

    def fetch(b_):
        return lambda src, dst, k: pltpu.make_async_copy(
            xs_hbm.at[pl.ds(src, k)], xbuf.at[b_, pl.ds(dst, k)], sem_in.at[b_])

    def writeback(b_):
        return lambda src, dst, k: pltpu.make_async_copy(
            ybuf.at[b_, pl.ds(dst, k)], ys_hbm.at[pl.ds(src, k)], sem_out.at[b_])

    def start(mk):
        return lambda src, dst, k: mk(src, dst, k).start()

    def wait_all(st, span):
        n = granules(st)
        size = PASS_GRAN
        while size >= 1:
            @pl.when((n & size) != 0)
            def _(size=size):
                span(size).wait()
            size //= 2
        return n

    def fetch_span(b_):
        return lambda k: pltpu.make_async_copy(
            xs_hbm.at[pl.ds(0, k)], xbuf.at[b_, pl.ds(0, k)], sem_in.at[b_])

    def writeback_span(b_):
        return lambda k: pltpu.make_async_copy(
            ybuf.at[b_, pl.ds(0, k)], ys_hbm.at[pl.ds(0, k)], sem_out.at[b_])

    @pl.when(step == 0)
    def _():
        xbuf[...] = jnp.zeros_like(xbuf)
        for_copies(step, start(fetch(0)))

    @pl.when(step + 1 < nsteps)
    def _():
        for_copies(step + 1, start(fetch(1 - buf)))

    ngran = wait_all(step, fetch_span(buf))

    @pl.when(step >= 2)
    def _():
        wait_all(step - 2, writeback_span(buf))

    @pl.when(ngran > 0)
    def _():
        wg_bf[...] = wg_ref[0].astype(BF16)
        wu_bf[...] = wu_ref[0].astype(BF16)
        wd_bf[...] = wd_ref[0].astype(BF16)

    x_cols = xbuf.shape[-1]

    def ffn_rows(base, rows):
        grans = pl.ds(pl.multiple_of(base // GRAN, rows // GRAN), rows // GRAN)
        x = xbuf[buf, grans].reshape(rows, x_cols)
        g = _dot(x, wg_bf[...])
        u = _dot(x, wu_bf[...])
        mid = (g * jax.nn.sigmoid(g) * u).astype(BF16)
        ybuf[buf, grans] = _dot(mid, wd_bf[...]).astype(BF16).reshape(rows // GRAN, GRAN, x_cols)

    nt = (ngran * GRAN + (FT - 1)) // FT
    big = FT_BIG // FT

    def big_tile(i, carry):
        ffn_rows(pl.multiple_of(i * FT_BIG, FT_BIG), FT_BIG)
        return carry

    lax.fori_loop(0, nt // big, big_tile, 0)
    size = big // 2
    while size >= 1:
        @pl.when((nt & size) != 0)
        def _(size=size):
            ffn_rows(pl.multiple_of((nt & ~(2 * size - 1)) * FT, size * FT), size * FT)
        size //= 2

    for_copies(step, start(writeback(buf)))

    @pl.when(step == nsteps - 1)
    def _():
        wait_all(step, writeback_span(buf))

        @pl.when(step >= 1)
        def _():
            wait_all(step - 1, writeback_span(1 - buf))


def _work_items(cnt, off):
    nb = cnt.shape[0]
    nseg = N_EXPERTS * nb
    i32 = jnp.int32
    seg_n = ((cnt.reshape(nb, N_EXPERTS) + (GRAN - 1)) // GRAN).T
    seg_row = ((off.reshape(nb, N_EXPERTS) + jnp.arange(nb, dtype=i32)[:, None] * CAP) // GRAN).T.reshape(-1)
    before = jnp.cumsum(seg_n, axis=1) - seg_n
    pass_id = (before // (PASS_GRAN - TB // GRAN)).reshape(-1)
    seg_n = seg_n.reshape(-1)
    nd, ns = seg_n // 2, seg_n % 2
    gmax = nb * (TB * TOP_K // GRAN + N_EXPERTS)

    d_end = jnp.cumsum(nd)
    d_start = d_end - nd
    prev_end = jnp.concatenate([jnp.full((1,), 2, i32), (seg_row + 2 * nd)[:-1]])
    dlist = jnp.cumsum(jnp.full((gmax // 2,), 2, i32).at[d_start].add(seg_row - prev_end, mode="drop"))
    s_end = jnp.cumsum(ns)
    s_start = s_end - ns
    slist = jnp.zeros((nseg,), i32).at[jnp.where(ns == 1, s_start, nseg)].add(seg_row + 2 * nd, mode="drop")

    seg = jnp.arange(nseg, dtype=i32)
    new_item = jnp.logical_or(seg % nb == 0, pass_id != jnp.concatenate([pass_id[:1], pass_id[:-1]]))
    item_of_seg = jnp.cumsum(new_item.astype(i32)) - 1
    n_items = N_EXPERTS + gmax // (PASS_GRAN - TB // GRAN)
    first_seg = jnp.full((n_items + 1,), nseg, i32).at[
        jnp.where(new_item, item_of_seg, n_items + 1)].add(seg - nseg, mode="drop")
    lo, hi = first_seg[:-1], first_seg[1:]
    d_bound = jnp.concatenate([d_start, d_end[-1:]])
    s_bound = jnp.concatenate([s_start, s_end[-1:]])
    item_e = jnp.minimum(lo // nb, N_EXPERTS - 1)
    return (item_e.astype(i32), d_bound[lo], d_bound[hi] - d_bound[lo], s_bound[lo], s_bound[hi] - s_bound[lo],
            dlist.astype(i32), slist)


def _ffn(xs, cnt, off, wg, wu, wd):
    _, _, d = xs.shape
    items = _work_items(cnt, off)
    per_expert = lambda shape: pl.BlockSpec((1,) + shape, lambda w, ie, *_: (ie[w], 0, 0))
    grid_spec = pltpu.PrefetchScalarGridSpec(
        num_scalar_prefetch=len(items),
        grid=(items[0].shape[0],),
        in_specs=[pl.BlockSpec(memory_space=pl.ANY), per_expert((d, D_EXPERT)), per_expert((d, D_EXPERT)),
                  per_expert((D_EXPERT, d))],
        out_specs=pl.BlockSpec(memory_space=pl.ANY),
        scratch_shapes=[pltpu.VMEM((2, PASS_GRAN, GRAN, d), BF16)] * 2 + [
                        pltpu.VMEM((d, D_EXPERT), BF16), pltpu.VMEM((d, D_EXPERT), BF16),
                        pltpu.VMEM((D_EXPERT, d), BF16),
                        pltpu.SemaphoreType.DMA((2,)), pltpu.SemaphoreType.DMA((2,))],
    )
    return pl.pallas_call(
        _ffn_kernel,
        grid_spec=grid_spec,
        out_shape=jax.ShapeDtypeStruct(xs.shape, xs.dtype),
        input_output_aliases={len(items): 0},
        compiler_params=pltpu.CompilerParams(
            dimension_semantics=("arbitrary",), vmem_limit_bytes=VMEM_LIMIT),
        name="ffn",
    )(*items, xs, wg, wu, wd)


def _combine_kernel(used_sm, tail_sm, x1_ref, h2_ref, *rest):
    n_chunks = CAP // SLOT_CHUNK
    ys_refs = rest[:n_chunks]
    slots_ref, topw_ref, wgus_ref, wds_ref, o_ref = rest[n_chunks:]
    tb = x1_ref.shape[0]
    gu = _dot(h2_ref[...], wgus_ref[...])
    g, u = gu[:, :D_SHARED], gu[:, D_SHARED:]
    acc = x1_ref[...] + _dot((g * jax.nn.sigmoid(g) * u).astype(BF16), wds_ref[...])
    slots = slots_ref[...].astype(F32)
    topw = topw_ref[...]
    used = used_sm[pl.program_id(0)]

    def gather_chunk(c):
        scol = (lax.broadcasted_iota(jnp.int32, (tb, SLOT_CHUNK), 1) + c * SLOT_CHUNK).astype(F32)
        gate = jnp.zeros((tb, SLOT_CHUNK), F32)
        for k in range(TOP_K):
            gate = jnp.where(scol == slots[:, k:k + 1], topw[:, k:k + 1], gate)
        return _dot(gate.astype(BF16), ys_refs[c][0])

    for c in range(CHUNKS_TYPICAL):
        acc = acc + gather_chunk(c)
    o_ref[...] = acc
    for c in range(CHUNKS_TYPICAL, CAP // SLOT_CHUNK):
        @pl.when(c * SLOT_CHUNK < used)
        def _(c=c):
            o_ref[...] += gather_chunk(c)


def _combine(x1, h2, ys, slots, topw, used, wgus, wds):
    n, d = x1.shape
    n_chunks = CAP // SLOT_CHUNK
    const = lambda shape: pl.BlockSpec(shape, lambda i, u, t: (0,) * len(shape))
    row = lambda w: pl.BlockSpec((TB, w), lambda i, u, t: (i, 0))
    blocks = jnp.arange(n // TB, dtype=jnp.int32)
    tails = [lax.cummax(jnp.where(used > c * SLOT_CHUNK, blocks, 0)) for c in range(CHUNKS_TYPICAL, n_chunks)]
    tail_idx = jnp.stack(tails).astype(jnp.int32)
    n_blocks = n // TB

    def chunk_spec(c):
        if c < CHUNKS_TYPICAL:
            return pl.BlockSpec((1, SLOT_CHUNK, d), lambda i, u, t: (i * n_chunks + c, 0, 0))
        late = c - CHUNKS_TYPICAL
        return pl.BlockSpec((1, SLOT_CHUNK, d), lambda i, u, t: (t[late * n_blocks + i] * n_chunks + c, 0, 0))

    grid_spec = pltpu.PrefetchScalarGridSpec(
        num_scalar_prefetch=2,
        grid=(n_blocks,),
        in_specs=[row(d), row(d)] + [chunk_spec(c) for c in range(n_chunks)] + [
            row(TOP_K), row(TOP_K), const(wgus.shape), const(wds.shape)],
        out_specs=row(d),
    )
    ys_chunks = ys.reshape(n_blocks * n_chunks, SLOT_CHUNK, d)
    return pl.pallas_call(
        _combine_kernel,
        grid_spec=grid_spec,
        out_shape=jax.ShapeDtypeStruct((n, d), F32),
        compiler_params=pltpu.CompilerParams(
            dimension_semantics=("arbitrary",), vmem_limit_bytes=VMEM_LIMIT),
        name="combine",
    )(used, tail_idx.reshape(-1), x1, h2, *([ys_chunks] * n_chunks), slots, topw, wgus, wds)


def _moe(x1, h2, topi, topw, w_gate_e, w_up_e, w_down_e, w_gate_s, w_up_s, w_down_s):
    n, d = x1.shape
    wgus = jnp.concatenate([w_gate_s.astype(BF16), w_up_s.astype(BF16)], axis=-1)
    xs, slots, cnt, off = _dispatch(h2, topi)
    ys = _ffn(xs.reshape(-1, GRAN, d), cnt, off, w_gate_e, w_up_e, w_down_e)
    used = jnp.max(off + (cnt + (GRAN - 1)) // GRAN * GRAN, axis=(1, 2)).astype(jnp.int32)
    return _combine(x1, h2, ys.reshape(n // TB, CAP, d), slots, topw, used, wgus, w_down_s.astype(BF16))


def kernel(x, g_mix, w_in, q_norm_a, k_norm_a, q_norm_b, k_norm_b, rel_bias, b_forget, w_gate, b_gate,
           w_proj_a, w_proj_b, w_out, g_ffn, w_router, router_bias, w_gate_e, w_up_e, w_down_e,
           w_gate_s, w_up_s, w_down_s):
    batch, seq, d = x.shape
    xf = x.reshape(batch * seq, d)
    fox_shift, fox_top, fox_online = _fox_shift(q_norm_b, k_norm_b)
    pa0, pa1, pa2, qkb, vb, cb = _inproj(xf, g_mix, w_in, q_norm_a, k_norm_a, q_norm_b, k_norm_b, b_forget,
                                     fox_shift, seq)
    bias = jnp.stack([_toeplitz_bias(rel_bias, g, dil) for g, (_, dil) in enumerate(DIL_GROUPS)])
    ya = _dilated(pa0, pa1, pa2, bias, batch, seq)

    yb = _fox(qkb, vb, cb, fox_top, fox_online, batch, seq)
    x1, h2, topi, topw = _post(xf, ya, yb, g_mix, w_gate, b_gate, w_proj_a, w_proj_b, w_out, g_ffn,
                               w_router, router_bias)
    out = _moe(x1, h2, topi, topw, w_gate_e, w_up_e, w_down_e, w_gate_s, w_up_s, w_down_s)
    return out.reshape(batch, seq, d)
```

```python
import functools
import math

import jax
import jax.numpy as jnp
import numpy as np
from jax import lax
from jax.experimental import pallas as pl
from jax.experimental.pallas import tpu as pltpu

HEAD_DIM = 64
DIL_GROUPS = ((128, 1), (512, 4), (2048, 16))
HEADS_PER_GROUP = 4
N_HEADS_A = HEADS_PER_GROUP * len(DIL_GROUPS)
N_HEADS_B = 8
REL_BUCKETS = 32
REL_MAX_DIST = 2048
N_EXPERTS = 64
TOP_K = 8
D_EXPERT = 256
D_SHARED = 256
ROUTE_SCALE = 2.5
EPS = 1e-6

WIDTH_A = 3 * N_HEADS_A * HEAD_DIM
WIDTH_B = 3 * N_HEADS_B * HEAD_DIM
QK_B = N_HEADS_B * HEAD_DIM
OUT_A = HEADS_PER_GROUP * HEAD_DIM
OUT_B = N_HEADS_B * HEAD_DIM

LANES = 128
GROUP_W = HEADS_PER_GROUP * HEAD_DIM
WIN_J = 128
SUPER = DIL_GROUPS[-1][1] * WIN_J
NEG = -1e30
VMEM_LIMIT = 56 * 1024 * 1024

DIL_UNROLL = 16
TM_IN = 512
TM_POST = 1024
TQ_FOX = 1024
FOX_ROUNDING_SLACK = 1.02
FOX_EXP_HEADROOM = 60.0
FOX_DEAD_EXPONENT = -105.0
FOX_MAX_SHIFT = 80.0
TB = 256
GRAN = 16
CAP = TB * TOP_K + N_EXPERTS * GRAN
SLOT_CHUNK = 512
CHUNKS_TYPICAL = -(-(TB * TOP_K + N_EXPERTS * GRAN // 2) // SLOT_CHUNK)
SLOT_RADIX = 64
NO_SLOT = SLOT_RADIX ** 2 - 1
assert CAP <= NO_SLOT
PASS_GRAN = 256
FT = 256
FT_BIG = 2048

BF16 = jnp.bfloat16
F32 = jnp.float32


def _dot(a, b):
    return jnp.dot(a, b, preferred_element_type=F32)


def _dot_nt(a, b):
    return lax.dot_general(a, b, (((1,), (1,)), ((), ())), preferred_element_type=F32)


def _split3(v):
    hi = v.astype(BF16).astype(F32)
    r = v - hi
    mid = r.astype(BF16).astype(F32)
    lo = (r - mid).astype(BF16).astype(F32)
    return hi, mid, lo


def _inproj_kernel(x_ref, g_ref, wa_ref, wb_ref, wf_ref, bd_ref, tri_ref, gain_a_ref, gain_b_ref,
                   bf_ref, shift_ref, pa0_ref, pa1_ref, pa2_ref, qkb_ref, vb_ref, cb_ref, carry_ref, h_ref, *,
                   tiles_per_seq):
    tm = x_ref.shape[0]
    x = x_ref[...]
    h = x * lax.rsqrt(jnp.mean(x * x, axis=-1, keepdims=True) + EPS) * g_ref[...]
    n_lane_chunks = h_ref.shape[0]
    for c in range(n_lane_chunks):
        h_ref[c] = h[:, c * LANES:(c + 1) * LANES]
    h = h.astype(BF16)
    bd = bd_ref[...]

    def headnorm(p, gain):
        ms = _dot((p * p).astype(BF16), bd)
        return p * lax.rsqrt(ms + EPS) * gain

    for g, (pa_ref, (_, dil)) in enumerate(zip((pa0_ref, pa1_ref, pa2_ref), DIL_GROUPS)):
        rows = tm // dil
        if dil == 1:
            hg = h
        else:
            hg = jnp.concatenate([jnp.concatenate(
                [h_ref[c, pl.ds(r, rows, stride=dil), :] for c in range(n_lane_chunks)], axis=1)
                for r in range(dil)], axis=0).astype(BF16)
        qkv = _dot(hg, wa_ref[g])
        for part in range(3):
            cols = slice(part * GROUP_W, (part + 1) * GROUP_W)
            p = qkv[:, cols]
            if part < 2:
                p = headnorm(p, gain_a_ref[part:part + 1, :])
            p = p.astype(BF16)
            if dil == 1:
                pa_ref[:, cols] = p
            else:
                for r in range(dil):
                    pa_ref[0, r, :, cols] = p[r * rows:(r + 1) * rows, :]

    f = _dot(h, wf_ref[...]) + bf_ref[...]
    logf = jnp.minimum(f, 0.0) - jnp.log1p(jnp.exp(-jnp.abs(f)))
    tri = tri_ref[...]
    lh, lm, ll = _split3(logf)
    cum3 = _dot(tri, jnp.concatenate([lh, lm, ll], axis=1).astype(BF16))
    cum = cum3[:, :LANES] + cum3[:, LANES:2 * LANES] + cum3[:, 2 * LANES:]

    @pl.when(pl.program_id(0) % tiles_per_seq == 0)
    def _():
        carry_ref[...] = jnp.zeros_like(carry_ref)

    cum = cum + carry_ref[0:1, :]
    carry_ref[0:1, :] = cum[tm - 1:tm, :]
    cb_ref[0] = jnp.concatenate([cum[0:1, :], cum[tm - 1:tm, :], jnp.zeros((6, LANES), F32)], axis=0)
    ch, cm, cl = _split3(cum)

    j = lax.broadcasted_iota(jnp.int32, (tm, HEAD_DIM), 1)

    def ext_cols(vals):
        out = jnp.zeros((tm, HEAD_DIM), F32)
        ones = [pos for pos, val in enumerate(vals) if isinstance(val, float)]
        if ones:
            is_one = functools.reduce(jnp.logical_or, [j == pos for pos in ones])
            out = jnp.where(is_one, 1.0, out)
        for pos, val in enumerate(vals):
            if not isinstance(val, float):
                out = jnp.where(j == pos, val, out)
        return out

    for c in range(QK_B // GROUP_W):
        qkv = _dot(h, wb_ref[c])
        pq = headnorm(qkv[:, 0:GROUP_W], gain_b_ref[0:1, :])
        pk = headnorm(qkv[:, GROUP_W:2 * GROUP_W], gain_b_ref[1:2, :])
        pv = qkv[:, 2 * GROUP_W:3 * GROUP_W]
        r = _dot((pq * pk).astype(BF16), bd) * HEAD_DIM + shift_ref[...]
        for hh in range(HEADS_PER_GROUP):
            head = c * HEADS_PER_GROUP + hh
            lanes = slice(hh * HEAD_DIM, (hh + 1) * HEAD_DIM)
            col = lambda a, idx: a[:, idx:idx + 1]
            cs = [col(ch, head), col(cm, head), col(cl, head)]
            ext_q = ext_cols(cs + [1.0] * 3 + [-col(r, hh * HEAD_DIM)])
            ext_k = ext_cols([1.0] * 3 + [-v for v in cs] + [1.0])
            ext_v = ext_cols([1.0])
            for part, (val, ext) in enumerate(((pq, ext_q), (pk, ext_k))):
                o0 = (part * N_HEADS_B + head) * LANES
                qkb_ref[:, o0:o0 + LANES] = jnp.concatenate([val[:, lanes], ext], axis=-1).astype(BF16)
            vb_ref[:, head * LANES:(head + 1) * LANES] = jnp.concatenate(
                [pv[:, lanes], ext_v], axis=-1).astype(BF16)


def _inproj(xf, g_mix, w_in, q_norm_a, k_norm_a, q_norm_b, k_norm_b, b_forget, fox_shift, seq):
    n, d = xf.shape
    tm = TM_IN
    scale = HEAD_DIM ** -0.5
    w_bf = w_in.astype(BF16)
    qkv_w = N_HEADS_A * HEAD_DIM
    wa = jnp.stack([jnp.concatenate(
        [w_bf[:, part * qkv_w + g * GROUP_W: part * qkv_w + (g + 1) * GROUP_W] for part in range(3)],
        axis=1) for g in range(len(DIL_GROUPS))])
    wb = jnp.stack([jnp.concatenate(
        [w_bf[:, WIDTH_A + part * QK_B + c * GROUP_W: WIDTH_A + part * QK_B + (c + 1) * GROUP_W]
         for part in range(3)], axis=1) for c in range(QK_B // GROUP_W)])
    wf = jnp.pad(w_bf[:, WIDTH_A + WIDTH_B:], ((0, 0), (0, LANES - N_HEADS_B)))
    bfp = jnp.pad(b_forget.astype(F32), (0, LANES - N_HEADS_B)).reshape(1, LANES)
    seg = np.arange(GROUP_W) // HEAD_DIM
    bd = jnp.asarray((seg[:, None] == seg[None, :]).astype(np.float32) / HEAD_DIM, BF16)
    tri = jnp.asarray(np.tril(np.ones((tm, tm), np.float32)), BF16)
    gain_a = jnp.stack([jnp.tile(q_norm_a, HEADS_PER_GROUP) * scale, jnp.tile(k_norm_a, HEADS_PER_GROUP)])
    gain_b = jnp.stack([jnp.tile(q_norm_b, HEADS_PER_GROUP) * scale, jnp.tile(k_norm_b, HEADS_PER_GROUP)])
    const = lambda shape: pl.BlockSpec(shape, lambda i: (0,) * len(shape))
    tps = seq // tm
    batch = n // seq
    qkv3 = 3 * GROUP_W
    (_, d1), (_, d2) = DIL_GROUPS[1], DIL_GROUPS[2]
    return pl.pallas_call(
        functools.partial(_inproj_kernel, tiles_per_seq=tps),
        grid=(n // tm,),
        in_specs=[
            pl.BlockSpec((tm, d), lambda i: (i, 0)),
            const((1, d)), const(wa.shape), const(wb.shape), const(wf.shape),
            const(bd.shape), const(tri.shape), const(gain_a.shape), const(gain_b.shape),
            const(bfp.shape), const((1, 1)),
        ],
        out_specs=[
            pl.BlockSpec((tm, qkv3), lambda i: (i, 0)),
            pl.BlockSpec((1, d1, tm // d1, qkv3), lambda i: (i // tps, 0, i % tps, 0)),
            pl.BlockSpec((1, d2, tm // d2, qkv3), lambda i: (i // tps, 0, i % tps, 0)),
            pl.BlockSpec((tm, 2 * N_HEADS_B * LANES), lambda i: (i, 0)),
            pl.BlockSpec((tm, N_HEADS_B * LANES), lambda i: (i, 0)),
            pl.BlockSpec((1, 8, LANES), lambda i: (i, 0, 0)),
        ],
        out_shape=[
            jax.ShapeDtypeStruct((n, qkv3), BF16),
            jax.ShapeDtypeStruct((batch, d1, seq // d1, qkv3), BF16),
            jax.ShapeDtypeStruct((batch, d2, seq // d2, qkv3), BF16),
            jax.ShapeDtypeStruct((n, 2 * N_HEADS_B * LANES), BF16),
            jax.ShapeDtypeStruct((n, N_HEADS_B * LANES), BF16),
            jax.ShapeDtypeStruct((n // tm, 8, LANES), F32),
        ],
        scratch_shapes=[pltpu.VMEM((8, LANES), F32), pltpu.VMEM((d // LANES, tm, LANES), F32)],
        compiler_params=pltpu.CompilerParams(
            dimension_semantics=("arbitrary",), vmem_limit_bytes=VMEM_LIMIT),
        name="inproj",
    )(xf, g_mix.reshape(1, d), wa, wb, wf, bd, tri, gain_a, gain_b, bfp, fox_shift.reshape(1, 1))


def _dilated_kernel(p0_ref, h0_ref, p1_ref, h1_ref, p2_ref, h2_ref, bias_ref, o_ref, acc_ref, lse_ref):
    tq = WIN_J
    first_sb = pl.program_id(1) == 0
    lane_head = lax.broadcasted_iota(jnp.int32, (tq, GROUP_W), 1) // HEAD_DIM
    prev_col = lax.broadcasted_iota(jnp.int32, (tq, 2 * tq), 1) < tq
    qc, kc_, vc_ = (slice(0, GROUP_W), slice(GROUP_W, 2 * GROUP_W), slice(2 * GROUP_W, 3 * GROUP_W))

    def attend(g, q, kp, kc, vp, vc, no_prev):
        kcat = jnp.concatenate([kp, kc], axis=0)
        vcat = jnp.concatenate([vp, vc], axis=0)
        dead = jnp.logical_and(no_prev, prev_col)
        q4 = jnp.concatenate([jnp.where(lane_head == hh, q, jnp.zeros_like(q))
                              for hh in range(HEADS_PER_GROUP)], axis=0)
        s = _dot_nt(q4, kcat) + bias_ref[g].reshape(HEADS_PER_GROUP * tq, 2 * tq)
        s = jnp.where(jnp.concatenate([dead] * HEADS_PER_GROUP, axis=0), NEG, s)
        m = jnp.max(s, axis=-1, keepdims=True)
        p = jnp.exp(s - m)
        l = jnp.sum(p, axis=-1, keepdims=True)
        o4 = _dot(p.astype(BF16), vcat) * (1.0 / l)
        lse4 = m + jnp.log(l)
        acc = o4[0:tq]
        lse = jnp.broadcast_to(lse4[0:tq], (tq, GROUP_W))
        for hh in range(1, HEADS_PER_GROUP):
            sel = lane_head == hh
            acc = jnp.where(sel, o4[hh * tq:(hh + 1) * tq], acc)
            lse = jnp.where(sel, lse4[hh * tq:(hh + 1) * tq], lse)
        return acc, lse

    n_half = GROUP_W // LANES

    def merge(rows, acc, lse):
        for c in range(n_half):
            lanes = slice(c * LANES, (c + 1) * LANES)
            l1 = lse_ref[c, rows, :]
            mx = jnp.maximum(l1, lse[:, lanes])
            w1 = jnp.exp(l1 - mx)
            w2 = jnp.exp(lse[:, lanes] - mx)
            den = w1 + w2
            acc_ref[c, rows, :] = (w1 * acc_ref[c, rows, :] + w2 * acc[:, lanes]) / den
            lse_ref[c, rows, :] = mx + jnp.log(den)

    def pick(first, halo, body):
        return jnp.where(first, halo, body)

    def loop(n, body):
        def trip(i, carry):
            for u in range(DIL_UNROLL):
                body(i * DIL_UNROLL + u, carry)
            return carry
        lax.fori_loop(0, n // DIL_UNROLL, trip, 0)

    def body0(j, carry):
        st = pl.multiple_of(j * tq, tq)
        pst = pl.multiple_of(jnp.maximum(j - 1, 0) * tq, tq)
        cur, prv = pl.ds(st, tq), pl.ds(pst, tq)
        acc, lse = attend(
            0, p0_ref[0, cur, qc],
            pick(j == 0, h0_ref[0, :, kc_], p0_ref[0, prv, kc_]), p0_ref[0, cur, kc_],
            pick(j == 0, h0_ref[0, :, vc_], p0_ref[0, prv, vc_]), p0_ref[0, cur, vc_],
            jnp.logical_and(j == 0, first_sb))
        for c in range(n_half):
            acc_ref[c, cur, :] = acc[:, c * LANES:(c + 1) * LANES]
            lse_ref[c, cur, :] = lse[:, c * LANES:(c + 1) * LANES]
        return carry

    loop(SUPER // tq, body0)

    d1 = DIL_GROUPS[1][1]
    nsub1 = SUPER // d1 // tq
    def body1(t, carry):
        r, ii = t // nsub1, t % nsub1
        st = pl.multiple_of(ii * tq, tq)
        pst = pl.multiple_of(jnp.maximum(ii - 1, 0) * tq, tq)
        cur, prv = pl.ds(st, tq), pl.ds(pst, tq)
        acc, lse = attend(
            1, p1_ref[0, r, cur, qc],
            pick(ii == 0, h1_ref[0, r, :, kc_], p1_ref[0, r, prv, kc_]), p1_ref[0, r, cur, kc_],
            pick(ii == 0, h1_ref[0, r, :, vc_], p1_ref[0, r, prv, vc_]), p1_ref[0, r, cur, vc_],
            jnp.logical_and(ii == 0, first_sb))
        merge(pl.ds(ii * (tq * d1) + r, tq, stride=d1), acc, lse)
        return carry

    loop(d1 * nsub1, body1)

    d2 = DIL_GROUPS[2][1]

    def body2(r, carry):
        acc, lse = attend(2, p2_ref[0, r, :, qc], h2_ref[0, r, :, kc_], p2_ref[0, r, :, kc_],
                          h2_ref[0, r, :, vc_], p2_ref[0, r, :, vc_], first_sb)
        merge(pl.ds(r, tq, stride=d2), acc, lse)
        return carry

    loop(d2, body2)

    for c in range(n_half):
        o_ref[0, :, c * LANES:(c + 1) * LANES] = acc_ref[c].astype(o_ref.dtype)


def _rel_bucket(dist):
    max_exact = REL_BUCKETS // 2
    n = jnp.maximum(dist.astype(F32), 1.0)
    large = max_exact + (jnp.log(n / max_exact) / math.log(REL_MAX_DIST / max_exact)
                         * (REL_BUCKETS - max_exact)).astype(jnp.int32)
    large = jnp.minimum(large, REL_BUCKETS - 1)
    return jnp.where(dist < max_exact, dist, large)


def _toeplitz_bias(rel_bias, g, dil):
    tq = WIN_J
    offs = dil * (WIN_J - jnp.arange(WIN_J + 1, dtype=jnp.int32))
    hs = slice(g * HEADS_PER_GROUP, (g + 1) * HEADS_PER_GROUP)
    tab_rev = rel_bias[_rel_bucket(offs)][:, hs].T.astype(F32)
    period = 3 * tq
    neg = lambda w: jnp.full((HEADS_PER_GROUP, w), NEG, F32)
    vec = jnp.concatenate([neg(tq - 1), tab_rev, neg(period - 2 * tq)], axis=1)
    flat = jnp.broadcast_to(vec[:, None, :], (HEADS_PER_GROUP, tq, period)).reshape(HEADS_PER_GROUP, -1)
    skew = flat[:, :tq * (period - 1)].reshape(HEADS_PER_GROUP, tq, period - 1)
    return skew[:, :, tq - 1:3 * tq - 1]


def _dilated(pa0, pa1, pa2, bias, batch, seq):
    tq = WIN_J
    qkv3 = 3 * GROUP_W
    (_, d1), (_, d2) = DIL_GROUPS[1], DIL_GROUPS[2]
    nsb = seq // SUPER
    p0 = pa0.reshape(batch, seq, qkv3)
    prev_blk = lambda per_sb: (lambda b, s: jnp.maximum(s * per_sb - 1, 0))
    h0i, h1i, h2i = prev_blk(SUPER // tq), prev_blk(SUPER // d1 // tq), prev_blk(SUPER // d2 // tq)
    out = pl.pallas_call(
        _dilated_kernel,
        grid=(batch, nsb),
        in_specs=[
            pl.BlockSpec((1, SUPER, qkv3), lambda b, s: (b, s, 0)),
            pl.BlockSpec((1, tq, qkv3), lambda b, s: (b, h0i(b, s), 0)),
            pl.BlockSpec((1, d1, SUPER // d1, qkv3), lambda b, s: (b, 0, s, 0)),
            pl.BlockSpec((1, d1, tq, qkv3), lambda b, s: (b, 0, h1i(b, s), 0)),
            pl.BlockSpec((1, d2, SUPER // d2, qkv3), lambda b, s: (b, 0, s, 0)),
            pl.BlockSpec((1, d2, tq, qkv3), lambda b, s: (b, 0, h2i(b, s), 0)),
            pl.BlockSpec(bias.shape, lambda b, s: (0, 0, 0, 0)),
        ],
        out_specs=pl.BlockSpec((1, SUPER, GROUP_W), lambda b, s: (b, s, 0)),
        out_shape=jax.ShapeDtypeStruct((batch, seq, GROUP_W), BF16),
        scratch_shapes=[pltpu.VMEM((GROUP_W // LANES, SUPER, LANES), F32)] * 2,
        compiler_params=pltpu.CompilerParams(
            dimension_semantics=("arbitrary", "arbitrary"), vmem_limit_bytes=VMEM_LIMIT),
        name="dilated",
    )(p0, p0, pa1, pa1, pa2, pa2, bias)
    return out.reshape(batch * seq, GROUP_W)


def _fox_kernel(nlive_sm, q_ref, k_ref, v_ref, o_ref, m_ref, acc_ref, *, online):
    tq = q_ref.shape[1]
    half = tq // 2
    qi = pl.program_id(2)
    step = (pl.program_id(0) * pl.num_programs(1) + pl.program_id(1)) * pl.num_programs(2) + qi
    row = lax.broadcasted_iota(jnp.int32, (half, half), 0)
    col = lax.broadcasted_iota(jnp.int32, (half, half), 1)
    causal = row >= col

    def attend(hh, rows, state, start, nkeys, masked):
        m, acc = state
        lanes = slice(hh * LANES, (hh + 1) * LANES)
        s = _dot_nt(q_ref[0, rows, lanes], k_ref[0, pl.ds(start, nkeys), lanes])
        if masked:
            s = jnp.where(causal, s, NEG)
        if online:
            m_new = jnp.maximum(m, jnp.max(s, axis=-1, keepdims=True))
            acc = acc * jnp.exp(m - m_new)
            s = s - m_new
            m = m_new
        return m, acc + _dot(jnp.exp(s).astype(BF16), v_ref[0, pl.ds(start, nkeys), lanes])

    first = [qi - nlive_sm[2 * step + hh] for hh in range(2)]
    for hh in range(2):
        m_ref[hh] = jnp.full((tq, 1), NEG, F32)
        acc_ref[hh] = jnp.zeros((tq, LANES), F32)

    def full_chunk(ki, carry):
        start = pl.multiple_of(ki * tq, tq)
        for hh in range(2):
            @pl.when(ki >= first[hh])
            def _(hh=hh):
                m, acc = attend(hh, slice(None), (m_ref[hh], acc_ref[hh]), start, tq, False)
                acc_ref[hh] = acc
                if online:
                    m_ref[hh] = m
        return carry

    lax.fori_loop(jnp.minimum(first[0], first[1]), qi, full_chunk, 0)

    d0 = pl.multiple_of(qi * tq, tq)
    outs = []
    for hh in range(2):
        m, acc = m_ref[hh], acc_ref[hh]
        top, bot = slice(0, half), slice(half, tq)
        s_top = attend(hh, top, (m[top], acc[top]), d0, half, True)
        s_bot = attend(hh, bot, (m[bot], acc[bot]), d0, half, False)
        s_bot = attend(hh, bot, s_bot, d0 + half, half, True)
        a = jnp.concatenate([s_top[1], s_bot[1]], axis=0)
        outs.append(a[:, :HEAD_DIM] / a[:, HEAD_DIM:HEAD_DIM + 1])
    o_ref[0] = jnp.concatenate(outs, axis=-1).astype(o_ref.dtype)


def _fox_live_chunks(cb, top, batch, seq):
    tps = seq // TM_IN
    per = TQ_FOX // TM_IN
    nq = seq // TQ_FOX
    c_first = cb[:, 0, :N_HEADS_B].reshape(batch, tps, N_HEADS_B)[:, ::per]
    c_last = cb[:, 1, :N_HEADS_B].reshape(batch, tps, N_HEADS_B)[:, per - 1::per]
    live = (top + c_first[:, :, None, :] - c_last[:, None, :, :]) >= FOX_DEAD_EXPONENT
    back = jnp.arange(nq)[:, None] - jnp.arange(nq)[None, :]
    reach = jnp.max(jnp.where(jnp.logical_and(live, (back > 0)[None, :, :, None]),
                              back[None, :, :, None], 0), axis=2)
    reach = reach.reshape(batch, nq, N_HEADS_B // 2, 2).transpose(0, 2, 1, 3)
    return reach.reshape(-1).astype(jnp.int32)


def _fox(qkb, vb, cb, top, online, batch, seq):
    tq = TQ_FOX
    pairs = N_HEADS_B // 2
    nq = seq // tq
    qkv = qkb.reshape(batch, seq, 2 * N_HEADS_B * LANES)
    vv = vb.reshape(batch, seq, N_HEADS_B * LANES)
    all_chunks = jnp.tile(jnp.repeat(jnp.arange(nq, dtype=jnp.int32), 2), batch * pairs)

    def call(is_online, nlive):
        grid_spec = pltpu.PrefetchScalarGridSpec(
            num_scalar_prefetch=1,
            grid=(batch, pairs, nq),
            in_specs=[
                pl.BlockSpec((1, tq, 2 * LANES), lambda b, p, i, n: (b, i, p)),
                pl.BlockSpec((1, seq, 2 * LANES), lambda b, p, i, n: (b, 0, pairs + p)),
                pl.BlockSpec((1, seq, 2 * LANES), lambda b, p, i, n: (b, 0, p)),
            ],
            out_specs=pl.BlockSpec((1, tq, LANES), lambda b, p, i, n: (b, i, p)),
            scratch_shapes=[pltpu.VMEM((2, tq, 1), F32), pltpu.VMEM((2, tq, LANES), F32)],
        )
        return pl.pallas_call(
            functools.partial(_fox_kernel, online=is_online),
            grid_spec=grid_spec,
            out_shape=jax.ShapeDtypeStruct((batch, seq, OUT_B), BF16),
            compiler_params=pltpu.CompilerParams(
                dimension_semantics=("arbitrary", "arbitrary", "arbitrary"), vmem_limit_bytes=VMEM_LIMIT),
            name="fox_online" if is_online else "fox",
        )(nlive, qkv, qkv, vv)

    out = lax.cond(online, lambda: call(True, all_chunks),
                   lambda: call(False, _fox_live_chunks(cb, top, batch, seq)))
    return out.reshape(batch * seq, OUT_B)


def _fox_shift(q_norm_b, k_norm_b):
    bound = HEAD_DIM * (HEAD_DIM ** -0.5) * jnp.max(jnp.abs(q_norm_b)) * jnp.max(jnp.abs(k_norm_b))
    shift = jnp.maximum(2.0 * FOX_ROUNDING_SLACK * bound - FOX_EXP_HEADROOM, 0.0).astype(F32)
    top = 2.0 * FOX_ROUNDING_SLACK * bound - shift
    return shift, top.astype(F32), shift > FOX_MAX_SHIFT


def _post_kernel(x_ref, ya_ref, yb_ref, gmix_ref, wg_ref, bg_ref, wpa_ref, wpb_ref, wo_ref,
                 gffn_ref, wr_ref, rb_ref, x1_ref, h2_ref, topi_ref, topw_ref):
    d = x_ref.shape[1]
    x = x_ref[...]
    h = (x * lax.rsqrt(jnp.mean(x * x, axis=-1, keepdims=True) + EPS) * gmix_ref[...]).astype(BF16)
    gates = jax.nn.sigmoid(_dot(h, wg_ref[...]) + bg_ref[...])
    merged = gates[:, :d] * _dot(ya_ref[...], wpa_ref[...]) + gates[:, d:] * _dot(yb_ref[...], wpb_ref[...])
    x1 = x + _dot(merged.astype(BF16), wo_ref[...])
    x1_ref[...] = x1
    h2 = x1 * lax.rsqrt(jnp.mean(x1 * x1, axis=-1, keepdims=True) + EPS) * gffn_ref[...]
    h2_ref[...] = h2.astype(BF16)

    hh, hm, _ = _split3(h2)
    wr = wr_ref[...]
    wh = wr.astype(BF16)
    wl = (wr - wh.astype(F32)).astype(BF16)
    hh, hm = hh.astype(BF16), hm.astype(BF16)
    logits = _dot_nt(wh, hh) + _dot_nt(wh, hm) + _dot_nt(wl, hh)
    scores = jax.nn.sigmoid(logits)
    biased = scores + rb_ref[...]
    eid = lax.broadcasted_iota(jnp.int32, scores.shape, 0).astype(F32)
    chosen = jnp.zeros(scores.shape, jnp.bool_)
    idx, val = [], []
    for _ in range(TOP_K):
        cur = jnp.where(chosen, -jnp.inf, biased)
        mx = jnp.max(cur, axis=0, keepdims=True)
        first = jnp.min(jnp.where(cur == mx, eid, float(N_EXPERTS)), axis=0, keepdims=True)
        pick = eid == first
        chosen = jnp.logical_or(chosen, pick)
        idx.append(first)
        val.append(jnp.sum(jnp.where(pick, scores, 0.0), axis=0, keepdims=True))
    top_s = jnp.concatenate(val, axis=0)
    top_w = top_s / jnp.sum(top_s, axis=0, keepdims=True) * ROUTE_SCALE
    tm = scores.shape[1]
    both = jnp.concatenate(idx + [top_w, jnp.zeros((LANES - 2 * TOP_K, tm), F32)], axis=0).T
    topi_ref[...] = both[:, :TOP_K].astype(jnp.int32)
    topw_ref[...] = both[:, TOP_K:2 * TOP_K]


def _post(xf, ya, yb, g_mix, w_gate, b_gate, w_proj_a, w_proj_b, w_out, g_ffn, w_router, router_bias):
    n, d = xf.shape
    tm = TM_POST
    const = lambda shape: pl.BlockSpec(shape, lambda i: (0,) * len(shape))
    row = lambda w: pl.BlockSpec((tm, w), lambda i: (i, 0))
    args = [xf, ya, yb, g_mix.reshape(1, d), w_gate.astype(BF16), b_gate.reshape(1, 2 * d),
            w_proj_a.astype(BF16), w_proj_b.astype(BF16), w_out.astype(BF16), g_ffn.reshape(1, d),
            w_router.astype(F32).T, router_bias.astype(F32).reshape(N_EXPERTS, 1)]
    in_specs = [row(d), row(OUT_A), row(OUT_B)] + [const(a.shape) for a in args[3:]]
    return pl.pallas_call(
        _post_kernel,
        grid=(n // tm,),
        in_specs=in_specs,
        out_specs=[row(d), row(d), row(TOP_K), row(TOP_K)],
        out_shape=[jax.ShapeDtypeStruct((n, d), F32), jax.ShapeDtypeStruct((n, d), BF16),
                   jax.ShapeDtypeStruct((n, TOP_K), jnp.int32), jax.ShapeDtypeStruct((n, TOP_K), F32)],
        compiler_params=pltpu.CompilerParams(
            dimension_semantics=("arbitrary",), vmem_limit_bytes=VMEM_LIMIT),
        name="post",
    )(*args)


def _dispatch_kernel(h2_ref, topi_ref, tri_ref, upper_ref, xs_ref, slots_ref, cnt_ref, off_ref):
    tb = h2_ref.shape[0]
    topi = topi_ref[...]
    lane = lax.broadcasted_iota(jnp.int32, (tb, N_EXPERTS), 1)
    picks = [lane == topi[:, k:k + 1] for k in range(TOP_K)]
    mask = picks[0]
    for pk in picks[1:]:
        mask = jnp.logical_or(mask, pk)
    maskf = jnp.where(mask, 1.0, 0.0)
    rank = _dot(tri_ref[...], maskf.astype(BF16))
    cnt = jnp.sum(maskf, axis=0, keepdims=True)
    gran = jnp.floor((cnt + (GRAN - 1)) * (1.0 / GRAN))
    goff = _dot(jnp.broadcast_to(gran, (8, N_EXPERTS)).astype(BF16), upper_ref[...])[0:1]
    off = goff * GRAN
    slot_te = off + rank
    slots = jnp.concatenate(
        [jnp.sum(jnp.where(pk, slot_te, 0.0), axis=-1, keepdims=True) for pk in picks], axis=1)
    slots_ref[...] = slots.astype(jnp.int32)
    cnt_ref[0] = cnt.astype(jnp.int32)
    off_ref[0] = off.astype(jnp.int32)
    v = jnp.where(mask, slot_te, float(NO_SLOT))
    v_hi = jnp.floor(v * (1.0 / SLOT_RADIX))
    w = jnp.concatenate([v_hi * SLOT_RADIX, v - v_hi * SLOT_RADIX], axis=1).T.astype(BF16)
    end = off + gran * GRAN
    used = jnp.max(end).astype(jnp.int32)
    h2 = h2_ref[...]

    def lookup(first, rows):
        s_e = (lax.broadcasted_iota(jnp.int32, (rows, N_EXPERTS), 0) + first).astype(F32)
        own = jnp.where(jnp.logical_and(s_e >= off, s_e < end), 1.0, 0.0)
        return _dot(jnp.concatenate([own, own], axis=1).astype(BF16), w)

    def sort_chunk(c, looked):
        s_t = (lax.broadcasted_iota(jnp.int32, (SLOT_CHUNK, tb), 0) + c * SLOT_CHUNK).astype(F32)
        onehot = jnp.where(looked == s_t, 1.0, 0.0).astype(BF16)
        xs_ref[0, c * SLOT_CHUNK:(c + 1) * SLOT_CHUNK, :] = _dot(onehot, h2).astype(BF16)

    looked_typ = lookup(0, CHUNKS_TYPICAL * SLOT_CHUNK)
    for c in range(CAP // SLOT_CHUNK):
        if c < CHUNKS_TYPICAL:
            sort_chunk(c, looked_typ[c * SLOT_CHUNK:(c + 1) * SLOT_CHUNK, :])
        else:
            @pl.when(c * SLOT_CHUNK < used)
            def _(c=c):
                sort_chunk(c, lookup(c * SLOT_CHUNK, SLOT_CHUNK))

            @pl.when(c * SLOT_CHUNK >= used)
            def _(c=c):
                xs_ref[0, c * SLOT_CHUNK:(c + 1) * SLOT_CHUNK, :] = jnp.zeros((SLOT_CHUNK, xs_ref.shape[2]), BF16)


def _dispatch(h2, topi):
    n, d = h2.shape
    nb = n // TB
    tri = jnp.asarray(np.tril(np.ones((TB, TB), np.float32), -1), BF16)
    upper = jnp.asarray(np.triu(np.ones((N_EXPERTS, N_EXPERTS), np.float32), 1), BF16)
    const = lambda shape: pl.BlockSpec(shape, lambda i: (0,) * len(shape))
    meta = pl.BlockSpec((1, 1, N_EXPERTS), lambda i: (i, 0, 0))
    return pl.pallas_call(
        _dispatch_kernel,
        grid=(nb,),
        in_specs=[pl.BlockSpec((TB, d), lambda i: (i, 0)), pl.BlockSpec((TB, TOP_K), lambda i: (i, 0)),
                  const(tri.shape), const(upper.shape)],
        out_specs=[pl.BlockSpec((1, CAP, d), lambda i: (i, 0, 0)),
                   pl.BlockSpec((TB, TOP_K), lambda i: (i, 0)), meta, meta],
        out_shape=[jax.ShapeDtypeStruct((nb, CAP, d), BF16), jax.ShapeDtypeStruct((n, TOP_K), jnp.int32),
                   jax.ShapeDtypeStruct((nb, 1, N_EXPERTS), jnp.int32),
                   jax.ShapeDtypeStruct((nb, 1, N_EXPERTS), jnp.int32)],
        compiler_params=pltpu.CompilerParams(
            dimension_semantics=("arbitrary",), vmem_limit_bytes=VMEM_LIMIT),
        name="dispatch",
    )(h2, topi, tri, upper)


def _ffn_kernel(item_e_sm, item_d0_sm, item_nd_sm, item_s0_sm, item_ns_sm, dlist_sm, slist_sm,
                xs_hbm, wg_ref, wu_ref, wd_ref, ys_hbm, xbuf, ybuf, wg_bf, wu_bf, wd_bf, sem_in, sem_out):
    step = pl.program_id(0)
    nsteps = pl.num_programs(0)
    buf = step % 2

    def granules(st):
        return 2 * item_nd_sm[st] + item_ns_sm[st]

    def for_copies(st, fn):
        d0, nd = item_d0_sm[st], item_nd_sm[st]
        s0, ns = item_s0_sm[st], item_ns_sm[st]

        def pair(j, carry):
            fn(dlist_sm[d0 + j], 2 * j, 2)
            return carry

        def single(j, carry):
            fn(slist_sm[s0 + j], 2 * nd + j, 1)
            return carry

        lax.fori_loop(0, nd, pair, 0)
        lax.fori_loop(0, ns, single, 0)

    def fetch(b_):
        return lambda src, dst, k: pltpu.make_async_copy(
            xs_hbm.at[pl.ds(src, k)], xbuf.at[b_, pl.ds(dst, k)], sem_in.at[b_])

    def writeback(b_):
        return lambda src, dst, k: pltpu.make_async_copy(
            ybuf.at[b_, pl.ds(dst, k)], ys_hbm.at[pl.ds(src, k)], sem_out.at[b_])

    def start(mk):
        return lambda src, dst, k: mk(src, dst, k).start()

    def wait_all(st, span):
        n = granules(st)
        size = PASS_GRAN
        while size >= 1:
            @pl.when((n & size) != 0)
            def _(size=size):
                span(size).wait()
            size //= 2
        return n

    def fetch_span(b_):
        return lambda k: pltpu.make_async_copy(
            xs_hbm.at[pl.ds(0, k)], xbuf.at[b_, pl.ds(0, k)], sem_in.at[b_])

    def writeback_span(b_):
        return lambda k: pltpu.make_async_copy(
            ybuf.at[b_, pl.ds(0, k)], ys_hbm.at[pl.ds(0, k)], sem_out.at[b_])

    @pl.when(step == 0)
    def _():
        xbuf[...] = jnp.zeros_like(xbuf)
        for_copies(step, start(fetch(0)))

    @pl.when(step + 1 < nsteps)
    def _():
        for_copies(step + 1, start(fetch(1 - buf)))

    ngran = wait_all(step, fetch_span(buf))

    @pl.when(step >= 2)
    def _():
        wait_all(step - 2, writeback_span(buf))

    @pl.when(ngran > 0)
    def _():
        wg_bf[...] = wg_ref[0].astype(BF16)
        wu_bf[...] = wu_ref[0].astype(BF16)
        wd_bf[...] = wd_ref[0].astype(BF16)

    x_cols = xbuf.shape[-1]

    def ffn_rows(base, rows):
        grans = pl.ds(pl.multiple_of(base // GRAN, rows // GRAN), rows // GRAN)
        x = xbuf[buf, grans].reshape(rows, x_cols)
        g = _dot(x, wg_bf[...])
        u = _dot(x, wu_bf[...])
        mid = (g * jax.nn.sigmoid(g) * u).astype(BF16)
        ybuf[buf, grans] = _dot(mid, wd_bf[...]).astype(BF16).reshape(rows // GRAN, GRAN, x_cols)

    nt = (ngran * GRAN + (FT - 1)) // FT
    big = FT_BIG // FT

    def big_tile(i, carry):
        ffn_rows(pl.multiple_of(i * FT_BIG, FT_BIG), FT_BIG)
        return carry

    lax.fori_loop(0, nt // big, big_tile, 0)
    size = big // 2
    while size >= 1:
        @pl.when((nt & size) != 0)
        def _(size=size):
            ffn_rows(pl.multiple_of((nt & ~(2 * size - 1)) * FT, size * FT), size * FT)
        size //= 2

    for_copies(step, start(writeback(buf)))

    @pl.when(step == nsteps - 1)
    def _():
        wait_all(step, writeback_span(buf))

        @pl.when(step >= 1)
        def _():
            wait_all(step - 1, writeback_span(1 - buf))


def _work_items(cnt, off):
    nb = cnt.shape[0]
    nseg = N_EXPERTS * nb
    i32 = jnp.int32
    seg_n = ((cnt.reshape(nb, N_EXPERTS) + (GRAN - 1)) // GRAN).T
    seg_row = ((off.reshape(nb, N_EXPERTS) + jnp.arange(nb, dtype=i32)[:, None] * CAP) // GRAN).T.reshape(-1)
    before = jnp.cumsum(seg_n, axis=1) - seg_n
    pass_id = (before // (PASS_GRAN - TB // GRAN)).reshape(-1)
    seg_n = seg_n.reshape(-1)
    nd, ns = seg_n // 2, seg_n % 2
    gmax = nb * (TB * TOP_K // GRAN + N_EXPERTS)

    d_end = jnp.cumsum(nd)
    d_start = d_end - nd
    prev_end = jnp.concatenate([jnp.full((1,), 2, i32), (seg_row + 2 * nd)[:-1]])
    dlist = jnp.cumsum(jnp.full((gmax // 2,), 2, i32).at[d_start].add(seg_row - prev_end, mode="drop"))
    s_end = jnp.cumsum(ns)
    s_start = s_end - ns
    slist = jnp.zeros((nseg,), i32).at[jnp.where(ns == 1, s_start, nseg)].add(seg_row + 2 * nd, mode="drop")

    seg = jnp.arange(nseg, dtype=i32)
    new_item = jnp.logical_or(seg % nb == 0, pass_id != jnp.concatenate([pass_id[:1], pass_id[:-1]]))
    item_of_seg = jnp.cumsum(new_item.astype(i32)) - 1
    n_items = N_EXPERTS + gmax // (PASS_GRAN - TB // GRAN)
    first_seg = jnp.full((n_items + 1,), nseg, i32).at[
        jnp.where(new_item, item_of_seg, n_items + 1)].add(seg - nseg, mode="drop")
    lo, hi = first_seg[:-1], first_seg[1:]
    d_bound = jnp.concatenate([d_start, d_end[-1:]])
    s_bound = jnp.concatenate([s_start, s_end[-1:]])
    item_e = jnp.minimum(lo // nb, N_EXPERTS - 1)
    return (item_e.astype(i32), d_bound[lo], d_bound[hi] - d_bound[lo], s_bound[lo], s_bound[hi] - s_bound[lo],
            dlist.astype(i32), slist)


def _ffn(xs, cnt, off, wg, wu, wd):
    _, _, d = xs.shape
    items = _work_items(cnt, off)
    per_expert = lambda shape: pl.BlockSpec((1,) + shape, lambda w, ie, *_: (ie[w], 0, 0))
    grid_spec = pltpu.PrefetchScalarGridSpec(
        num_scalar_prefetch=len(items),
        grid=(items[0].shape[0],),
        in_specs=[pl.BlockSpec(memory_space=pl.ANY), per_expert((d, D_EXPERT)), per_expert((d, D_EXPERT)),
                  per_expert((D_EXPERT, d))],
        out_specs=pl.BlockSpec(memory_space=pl.ANY),
        scratch_shapes=[pltpu.VMEM((2, PASS_GRAN, GRAN, d), BF16)] * 2 + [
                        pltpu.VMEM((d, D_EXPERT), BF16), pltpu.VMEM((d, D_EXPERT), BF16),
                        pltpu.VMEM((D_EXPERT, d), BF16),
                        pltpu.SemaphoreType.DMA((2,)), pltpu.SemaphoreType.DMA((2,))],
    )
    return pl.pallas_call(
        _ffn_kernel,
        grid_spec=grid_spec,
        out_shape=jax.ShapeDtypeStruct(xs.shape, xs.dtype),
        input_output_aliases={len(items): 0},
        compiler_params=pltpu.CompilerParams(
            dimension_semantics=("arbitrary",), vmem_limit_bytes=VMEM_LIMIT),
        name="ffn",
    )(*items, xs, wg, wu, wd)


def _combine_kernel(used_sm, tail_sm, x1_ref, h2_ref, *rest):
    n_chunks = CAP // SLOT_CHUNK
    ys_refs = rest[:n_chunks]
    slots_ref, topw_ref, wgus_ref, wds_ref, o_ref = rest[n_chunks:]
    tb = x1_ref.shape[0]
    gu = _dot(h2_ref[...], wgus_ref[...])
    g, u = gu[:, :D_SHARED], gu[:, D_SHARED:]
    acc = x1_ref[...] + _dot((g * jax.nn.sigmoid(g) * u).astype(BF16), wds_ref[...])
    slots = slots_ref[...].astype(F32)
    topw = topw_ref[...]
    used = used_sm[pl.program_id(0)]

    def gather_chunk(c):
        scol = (lax.broadcasted_iota(jnp.int32, (tb, SLOT_CHUNK), 1) + c * SLOT_CHUNK).astype(F32)
        gate = jnp.zeros((tb, SLOT_CHUNK), F32)
        for k in range(TOP_K):
            gate = jnp.where(scol == slots[:, k:k + 1], topw[:, k:k + 1], gate)
        return _dot(gate.astype(BF16), ys_refs[c][0])

    for c in range(CHUNKS_TYPICAL):
        acc = acc + gather_chunk(c)
    o_ref[...] = acc
    for c in range(CHUNKS_TYPICAL, CAP // SLOT_CHUNK):
        @pl.when(c * SLOT_CHUNK < used)
        def _(c=c):
            o_ref[...] += gather_chunk(c)


def _combine(x1, h2, ys, slots, topw, used, wgus, wds):
    n, d = x1.shape
    n_chunks = CAP // SLOT_CHUNK
    const = lambda shape: pl.BlockSpec(shape, lambda i, u, t: (0,) * len(shape))
    row = lambda w: pl.BlockSpec((TB, w), lambda i, u, t: (i, 0))
    blocks = jnp.arange(n // TB, dtype=jnp.int32)
    tails = [lax.cummax(jnp.where(used > c * SLOT_CHUNK, blocks, 0)) for c in range(CHUNKS_TYPICAL, n_chunks)]
    tail_idx = jnp.stack(tails).astype(jnp.int32)
    n_blocks = n // TB

    def chunk_spec(c):
        if c < CHUNKS_TYPICAL:
            return pl.BlockSpec((1, SLOT_CHUNK, d), lambda i, u, t: (i * n_chunks + c, 0, 0))
        late = c - CHUNKS_TYPICAL
        return pl.BlockSpec((1, SLOT_CHUNK, d), lambda i, u, t: (t[late * n_blocks + i] * n_chunks + c, 0, 0))

    grid_spec = pltpu.PrefetchScalarGridSpec(
        num_scalar_prefetch=2,
        grid=(n_blocks,),
        in_specs=[row(d), row(d)] + [chunk_spec(c) for c in range(n_chunks)] + [
            row(TOP_K), row(TOP_K), const(wgus.shape), const(wds.shape)],
        out_specs=row(d),
    )
    ys_chunks = ys.reshape(n_blocks * n_chunks, SLOT_CHUNK, d)
    return pl.pallas_call(
        _combine_kernel,
        grid_spec=grid_spec,
        out_shape=jax.ShapeDtypeStruct((n, d), F32),
        compiler_params=pltpu.CompilerParams(
            dimension_semantics=("arbitrary",), vmem_limit_bytes=VMEM_LIMIT),
        name="combine",
    )(used, tail_idx.reshape(-1), x1, h2, *([ys_chunks] * n_chunks), slots, topw, wgus, wds)


def _moe(x1, h2, topi, topw, w_gate_e, w_up_e, w_down_e, w_gate_s, w_up_s, w_down_s):
    n, d = x1.shape
    wgus = jnp.concatenate([w_gate_s.astype(BF16), w_up_s.astype(BF16)], axis=-1)
    xs, slots, cnt, off = _dispatch(h2, topi)
    ys = _ffn(xs.reshape(-1, GRAN, d), cnt, off, w_gate_e, w_up_e, w_down_e)
    used = jnp.max(off + (cnt + (GRAN - 1)) // GRAN * GRAN, axis=(1, 2)).astype(jnp.int32)
    return _combine(x1, h2, ys.reshape(n // TB, CAP, d), slots, topw, used, wgus, w_down_s.astype(BF16))


def kernel(x, g_mix, w_in, q_norm_a, k_norm_a, q_norm_b, k_norm_b, rel_bias, b_forget, w_gate, b_gate,
           w_proj_a, w_proj_b, w_out, g_ffn, w_router, router_bias, w_gate_e, w_up_e, w_down_e,
           w_gate_s, w_up_s, w_down_s):
    batch, seq, d = x.shape
    xf = x.reshape(batch * seq, d)
    fox_shift, fox_top, fox_online = _fox_shift(q_norm_b, k_norm_b)
    pa0, pa1, pa2, qkb, vb, cb = _inproj(xf, g_mix, w_in, q_norm_a, k_norm_a, q_norm_b, k_norm_b, b_forget,
                                     fox_shift, seq)
    bias = jnp.stack([_toeplitz_bias(rel_bias, g, dil) for g, (_, dil) in enumerate(DIL_GROUPS)])
    ya = _dilated(pa0, pa1, pa2, bias, batch, seq)

    yb = _fox(qkb, vb, cb, fox_top, fox_online, batch, seq)
    x1, h2, topi, topw = _post(xf, ya, yb, g_mix, w_gate, b_gate, w_proj_a, w_proj_b, w_out, g_ffn,
                               w_router, router_bias)
    out = _moe(x1, h2, topi, topw, w_gate_e, w_up_e, w_down_e, w_gate_s, w_up_s, w_down_s)
    return out.reshape(batch, seq, d)
```
